```python
import jax, jax.numpy as jnp
from jax import lax
import numpy as np

D_MODEL = 1024
BATCH = 8
SEQ = 8192
DEPTH = 4

N_EVEN = (DEPTH + 1) // 2
N_ODD = DEPTH // 2
D_A = D_MODEL // 2
A_HEADS = 8
CONV_WIDTH_A = 3
POOL_WINDOWS = (2, 4, 8, 16)
D_B = D_MODEL // 2
B_GROUP = D_B // len(POOL_WINDOWS)
D_IN_AB = 3 * D_A + D_B
CONF_KERNEL = 31
D_FF = 256 * (-(-(8 * D_MODEL) // (3 * 256)))
N_MOD = 6
RMS_EPS = 1e-6
LN_EPS = 1e-5

kernel_name = "hybrid_conv_pool_conformer_encoder"


def rms_norm(x, g):
    xf = x.astype(jnp.float32)
    y = xf * lax.rsqrt(jnp.mean(xf * xf, axis=-1, keepdims=True) + RMS_EPS)
    return (y * g.astype(jnp.float32)).astype(x.dtype)


def layer_norm(x, g, b):
    xf = x.astype(jnp.float32)
    mu = jnp.mean(xf, axis=-1, keepdims=True)
    var = jnp.mean(jnp.square(xf - mu), axis=-1, keepdims=True)
    y = (xf - mu) * lax.rsqrt(var + LN_EPS)
    return (y * g.astype(jnp.float32) + b.astype(jnp.float32)).astype(x.dtype)


def modulate(h, shift, scale):
    return h * (1 + scale[:, None, :]) + shift[:, None, :]


def depthwise_conv(x, w):
    k = w.shape[0]
    left = (k - 1) // 2
    return lax.conv_general_dilated(
        x, w[:, None, :].astype(x.dtype), window_strides=(1,),
        padding=[(left, k - 1 - left)],
        dimension_numbers=('NWC', 'WIO', 'NWC'),
        feature_group_count=x.shape[-1])


def centred_window_mean(x, w):
    L = x.shape[1]
    left = w // 2
    right = w - 1 - left
    xp = jnp.pad(x.astype(jnp.float32), ((0, 0), (left + 1, right), (0, 0)))
    cs = jnp.cumsum(xp, axis=1)
    s = cs[:, w:w + L] - cs[:, :L]
    t = jnp.arange(L)
    cnt = (jnp.minimum(t + right, L - 1) - jnp.maximum(t - left, 0) + 1).astype(jnp.float32)
    return (s / cnt[None, :, None]).astype(x.dtype)


def conv_pool_mixer(h, w_in, conv_a, w_pool, pool_scale, w_out):
    u = jnp.einsum('bsd,de->bse', h, w_in)
    b_gate, c_gate, v, p = jnp.split(u, [D_A, 2 * D_A, 3 * D_A], axis=-1)
    y_a = b_gate * depthwise_conv(c_gate * v, conv_a)
    groups = jnp.split(p, len(POOL_WINDOWS), axis=-1)
    pooled = jnp.stack([centred_window_mean(g, w) - g for g, w in zip(groups, POOL_WINDOWS)],
                       axis=2)
    y_b = jnp.einsum('bsgc,gce->bsge', pooled, w_pool)
    y_b = y_b.reshape(h.shape[0], h.shape[1], D_B) * pool_scale
    return jnp.einsum('bse,ed->bsd', jnp.concatenate([y_a, y_b], axis=-1), w_out)


def conformer_conv(h, w_pw1, b_pw1, w_dw, b_dw, ln_g, ln_b, w_pw2, b_pw2):
    u = jnp.einsum('bsd,de->bse', h, w_pw1) + b_pw1
    a, g = jnp.split(u, 2, axis=-1)
    z = a * jax.nn.sigmoid(g)
    z = depthwise_conv(z, w_dw) + b_dw
    z = jax.nn.silu(layer_norm(z, ln_g, ln_b))
    return jnp.einsum('bsd,de->bse', z, w_pw2) + b_pw2


def swiglu(h, w_gate, w_up, w_down):
    a = jnp.einsum('bsd,df->bsf', h, w_gate)
    b = jnp.einsum('bsd,df->bsf', h, w_up)
    return jnp.einsum('bsf,fd->bsd', jax.nn.silu(a) * b, w_down)


def _fwd_setup_inputs(seed: int = 0) -> dict:
    key = jax.random.key(seed)
    ks = iter(jax.random.split(key, 32))
    D = D_MODEL
    f32 = jnp.float32

    def nrm(shape, scale):
        return jax.random.normal(next(ks), shape, f32) * scale

    return {
        "x": nrm((BATCH, SEQ, D), 1.0),
        "c": nrm((BATCH, D), 1.0),
        "norm_mix_g": 1.0 + nrm((DEPTH, D), 0.05),
        "norm_ffn_g": 1.0 + nrm((DEPTH, D), 0.05),
        "w_mod": nrm((DEPTH, D, N_MOD * D), 0.5 * D ** -0.5),
        "b_mod": nrm((DEPTH, N_MOD * D), 0.02),
        "ab_w_in": nrm((N_EVEN, D, D_IN_AB), D ** -0.5),
        "ab_conv": nrm((N_EVEN, CONV_WIDTH_A, D_A), CONV_WIDTH_A ** -0.5),
        "ab_w_pool": nrm((N_EVEN, len(POOL_WINDOWS), B_GROUP, B_GROUP), B_GROUP ** -0.5),
        "ab_pool_scale": 1.0 + nrm((N_EVEN, D_B), 0.1),
        "ab_w_out": nrm((N_EVEN, D_A + D_B, D), (D_A + D_B) ** -0.5),
        "cf_w_pw1": nrm((N_ODD, D, 2 * D), D ** -0.5),
        "cf_b_pw1": nrm((N_ODD, 2 * D), 0.02),
        "cf_w_dw": nrm((N_ODD, CONF_KERNEL, D), CONF_KERNEL ** -0.5),
        "cf_b_dw": nrm((N_ODD, D), 0.02),
        "cf_ln_g": 1.0 + nrm((N_ODD, D), 0.05),
        "cf_ln_b": nrm((N_ODD, D), 0.02),
        "cf_w_pw2": nrm((N_ODD, D, D), D ** -0.5),
        "cf_b_pw2": nrm((N_ODD, D), 0.02),
        "ffn_w_gate": nrm((DEPTH, D, D_FF), D ** -0.5),
        "ffn_w_up": nrm((DEPTH, D, D_FF), D ** -0.5),
        "ffn_w_down": nrm((DEPTH, D_FF, D), D_FF ** -0.5),
        "final_norm_g": 1.0 + nrm((D,), 0.05),
    }


def _fwd_reference(x, c, norm_mix_g, norm_ffn_g, w_mod, b_mod,
              ab_w_in, ab_conv, ab_w_pool, ab_pool_scale, ab_w_out,
              cf_w_pw1, cf_b_pw1, cf_w_dw, cf_b_dw, cf_ln_g, cf_ln_b, cf_w_pw2, cf_b_pw2,
              ffn_w_gate, ffn_w_up, ffn_w_down, final_norm_g):
    c_act = jax.nn.silu(c)
    for layer in range(DEPTH):
        mod = jnp.einsum('bd,de->be', c_act, w_mod[layer]) + b_mod[layer]
        sh1, sc1, g1, sh2, sc2, g2 = jnp.split(mod, N_MOD, axis=-1)
        h = modulate(rms_norm(x, norm_mix_g[layer]), sh1, sc1)
        i = layer // 2
        if layer % 2 == 0:
            y = conv_pool_mixer(h, ab_w_in[i], ab_conv[i], ab_w_pool[i],
                                ab_pool_scale[i], ab_w_out[i])
        else:
            y = conformer_conv(h, cf_w_pw1[i], cf_b_pw1[i], cf_w_dw[i], cf_b_dw[i],
                               cf_ln_g[i], cf_ln_b[i], cf_w_pw2[i], cf_b_pw2[i])
        x = x + g1[:, None, :] * y
        h = modulate(rms_norm(x, norm_ffn_g[layer]), sh2, sc2)
        x = x + g2[:, None, :] * swiglu(h, ffn_w_gate[layer], ffn_w_up[layer], ffn_w_down[layer])
    return rms_norm(x, final_norm_g)


import jax as _jax
import jax.numpy as _jnp

TWIN_FORMAT = 'train_step'
FWD_PARAMS = ['x', 'c', 'norm_mix_g', 'norm_ffn_g', 'w_mod', 'b_mod', 'ab_w_in', 'ab_conv', 'ab_w_pool', 'ab_pool_scale', 'ab_w_out', 'cf_w_pw1', 'cf_b_pw1', 'cf_w_dw', 'cf_b_dw', 'cf_ln_g', 'cf_ln_b', 'cf_w_pw2', 'cf_b_pw2', 'ffn_w_gate', 'ffn_w_up', 'ffn_w_down', 'final_norm_g']
TWIN_WEIGHTS = ['norm_mix_g', 'norm_ffn_g', 'w_mod', 'b_mod', 'ab_w_in', 'ab_conv', 'ab_w_pool', 'ab_pool_scale', 'ab_w_out', 'cf_w_pw1', 'cf_b_pw1', 'cf_w_dw', 'cf_b_dw', 'cf_ln_g', 'cf_ln_b', 'cf_w_pw2', 'cf_b_pw2', 'ffn_w_gate', 'ffn_w_up', 'ffn_w_down', 'final_norm_g']
TWIN_DIFF_INPUT = 'x'
TWIN_INPUTS = ['x', 'c', 'norm_mix_g', 'norm_ffn_g', 'w_mod', 'b_mod', 'ab_w_in', 'ab_conv', 'ab_w_pool', 'ab_pool_scale', 'ab_w_out', 'cf_w_pw1', 'cf_b_pw1', 'cf_w_dw', 'cf_b_dw', 'cf_ln_g', 'cf_ln_b', 'cf_w_pw2', 'cf_b_pw2', 'ffn_w_gate', 'ffn_w_up', 'ffn_w_down', 'final_norm_g', 'loss_target', 'm_norm_mix_g', 'm_norm_ffn_g', 'm_w_mod', 'm_b_mod', 'm_ab_w_in', 'm_ab_conv', 'm_ab_w_pool', 'm_ab_pool_scale', 'm_ab_w_out', 'm_cf_w_pw1', 'm_cf_b_pw1', 'm_cf_w_dw', 'm_cf_b_dw', 'm_cf_ln_g', 'm_cf_ln_b', 'm_cf_w_pw2', 'm_cf_b_pw2', 'm_ffn_w_gate', 'm_ffn_w_up', 'm_ffn_w_down', 'm_final_norm_g', 'v_norm_mix_g', 'v_norm_ffn_g', 'v_w_mod', 'v_b_mod', 'v_ab_w_in', 'v_ab_conv', 'v_ab_w_pool', 'v_ab_pool_scale', 'v_ab_w_out', 'v_cf_w_pw1', 'v_cf_b_pw1', 'v_cf_w_dw', 'v_cf_b_dw', 'v_cf_ln_g', 'v_cf_ln_b', 'v_cf_w_pw2', 'v_cf_b_pw2', 'v_ffn_w_gate', 'v_ffn_w_up', 'v_ffn_w_down', 'v_final_norm_g']
TWIN_OUTPUTS = ['loss', 'grad_x', 'grad_norm_mix_g', 'grad_norm_ffn_g', 'grad_w_mod', 'grad_b_mod', 'grad_ab_w_in', 'grad_ab_conv', 'grad_ab_w_pool', 'grad_ab_pool_scale', 'grad_ab_w_out', 'grad_cf_w_pw1', 'grad_cf_b_pw1', 'grad_cf_w_dw', 'grad_cf_b_dw', 'grad_cf_ln_g', 'grad_cf_ln_b', 'grad_cf_w_pw2', 'grad_cf_b_pw2', 'grad_ffn_w_gate', 'grad_ffn_w_up', 'grad_ffn_w_down', 'grad_final_norm_g', 'delta_norm_mix_g', 'delta_norm_ffn_g', 'delta_w_mod', 'delta_b_mod', 'delta_ab_w_in', 'delta_ab_conv', 'delta_ab_w_pool', 'delta_ab_pool_scale', 'delta_ab_w_out', 'delta_cf_w_pw1', 'delta_cf_b_pw1', 'delta_cf_w_dw', 'delta_cf_b_dw', 'delta_cf_ln_g', 'delta_cf_ln_b', 'delta_cf_w_pw2', 'delta_cf_b_pw2', 'delta_ffn_w_gate', 'delta_ffn_w_up', 'delta_ffn_w_down', 'delta_final_norm_g', 'new_m_norm_mix_g', 'new_m_norm_ffn_g', 'new_m_w_mod', 'new_m_b_mod', 'new_m_ab_w_in', 'new_m_ab_conv', 'new_m_ab_w_pool', 'new_m_ab_pool_scale', 'new_m_ab_w_out', 'new_m_cf_w_pw1', 'new_m_cf_b_pw1', 'new_m_cf_w_dw', 'new_m_cf_b_dw', 'new_m_cf_ln_g', 'new_m_cf_ln_b', 'new_m_cf_w_pw2', 'new_m_cf_b_pw2', 'new_m_ffn_w_gate', 'new_m_ffn_w_up', 'new_m_ffn_w_down', 'new_m_final_norm_g', 'new_v_norm_mix_g', 'new_v_norm_ffn_g', 'new_v_w_mod', 'new_v_b_mod', 'new_v_ab_w_in', 'new_v_ab_conv', 'new_v_ab_w_pool', 'new_v_ab_pool_scale', 'new_v_ab_w_out', 'new_v_cf_w_pw1', 'new_v_cf_b_pw1', 'new_v_cf_w_dw', 'new_v_cf_b_dw', 'new_v_cf_ln_g', 'new_v_cf_ln_b', 'new_v_cf_w_pw2', 'new_v_cf_b_pw2', 'new_v_ffn_w_gate', 'new_v_ffn_w_up', 'new_v_ffn_w_down', 'new_v_final_norm_g']
TWIN_LEAF_KINDS = {'loss': 'loss', 'grad_x': 'grad_x', 'grad_norm_mix_g': 'grad_w', 'grad_norm_ffn_g': 'grad_w', 'grad_w_mod': 'grad_w', 'grad_b_mod': 'grad_w', 'grad_ab_w_in': 'grad_w', 'grad_ab_conv': 'grad_w', 'grad_ab_w_pool': 'grad_w', 'grad_ab_pool_scale': 'grad_w', 'grad_ab_w_out': 'grad_w', 'grad_cf_w_pw1': 'grad_w', 'grad_cf_b_pw1': 'grad_w', 'grad_cf_w_dw': 'grad_w', 'grad_cf_b_dw': 'grad_w', 'grad_cf_ln_g': 'grad_w', 'grad_cf_ln_b': 'grad_w', 'grad_cf_w_pw2': 'grad_w', 'grad_cf_b_pw2': 'grad_w', 'grad_ffn_w_gate': 'grad_w', 'grad_ffn_w_up': 'grad_w', 'grad_ffn_w_down': 'grad_w', 'grad_final_norm_g': 'grad_w', 'delta_norm_mix_g': 'delta_w', 'delta_norm_ffn_g': 'delta_w', 'delta_w_mod': 'delta_w', 'delta_b_mod': 'delta_w', 'delta_ab_w_in': 'delta_w', 'delta_ab_conv': 'delta_w', 'delta_ab_w_pool': 'delta_w', 'delta_ab_pool_scale': 'delta_w', 'delta_ab_w_out': 'delta_w', 'delta_cf_w_pw1': 'delta_w', 'delta_cf_b_pw1': 'delta_w', 'delta_cf_w_dw': 'delta_w', 'delta_cf_b_dw': 'delta_w', 'delta_cf_ln_g': 'delta_w', 'delta_cf_ln_b': 'delta_w', 'delta_cf_w_pw2': 'delta_w', 'delta_cf_b_pw2': 'delta_w', 'delta_ffn_w_gate': 'delta_w', 'delta_ffn_w_up': 'delta_w', 'delta_ffn_w_down': 'delta_w', 'delta_final_norm_g': 'delta_w', 'new_m_norm_mix_g': 'new_m', 'new_m_norm_ffn_g': 'new_m', 'new_m_w_mod': 'new_m', 'new_m_b_mod': 'new_m', 'new_m_ab_w_in': 'new_m', 'new_m_ab_conv': 'new_m', 'new_m_ab_w_pool': 'new_m', 'new_m_ab_pool_scale': 'new_m', 'new_m_ab_w_out': 'new_m', 'new_m_cf_w_pw1': 'new_m', 'new_m_cf_b_pw1': 'new_m', 'new_m_cf_w_dw': 'new_m', 'new_m_cf_b_dw': 'new_m', 'new_m_cf_ln_g': 'new_m', 'new_m_cf_ln_b': 'new_m', 'new_m_cf_w_pw2': 'new_m', 'new_m_cf_b_pw2': 'new_m', 'new_m_ffn_w_gate': 'new_m', 'new_m_ffn_w_up': 'new_m', 'new_m_ffn_w_down': 'new_m', 'new_m_final_norm_g': 'new_m', 'new_v_norm_mix_g': 'new_v', 'new_v_norm_ffn_g': 'new_v', 'new_v_w_mod': 'new_v', 'new_v_b_mod': 'new_v', 'new_v_ab_w_in': 'new_v', 'new_v_ab_conv': 'new_v', 'new_v_ab_w_pool': 'new_v', 'new_v_ab_pool_scale': 'new_v', 'new_v_ab_w_out': 'new_v', 'new_v_cf_w_pw1': 'new_v', 'new_v_cf_b_pw1': 'new_v', 'new_v_cf_w_dw': 'new_v', 'new_v_cf_b_dw': 'new_v', 'new_v_cf_ln_g': 'new_v', 'new_v_cf_ln_b': 'new_v', 'new_v_cf_w_pw2': 'new_v', 'new_v_cf_b_pw2': 'new_v', 'new_v_ffn_w_gate': 'new_v', 'new_v_ffn_w_up': 'new_v', 'new_v_ffn_w_down': 'new_v', 'new_v_final_norm_g': 'new_v'}


def _forward(args):
    return _fwd_reference(*[args[k] for k in FWD_PARAMS])


def _output_shape():
    def fwd():
        inp = _fwd_setup_inputs(0)
        return _fwd_reference(*[inp[k] for k in FWD_PARAMS])
    out = _jax.eval_shape(fwd)
    return out.shape, out.dtype

N_MICROBATCH = 1
ADAM_LR = 0.001
ADAM_B1 = 0.9
ADAM_B2 = 0.999
ADAM_EPS = 1e-08
ADAM_WD = 0.01
ADAM_STEP = 10
PER_EXAMPLE_BATCH_AXIS = {'x': 0, 'c': 0, 'loss_target': 0}
SHARED_INPUTS = []
_WEIGHT_DTYPES = {'norm_mix_g': _jnp.float32, 'norm_ffn_g': _jnp.float32, 'w_mod': _jnp.float32, 'b_mod': _jnp.float32, 'ab_w_in': _jnp.float32, 'ab_conv': _jnp.float32, 'ab_w_pool': _jnp.float32, 'ab_pool_scale': _jnp.float32, 'ab_w_out': _jnp.float32, 'cf_w_pw1': _jnp.float32, 'cf_b_pw1': _jnp.float32, 'cf_w_dw': _jnp.float32, 'cf_b_dw': _jnp.float32, 'cf_ln_g': _jnp.float32, 'cf_ln_b': _jnp.float32, 'cf_w_pw2': _jnp.float32, 'cf_b_pw2': _jnp.float32, 'ffn_w_gate': _jnp.float32, 'ffn_w_up': _jnp.float32, 'ffn_w_down': _jnp.float32, 'final_norm_g': _jnp.float32}
MOMENT_SCALE = {'norm_mix_g': 9.239856e-02, 'norm_ffn_g': 7.043462e-02, 'w_mod': 1.047082e-01, 'b_mod': 1.957711e-01, 'ab_w_in': 9.045340e-02, 'ab_conv': 9.753053e-02, 'ab_w_pool': 7.023006e-02, 'ab_pool_scale': 7.023617e-02, 'ab_w_out': 8.528318e-02, 'cf_w_pw1': 3.346746e-02, 'cf_b_pw1': 3.507828e-02, 'cf_w_dw': 4.343235e-02, 'cf_b_dw': 8.701800e-02, 'cf_ln_g': 5.291030e-02, 'cf_ln_b': 5.296554e-02, 'cf_w_pw2': 4.415566e-02, 'cf_b_pw2': 9.247946e-02, 'ffn_w_gate': 3.148811e-02, 'ffn_w_up': 3.045876e-02, 'ffn_w_down': 5.063283e-02, 'final_norm_g': 6.427724e+01}


def _to_microbatches(a, axis):
    t = _jnp.moveaxis(a, axis, 0)
    t = t.reshape((N_MICROBATCH, t.shape[0] // N_MICROBATCH) + t.shape[1:])
    return _jnp.moveaxis(t, 1, axis + 1)


def setup_inputs(seed: int = 0) -> dict:
    inp = _fwd_setup_inputs(seed)
    key = _jax.random.fold_in(_jax.random.key(seed), 7919)
    shape, _ = _output_shape()
    out = dict(inp)
    out["loss_target"] = _jax.random.normal(_jax.random.fold_in(key, 0), shape, _jnp.float32)
    for i, name in enumerate(TWIN_WEIGHTS):
        w = inp[name].astype(_jnp.float32)
        if MOMENT_SCALE is None:
            s = _jnp.sqrt(_jnp.mean(_jnp.square(w)) + 1e-30)
        else:
            s = MOMENT_SCALE[name]
        km, kv = _jax.random.split(_jax.random.fold_in(key, i + 1))
        out[name] = w
        out["m_" + name] = s * _jax.random.normal(km, w.shape, _jnp.float32)
        out["v_" + name] = (s * s) * _jax.random.uniform(kv, w.shape, _jnp.float32, 0.5, 1.5)
    if N_MICROBATCH > 1:
        for name, axis in PER_EXAMPLE_BATCH_AXIS.items():
            out[name] = _to_microbatches(out[name], axis)
    return {'x': out['x'], 'c': out['c'], 'norm_mix_g': out['norm_mix_g'], 'norm_ffn_g': out['norm_ffn_g'], 'w_mod': out['w_mod'], 'b_mod': out['b_mod'], 'ab_w_in': out['ab_w_in'], 'ab_conv': out['ab_conv'], 'ab_w_pool': out['ab_w_pool'], 'ab_pool_scale': out['ab_pool_scale'], 'ab_w_out': out['ab_w_out'], 'cf_w_pw1': out['cf_w_pw1'], 'cf_b_pw1': out['cf_b_pw1'], 'cf_w_dw': out['cf_w_dw'], 'cf_b_dw': out['cf_b_dw'], 'cf_ln_g': out['cf_ln_g'], 'cf_ln_b': out['cf_ln_b'], 'cf_w_pw2': out['cf_w_pw2'], 'cf_b_pw2': out['cf_b_pw2'], 'ffn_w_gate': out['ffn_w_gate'], 'ffn_w_up': out['ffn_w_up'], 'ffn_w_down': out['ffn_w_down'], 'final_norm_g': out['final_norm_g'], 'loss_target': out['loss_target'], 'm_norm_mix_g': out['m_norm_mix_g'], 'm_norm_ffn_g': out['m_norm_ffn_g'], 'm_w_mod': out['m_w_mod'], 'm_b_mod': out['m_b_mod'], 'm_ab_w_in': out['m_ab_w_in'], 'm_ab_conv': out['m_ab_conv'], 'm_ab_w_pool': out['m_ab_w_pool'], 'm_ab_pool_scale': out['m_ab_pool_scale'], 'm_ab_w_out': out['m_ab_w_out'], 'm_cf_w_pw1': out['m_cf_w_pw1'], 'm_cf_b_pw1': out['m_cf_b_pw1'], 'm_cf_w_dw': out['m_cf_w_dw'], 'm_cf_b_dw': out['m_cf_b_dw'], 'm_cf_ln_g': out['m_cf_ln_g'], 'm_cf_ln_b': out['m_cf_ln_b'], 'm_cf_w_pw2': out['m_cf_w_pw2'], 'm_cf_b_pw2': out['m_cf_b_pw2'], 'm_ffn_w_gate': out['m_ffn_w_gate'], 'm_ffn_w_up': out['m_ffn_w_up'], 'm_ffn_w_down': out['m_ffn_w_down'], 'm_final_norm_g': out['m_final_norm_g'], 'v_norm_mix_g': out['v_norm_mix_g'], 'v_norm_ffn_g': out['v_norm_ffn_g'], 'v_w_mod': out['v_w_mod'], 'v_b_mod': out['v_b_mod'], 'v_ab_w_in': out['v_ab_w_in'], 'v_ab_conv': out['v_ab_conv'], 'v_ab_w_pool': out['v_ab_w_pool'], 'v_ab_pool_scale': out['v_ab_pool_scale'], 'v_ab_w_out': out['v_ab_w_out'], 'v_cf_w_pw1': out['v_cf_w_pw1'], 'v_cf_b_pw1': out['v_cf_b_pw1'], 'v_cf_w_dw': out['v_cf_w_dw'], 'v_cf_b_dw': out['v_cf_b_dw'], 'v_cf_ln_g': out['v_cf_ln_g'], 'v_cf_ln_b': out['v_cf_ln_b'], 'v_cf_w_pw2': out['v_cf_w_pw2'], 'v_cf_b_pw2': out['v_cf_b_pw2'], 'v_ffn_w_gate': out['v_ffn_w_gate'], 'v_ffn_w_up': out['v_ffn_w_up'], 'v_ffn_w_down': out['v_ffn_w_down'], 'v_final_norm_g': out['v_final_norm_g']}


def _loss(weights, diff, rest, loss_target):
    with _jax.named_scope("forward"):
        args = {**rest, TWIN_DIFF_INPUT: diff, **{k: w.astype(_WEIGHT_DTYPES[k]) for k, w in weights.items()}}
        y = _forward(args)
    with _jax.named_scope("loss_head"):
        err = _jnp.square(y.astype(_jnp.float32) - loss_target)
        return 0.5 * _jnp.sum(_jnp.mean(err, axis=-1)) if err.ndim else 0.5 * err


def _adamw(w, g, m, v):
    m = ADAM_B1 * m + (1.0 - ADAM_B1) * g
    v = ADAM_B2 * v + (1.0 - ADAM_B2) * _jnp.square(g)
    m_hat = m / (1.0 - ADAM_B1 ** ADAM_STEP)
    v_hat = v / (1.0 - ADAM_B2 ** ADAM_STEP)
    delta = -ADAM_LR * (m_hat / (_jnp.sqrt(v_hat) + ADAM_EPS) + ADAM_WD * w)
    return delta, m, v


def reference(x, c, norm_mix_g, norm_ffn_g, w_mod, b_mod, ab_w_in, ab_conv, ab_w_pool, ab_pool_scale, ab_w_out, cf_w_pw1, cf_b_pw1, cf_w_dw, cf_b_dw, cf_ln_g, cf_ln_b, cf_w_pw2, cf_b_pw2, ffn_w_gate, ffn_w_up, ffn_w_down, final_norm_g, loss_target, m_norm_mix_g, m_norm_ffn_g, m_w_mod, m_b_mod, m_ab_w_in, m_ab_conv, m_ab_w_pool, m_ab_pool_scale, m_ab_w_out, m_cf_w_pw1, m_cf_b_pw1, m_cf_w_dw, m_cf_b_dw, m_cf_ln_g, m_cf_ln_b, m_cf_w_pw2, m_cf_b_pw2, m_ffn_w_gate, m_ffn_w_up, m_ffn_w_down, m_final_norm_g, v_norm_mix_g, v_norm_ffn_g, v_w_mod, v_b_mod, v_ab_w_in, v_ab_conv, v_ab_w_pool, v_ab_pool_scale, v_ab_w_out, v_cf_w_pw1, v_cf_b_pw1, v_cf_w_dw, v_cf_b_dw, v_cf_ln_g, v_cf_ln_b, v_cf_w_pw2, v_cf_b_pw2, v_ffn_w_gate, v_ffn_w_up, v_ffn_w_down, v_final_norm_g):
    given = dict(x=x, c=c, norm_mix_g=norm_mix_g, norm_ffn_g=norm_ffn_g, w_mod=w_mod, b_mod=b_mod, ab_w_in=ab_w_in, ab_conv=ab_conv, ab_w_pool=ab_w_pool, ab_pool_scale=ab_pool_scale, ab_w_out=ab_w_out, cf_w_pw1=cf_w_pw1, cf_b_pw1=cf_b_pw1, cf_w_dw=cf_w_dw, cf_b_dw=cf_b_dw, cf_ln_g=cf_ln_g, cf_ln_b=cf_ln_b, cf_w_pw2=cf_w_pw2, cf_b_pw2=cf_b_pw2, ffn_w_gate=ffn_w_gate, ffn_w_up=ffn_w_up, ffn_w_down=ffn_w_down, final_norm_g=final_norm_g, loss_target=loss_target, m_norm_mix_g=m_norm_mix_g, m_norm_ffn_g=m_norm_ffn_g, m_w_mod=m_w_mod, m_b_mod=m_b_mod, m_ab_w_in=m_ab_w_in, m_ab_conv=m_ab_conv, m_ab_w_pool=m_ab_w_pool, m_ab_pool_scale=m_ab_pool_scale, m_ab_w_out=m_ab_w_out, m_cf_w_pw1=m_cf_w_pw1, m_cf_b_pw1=m_cf_b_pw1, m_cf_w_dw=m_cf_w_dw, m_cf_b_dw=m_cf_b_dw, m_cf_ln_g=m_cf_ln_g, m_cf_ln_b=m_cf_ln_b, m_cf_w_pw2=m_cf_w_pw2, m_cf_b_pw2=m_cf_b_pw2, m_ffn_w_gate=m_ffn_w_gate, m_ffn_w_up=m_ffn_w_up, m_ffn_w_down=m_ffn_w_down, m_final_norm_g=m_final_norm_g, v_norm_mix_g=v_norm_mix_g, v_norm_ffn_g=v_norm_ffn_g, v_w_mod=v_w_mod, v_b_mod=v_b_mod, v_ab_w_in=v_ab_w_in, v_ab_conv=v_ab_conv, v_ab_w_pool=v_ab_w_pool, v_ab_pool_scale=v_ab_pool_scale, v_ab_w_out=v_ab_w_out, v_cf_w_pw1=v_cf_w_pw1, v_cf_b_pw1=v_cf_b_pw1, v_cf_w_dw=v_cf_w_dw, v_cf_b_dw=v_cf_b_dw, v_cf_ln_g=v_cf_ln_g, v_cf_ln_b=v_cf_ln_b, v_cf_w_pw2=v_cf_w_pw2, v_cf_b_pw2=v_cf_b_pw2, v_ffn_w_gate=v_ffn_w_gate, v_ffn_w_up=v_ffn_w_up, v_ffn_w_down=v_ffn_w_down, v_final_norm_g=v_final_norm_g)
    weights = {n: given[n] for n in TWIN_WEIGHTS}
    shared = {n: given[n] for n in SHARED_INPUTS}
    per_example = {n: given[n] for n in ['x', 'c']}
    grad_fn = _jax.value_and_grad(_loss, argnums=(0, 1))

    def one_microbatch(ex, loss_target):
        ex = dict(ex)
        diff = ex.pop(TWIN_DIFF_INPUT)
        return grad_fn(weights, diff, {**shared, **ex}, loss_target)

    if N_MICROBATCH == 1:
        loss, (grad_w, grad_x) = one_microbatch(per_example, given["loss_target"])
    else:
        def body(carry, xs):
            loss_sum, grad_sum = carry
            l_k, (gw_k, gx_k) = one_microbatch(xs[0], xs[1])
            with _jax.named_scope("update"):
                return (loss_sum + l_k, _jax.tree.map(_jnp.add, grad_sum, gw_k)), gx_k

        init = (_jnp.zeros((), _jnp.float32), _jax.tree.map(_jnp.zeros_like, weights))
        (loss, grad_w), grad_x = _jax.lax.scan(body, init, (per_example, given["loss_target"]))
    with _jax.named_scope("update"):
        delta_w, new_m, new_v = {}, {}, {}
        for n in TWIN_WEIGHTS:
            delta_w[n], new_m[n], new_v[n] = _adamw(weights[n], grad_w[n], given["m_" + n], given["v_" + n])
    return (loss, grad_x, *[grad_w[n] for n in TWIN_WEIGHTS], *[delta_w[n] for n in TWIN_WEIGHTS],
            *[new_m[n] for n in TWIN_WEIGHTS], *[new_v[n] for n in TWIN_WEIGHTS])
```

```python
import functools

import jax
import jax.numpy as jnp
from jax import lax
from jax.experimental import pallas as pl
from jax.experimental.pallas import tpu as pltpu

F32 = jnp.float32
BF16 = jnp.bfloat16
RMS_EPS = 1e-6
LN_EPS = 1e-5
ADAM_LR = 0.001
ADAM_B1 = 0.9
ADAM_B2 = 0.999
ADAM_EPS = 1e-08
ADAM_WD = 0.01
ADAM_STEP = 10
POOL_WINDOWS = (2, 4, 8, 16)
CONF_KERNEL = 31
N_CHIPS = 4
N_DEV = 8
HALO = 16
LANES = 1024
VMEM_LIMIT = 56 * 1024 * 1024
EW_BLOCK_ELEMS = 256 * 1024
MESH = pl.DeviceIdType.MESH
HIGHEST = lax.Precision.HIGHEST

_pcall = pl.pallas_call


def _dot(a, b):
    return jnp.dot(a, b, preferred_element_type=F32)


def _dot_tn(a, b):
    return lax.dot_general(a, b, (((0,), (0,)), ((), ())), preferred_element_type=F32)


def _dot_nt(a, b):
    return lax.dot_general(a, b, (((1,), (1,)), ((), ())), preferred_element_type=F32)


def _colsum(v):
    return jnp.sum(v, axis=0, keepdims=True)


def _sigmoid(v):
    return 1.0 / (1.0 + jnp.exp(-v))


def _rows(tm, c):
    return pl.BlockSpec((tm, c), lambda i: (i, 0))


def _full(shape):
    nd = len(shape)
    return pl.BlockSpec(shape, lambda i: (0,) * nd)


_VM = pl.BlockSpec(memory_space=pltpu.VMEM)
_ANY = pl.BlockSpec(memory_space=pl.ANY)


def _halo_specs(tm, c, t_total):
    r = tm // HALO
    last = t_total // HALO - 1
    prev = pl.BlockSpec((HALO, c), lambda i: (jnp.maximum(i * r - 1, 0), 0))
    nxt = pl.BlockSpec((HALO, c), lambda i: (jnp.minimum((i + 1) * r, last), 0))
    return prev, _rows(tm, c), nxt


def _seq_params():
    return pltpu.CompilerParams(dimension_semantics=("arbitrary",), vmem_limit_bytes=VMEM_LIMIT)


def _rms(x):
    r = lax.rsqrt(jnp.mean(x * x, axis=-1, keepdims=True) + RMS_EPS)
    return x * r, r


def _norm_mod(x, g, sh, sc):
    xhat, _ = _rms(x)
    return xhat * g * (1.0 + sc) + sh


def _norm_mod_bwd(dh, x, g, sc):
    xhat, r = _rms(x)
    n = xhat * g
    dsh = _colsum(dh)
    dsc = _colsum(dh * n)
    dn = dh * (1.0 + sc)
    dg = _colsum(dn * xhat)
    dxn = dn * g
    dx = r * (dxn - xhat * jnp.mean(dxn * xhat, axis=-1, keepdims=True))
    return dx, dsh, dsc, dg


def _fill_ext(ext_ref, prev, cur, nxt, i, nsteps, tm):
    ext_ref[0:HALO, :] = jnp.where(i > 0, prev, 0.0)
    ext_ref[HALO:HALO + tm, :] = cur
    ext_ref[HALO + tm:HALO + tm + HALO, :] = jnp.where(i < nsteps - 1, nxt, 0.0)


def _window_count(t, wdw, t_total):
    left = wdw // 2
    right = wdw - 1 - left
    cnt = jnp.minimum(t + right, t_total - 1) - jnp.maximum(t - left, 0) + 1
    return jnp.maximum(cnt, 1).astype(F32)


def _in_proj(x, vec, w, bias, tm):
    t_total, d = x.shape
    n = w.shape[1]
    has_bias = bias is not None

    def body(*refs):
        if has_bias:
            x_ref, vec_ref, w_ref, b_ref, u_ref = refs
        else:
            x_ref, vec_ref, w_ref, u_ref = refs
        h = _norm_mod(x_ref[...], vec_ref[0:1, :], vec_ref[1:2, :], vec_ref[2:3, :])
        u = _dot(h.astype(BF16), w_ref[...])
        if has_bias:
            u = u + b_ref[...]
        u_ref[...] = u.astype(BF16)

    in_specs = [_rows(tm, d), _full(vec.shape), _VM]
    args = [x, vec, w]
    if has_bias:
        in_specs.append(_full(bias.shape))
        args.append(bias)
    return _pcall(
        body, name="in_proj_bias" if has_bias else "in_proj",
        grid=(t_total // tm,), in_specs=in_specs, out_specs=_rows(tm, n),
        out_shape=jax.ShapeDtypeStruct((t_total, n), BF16),
        compiler_params=_seq_params(),
    )(*args)


def _ab_core(up_ref, uc_ref, un_ref, conv_ref, wpool_ref, q_ext, p_ext, i, nsteps, tm, t_total):
    da = uc_ref.shape[1] // 4

    def cols(ref, k):
        return ref[:, k * da:(k + 1) * da].astype(F32)

    _fill_ext(q_ext, cols(up_ref, 1) * cols(up_ref, 2), cols(uc_ref, 1) * cols(uc_ref, 2),
              cols(un_ref, 1) * cols(un_ref, 2), i, nsteps, tm)
    _fill_ext(p_ext, cols(up_ref, 3), cols(uc_ref, 3), cols(un_ref, 3), i, nsteps, tm)
    bg = cols(uc_ref, 0)
    cq = (conv_ref[0:1, :] * q_ext[HALO - 1:HALO - 1 + tm, :] + conv_ref[1:2, :] * q_ext[HALO:HALO + tm, :]
          + conv_ref[2:3, :] * q_ext[HALO + 1:HALO + 1 + tm, :])
    t = i * tm + lax.broadcasted_iota(jnp.int32, (tm, 1), 0)
    gw = da // len(POOL_WINDOWS)
    pooled, ybpre = [], []
    for g, wdw in enumerate(POOL_WINDOWS):
        left = wdw // 2
        right = wdw - 1 - left
        lo, hi = g * gw, (g + 1) * gw
        s = p_ext[HALO - left:HALO - left + tm, lo:hi]
        for o in range(-left + 1, right + 1):
            s = s + p_ext[HALO + o:HALO + o + tm, lo:hi]
        pg = s / _window_count(t, wdw, t_total) - p_ext[HALO:HALO + tm, lo:hi]
        pooled.append(pg.astype(BF16))
        ybpre.append(_dot(pooled[-1], wpool_ref[g]))
    return bg, cq, pooled, jnp.concatenate(ybpre, axis=1)


def _ab_fwd(u, x, vec, conv, wpool, scale, wout, tm):
    t_total, d = x.shape
    nu = u.shape[1]
    da = nu // 4
    nsteps = t_total // tm

    def body(up_ref, uc_ref, un_ref, x_ref, vec_ref, conv_ref, wpool_ref, scale_ref, wout_ref,
             y_ref, x2_ref, q_ext, p_ext):
        i = pl.program_id(0)
        bg, cq, _, ybpre = _ab_core(up_ref, uc_ref, un_ref, conv_ref, wpool_ref, q_ext, p_ext,
                                    i, nsteps, tm, t_total)
        cat = jnp.concatenate([bg * cq, ybpre * scale_ref[...]], axis=1).astype(BF16)
        y = _dot(cat, wout_ref[...])
        y_ref[...] = y.astype(BF16)
        x2_ref[...] = x_ref[...] + vec_ref[0:1, :] * y

    return _pcall(
        body, name="ab_fwd", grid=(nsteps,),
        in_specs=[*_halo_specs(tm, nu, t_total), _rows(tm, d), _full(vec.shape), _full(conv.shape),
                  _full(wpool.shape), _full(scale.shape), _VM],
        out_specs=[_rows(tm, d), _rows(tm, d)],
        out_shape=[jax.ShapeDtypeStruct((t_total, d), BF16), jax.ShapeDtypeStruct((t_total, d), F32)],
        scratch_shapes=[pltpu.VMEM((tm + 2 * HALO, da), F32), pltpu.VMEM((tm + 2 * HALO, da), F32)],
        compiler_params=_seq_params(),
    )(u, u, u, x, vec, conv, wpool, scale, wout)


def _glu_ext(up_ref, uc_ref, un_ref, z_ext, i, nsteps, tm):
    dz = uc_ref.shape[1] // 2

    def glu(ref):
        return ref[:, 0:dz].astype(F32) * _sigmoid(ref[:, dz:2 * dz].astype(F32))

    _fill_ext(z_ext, glu(up_ref), glu(uc_ref), glu(un_ref), i, nsteps, tm)


def _layer_norm_stats(zc):
    mu = jnp.mean(zc, axis=-1, keepdims=True)
    dlt = zc - mu
    rstd = lax.rsqrt(jnp.mean(dlt * dlt, axis=-1, keepdims=True) + LN_EPS)
    return dlt * rstd, rstd


def _cf_fwd(u, x, vec, wdw, wpw2, tm):
    t_total, d = x.shape
    nu = u.shape[1]
    nsteps = t_total // tm
    left = (CONF_KERNEL - 1) // 2

    def body(up_ref, uc_ref, un_ref, x_ref, vec_ref, wdw_ref, wpw2_ref, zc_ref, y_ref, x2_ref, z_ext):
        i = pl.program_id(0)
        _glu_ext(up_ref, uc_ref, un_ref, z_ext, i, nsteps, tm)
        zc = wdw_ref[0:1, :] * z_ext[HALO - left:HALO - left + tm, :]
        for k in range(1, CONF_KERNEL):
            zc = zc + wdw_ref[k:k + 1, :] * z_ext[HALO + k - left:HALO + k - left + tm, :]
        zc = zc + vec_ref[1:2, :]
        zc_ref[...] = zc.astype(BF16)
        zn, _ = _layer_norm_stats(zc)
        zl = zn * vec_ref[2:3, :] + vec_ref[3:4, :]
        zs = zl * _sigmoid(zl)
        y = _dot(zs.astype(BF16), wpw2_ref[...]) + vec_ref[4:5, :]
        y_ref[...] = y.astype(BF16)
        x2_ref[...] = x_ref[...] + vec_ref[0:1, :] * y

    return _pcall(
        body, name="cf_fwd", grid=(nsteps,),
        in_specs=[*_halo_specs(tm, nu, t_total), _rows(tm, d), _full(vec.shape), _full(wdw.shape), _VM],
        out_specs=[_rows(tm, d), _rows(tm, d), _rows(tm, d)],
        out_shape=[jax.ShapeDtypeStruct((t_total, d), BF16), jax.ShapeDtypeStruct((t_total, d), BF16),
                   jax.ShapeDtypeStruct((t_total, d), F32)],
        scratch_shapes=[pltpu.VMEM((tm + 2 * HALO, d), F32)],
        compiler_params=_seq_params(),
    )(u, u, u, x, vec, wdw, wpw2)


def _ffn_fwd(x2, vec, wg, wu, wd, tm):
    t_total, d = x2.shape
    f = wg.shape[1]

    def body(x_ref, vec_ref, wg_ref, wu_ref, wd_ref, a_ref, b_ref, f_ref, x3_ref):
        xv = x_ref[...]
        h = _norm_mod(xv, vec_ref[0:1, :], vec_ref[1:2, :], vec_ref[2:3, :]).astype(BF16)
        a = _dot(h, wg_ref[...])
        b = _dot(h, wu_ref[...])
        a_ref[...] = a.astype(BF16)
        b_ref[...] = b.astype(BF16)
        s = (a * _sigmoid(a) * b).astype(BF16)
        y = _dot(s, wd_ref[...])
        f_ref[...] = y.astype(BF16)
        x3_ref[...] = xv + vec_ref[3:4, :] * y

    return _pcall(
        body, name="ffn_fwd", grid=(t_total // tm,),
        in_specs=[_rows(tm, d), _full(vec.shape), _VM, _VM, _VM],
        out_specs=[_rows(tm, f), _rows(tm, f), _rows(tm, d), _rows(tm, d)],
        out_shape=[jax.ShapeDtypeStruct((t_total, f), BF16), jax.ShapeDtypeStruct((t_total, f), BF16),
                   jax.ShapeDtypeStruct((t_total, d), BF16), jax.ShapeDtypeStruct((t_total, d), F32)],
        compiler_params=_seq_params(),
    )(x2, vec, wg, wu, wd)


def _final_fwd_bwd(x, target, vec, tm):
    t_total, d = x.shape

    def body(x_ref, t_ref, vec_ref, dx_ref, acc_ref):
        @pl.when(pl.program_id(0) == 0)
        def _():
            acc_ref[...] = jnp.zeros_like(acc_ref)

        g = vec_ref[0:1, :]
        xhat, r = _rms(x_ref[...])
        e = xhat * g - t_ref[...]
        acc_ref[1:2, :] += jnp.zeros((1, d), F32) + 0.5 * jnp.sum(jnp.mean(e * e, axis=-1, keepdims=True))
        dout = e * (1.0 / d)
        acc_ref[0:1, :] += _colsum(dout * xhat)
        dxn = dout * g
        dx_ref[...] = r * (dxn - xhat * jnp.mean(dxn * xhat, axis=-1, keepdims=True))

    return _pcall(
        body, name="final_fwd_bwd", grid=(t_total // tm,),
        in_specs=[_rows(tm, d), _rows(tm, d), _full(vec.shape)],
        out_specs=[_rows(tm, d), _VM],
        out_shape=[jax.ShapeDtypeStruct((t_total, d), F32), jax.ShapeDtypeStruct((8, d), F32)],
        compiler_params=_seq_params(),
    )(x, target, vec)


def _zero_at_start(*refs):
    @pl.when(pl.program_id(0) == 0)
    def _():
        for ref in refs:
            ref[...] = jnp.zeros_like(ref)


def _ffn_bwd_down(dx3, fout, a, b, vec, wd, tm):
    t_total, d = dx3.shape
    f = a.shape[1]

    def body(dx_ref, f_ref, a_ref, b_ref, vec_ref, wd_ref, da_ref, db_ref, dwd_ref, acc_ref):
        _zero_at_start(dwd_ref, acc_ref)
        dx = dx_ref[...]
        acc_ref[0:1, :] += _colsum(dx * f_ref[...].astype(F32))
        dy = (dx * vec_ref[0:1, :]).astype(BF16)
        av = a_ref[...].astype(F32)
        bv = b_ref[...].astype(F32)
        sg = _sigmoid(av)
        silu = av * sg
        dwd_ref[...] += _dot_tn((silu * bv).astype(BF16), dy)
        ds = _dot_nt(dy, wd_ref[...])
        da_ref[...] = (ds * bv * (sg * (1.0 + av * (1.0 - sg)))).astype(BF16)
        db_ref[...] = (ds * silu).astype(BF16)

    return _pcall(
        body, name="ffn_bwd_down", grid=(t_total // tm,),
        in_specs=[_rows(tm, d), _rows(tm, d), _rows(tm, f), _rows(tm, f), _full(vec.shape), _VM],
        out_specs=[_rows(tm, f), _rows(tm, f), _VM, _VM],
        out_shape=[jax.ShapeDtypeStruct((t_total, f), BF16), jax.ShapeDtypeStruct((t_total, f), BF16),
                   jax.ShapeDtypeStruct(wd.shape, F32), jax.ShapeDtypeStruct((8, d), F32)],
        compiler_params=_seq_params(),
    )(dx3, fout, a, b, vec, wd)


def _ffn_bwd_up(da, db, x2, dx3, vec, wg, wu, tm):
    t_total, d = x2.shape
    f = da.shape[1]

    def body(da_ref, db_ref, x_ref, dx_ref, vec_ref, wg_ref, wu_ref, dx2_ref, dwg_ref, dwu_ref, acc_ref):
        _zero_at_start(dwg_ref, dwu_ref, acc_ref)
        xv = x_ref[...]
        g, sh, sc = vec_ref[0:1, :], vec_ref[1:2, :], vec_ref[2:3, :]
        h = _norm_mod(xv, g, sh, sc).astype(BF16)
        dav = da_ref[...]
        dbv = db_ref[...]
        dwg_ref[...] += _dot_tn(h, dav)
        dwu_ref[...] += _dot_tn(h, dbv)
        dh = _dot_nt(dav, wg_ref[...]) + _dot_nt(dbv, wu_ref[...])
        dxn, dsh, dsc, dg = _norm_mod_bwd(dh, xv, g, sc)
        acc_ref[0:1, :] += dsh
        acc_ref[1:2, :] += dsc
        acc_ref[2:3, :] += dg
        dx2_ref[...] = dx_ref[...] + dxn

    return _pcall(
        body, name="ffn_bwd_up", grid=(t_total // tm,),
        in_specs=[_rows(tm, f), _rows(tm, f), _rows(tm, d), _rows(tm, d), _full(vec.shape), _VM, _VM],
        out_specs=[_rows(tm, d), _VM, _VM, _VM],
        out_shape=[jax.ShapeDtypeStruct((t_total, d), F32), jax.ShapeDtypeStruct(wg.shape, F32),
                   jax.ShapeDtypeStruct(wu.shape, F32), jax.ShapeDtypeStruct((8, d), F32)],
        compiler_params=_seq_params(),
    )(da, db, x2, dx3, vec, wg, wu)


def _ab_bwd_out(dx, y, u, vec, conv, wpool, scale, wout, tm):
    t_total, d = dx.shape
    nu = u.shape[1]
    da = nu // 4
    gw = da // len(POOL_WINDOWS)
    nsteps = t_total // tm

    def body(dx_ref, y_ref, up_ref, uc_ref, un_ref, vec_ref, conv_ref, wpool_ref, scale_ref, wout_ref,
             dpre_ref, dwout_ref, dwpool_ref, acc_ref, q_ext, p_ext):
        _zero_at_start(dwout_ref, dwpool_ref, acc_ref)
        i = pl.program_id(0)
        dxv = dx_ref[...]
        acc_ref[0:1, :] += _colsum(dxv * y_ref[...].astype(F32))
        dy = (dxv * vec_ref[0:1, :]).astype(BF16)
        bg, cq, pooled, ybpre = _ab_core(up_ref, uc_ref, un_ref, conv_ref, wpool_ref, q_ext, p_ext,
                                         i, nsteps, tm, t_total)
        cat = jnp.concatenate([bg * cq, ybpre * scale_ref[...]], axis=1).astype(BF16)
        dwout_ref[...] += _dot_tn(cat, dy)
        dcat = _dot_nt(dy, wout_ref[...])
        dya = dcat[:, 0:da]
        dyb = dcat[:, da:2 * da]
        acc_ref[1:2, 0:da] += _colsum(dyb * ybpre)
        dybpre = (dyb * scale_ref[...]).astype(BF16)
        dpooled = []
        for g in range(len(POOL_WINDOWS)):
            dg = dybpre[:, g * gw:(g + 1) * gw]
            dwpool_ref[g] += _dot_tn(pooled[g], dg)
            dpooled.append(_dot_nt(dg, wpool_ref[g]))
        dpre_ref[...] = jnp.concatenate([dya * cq, dya * bg] + dpooled, axis=1).astype(BF16)

    return _pcall(
        body, name="ab_bwd_out", grid=(nsteps,),
        in_specs=[_rows(tm, d), _rows(tm, d), *_halo_specs(tm, nu, t_total), _full(vec.shape),
                  _full(conv.shape), _full(wpool.shape), _full(scale.shape), _VM],
        out_specs=[_rows(tm, 3 * da), _VM, _VM, _VM],
        out_shape=[jax.ShapeDtypeStruct((t_total, 3 * da), BF16), jax.ShapeDtypeStruct(wout.shape, F32),
                   jax.ShapeDtypeStruct(wpool.shape, F32), jax.ShapeDtypeStruct((8, d), F32)],
        scratch_shapes=[pltpu.VMEM((tm + 2 * HALO, da), F32), pltpu.VMEM((tm + 2 * HALO, da), F32)],
        compiler_params=_seq_params(),
    )(dx, y, u, u, u, vec, conv, wpool, scale, wout)


def _ab_bwd_in(dpre, u, x, dx, vec, conv, win, tm):
    t_total, d = x.shape
    nu = u.shape[1]
    da = nu // 4
    gw = da // len(POOL_WINDOWS)
    nsteps = t_total // tm

    def body(dp_ref, dc_ref, dn_ref, up_ref, uc_ref, un_ref, x_ref, dx_ref, vec_ref, conv_ref, win_ref,
             dxin_ref, dwin_ref, dconv_ref, acc_ref, dcq_ext, q_ext, dpl_ext):
        _zero_at_start(dwin_ref, dconv_ref, acc_ref)
        i = pl.program_id(0)

        def ucols(ref, k):
            return ref[:, k * da:(k + 1) * da].astype(F32)

        def dcols(ref, k):
            return ref[:, k * da:(k + 1) * da].astype(F32)

        _fill_ext(dcq_ext, dcols(dp_ref, 1), dcols(dc_ref, 1), dcols(dn_ref, 1), i, nsteps, tm)
        _fill_ext(q_ext, ucols(up_ref, 1) * ucols(up_ref, 2), ucols(uc_ref, 1) * ucols(uc_ref, 2),
                  ucols(un_ref, 1) * ucols(un_ref, 2), i, nsteps, tm)
        _fill_ext(dpl_ext, dcols(dp_ref, 2), dcols(dc_ref, 2), dcols(dn_ref, 2), i, nsteps, tm)
        dq = (conv_ref[0:1, :] * dcq_ext[HALO + 1:HALO + 1 + tm, :] + conv_ref[1:2, :] * dcq_ext[HALO:HALO + tm, :]
              + conv_ref[2:3, :] * dcq_ext[HALO - 1:HALO - 1 + tm, :])
        dcq = dcq_ext[HALO:HALO + tm, :]
        for k in range(3):
            dconv_ref[k:k + 1, :] += _colsum(dcq * q_ext[HALO + k - 1:HALO + k - 1 + tm, :])
        dcg = dq * ucols(uc_ref, 2)
        dv = dq * ucols(uc_ref, 1)
        t_ext = i * tm - HALO + lax.broadcasted_iota(jnp.int32, (tm + 2 * HALO, 1), 0)
        dps = []
        for g, wdw in enumerate(POOL_WINDOWS):
            left = wdw // 2
            right = wdw - 1 - left
            lo, hi = g * gw, (g + 1) * gw
            dpg = dpl_ext[HALO:HALO + tm, lo:hi]
            dpl_ext[:, lo:hi] = dpl_ext[:, lo:hi] / _window_count(t_ext, wdw, t_total)
            s = dpl_ext[HALO - right:HALO - right + tm, lo:hi]
            for o in range(-right + 1, left + 1):
                s = s + dpl_ext[HALO + o:HALO + o + tm, lo:hi]
            dps.append(s - dpg)
        du = jnp.concatenate([dcols(dc_ref, 0), dcg, dv] + dps, axis=1).astype(BF16)
        xv = x_ref[...]
        g, sh, sc = vec_ref[0:1, :], vec_ref[1:2, :], vec_ref[2:3, :]
        h = _norm_mod(xv, g, sh, sc).astype(BF16)
        dwin_ref[...] += _dot_tn(h, du)
        dh = _dot_nt(du, win_ref[...])
        dxn, dsh, dsc, dg = _norm_mod_bwd(dh, xv, g, sc)
        acc_ref[0:1, :] += dsh
        acc_ref[1:2, :] += dsc
        acc_ref[2:3, :] += dg
        dxin_ref[...] = dx_ref[...] + dxn

    ext = pltpu.VMEM((tm + 2 * HALO, da), F32)
    return _pcall(
        body, name="ab_bwd_in", grid=(nsteps,),
        in_specs=[*_halo_specs(tm, 3 * da, t_total), *_halo_specs(tm, nu, t_total), _rows(tm, d), _rows(tm, d),
                  _full(vec.shape), _full(conv.shape), _VM],
        out_specs=[_rows(tm, d), _VM, _VM, _VM],
        out_shape=[jax.ShapeDtypeStruct((t_total, d), F32), jax.ShapeDtypeStruct(win.shape, F32),
                   jax.ShapeDtypeStruct((8, da), F32), jax.ShapeDtypeStruct((8, d), F32)],
        scratch_shapes=[ext, ext, ext],
        compiler_params=_seq_params(),
    )(dpre, dpre, dpre, u, u, u, x, dx, vec, conv, win)


def _cf_bwd_out(dx, y, zc, vec, wpw2, tm):
    t_total, d = dx.shape

    def body(dx_ref, y_ref, zc_ref, vec_ref, w_ref, dzc_ref, dw_ref, acc_ref):
        _zero_at_start(dw_ref, acc_ref)
        dxv = dx_ref[...]
        acc_ref[0:1, :] += _colsum(dxv * y_ref[...].astype(F32))
        dyf = dxv * vec_ref[0:1, :]
        acc_ref[1:2, :] += _colsum(dyf)
        dy = dyf.astype(BF16)
        zn, rstd = _layer_norm_stats(zc_ref[...].astype(F32))
        lng = vec_ref[1:2, :]
        zl = zn * lng + vec_ref[2:3, :]
        sg = _sigmoid(zl)
        dw_ref[...] += _dot_tn((zl * sg).astype(BF16), dy)
        dzl = _dot_nt(dy, w_ref[...]) * (sg * (1.0 + zl * (1.0 - sg)))
        acc_ref[2:3, :] += _colsum(dzl * zn)
        acc_ref[3:4, :] += _colsum(dzl)
        dzn = dzl * lng
        dzc = rstd * (dzn - jnp.mean(dzn, axis=-1, keepdims=True)
                      - zn * jnp.mean(dzn * zn, axis=-1, keepdims=True))
        acc_ref[4:5, :] += _colsum(dzc)
        dzc_ref[...] = dzc.astype(BF16)

    return _pcall(
        body, name="cf_bwd_out", grid=(t_total // tm,),
        in_specs=[_rows(tm, d), _rows(tm, d), _rows(tm, d), _full(vec.shape), _VM],
        out_specs=[_rows(tm, d), _VM, _VM],
        out_shape=[jax.ShapeDtypeStruct((t_total, d), BF16), jax.ShapeDtypeStruct(wpw2.shape, F32),
                   jax.ShapeDtypeStruct((8, d), F32)],
        compiler_params=_seq_params(),
    )(dx, y, zc, vec, wpw2)


def _cf_bwd_in(dzc, u, x, dx, vec, wdw, wpw1, tm):
    t_total, d = x.shape
    nu = u.shape[1]
    nsteps = t_total // tm
    left = (CONF_KERNEL - 1) // 2

    def body(dp_ref, dc_ref, dn_ref, up_ref, uc_ref, un_ref, x_ref, dx_ref, vec_ref, wdw_ref, w_ref,
             dxin_ref, dw_ref, dwdw_ref, db1_ref, acc_ref, dzc_ext, z_ext):
        _zero_at_start(dw_ref, dwdw_ref, db1_ref, acc_ref)
        i = pl.program_id(0)
        _fill_ext(dzc_ext, dp_ref[...].astype(F32), dc_ref[...].astype(F32), dn_ref[...].astype(F32),
                  i, nsteps, tm)
        _glu_ext(up_ref, uc_ref, un_ref, z_ext, i, nsteps, tm)
        dzc = dzc_ext[HALO:HALO + tm, :]
        dz = wdw_ref[0:1, :] * dzc_ext[HALO + left:HALO + left + tm, :]
        dwdw_ref[0:1, :] += _colsum(dzc * z_ext[HALO - left:HALO - left + tm, :])
        for k in range(1, CONF_KERNEL):
            dz = dz + wdw_ref[k:k + 1, :] * dzc_ext[HALO + left - k:HALO + left - k + tm, :]
            dwdw_ref[k:k + 1, :] += _colsum(dzc * z_ext[HALO + k - left:HALO + k - left + tm, :])
        av = uc_ref[:, 0:d].astype(F32)
        sg = _sigmoid(uc_ref[:, d:2 * d].astype(F32))
        duf = jnp.concatenate([dz * sg, dz * av * sg * (1.0 - sg)], axis=1)
        db1_ref[0:1, :] += _colsum(duf)
        du = duf.astype(BF16)
        xv = x_ref[...]
        g, sh, sc = vec_ref[0:1, :], vec_ref[1:2, :], vec_ref[2:3, :]
        h = _norm_mod(xv, g, sh, sc).astype(BF16)
        dw_ref[...] += _dot_tn(h, du)
        dh = _dot_nt(du, w_ref[...])
        dxn, dsh, dsc, dg = _norm_mod_bwd(dh, xv, g, sc)
        acc_ref[0:1, :] += dsh
        acc_ref[1:2, :] += dsc
        acc_ref[2:3, :] += dg
        dxin_ref[...] = dx_ref[...] + dxn

    ext = pltpu.VMEM((tm + 2 * HALO, d), F32)
    return _pcall(
        body, name="cf_bwd_in", grid=(nsteps,),
        in_specs=[*_halo_specs(tm, d, t_total), *_halo_specs(tm, nu, t_total), _rows(tm, d), _rows(tm, d),
                  _full(vec.shape), _full(wdw.shape), _VM],
        out_specs=[_rows(tm, d), _VM, _VM, _VM, _VM],
        out_shape=[jax.ShapeDtypeStruct((t_total, d), F32), jax.ShapeDtypeStruct(wpw1.shape, F32),
                   jax.ShapeDtypeStruct((32, d), F32), jax.ShapeDtypeStruct((8, nu), F32),
                   jax.ShapeDtypeStruct((8, d), F32)],
        scratch_shapes=[ext, ext],
        compiler_params=_seq_params(),
    )(dzc, dzc, dzc, u, u, u, x, dx, vec, wdw, wpw1)


def _mod_fwd(c_all, w_mod, b_cols):
    nl, d, ncol = w_mod.shape
    nb = c_all.shape[0]

    def body(c_ref, w_ref, b_ref, o_ref):
        cv = c_ref[...]
        ca = cv * _sigmoid(cv)
        o_ref[0] = jnp.dot(ca, w_ref[0], preferred_element_type=F32, precision=HIGHEST) + b_ref[0]

    return _pcall(
        body, name="mod_fwd", grid=(nl,),
        in_specs=[_full(c_all.shape), pl.BlockSpec((1, d, ncol), lambda l: (l, 0, 0)),
                  pl.BlockSpec((1, 1, ncol), lambda l: (l, 0, 0))],
        out_specs=pl.BlockSpec((1, nb, ncol), lambda l: (l, 0, 0)),
        out_shape=jax.ShapeDtypeStruct((nl, nb, ncol), F32),
        compiler_params=_seq_params(),
    )(c_all, w_mod, b_cols.reshape(nl, 1, ncol))


def _mod_bwd(c_all_t, dmod_cols):
    d, nb = c_all_t.shape
    nl, _, ncol = dmod_cols.shape

    def body(c_ref, dm_ref, o_ref):
        cv = c_ref[...]
        ca = cv * _sigmoid(cv)
        o_ref[0] = jnp.dot(ca, dm_ref[0], preferred_element_type=F32, precision=HIGHEST)

    return _pcall(
        body, name="mod_bwd", grid=(nl,),
        in_specs=[_full(c_all_t.shape), pl.BlockSpec((1, nb, ncol), lambda l: (l, 0, 0))],
        out_specs=pl.BlockSpec((1, d, ncol), lambda l: (l, 0, 0)),
        out_shape=jax.ShapeDtypeStruct((nl, d, ncol), F32),
        compiler_params=_seq_params(),
    )(c_all_t, dmod_cols)


def _row_block(r, c):
    if r * c <= EW_BLOCK_ELEMS:
        return r
    best = None
    for br in range(8, r, 8):
        if r % br == 0 and br * c <= EW_BLOCK_ELEMS:
            best = br
    assert best is not None, (r, c)
    return best


def _as2d(a):
    return a.reshape(-1, a.shape[-1])


def _sum_parts(parts):
    k = parts.shape[0]
    p3 = parts.reshape(k, -1, parts.shape[-1])
    _, r, c = p3.shape
    br = _row_block(r, c)

    def body(p_ref, o_ref):
        acc = p_ref[0].astype(F32)
        for j in range(1, k):
            acc = acc + p_ref[j].astype(F32)
        o_ref[...] = acc

    out = _pcall(
        body, name="sum_parts", grid=(r // br,),
        in_specs=[pl.BlockSpec((k, br, c), lambda i: (0, i, 0))],
        out_specs=pl.BlockSpec((br, c), lambda i: (i, 0)),
        out_shape=jax.ShapeDtypeStruct((r, c), F32),
        compiler_params=_seq_params(),
    )(p3)
    return out.reshape(parts.shape[1:])


def _adamw(w, gparts, m, v):
    shape = w.shape
    w2, m2, v2 = _as2d(w), _as2d(m), _as2d(v)
    g2 = [_as2d(g) for g in gparts]
    r, c = w2.shape
    br = _row_block(r, c)
    ng = len(g2)

    def body(*refs):
        w_ref, m_ref, v_ref = refs[0:3]
        g_refs = refs[3:3 + ng]
        go_ref, d_ref, mo_ref, vo_ref = refs[3 + ng:]
        g = g_refs[0][...]
        for gr in g_refs[1:]:
            g = g + gr[...]
        mn = ADAM_B1 * m_ref[...] + (1.0 - ADAM_B1) * g
        vn = ADAM_B2 * v_ref[...] + (1.0 - ADAM_B2) * (g * g)
        m_hat = mn / (1.0 - ADAM_B1 ** ADAM_STEP)
        v_hat = vn / (1.0 - ADAM_B2 ** ADAM_STEP)
        go_ref[...] = g
        d_ref[...] = -ADAM_LR * (m_hat / (jnp.sqrt(v_hat) + ADAM_EPS) + ADAM_WD * w_ref[...])
        mo_ref[...] = mn
        vo_ref[...] = vn

    spec = pl.BlockSpec((br, c), lambda i: (i, 0))
    outs = _pcall(
        body, name="adamw", grid=(r // br,),
        in_specs=[spec] * (3 + ng), out_specs=[spec] * 4,
        out_shape=[jax.ShapeDtypeStruct((r, c), F32)] * 4,
        compiler_params=_seq_params(),
    )(w2, m2, v2, *g2)
    return tuple(o.reshape(shape) for o in outs)


def _place():
    return lax.axis_index("x"), lax.axis_index("y"), lax.axis_index("c")


def _allgather8(block, with_sum):
    m_per, n = block.shape

    def body(x_ref, out_ref, *rest):
        if with_sum:
            sum_ref, send_sems, recv_sems, local_sem = rest
        else:
            send_sems, recv_sems, local_sem = rest
        x, y, c = _place()
        me, sibling = (x, y, c), (x, y, 1 - c)
        chips = [(1 - x, y), (x, 1 - y), (1 - x, 1 - y)]

        def rows(px, py, pc):
            return out_ref.at[pl.ds((4 * px + 2 * py + pc) * m_per, m_per), :]

        def copy(k, blk, to, src=None):
            return pltpu.make_async_remote_copy(
                src_ref=rows(*blk) if src is None else src, dst_ref=rows(*blk),
                send_sem=send_sems.at[k], recv_sem=recv_sems.at[k], device_id=to, device_id_type=MESH)

        mine = pltpu.make_async_copy(x_ref, rows(*me), local_sem)
        mine.start()
        first = [copy(0, me, sibling, src=x_ref)]
        first += [copy(1 + j, me, (*chip, c), src=x_ref) for j, chip in enumerate(chips)]
        for cp in first:
            cp.start()
        passed = [copy(4 + j, (*chip, c), sibling) for j, chip in enumerate(chips)]
        for j, chip in enumerate(chips):
            copy(1 + j, (*chip, c), me).wait_recv()
            passed[j].start()
        copy(0, sibling, me).wait_recv()
        for j, chip in enumerate(chips):
            copy(4 + j, (*chip, 1 - c), me).wait_recv()
        for cp in first + passed:
            cp.wait_send()
        mine.wait()
        if with_sum:
            acc = out_ref[0:m_per, :]
            for k in range(1, N_DEV):
                acc = acc + out_ref[k * m_per:(k + 1) * m_per, :]
            sum_ref[...] = acc

    out_shape = [jax.ShapeDtypeStruct((N_DEV * m_per, n), F32)]
    out_specs = [_VM]
    if with_sum:
        out_shape.append(jax.ShapeDtypeStruct((m_per, n), F32))
        out_specs.append(_VM)
    res = _pcall(
        body, name="allgather8_sum" if with_sum else "allgather8",
        in_specs=[_VM], out_specs=out_specs, out_shape=out_shape,
        scratch_shapes=[pltpu.SemaphoreType.DMA((7,)), pltpu.SemaphoreType.DMA((7,)), pltpu.SemaphoreType.DMA],
        compiler_params=pltpu.CompilerParams(vmem_limit_bytes=VMEM_LIMIT),
    )(block)
    return res if with_sum else res[0]


def _chip_exchange(arrs, scatter):
    n = len(arrs)

    def body(*refs):
        ins, outs = refs[:n], refs[n:2 * n]
        send_sems, recv_sems, local_sems = refs[2 * n:]
        x, y, c = _place()
        me = 2 * x + y
        peers = [(1 - x, y), (x, 1 - y), (1 - x, 1 - y)]
        started = []
        for j in range(n):
            src_me = ins[j].at[me] if scatter else ins[j]
            loc = pltpu.make_async_copy(src_me, outs[j].at[me], local_sems.at[j])
            loc.start()
            started.append(loc)
        sends = []
        for j in range(n):
            for k, (px, py) in enumerate(peers):
                src = ins[j].at[2 * px + py] if scatter else ins[j]
                cp = pltpu.make_async_remote_copy(
                    src_ref=src, dst_ref=outs[j].at[me], send_sem=send_sems.at[3 * j + k],
                    recv_sem=recv_sems.at[3 * j + k], device_id=(px, py, c), device_id_type=MESH)
                cp.start()
                sends.append(cp)
        for j in range(n):
            for k, (px, py) in enumerate(peers):
                src = ins[j].at[me] if scatter else ins[j]
                pltpu.make_async_remote_copy(
                    src_ref=src, dst_ref=outs[j].at[2 * px + py], send_sem=send_sems.at[3 * j + k],
                    recv_sem=recv_sems.at[3 * j + k], device_id=(px, py, c), device_id_type=MESH).wait_recv()
        for cp in sends:
            cp.wait_send()
        for loc in started:
            loc.wait()

    if scatter:
        out_shape = [jax.ShapeDtypeStruct(a.shape, a.dtype) for a in arrs]
    else:
        out_shape = [jax.ShapeDtypeStruct((N_CHIPS,) + a.shape, a.dtype) for a in arrs]
    return _pcall(
        body, name="chip_scatter" if scatter else "chip_gather",
        in_specs=[_ANY] * n, out_specs=[_ANY] * n, out_shape=out_shape,
        scratch_shapes=[pltpu.SemaphoreType.DMA((3 * n,)), pltpu.SemaphoreType.DMA((3 * n,)),
                        pltpu.SemaphoreType.DMA((n,))],
    )(*arrs)


def _core_swap(arrs):
    n = len(arrs)

    def body(*refs):
        ins, outs = refs[:n], refs[n:2 * n]
        send_sems, recv_sems = refs[2 * n:]
        x, y, c = _place()
        cps = []
        for j in range(n):
            cp = pltpu.make_async_remote_copy(
                src_ref=ins[j], dst_ref=outs[j], send_sem=send_sems.at[j], recv_sem=recv_sems.at[j],
                device_id=(x, y, 1 - c), device_id_type=MESH)
            cp.start()
            cps.append(cp)
        for cp in cps:
            cp.wait()

    return _pcall(
        body, name="core_swap",
        in_specs=[_ANY] * n, out_specs=[_ANY] * n,
        out_shape=[jax.ShapeDtypeStruct(a.shape, a.dtype) for a in arrs],
        scratch_shapes=[pltpu.SemaphoreType.DMA((n,)), pltpu.SemaphoreType.DMA((n,))],
    )(*arrs)


def _cols_to_chips(w):
    *lead, a, nb = w.shape
    w = w.reshape(*lead, a, N_CHIPS, nb // N_CHIPS)
    return jnp.moveaxis(w, -2, 0)


def _chips_to_cols(g):
    g = jnp.moveaxis(g, 0, -2)
    *lead, a, k, b = g.shape
    return g.reshape(*lead, a, k * b)


def _rows_to_chips(w):
    l, na, b = w.shape
    return jnp.moveaxis(w.reshape(l, N_CHIPS, na // N_CHIPS, b), 1, 0)


def _chips_to_rows(g):
    k, l, a, b = g.shape
    return jnp.moveaxis(g, 0, 1).reshape(l, k * a, b)


def _my_cols(full, chip):
    w = full.shape[-1] // N_CHIPS
    return lax.dynamic_slice_in_dim(full, chip * w, w, axis=full.ndim - 1)


def _pad_rows(a, rows):
    return jnp.pad(a, ((0, rows - a.shape[0]), (0, 0)))


def _to_lanes(a):
    flat = a.reshape(-1)
    n = -(-flat.shape[0] // (8 * LANES)) * (8 * LANES)
    return jnp.pad(flat, (0, n - flat.shape[0])).reshape(-1, LANES)


class _Packer:
    def __init__(self):
        self.items = []
        self.rows = 0

    def add(self, name, a):
        lanes = _to_lanes(a)
        self.items.append((name, self.rows, a.shape, lanes))
        self.rows += lanes.shape[0]

    def pack(self):
        total = -(-self.rows // 8) * 8
        return _pad_rows(jnp.concatenate([it[3] for it in self.items], axis=0), total)

    def unpack(self, buf):
        out = {}
        for name, row, shape, lanes in self.items:
            size = 1
            for s in shape:
                size *= s
            out[name] = buf[row:row + lanes.shape[0]].reshape(-1)[:size].reshape(shape)
        return out


TM_SEQ = 512
TM_FFN = 256


def _local_step(x, target, mods, p):
    t_total, d = x.shape
    depth = mods.shape[0]
    tm = min(TM_SEQ, t_total)
    tmf = min(TM_FFN, t_total)
    saved = []
    xin = x
    for layer in range(depth):
        i = layer // 2
        sh1, sc1, g1, sh2, sc2, g2 = (mods[layer, k:k + 1] for k in range(6))
        vec_in = jnp.concatenate([p["norm_mix_g"][layer:layer + 1], sh1, sc1], axis=0)
        if layer % 2 == 0:
            u = _in_proj(xin, vec_in, p["ab_w_in"][i], None, tm)
            y, x2 = _ab_fwd(u, xin, g1, p["ab_conv"][i], p["ab_w_pool"][i].astype(BF16), p["ab_pool_scale"][i:i + 1],
                            p["ab_w_out"][i], tm)
            zc = None
        else:
            u = _in_proj(xin, vec_in, p["cf_w_pw1"][i], p["cf_b_pw1"][i:i + 1], tm)
            vec_cf = jnp.concatenate([g1, p["cf_b_dw"][i:i + 1], p["cf_ln_g"][i:i + 1], p["cf_ln_b"][i:i + 1],
                                      p["cf_b_pw2"][i:i + 1]], axis=0)
            zc, y, x2 = _cf_fwd(u, xin, vec_cf, _pad_rows(p["cf_w_dw"][i], 32), p["cf_w_pw2"][i], tm)
        vec_ffn = jnp.concatenate([p["norm_ffn_g"][layer:layer + 1], sh2, sc2, g2], axis=0)
        a, b, fout, x3 = _ffn_fwd(x2, vec_ffn, p["ffn_w_gate"][layer], p["ffn_w_up"][layer],
                                  p["ffn_w_down"][layer], tmf)
        saved.append((xin, u, y, zc, x2, a, b, fout))
        xin = x3

    dx, fin = _final_fwd_bwd(xin, target, p["final_norm_g"].reshape(1, d), tm)
    grads = {"final_norm_g": fin[0], "loss": fin[1, 0:1]}
    per_layer = {k: [None] * depth for k in ("norm_mix_g", "norm_ffn_g", "ffn_w_gate", "ffn_w_up", "ffn_w_down")}
    half = {k: [None] * (depth // 2) for k in (
        "ab_w_in", "ab_conv", "ab_w_pool", "ab_pool_scale", "ab_w_out", "cf_w_pw1", "cf_b_pw1", "cf_w_dw",
        "cf_b_dw", "cf_ln_g", "cf_ln_b", "cf_w_pw2", "cf_b_pw2")}
    dmods = [None] * depth
    for layer in reversed(range(depth)):
        i = layer // 2
        xin, u, y, zc, x2, a, b, fout = saved[layer]
        sh1, sc1, g1, sh2, sc2, g2 = (mods[layer, k:k + 1] for k in range(6))
        da, db, dwd, acc_d = _ffn_bwd_down(dx, fout, a, b, g2, p["ffn_w_down"][layer], tmf)
        vec_ffn = jnp.concatenate([p["norm_ffn_g"][layer:layer + 1], sh2, sc2], axis=0)
        dx2, dwg, dwu, acc_u = _ffn_bwd_up(da, db, x2, dx, vec_ffn, p["ffn_w_gate"][layer],
                                           p["ffn_w_up"][layer], tmf)
        per_layer["ffn_w_down"][layer] = dwd
        per_layer["ffn_w_gate"][layer] = dwg
        per_layer["ffn_w_up"][layer] = dwu
        per_layer["norm_ffn_g"][layer] = acc_u[2]
        vec_in = jnp.concatenate([p["norm_mix_g"][layer:layer + 1], sh1, sc1], axis=0)
        if layer % 2 == 0:
            dpre, dwout, dwpool, acc_o = _ab_bwd_out(dx2, y, u, g1, p["ab_conv"][i], p["ab_w_pool"][i].astype(BF16),
                                                     p["ab_pool_scale"][i:i + 1], p["ab_w_out"][i], tm)
            dx, dwin, dconv, acc_i = _ab_bwd_in(dpre, u, xin, dx2, vec_in, p["ab_conv"][i], p["ab_w_in"][i], tm)
            half["ab_w_out"][i] = dwout
            half["ab_w_pool"][i] = dwpool
            half["ab_pool_scale"][i] = acc_o[1, 0:d // 2]
            half["ab_w_in"][i] = dwin
            half["ab_conv"][i] = dconv[0:3]
        else:
            vec_cf = jnp.concatenate([g1, p["cf_ln_g"][i:i + 1], p["cf_ln_b"][i:i + 1]], axis=0)
            dzc, dwpw2, acc_o = _cf_bwd_out(dx2, y, zc, vec_cf, p["cf_w_pw2"][i], tm)
            dx, dwpw1, dwdw, db1, acc_i = _cf_bwd_in(dzc, u, xin, dx2, vec_in, _pad_rows(p["cf_w_dw"][i], 32),
                                                     p["cf_w_pw1"][i], tm)
            half["cf_w_pw2"][i] = dwpw2
            half["cf_b_pw2"][i] = acc_o[1]
            half["cf_ln_g"][i] = acc_o[2]
            half["cf_ln_b"][i] = acc_o[3]
            half["cf_b_dw"][i] = acc_o[4]
            half["cf_w_pw1"][i] = dwpw1
            half["cf_w_dw"][i] = dwdw[0:CONF_KERNEL]
            half["cf_b_pw1"][i] = db1[0]
        per_layer["norm_mix_g"][layer] = acc_i[2]
        dmods[layer] = jnp.stack([acc_i[0], acc_i[1], acc_o[0], acc_u[0], acc_u[1], acc_d[0]], axis=0)
    for k, v in {**per_layer, **half}.items():
        grads[k] = jnp.stack(v, axis=0)
    return dx, grads, jnp.stack(dmods, axis=0)


BIG_COLS = ("ab_w_in", "cf_w_pw1", "ffn_w_gate", "ffn_w_up")
BIG_ROWS = ("ab_w_out", "cf_w_pw2", "ffn_w_down")
SMALL_COLS = ("ab_conv", "cf_b_pw1", "cf_w_dw", "cf_b_dw", "cf_ln_g", "cf_ln_b", "cf_b_pw2")
SMALL_REPL = ("norm_mix_g", "norm_ffn_g", "ab_w_pool", "ab_pool_scale", "final_norm_g")
WEIGHTS = ("norm_mix_g", "norm_ffn_g", "w_mod", "b_mod", "ab_w_in", "ab_conv", "ab_w_pool", "ab_pool_scale",
           "ab_w_out", "cf_w_pw1", "cf_b_pw1", "cf_w_dw", "cf_b_dw", "cf_ln_g", "cf_ln_b", "cf_w_pw2",
           "cf_b_pw2", "ffn_w_gate", "ffn_w_up", "ffn_w_down", "final_norm_g")


def kernel(x, c, norm_mix_g, norm_ffn_g, w_mod, b_mod, ab_w_in, ab_conv, ab_w_pool, ab_pool_scale, ab_w_out, cf_w_pw1, cf_b_pw1, cf_w_dw, cf_b_dw, cf_ln_g, cf_ln_b, cf_w_pw2, cf_b_pw2, ffn_w_gate, ffn_w_up, ffn_w_down, final_norm_g, loss_target, m_norm_mix_g, m_norm_ffn_g, m_w_mod, m_b_mod, m_ab_w_in, m_ab_conv, m_ab_w_pool, m_ab_pool_scale, m_ab_w_out, m_cf_w_pw1, m_cf_b_pw1, m_cf_w_dw, m_cf_b_dw, m_cf_ln_g, m_cf_ln_b, m_cf_w_pw2, m_cf_b_pw2, m_ffn_w_gate, m_ffn_w_up, m_ffn_w_down, m_final_norm_g, v_norm_mix_g, v_norm_ffn_g, v_w_mod, v_b_mod, v_ab_w_in, v_ab_conv, v_ab_w_pool, v_ab_pool_scale, v_ab_w_out, v_cf_w_pw1, v_cf_b_pw1, v_cf_w_dw, v_cf_b_dw, v_cf_ln_g, v_cf_ln_b, v_cf_w_pw2, v_cf_b_pw2, v_ffn_w_gate, v_ffn_w_up, v_ffn_w_down, v_final_norm_g):
    w = dict(norm_mix_g=norm_mix_g, norm_ffn_g=norm_ffn_g, w_mod=w_mod, b_mod=b_mod, ab_w_in=ab_w_in,
             ab_conv=ab_conv, ab_w_pool=ab_w_pool, ab_pool_scale=ab_pool_scale, ab_w_out=ab_w_out,
             cf_w_pw1=cf_w_pw1, cf_b_pw1=cf_b_pw1, cf_w_dw=cf_w_dw, cf_b_dw=cf_b_dw, cf_ln_g=cf_ln_g,
             cf_ln_b=cf_ln_b, cf_w_pw2=cf_w_pw2, cf_b_pw2=cf_b_pw2, ffn_w_gate=ffn_w_gate, ffn_w_up=ffn_w_up,
             ffn_w_down=ffn_w_down, final_norm_g=final_norm_g)
    mom = dict(norm_mix_g=m_norm_mix_g, norm_ffn_g=m_norm_ffn_g, w_mod=m_w_mod, b_mod=m_b_mod, ab_w_in=m_ab_w_in,
               ab_conv=m_ab_conv, ab_w_pool=m_ab_w_pool, ab_pool_scale=m_ab_pool_scale, ab_w_out=m_ab_w_out,
               cf_w_pw1=m_cf_w_pw1, cf_b_pw1=m_cf_b_pw1, cf_w_dw=m_cf_w_dw, cf_b_dw=m_cf_b_dw, cf_ln_g=m_cf_ln_g,
               cf_ln_b=m_cf_ln_b, cf_w_pw2=m_cf_w_pw2, cf_b_pw2=m_cf_b_pw2, ffn_w_gate=m_ffn_w_gate,
               ffn_w_up=m_ffn_w_up, ffn_w_down=m_ffn_w_down, final_norm_g=m_final_norm_g)
    var = dict(norm_mix_g=v_norm_mix_g, norm_ffn_g=v_norm_ffn_g, w_mod=v_w_mod, b_mod=v_b_mod, ab_w_in=v_ab_w_in,
               ab_conv=v_ab_conv, ab_w_pool=v_ab_w_pool, ab_pool_scale=v_ab_pool_scale, ab_w_out=v_ab_w_out,
               cf_w_pw1=v_cf_w_pw1, cf_b_pw1=v_cf_b_pw1, cf_w_dw=v_cf_w_dw, cf_b_dw=v_cf_b_dw, cf_ln_g=v_cf_ln_g,
               cf_ln_b=v_cf_ln_b, cf_w_pw2=v_cf_w_pw2, cf_b_pw2=v_cf_b_pw2, ffn_w_gate=v_ffn_w_gate,
               ffn_w_up=v_ffn_w_up, ffn_w_down=v_ffn_w_down, final_norm_g=v_final_norm_g)
    px, py, pc = _place()
    chip = 2 * px + py
    dev = 2 * chip + pc
    depth, d, mod_cols = w_mod.shape
    x = x[0]
    target = loss_target[0]

    small_in = _Packer()
    small_in.add("c", c)
    for name in SMALL_COLS:
        small_in.add(name, w[name])
    gathered = _allgather8(small_in.pack(), with_sum=False).reshape(N_DEV, -1, LANES)
    per_dev = [small_in.unpack(gathered[k]) for k in range(N_DEV)]
    c_all = jnp.concatenate([pd["c"] for pd in per_dev], axis=0)
    params = {name: jnp.concatenate([per_dev[2 * k][name] for k in range(N_CHIPS)], axis=-1)
              for name in SMALL_COLS}
    for name in SMALL_REPL:
        params[name] = w[name]

    mod_part = _mod_fwd(c_all, w_mod, _my_cols(b_mod, chip))
    mod_all = _allgather8(mod_part.reshape(-1, LANES), with_sum=False)
    mod_all = mod_all.reshape(N_CHIPS, 2, depth, N_DEV, mod_cols)[:, 0]
    mod_all = jnp.moveaxis(mod_all, 0, 2).reshape(depth, N_DEV, N_CHIPS * mod_cols)
    mods = lax.dynamic_index_in_dim(mod_all, dev, axis=1, keepdims=False).reshape(depth, 6, d)

    big = BIG_COLS + BIG_ROWS
    full = _chip_exchange([w[name].astype(BF16) for name in big], scatter=False)
    for name, g in zip(big, full):
        params[name] = _chips_to_cols(g) if name in BIG_COLS else _chips_to_rows(g)

    grad_x, grads, dmods = _local_step(x, target, mods, params)

    small_out = _Packer()
    small_out.add("dmods", dmods)
    for name in ("loss",) + SMALL_REPL + SMALL_COLS:
        small_out.add(name, grads[name])
    parts_all, parts_sum = _allgather8(small_out.pack(), with_sum=True)
    small_sum = small_out.unpack(parts_sum)
    loss = small_sum["loss"][0]
    dmods_all = jnp.stack([small_out.unpack(pa)["dmods"] for pa in parts_all.reshape(N_DEV, -1, LANES)], axis=1)
    dmods_all = dmods_all.reshape(depth, N_DEV, 6 * d)

    g_final = {}
    g_final["w_mod"] = [_mod_bwd(c_all.T, _my_cols(dmods_all, chip))]
    g_final["b_mod"] = [small_sum["dmods"].reshape(depth, 6 * d)]
    for name in SMALL_REPL:
        g_final[name] = [small_sum[name]]
    for name in SMALL_COLS:
        g_final[name] = [_my_cols(small_sum[name], chip)]

    chunks = [(_cols_to_chips(grads[name]) if name in BIG_COLS else _rows_to_chips(grads[name])).astype(BF16)
              for name in big]
    received = _chip_exchange(chunks, scatter=True)
    mine = [_sum_parts(r) for r in received]
    theirs = _core_swap(mine)
    for name, a, b in zip(big, mine, theirs):
        g_final[name] = [a, b]

    out_g, out_d, out_m, out_v = [], [], [], []
    for name in WEIGHTS:
        g, dlt, mn, vn = _adamw(w[name], g_final[name], mom[name], var[name])
        out_g.append(g)
        out_d.append(dlt)
        out_m.append(mn)
        out_v.append(vn)
    return (loss, grad_x[None], *out_g, *out_d, *out_m, *out_v)
```

```python
import functools

import jax
import jax.numpy as jnp
from jax import lax
from jax.experimental import pallas as pl
from jax.experimental.pallas import tpu as pltpu

F32 = jnp.float32
BF16 = jnp.bfloat16
RMS_EPS = 1e-6
LN_EPS = 1e-5
ADAM_LR = 0.001
ADAM_B1 = 0.9
ADAM_B2 = 0.999
ADAM_EPS = 1e-08
ADAM_WD = 0.01
ADAM_STEP = 10
POOL_WINDOWS = (2, 4, 8, 16)
CONF_KERNEL = 31
N_CHIPS = 4
N_DEV = 8
HALO = 16
LANES = 1024
VMEM_LIMIT = 56 * 1024 * 1024
EW_BLOCK_ELEMS = 256 * 1024
MESH = pl.DeviceIdType.MESH
HIGHEST = lax.Precision.HIGHEST

_pcall = pl.pallas_call


def _dot(a, b):
    return jnp.dot(a, b, preferred_element_type=F32)


def _dot_tn(a, b):
    return lax.dot_general(a, b, (((0,), (0,)), ((), ())), preferred_element_type=F32)


def _dot_nt(a, b):
    return lax.dot_general(a, b, (((1,), (1,)), ((), ())), preferred_element_type=F32)


def _colsum(v):
    return jnp.sum(v, axis=0, keepdims=True)


def _sigmoid(v):
    return 1.0 / (1.0 + jnp.exp(-v))


def _rows(tm, c):
    return pl.BlockSpec((tm, c), lambda i: (i, 0))


def _full(shape):
    nd = len(shape)
    return pl.BlockSpec(shape, lambda i: (0,) * nd)


_VM = pl.BlockSpec(memory_space=pltpu.VMEM)
_ANY = pl.BlockSpec(memory_space=pl.ANY)


def _halo_specs(tm, c, t_total):
    r = tm // HALO
    last = t_total // HALO - 1
    prev = pl.BlockSpec((HALO, c), lambda i: (jnp.maximum(i * r - 1, 0), 0))
    nxt = pl.BlockSpec((HALO, c), lambda i: (jnp.minimum((i + 1) * r, last), 0))
    return prev, _rows(tm, c), nxt


def _seq_params():
    return pltpu.CompilerParams(dimension_semantics=("arbitrary",), vmem_limit_bytes=VMEM_LIMIT)


def _place():
    return lax.axis_index("x"), lax.axis_index("y"), lax.axis_index("c")


def _exch_copies(ins, outs, sems, scatter, with_recvs=True):
    send_sems, recv_sems, local_sems = sems
    x, y, c = _place()
    me = 2 * x + y
    peers = [(1 - x, y), (x, 1 - y), (1 - x, 1 - y)]
    local, sends, recvs = [], [], []
    for j in range(len(ins)):
        local.append(pltpu.make_async_copy(ins[j].at[me] if scatter else ins[j], outs[j].at[me], local_sems.at[j]))
        for k, (px, py) in enumerate(peers):
            sends.append(pltpu.make_async_remote_copy(
                src_ref=ins[j].at[2 * px + py] if scatter else ins[j], dst_ref=outs[j].at[me],
                send_sem=send_sems.at[3 * j + k], recv_sem=recv_sems.at[3 * j + k],
                device_id=(px, py, c), device_id_type=MESH))
            if with_recvs:
                recvs.append(pltpu.make_async_remote_copy(
                    src_ref=ins[j].at[me] if scatter else ins[j], dst_ref=outs[j].at[2 * px + py],
                    send_sem=send_sems.at[3 * j + k], recv_sem=recv_sems.at[3 * j + k],
                    device_id=(px, py, c), device_id_type=MESH))
    return local, sends, recvs


def _exch_start(ins, outs, sems, scatter):
    local, sends, _ = _exch_copies(ins, outs, sems, scatter, with_recvs=False)
    for cp in local + sends:
        cp.start()


def _exch_wait(ins, outs, sems, scatter):
    local, sends, recvs = _exch_copies(ins, outs, sems, scatter)
    for cp in recvs:
        cp.wait_recv()
    for cp in sends:
        cp.wait_send()
    for cp in local:
        cp.wait()


def _exch_out_shapes(arrs, scatter):
    if scatter:
        return [jax.ShapeDtypeStruct(a.shape, a.dtype) for a in arrs]
    return [jax.ShapeDtypeStruct((N_CHIPS,) + a.shape, a.dtype) for a in arrs]


def _exch_sems(n):
    return [pltpu.SemaphoreType.DMA((3 * n,)), pltpu.SemaphoreType.DMA((3 * n,)), pltpu.SemaphoreType.DMA((n,))]


def _call(body, *, name, nsteps, in_specs, out_specs, out_shape, args, scratch_shapes=(), exch=None):
    if exch is None:
        outs = _pcall(body, name=name, grid=(nsteps,), in_specs=list(in_specs), out_specs=list(out_specs),
                      out_shape=list(out_shape), scratch_shapes=list(scratch_shapes),
                      compiler_params=_seq_params())(*args)
        return list(outs), []
    arrs, scatter = exch
    n, ni, no, ns = len(arrs), len(in_specs), len(out_specs), len(scratch_shapes)

    def hosted(*refs):
        xin = refs[ni:ni + n]
        xout = refs[ni + n + no:ni + 2 * n + no]
        scr = refs[ni + 2 * n + no:]

        @pl.when(pl.program_id(0) == 0)
        def _():
            _exch_start(xin, xout, scr[ns:], scatter)

        body(*refs[:ni], *refs[ni + n:ni + n + no], *scr[:ns])

        @pl.when(pl.program_id(0) == nsteps - 1)
        def _():
            _exch_wait(xin, xout, scr[ns:], scatter)

    outs = _pcall(hosted, name=name + ("_scatter" if scatter else "_gather"), grid=(nsteps,),
                  in_specs=[*in_specs, *[_ANY] * n], out_specs=[*out_specs, *[_ANY] * n],
                  out_shape=[*out_shape, *_exch_out_shapes(arrs, scatter)],
                  scratch_shapes=[*scratch_shapes, *_exch_sems(n)],
                  compiler_params=_seq_params())(*args, *arrs)
    return list(outs[:no]), list(outs[no:])


def _rms(x):
    r = lax.rsqrt(jnp.mean(x * x, axis=-1, keepdims=True) + RMS_EPS)
    return x * r, r


def _norm_mod(x, g, sh, sc):
    xhat, _ = _rms(x)
    return xhat * g * (1.0 + sc) + sh


def _norm_mod_bwd(dh, x, g, sc):
    xhat, r = _rms(x)
    n = xhat * g
    dsh = _colsum(dh)
    dsc = _colsum(dh * n)
    dn = dh * (1.0 + sc)
    dg = _colsum(dn * xhat)
    dxn = dn * g
    dx = r * (dxn - xhat * jnp.mean(dxn * xhat, axis=-1, keepdims=True))
    return dx, dsh, dsc, dg


def _fill_ext(ext_ref, prev, cur, nxt, i, nsteps, tm):
    ext_ref[0:HALO, :] = jnp.where(i > 0, prev, 0.0)
    ext_ref[HALO:HALO + tm, :] = cur
    ext_ref[HALO + tm:HALO + tm + HALO, :] = jnp.where(i < nsteps - 1, nxt, 0.0)


def _window_count(t, wdw, t_total):
    left = wdw // 2
    right = wdw - 1 - left
    cnt = jnp.minimum(t + right, t_total - 1) - jnp.maximum(t - left, 0) + 1
    return jnp.maximum(cnt, 1).astype(F32)


def _in_proj(x, vec, w, bias, tm, exch=None):
    t_total, d = x.shape
    n = w.shape[1]
    has_bias = bias is not None

    def body(*refs):
        if has_bias:
            x_ref, vec_ref, w_ref, b_ref, u_ref = refs
        else:
            x_ref, vec_ref, w_ref, u_ref = refs
        h = _norm_mod(x_ref[...], vec_ref[0:1, :], vec_ref[1:2, :], vec_ref[2:3, :])
        u = _dot(h.astype(BF16), w_ref[...])
        if has_bias:
            u = u + b_ref[...]
        u_ref[...] = u.astype(BF16)

    in_specs = [_rows(tm, d), _full(vec.shape), _VM]
    args = [x, vec, w]
    if has_bias:
        in_specs.append(_full(bias.shape))
        args.append(bias)
    return _call(
        body, name="in_proj_bias" if has_bias else "in_proj", nsteps=t_total // tm,
        in_specs=in_specs, out_specs=[_rows(tm, n)], out_shape=[jax.ShapeDtypeStruct((t_total, n), BF16)],
        args=args, exch=exch)


def _ab_core(up_ref, uc_ref, un_ref, conv_ref, wpool_ref, q_ext, p_ext, i, nsteps, tm, t_total):
    da = uc_ref.shape[1] // 4

    def cols(ref, k):
        return ref[:, k * da:(k + 1) * da].astype(F32)

    _fill_ext(q_ext, cols(up_ref, 1) * cols(up_ref, 2), cols(uc_ref, 1) * cols(uc_ref, 2),
              cols(un_ref, 1) * cols(un_ref, 2), i, nsteps, tm)
    _fill_ext(p_ext, cols(up_ref, 3), cols(uc_ref, 3), cols(un_ref, 3), i, nsteps, tm)
    bg = cols(uc_ref, 0)
    cq = (conv_ref[0:1, :] * q_ext[HALO - 1:HALO - 1 + tm, :] + conv_ref[1:2, :] * q_ext[HALO:HALO + tm, :]
          + conv_ref[2:3, :] * q_ext[HALO + 1:HALO + 1 + tm, :])
    t = i * tm + lax.broadcasted_iota(jnp.int32, (tm, 1), 0)
    gw = da // len(POOL_WINDOWS)
    pooled, ybpre = [], []
    for g, wdw in enumerate(POOL_WINDOWS):
        left = wdw // 2
        right = wdw - 1 - left
        lo, hi = g * gw, (g + 1) * gw
        s = p_ext[HALO - left:HALO - left + tm, lo:hi]
        for o in range(-left + 1, right + 1):
            s = s + p_ext[HALO + o:HALO + o + tm, lo:hi]
        pg = s / _window_count(t, wdw, t_total) - p_ext[HALO:HALO + tm, lo:hi]
        pooled.append(pg.astype(BF16))
        ybpre.append(_dot(pooled[-1], wpool_ref[g]))
    return bg, cq, pooled, jnp.concatenate(ybpre, axis=1)


def _ab_fwd(u, x, vec, conv, wpool, scale, wout, tm, exch=None):
    t_total, d = x.shape
    nu = u.shape[1]
    da = nu // 4
    nsteps = t_total // tm

    def body(up_ref, uc_ref, un_ref, x_ref, vec_ref, conv_ref, wpool_ref, scale_ref, wout_ref,
             y_ref, x2_ref, q_ext, p_ext):
        i = pl.program_id(0)
        bg, cq, _, ybpre = _ab_core(up_ref, uc_ref, un_ref, conv_ref, wpool_ref, q_ext, p_ext,
                                    i, nsteps, tm, t_total)
        cat = jnp.concatenate([bg * cq, ybpre * scale_ref[...]], axis=1).astype(BF16)
        y = _dot(cat, wout_ref[...])
        y_ref[...] = y.astype(BF16)
        x2_ref[...] = x_ref[...] + vec_ref[0:1, :] * y

    return _call(
        body, name="ab_fwd", nsteps=nsteps,
        in_specs=[*_halo_specs(tm, nu, t_total), _rows(tm, d), _full(vec.shape), _full(conv.shape),
                  _full(wpool.shape), _full(scale.shape), _VM],
        out_specs=[_rows(tm, d), _rows(tm, d)],
        out_shape=[jax.ShapeDtypeStruct((t_total, d), BF16), jax.ShapeDtypeStruct((t_total, d), F32)],
        scratch_shapes=[pltpu.VMEM((tm + 2 * HALO, da), F32), pltpu.VMEM((tm + 2 * HALO, da), F32)],
        args=(u, u, u, x, vec, conv, wpool, scale, wout), exch=exch)


def _glu_ext(up_ref, uc_ref, un_ref, z_ext, i, nsteps, tm):
    dz = uc_ref.shape[1] // 2

    def glu(ref):
        return ref[:, 0:dz].astype(F32) * _sigmoid(ref[:, dz:2 * dz].astype(F32))

    _fill_ext(z_ext, glu(up_ref), glu(uc_ref), glu(un_ref), i, nsteps, tm)


def _layer_norm_stats(zc):
    mu = jnp.mean(zc, axis=-1, keepdims=True)
    dlt = zc - mu
    rstd = lax.rsqrt(jnp.mean(dlt * dlt, axis=-1, keepdims=True) + LN_EPS)
    return dlt * rstd, rstd


def _cf_fwd(u, x, vec, wdw, wpw2, tm, exch=None):
    t_total, d = x.shape
    nu = u.shape[1]
    nsteps = t_total // tm
    left = (CONF_KERNEL - 1) // 2

    def body(up_ref, uc_ref, un_ref, x_ref, vec_ref, wdw_ref, wpw2_ref, zc_ref, y_ref, x2_ref, z_ext):
        i = pl.program_id(0)
        _glu_ext(up_ref, uc_ref, un_ref, z_ext, i, nsteps, tm)
        zc = wdw_ref[0:1, :] * z_ext[HALO - left:HALO - left + tm, :]
        for k in range(1, CONF_KERNEL):
            zc = zc + wdw_ref[k:k + 1, :] * z_ext[HALO + k - left:HALO + k - left + tm, :]
        zc = zc + vec_ref[1:2, :]
        zc_ref[...] = zc.astype(BF16)
        zn, _ = _layer_norm_stats(zc)
        zl = zn * vec_ref[2:3, :] + vec_ref[3:4, :]
        zs = zl * _sigmoid(zl)
        y = _dot(zs.astype(BF16), wpw2_ref[...]) + vec_ref[4:5, :]
        y_ref[...] = y.astype(BF16)
        x2_ref[...] = x_ref[...] + vec_ref[0:1, :] * y

    return _call(
        body, name="cf_fwd", nsteps=nsteps,
        in_specs=[*_halo_specs(tm, nu, t_total), _rows(tm, d), _full(vec.shape), _full(wdw.shape), _VM],
        out_specs=[_rows(tm, d), _rows(tm, d), _rows(tm, d)],
        out_shape=[jax.ShapeDtypeStruct((t_total, d), BF16), jax.ShapeDtypeStruct((t_total, d), BF16),
                   jax.ShapeDtypeStruct((t_total, d), F32)],
        scratch_shapes=[pltpu.VMEM((tm + 2 * HALO, d), F32)],
        args=(u, u, u, x, vec, wdw, wpw2), exch=exch)


def _ffn_fwd(x2, vec, wg, wu, wd, tm, exch=None):
    t_total, d = x2.shape
    f = wg.shape[1]

    def body(x_ref, vec_ref, wg_ref, wu_ref, wd_ref, a_ref, b_ref, f_ref, x3_ref):
        xv = x_ref[...]
        h = _norm_mod(xv, vec_ref[0:1, :], vec_ref[1:2, :], vec_ref[2:3, :]).astype(BF16)
        a = _dot(h, wg_ref[...])
        b = _dot(h, wu_ref[...])
        a_ref[...] = a.astype(BF16)
        b_ref[...] = b.astype(BF16)
        s = (a * _sigmoid(a) * b).astype(BF16)
        y = _dot(s, wd_ref[...])
        f_ref[...] = y.astype(BF16)
        x3_ref[...] = xv + vec_ref[3:4, :] * y

    return _call(
        body, name="ffn_fwd", nsteps=t_total // tm,
        in_specs=[_rows(tm, d), _full(vec.shape), _VM, _VM, _VM],
        out_specs=[_rows(tm, f), _rows(tm, f), _rows(tm, d), _rows(tm, d)],
        out_shape=[jax.ShapeDtypeStruct((t_total, f), BF16), jax.ShapeDtypeStruct((t_total, f), BF16),
                   jax.ShapeDtypeStruct((t_total, d), BF16), jax.ShapeDtypeStruct((t_total, d), F32)],
        args=(x2, vec, wg, wu, wd), exch=exch)


def _final_fwd_bwd(x, target, vec, tm):
    t_total, d = x.shape

    def body(x_ref, t_ref, vec_ref, dx_ref, acc_ref):
        @pl.when(pl.program_id(0) == 0)
        def _():
            acc_ref[...] = jnp.zeros_like(acc_ref)

        g = vec_ref[0:1, :]
        xhat, r = _rms(x_ref[...])
        e = xhat * g - t_ref[...]
        acc_ref[1:2, :] += jnp.zeros((1, d), F32) + 0.5 * jnp.sum(jnp.mean(e * e, axis=-1, keepdims=True))
        dout = e * (1.0 / d)
        acc_ref[0:1, :] += _colsum(dout * xhat)
        dxn = dout * g
        dx_ref[...] = r * (dxn - xhat * jnp.mean(dxn * xhat, axis=-1, keepdims=True))

    return _call(
        body, name="final_fwd_bwd", nsteps=t_total // tm,
        in_specs=[_rows(tm, d), _rows(tm, d), _full(vec.shape)],
        out_specs=[_rows(tm, d), _VM],
        out_shape=[jax.ShapeDtypeStruct((t_total, d), F32), jax.ShapeDtypeStruct((8, d), F32)],
        args=(x, target, vec))


def _zero_at_start(*refs):
    @pl.when(pl.program_id(0) == 0)
    def _():
        for ref in refs:
            ref[...] = jnp.zeros_like(ref)


def _ffn_bwd_down(dx3, fout, a, b, vec, wd, tm, exch=None):
    t_total, d = dx3.shape
    f = a.shape[1]

    def body(dx_ref, f_ref, a_ref, b_ref, vec_ref, wd_ref, da_ref, db_ref, dwd_ref, acc_ref):
        _zero_at_start(dwd_ref, acc_ref)
        dx = dx_ref[...]
        acc_ref[0:1, :] += _colsum(dx * f_ref[...].astype(F32))
        dy = (dx * vec_ref[0:1, :]).astype(BF16)
        av = a_ref[...].astype(F32)
        bv = b_ref[...].astype(F32)
        sg = _sigmoid(av)
        silu = av * sg
        dwd_ref[...] += _dot_tn((silu * bv).astype(BF16), dy)
        ds = _dot_nt(dy, wd_ref[...])
        da_ref[...] = (ds * bv * (sg * (1.0 + av * (1.0 - sg)))).astype(BF16)
        db_ref[...] = (ds * silu).astype(BF16)

    return _call(
        body, name="ffn_bwd_down", nsteps=t_total // tm,
        in_specs=[_rows(tm, d), _rows(tm, d), _rows(tm, f), _rows(tm, f), _full(vec.shape), _VM],
        out_specs=[_rows(tm, f), _rows(tm, f), _VM, _VM],
        out_shape=[jax.ShapeDtypeStruct((t_total, f), BF16), jax.ShapeDtypeStruct((t_total, f), BF16),
                   jax.ShapeDtypeStruct(wd.shape, F32), jax.ShapeDtypeStruct((8, d), F32)],
        args=(dx3, fout, a, b, vec, wd), exch=exch)


def _ffn_bwd_up(da, db, x2, dx3, vec, wg, wu, tm, exch=None):
    t_total, d = x2.shape
    f = da.shape[1]

    def body(da_ref, db_ref, x_ref, dx_ref, vec_ref, wg_ref, wu_ref, dx2_ref, dwg_ref, dwu_ref, acc_ref):
        _zero_at_start(dwg_ref, dwu_ref, acc_ref)
        xv = x_ref[...]
        g, sh, sc = vec_ref[0:1, :], vec_ref[1:2, :], vec_ref[2:3, :]
        h = _norm_mod(xv, g, sh, sc).astype(BF16)
        dav = da_ref[...]
        dbv = db_ref[...]
        dwg_ref[...] += _dot_tn(h, dav)
        dwu_ref[...] += _dot_tn(h, dbv)
        dh = _dot_nt(dav, wg_ref[...]) + _dot_nt(dbv, wu_ref[...])
        dxn, dsh, dsc, dg = _norm_mod_bwd(dh, xv, g, sc)
        acc_ref[0:1, :] += dsh
        acc_ref[1:2, :] += dsc
        acc_ref[2:3, :] += dg
        dx2_ref[...] = dx_ref[...] + dxn

    return _call(
        body, name="ffn_bwd_up", nsteps=t_total // tm,
        in_specs=[_rows(tm, f), _rows(tm, f), _rows(tm, d), _rows(tm, d), _full(vec.shape), _VM, _VM],
        out_specs=[_rows(tm, d), _VM, _VM, _VM],
        out_shape=[jax.ShapeDtypeStruct((t_total, d), F32), jax.ShapeDtypeStruct(wg.shape, F32),
                   jax.ShapeDtypeStruct(wu.shape, F32), jax.ShapeDtypeStruct((8, d), F32)],
        args=(da, db, x2, dx3, vec, wg, wu), exch=exch)


def _ab_bwd_out(dx, y, u, vec, conv, wpool, scale, wout, tm, exch=None):
    t_total, d = dx.shape
    nu = u.shape[1]
    da = nu // 4
    gw = da // len(POOL_WINDOWS)
    nsteps = t_total // tm

    def body(dx_ref, y_ref, up_ref, uc_ref, un_ref, vec_ref, conv_ref, wpool_ref, scale_ref, wout_ref,
             dpre_ref, dwout_ref, dwpool_ref, acc_ref, q_ext, p_ext):
        _zero_at_start(dwout_ref, dwpool_ref, acc_ref)
        i = pl.program_id(0)
        dxv = dx_ref[...]
        acc_ref[0:1, :] += _colsum(dxv * y_ref[...].astype(F32))
        dy = (dxv * vec_ref[0:1, :]).astype(BF16)
        bg, cq, pooled, ybpre = _ab_core(up_ref, uc_ref, un_ref, conv_ref, wpool_ref, q_ext, p_ext,
                                         i, nsteps, tm, t_total)
        cat = jnp.concatenate([bg * cq, ybpre * scale_ref[...]], axis=1).astype(BF16)
        dwout_ref[...] += _dot_tn(cat, dy)
        dcat = _dot_nt(dy, wout_ref[...])
        dya = dcat[:, 0:da]
        dyb = dcat[:, da:2 * da]
        acc_ref[1:2, 0:da] += _colsum(dyb * ybpre)
        dybpre = (dyb * scale_ref[...]).astype(BF16)
        dpooled = []
        for g in range(len(POOL_WINDOWS)):
            dg = dybpre[:, g * gw:(g + 1) * gw]
            dwpool_ref[g] += _dot_tn(pooled[g], dg)
            dpooled.append(_dot_nt(dg, wpool_ref[g]))
        dpre_ref[...] = jnp.concatenate([dya * cq, dya * bg] + dpooled, axis=1).astype(BF16)

    return _call(
        body, name="ab_bwd_out", nsteps=nsteps,
        in_specs=[_rows(tm, d), _rows(tm, d), *_halo_specs(tm, nu, t_total), _full(vec.shape),
                  _full(conv.shape), _full(wpool.shape), _full(scale.shape), _VM],
        out_specs=[_rows(tm, 3 * da), _VM, _VM, _VM],
        out_shape=[jax.ShapeDtypeStruct((t_total, 3 * da), BF16), jax.ShapeDtypeStruct(wout.shape, F32),
                   jax.ShapeDtypeStruct(wpool.shape, F32), jax.ShapeDtypeStruct((8, d), F32)],
        scratch_shapes=[pltpu.VMEM((tm + 2 * HALO, da), F32), pltpu.VMEM((tm + 2 * HALO, da), F32)],
        args=(dx, y, u, u, u, vec, conv, wpool, scale, wout), exch=exch)


def _ab_bwd_in(dpre, u, x, dx, vec, conv, win, tm, exch=None):
    t_total, d = x.shape
    nu = u.shape[1]
    da = nu // 4
    gw = da // len(POOL_WINDOWS)
    nsteps = t_total // tm

    def body(dp_ref, dc_ref, dn_ref, up_ref, uc_ref, un_ref, x_ref, dx_ref, vec_ref, conv_ref, win_ref,
             dxin_ref, dwin_ref, dconv_ref, acc_ref, dcq_ext, q_ext, dpl_ext):
        _zero_at_start(dwin_ref, dconv_ref, acc_ref)
        i = pl.program_id(0)

        def ucols(ref, k):
            return ref[:, k * da:(k + 1) * da].astype(F32)

        def dcols(ref, k):
            return ref[:, k * da:(k + 1) * da].astype(F32)

        _fill_ext(dcq_ext, dcols(dp_ref, 1), dcols(dc_ref, 1), dcols(dn_ref, 1), i, nsteps, tm)
        _fill_ext(q_ext, ucols(up_ref, 1) * ucols(up_ref, 2), ucols(uc_ref, 1) * ucols(uc_ref, 2),
                  ucols(un_ref, 1) * ucols(un_ref, 2), i, nsteps, tm)
        _fill_ext(dpl_ext, dcols(dp_ref, 2), dcols(dc_ref, 2), dcols(dn_ref, 2), i, nsteps, tm)
        dq = (conv_ref[0:1, :] * dcq_ext[HALO + 1:HALO + 1 + tm, :] + conv_ref[1:2, :] * dcq_ext[HALO:HALO + tm, :]
              + conv_ref[2:3, :] * dcq_ext[HALO - 1:HALO - 1 + tm, :])
        dcq = dcq_ext[HALO:HALO + tm, :]
        for k in range(3):
            dconv_ref[k:k + 1, :] += _colsum(dcq * q_ext[HALO + k - 1:HALO + k - 1 + tm, :])
        dcg = dq * ucols(uc_ref, 2)
        dv = dq * ucols(uc_ref, 1)
        t_ext = i * tm - HALO + lax.broadcasted_iota(jnp.int32, (tm + 2 * HALO, 1), 0)
        dps = []
        for g, wdw in enumerate(POOL_WINDOWS):
            left = wdw // 2
            right = wdw - 1 - left
            lo, hi = g * gw, (g + 1) * gw
            dpg = dpl_ext[HALO:HALO + tm, lo:hi]
            dpl_ext[:, lo:hi] = dpl_ext[:, lo:hi] / _window_count(t_ext, wdw, t_total)
            s = dpl_ext[HALO - right:HALO - right + tm, lo:hi]
            for o in range(-right + 1, left + 1):
                s = s + dpl_ext[HALO + o:HALO + o + tm, lo:hi]
            dps.append(s - dpg)
        du = jnp.concatenate([dcols(dc_ref, 0), dcg, dv] + dps, axis=1).astype(BF16)
        xv = x_ref[...]
        g, sh, sc = vec_ref[0:1, :], vec_ref[1:2, :], vec_ref[2:3, :]
        h = _norm_mod(xv, g, sh, sc).astype(BF16)
        dwin_ref[...] += _dot_tn(h, du)
        dh = _dot_nt(du, win_ref[...])
        dxn, dsh, dsc, dg = _norm_mod_bwd(dh, xv, g, sc)
        acc_ref[0:1, :] += dsh
        acc_ref[1:2, :] += dsc
        acc_ref[2:3, :] += dg
        dxin_ref[...] = dx_ref[...] + dxn

    ext = pltpu.VMEM((tm + 2 * HALO, da), F32)
    return _call(
        body, name="ab_bwd_in", nsteps=nsteps,
        in_specs=[*_halo_specs(tm, 3 * da, t_total), *_halo_specs(tm, nu, t_total), _rows(tm, d), _rows(tm, d),
                  _full(vec.shape), _full(conv.shape), _VM],
        out_specs=[_rows(tm, d), _VM, _VM, _VM],
        out_shape=[jax.ShapeDtypeStruct((t_total, d), F32), jax.ShapeDtypeStruct(win.shape, F32),
                   jax.ShapeDtypeStruct((8, da), F32), jax.ShapeDtypeStruct((8, d), F32)],
        scratch_shapes=[ext, ext, ext],
        args=(dpre, dpre, dpre, u, u, u, x, dx, vec, conv, win), exch=exch)


def _cf_bwd_out(dx, y, zc, vec, wpw2, tm, exch=None):
    t_total, d = dx.shape

    def body(dx_ref, y_ref, zc_ref, vec_ref, w_ref, dzc_ref, dw_ref, acc_ref):
        _zero_at_start(dw_ref, acc_ref)
        dxv = dx_ref[...]
        acc_ref[0:1, :] += _colsum(dxv * y_ref[...].astype(F32))
        dyf = dxv * vec_ref[0:1, :]
        acc_ref[1:2, :] += _colsum(dyf)
        dy = dyf.astype(BF16)
        zn, rstd = _layer_norm_stats(zc_ref[...].astype(F32))
        lng = vec_ref[1:2, :]
        zl = zn * lng + vec_ref[2:3, :]
        sg = _sigmoid(zl)
        dw_ref[...] += _dot_tn((zl * sg).astype(BF16), dy)
        dzl = _dot_nt(dy, w_ref[...]) * (sg * (1.0 + zl * (1.0 - sg)))
        acc_ref[2:3, :] += _colsum(dzl * zn)
        acc_ref[3:4, :] += _colsum(dzl)
        dzn = dzl * lng
        dzc = rstd * (dzn - jnp.mean(dzn, axis=-1, keepdims=True)
                      - zn * jnp.mean(dzn * zn, axis=-1, keepdims=True))
        acc_ref[4:5, :] += _colsum(dzc)
        dzc_ref[...] = dzc.astype(BF16)

    return _call(
        body, name="cf_bwd_out", nsteps=t_total // tm,
        in_specs=[_rows(tm, d), _rows(tm, d), _rows(tm, d), _full(vec.shape), _VM],
        out_specs=[_rows(tm, d), _VM, _VM],
        out_shape=[jax.ShapeDtypeStruct((t_total, d), BF16), jax.ShapeDtypeStruct(wpw2.shape, F32),
                   jax.ShapeDtypeStruct((8, d), F32)],
        args=(dx, y, zc, vec, wpw2), exch=exch)


def _cf_bwd_in(dzc, u, x, dx, vec, wdw, wpw1, tm, exch=None):
    t_total, d = x.shape
    nu = u.shape[1]
    nsteps = t_total // tm
    left = (CONF_KERNEL - 1) // 2

    def body(dp_ref, dc_ref, dn_ref, up_ref, uc_ref, un_ref, x_ref, dx_ref, vec_ref, wdw_ref, w_ref,
             dxin_ref, dw_ref, dwdw_ref, db1_ref, acc_ref, dzc_ext, z_ext):
        _zero_at_start(dw_ref, dwdw_ref, db1_ref, acc_ref)
        i = pl.program_id(0)
        _fill_ext(dzc_ext, dp_ref[...].astype(F32), dc_ref[...].astype(F32), dn_ref[...].astype(F32),
                  i, nsteps, tm)
        _glu_ext(up_ref, uc_ref, un_ref, z_ext, i, nsteps, tm)
        dzc = dzc_ext[HALO:HALO + tm, :]
        dz = wdw_ref[0:1, :] * dzc_ext[HALO + left:HALO + left + tm, :]
        dwdw_ref[0:1, :] += _colsum(dzc * z_ext[HALO - left:HALO - left + tm, :])
        for k in range(1, CONF_KERNEL):
            dz = dz + wdw_ref[k:k + 1, :] * dzc_ext[HALO + left - k:HALO + left - k + tm, :]
            dwdw_ref[k:k + 1, :] += _colsum(dzc * z_ext[HALO + k - left:HALO + k - left + tm, :])
        av = uc_ref[:, 0:d].astype(F32)
        sg = _sigmoid(uc_ref[:, d:2 * d].astype(F32))
        duf = jnp.concatenate([dz * sg, dz * av * sg * (1.0 - sg)], axis=1)
        db1_ref[0:1, :] += _colsum(duf)
        du = duf.astype(BF16)
        xv = x_ref[...]
        g, sh, sc = vec_ref[0:1, :], vec_ref[1:2, :], vec_ref[2:3, :]
        h = _norm_mod(xv, g, sh, sc).astype(BF16)
        dw_ref[...] += _dot_tn(h, du)
        dh = _dot_nt(du, w_ref[...])
        dxn, dsh, dsc, dg = _norm_mod_bwd(dh, xv, g, sc)
        acc_ref[0:1, :] += dsh
        acc_ref[1:2, :] += dsc
        acc_ref[2:3, :] += dg
        dxin_ref[...] = dx_ref[...] + dxn

    ext = pltpu.VMEM((tm + 2 * HALO, d), F32)
    return _call(
        body, name="cf_bwd_in", nsteps=nsteps,
        in_specs=[*_halo_specs(tm, d, t_total), *_halo_specs(tm, nu, t_total), _rows(tm, d), _rows(tm, d),
                  _full(vec.shape), _full(wdw.shape), _VM],
        out_specs=[_rows(tm, d), _VM, _VM, _VM, _VM],
        out_shape=[jax.ShapeDtypeStruct((t_total, d), F32), jax.ShapeDtypeStruct(wpw1.shape, F32),
                   jax.ShapeDtypeStruct((32, d), F32), jax.ShapeDtypeStruct((8, nu), F32),
                   jax.ShapeDtypeStruct((8, d), F32)],
        scratch_shapes=[ext, ext],
        args=(dzc, dzc, dzc, u, u, u, x, dx, vec, wdw, wpw1), exch=exch)


def _mod_fwd(c_all, w_mod, b_cols):
    nl, d, ncol = w_mod.shape
    nb = c_all.shape[0]

    def body(c_ref, w_ref, b_ref, o_ref):
        cv = c_ref[...]
        ca = cv * _sigmoid(cv)
        o_ref[0] = jnp.dot(ca, w_ref[0], preferred_element_type=F32, precision=HIGHEST) + b_ref[0]

    return _pcall(
        body, name="mod_fwd", grid=(nl,),
        in_specs=[_full(c_all.shape), pl.BlockSpec((1, d, ncol), lambda l: (l, 0, 0)),
                  pl.BlockSpec((1, 1, ncol), lambda l: (l, 0, 0))],
        out_specs=pl.BlockSpec((1, nb, ncol), lambda l: (l, 0, 0)),
        out_shape=jax.ShapeDtypeStruct((nl, nb, ncol), F32),
        compiler_params=_seq_params(),
    )(c_all, w_mod, b_cols.reshape(nl, 1, ncol))


def _mod_bwd(c_all_t, dmod_cols):
    d, nb = c_all_t.shape
    nl, _, ncol = dmod_cols.shape

    def body(c_ref, dm_ref, o_ref):
        cv = c_ref[...]
        ca = cv * _sigmoid(cv)
        o_ref[0] = jnp.dot(ca, dm_ref[0], preferred_element_type=F32, precision=HIGHEST)

    return _pcall(
        body, name="mod_bwd", grid=(nl,),
        in_specs=[_full(c_all_t.shape), pl.BlockSpec((1, nb, ncol), lambda l: (l, 0, 0))],
        out_specs=pl.BlockSpec((1, d, ncol), lambda l: (l, 0, 0)),
        out_shape=jax.ShapeDtypeStruct((nl, d, ncol), F32),
        compiler_params=_seq_params(),
    )(c_all_t, dmod_cols)


def _row_block(r, c):
    if r * c <= EW_BLOCK_ELEMS:
        return r
    best = None
    for br in range(8, r, 8):
        if r % br == 0 and br * c <= EW_BLOCK_ELEMS:
            best = br
    assert best is not None, (r, c)
    return best


def _as2d(a):
    return a.reshape(-1, a.shape[-1])


def _sum_parts(parts):
    k = parts.shape[0]
    p3 = parts.reshape(k, -1, parts.shape[-1])
    _, r, c = p3.shape
    br = _row_block(r, c)

    def body(p_ref, o_ref):
        acc = p_ref[0].astype(F32)
        for j in range(1, k):
            acc = acc + p_ref[j].astype(F32)
        o_ref[...] = acc

    out = _pcall(
        body, name="sum_parts", grid=(r // br,),
        in_specs=[pl.BlockSpec((k, br, c), lambda i: (0, i, 0))],
        out_specs=pl.BlockSpec((br, c), lambda i: (i, 0)),
        out_shape=jax.ShapeDtypeStruct((r, c), F32),
        compiler_params=_seq_params(),
    )(p3)
    return out.reshape(parts.shape[1:])


def _adamw(w, gparts, m, v):
    shape = w.shape
    w2, m2, v2 = _as2d(w), _as2d(m), _as2d(v)
    g2 = [_as2d(g) for g in gparts]
    r, c = w2.shape
    br = _row_block(r, c)
    ng = len(g2)

    def body(*refs):
        w_ref, m_ref, v_ref = refs[0:3]
        g_refs = refs[3:3 + ng]
        go_ref, d_ref, mo_ref, vo_ref = refs[3 + ng:]
        g = g_refs[0][...]
        for gr in g_refs[1:]:
            g = g + gr[...]
        mn = ADAM_B1 * m_ref[...] + (1.0 - ADAM_B1) * g
        vn = ADAM_B2 * v_ref[...] + (1.0 - ADAM_B2) * (g * g)
        m_hat = mn / (1.0 - ADAM_B1 ** ADAM_STEP)
        v_hat = vn / (1.0 - ADAM_B2 ** ADAM_STEP)
        go_ref[...] = g
        d_ref[...] = -ADAM_LR * (m_hat / (jnp.sqrt(v_hat) + ADAM_EPS) + ADAM_WD * w_ref[...])
        mo_ref[...] = mn
        vo_ref[...] = vn

    spec = pl.BlockSpec((br, c), lambda i: (i, 0))
    outs = _pcall(
        body, name="adamw", grid=(r // br,),
        in_specs=[spec] * (3 + ng), out_specs=[spec] * 4,
        out_shape=[jax.ShapeDtypeStruct((r, c), F32)] * 4,
        compiler_params=_seq_params(),
    )(w2, m2, v2, *g2)
    return tuple(o.reshape(shape) for o in outs)


def _allgather8(block, with_sum):
    m_per, n = block.shape

    def body(x_ref, out_ref, *rest):
        if with_sum:
            sum_ref, send_sems, recv_sems, local_sem = rest
        else:
            send_sems, recv_sems, local_sem = rest
        x, y, c = _place()
        me, sibling = (x, y, c), (x, y, 1 - c)
        chips = [(1 - x, y), (x, 1 - y), (1 - x, 1 - y)]

        def rows(px, py, pc):
            return out_ref.at[pl.ds((4 * px + 2 * py + pc) * m_per, m_per), :]

        def copy(k, blk, to, src=None):
            return pltpu.make_async_remote_copy(
                src_ref=rows(*blk) if src is None else src, dst_ref=rows(*blk),
                send_sem=send_sems.at[k], recv_sem=recv_sems.at[k], device_id=to, device_id_type=MESH)

        mine = pltpu.make_async_copy(x_ref, rows(*me), local_sem)
        mine.start()
        first = [copy(0, me, sibling, src=x_ref)]
        first += [copy(1 + j, me, (*chip, c), src=x_ref) for j, chip in enumerate(chips)]
        for cp in first:
            cp.start()
        passed = [copy(4 + j, (*chip, c), sibling) for j, chip in enumerate(chips)]
        for j, chip in enumerate(chips):
            copy(1 + j, (*chip, c), me).wait_recv()
            passed[j].start()
        copy(0, sibling, me).wait_recv()
        for j, chip in enumerate(chips):
            copy(4 + j, (*chip, 1 - c), me).wait_recv()
        for cp in first + passed:
            cp.wait_send()
        mine.wait()
        if with_sum:
            acc = out_ref[0:m_per, :]
            for k in range(1, N_DEV):
                acc = acc + out_ref[k * m_per:(k + 1) * m_per, :]
            sum_ref[...] = acc

    out_shape = [jax.ShapeDtypeStruct((N_DEV * m_per, n), F32)]
    out_specs = [_VM]
    if with_sum:
        out_shape.append(jax.ShapeDtypeStruct((m_per, n), F32))
        out_specs.append(_VM)
    res = _pcall(
        body, name="allgather8_sum" if with_sum else "allgather8",
        in_specs=[_VM], out_specs=out_specs, out_shape=out_shape,
        scratch_shapes=[pltpu.SemaphoreType.DMA((7,)), pltpu.SemaphoreType.DMA((7,)), pltpu.SemaphoreType.DMA],
        compiler_params=pltpu.CompilerParams(vmem_limit_bytes=VMEM_LIMIT),
    )(block)
    return res if with_sum else res[0]


def _chip_exchange(arrs, scatter):
    n = len(arrs)

    def body(*refs):
        _exch_start(refs[:n], refs[n:2 * n], refs[2 * n:], scatter)
        _exch_wait(refs[:n], refs[n:2 * n], refs[2 * n:], scatter)

    return _pcall(
        body, name="chip_scatter" if scatter else "chip_gather",
        in_specs=[_ANY] * n, out_specs=[_ANY] * n, out_shape=_exch_out_shapes(arrs, scatter),
        scratch_shapes=_exch_sems(n),
    )(*arrs)


def _core_swap(arrs):
    n = len(arrs)

    def body(*refs):
        ins, outs = refs[:n], refs[n:2 * n]
        send_sems, recv_sems = refs[2 * n:]
        x, y, c = _place()
        cps = []
        for j in range(n):
            cp = pltpu.make_async_remote_copy(
                src_ref=ins[j], dst_ref=outs[j], send_sem=send_sems.at[j], recv_sem=recv_sems.at[j],
                device_id=(x, y, 1 - c), device_id_type=MESH)
            cp.start()
            cps.append(cp)
        for cp in cps:
            cp.wait()

    return _pcall(
        body, name="core_swap",
        in_specs=[_ANY] * n, out_specs=[_ANY] * n,
        out_shape=[jax.ShapeDtypeStruct(a.shape, a.dtype) for a in arrs],
        scratch_shapes=[pltpu.SemaphoreType.DMA((n,)), pltpu.SemaphoreType.DMA((n,))],
    )(*arrs)


def _cols_to_chips(w):
    *lead, a, nb = w.shape
    w = w.reshape(*lead, a, N_CHIPS, nb // N_CHIPS)
    return jnp.moveaxis(w, -2, 0)


def _chips_to_cols(g):
    g = jnp.moveaxis(g, 0, -2)
    *lead, a, k, b = g.shape
    return g.reshape(*lead, a, k * b)


def _my_cols(full, chip):
    w = full.shape[-1] // N_CHIPS
    return lax.dynamic_slice_in_dim(full, chip * w, w, axis=full.ndim - 1)


def _pad_rows(a, rows):
    return jnp.pad(a, ((0, rows - a.shape[0]), (0, 0)))


def _to_lanes(a):
    flat = a.reshape(-1)
    n = -(-flat.shape[0] // (8 * LANES)) * (8 * LANES)
    return jnp.pad(flat, (0, n - flat.shape[0])).reshape(-1, LANES)


class _Packer:
    def __init__(self):
        self.items = []
        self.rows = 0

    def add(self, name, a):
        lanes = _to_lanes(a)
        self.items.append((name, self.rows, a.shape, lanes))
        self.rows += lanes.shape[0]

    def pack(self):
        total = -(-self.rows // 8) * 8
        return _pad_rows(jnp.concatenate([it[3] for it in self.items], axis=0), total)

    def unpack(self, buf):
        out = {}
        for name, row, shape, lanes in self.items:
            size = 1
            for s in shape:
                size *= s
            out[name] = buf[row:row + lanes.shape[0]].reshape(-1)[:size].reshape(shape)
        return out


TM_SEQ = 512
TM_FFN = 256


LAYER_KEYS = ("in", "out", "gate", "up", "down")
COL_KEYS = ("in", "gate", "up")


def _layer_big_names(layer):
    i = layer // 2
    mix = (("ab_w_in", i), ("ab_w_out", i)) if layer % 2 == 0 else (("cf_w_pw1", i), ("cf_w_pw2", i))
    return dict(zip(LAYER_KEYS, mix + (("ffn_w_gate", layer), ("ffn_w_up", layer), ("ffn_w_down", layer))))


def _unpack_weight(key, g):
    return _chips_to_cols(g) if key in COL_KEYS else g.reshape(-1, g.shape[-1])


def _chunk_grad(key, dw):
    parts = _cols_to_chips(dw) if key in COL_KEYS else dw.reshape(N_CHIPS, -1, dw.shape[-1])
    return parts.astype(BF16)


def _local_step(x, target, mods, p, shards):
    t_total, d = x.shape
    depth = mods.shape[0]
    tm = min(TM_SEQ, t_total)
    tmf = min(TM_FFN, t_total)
    saved, weights = [], []
    xin = x
    lw = {k: _unpack_weight(k, g) for k, g in zip(
        LAYER_KEYS, _chip_exchange([shards[0][k] for k in LAYER_KEYS], scatter=False))}
    for layer in range(depth):
        i = layer // 2
        nxt = shards[layer + 1] if layer + 1 < depth else None
        got = {}

        def gather(*keys):
            return ([nxt[k] for k in keys], False) if nxt is not None else None

        def keep(keys, arrs):
            for k, g in zip(keys, arrs):
                got[k] = _unpack_weight(k, g)

        sh1, sc1, g1, sh2, sc2, g2 = (mods[layer, k:k + 1] for k in range(6))
        vec_in = jnp.concatenate([p["norm_mix_g"][layer:layer + 1], sh1, sc1], axis=0)
        bias = None if layer % 2 == 0 else p["cf_b_pw1"][i:i + 1]
        (u,), arrived = _in_proj(xin, vec_in, lw["in"], bias, tm, exch=gather("in"))
        keep(("in",), arrived)
        if layer % 2 == 0:
            (y, x2), arrived = _ab_fwd(u, xin, g1, p["ab_conv"][i], p["ab_w_pool"][i].astype(BF16),
                                       p["ab_pool_scale"][i:i + 1], lw["out"], tm, exch=gather("out", "gate"))
            zc = None
        else:
            vec_cf = jnp.concatenate([g1, p["cf_b_dw"][i:i + 1], p["cf_ln_g"][i:i + 1], p["cf_ln_b"][i:i + 1],
                                      p["cf_b_pw2"][i:i + 1]], axis=0)
            (zc, y, x2), arrived = _cf_fwd(u, xin, vec_cf, _pad_rows(p["cf_w_dw"][i], 32), lw["out"], tm,
                                           exch=gather("out", "gate"))
        keep(("out", "gate"), arrived)
        vec_ffn = jnp.concatenate([p["norm_ffn_g"][layer:layer + 1], sh2, sc2, g2], axis=0)
        (a, b, fout, x3), arrived = _ffn_fwd(x2, vec_ffn, lw["gate"], lw["up"], lw["down"], tmf,
                                             exch=gather("up", "down"))
        keep(("up", "down"), arrived)
        saved.append((xin, u, y, zc, x2, a, b, fout))
        weights.append(lw)
        lw = got
        xin = x3

    (dx, fin), _ = _final_fwd_bwd(xin, target, p["final_norm_g"].reshape(1, d), tm)
    grads = {"final_norm_g": fin[0], "loss": fin[1, 0:1]}
    per_layer = {k: [None] * depth for k in ("norm_mix_g", "norm_ffn_g")}
    half = {k: [None] * (depth // 2) for k in (
        "ab_conv", "ab_w_pool", "ab_pool_scale", "cf_b_pw1", "cf_w_dw", "cf_b_dw", "cf_ln_g", "cf_ln_b", "cf_b_pw2")}
    dmods = [None] * depth
    received = {}
    pending = None
    for layer in reversed(range(depth)):
        i = layer // 2
        lw = weights[layer]
        xin, u, y, zc, x2, a, b, fout = saved[layer]
        sh1, sc1, g1, sh2, sc2, g2 = (mods[layer, k:k + 1] for k in range(6))
        above = ([pending["out"], pending["in"]], True) if pending is not None else None
        (da, db, dwd, acc_d), arrived = _ffn_bwd_down(dx, fout, a, b, g2, lw["down"], tmf, exch=above)
        if pending is not None:
            received[(layer + 1, "out")], received[(layer + 1, "in")] = arrived
        vec_ffn = jnp.concatenate([p["norm_ffn_g"][layer:layer + 1], sh2, sc2], axis=0)
        (dx2, dwg, dwu, acc_u), arrived = _ffn_bwd_up(da, db, x2, dx, vec_ffn, lw["gate"], lw["up"], tmf,
                                                      exch=([_chunk_grad("down", dwd)], True))
        received[(layer, "down")] = arrived[0]
        per_layer["norm_ffn_g"][layer] = acc_u[2]
        vec_in = jnp.concatenate([p["norm_mix_g"][layer:layer + 1], sh1, sc1], axis=0)
        send_gate = ([_chunk_grad("gate", dwg)], True)
        send_up = ([_chunk_grad("up", dwu)], True)
        if layer % 2 == 0:
            (dpre, dwout, dwpool, acc_o), arrived = _ab_bwd_out(
                dx2, y, u, g1, p["ab_conv"][i], p["ab_w_pool"][i].astype(BF16), p["ab_pool_scale"][i:i + 1],
                lw["out"], tm, exch=send_gate)
            received[(layer, "gate")] = arrived[0]
            (dx, dwin, dconv, acc_i), arrived = _ab_bwd_in(dpre, u, xin, dx2, vec_in, p["ab_conv"][i], lw["in"], tm,
                                                           exch=send_up)
            half["ab_w_pool"][i] = dwpool
            half["ab_pool_scale"][i] = acc_o[1, 0:d // 2]
            half["ab_conv"][i] = dconv[0:3]
        else:
            vec_cf = jnp.concatenate([g1, p["cf_ln_g"][i:i + 1], p["cf_ln_b"][i:i + 1]], axis=0)
            (dzc, dwout, acc_o), arrived = _cf_bwd_out(dx2, y, zc, vec_cf, lw["out"], tm, exch=send_gate)
            received[(layer, "gate")] = arrived[0]
            (dx, dwin, dwdw, db1, acc_i), arrived = _cf_bwd_in(
                dzc, u, xin, dx2, vec_in, _pad_rows(p["cf_w_dw"][i], 32), lw["in"], tm, exch=send_up)
            half["cf_b_pw2"][i] = acc_o[1]
            half["cf_ln_g"][i] = acc_o[2]
            half["cf_ln_b"][i] = acc_o[3]
            half["cf_b_dw"][i] = acc_o[4]
            half["cf_w_dw"][i] = dwdw[0:CONF_KERNEL]
            half["cf_b_pw1"][i] = db1[0]
        received[(layer, "up")] = arrived[0]
        per_layer["norm_mix_g"][layer] = acc_i[2]
        dmods[layer] = jnp.stack([acc_i[0], acc_i[1], acc_o[0], acc_u[0], acc_u[1], acc_d[0]], axis=0)
        pending = {"out": _chunk_grad("out", dwout), "in": _chunk_grad("in", dwin)}
    received[(0, "out")], received[(0, "in")] = _chip_exchange([pending["out"], pending["in"]], scatter=True)
    for k, v in {**per_layer, **half}.items():
        grads[k] = jnp.stack(v, axis=0)
    return dx, grads, jnp.stack(dmods, axis=0), received


SMALL_COLS = ("ab_conv", "cf_b_pw1", "cf_w_dw", "cf_b_dw", "cf_ln_g", "cf_ln_b", "cf_b_pw2")
SMALL_REPL = ("norm_mix_g", "norm_ffn_g", "ab_w_pool", "ab_pool_scale", "final_norm_g")
WEIGHTS = ("norm_mix_g", "norm_ffn_g", "w_mod", "b_mod", "ab_w_in", "ab_conv", "ab_w_pool", "ab_pool_scale",
           "ab_w_out", "cf_w_pw1", "cf_b_pw1", "cf_w_dw", "cf_b_dw", "cf_ln_g", "cf_ln_b", "cf_w_pw2",
           "cf_b_pw2", "ffn_w_gate", "ffn_w_up", "ffn_w_down", "final_norm_g")


def kernel(x, c, norm_mix_g, norm_ffn_g, w_mod, b_mod, ab_w_in, ab_conv, ab_w_pool, ab_pool_scale, ab_w_out, cf_w_pw1, cf_b_pw1, cf_w_dw, cf_b_dw, cf_ln_g, cf_ln_b, cf_w_pw2, cf_b_pw2, ffn_w_gate, ffn_w_up, ffn_w_down, final_norm_g, loss_target, m_norm_mix_g, m_norm_ffn_g, m_w_mod, m_b_mod, m_ab_w_in, m_ab_conv, m_ab_w_pool, m_ab_pool_scale, m_ab_w_out, m_cf_w_pw1, m_cf_b_pw1, m_cf_w_dw, m_cf_b_dw, m_cf_ln_g, m_cf_ln_b, m_cf_w_pw2, m_cf_b_pw2, m_ffn_w_gate, m_ffn_w_up, m_ffn_w_down, m_final_norm_g, v_norm_mix_g, v_norm_ffn_g, v_w_mod, v_b_mod, v_ab_w_in, v_ab_conv, v_ab_w_pool, v_ab_pool_scale, v_ab_w_out, v_cf_w_pw1, v_cf_b_pw1, v_cf_w_dw, v_cf_b_dw, v_cf_ln_g, v_cf_ln_b, v_cf_w_pw2, v_cf_b_pw2, v_ffn_w_gate, v_ffn_w_up, v_ffn_w_down, v_final_norm_g):
    w = dict(norm_mix_g=norm_mix_g, norm_ffn_g=norm_ffn_g, w_mod=w_mod, b_mod=b_mod, ab_w_in=ab_w_in,
             ab_conv=ab_conv, ab_w_pool=ab_w_pool, ab_pool_scale=ab_pool_scale, ab_w_out=ab_w_out,
             cf_w_pw1=cf_w_pw1, cf_b_pw1=cf_b_pw1, cf_w_dw=cf_w_dw, cf_b_dw=cf_b_dw, cf_ln_g=cf_ln_g,
             cf_ln_b=cf_ln_b, cf_w_pw2=cf_w_pw2, cf_b_pw2=cf_b_pw2, ffn_w_gate=ffn_w_gate, ffn_w_up=ffn_w_up,
             ffn_w_down=ffn_w_down, final_norm_g=final_norm_g)
    mom = dict(norm_mix_g=m_norm_mix_g, norm_ffn_g=m_norm_ffn_g, w_mod=m_w_mod, b_mod=m_b_mod, ab_w_in=m_ab_w_in,
               ab_conv=m_ab_conv, ab_w_pool=m_ab_w_pool, ab_pool_scale=m_ab_pool_scale, ab_w_out=m_ab_w_out,
               cf_w_pw1=m_cf_w_pw1, cf_b_pw1=m_cf_b_pw1, cf_w_dw=m_cf_w_dw, cf_b_dw=m_cf_b_dw, cf_ln_g=m_cf_ln_g,
               cf_ln_b=m_cf_ln_b, cf_w_pw2=m_cf_w_pw2, cf_b_pw2=m_cf_b_pw2, ffn_w_gate=m_ffn_w_gate,
               ffn_w_up=m_ffn_w_up, ffn_w_down=m_ffn_w_down, final_norm_g=m_final_norm_g)
    var = dict(norm_mix_g=v_norm_mix_g, norm_ffn_g=v_norm_ffn_g, w_mod=v_w_mod, b_mod=v_b_mod, ab_w_in=v_ab_w_in,
               ab_conv=v_ab_conv, ab_w_pool=v_ab_w_pool, ab_pool_scale=v_ab_pool_scale, ab_w_out=v_ab_w_out,
               cf_w_pw1=v_cf_w_pw1, cf_b_pw1=v_cf_b_pw1, cf_w_dw=v_cf_w_dw, cf_b_dw=v_cf_b_dw, cf_ln_g=v_cf_ln_g,
               cf_ln_b=v_cf_ln_b, cf_w_pw2=v_cf_w_pw2, cf_b_pw2=v_cf_b_pw2, ffn_w_gate=v_ffn_w_gate,
               ffn_w_up=v_ffn_w_up, ffn_w_down=v_ffn_w_down, final_norm_g=v_final_norm_g)
    px, py, pc = _place()
    chip = 2 * px + py
    dev = 2 * chip + pc
    depth, d, mod_cols = w_mod.shape
    x = x[0]
    target = loss_target[0]

    small_in = _Packer()
    small_in.add("c", c)
    for name in SMALL_COLS:
        small_in.add(name, w[name])
    gathered = _allgather8(small_in.pack(), with_sum=False).reshape(N_DEV, -1, LANES)
    per_dev = [small_in.unpack(gathered[k]) for k in range(N_DEV)]
    c_all = jnp.concatenate([pd["c"] for pd in per_dev], axis=0)
    params = {name: jnp.concatenate([per_dev[2 * k][name] for k in range(N_CHIPS)], axis=-1)
              for name in SMALL_COLS}
    for name in SMALL_REPL:
        params[name] = w[name]

    mod_part = _mod_fwd(c_all, w_mod, _my_cols(b_mod, chip))
    mod_all = _allgather8(mod_part.reshape(-1, LANES), with_sum=False)
    mod_all = mod_all.reshape(N_CHIPS, 2, depth, N_DEV, mod_cols)[:, 0]
    mod_all = jnp.moveaxis(mod_all, 0, 2).reshape(depth, N_DEV, N_CHIPS * mod_cols)
    mods = lax.dynamic_index_in_dim(mod_all, dev, axis=1, keepdims=False).reshape(depth, 6, d)

    shards = [{k: w[name][idx].astype(BF16) for k, (name, idx) in _layer_big_names(layer).items()}
              for layer in range(depth)]
    grad_x, grads, dmods, received = _local_step(x, target, mods, params, shards)

    small_out = _Packer()
    small_out.add("dmods", dmods)
    for name in ("loss",) + SMALL_REPL + SMALL_COLS:
        small_out.add(name, grads[name])
    parts_all, parts_sum = _allgather8(small_out.pack(), with_sum=True)
    small_sum = small_out.unpack(parts_sum)
    loss = small_sum["loss"][0]
    dmods_all = jnp.stack([small_out.unpack(pa)["dmods"] for pa in parts_all.reshape(N_DEV, -1, LANES)], axis=1)
    dmods_all = dmods_all.reshape(depth, N_DEV, 6 * d)

    g_final = {}
    g_final["w_mod"] = [_mod_bwd(c_all.T, _my_cols(dmods_all, chip))]
    g_final["b_mod"] = [small_sum["dmods"].reshape(depth, 6 * d)]
    for name in SMALL_REPL:
        g_final[name] = [small_sum[name]]
    for name in SMALL_COLS:
        g_final[name] = [_my_cols(small_sum[name], chip)]

    by_name = {}
    for layer in range(depth):
        for k, (name, _) in _layer_big_names(layer).items():
            by_name.setdefault(name, []).append(_sum_parts(received[(layer, k)]))
    big = sorted(by_name)
    mine = [jnp.stack(by_name[name], axis=0) for name in big]
    theirs = _core_swap(mine)
    for name, a, b in zip(big, mine, theirs):
        g_final[name] = [a, b]

    out_g, out_d, out_m, out_v = [], [], [], []
    for name in WEIGHTS:
        g, dlt, mn, vn = _adamw(w[name], g_final[name], mom[name], var[name])
        out_g.append(g)
        out_d.append(dlt)
        out_m.append(mn)
        out_v.append(vn)
    return (loss, grad_x[None], *out_g, *out_d, *out_m, *out_v)
```

```python
import functools

import jax
import jax.numpy as jnp
from jax import lax
from jax.experimental import pallas as pl
from jax.experimental.pallas import tpu as pltpu

F32 = jnp.float32
BF16 = jnp.bfloat16
RMS_EPS = 1e-6
LN_EPS = 1e-5
ADAM_LR = 0.001
ADAM_B1 = 0.9
ADAM_B2 = 0.999
ADAM_EPS = 1e-08
ADAM_WD = 0.01
ADAM_STEP = 10
POOL_WINDOWS = (2, 4, 8, 16)
CONF_KERNEL = 31
N_CHIPS = 4
N_DEV = 8
HALO = 16
CONV_COLS = 256
FFN_CHUNK = 1536
LANES = 1024
VMEM_LIMIT = 56 * 1024 * 1024
EW_BLOCK_ELEMS = 256 * 1024
MESH = pl.DeviceIdType.MESH
HIGHEST = lax.Precision.HIGHEST

_pcall = pl.pallas_call


def _dot(a, b):
    return jnp.dot(a, b, preferred_element_type=F32)


def _dot_tn(a, b):
    return lax.dot_general(a, b, (((0,), (0,)), ((), ())), preferred_element_type=F32)


def _dot_nt(a, b):
    return lax.dot_general(a, b, (((1,), (1,)), ((), ())), preferred_element_type=F32)


def _colsum(v):
    return jnp.sum(v, axis=0, keepdims=True)


def _sigmoid(v):
    return 1.0 / (1.0 + jnp.exp(-v))


def _rows(tm, c):
    return pl.BlockSpec((tm, c), lambda i: (i, 0))


def _full(shape):
    nd = len(shape)
    return pl.BlockSpec(shape, lambda i: (0,) * nd)


_VM = pl.BlockSpec(memory_space=pltpu.VMEM)
_ANY = pl.BlockSpec(memory_space=pl.ANY)


def _halo_specs(tm, c, t_total):
    r = tm // HALO
    last = t_total // HALO - 1
    prev = pl.BlockSpec((HALO, c), lambda i: (jnp.maximum(i * r - 1, 0), 0))
    nxt = pl.BlockSpec((HALO, c), lambda i: (jnp.minimum((i + 1) * r, last), 0))
    return prev, _rows(tm, c), nxt


def _seq_params():
    return pltpu.CompilerParams(dimension_semantics=("arbitrary",), vmem_limit_bytes=VMEM_LIMIT)


def _place():
    return lax.axis_index("x"), lax.axis_index("y"), lax.axis_index("c")


def _peer_chips(x, y):
    return [(1 - x, y), (x, 1 - y), (1 - x, 1 - y)]


class _Gather:
    tag = "gather"

    def __init__(self, arrs):
        self.arrs = list(arrs)

    def out_shapes(self):
        return [jax.ShapeDtypeStruct((N_CHIPS,) + a.shape, a.dtype) for a in self.arrs]

    def sems(self):
        n = len(self.arrs)
        return [pltpu.SemaphoreType.DMA((3 * n,)) for _ in range(4)] + [pltpu.SemaphoreType.DMA((n,))]

    def _copies(self, ins, outs, sems, kinds):
        ici_send, ici_recv, d2d_send, d2d_recv, local_sems = sems
        x, y, c = _place()
        me = 2 * x + y
        found = {kind: [] for kind in kinds}
        for j in range(len(ins)):
            if "local" in kinds:
                found["local"].append(pltpu.make_async_copy(ins[j], outs[j].at[me], local_sems.at[j]))
            for k, (px, py) in enumerate(_peer_chips(x, y)):
                ici = dict(send_sem=ici_send.at[3 * j + k], recv_sem=ici_recv.at[3 * j + k],
                           device_id=(px, py, c), device_id_type=MESH)
                d2d = dict(send_sem=d2d_send.at[3 * j + k], recv_sem=d2d_recv.at[3 * j + k],
                           device_id=(x, y, 1 - c), device_id_type=MESH)
                theirs = outs[j].at[2 * px + py]
                if "send" in kinds:
                    found["send"].append(pltpu.make_async_remote_copy(
                        src_ref=ins[j].at[c], dst_ref=outs[j].at[me, c], **ici))
                if "arrival" in kinds:
                    found["arrival"].append(pltpu.make_async_remote_copy(
                        src_ref=ins[j].at[c], dst_ref=theirs.at[c], **ici))
                if "pass" in kinds:
                    found["pass"].append(pltpu.make_async_remote_copy(
                        src_ref=theirs.at[c], dst_ref=theirs.at[c], **d2d))
                if "passed" in kinds:
                    found["passed"].append(pltpu.make_async_remote_copy(
                        src_ref=theirs.at[c], dst_ref=theirs.at[1 - c], **d2d))
        return found

    def start(self, ins, outs, sems):
        found = self._copies(ins, outs, sems, ("local", "send"))
        for cp in found["local"] + found["send"]:
            cp.start()

    def mid(self, ins, outs, sems):
        found = self._copies(ins, outs, sems, ("arrival", "pass"))
        for arrived, onward in zip(found["arrival"], found["pass"]):
            arrived.wait_recv()
            onward.start()

    def wait(self, ins, outs, sems):
        found = self._copies(ins, outs, sems, ("local", "send", "pass", "passed"))
        for cp in found["passed"]:
            cp.wait_recv()
        for cp in found["send"] + found["pass"]:
            cp.wait_send()
        for cp in found["local"]:
            cp.wait()


class _Scatter:
    tag = "scatter"

    def __init__(self, arrs):
        self.arrs = list(arrs)

    def out_shapes(self):
        return [jax.ShapeDtypeStruct(a.shape, a.dtype) for a in self.arrs]

    def sems(self):
        n = len(self.arrs)
        return [pltpu.SemaphoreType.DMA((3 * n,)), pltpu.SemaphoreType.DMA((3 * n,)), pltpu.SemaphoreType.DMA((n,))]

    def _copies(self, ins, outs, sems, kinds):
        send_sems, recv_sems, local_sems = sems
        x, y, c = _place()
        me = 2 * x + y
        found = {kind: [] for kind in kinds}
        for j in range(len(ins)):
            if "local" in kinds:
                found["local"].append(pltpu.make_async_copy(ins[j].at[me], outs[j].at[me], local_sems.at[j]))
            for k, (px, py) in enumerate(_peer_chips(x, y)):
                sem = dict(send_sem=send_sems.at[3 * j + k], recv_sem=recv_sems.at[3 * j + k],
                           device_id=(px, py, c), device_id_type=MESH)
                if "send" in kinds:
                    found["send"].append(pltpu.make_async_remote_copy(
                        src_ref=ins[j].at[2 * px + py], dst_ref=outs[j].at[me], **sem))
                if "arrival" in kinds:
                    found["arrival"].append(pltpu.make_async_remote_copy(
                        src_ref=ins[j].at[me], dst_ref=outs[j].at[2 * px + py], **sem))
        return found

    def start(self, ins, outs, sems):
        found = self._copies(ins, outs, sems, ("local", "send"))
        for cp in found["local"] + found["send"]:
            cp.start()

    def mid(self, ins, outs, sems):
        pass

    def wait(self, ins, outs, sems):
        found = self._copies(ins, outs, sems, ("local", "send", "arrival"))
        for cp in found["arrival"]:
            cp.wait_recv()
        for cp in found["send"]:
            cp.wait_send()
        for cp in found["local"]:
            cp.wait()


def _standalone(op):
    n = len(op.arrs)

    def body(*refs):
        for phase in (op.start, op.mid, op.wait):
            phase(refs[:n], refs[n:2 * n], refs[2 * n:])

    return _pcall(body, name="chip_" + op.tag, in_specs=[_ANY] * n, out_specs=[_ANY] * n,
                  out_shape=op.out_shapes(), scratch_shapes=op.sems())(*op.arrs)


def _call(body, *, name, nsteps, in_specs, out_specs, out_shape, args, scratch_shapes=(), exch=None):
    if exch is None:
        outs = _pcall(body, name=name, grid=(nsteps,), in_specs=list(in_specs), out_specs=list(out_specs),
                      out_shape=list(out_shape), scratch_shapes=list(scratch_shapes),
                      compiler_params=_seq_params())(*args)
        return list(outs), []
    n, ni, no, ns = len(exch.arrs), len(in_specs), len(out_specs), len(scratch_shapes)

    def hosted(*refs):
        xin = refs[ni:ni + n]
        xout = refs[ni + n + no:ni + 2 * n + no]
        scr = refs[ni + 2 * n + no:]

        @pl.when(pl.program_id(0) == 0)
        def _():
            exch.start(xin, xout, scr[ns:])

        body(*refs[:ni], *refs[ni + n:ni + n + no], *scr[:ns])

        @pl.when(pl.program_id(0) == max(nsteps - 3, 0))
        def _():
            exch.mid(xin, xout, scr[ns:])

        @pl.when(pl.program_id(0) == nsteps - 1)
        def _():
            exch.wait(xin, xout, scr[ns:])

    outs = _pcall(hosted, name=name + "_" + exch.tag, grid=(nsteps,),
                  in_specs=[*in_specs, *[_ANY] * n], out_specs=[*out_specs, *[_ANY] * n],
                  out_shape=[*out_shape, *exch.out_shapes()],
                  scratch_shapes=[*scratch_shapes, *exch.sems()],
                  compiler_params=_seq_params())(*args, *exch.arrs)
    return list(outs[:no]), list(outs[no:])


def _rms(x):
    r = lax.rsqrt(jnp.mean(x * x, axis=-1, keepdims=True) + RMS_EPS)
    return x * r, r


def _norm_mod(x, g, sh, sc):
    xhat, _ = _rms(x)
    return xhat * g * (1.0 + sc) + sh


def _norm_mod_bwd(dh, x, g, sc):
    xhat, r = _rms(x)
    n = xhat * g
    dsh = _colsum(dh)
    dsc = _colsum(dh * n)
    dn = dh * (1.0 + sc)
    dg = _colsum(dn * xhat)
    dxn = dn * g
    dx = r * (dxn - xhat * jnp.mean(dxn * xhat, axis=-1, keepdims=True))
    return dx, dsh, dsc, dg


def _fill_ext(ext_ref, prev, cur, nxt, i, nsteps, tm):
    ext_ref[0:HALO, :] = jnp.where(i > 0, prev, 0.0)
    ext_ref[HALO:HALO + tm, :] = cur
    ext_ref[HALO + tm:HALO + tm + HALO, :] = jnp.where(i < nsteps - 1, nxt, 0.0)


def _shift_scratch(tm):
    return pltpu.VMEM((8, tm + 2 * HALO - 8, CONV_COLS), F32)


def _fill_shifts(sh_ref, ext_ref, lo, hi, tm):
    for b in range(8):
        sh_ref[b] = ext_ref[b:b + tm + 2 * HALO - 8, lo:hi]


def _shifted(sh_ref, offset, tm):
    b = offset % 8
    start = HALO + offset - b
    return sh_ref[b, start:start + tm, :]


def _window_count(t, wdw, t_total):
    left = wdw // 2
    right = wdw - 1 - left
    cnt = jnp.minimum(t + right, t_total - 1) - jnp.maximum(t - left, 0) + 1
    return jnp.maximum(cnt, 1).astype(F32)


def _in_proj(x, vec, w, bias, tm, exch=None):
    t_total, d = x.shape
    n = w.shape[1]
    has_bias = bias is not None

    def body(*refs):
        if has_bias:
            x_ref, vec_ref, w_ref, b_ref, u_ref = refs
        else:
            x_ref, vec_ref, w_ref, u_ref = refs
        h = _norm_mod(x_ref[...], vec_ref[0:1, :], vec_ref[1:2, :], vec_ref[2:3, :])
        u = _dot(h.astype(BF16), w_ref[...])
        if has_bias:
            u = u + b_ref[...]
        u_ref[...] = u.astype(BF16)

    in_specs = [_rows(tm, d), _full(vec.shape), _VM]
    args = [x, vec, w]
    if has_bias:
        in_specs.append(_full(bias.shape))
        args.append(bias)
    return _call(
        body, name="in_proj_bias" if has_bias else "in_proj", nsteps=t_total // tm,
        in_specs=in_specs, out_specs=[_rows(tm, n)], out_shape=[jax.ShapeDtypeStruct((t_total, n), BF16)],
        args=args, exch=exch)


def _ab_core(up_ref, uc_ref, un_ref, conv_ref, wpool_ref, q_ext, p_ext, i, nsteps, tm, t_total):
    da = uc_ref.shape[1] // 4

    def cols(ref, k):
        return ref[:, k * da:(k + 1) * da].astype(F32)

    _fill_ext(q_ext, cols(up_ref, 1) * cols(up_ref, 2), cols(uc_ref, 1) * cols(uc_ref, 2),
              cols(un_ref, 1) * cols(un_ref, 2), i, nsteps, tm)
    _fill_ext(p_ext, cols(up_ref, 3), cols(uc_ref, 3), cols(un_ref, 3), i, nsteps, tm)
    bg = cols(uc_ref, 0)
    cq = (conv_ref[0:1, :] * q_ext[HALO - 1:HALO - 1 + tm, :] + conv_ref[1:2, :] * q_ext[HALO:HALO + tm, :]
          + conv_ref[2:3, :] * q_ext[HALO + 1:HALO + 1 + tm, :])
    t = i * tm + lax.broadcasted_iota(jnp.int32, (tm, 1), 0)
    gw = da // len(POOL_WINDOWS)
    pooled, ybpre = [], []
    for g, wdw in enumerate(POOL_WINDOWS):
        left = wdw // 2
        right = wdw - 1 - left
        lo, hi = g * gw, (g + 1) * gw
        s = p_ext[HALO - left:HALO - left + tm, lo:hi]
        for o in range(-left + 1, right + 1):
            s = s + p_ext[HALO + o:HALO + o + tm, lo:hi]
        pg = s / _window_count(t, wdw, t_total) - p_ext[HALO:HALO + tm, lo:hi]
        pooled.append(pg.astype(BF16))
        ybpre.append(_dot(pooled[-1], wpool_ref[g]))
    return bg, cq, pooled, jnp.concatenate(ybpre, axis=1)


def _ab_fwd(u, x, vec, conv, wpool, scale, wout, tm, exch=None):
    t_total, d = x.shape
    nu = u.shape[1]
    da = nu // 4
    nsteps = t_total // tm

    def body(up_ref, uc_ref, un_ref, x_ref, vec_ref, conv_ref, wpool_ref, scale_ref, wout_ref,
             y_ref, x2_ref, q_ext, p_ext):
        i = pl.program_id(0)
        bg, cq, _, ybpre = _ab_core(up_ref, uc_ref, un_ref, conv_ref, wpool_ref, q_ext, p_ext,
                                    i, nsteps, tm, t_total)
        cat = jnp.concatenate([bg * cq, ybpre * scale_ref[...]], axis=1).astype(BF16)
        y = _dot(cat, wout_ref[...])
        y_ref[...] = y.astype(BF16)
        x2_ref[...] = x_ref[...] + vec_ref[0:1, :] * y

    return _call(
        body, name="ab_fwd", nsteps=nsteps,
        in_specs=[*_halo_specs(tm, nu, t_total), _rows(tm, d), _full(vec.shape), _full(conv.shape),
                  _full(wpool.shape), _full(scale.shape), _VM],
        out_specs=[_rows(tm, d), _rows(tm, d)],
        out_shape=[jax.ShapeDtypeStruct((t_total, d), BF16), jax.ShapeDtypeStruct((t_total, d), F32)],
        scratch_shapes=[pltpu.VMEM((tm + 2 * HALO, da), F32), pltpu.VMEM((tm + 2 * HALO, da), F32)],
        args=(u, u, u, x, vec, conv, wpool, scale, wout), exch=exch)


def _glu_ext(up_ref, uc_ref, un_ref, z_ext, i, nsteps, tm):
    dz = uc_ref.shape[1] // 2

    def glu(ref):
        return ref[:, 0:dz].astype(F32) * _sigmoid(ref[:, dz:2 * dz].astype(F32))

    _fill_ext(z_ext, glu(up_ref), glu(uc_ref), glu(un_ref), i, nsteps, tm)


def _layer_norm_stats(zc):
    mu = jnp.mean(zc, axis=-1, keepdims=True)
    dlt = zc - mu
    rstd = lax.rsqrt(jnp.mean(dlt * dlt, axis=-1, keepdims=True) + LN_EPS)
    return dlt * rstd, rstd


def _cf_fwd(u, x, vec, wdw, wpw2, tm, exch=None):
    t_total, d = x.shape
    nu = u.shape[1]
    nsteps = t_total // tm
    left = (CONF_KERNEL - 1) // 2

    def body(up_ref, uc_ref, un_ref, x_ref, vec_ref, wdw_ref, wpw2_ref, zc_ref, y_ref, x2_ref, z_ext, sh_ref,
             zc_buf):
        i = pl.program_id(0)
        _glu_ext(up_ref, uc_ref, un_ref, z_ext, i, nsteps, tm)
        for lo in range(0, d, CONV_COLS):
            hi = lo + CONV_COLS
            _fill_shifts(sh_ref, z_ext, lo, hi, tm)
            acc = wdw_ref[0:1, lo:hi] * _shifted(sh_ref, -left, tm)
            for k in range(1, CONF_KERNEL):
                acc = acc + wdw_ref[k:k + 1, lo:hi] * _shifted(sh_ref, k - left, tm)
            zc_buf[:, lo:hi] = acc
        zc = zc_buf[...] + vec_ref[1:2, :]
        zc_ref[...] = zc.astype(BF16)
        zn, _ = _layer_norm_stats(zc)
        zl = zn * vec_ref[2:3, :] + vec_ref[3:4, :]
        zs = zl * _sigmoid(zl)
        y = _dot(zs.astype(BF16), wpw2_ref[...]) + vec_ref[4:5, :]
        y_ref[...] = y.astype(BF16)
        x2_ref[...] = x_ref[...] + vec_ref[0:1, :] * y

    return _call(
        body, name="cf_fwd", nsteps=nsteps,
        in_specs=[*_halo_specs(tm, nu, t_total), _rows(tm, d), _full(vec.shape), _full(wdw.shape), _VM],
        out_specs=[_rows(tm, d), _rows(tm, d), _rows(tm, d)],
        out_shape=[jax.ShapeDtypeStruct((t_total, d), BF16), jax.ShapeDtypeStruct((t_total, d), BF16),
                   jax.ShapeDtypeStruct((t_total, d), F32)],
        scratch_shapes=[pltpu.VMEM((tm + 2 * HALO, d), F32), _shift_scratch(tm), pltpu.VMEM((tm, d), F32)],
        args=(u, u, u, x, vec, wdw, wpw2), exch=exch)


def _ffn_chunks(f):
    return [(lo, min(lo + FFN_CHUNK, f)) for lo in range(0, f, FFN_CHUNK)]


def _ffn_fwd(x2, vec, wg, wu, wd, tm, exch=None):
    t_total, d = x2.shape
    f = wg.shape[1]

    def body(x_ref, vec_ref, wg_ref, wu_ref, wd_ref, a_ref, b_ref, f_ref, x3_ref):
        xv = x_ref[...]
        h = _norm_mod(xv, vec_ref[0:1, :], vec_ref[1:2, :], vec_ref[2:3, :]).astype(BF16)
        y = None
        for lo, hi in _ffn_chunks(f):
            a = _dot(h, wg_ref[:, lo:hi])
            b = _dot(h, wu_ref[:, lo:hi])
            a_ref[:, lo:hi] = a.astype(BF16)
            b_ref[:, lo:hi] = b.astype(BF16)
            s = (a * _sigmoid(a) * b).astype(BF16)
            part = _dot(s, wd_ref[lo:hi, :])
            y = part if y is None else y + part
        f_ref[...] = y.astype(BF16)
        x3_ref[...] = xv + vec_ref[3:4, :] * y

    return _call(
        body, name="ffn_fwd", nsteps=t_total // tm,
        in_specs=[_rows(tm, d), _full(vec.shape), _VM, _VM, _VM],
        out_specs=[_rows(tm, f), _rows(tm, f), _rows(tm, d), _rows(tm, d)],
        out_shape=[jax.ShapeDtypeStruct((t_total, f), BF16), jax.ShapeDtypeStruct((t_total, f), BF16),
                   jax.ShapeDtypeStruct((t_total, d), BF16), jax.ShapeDtypeStruct((t_total, d), F32)],
        args=(x2, vec, wg, wu, wd), exch=exch)


def _final_fwd_bwd(x, target, vec, tm):
    t_total, d = x.shape

    def body(x_ref, t_ref, vec_ref, dx_ref, acc_ref):
        @pl.when(pl.program_id(0) == 0)
        def _():
            acc_ref[...] = jnp.zeros_like(acc_ref)

        g = vec_ref[0:1, :]
        xhat, r = _rms(x_ref[...])
        e = xhat * g - t_ref[...]
        acc_ref[1:2, :] += jnp.zeros((1, d), F32) + 0.5 * jnp.sum(jnp.mean(e * e, axis=-1, keepdims=True))
        dout = e * (1.0 / d)
        acc_ref[0:1, :] += _colsum(dout * xhat)
        dxn = dout * g
        dx_ref[...] = r * (dxn - xhat * jnp.mean(dxn * xhat, axis=-1, keepdims=True))

    return _call(
        body, name="final_fwd_bwd", nsteps=t_total // tm,
        in_specs=[_rows(tm, d), _rows(tm, d), _full(vec.shape)],
        out_specs=[_rows(tm, d), _VM],
        out_shape=[jax.ShapeDtypeStruct((t_total, d), F32), jax.ShapeDtypeStruct((8, d), F32)],
        args=(x, target, vec))


def _zero_at_start(*refs):
    @pl.when(pl.program_id(0) == 0)
    def _():
        for ref in refs:
            ref[...] = jnp.zeros_like(ref)


def _ffn_bwd_down(dx3, fout, a, b, vec, wd, tm, exch=None):
    t_total, d = dx3.shape
    f = a.shape[1]

    def body(dx_ref, f_ref, a_ref, b_ref, vec_ref, wd_ref, da_ref, db_ref, dwd_ref, acc_ref):
        _zero_at_start(dwd_ref, acc_ref)
        dx = dx_ref[...]
        acc_ref[0:1, :] += _colsum(dx * f_ref[...].astype(F32))
        dy = (dx * vec_ref[0:1, :]).astype(BF16)
        for lo, hi in _ffn_chunks(f):
            av = a_ref[:, lo:hi].astype(F32)
            bv = b_ref[:, lo:hi].astype(F32)
            sg = _sigmoid(av)
            silu = av * sg
            dwd_ref[lo:hi, :] += _dot_tn((silu * bv).astype(BF16), dy)
            ds = _dot_nt(dy, wd_ref[lo:hi, :])
            da_ref[:, lo:hi] = (ds * bv * (sg * (1.0 + av * (1.0 - sg)))).astype(BF16)
            db_ref[:, lo:hi] = (ds * silu).astype(BF16)

    return _call(
        body, name="ffn_bwd_down", nsteps=t_total // tm,
        in_specs=[_rows(tm, d), _rows(tm, d), _rows(tm, f), _rows(tm, f), _full(vec.shape), _VM],
        out_specs=[_rows(tm, f), _rows(tm, f), _VM, _VM],
        out_shape=[jax.ShapeDtypeStruct((t_total, f), BF16), jax.ShapeDtypeStruct((t_total, f), BF16),
                   jax.ShapeDtypeStruct(wd.shape, F32), jax.ShapeDtypeStruct((8, d), F32)],
        args=(dx3, fout, a, b, vec, wd), exch=exch)


def _ffn_bwd_up(da, db, x2, dx3, vec, wg, wu, tm, exch=None):
    t_total, d = x2.shape
    f = da.shape[1]

    def body(da_ref, db_ref, x_ref, dx_ref, vec_ref, wg_ref, wu_ref, dx2_ref, dwg_ref, dwu_ref, acc_ref):
        _zero_at_start(dwg_ref, dwu_ref, acc_ref)
        xv = x_ref[...]
        g, sh, sc = vec_ref[0:1, :], vec_ref[1:2, :], vec_ref[2:3, :]
        h = _norm_mod(xv, g, sh, sc).astype(BF16)
        dav = da_ref[...]
        dbv = db_ref[...]
        dwg_ref[...] += _dot_tn(h, dav)
        dwu_ref[...] += _dot_tn(h, dbv)
        dh = _dot_nt(dav, wg_ref[...]) + _dot_nt(dbv, wu_ref[...])
        dxn, dsh, dsc, dg = _norm_mod_bwd(dh, xv, g, sc)
        acc_ref[0:1, :] += dsh
        acc_ref[1:2, :] += dsc
        acc_ref[2:3, :] += dg
        dx2_ref[...] = dx_ref[...] + dxn

    return _call(
        body, name="ffn_bwd_up", nsteps=t_total // tm,
        in_specs=[_rows(tm, f), _rows(tm, f), _rows(tm, d), _rows(tm, d), _full(vec.shape), _VM, _VM],
        out_specs=[_rows(tm, d), _VM, _VM, _VM],
        out_shape=[jax.ShapeDtypeStruct((t_total, d), F32), jax.ShapeDtypeStruct(wg.shape, F32),
                   jax.ShapeDtypeStruct(wu.shape, F32), jax.ShapeDtypeStruct((8, d), F32)],
        args=(da, db, x2, dx3, vec, wg, wu), exch=exch)


def _ab_bwd_out(dx, y, u, vec, conv, wpool, scale, wout, tm, exch=None):
    t_total, d = dx.shape
    nu = u.shape[1]
    da = nu // 4
    gw = da // len(POOL_WINDOWS)
    nsteps = t_total // tm

    def body(dx_ref, y_ref, up_ref, uc_ref, un_ref, vec_ref, conv_ref, wpool_ref, scale_ref, wout_ref,
             dpre_ref, dwout_ref, dwpool_ref, acc_ref, q_ext, p_ext):
        _zero_at_start(dwout_ref, dwpool_ref, acc_ref)
        i = pl.program_id(0)
        dxv = dx_ref[...]
        acc_ref[0:1, :] += _colsum(dxv * y_ref[...].astype(F32))
        dy = (dxv * vec_ref[0:1, :]).astype(BF16)
        bg, cq, pooled, ybpre = _ab_core(up_ref, uc_ref, un_ref, conv_ref, wpool_ref, q_ext, p_ext,
                                         i, nsteps, tm, t_total)
        cat = jnp.concatenate([bg * cq, ybpre * scale_ref[...]], axis=1).astype(BF16)
        dwout_ref[...] += _dot_tn(cat, dy)
        dcat = _dot_nt(dy, wout_ref[...])
        dya = dcat[:, 0:da]
        dyb = dcat[:, da:2 * da]
        acc_ref[1:2, 0:da] += _colsum(dyb * ybpre)
        dybpre = (dyb * scale_ref[...]).astype(BF16)
        dpooled = []
        for g in range(len(POOL_WINDOWS)):
            dg = dybpre[:, g * gw:(g + 1) * gw]
            dwpool_ref[g] += _dot_tn(pooled[g], dg)
            dpooled.append(_dot_nt(dg, wpool_ref[g]))
        dpre_ref[...] = jnp.concatenate([dya * cq, dya * bg] + dpooled, axis=1).astype(BF16)

    return _call(
        body, name="ab_bwd_out", nsteps=nsteps,
        in_specs=[_rows(tm, d), _rows(tm, d), *_halo_specs(tm, nu, t_total), _full(vec.shape),
                  _full(conv.shape), _full(wpool.shape), _full(scale.shape), _VM],
        out_specs=[_rows(tm, 3 * da), _VM, _VM, _VM],
        out_shape=[jax.ShapeDtypeStruct((t_total, 3 * da), BF16), jax.ShapeDtypeStruct(wout.shape, F32),
                   jax.ShapeDtypeStruct(wpool.shape, F32), jax.ShapeDtypeStruct((8, d), F32)],
        scratch_shapes=[pltpu.VMEM((tm + 2 * HALO, da), F32), pltpu.VMEM((tm + 2 * HALO, da), F32)],
        args=(dx, y, u, u, u, vec, conv, wpool, scale, wout), exch=exch)


def _ab_bwd_in(dpre, u, x, dx, vec, conv, win, tm, exch=None):
    t_total, d = x.shape
    nu = u.shape[1]
    da = nu // 4
    gw = da // len(POOL_WINDOWS)
    nsteps = t_total // tm

    def body(dp_ref, dc_ref, dn_ref, up_ref, uc_ref, un_ref, x_ref, dx_ref, vec_ref, conv_ref, win_ref,
             dxin_ref, dwin_ref, dconv_ref, acc_ref, dcq_ext, q_ext, dpl_ext):
        _zero_at_start(dwin_ref, dconv_ref, acc_ref)
        i = pl.program_id(0)

        def ucols(ref, k):
            return ref[:, k * da:(k + 1) * da].astype(F32)

        def dcols(ref, k):
            return ref[:, k * da:(k + 1) * da].astype(F32)

        _fill_ext(dcq_ext, dcols(dp_ref, 1), dcols(dc_ref, 1), dcols(dn_ref, 1), i, nsteps, tm)
        _fill_ext(q_ext, ucols(up_ref, 1) * ucols(up_ref, 2), ucols(uc_ref, 1) * ucols(uc_ref, 2),
                  ucols(un_ref, 1) * ucols(un_ref, 2), i, nsteps, tm)
        _fill_ext(dpl_ext, dcols(dp_ref, 2), dcols(dc_ref, 2), dcols(dn_ref, 2), i, nsteps, tm)
        dq = (conv_ref[0:1, :] * dcq_ext[HALO + 1:HALO + 1 + tm, :] + conv_ref[1:2, :] * dcq_ext[HALO:HALO + tm, :]
              + conv_ref[2:3, :] * dcq_ext[HALO - 1:HALO - 1 + tm, :])
        dcq = dcq_ext[HALO:HALO + tm, :]
        for k in range(3):
            dconv_ref[k:k + 1, :] += _colsum(dcq * q_ext[HALO + k - 1:HALO + k - 1 + tm, :])
        dcg = dq * ucols(uc_ref, 2)
        dv = dq * ucols(uc_ref, 1)
        t_ext = i * tm - HALO + lax.broadcasted_iota(jnp.int32, (tm + 2 * HALO, 1), 0)
        dps = []
        for g, wdw in enumerate(POOL_WINDOWS):
            left = wdw // 2
            right = wdw - 1 - left
            lo, hi = g * gw, (g + 1) * gw
            dpg = dpl_ext[HALO:HALO + tm, lo:hi]
            dpl_ext[:, lo:hi] = dpl_ext[:, lo:hi] / _window_count(t_ext, wdw, t_total)
            s = dpl_ext[HALO - right:HALO - right + tm, lo:hi]
            for o in range(-right + 1, left + 1):
                s = s + dpl_ext[HALO + o:HALO + o + tm, lo:hi]
            dps.append(s - dpg)
        du = jnp.concatenate([dcols(dc_ref, 0), dcg, dv] + dps, axis=1).astype(BF16)
        xv = x_ref[...]
        g, sh, sc = vec_ref[0:1, :], vec_ref[1:2, :], vec_ref[2:3, :]
        h = _norm_mod(xv, g, sh, sc).astype(BF16)
        dwin_ref[...] += _dot_tn(h, du)
        dh = _dot_nt(du, win_ref[...])
        dxn, dsh, dsc, dg = _norm_mod_bwd(dh, xv, g, sc)
        acc_ref[0:1, :] += dsh
        acc_ref[1:2, :] += dsc
        acc_ref[2:3, :] += dg
        dxin_ref[...] = dx_ref[...] + dxn

    ext = pltpu.VMEM((tm + 2 * HALO, da), F32)
    return _call(
        body, name="ab_bwd_in", nsteps=nsteps,
        in_specs=[*_halo_specs(tm, 3 * da, t_total), *_halo_specs(tm, nu, t_total), _rows(tm, d), _rows(tm, d),
                  _full(vec.shape), _full(conv.shape), _VM],
        out_specs=[_rows(tm, d), _VM, _VM, _VM],
        out_shape=[jax.ShapeDtypeStruct((t_total, d), F32), jax.ShapeDtypeStruct(win.shape, F32),
                   jax.ShapeDtypeStruct((8, da), F32), jax.ShapeDtypeStruct((8, d), F32)],
        scratch_shapes=[ext, ext, ext],
        args=(dpre, dpre, dpre, u, u, u, x, dx, vec, conv, win), exch=exch)


def _cf_bwd_out(dx, y, zc, vec, wpw2, tm, exch=None):
    t_total, d = dx.shape

    def body(dx_ref, y_ref, zc_ref, vec_ref, w_ref, dzc_ref, dw_ref, acc_ref):
        _zero_at_start(dw_ref, acc_ref)
        dxv = dx_ref[...]
        acc_ref[0:1, :] += _colsum(dxv * y_ref[...].astype(F32))
        dyf = dxv * vec_ref[0:1, :]
        acc_ref[1:2, :] += _colsum(dyf)
        dy = dyf.astype(BF16)
        zn, rstd = _layer_norm_stats(zc_ref[...].astype(F32))
        lng = vec_ref[1:2, :]
        zl = zn * lng + vec_ref[2:3, :]
        sg = _sigmoid(zl)
        dw_ref[...] += _dot_tn((zl * sg).astype(BF16), dy)
        dzl = _dot_nt(dy, w_ref[...]) * (sg * (1.0 + zl * (1.0 - sg)))
        acc_ref[2:3, :] += _colsum(dzl * zn)
        acc_ref[3:4, :] += _colsum(dzl)
        dzn = dzl * lng
        dzc = rstd * (dzn - jnp.mean(dzn, axis=-1, keepdims=True)
                      - zn * jnp.mean(dzn * zn, axis=-1, keepdims=True))
        acc_ref[4:5, :] += _colsum(dzc)
        dzc_ref[...] = dzc.astype(BF16)

    return _call(
        body, name="cf_bwd_out", nsteps=t_total // tm,
        in_specs=[_rows(tm, d), _rows(tm, d), _rows(tm, d), _full(vec.shape), _VM],
        out_specs=[_rows(tm, d), _VM, _VM],
        out_shape=[jax.ShapeDtypeStruct((t_total, d), BF16), jax.ShapeDtypeStruct(wpw2.shape, F32),
                   jax.ShapeDtypeStruct((8, d), F32)],
        args=(dx, y, zc, vec, wpw2), exch=exch)


def _cf_bwd_in(dzc, u, x, dx, vec, wdw, wpw1, tm, exch=None):
    t_total, d = x.shape
    nu = u.shape[1]
    nsteps = t_total // tm
    left = (CONF_KERNEL - 1) // 2

    def body(dp_ref, dc_ref, dn_ref, up_ref, uc_ref, un_ref, x_ref, dx_ref, vec_ref, wdw_ref, w_ref,
             dxin_ref, dw_ref, dwdw_ref, db1_ref, acc_ref, dzc_ext, z_ext, sh_ref, dz_buf):
        _zero_at_start(dw_ref, dwdw_ref, db1_ref, acc_ref)
        i = pl.program_id(0)
        _fill_ext(dzc_ext, dp_ref[...].astype(F32), dc_ref[...].astype(F32), dn_ref[...].astype(F32),
                  i, nsteps, tm)
        _glu_ext(up_ref, uc_ref, un_ref, z_ext, i, nsteps, tm)
        for lo in range(0, d, CONV_COLS):
            hi = lo + CONV_COLS
            _fill_shifts(sh_ref, dzc_ext, lo, hi, tm)
            acc = wdw_ref[0:1, lo:hi] * _shifted(sh_ref, left, tm)
            for k in range(1, CONF_KERNEL):
                acc = acc + wdw_ref[k:k + 1, lo:hi] * _shifted(sh_ref, left - k, tm)
            dz_buf[:, lo:hi] = acc
            dzc = dzc_ext[HALO:HALO + tm, lo:hi]
            _fill_shifts(sh_ref, z_ext, lo, hi, tm)
            for k in range(CONF_KERNEL):
                dwdw_ref[k:k + 1, lo:hi] += _colsum(dzc * _shifted(sh_ref, k - left, tm))
        dz = dz_buf[...]
        av = uc_ref[:, 0:d].astype(F32)
        sg = _sigmoid(uc_ref[:, d:2 * d].astype(F32))
        duf = jnp.concatenate([dz * sg, dz * av * sg * (1.0 - sg)], axis=1)
        db1_ref[0:1, :] += _colsum(duf)
        du = duf.astype(BF16)
        xv = x_ref[...]
        g, sh, sc = vec_ref[0:1, :], vec_ref[1:2, :], vec_ref[2:3, :]
        h = _norm_mod(xv, g, sh, sc).astype(BF16)
        dw_ref[...] += _dot_tn(h, du)
        dh = _dot_nt(du, w_ref[...])
        dxn, dsh, dsc, dg = _norm_mod_bwd(dh, xv, g, sc)
        acc_ref[0:1, :] += dsh
        acc_ref[1:2, :] += dsc
        acc_ref[2:3, :] += dg
        dxin_ref[...] = dx_ref[...] + dxn

    ext = pltpu.VMEM((tm + 2 * HALO, d), F32)
    return _call(
        body, name="cf_bwd_in", nsteps=nsteps,
        in_specs=[*_halo_specs(tm, d, t_total), *_halo_specs(tm, nu, t_total), _rows(tm, d), _rows(tm, d),
                  _full(vec.shape), _full(wdw.shape), _VM],
        out_specs=[_rows(tm, d), _VM, _VM, _VM, _VM],
        out_shape=[jax.ShapeDtypeStruct((t_total, d), F32), jax.ShapeDtypeStruct(wpw1.shape, F32),
                   jax.ShapeDtypeStruct((32, d), F32), jax.ShapeDtypeStruct((8, nu), F32),
                   jax.ShapeDtypeStruct((8, d), F32)],
        scratch_shapes=[ext, ext, _shift_scratch(tm), pltpu.VMEM((tm, d), F32)],
        args=(dzc, dzc, dzc, u, u, u, x, dx, vec, wdw, wpw1), exch=exch)


def _mod_fwd(c_all, w_mod, b_cols):
    nl, d, ncol = w_mod.shape
    nb = c_all.shape[0]

    def body(c_ref, w_ref, b_ref, o_ref):
        cv = c_ref[...]
        ca = cv * _sigmoid(cv)
        o_ref[0] = jnp.dot(ca, w_ref[0], preferred_element_type=F32, precision=HIGHEST) + b_ref[0]

    return _pcall(
        body, name="mod_fwd", grid=(nl,),
        in_specs=[_full(c_all.shape), pl.BlockSpec((1, d, ncol), lambda l: (l, 0, 0)),
                  pl.BlockSpec((1, 1, ncol), lambda l: (l, 0, 0))],
        out_specs=pl.BlockSpec((1, nb, ncol), lambda l: (l, 0, 0)),
        out_shape=jax.ShapeDtypeStruct((nl, nb, ncol), F32),
        compiler_params=_seq_params(),
    )(c_all, w_mod, b_cols.reshape(nl, 1, ncol))


def _mod_bwd(c_all_t, dmod_cols):
    d, nb = c_all_t.shape
    nl, _, ncol = dmod_cols.shape

    def body(c_ref, dm_ref, o_ref):
        cv = c_ref[...]
        ca = cv * _sigmoid(cv)
        o_ref[0] = jnp.dot(ca, dm_ref[0], preferred_element_type=F32, precision=HIGHEST)

    return _pcall(
        body, name="mod_bwd", grid=(nl,),
        in_specs=[_full(c_all_t.shape), pl.BlockSpec((1, nb, ncol), lambda l: (l, 0, 0))],
        out_specs=pl.BlockSpec((1, d, ncol), lambda l: (l, 0, 0)),
        out_shape=jax.ShapeDtypeStruct((nl, d, ncol), F32),
        compiler_params=_seq_params(),
    )(c_all_t, dmod_cols)


def _row_block(r, c):
    if r * c <= EW_BLOCK_ELEMS:
        return r
    best = None
    for br in range(8, r, 8):
        if r % br == 0 and br * c <= EW_BLOCK_ELEMS:
            best = br
    assert best is not None, (r, c)
    return best


def _as2d(a):
    return a.reshape(-1, a.shape[-1])


def _sum_parts(parts):
    k = parts.shape[0]
    p3 = parts.reshape(k, -1, parts.shape[-1])
    _, r, c = p3.shape
    br = _row_block(r, c)

    def body(p_ref, o_ref):
        acc = p_ref[0].astype(F32)
        for j in range(1, k):
            acc = acc + p_ref[j].astype(F32)
        o_ref[...] = acc

    out = _pcall(
        body, name="sum_parts", grid=(r // br,),
        in_specs=[pl.BlockSpec((k, br, c), lambda i: (0, i, 0))],
        out_specs=pl.BlockSpec((br, c), lambda i: (i, 0)),
        out_shape=jax.ShapeDtypeStruct((r, c), F32),
        compiler_params=_seq_params(),
    )(p3)
    return out.reshape(parts.shape[1:])


def _adamw(w, gparts, m, v):
    shape = w.shape
    w2, m2, v2 = _as2d(w), _as2d(m), _as2d(v)
    g2 = [_as2d(g) for g in gparts]
    r, c = w2.shape
    br = _row_block(r, c)
    ng = len(g2)

    def body(*refs):
        w_ref, m_ref, v_ref = refs[0:3]
        g_refs = refs[3:3 + ng]
        go_ref, d_ref, mo_ref, vo_ref = refs[3 + ng:]
        g = g_refs[0][...]
        for gr in g_refs[1:]:
            g = g + gr[...]
        mn = ADAM_B1 * m_ref[...] + (1.0 - ADAM_B1) * g
        vn = ADAM_B2 * v_ref[...] + (1.0 - ADAM_B2) * (g * g)
        m_hat = mn / (1.0 - ADAM_B1 ** ADAM_STEP)
        v_hat = vn / (1.0 - ADAM_B2 ** ADAM_STEP)
        go_ref[...] = g
        d_ref[...] = -ADAM_LR * (m_hat / (jnp.sqrt(v_hat) + ADAM_EPS) + ADAM_WD * w_ref[...])
        mo_ref[...] = mn
        vo_ref[...] = vn

    spec = pl.BlockSpec((br, c), lambda i: (i, 0))
    outs = _pcall(
        body, name="adamw", grid=(r // br,),
        in_specs=[spec] * (3 + ng), out_specs=[spec] * 4,
        out_shape=[jax.ShapeDtypeStruct((r, c), F32)] * 4,
        compiler_params=_seq_params(),
    )(w2, m2, v2, *g2)
    return tuple(o.reshape(shape) for o in outs)


def _allgather8(block, with_sum):
    m_per, n = block.shape

    def body(x_ref, out_ref, *rest):
        if with_sum:
            sum_ref, send_sems, recv_sems, local_sem = rest
        else:
            send_sems, recv_sems, local_sem = rest
        x, y, c = _place()
        me, sibling = (x, y, c), (x, y, 1 - c)
        chips = [(1 - x, y), (x, 1 - y), (1 - x, 1 - y)]

        def rows(px, py, pc):
            return out_ref.at[pl.ds((4 * px + 2 * py + pc) * m_per, m_per), :]

        def copy(k, blk, to, src=None):
            return pltpu.make_async_remote_copy(
                src_ref=rows(*blk) if src is None else src, dst_ref=rows(*blk),
                send_sem=send_sems.at[k], recv_sem=recv_sems.at[k], device_id=to, device_id_type=MESH)

        mine = pltpu.make_async_copy(x_ref, rows(*me), local_sem)
        mine.start()
        first = [copy(0, me, sibling, src=x_ref)]
        first += [copy(1 + j, me, (*chip, c), src=x_ref) for j, chip in enumerate(chips)]
        for cp in first:
            cp.start()
        passed = [copy(4 + j, (*chip, c), sibling) for j, chip in enumerate(chips)]
        for j, chip in enumerate(chips):
            copy(1 + j, (*chip, c), me).wait_recv()
            passed[j].start()
        copy(0, sibling, me).wait_recv()
        for j, chip in enumerate(chips):
            copy(4 + j, (*chip, 1 - c), me).wait_recv()
        for cp in first + passed:
            cp.wait_send()
        mine.wait()
        if with_sum:
            acc = out_ref[0:m_per, :]
            for k in range(1, N_DEV):
                acc = acc + out_ref[k * m_per:(k + 1) * m_per, :]
            sum_ref[...] = acc

    out_shape = [jax.ShapeDtypeStruct((N_DEV * m_per, n), F32)]
    out_specs = [_VM]
    if with_sum:
        out_shape.append(jax.ShapeDtypeStruct((m_per, n), F32))
        out_specs.append(_VM)
    res = _pcall(
        body, name="allgather8_sum" if with_sum else "allgather8",
        in_specs=[_VM], out_specs=out_specs, out_shape=out_shape,
        scratch_shapes=[pltpu.SemaphoreType.DMA((7,)), pltpu.SemaphoreType.DMA((7,)), pltpu.SemaphoreType.DMA],
        compiler_params=pltpu.CompilerParams(vmem_limit_bytes=VMEM_LIMIT),
    )(block)
    return res if with_sum else res[0]


def _core_swap(arrs):
    n = len(arrs)

    def body(*refs):
        ins, outs = refs[:n], refs[n:2 * n]
        send_sems, recv_sems = refs[2 * n:]
        x, y, c = _place()
        cps = []
        for j in range(n):
            cp = pltpu.make_async_remote_copy(
                src_ref=ins[j], dst_ref=outs[j], send_sem=send_sems.at[j], recv_sem=recv_sems.at[j],
                device_id=(x, y, 1 - c), device_id_type=MESH)
            cp.start()
            cps.append(cp)
        for cp in cps:
            cp.wait()

    return _pcall(
        body, name="core_swap",
        in_specs=[_ANY] * n, out_specs=[_ANY] * n,
        out_shape=[jax.ShapeDtypeStruct(a.shape, a.dtype) for a in arrs],
        scratch_shapes=[pltpu.SemaphoreType.DMA((n,)), pltpu.SemaphoreType.DMA((n,))],
    )(*arrs)


def _cols_to_chips(w):
    *lead, a, nb = w.shape
    w = w.reshape(*lead, a, N_CHIPS, nb // N_CHIPS)
    return jnp.moveaxis(w, -2, 0)


def _chips_to_cols(g):
    g = jnp.moveaxis(g, 0, -2)
    *lead, a, k, b = g.shape
    return g.reshape(*lead, a, k * b)


def _my_cols(full, chip):
    w = full.shape[-1] // N_CHIPS
    return lax.dynamic_slice_in_dim(full, chip * w, w, axis=full.ndim - 1)


def _pad_rows(a, rows):
    return jnp.pad(a, ((0, rows - a.shape[0]), (0, 0)))


def _to_lanes(a):
    flat = a.reshape(-1)
    n = -(-flat.shape[0] // (8 * LANES)) * (8 * LANES)
    return jnp.pad(flat, (0, n - flat.shape[0])).reshape(-1, LANES)


class _Packer:
    def __init__(self):
        self.items = []
        self.rows = 0

    def add(self, name, a):
        lanes = _to_lanes(a)
        self.items.append((name, self.rows, a.shape, lanes))
        self.rows += lanes.shape[0]

    def pack(self):
        total = -(-self.rows // 8) * 8
        return _pad_rows(jnp.concatenate([it[3] for it in self.items], axis=0), total)

    def unpack(self, buf):
        out = {}
        for name, row, shape, lanes in self.items:
            size = 1
            for s in shape:
                size *= s
            out[name] = buf[row:row + lanes.shape[0]].reshape(-1)[:size].reshape(shape)
        return out


TM_SEQ = 512
TM_FFN = 256


LAYER_KEYS = ("in", "out", "gate", "up", "down")
COL_KEYS = ("in", "gate", "up")


def _layer_big_names(layer):
    i = layer // 2
    mix = (("ab_w_in", i), ("ab_w_out", i)) if layer % 2 == 0 else (("cf_w_pw1", i), ("cf_w_pw2", i))
    return dict(zip(LAYER_KEYS, mix + (("ffn_w_gate", layer), ("ffn_w_up", layer), ("ffn_w_down", layer))))


def _unpack_weight(key, g):
    g = g.reshape(N_CHIPS, -1, g.shape[-1])
    return _chips_to_cols(g) if key in COL_KEYS else g.reshape(-1, g.shape[-1])


def _chunk_grad(key, dw):
    parts = _cols_to_chips(dw) if key in COL_KEYS else dw.reshape(N_CHIPS, -1, dw.shape[-1])
    return parts.astype(BF16)


def _local_step(x, target, mods, p, shards):
    t_total, d = x.shape
    depth = mods.shape[0]
    tm = min(TM_SEQ, t_total)
    tmf = min(TM_FFN, t_total)
    saved, weights = [], []
    xin = x
    lw, nxt_lw = {}, {}

    def gather(layer, *keys):
        if layer >= depth:
            return None
        return _Gather([shards[layer][k].reshape(2, -1, shards[layer][k].shape[-1]) for k in keys])

    def keep(into, keys, arrs):
        for k, g in zip(keys, arrs):
            into[k] = _unpack_weight(k, g)

    keep(lw, ("in",), _standalone(gather(0, "in")))
    for layer in range(depth):
        i = layer // 2
        first = layer == 0
        sh1, sc1, g1, sh2, sc2, g2 = (mods[layer, k:k + 1] for k in range(6))
        vec_in = jnp.concatenate([p["norm_mix_g"][layer:layer + 1], sh1, sc1], axis=0)
        bias = None if layer % 2 == 0 else p["cf_b_pw1"][i:i + 1]
        (u,), arrived = _in_proj(xin, vec_in, lw["in"], bias, tm, exch=gather(0, "out", "gate") if first else None)
        if first:
            keep(lw, ("out", "gate"), arrived)
        mix_exch = gather(0, "up", "down") if first else None
        if layer % 2 == 0:
            (y, x2), arrived = _ab_fwd(u, xin, g1, p["ab_conv"][i], p["ab_w_pool"][i].astype(BF16),
                                       p["ab_pool_scale"][i:i + 1], lw["out"], tm, exch=mix_exch)
            zc = None
        else:
            vec_cf = jnp.concatenate([g1, p["cf_b_dw"][i:i + 1], p["cf_ln_g"][i:i + 1], p["cf_ln_b"][i:i + 1],
                                      p["cf_b_pw2"][i:i + 1]], axis=0)
            (zc, y, x2), arrived = _cf_fwd(u, xin, vec_cf, _pad_rows(p["cf_w_dw"][i], 32), lw["out"], tm,
                                           exch=mix_exch)
        if first:
            keep(lw, ("up", "down"), arrived)
        vec_ffn = jnp.concatenate([p["norm_ffn_g"][layer:layer + 1], sh2, sc2, g2], axis=0)
        (a, b, fout, x3), arrived = _ffn_fwd(x2, vec_ffn, lw["gate"], lw["up"], lw["down"], tmf,
                                             exch=gather(layer + 1, *LAYER_KEYS))
        nxt_lw = {}
        keep(nxt_lw, LAYER_KEYS, arrived)
        saved.append((xin, u, y, zc, x2, a, b, fout))
        weights.append(lw)
        lw = nxt_lw
        xin = x3

    (dx, fin), _ = _final_fwd_bwd(xin, target, p["final_norm_g"].reshape(1, d), tm)
    grads = {"final_norm_g": fin[0], "loss": fin[1, 0:1]}
    per_layer = {k: [None] * depth for k in ("norm_mix_g", "norm_ffn_g")}
    half = {k: [None] * (depth // 2) for k in (
        "ab_conv", "ab_w_pool", "ab_pool_scale", "cf_b_pw1", "cf_w_dw", "cf_b_dw", "cf_ln_g", "cf_ln_b", "cf_b_pw2")}
    dmods = [None] * depth
    received = {}
    pending = None
    for layer in reversed(range(depth)):
        i = layer // 2
        lw = weights[layer]
        xin, u, y, zc, x2, a, b, fout = saved[layer]
        sh1, sc1, g1, sh2, sc2, g2 = (mods[layer, k:k + 1] for k in range(6))
        above = _Scatter([pending["out"], pending["in"]]) if pending is not None else None
        (da, db, dwd, acc_d), arrived = _ffn_bwd_down(dx, fout, a, b, g2, lw["down"], tmf, exch=above)
        if pending is not None:
            received[(layer + 1, "out")], received[(layer + 1, "in")] = arrived
        vec_ffn = jnp.concatenate([p["norm_ffn_g"][layer:layer + 1], sh2, sc2], axis=0)
        (dx2, dwg, dwu, acc_u), arrived = _ffn_bwd_up(da, db, x2, dx, vec_ffn, lw["gate"], lw["up"], tmf,
                                                      exch=_Scatter([_chunk_grad("down", dwd)]))
        received[(layer, "down")] = arrived[0]
        per_layer["norm_ffn_g"][layer] = acc_u[2]
        vec_in = jnp.concatenate([p["norm_mix_g"][layer:layer + 1], sh1, sc1], axis=0)
        send_gate = _Scatter([_chunk_grad("gate", dwg)])
        send_up = _Scatter([_chunk_grad("up", dwu)])
        if layer % 2 == 0:
            (dpre, dwout, dwpool, acc_o), arrived = _ab_bwd_out(
                dx2, y, u, g1, p["ab_conv"][i], p["ab_w_pool"][i].astype(BF16), p["ab_pool_scale"][i:i + 1],
                lw["out"], tm, exch=send_gate)
            received[(layer, "gate")] = arrived[0]
            (dx, dwin, dconv, acc_i), arrived = _ab_bwd_in(dpre, u, xin, dx2, vec_in, p["ab_conv"][i], lw["in"], tm,
                                                           exch=send_up)
            half["ab_w_pool"][i] = dwpool
            half["ab_pool_scale"][i] = acc_o[1, 0:d // 2]
            half["ab_conv"][i] = dconv[0:3]
        else:
            vec_cf = jnp.concatenate([g1, p["cf_ln_g"][i:i + 1], p["cf_ln_b"][i:i + 1]], axis=0)
            (dzc, dwout, acc_o), arrived = _cf_bwd_out(dx2, y, zc, vec_cf, lw["out"], tm, exch=send_gate)
            received[(layer, "gate")] = arrived[0]
            (dx, dwin, dwdw, db1, acc_i), arrived = _cf_bwd_in(
                dzc, u, xin, dx2, vec_in, _pad_rows(p["cf_w_dw"][i], 32), lw["in"], tm, exch=send_up)
            half["cf_b_pw2"][i] = acc_o[1]
            half["cf_ln_g"][i] = acc_o[2]
            half["cf_ln_b"][i] = acc_o[3]
            half["cf_b_dw"][i] = acc_o[4]
            half["cf_w_dw"][i] = dwdw[0:CONF_KERNEL]
            half["cf_b_pw1"][i] = db1[0]
        received[(layer, "up")] = arrived[0]
        per_layer["norm_mix_g"][layer] = acc_i[2]
        dmods[layer] = jnp.stack([acc_i[0], acc_i[1], acc_o[0], acc_u[0], acc_u[1], acc_d[0]], axis=0)
        pending = {"out": _chunk_grad("out", dwout), "in": _chunk_grad("in", dwin)}
    received[(0, "out")], received[(0, "in")] = _standalone(_Scatter([pending["out"], pending["in"]]))
    for k, v in {**per_layer, **half}.items():
        grads[k] = jnp.stack(v, axis=0)
    return dx, grads, jnp.stack(dmods, axis=0), received


SMALL_COLS = ("ab_conv", "cf_b_pw1", "cf_w_dw", "cf_b_dw", "cf_ln_g", "cf_ln_b", "cf_b_pw2")
SMALL_REPL = ("norm_mix_g", "norm_ffn_g", "ab_w_pool", "ab_pool_scale", "final_norm_g")
WEIGHTS = ("norm_mix_g", "norm_ffn_g", "w_mod", "b_mod", "ab_w_in", "ab_conv", "ab_w_pool", "ab_pool_scale",
           "ab_w_out", "cf_w_pw1", "cf_b_pw1", "cf_w_dw", "cf_b_dw", "cf_ln_g", "cf_ln_b", "cf_w_pw2",
           "cf_b_pw2", "ffn_w_gate", "ffn_w_up", "ffn_w_down", "final_norm_g")


def kernel(x, c, norm_mix_g, norm_ffn_g, w_mod, b_mod, ab_w_in, ab_conv, ab_w_pool, ab_pool_scale, ab_w_out, cf_w_pw1, cf_b_pw1, cf_w_dw, cf_b_dw, cf_ln_g, cf_ln_b, cf_w_pw2, cf_b_pw2, ffn_w_gate, ffn_w_up, ffn_w_down, final_norm_g, loss_target, m_norm_mix_g, m_norm_ffn_g, m_w_mod, m_b_mod, m_ab_w_in, m_ab_conv, m_ab_w_pool, m_ab_pool_scale, m_ab_w_out, m_cf_w_pw1, m_cf_b_pw1, m_cf_w_dw, m_cf_b_dw, m_cf_ln_g, m_cf_ln_b, m_cf_w_pw2, m_cf_b_pw2, m_ffn_w_gate, m_ffn_w_up, m_ffn_w_down, m_final_norm_g, v_norm_mix_g, v_norm_ffn_g, v_w_mod, v_b_mod, v_ab_w_in, v_ab_conv, v_ab_w_pool, v_ab_pool_scale, v_ab_w_out, v_cf_w_pw1, v_cf_b_pw1, v_cf_w_dw, v_cf_b_dw, v_cf_ln_g, v_cf_ln_b, v_cf_w_pw2, v_cf_b_pw2, v_ffn_w_gate, v_ffn_w_up, v_ffn_w_down, v_final_norm_g):
    w = dict(norm_mix_g=norm_mix_g, norm_ffn_g=norm_ffn_g, w_mod=w_mod, b_mod=b_mod, ab_w_in=ab_w_in,
             ab_conv=ab_conv, ab_w_pool=ab_w_pool, ab_pool_scale=ab_pool_scale, ab_w_out=ab_w_out,
             cf_w_pw1=cf_w_pw1, cf_b_pw1=cf_b_pw1, cf_w_dw=cf_w_dw, cf_b_dw=cf_b_dw, cf_ln_g=cf_ln_g,
             cf_ln_b=cf_ln_b, cf_w_pw2=cf_w_pw2, cf_b_pw2=cf_b_pw2, ffn_w_gate=ffn_w_gate, ffn_w_up=ffn_w_up,
             ffn_w_down=ffn_w_down, final_norm_g=final_norm_g)
    mom = dict(norm_mix_g=m_norm_mix_g, norm_ffn_g=m_norm_ffn_g, w_mod=m_w_mod, b_mod=m_b_mod, ab_w_in=m_ab_w_in,
               ab_conv=m_ab_conv, ab_w_pool=m_ab_w_pool, ab_pool_scale=m_ab_pool_scale, ab_w_out=m_ab_w_out,
               cf_w_pw1=m_cf_w_pw1, cf_b_pw1=m_cf_b_pw1, cf_w_dw=m_cf_w_dw, cf_b_dw=m_cf_b_dw, cf_ln_g=m_cf_ln_g,
               cf_ln_b=m_cf_ln_b, cf_w_pw2=m_cf_w_pw2, cf_b_pw2=m_cf_b_pw2, ffn_w_gate=m_ffn_w_gate,
               ffn_w_up=m_ffn_w_up, ffn_w_down=m_ffn_w_down, final_norm_g=m_final_norm_g)
    var = dict(norm_mix_g=v_norm_mix_g, norm_ffn_g=v_norm_ffn_g, w_mod=v_w_mod, b_mod=v_b_mod, ab_w_in=v_ab_w_in,
               ab_conv=v_ab_conv, ab_w_pool=v_ab_w_pool, ab_pool_scale=v_ab_pool_scale, ab_w_out=v_ab_w_out,
               cf_w_pw1=v_cf_w_pw1, cf_b_pw1=v_cf_b_pw1, cf_w_dw=v_cf_w_dw, cf_b_dw=v_cf_b_dw, cf_ln_g=v_cf_ln_g,
               cf_ln_b=v_cf_ln_b, cf_w_pw2=v_cf_w_pw2, cf_b_pw2=v_cf_b_pw2, ffn_w_gate=v_ffn_w_gate,
               ffn_w_up=v_ffn_w_up, ffn_w_down=v_ffn_w_down, final_norm_g=v_final_norm_g)
    px, py, pc = _place()
    chip = 2 * px + py
    dev = 2 * chip + pc
    depth, d, mod_cols = w_mod.shape
    x = x[0]
    target = loss_target[0]

    small_in = _Packer()
    small_in.add("c", c)
    for name in SMALL_COLS:
        small_in.add(name, w[name])
    gathered = _allgather8(small_in.pack(), with_sum=False).reshape(N_DEV, -1, LANES)
    per_dev = [small_in.unpack(gathered[k]) for k in range(N_DEV)]
    c_all = jnp.concatenate([pd["c"] for pd in per_dev], axis=0)
    params = {name: jnp.concatenate([per_dev[2 * k][name] for k in range(N_CHIPS)], axis=-1)
              for name in SMALL_COLS}
    for name in SMALL_REPL:
        params[name] = w[name]

    mod_part = _mod_fwd(c_all, w_mod, _my_cols(b_mod, chip))
    mod_all = _allgather8(mod_part.reshape(-1, LANES), with_sum=False)
    mod_all = mod_all.reshape(N_CHIPS, 2, depth, N_DEV, mod_cols)[:, 0]
    mod_all = jnp.moveaxis(mod_all, 0, 2).reshape(depth, N_DEV, N_CHIPS * mod_cols)
    mods = lax.dynamic_index_in_dim(mod_all, dev, axis=1, keepdims=False).reshape(depth, 6, d)

    shards = [{k: w[name][idx].astype(BF16) for k, (name, idx) in _layer_big_names(layer).items()}
              for layer in range(depth)]
    grad_x, grads, dmods, received = _local_step(x, target, mods, params, shards)

    small_out = _Packer()
    small_out.add("dmods", dmods)
    for name in ("loss",) + SMALL_REPL + SMALL_COLS:
        small_out.add(name, grads[name])
    parts_all, parts_sum = _allgather8(small_out.pack(), with_sum=True)
    small_sum = small_out.unpack(parts_sum)
    loss = small_sum["loss"][0]
    dmods_all = jnp.stack([small_out.unpack(pa)["dmods"] for pa in parts_all.reshape(N_DEV, -1, LANES)], axis=1)
    dmods_all = dmods_all.reshape(depth, N_DEV, 6 * d)

    g_final = {}
    g_final["w_mod"] = [_mod_bwd(c_all.T, _my_cols(dmods_all, chip))]
    g_final["b_mod"] = [small_sum["dmods"].reshape(depth, 6 * d)]
    for name in SMALL_REPL:
        g_final[name] = [small_sum[name]]
    for name in SMALL_COLS:
        g_final[name] = [_my_cols(small_sum[name], chip)]

    by_name = {}
    for layer in range(depth):
        for k, (name, _) in _layer_big_names(layer).items():
            by_name.setdefault(name, []).append(_sum_parts(received[(layer, k)]))
    big = sorted(by_name)
    mine = [jnp.stack(by_name[name], axis=0) for name in big]
    theirs = _core_swap(mine)
    for name, a, b in zip(big, mine, theirs):
        g_final[name] = [a, b]

    out_g, out_d, out_m, out_v = [], [], [], []
    for name in WEIGHTS:
        g, dlt, mn, vn = _adamw(w[name], g_final[name], mom[name], var[name])
        out_g.append(g)
        out_d.append(dlt)
        out_m.append(mn)
        out_v.append(vn)
    return (loss, grad_x[None], *out_g, *out_d, *out_m, *out_v)
```

```python
import functools

import jax
import jax.numpy as jnp
from jax import lax
from jax.experimental import pallas as pl
from jax.experimental.pallas import tpu as pltpu

F32 = jnp.float32
BF16 = jnp.bfloat16
RMS_EPS = 1e-6
LN_EPS = 1e-5
ADAM_LR = 0.001
ADAM_B1 = 0.9
ADAM_B2 = 0.999
ADAM_EPS = 1e-08
ADAM_WD = 0.01
ADAM_STEP = 10
POOL_WINDOWS = (2, 4, 8, 16)
CONF_KERNEL = 31
N_CHIPS = 4
N_DEV = 8
HALO = 16
CONV_COLS = 256
FFN_CHUNK = 1536
LANES = 1024
VMEM_LIMIT = 56 * 1024 * 1024
EW_BLOCK_ELEMS = 256 * 1024
MESH = pl.DeviceIdType.MESH
HIGHEST = lax.Precision.HIGHEST

_pcall = pl.pallas_call


def _dot(a, b):
    return jnp.dot(a, b, preferred_element_type=F32)


def _dot_tn(a, b):
    return lax.dot_general(a, b, (((0,), (0,)), ((), ())), preferred_element_type=F32)


def _dot_nt(a, b):
    return lax.dot_general(a, b, (((1,), (1,)), ((), ())), preferred_element_type=F32)


def _colsum(v):
    return jnp.sum(v, axis=0, keepdims=True)


def _sigmoid(v):
    return 1.0 / (1.0 + jnp.exp(-v))


def _rows(tm, c):
    return pl.BlockSpec((tm, c), lambda i: (i, 0))


def _full(shape):
    nd = len(shape)
    return pl.BlockSpec(shape, lambda i: (0,) * nd)


_VM = pl.BlockSpec(memory_space=pltpu.VMEM)
_ANY = pl.BlockSpec(memory_space=pl.ANY)


def _halo_specs(tm, c, t_total):
    r = tm // HALO
    last = t_total // HALO - 1
    prev = pl.BlockSpec((HALO, c), lambda i: (jnp.maximum(i * r - 1, 0), 0))
    nxt = pl.BlockSpec((HALO, c), lambda i: (jnp.minimum((i + 1) * r, last), 0))
    return prev, _rows(tm, c), nxt


def _seq_params():
    return pltpu.CompilerParams(dimension_semantics=("arbitrary",), vmem_limit_bytes=VMEM_LIMIT)


def _place():
    return lax.axis_index("x"), lax.axis_index("y"), lax.axis_index("c")


def _peer_chips(x, y):
    return [(1 - x, y), (x, 1 - y), (1 - x, 1 - y)]


class _Gather:
    tag = "gather"

    def __init__(self, arrs):
        self.arrs = list(arrs)

    def out_shapes(self):
        return [jax.ShapeDtypeStruct((N_CHIPS,) + a.shape, a.dtype) for a in self.arrs]

    def sems(self):
        n = len(self.arrs)
        return [pltpu.SemaphoreType.DMA((3 * n,)) for _ in range(4)] + [pltpu.SemaphoreType.DMA((n,))]

    def _copies(self, ins, outs, sems, kinds):
        ici_send, ici_recv, d2d_send, d2d_recv, local_sems = sems
        x, y, c = _place()
        me = 2 * x + y
        found = {kind: [] for kind in kinds}
        for j in range(len(ins)):
            if "local" in kinds:
                found["local"].append(pltpu.make_async_copy(ins[j], outs[j].at[me], local_sems.at[j]))
            for k, (px, py) in enumerate(_peer_chips(x, y)):
                ici = dict(send_sem=ici_send.at[3 * j + k], recv_sem=ici_recv.at[3 * j + k],
                           device_id=(px, py, c), device_id_type=MESH)
                d2d = dict(send_sem=d2d_send.at[3 * j + k], recv_sem=d2d_recv.at[3 * j + k],
                           device_id=(x, y, 1 - c), device_id_type=MESH)
                theirs = outs[j].at[2 * px + py]
                if "send" in kinds:
                    found["send"].append(pltpu.make_async_remote_copy(
                        src_ref=ins[j].at[c], dst_ref=outs[j].at[me, c], **ici))
                if "arrival" in kinds:
                    found["arrival"].append(pltpu.make_async_remote_copy(
                        src_ref=ins[j].at[c], dst_ref=theirs.at[c], **ici))
                if "pass" in kinds:
                    found["pass"].append(pltpu.make_async_remote_copy(
                        src_ref=theirs.at[c], dst_ref=theirs.at[c], **d2d))
                if "passed" in kinds:
                    found["passed"].append(pltpu.make_async_remote_copy(
                        src_ref=theirs.at[c], dst_ref=theirs.at[1 - c], **d2d))
        return found

    def start(self, ins, outs, sems):
        found = self._copies(ins, outs, sems, ("local", "send"))
        for cp in found["local"] + found["send"]:
            cp.start()

    def mid(self, ins, outs, sems):
        found = self._copies(ins, outs, sems, ("arrival", "pass"))
        for arrived, onward in zip(found["arrival"], found["pass"]):
            arrived.wait_recv()
            onward.start()

    def wait(self, ins, outs, sems):
        found = self._copies(ins, outs, sems, ("local", "send", "pass", "passed"))
        for cp in found["passed"]:
            cp.wait_recv()
        for cp in found["send"] + found["pass"]:
            cp.wait_send()
        for cp in found["local"]:
            cp.wait()


class _Scatter:
    tag = "scatter"

    def __init__(self, arrs):
        self.arrs = list(arrs)

    def out_shapes(self):
        return [jax.ShapeDtypeStruct(a.shape, a.dtype) for a in self.arrs]

    def sems(self):
        n = len(self.arrs)
        return [pltpu.SemaphoreType.DMA((3 * n,)), pltpu.SemaphoreType.DMA((3 * n,)), pltpu.SemaphoreType.DMA((n,))]

    def _copies(self, ins, outs, sems, kinds):
        send_sems, recv_sems, local_sems = sems
        x, y, c = _place()
        me = 2 * x + y
        found = {kind: [] for kind in kinds}
        for j in range(len(ins)):
            if "local" in kinds:
                found["local"].append(pltpu.make_async_copy(ins[j].at[me], outs[j].at[me], local_sems.at[j]))
            for k, (px, py) in enumerate(_peer_chips(x, y)):
                sem = dict(send_sem=send_sems.at[3 * j + k], recv_sem=recv_sems.at[3 * j + k],
                           device_id=(px, py, c), device_id_type=MESH)
                if "send" in kinds:
                    found["send"].append(pltpu.make_async_remote_copy(
                        src_ref=ins[j].at[2 * px + py], dst_ref=outs[j].at[me], **sem))
                if "arrival" in kinds:
                    found["arrival"].append(pltpu.make_async_remote_copy(
                        src_ref=ins[j].at[me], dst_ref=outs[j].at[2 * px + py], **sem))
        return found

    def start(self, ins, outs, sems):
        found = self._copies(ins, outs, sems, ("local", "send"))
        for cp in found["local"] + found["send"]:
            cp.start()

    def mid(self, ins, outs, sems):
        pass

    def wait(self, ins, outs, sems):
        found = self._copies(ins, outs, sems, ("local", "send", "arrival"))
        for cp in found["arrival"]:
            cp.wait_recv()
        for cp in found["send"]:
            cp.wait_send()
        for cp in found["local"]:
            cp.wait()


def _standalone(op):
    n = len(op.arrs)

    def body(*refs):
        for phase in (op.start, op.mid, op.wait):
            phase(refs[:n], refs[n:2 * n], refs[2 * n:])

    return _pcall(body, name="chip_" + op.tag, in_specs=[_ANY] * n, out_specs=[_ANY] * n,
                  out_shape=op.out_shapes(), scratch_shapes=op.sems())(*op.arrs)


def _call(body, *, name, nsteps, in_specs, out_specs, out_shape, args, scratch_shapes=(), exch=None):
    if exch is None:
        outs = _pcall(body, name=name, grid=(nsteps,), in_specs=list(in_specs), out_specs=list(out_specs),
                      out_shape=list(out_shape), scratch_shapes=list(scratch_shapes),
                      compiler_params=_seq_params())(*args)
        return list(outs), []
    n, ni, no, ns = len(exch.arrs), len(in_specs), len(out_specs), len(scratch_shapes)

    def hosted(*refs):
        xin = refs[ni:ni + n]
        xout = refs[ni + n + no:ni + 2 * n + no]
        scr = refs[ni + 2 * n + no:]

        @pl.when(pl.program_id(0) == 0)
        def _():
            exch.start(xin, xout, scr[ns:])

        body(*refs[:ni], *refs[ni + n:ni + n + no], *scr[:ns])

        @pl.when(pl.program_id(0) == max(nsteps - 3, 0))
        def _():
            exch.mid(xin, xout, scr[ns:])

        @pl.when(pl.program_id(0) == nsteps - 1)
        def _():
            exch.wait(xin, xout, scr[ns:])

    outs = _pcall(hosted, name=name + "_" + exch.tag, grid=(nsteps,),
                  in_specs=[*in_specs, *[_ANY] * n], out_specs=[*out_specs, *[_ANY] * n],
                  out_shape=[*out_shape, *exch.out_shapes()],
                  scratch_shapes=[*scratch_shapes, *exch.sems()],
                  compiler_params=_seq_params())(*args, *exch.arrs)
    return list(outs[:no]), list(outs[no:])


def _rms(x):
    r = lax.rsqrt(jnp.mean(x * x, axis=-1, keepdims=True) + RMS_EPS)
    return x * r, r


def _norm_mod(x, g, sh, sc):
    xhat, _ = _rms(x)
    return xhat * g * (1.0 + sc) + sh


def _norm_mod_bwd(dh, x, g, sc):
    xhat, r = _rms(x)
    n = xhat * g
    dsh = _colsum(dh)
    dsc = _colsum(dh * n)
    dn = dh * (1.0 + sc)
    dg = _colsum(dn * xhat)
    dxn = dn * g
    dx = r * (dxn - xhat * jnp.mean(dxn * xhat, axis=-1, keepdims=True))
    return dx, dsh, dsc, dg


def _fill_ext(ext_ref, prev, cur, nxt, i, nsteps, tm):
    ext_ref[0:HALO, :] = jnp.where(i > 0, prev, 0.0)
    ext_ref[HALO:HALO + tm, :] = cur
    ext_ref[HALO + tm:HALO + tm + HALO, :] = jnp.where(i < nsteps - 1, nxt, 0.0)


def _shift_scratch(tm):
    return pltpu.VMEM((8, tm + 2 * HALO - 8, CONV_COLS), F32)


def _fill_shifts(sh_ref, ext_ref, lo, hi, tm):
    for b in range(8):
        sh_ref[b] = ext_ref[b:b + tm + 2 * HALO - 8, lo:hi]


def _shifted(sh_ref, offset, tm):
    b = offset % 8
    start = HALO + offset - b
    return sh_ref[b, start:start + tm, :]


def _window_count(t, wdw, t_total):
    left = wdw // 2
    right = wdw - 1 - left
    cnt = jnp.minimum(t + right, t_total - 1) - jnp.maximum(t - left, 0) + 1
    return jnp.maximum(cnt, 1).astype(F32)


def _in_proj(x, vec, w, bias, tm, exch=None):
    t_total, d = x.shape
    n = w.shape[1]
    has_bias = bias is not None

    def body(*refs):
        if has_bias:
            x_ref, vec_ref, w_ref, b_ref, u_ref = refs
        else:
            x_ref, vec_ref, w_ref, u_ref = refs
        h = _norm_mod(x_ref[...], vec_ref[0:1, :], vec_ref[1:2, :], vec_ref[2:3, :])
        u = _dot(h.astype(BF16), w_ref[...])
        if has_bias:
            u = u + b_ref[...]
        u_ref[...] = u.astype(BF16)

    in_specs = [_rows(tm, d), _full(vec.shape), _VM]
    args = [x, vec, w]
    if has_bias:
        in_specs.append(_full(bias.shape))
        args.append(bias)
    return _call(
        body, name="in_proj_bias" if has_bias else "in_proj", nsteps=t_total // tm,
        in_specs=in_specs, out_specs=[_rows(tm, n)], out_shape=[jax.ShapeDtypeStruct((t_total, n), BF16)],
        args=args, exch=exch)


def _ab_core(up_ref, uc_ref, un_ref, conv_ref, wpool_ref, q_ext, p_ext, i, nsteps, tm, t_total):
    da = uc_ref.shape[1] // 4

    def cols(ref, k):
        return ref[:, k * da:(k + 1) * da].astype(F32)

    _fill_ext(q_ext, cols(up_ref, 1) * cols(up_ref, 2), cols(uc_ref, 1) * cols(uc_ref, 2),
              cols(un_ref, 1) * cols(un_ref, 2), i, nsteps, tm)
    _fill_ext(p_ext, cols(up_ref, 3), cols(uc_ref, 3), cols(un_ref, 3), i, nsteps, tm)
    bg = cols(uc_ref, 0)
    cq = (conv_ref[0:1, :] * q_ext[HALO - 1:HALO - 1 + tm, :] + conv_ref[1:2, :] * q_ext[HALO:HALO + tm, :]
          + conv_ref[2:3, :] * q_ext[HALO + 1:HALO + 1 + tm, :])
    t = i * tm + lax.broadcasted_iota(jnp.int32, (tm, 1), 0)
    gw = da // len(POOL_WINDOWS)
    pooled, ybpre = [], []
    for g, wdw in enumerate(POOL_WINDOWS):
        left = wdw // 2
        right = wdw - 1 - left
        lo, hi = g * gw, (g + 1) * gw
        s = p_ext[HALO - left:HALO - left + tm, lo:hi]
        for o in range(-left + 1, right + 1):
            s = s + p_ext[HALO + o:HALO + o + tm, lo:hi]
        pg = s / _window_count(t, wdw, t_total) - p_ext[HALO:HALO + tm, lo:hi]
        pooled.append(pg.astype(BF16))
        ybpre.append(_dot(pooled[-1], wpool_ref[g]))
    return bg, cq, pooled, jnp.concatenate(ybpre, axis=1)


def _ab_fwd(u, x, vec, conv, wpool, scale, wout, tm, exch=None):
    t_total, d = x.shape
    nu = u.shape[1]
    da = nu // 4
    nsteps = t_total // tm

    def body(up_ref, uc_ref, un_ref, x_ref, vec_ref, conv_ref, wpool_ref, scale_ref, wout_ref,
             y_ref, x2_ref, q_ext, p_ext):
        i = pl.program_id(0)
        bg, cq, _, ybpre = _ab_core(up_ref, uc_ref, un_ref, conv_ref, wpool_ref, q_ext, p_ext,
                                    i, nsteps, tm, t_total)
        cat = jnp.concatenate([bg * cq, ybpre * scale_ref[...]], axis=1).astype(BF16)
        y = _dot(cat, wout_ref[...])
        y_ref[...] = y.astype(BF16)
        x2_ref[...] = x_ref[...] + vec_ref[0:1, :] * y

    return _call(
        body, name="ab_fwd", nsteps=nsteps,
        in_specs=[*_halo_specs(tm, nu, t_total), _rows(tm, d), _full(vec.shape), _full(conv.shape),
                  _full(wpool.shape), _full(scale.shape), _VM],
        out_specs=[_rows(tm, d), _rows(tm, d)],
        out_shape=[jax.ShapeDtypeStruct((t_total, d), BF16), jax.ShapeDtypeStruct((t_total, d), F32)],
        scratch_shapes=[pltpu.VMEM((tm + 2 * HALO, da), F32), pltpu.VMEM((tm + 2 * HALO, da), F32)],
        args=(u, u, u, x, vec, conv, wpool, scale, wout), exch=exch)


def _glu_ext(up_ref, uc_ref, un_ref, z_ext, i, nsteps, tm):
    dz = uc_ref.shape[1] // 2

    def glu(ref):
        return ref[:, 0:dz].astype(F32) * _sigmoid(ref[:, dz:2 * dz].astype(F32))

    _fill_ext(z_ext, glu(up_ref), glu(uc_ref), glu(un_ref), i, nsteps, tm)


def _layer_norm_stats(zc):
    mu = jnp.mean(zc, axis=-1, keepdims=True)
    dlt = zc - mu
    rstd = lax.rsqrt(jnp.mean(dlt * dlt, axis=-1, keepdims=True) + LN_EPS)
    return dlt * rstd, rstd


def _cf_fwd(u, x, vec, wdw, wpw2, tm, exch=None):
    t_total, d = x.shape
    nu = u.shape[1]
    nsteps = t_total // tm
    left = (CONF_KERNEL - 1) // 2

    def body(up_ref, uc_ref, un_ref, x_ref, vec_ref, wdw_ref, wpw2_ref, zc_ref, y_ref, x2_ref, z_ext, sh_ref,
             zc_buf):
        i = pl.program_id(0)
        _glu_ext(up_ref, uc_ref, un_ref, z_ext, i, nsteps, tm)
        for lo in range(0, d, CONV_COLS):
            hi = lo + CONV_COLS
            _fill_shifts(sh_ref, z_ext, lo, hi, tm)
            acc = wdw_ref[0:1, lo:hi] * _shifted(sh_ref, -left, tm)
            for k in range(1, CONF_KERNEL):
                acc = acc + wdw_ref[k:k + 1, lo:hi] * _shifted(sh_ref, k - left, tm)
            zc_buf[:, lo:hi] = acc
        zc = zc_buf[...] + vec_ref[1:2, :]
        zc_ref[...] = zc.astype(BF16)
        zn, _ = _layer_norm_stats(zc)
        zl = zn * vec_ref[2:3, :] + vec_ref[3:4, :]
        zs = zl * _sigmoid(zl)
        y = _dot(zs.astype(BF16), wpw2_ref[...]) + vec_ref[4:5, :]
        y_ref[...] = y.astype(BF16)
        x2_ref[...] = x_ref[...] + vec_ref[0:1, :] * y

    return _call(
        body, name="cf_fwd", nsteps=nsteps,
        in_specs=[*_halo_specs(tm, nu, t_total), _rows(tm, d), _full(vec.shape), _full(wdw.shape), _VM],
        out_specs=[_rows(tm, d), _rows(tm, d), _rows(tm, d)],
        out_shape=[jax.ShapeDtypeStruct((t_total, d), BF16), jax.ShapeDtypeStruct((t_total, d), BF16),
                   jax.ShapeDtypeStruct((t_total, d), F32)],
        scratch_shapes=[pltpu.VMEM((tm + 2 * HALO, d), F32), _shift_scratch(tm), pltpu.VMEM((tm, d), F32)],
        args=(u, u, u, x, vec, wdw, wpw2), exch=exch)


def _ffn_chunks(f):
    return [(lo, min(lo + FFN_CHUNK, f)) for lo in range(0, f, FFN_CHUNK)]


def _ffn_fwd(x2, vec, wg, wu, wd, tm, exch=None):
    t_total, d = x2.shape
    f = wg.shape[1]

    def body(x_ref, vec_ref, wg_ref, wu_ref, wd_ref, a_ref, b_ref, f_ref, x3_ref):
        xv = x_ref[...]
        h = _norm_mod(xv, vec_ref[0:1, :], vec_ref[1:2, :], vec_ref[2:3, :]).astype(BF16)
        y = None
        for lo, hi in _ffn_chunks(f):
            a = _dot(h, wg_ref[:, lo:hi])
            b = _dot(h, wu_ref[:, lo:hi])
            a_ref[:, lo:hi] = a.astype(BF16)
            b_ref[:, lo:hi] = b.astype(BF16)
            s = (a * _sigmoid(a) * b).astype(BF16)
            part = _dot(s, wd_ref[lo:hi, :])
            y = part if y is None else y + part
        f_ref[...] = y.astype(BF16)
        x3_ref[...] = xv + vec_ref[3:4, :] * y

    return _call(
        body, name="ffn_fwd", nsteps=t_total // tm,
        in_specs=[_rows(tm, d), _full(vec.shape), _VM, _VM, _VM],
        out_specs=[_rows(tm, f), _rows(tm, f), _rows(tm, d), _rows(tm, d)],
        out_shape=[jax.ShapeDtypeStruct((t_total, f), BF16), jax.ShapeDtypeStruct((t_total, f), BF16),
                   jax.ShapeDtypeStruct((t_total, d), BF16), jax.ShapeDtypeStruct((t_total, d), F32)],
        args=(x2, vec, wg, wu, wd), exch=exch)


def _final_fwd_bwd(x, target, vec, tm):
    t_total, d = x.shape

    def body(x_ref, t_ref, vec_ref, dx_ref, acc_ref):
        @pl.when(pl.program_id(0) == 0)
        def _():
            acc_ref[...] = jnp.zeros_like(acc_ref)

        g = vec_ref[0:1, :]
        xhat, r = _rms(x_ref[...])
        e = xhat * g - t_ref[...]
        acc_ref[1:2, :] += jnp.zeros((1, d), F32) + 0.5 * jnp.sum(jnp.mean(e * e, axis=-1, keepdims=True))
        dout = e * (1.0 / d)
        acc_ref[0:1, :] += _colsum(dout * xhat)
        dxn = dout * g
        dx_ref[...] = r * (dxn - xhat * jnp.mean(dxn * xhat, axis=-1, keepdims=True))

    return _call(
        body, name="final_fwd_bwd", nsteps=t_total // tm,
        in_specs=[_rows(tm, d), _rows(tm, d), _full(vec.shape)],
        out_specs=[_rows(tm, d), _VM],
        out_shape=[jax.ShapeDtypeStruct((t_total, d), F32), jax.ShapeDtypeStruct((8, d), F32)],
        args=(x, target, vec))


def _zero_at_start(*refs):
    @pl.when(pl.program_id(0) == 0)
    def _():
        for ref in refs:
            ref[...] = jnp.zeros_like(ref)


def _ffn_bwd_down(dx3, fout, a, b, vec, wd, tm, exch=None):
    t_total, d = dx3.shape
    f = a.shape[1]

    def body(dx_ref, f_ref, a_ref, b_ref, vec_ref, wd_ref, da_ref, db_ref, dwd_ref, acc_ref):
        _zero_at_start(dwd_ref, acc_ref)
        dx = dx_ref[...]
        acc_ref[0:1, :] += _colsum(dx * f_ref[...].astype(F32))
        dy = (dx * vec_ref[0:1, :]).astype(BF16)
        for lo, hi in _ffn_chunks(f):
            av = a_ref[:, lo:hi].astype(F32)
            bv = b_ref[:, lo:hi].astype(F32)
            sg = _sigmoid(av)
            silu = av * sg
            dwd_ref[lo:hi, :] += _dot_tn((silu * bv).astype(BF16), dy)
            ds = _dot_nt(dy, wd_ref[lo:hi, :])
            da_ref[:, lo:hi] = (ds * bv * (sg * (1.0 + av * (1.0 - sg)))).astype(BF16)
            db_ref[:, lo:hi] = (ds * silu).astype(BF16)

    return _call(
        body, name="ffn_bwd_down", nsteps=t_total // tm,
        in_specs=[_rows(tm, d), _rows(tm, d), _rows(tm, f), _rows(tm, f), _full(vec.shape), _VM],
        out_specs=[_rows(tm, f), _rows(tm, f), _VM, _VM],
        out_shape=[jax.ShapeDtypeStruct((t_total, f), BF16), jax.ShapeDtypeStruct((t_total, f), BF16),
                   jax.ShapeDtypeStruct(wd.shape, F32), jax.ShapeDtypeStruct((8, d), F32)],
        args=(dx3, fout, a, b, vec, wd), exch=exch)


def _ffn_bwd_up(da, db, x2, dx3, vec, wg, wu, tm, exch=None):
    t_total, d = x2.shape
    f = da.shape[1]

    def body(da_ref, db_ref, x_ref, dx_ref, vec_ref, wg_ref, wu_ref, dx2_ref, dwg_ref, dwu_ref, acc_ref):
        _zero_at_start(dwg_ref, dwu_ref, acc_ref)
        xv = x_ref[...]
        g, sh, sc = vec_ref[0:1, :], vec_ref[1:2, :], vec_ref[2:3, :]
        h = _norm_mod(xv, g, sh, sc).astype(BF16)
        dav = da_ref[...]
        dbv = db_ref[...]
        dwg_ref[...] += _dot_tn(h, dav)
        dwu_ref[...] += _dot_tn(h, dbv)
        dh = _dot_nt(dav, wg_ref[...]) + _dot_nt(dbv, wu_ref[...])
        dxn, dsh, dsc, dg = _norm_mod_bwd(dh, xv, g, sc)
        acc_ref[0:1, :] += dsh
        acc_ref[1:2, :] += dsc
        acc_ref[2:3, :] += dg
        dx2_ref[...] = dx_ref[...] + dxn

    return _call(
        body, name="ffn_bwd_up", nsteps=t_total // tm,
        in_specs=[_rows(tm, f), _rows(tm, f), _rows(tm, d), _rows(tm, d), _full(vec.shape), _VM, _VM],
        out_specs=[_rows(tm, d), _VM, _VM, _VM],
        out_shape=[jax.ShapeDtypeStruct((t_total, d), F32), jax.ShapeDtypeStruct(wg.shape, F32),
                   jax.ShapeDtypeStruct(wu.shape, F32), jax.ShapeDtypeStruct((8, d), F32)],
        args=(da, db, x2, dx3, vec, wg, wu), exch=exch)


def _ab_bwd_out(dx, y, u, vec, conv, wpool, scale, wout, tm, exch=None):
    t_total, d = dx.shape
    nu = u.shape[1]
    da = nu // 4
    gw = da // len(POOL_WINDOWS)
    nsteps = t_total // tm

    def body(dx_ref, y_ref, up_ref, uc_ref, un_ref, vec_ref, conv_ref, wpool_ref, scale_ref, wout_ref,
             dpre_ref, dwout_ref, dwpool_ref, acc_ref, q_ext, p_ext):
        _zero_at_start(dwout_ref, dwpool_ref, acc_ref)
        i = pl.program_id(0)
        dxv = dx_ref[...]
        acc_ref[0:1, :] += _colsum(dxv * y_ref[...].astype(F32))
        dy = (dxv * vec_ref[0:1, :]).astype(BF16)
        bg, cq, pooled, ybpre = _ab_core(up_ref, uc_ref, un_ref, conv_ref, wpool_ref, q_ext, p_ext,
                                         i, nsteps, tm, t_total)
        cat = jnp.concatenate([bg * cq, ybpre * scale_ref[...]], axis=1).astype(BF16)
        dwout_ref[...] += _dot_tn(cat, dy)
        dcat = _dot_nt(dy, wout_ref[...])
        dya = dcat[:, 0:da]
        dyb = dcat[:, da:2 * da]
        acc_ref[1:2, 0:da] += _colsum(dyb * ybpre)
        dybpre = (dyb * scale_ref[...]).astype(BF16)
        dpooled = []
        for g in range(len(POOL_WINDOWS)):
            dg = dybpre[:, g * gw:(g + 1) * gw]
            dwpool_ref[g] += _dot_tn(pooled[g], dg)
            dpooled.append(_dot_nt(dg, wpool_ref[g]))
        dpre_ref[...] = jnp.concatenate([dya * cq, dya * bg] + dpooled, axis=1).astype(BF16)

    return _call(
        body, name="ab_bwd_out", nsteps=nsteps,
        in_specs=[_rows(tm, d), _rows(tm, d), *_halo_specs(tm, nu, t_total), _full(vec.shape),
                  _full(conv.shape), _full(wpool.shape), _full(scale.shape), _VM],
        out_specs=[_rows(tm, 3 * da), _VM, _VM, _VM],
        out_shape=[jax.ShapeDtypeStruct((t_total, 3 * da), BF16), jax.ShapeDtypeStruct(wout.shape, F32),
                   jax.ShapeDtypeStruct(wpool.shape, F32), jax.ShapeDtypeStruct((8, d), F32)],
        scratch_shapes=[pltpu.VMEM((tm + 2 * HALO, da), F32), pltpu.VMEM((tm + 2 * HALO, da), F32)],
        args=(dx, y, u, u, u, vec, conv, wpool, scale, wout), exch=exch)


def _ab_bwd_in(dpre, u, x, dx, vec, conv, win, tm, exch=None):
    t_total, d = x.shape
    nu = u.shape[1]
    da = nu // 4
    gw = da // len(POOL_WINDOWS)
    nsteps = t_total // tm

    def body(dp_ref, dc_ref, dn_ref, up_ref, uc_ref, un_ref, x_ref, dx_ref, vec_ref, conv_ref, win_ref,
             dxin_ref, dwin_ref, dconv_ref, acc_ref, dcq_ext, q_ext, dpl_ext):
        _zero_at_start(dwin_ref, dconv_ref, acc_ref)
        i = pl.program_id(0)

        def ucols(ref, k):
            return ref[:, k * da:(k + 1) * da].astype(F32)

        def dcols(ref, k):
            return ref[:, k * da:(k + 1) * da].astype(F32)

        _fill_ext(dcq_ext, dcols(dp_ref, 1), dcols(dc_ref, 1), dcols(dn_ref, 1), i, nsteps, tm)
        _fill_ext(q_ext, ucols(up_ref, 1) * ucols(up_ref, 2), ucols(uc_ref, 1) * ucols(uc_ref, 2),
                  ucols(un_ref, 1) * ucols(un_ref, 2), i, nsteps, tm)
        _fill_ext(dpl_ext, dcols(dp_ref, 2), dcols(dc_ref, 2), dcols(dn_ref, 2), i, nsteps, tm)
        dq = (conv_ref[0:1, :] * dcq_ext[HALO + 1:HALO + 1 + tm, :] + conv_ref[1:2, :] * dcq_ext[HALO:HALO + tm, :]
              + conv_ref[2:3, :] * dcq_ext[HALO - 1:HALO - 1 + tm, :])
        dcq = dcq_ext[HALO:HALO + tm, :]
        for k in range(3):
            dconv_ref[k:k + 1, :] += _colsum(dcq * q_ext[HALO + k - 1:HALO + k - 1 + tm, :])
        dcg = dq * ucols(uc_ref, 2)
        dv = dq * ucols(uc_ref, 1)
        t_ext = i * tm - HALO + lax.broadcasted_iota(jnp.int32, (tm + 2 * HALO, 1), 0)
        dps = []
        for g, wdw in enumerate(POOL_WINDOWS):
            left = wdw // 2
            right = wdw - 1 - left
            lo, hi = g * gw, (g + 1) * gw
            dpg = dpl_ext[HALO:HALO + tm, lo:hi]
            dpl_ext[:, lo:hi] = dpl_ext[:, lo:hi] / _window_count(t_ext, wdw, t_total)
            s = dpl_ext[HALO - right:HALO - right + tm, lo:hi]
            for o in range(-right + 1, left + 1):
                s = s + dpl_ext[HALO + o:HALO + o + tm, lo:hi]
            dps.append(s - dpg)
        du = jnp.concatenate([dcols(dc_ref, 0), dcg, dv] + dps, axis=1).astype(BF16)
        xv = x_ref[...]
        g, sh, sc = vec_ref[0:1, :], vec_ref[1:2, :], vec_ref[2:3, :]
        h = _norm_mod(xv, g, sh, sc).astype(BF16)
        dwin_ref[...] += _dot_tn(h, du)
        dh = _dot_nt(du, win_ref[...])
        dxn, dsh, dsc, dg = _norm_mod_bwd(dh, xv, g, sc)
        acc_ref[0:1, :] += dsh
        acc_ref[1:2, :] += dsc
        acc_ref[2:3, :] += dg
        dxin_ref[...] = dx_ref[...] + dxn

    ext = pltpu.VMEM((tm + 2 * HALO, da), F32)
    return _call(
        body, name="ab_bwd_in", nsteps=nsteps,
        in_specs=[*_halo_specs(tm, 3 * da, t_total), *_halo_specs(tm, nu, t_total), _rows(tm, d), _rows(tm, d),
                  _full(vec.shape), _full(conv.shape), _VM],
        out_specs=[_rows(tm, d), _VM, _VM, _VM],
        out_shape=[jax.ShapeDtypeStruct((t_total, d), F32), jax.ShapeDtypeStruct(win.shape, F32),
                   jax.ShapeDtypeStruct((8, da), F32), jax.ShapeDtypeStruct((8, d), F32)],
        scratch_shapes=[ext, ext, ext],
        args=(dpre, dpre, dpre, u, u, u, x, dx, vec, conv, win), exch=exch)


def _cf_bwd_out(dx, y, zc, vec, wpw2, tm, exch=None):
    t_total, d = dx.shape

    def body(dx_ref, y_ref, zc_ref, vec_ref, w_ref, dzc_ref, dw_ref, acc_ref):
        _zero_at_start(dw_ref, acc_ref)
        dxv = dx_ref[...]
        acc_ref[0:1, :] += _colsum(dxv * y_ref[...].astype(F32))
        dyf = dxv * vec_ref[0:1, :]
        acc_ref[1:2, :] += _colsum(dyf)
        dy = dyf.astype(BF16)
        zn, rstd = _layer_norm_stats(zc_ref[...].astype(F32))
        lng = vec_ref[1:2, :]
        zl = zn * lng + vec_ref[2:3, :]
        sg = _sigmoid(zl)
        dw_ref[...] += _dot_tn((zl * sg).astype(BF16), dy)
        dzl = _dot_nt(dy, w_ref[...]) * (sg * (1.0 + zl * (1.0 - sg)))
        acc_ref[2:3, :] += _colsum(dzl * zn)
        acc_ref[3:4, :] += _colsum(dzl)
        dzn = dzl * lng
        dzc = rstd * (dzn - jnp.mean(dzn, axis=-1, keepdims=True)
                      - zn * jnp.mean(dzn * zn, axis=-1, keepdims=True))
        acc_ref[4:5, :] += _colsum(dzc)
        dzc_ref[...] = dzc.astype(BF16)

    return _call(
        body, name="cf_bwd_out", nsteps=t_total // tm,
        in_specs=[_rows(tm, d), _rows(tm, d), _rows(tm, d), _full(vec.shape), _VM],
        out_specs=[_rows(tm, d), _VM, _VM],
        out_shape=[jax.ShapeDtypeStruct((t_total, d), BF16), jax.ShapeDtypeStruct(wpw2.shape, F32),
                   jax.ShapeDtypeStruct((8, d), F32)],
        args=(dx, y, zc, vec, wpw2), exch=exch)


def _cf_bwd_in(dzc, u, x, dx, vec, wdw, wpw1, tm, exch=None):
    t_total, d = x.shape
    nu = u.shape[1]
    nsteps = t_total // tm
    left = (CONF_KERNEL - 1) // 2

    def body(dp_ref, dc_ref, dn_ref, up_ref, uc_ref, un_ref, x_ref, dx_ref, vec_ref, wdw_ref, w_ref,
             dxin_ref, dw_ref, dwdw_ref, db1_ref, acc_ref, dzc_ext, z_ext, sh_ref, dz_buf):
        _zero_at_start(dw_ref, dwdw_ref, db1_ref, acc_ref)
        i = pl.program_id(0)
        _fill_ext(dzc_ext, dp_ref[...].astype(F32), dc_ref[...].astype(F32), dn_ref[...].astype(F32),
                  i, nsteps, tm)
        _glu_ext(up_ref, uc_ref, un_ref, z_ext, i, nsteps, tm)
        for lo in range(0, d, CONV_COLS):
            hi = lo + CONV_COLS
            _fill_shifts(sh_ref, dzc_ext, lo, hi, tm)
            acc = wdw_ref[0:1, lo:hi] * _shifted(sh_ref, left, tm)
            for k in range(1, CONF_KERNEL):
                acc = acc + wdw_ref[k:k + 1, lo:hi] * _shifted(sh_ref, left - k, tm)
            dz_buf[:, lo:hi] = acc
            dzc = dzc_ext[HALO:HALO + tm, lo:hi]
            _fill_shifts(sh_ref, z_ext, lo, hi, tm)
            for k in range(CONF_KERNEL):
                dwdw_ref[k:k + 1, lo:hi] += _colsum(dzc * _shifted(sh_ref, k - left, tm))
        dz = dz_buf[...]
        av = uc_ref[:, 0:d].astype(F32)
        sg = _sigmoid(uc_ref[:, d:2 * d].astype(F32))
        duf = jnp.concatenate([dz * sg, dz * av * sg * (1.0 - sg)], axis=1)
        db1_ref[0:1, :] += _colsum(duf)
        du = duf.astype(BF16)
        xv = x_ref[...]
        g, sh, sc = vec_ref[0:1, :], vec_ref[1:2, :], vec_ref[2:3, :]
        h = _norm_mod(xv, g, sh, sc).astype(BF16)
        dw_ref[...] += _dot_tn(h, du)
        dh = _dot_nt(du, w_ref[...])
        dxn, dsh, dsc, dg = _norm_mod_bwd(dh, xv, g, sc)
        acc_ref[0:1, :] += dsh
        acc_ref[1:2, :] += dsc
        acc_ref[2:3, :] += dg
        dxin_ref[...] = dx_ref[...] + dxn

    ext = pltpu.VMEM((tm + 2 * HALO, d), F32)
    return _call(
        body, name="cf_bwd_in", nsteps=nsteps,
        in_specs=[*_halo_specs(tm, d, t_total), *_halo_specs(tm, nu, t_total), _rows(tm, d), _rows(tm, d),
                  _full(vec.shape), _full(wdw.shape), _VM],
        out_specs=[_rows(tm, d), _VM, _VM, _VM, _VM],
        out_shape=[jax.ShapeDtypeStruct((t_total, d), F32), jax.ShapeDtypeStruct(wpw1.shape, F32),
                   jax.ShapeDtypeStruct((32, d), F32), jax.ShapeDtypeStruct((8, nu), F32),
                   jax.ShapeDtypeStruct((8, d), F32)],
        scratch_shapes=[ext, ext, _shift_scratch(tm), pltpu.VMEM((tm, d), F32)],
        args=(dzc, dzc, dzc, u, u, u, x, dx, vec, wdw, wpw1), exch=exch)


def _mod_fwd(c_all, w_mod, b_cols):
    nl, d, ncol = w_mod.shape
    nb = c_all.shape[0]

    def body(c_ref, w_ref, b_ref, o_ref):
        cv = c_ref[...]
        ca = cv * _sigmoid(cv)
        o_ref[0] = jnp.dot(ca, w_ref[0], preferred_element_type=F32, precision=HIGHEST) + b_ref[0]

    return _pcall(
        body, name="mod_fwd", grid=(nl,),
        in_specs=[_full(c_all.shape), pl.BlockSpec((1, d, ncol), lambda l: (l, 0, 0)),
                  pl.BlockSpec((1, 1, ncol), lambda l: (l, 0, 0))],
        out_specs=pl.BlockSpec((1, nb, ncol), lambda l: (l, 0, 0)),
        out_shape=jax.ShapeDtypeStruct((nl, nb, ncol), F32),
        compiler_params=_seq_params(),
    )(c_all, w_mod, b_cols.reshape(nl, 1, ncol))


def _mod_bwd(c_all_t, dmod_cols):
    d, nb = c_all_t.shape
    nl, _, ncol = dmod_cols.shape

    def body(c_ref, dm_ref, o_ref):
        cv = c_ref[...]
        ca = cv * _sigmoid(cv)
        o_ref[0] = jnp.dot(ca, dm_ref[0], preferred_element_type=F32, precision=HIGHEST)

    return _pcall(
        body, name="mod_bwd", grid=(nl,),
        in_specs=[_full(c_all_t.shape), pl.BlockSpec((1, nb, ncol), lambda l: (l, 0, 0))],
        out_specs=pl.BlockSpec((1, d, ncol), lambda l: (l, 0, 0)),
        out_shape=jax.ShapeDtypeStruct((nl, d, ncol), F32),
        compiler_params=_seq_params(),
    )(c_all_t, dmod_cols)


def _row_block(r, c):
    if r * c <= EW_BLOCK_ELEMS:
        return r
    best = None
    for br in range(8, r, 8):
        if r % br == 0 and br * c <= EW_BLOCK_ELEMS:
            best = br
    assert best is not None, (r, c)
    return best


def _as2d(a):
    return a.reshape(-1, a.shape[-1])


def _sum_parts(parts):
    k = parts.shape[0]
    p3 = parts.reshape(k, -1, parts.shape[-1])
    _, r, c = p3.shape
    br = _row_block(r, c)

    def body(p_ref, o_ref):
        acc = p_ref[0].astype(F32)
        for j in range(1, k):
            acc = acc + p_ref[j].astype(F32)
        o_ref[...] = acc

    out = _pcall(
        body, name="sum_parts", grid=(r // br,),
        in_specs=[pl.BlockSpec((k, br, c), lambda i: (0, i, 0))],
        out_specs=pl.BlockSpec((br, c), lambda i: (i, 0)),
        out_shape=jax.ShapeDtypeStruct((r, c), F32),
        compiler_params=_seq_params(),
    )(p3)
    return out.reshape(parts.shape[1:])


def _adamw(w, gparts, m, v):
    shape = w.shape
    w2, m2, v2 = _as2d(w), _as2d(m), _as2d(v)
    g2 = [_as2d(g) for g in gparts]
    r, c = w2.shape
    br = _row_block(r, c)
    ng = len(g2)

    def body(*refs):
        w_ref, m_ref, v_ref = refs[0:3]
        g_refs = refs[3:3 + ng]
        go_ref, d_ref, mo_ref, vo_ref = refs[3 + ng:]
        g = g_refs[0][...]
        for gr in g_refs[1:]:
            g = g + gr[...]
        mn = ADAM_B1 * m_ref[...] + (1.0 - ADAM_B1) * g
        vn = ADAM_B2 * v_ref[...] + (1.0 - ADAM_B2) * (g * g)
        m_hat = mn / (1.0 - ADAM_B1 ** ADAM_STEP)
        v_hat = vn / (1.0 - ADAM_B2 ** ADAM_STEP)
        go_ref[...] = g
        d_ref[...] = -ADAM_LR * (m_hat / (jnp.sqrt(v_hat) + ADAM_EPS) + ADAM_WD * w_ref[...])
        mo_ref[...] = mn
        vo_ref[...] = vn

    spec = pl.BlockSpec((br, c), lambda i: (i, 0))
    outs = _pcall(
        body, name="adamw", grid=(r // br,),
        in_specs=[spec] * (3 + ng), out_specs=[spec] * 4,
        out_shape=[jax.ShapeDtypeStruct((r, c), F32)] * 4,
        compiler_params=_seq_params(),
    )(w2, m2, v2, *g2)
    return tuple(o.reshape(shape) for o in outs)


def _allgather8(block, with_sum):
    m_per, n = block.shape

    def body(x_ref, out_ref, *rest):
        if with_sum:
            sum_ref, send_sems, recv_sems, local_sem = rest
        else:
            send_sems, recv_sems, local_sem = rest
        x, y, c = _place()
        me, sibling = (x, y, c), (x, y, 1 - c)
        chips = [(1 - x, y), (x, 1 - y), (1 - x, 1 - y)]

        def rows(px, py, pc):
            return out_ref.at[pl.ds((4 * px + 2 * py + pc) * m_per, m_per), :]

        def copy(k, blk, to, src=None):
            return pltpu.make_async_remote_copy(
                src_ref=rows(*blk) if src is None else src, dst_ref=rows(*blk),
                send_sem=send_sems.at[k], recv_sem=recv_sems.at[k], device_id=to, device_id_type=MESH)

        mine = pltpu.make_async_copy(x_ref, rows(*me), local_sem)
        mine.start()
        first = [copy(0, me, sibling, src=x_ref)]
        first += [copy(1 + j, me, (*chip, c), src=x_ref) for j, chip in enumerate(chips)]
        for cp in first:
            cp.start()
        passed = [copy(4 + j, (*chip, c), sibling) for j, chip in enumerate(chips)]
        for j, chip in enumerate(chips):
            copy(1 + j, (*chip, c), me).wait_recv()
            passed[j].start()
        copy(0, sibling, me).wait_recv()
        for j, chip in enumerate(chips):
            copy(4 + j, (*chip, 1 - c), me).wait_recv()
        for cp in first + passed:
            cp.wait_send()
        mine.wait()
        if with_sum:
            acc = out_ref[0:m_per, :]
            for k in range(1, N_DEV):
                acc = acc + out_ref[k * m_per:(k + 1) * m_per, :]
            sum_ref[...] = acc

    out_shape = [jax.ShapeDtypeStruct((N_DEV * m_per, n), F32)]
    out_specs = [_VM]
    if with_sum:
        out_shape.append(jax.ShapeDtypeStruct((m_per, n), F32))
        out_specs.append(_VM)
    res = _pcall(
        body, name="allgather8_sum" if with_sum else "allgather8",
        in_specs=[_VM], out_specs=out_specs, out_shape=out_shape,
        scratch_shapes=[pltpu.SemaphoreType.DMA((7,)), pltpu.SemaphoreType.DMA((7,)), pltpu.SemaphoreType.DMA],
        compiler_params=pltpu.CompilerParams(vmem_limit_bytes=VMEM_LIMIT),
    )(block)
    return res if with_sum else res[0]


def _core_swap(arrs):
    n = len(arrs)

    def body(*refs):
        ins, outs = refs[:n], refs[n:2 * n]
        send_sems, recv_sems = refs[2 * n:]
        x, y, c = _place()
        cps = []
        for j in range(n):
            cp = pltpu.make_async_remote_copy(
                src_ref=ins[j], dst_ref=outs[j], send_sem=send_sems.at[j], recv_sem=recv_sems.at[j],
                device_id=(x, y, 1 - c), device_id_type=MESH)
            cp.start()
            cps.append(cp)
        for cp in cps:
            cp.wait()

    return _pcall(
        body, name="core_swap",
        in_specs=[_ANY] * n, out_specs=[_ANY] * n,
        out_shape=[jax.ShapeDtypeStruct(a.shape, a.dtype) for a in arrs],
        scratch_shapes=[pltpu.SemaphoreType.DMA((n,)), pltpu.SemaphoreType.DMA((n,))],
    )(*arrs)


def _cols_to_chips(w):
    *lead, a, nb = w.shape
    w = w.reshape(*lead, a, N_CHIPS, nb // N_CHIPS)
    return jnp.moveaxis(w, -2, 0)


def _chips_to_cols(g):
    g = jnp.moveaxis(g, 0, -2)
    *lead, a, k, b = g.shape
    return g.reshape(*lead, a, k * b)


def _my_cols(full, chip):
    w = full.shape[-1] // N_CHIPS
    return lax.dynamic_slice_in_dim(full, chip * w, w, axis=full.ndim - 1)


def _pad_rows(a, rows):
    return jnp.pad(a, ((0, rows - a.shape[0]), (0, 0)))


def _to_lanes(a):
    flat = a.reshape(-1)
    n = -(-flat.shape[0] // (8 * LANES)) * (8 * LANES)
    return jnp.pad(flat, (0, n - flat.shape[0])).reshape(-1, LANES)


class _Packer:
    def __init__(self):
        self.items = []
        self.rows = 0

    def add(self, name, a):
        lanes = _to_lanes(a)
        self.items.append((name, self.rows, a.shape, lanes))
        self.rows += lanes.shape[0]

    def pack(self):
        total = -(-self.rows // 8) * 8
        return _pad_rows(jnp.concatenate([it[3] for it in self.items], axis=0), total)

    def unpack(self, buf):
        out = {}
        for name, row, shape, lanes in self.items:
            size = 1
            for s in shape:
                size *= s
            out[name] = buf[row:row + lanes.shape[0]].reshape(-1)[:size].reshape(shape)
        return out


TM_SEQ = 512
TM_FFN = 256


LAYER_KEYS = ("in", "out", "gate", "up", "down")
COL_KEYS = ("in", "gate", "up")


def _layer_big_names(layer):
    i = layer // 2
    mix = (("ab_w_in", i), ("ab_w_out", i)) if layer % 2 == 0 else (("cf_w_pw1", i), ("cf_w_pw2", i))
    return dict(zip(LAYER_KEYS, mix + (("ffn_w_gate", layer), ("ffn_w_up", layer), ("ffn_w_down", layer))))


def _unpack_weight(key, g):
    g = g.reshape(N_CHIPS, -1, g.shape[-1])
    return _chips_to_cols(g) if key in COL_KEYS else g.reshape(-1, g.shape[-1])


def _chunk_grad(key, dw):
    parts = _cols_to_chips(dw) if key in COL_KEYS else dw.reshape(N_CHIPS, -1, dw.shape[-1])
    return parts.astype(BF16)


def _local_step(x, target, mods, p, shards):
    t_total, d = x.shape
    depth = mods.shape[0]
    tm = min(TM_SEQ, t_total)
    tmf = min(TM_FFN, t_total)
    saved, weights = [], []
    xin = x
    weights = [{} for _ in range(depth)]

    def carried(stage, layer):
        if layer == 0:
            return {"in": (0, ("out", "gate")), "mix": (0, ("up", "down")), "ffn": (1, ("in", "out", "gate", "up"))}[stage]
        return {"in": (layer, ("down",)), "mix": (layer + 1, ("in", "out")), "ffn": (layer + 1, ("gate", "up"))}[stage]

    def gather(stage, layer):
        of, keys = carried(stage, layer)
        if of >= depth:
            return None
        return _Gather([shards[of][k].reshape(2, -1, shards[of][k].shape[-1]) for k in keys])

    def keep(stage, layer, arrs):
        of, keys = carried(stage, layer)
        for k, g in zip(keys, arrs):
            weights[of][k] = _unpack_weight(k, g)

    weights[0]["in"] = _unpack_weight("in", _standalone(_Gather([shards[0]["in"].reshape(2, -1, shards[0]["in"].shape[-1])]))[0])
    for layer in range(depth):
        i = layer // 2
        lw = weights[layer]
        sh1, sc1, g1, sh2, sc2, g2 = (mods[layer, k:k + 1] for k in range(6))
        vec_in = jnp.concatenate([p["norm_mix_g"][layer:layer + 1], sh1, sc1], axis=0)
        bias = None if layer % 2 == 0 else p["cf_b_pw1"][i:i + 1]
        (u,), arrived = _in_proj(xin, vec_in, lw["in"], bias, tm, exch=gather("in", layer))
        keep("in", layer, arrived)
        if layer % 2 == 0:
            (y, x2), arrived = _ab_fwd(u, xin, g1, p["ab_conv"][i], p["ab_w_pool"][i].astype(BF16),
                                       p["ab_pool_scale"][i:i + 1], lw["out"], tm, exch=gather("mix", layer))
            zc = None
        else:
            vec_cf = jnp.concatenate([g1, p["cf_b_dw"][i:i + 1], p["cf_ln_g"][i:i + 1], p["cf_ln_b"][i:i + 1],
                                      p["cf_b_pw2"][i:i + 1]], axis=0)
            (zc, y, x2), arrived = _cf_fwd(u, xin, vec_cf, _pad_rows(p["cf_w_dw"][i], 32), lw["out"], tm,
                                           exch=gather("mix", layer))
        keep("mix", layer, arrived)
        vec_ffn = jnp.concatenate([p["norm_ffn_g"][layer:layer + 1], sh2, sc2, g2], axis=0)
        (a, b, fout, x3), arrived = _ffn_fwd(x2, vec_ffn, lw["gate"], lw["up"], lw["down"], tmf,
                                             exch=gather("ffn", layer))
        keep("ffn", layer, arrived)
        saved.append((xin, u, y, zc, x2, a, b, fout))
        xin = x3

    (dx, fin), _ = _final_fwd_bwd(xin, target, p["final_norm_g"].reshape(1, d), tm)
    grads = {"final_norm_g": fin[0], "loss": fin[1, 0:1]}
    per_layer = {k: [None] * depth for k in ("norm_mix_g", "norm_ffn_g")}
    half = {k: [None] * (depth // 2) for k in (
        "ab_conv", "ab_w_pool", "ab_pool_scale", "cf_b_pw1", "cf_w_dw", "cf_b_dw", "cf_ln_g", "cf_ln_b", "cf_b_pw2")}
    dmods = [None] * depth
    received = {}
    pending = None
    for layer in reversed(range(depth)):
        i = layer // 2
        lw = weights[layer]
        xin, u, y, zc, x2, a, b, fout = saved[layer]
        sh1, sc1, g1, sh2, sc2, g2 = (mods[layer, k:k + 1] for k in range(6))
        above = _Scatter([pending]) if pending is not None else None
        (da, db, dwd, acc_d), arrived = _ffn_bwd_down(dx, fout, a, b, g2, lw["down"], tmf, exch=above)
        if pending is not None:
            received[(layer + 1, "in")] = arrived[0]
        vec_ffn = jnp.concatenate([p["norm_ffn_g"][layer:layer + 1], sh2, sc2], axis=0)
        (dx2, dwg, dwu, acc_u), arrived = _ffn_bwd_up(da, db, x2, dx, vec_ffn, lw["gate"], lw["up"], tmf,
                                                      exch=_Scatter([_chunk_grad("down", dwd)]))
        received[(layer, "down")] = arrived[0]
        per_layer["norm_ffn_g"][layer] = acc_u[2]
        vec_in = jnp.concatenate([p["norm_mix_g"][layer:layer + 1], sh1, sc1], axis=0)
        send_gate = _Scatter([_chunk_grad("gate", dwg)])
        if layer % 2 == 0:
            (dpre, dwout, dwpool, acc_o), arrived = _ab_bwd_out(
                dx2, y, u, g1, p["ab_conv"][i], p["ab_w_pool"][i].astype(BF16), p["ab_pool_scale"][i:i + 1],
                lw["out"], tm, exch=send_gate)
            received[(layer, "gate")] = arrived[0]
            send_up_out = _Scatter([_chunk_grad("up", dwu), _chunk_grad("out", dwout)])
            (dx, dwin, dconv, acc_i), arrived = _ab_bwd_in(dpre, u, xin, dx2, vec_in, p["ab_conv"][i], lw["in"], tm,
                                                           exch=send_up_out)
            half["ab_w_pool"][i] = dwpool
            half["ab_pool_scale"][i] = acc_o[1, 0:d // 2]
            half["ab_conv"][i] = dconv[0:3]
        else:
            vec_cf = jnp.concatenate([g1, p["cf_ln_g"][i:i + 1], p["cf_ln_b"][i:i + 1]], axis=0)
            (dzc, dwout, acc_o), arrived = _cf_bwd_out(dx2, y, zc, vec_cf, lw["out"], tm, exch=send_gate)
            received[(layer, "gate")] = arrived[0]
            send_up_out = _Scatter([_chunk_grad("up", dwu), _chunk_grad("out", dwout)])
            (dx, dwin, dwdw, db1, acc_i), arrived = _cf_bwd_in(
                dzc, u, xin, dx2, vec_in, _pad_rows(p["cf_w_dw"][i], 32), lw["in"], tm, exch=send_up_out)
            half["cf_b_pw2"][i] = acc_o[1]
            half["cf_ln_g"][i] = acc_o[2]
            half["cf_ln_b"][i] = acc_o[3]
            half["cf_b_dw"][i] = acc_o[4]
            half["cf_w_dw"][i] = dwdw[0:CONF_KERNEL]
            half["cf_b_pw1"][i] = db1[0]
        received[(layer, "up")], received[(layer, "out")] = arrived
        per_layer["norm_mix_g"][layer] = acc_i[2]
        dmods[layer] = jnp.stack([acc_i[0], acc_i[1], acc_o[0], acc_u[0], acc_u[1], acc_d[0]], axis=0)
        pending = _chunk_grad("in", dwin)
    received[(0, "in")] = _standalone(_Scatter([pending]))[0]
    for k, v in {**per_layer, **half}.items():
        grads[k] = jnp.stack(v, axis=0)
    return dx, grads, jnp.stack(dmods, axis=0), received


SMALL_COLS = ("ab_conv", "cf_b_pw1", "cf_w_dw", "cf_b_dw", "cf_ln_g", "cf_ln_b", "cf_b_pw2")
SMALL_REPL = ("norm_mix_g", "norm_ffn_g", "ab_w_pool", "ab_pool_scale", "final_norm_g")
WEIGHTS = ("norm_mix_g", "norm_ffn_g", "w_mod", "b_mod", "ab_w_in", "ab_conv", "ab_w_pool", "ab_pool_scale",
           "ab_w_out", "cf_w_pw1", "cf_b_pw1", "cf_w_dw", "cf_b_dw", "cf_ln_g", "cf_ln_b", "cf_w_pw2",
           "cf_b_pw2", "ffn_w_gate", "ffn_w_up", "ffn_w_down", "final_norm_g")


def kernel(x, c, norm_mix_g, norm_ffn_g, w_mod, b_mod, ab_w_in, ab_conv, ab_w_pool, ab_pool_scale, ab_w_out, cf_w_pw1, cf_b_pw1, cf_w_dw, cf_b_dw, cf_ln_g, cf_ln_b, cf_w_pw2, cf_b_pw2, ffn_w_gate, ffn_w_up, ffn_w_down, final_norm_g, loss_target, m_norm_mix_g, m_norm_ffn_g, m_w_mod, m_b_mod, m_ab_w_in, m_ab_conv, m_ab_w_pool, m_ab_pool_scale, m_ab_w_out, m_cf_w_pw1, m_cf_b_pw1, m_cf_w_dw, m_cf_b_dw, m_cf_ln_g, m_cf_ln_b, m_cf_w_pw2, m_cf_b_pw2, m_ffn_w_gate, m_ffn_w_up, m_ffn_w_down, m_final_norm_g, v_norm_mix_g, v_norm_ffn_g, v_w_mod, v_b_mod, v_ab_w_in, v_ab_conv, v_ab_w_pool, v_ab_pool_scale, v_ab_w_out, v_cf_w_pw1, v_cf_b_pw1, v_cf_w_dw, v_cf_b_dw, v_cf_ln_g, v_cf_ln_b, v_cf_w_pw2, v_cf_b_pw2, v_ffn_w_gate, v_ffn_w_up, v_ffn_w_down, v_final_norm_g):
    w = dict(norm_mix_g=norm_mix_g, norm_ffn_g=norm_ffn_g, w_mod=w_mod, b_mod=b_mod, ab_w_in=ab_w_in,
             ab_conv=ab_conv, ab_w_pool=ab_w_pool, ab_pool_scale=ab_pool_scale, ab_w_out=ab_w_out,
             cf_w_pw1=cf_w_pw1, cf_b_pw1=cf_b_pw1, cf_w_dw=cf_w_dw, cf_b_dw=cf_b_dw, cf_ln_g=cf_ln_g,
             cf_ln_b=cf_ln_b, cf_w_pw2=cf_w_pw2, cf_b_pw2=cf_b_pw2, ffn_w_gate=ffn_w_gate, ffn_w_up=ffn_w_up,
             ffn_w_down=ffn_w_down, final_norm_g=final_norm_g)
    mom = dict(norm_mix_g=m_norm_mix_g, norm_ffn_g=m_norm_ffn_g, w_mod=m_w_mod, b_mod=m_b_mod, ab_w_in=m_ab_w_in,
               ab_conv=m_ab_conv, ab_w_pool=m_ab_w_pool, ab_pool_scale=m_ab_pool_scale, ab_w_out=m_ab_w_out,
               cf_w_pw1=m_cf_w_pw1, cf_b_pw1=m_cf_b_pw1, cf_w_dw=m_cf_w_dw, cf_b_dw=m_cf_b_dw, cf_ln_g=m_cf_ln_g,
               cf_ln_b=m_cf_ln_b, cf_w_pw2=m_cf_w_pw2, cf_b_pw2=m_cf_b_pw2, ffn_w_gate=m_ffn_w_gate,
               ffn_w_up=m_ffn_w_up, ffn_w_down=m_ffn_w_down, final_norm_g=m_final_norm_g)
    var = dict(norm_mix_g=v_norm_mix_g, norm_ffn_g=v_norm_ffn_g, w_mod=v_w_mod, b_mod=v_b_mod, ab_w_in=v_ab_w_in,
               ab_conv=v_ab_conv, ab_w_pool=v_ab_w_pool, ab_pool_scale=v_ab_pool_scale, ab_w_out=v_ab_w_out,
               cf_w_pw1=v_cf_w_pw1, cf_b_pw1=v_cf_b_pw1, cf_w_dw=v_cf_w_dw, cf_b_dw=v_cf_b_dw, cf_ln_g=v_cf_ln_g,
               cf_ln_b=v_cf_ln_b, cf_w_pw2=v_cf_w_pw2, cf_b_pw2=v_cf_b_pw2, ffn_w_gate=v_ffn_w_gate,
               ffn_w_up=v_ffn_w_up, ffn_w_down=v_ffn_w_down, final_norm_g=v_final_norm_g)
    px, py, pc = _place()
    chip = 2 * px + py
    dev = 2 * chip + pc
    depth, d, mod_cols = w_mod.shape
    x = x[0]
    target = loss_target[0]

    small_in = _Packer()
    small_in.add("c", c)
    for name in SMALL_COLS:
        small_in.add(name, w[name])
    gathered = _allgather8(small_in.pack(), with_sum=False).reshape(N_DEV, -1, LANES)
    per_dev = [small_in.unpack(gathered[k]) for k in range(N_DEV)]
    c_all = jnp.concatenate([pd["c"] for pd in per_dev], axis=0)
    params = {name: jnp.concatenate([per_dev[2 * k][name] for k in range(N_CHIPS)], axis=-1)
              for name in SMALL_COLS}
    for name in SMALL_REPL:
        params[name] = w[name]

    mod_part = _mod_fwd(c_all, w_mod, _my_cols(b_mod, chip))
    mod_all = _allgather8(mod_part.reshape(-1, LANES), with_sum=False)
    mod_all = mod_all.reshape(N_CHIPS, 2, depth, N_DEV, mod_cols)[:, 0]
    mod_all = jnp.moveaxis(mod_all, 0, 2).reshape(depth, N_DEV, N_CHIPS * mod_cols)
    mods = lax.dynamic_index_in_dim(mod_all, dev, axis=1, keepdims=False).reshape(depth, 6, d)

    shards = [{k: w[name][idx].astype(BF16) for k, (name, idx) in _layer_big_names(layer).items()}
              for layer in range(depth)]
    grad_x, grads, dmods, received = _local_step(x, target, mods, params, shards)

    small_out = _Packer()
    small_out.add("dmods", dmods)
    for name in ("loss",) + SMALL_REPL + SMALL_COLS:
        small_out.add(name, grads[name])
    parts_all, parts_sum = _allgather8(small_out.pack(), with_sum=True)
    small_sum = small_out.unpack(parts_sum)
    loss = small_sum["loss"][0]
    dmods_all = jnp.stack([small_out.unpack(pa)["dmods"] for pa in parts_all.reshape(N_DEV, -1, LANES)], axis=1)
    dmods_all = dmods_all.reshape(depth, N_DEV, 6 * d)

    g_final = {}
    g_final["w_mod"] = [_mod_bwd(c_all.T, _my_cols(dmods_all, chip))]
    g_final["b_mod"] = [small_sum["dmods"].reshape(depth, 6 * d)]
    for name in SMALL_REPL:
        g_final[name] = [small_sum[name]]
    for name in SMALL_COLS:
        g_final[name] = [_my_cols(small_sum[name], chip)]

    by_name = {}
    for layer in range(depth):
        for k, (name, _) in _layer_big_names(layer).items():
            by_name.setdefault(name, []).append(_sum_parts(received[(layer, k)]))
    big = sorted(by_name)
    mine = [jnp.stack(by_name[name], axis=0) for name in big]
    theirs = _core_swap(mine)
    for name, a, b in zip(big, mine, theirs):
        g_final[name] = [a, b]

    out_g, out_d, out_m, out_v = [], [], [], []
    for name in WEIGHTS:
        g, dlt, mn, vn = _adamw(w[name], g_final[name], mom[name], var[name])
        out_g.append(g)
        out_d.append(dlt)
        out_m.append(mn)
        out_v.append(vn)
    return (loss, grad_x[None], *out_g, *out_d, *out_m, *out_v)
```

```python
import functools

import jax
import jax.numpy as jnp
from jax import lax
from jax.experimental import pallas as pl
from jax.experimental.pallas import tpu as pltpu

F32 = jnp.float32
BF16 = jnp.bfloat16
RMS_EPS = 1e-6
LN_EPS = 1e-5
ADAM_LR = 0.001
ADAM_B1 = 0.9
ADAM_B2 = 0.999
ADAM_EPS = 1e-08
ADAM_WD = 0.01
ADAM_STEP = 10
POOL_WINDOWS = (2, 4, 8, 16)
CONF_KERNEL = 31
N_CHIPS = 4
N_DEV = 8
HALO = 16
CONV_COLS = 256
FFN_CHUNK = 1536
LANES = 1024
VMEM_LIMIT = 56 * 1024 * 1024
EW_BLOCK_ELEMS = 256 * 1024
MESH = pl.DeviceIdType.MESH
HIGHEST = lax.Precision.HIGHEST

_pcall = pl.pallas_call


def _dot(a, b):
    return jnp.dot(a, b, preferred_element_type=F32)


def _dot_tn(a, b):
    return lax.dot_general(a, b, (((0,), (0,)), ((), ())), preferred_element_type=F32)


def _dot_nt(a, b):
    return lax.dot_general(a, b, (((1,), (1,)), ((), ())), preferred_element_type=F32)


def _colsum(v):
    return jnp.sum(v, axis=0, keepdims=True)


def _sigmoid(v):
    return 1.0 / (1.0 + jnp.exp(-v))


def _rows(tm, c):
    return pl.BlockSpec((tm, c), lambda i: (i, 0))


def _full(shape):
    nd = len(shape)
    return pl.BlockSpec(shape, lambda i: (0,) * nd)


_VM = pl.BlockSpec(memory_space=pltpu.VMEM)
_ANY = pl.BlockSpec(memory_space=pl.ANY)


def _halo_specs(tm, c, t_total):
    r = tm // HALO
    last = t_total // HALO - 1
    prev = pl.BlockSpec((HALO, c), lambda i: (jnp.maximum(i * r - 1, 0), 0))
    nxt = pl.BlockSpec((HALO, c), lambda i: (jnp.minimum((i + 1) * r, last), 0))
    return prev, _rows(tm, c), nxt


def _seq_params():
    return pltpu.CompilerParams(dimension_semantics=("arbitrary",), vmem_limit_bytes=VMEM_LIMIT)


def _place():
    return lax.axis_index("x"), lax.axis_index("y"), lax.axis_index("c")


def _peer_chips(x, y):
    return [(1 - x, y), (x, 1 - y), (1 - x, 1 - y)]


class _Gather:
    tag = "gather"

    def __init__(self, arrs):
        self.arrs = list(arrs)

    def out_shapes(self):
        return [jax.ShapeDtypeStruct((N_CHIPS,) + a.shape, a.dtype) for a in self.arrs]

    def sems(self):
        n = len(self.arrs)
        return [pltpu.SemaphoreType.DMA((3 * n,)) for _ in range(4)] + [pltpu.SemaphoreType.DMA((n,))]

    def _copies(self, ins, outs, sems, kinds):
        ici_send, ici_recv, d2d_send, d2d_recv, local_sems = sems
        x, y, c = _place()
        me = 2 * x + y
        found = {kind: [] for kind in kinds}
        for j in range(len(ins)):
            if "local" in kinds:
                found["local"].append(pltpu.make_async_copy(ins[j], outs[j].at[me], local_sems.at[j]))
            for k, (px, py) in enumerate(_peer_chips(x, y)):
                ici = dict(send_sem=ici_send.at[3 * j + k], recv_sem=ici_recv.at[3 * j + k],
                           device_id=(px, py, c), device_id_type=MESH)
                d2d = dict(send_sem=d2d_send.at[3 * j + k], recv_sem=d2d_recv.at[3 * j + k],
                           device_id=(x, y, 1 - c), device_id_type=MESH)
                theirs = outs[j].at[2 * px + py]
                if "send" in kinds:
                    found["send"].append(pltpu.make_async_remote_copy(
                        src_ref=ins[j].at[c], dst_ref=outs[j].at[me, c], **ici))
                if "arrival" in kinds:
                    found["arrival"].append(pltpu.make_async_remote_copy(
                        src_ref=ins[j].at[c], dst_ref=theirs.at[c], **ici))
                if "pass" in kinds:
                    found["pass"].append(pltpu.make_async_remote_copy(
                        src_ref=theirs.at[c], dst_ref=theirs.at[c], **d2d))
                if "passed" in kinds:
                    found["passed"].append(pltpu.make_async_remote_copy(
                        src_ref=theirs.at[c], dst_ref=theirs.at[1 - c], **d2d))
        return found

    def start(self, ins, outs, sems):
        found = self._copies(ins, outs, sems, ("local", "send"))
        for cp in found["local"] + found["send"]:
            cp.start()

    def mid(self, ins, outs, sems):
        found = self._copies(ins, outs, sems, ("arrival", "pass"))
        for arrived, onward in zip(found["arrival"], found["pass"]):
            arrived.wait_recv()
            onward.start()

    def wait(self, ins, outs, sems):
        found = self._copies(ins, outs, sems, ("local", "send", "pass", "passed"))
        for cp in found["passed"]:
            cp.wait_recv()
        for cp in found["send"] + found["pass"]:
            cp.wait_send()
        for cp in found["local"]:
            cp.wait()


class _Scatter:
    tag = "scatter"

    def __init__(self, arrs):
        self.arrs = list(arrs)

    def out_shapes(self):
        return [jax.ShapeDtypeStruct((2,) + a.shape, a.dtype) for a in self.arrs]

    def sems(self):
        n = len(self.arrs)
        dma = pltpu.SemaphoreType.DMA
        return [dma((3 * n,)), dma((3 * n,)), dma((4 * n,)), dma((4 * n,)), dma((n,))]

    def _copies(self, ins, outs, sems, kinds):
        ici_send, ici_recv, d2d_send, d2d_recv, local_sems = sems
        x, y, c = _place()
        me = 2 * x + y
        found = {kind: [] for kind in kinds}
        for j in range(len(ins)):
            def d2d(k):
                return dict(send_sem=d2d_send.at[4 * j + k], recv_sem=d2d_recv.at[4 * j + k],
                            device_id=(x, y, 1 - c), device_id_type=MESH)

            if "local" in kinds:
                found["local"].append(pltpu.make_async_copy(ins[j].at[me], outs[j].at[0, me], local_sems.at[j]))
            if "own" in kinds:
                found["own"].append(pltpu.make_async_remote_copy(
                    src_ref=ins[j].at[me], dst_ref=outs[j].at[1, me], **d2d(3)))
            if "passed" in kinds:
                found["passed"].append(pltpu.make_async_remote_copy(
                    src_ref=ins[j].at[me], dst_ref=outs[j].at[1, me], **d2d(3)))
            for k, (px, py) in enumerate(_peer_chips(x, y)):
                ici = dict(send_sem=ici_send.at[3 * j + k], recv_sem=ici_recv.at[3 * j + k],
                           device_id=(px, py, c), device_id_type=MESH)
                peer = 2 * px + py
                if "send" in kinds:
                    found["send"].append(pltpu.make_async_remote_copy(
                        src_ref=ins[j].at[peer], dst_ref=outs[j].at[0, me], **ici))
                if "arrival" in kinds:
                    found["arrival"].append(pltpu.make_async_remote_copy(
                        src_ref=ins[j].at[me], dst_ref=outs[j].at[0, peer], **ici))
                if "pass" in kinds:
                    found["pass"].append(pltpu.make_async_remote_copy(
                        src_ref=outs[j].at[0, peer], dst_ref=outs[j].at[1, peer], **d2d(k)))
                if "passed" in kinds:
                    found["passed"].append(pltpu.make_async_remote_copy(
                        src_ref=outs[j].at[0, peer], dst_ref=outs[j].at[1, peer], **d2d(k)))
        return found

    def start(self, ins, outs, sems):
        found = self._copies(ins, outs, sems, ("local", "own", "send"))
        for cp in found["local"] + found["own"] + found["send"]:
            cp.start()

    def mid(self, ins, outs, sems):
        found = self._copies(ins, outs, sems, ("arrival", "pass"))
        for arrived, onward in zip(found["arrival"], found["pass"]):
            arrived.wait_recv()
            onward.start()

    def wait(self, ins, outs, sems):
        found = self._copies(ins, outs, sems, ("local", "own", "send", "pass", "passed"))
        for cp in found["passed"]:
            cp.wait_recv()
        for cp in found["own"] + found["send"] + found["pass"]:
            cp.wait_send()
        for cp in found["local"]:
            cp.wait()


def _standalone(op):
    n = len(op.arrs)

    def body(*refs):
        for phase in (op.start, op.mid, op.wait):
            phase(refs[:n], refs[n:2 * n], refs[2 * n:])

    return _pcall(body, name="chip_" + op.tag, in_specs=[_ANY] * n, out_specs=[_ANY] * n,
                  out_shape=op.out_shapes(), scratch_shapes=op.sems())(*op.arrs)


def _call(body, *, name, nsteps, in_specs, out_specs, out_shape, args, scratch_shapes=(), exch=None):
    if exch is None:
        outs = _pcall(body, name=name, grid=(nsteps,), in_specs=list(in_specs), out_specs=list(out_specs),
                      out_shape=list(out_shape), scratch_shapes=list(scratch_shapes),
                      compiler_params=_seq_params())(*args)
        return list(outs), []
    n, ni, no, ns = len(exch.arrs), len(in_specs), len(out_specs), len(scratch_shapes)

    def hosted(*refs):
        xin = refs[ni:ni + n]
        xout = refs[ni + n + no:ni + 2 * n + no]
        scr = refs[ni + 2 * n + no:]

        @pl.when(pl.program_id(0) == 0)
        def _():
            exch.start(xin, xout, scr[ns:])

        body(*refs[:ni], *refs[ni + n:ni + n + no], *scr[:ns])

        @pl.when(pl.program_id(0) == max(nsteps - 3, 0))
        def _():
            exch.mid(xin, xout, scr[ns:])

        @pl.when(pl.program_id(0) == nsteps - 1)
        def _():
            exch.wait(xin, xout, scr[ns:])

    outs = _pcall(hosted, name=name + "_" + exch.tag, grid=(nsteps,),
                  in_specs=[*in_specs, *[_ANY] * n], out_specs=[*out_specs, *[_ANY] * n],
                  out_shape=[*out_shape, *exch.out_shapes()],
                  scratch_shapes=[*scratch_shapes, *exch.sems()],
                  compiler_params=_seq_params())(*args, *exch.arrs)
    return list(outs[:no]), list(outs[no:])


def _rms(x):
    r = lax.rsqrt(jnp.mean(x * x, axis=-1, keepdims=True) + RMS_EPS)
    return x * r, r


def _norm_mod(x, g, sh, sc):
    xhat, _ = _rms(x)
    return xhat * g * (1.0 + sc) + sh


def _norm_mod_bwd(dh, x, g, sc):
    xhat, r = _rms(x)
    n = xhat * g
    dsh = _colsum(dh)
    dsc = _colsum(dh * n)
    dn = dh * (1.0 + sc)
    dg = _colsum(dn * xhat)
    dxn = dn * g
    dx = r * (dxn - xhat * jnp.mean(dxn * xhat, axis=-1, keepdims=True))
    return dx, dsh, dsc, dg


def _fill_ext(ext_ref, prev, cur, nxt, i, nsteps, tm):
    ext_ref[0:HALO, :] = jnp.where(i > 0, prev, 0.0)
    ext_ref[HALO:HALO + tm, :] = cur
    ext_ref[HALO + tm:HALO + tm + HALO, :] = jnp.where(i < nsteps - 1, nxt, 0.0)


def _shift_scratch(tm):
    return pltpu.VMEM((8, tm + 2 * HALO - 8, CONV_COLS), F32)


def _fill_shifts(sh_ref, ext_ref, lo, hi, tm):
    for b in range(8):
        sh_ref[b] = ext_ref[b:b + tm + 2 * HALO - 8, lo:hi]


def _shifted(sh_ref, offset, tm):
    b = offset % 8
    start = HALO + offset - b
    return sh_ref[b, start:start + tm, :]


def _window_count(t, wdw, t_total):
    left = wdw // 2
    right = wdw - 1 - left
    cnt = jnp.minimum(t + right, t_total - 1) - jnp.maximum(t - left, 0) + 1
    return jnp.maximum(cnt, 1).astype(F32)


def _in_proj(x, vec, w, bias, tm, exch=None):
    t_total, d = x.shape
    n = w.shape[1]
    has_bias = bias is not None

    def body(*refs):
        if has_bias:
            x_ref, vec_ref, w_ref, b_ref, u_ref = refs
        else:
            x_ref, vec_ref, w_ref, u_ref = refs
        h = _norm_mod(x_ref[...], vec_ref[0:1, :], vec_ref[1:2, :], vec_ref[2:3, :])
        u = _dot(h.astype(BF16), w_ref[...])
        if has_bias:
            u = u + b_ref[...]
        u_ref[...] = u.astype(BF16)

    in_specs = [_rows(tm, d), _full(vec.shape), _VM]
    args = [x, vec, w]
    if has_bias:
        in_specs.append(_full(bias.shape))
        args.append(bias)
    return _call(
        body, name="in_proj_bias" if has_bias else "in_proj", nsteps=t_total // tm,
        in_specs=in_specs, out_specs=[_rows(tm, n)], out_shape=[jax.ShapeDtypeStruct((t_total, n), BF16)],
        args=args, exch=exch)


def _ab_core(up_ref, uc_ref, un_ref, conv_ref, wpool_ref, q_ext, p_ext, i, nsteps, tm, t_total):
    da = uc_ref.shape[1] // 4

    def cols(ref, k):
        return ref[:, k * da:(k + 1) * da].astype(F32)

    _fill_ext(q_ext, cols(up_ref, 1) * cols(up_ref, 2), cols(uc_ref, 1) * cols(uc_ref, 2),
              cols(un_ref, 1) * cols(un_ref, 2), i, nsteps, tm)
    _fill_ext(p_ext, cols(up_ref, 3), cols(uc_ref, 3), cols(un_ref, 3), i, nsteps, tm)
    bg = cols(uc_ref, 0)
    cq = (conv_ref[0:1, :] * q_ext[HALO - 1:HALO - 1 + tm, :] + conv_ref[1:2, :] * q_ext[HALO:HALO + tm, :]
          + conv_ref[2:3, :] * q_ext[HALO + 1:HALO + 1 + tm, :])
    t = i * tm + lax.broadcasted_iota(jnp.int32, (tm, 1), 0)
    gw = da // len(POOL_WINDOWS)
    pooled, ybpre = [], []
    for g, wdw in enumerate(POOL_WINDOWS):
        left = wdw // 2
        right = wdw - 1 - left
        lo, hi = g * gw, (g + 1) * gw
        s = p_ext[HALO - left:HALO - left + tm, lo:hi]
        for o in range(-left + 1, right + 1):
            s = s + p_ext[HALO + o:HALO + o + tm, lo:hi]
        pg = s / _window_count(t, wdw, t_total) - p_ext[HALO:HALO + tm, lo:hi]
        pooled.append(pg.astype(BF16))
        ybpre.append(_dot(pooled[-1], wpool_ref[g]))
    return bg, cq, pooled, jnp.concatenate(ybpre, axis=1)


def _ab_fwd(u, x, vec, conv, wpool, scale, wout, tm, exch=None):
    t_total, d = x.shape
    nu = u.shape[1]
    da = nu // 4
    nsteps = t_total // tm

    def body(up_ref, uc_ref, un_ref, x_ref, vec_ref, conv_ref, wpool_ref, scale_ref, wout_ref,
             y_ref, x2_ref, q_ext, p_ext):
        i = pl.program_id(0)
        bg, cq, _, ybpre = _ab_core(up_ref, uc_ref, un_ref, conv_ref, wpool_ref, q_ext, p_ext,
                                    i, nsteps, tm, t_total)
        cat = jnp.concatenate([bg * cq, ybpre * scale_ref[...]], axis=1).astype(BF16)
        y = _dot(cat, wout_ref[...])
        y_ref[...] = y.astype(BF16)
        x2_ref[...] = x_ref[...] + vec_ref[0:1, :] * y

    return _call(
        body, name="ab_fwd", nsteps=nsteps,
        in_specs=[*_halo_specs(tm, nu, t_total), _rows(tm, d), _full(vec.shape), _full(conv.shape),
                  _full(wpool.shape), _full(scale.shape), _VM],
        out_specs=[_rows(tm, d), _rows(tm, d)],
        out_shape=[jax.ShapeDtypeStruct((t_total, d), BF16), jax.ShapeDtypeStruct((t_total, d), F32)],
        scratch_shapes=[pltpu.VMEM((tm + 2 * HALO, da), F32), pltpu.VMEM((tm + 2 * HALO, da), F32)],
        args=(u, u, u, x, vec, conv, wpool, scale, wout), exch=exch)


def _glu_ext(up_ref, uc_ref, un_ref, z_ext, i, nsteps, tm):
    dz = uc_ref.shape[1] // 2

    def glu(ref):
        return ref[:, 0:dz].astype(F32) * _sigmoid(ref[:, dz:2 * dz].astype(F32))

    _fill_ext(z_ext, glu(up_ref), glu(uc_ref), glu(un_ref), i, nsteps, tm)


def _layer_norm_stats(zc):
    mu = jnp.mean(zc, axis=-1, keepdims=True)
    dlt = zc - mu
    rstd = lax.rsqrt(jnp.mean(dlt * dlt, axis=-1, keepdims=True) + LN_EPS)
    return dlt * rstd, rstd


def _cf_fwd(u, x, vec, wdw, wpw2, tm, exch=None):
    t_total, d = x.shape
    nu = u.shape[1]
    nsteps = t_total // tm
    left = (CONF_KERNEL - 1) // 2

    def body(up_ref, uc_ref, un_ref, x_ref, vec_ref, wdw_ref, wpw2_ref, zc_ref, y_ref, x2_ref, z_ext, sh_ref,
             zc_buf):
        i = pl.program_id(0)
        _glu_ext(up_ref, uc_ref, un_ref, z_ext, i, nsteps, tm)
        for lo in range(0, d, CONV_COLS):
            hi = lo + CONV_COLS
            _fill_shifts(sh_ref, z_ext, lo, hi, tm)
            acc = wdw_ref[0:1, lo:hi] * _shifted(sh_ref, -left, tm)
            for k in range(1, CONF_KERNEL):
                acc = acc + wdw_ref[k:k + 1, lo:hi] * _shifted(sh_ref, k - left, tm)
            zc_buf[:, lo:hi] = acc
        zc = zc_buf[...] + vec_ref[1:2, :]
        zc_ref[...] = zc.astype(BF16)
        zn, _ = _layer_norm_stats(zc)
        zl = zn * vec_ref[2:3, :] + vec_ref[3:4, :]
        zs = zl * _sigmoid(zl)
        y = _dot(zs.astype(BF16), wpw2_ref[...]) + vec_ref[4:5, :]
        y_ref[...] = y.astype(BF16)
        x2_ref[...] = x_ref[...] + vec_ref[0:1, :] * y

    return _call(
        body, name="cf_fwd", nsteps=nsteps,
        in_specs=[*_halo_specs(tm, nu, t_total), _rows(tm, d), _full(vec.shape), _full(wdw.shape), _VM],
        out_specs=[_rows(tm, d), _rows(tm, d), _rows(tm, d)],
        out_shape=[jax.ShapeDtypeStruct((t_total, d), BF16), jax.ShapeDtypeStruct((t_total, d), BF16),
                   jax.ShapeDtypeStruct((t_total, d), F32)],
        scratch_shapes=[pltpu.VMEM((tm + 2 * HALO, d), F32), _shift_scratch(tm), pltpu.VMEM((tm, d), F32)],
        args=(u, u, u, x, vec, wdw, wpw2), exch=exch)


def _ffn_chunks(f, width=FFN_CHUNK):
    return [(lo, min(lo + width, f)) for lo in range(0, f, width)]


def _ffn_fwd(x2, vec, wg, wu, wd, tm, exch=None):
    t_total, d = x2.shape
    f = wg.shape[1]

    def body(x_ref, vec_ref, wg_ref, wu_ref, wd_ref, a_ref, b_ref, f_ref, x3_ref):
        xv = x_ref[...]
        h = _norm_mod(xv, vec_ref[0:1, :], vec_ref[1:2, :], vec_ref[2:3, :]).astype(BF16)
        y = None
        for lo, hi in _ffn_chunks(f):
            a = _dot(h, wg_ref[:, lo:hi])
            b = _dot(h, wu_ref[:, lo:hi])
            a_ref[:, lo:hi] = a.astype(BF16)
            b_ref[:, lo:hi] = b.astype(BF16)
            s = (a * _sigmoid(a) * b).astype(BF16)
            part = _dot(s, wd_ref[lo:hi, :])
            y = part if y is None else y + part
        f_ref[...] = y.astype(BF16)
        x3_ref[...] = xv + vec_ref[3:4, :] * y

    return _call(
        body, name="ffn_fwd", nsteps=t_total // tm,
        in_specs=[_rows(tm, d), _full(vec.shape), _VM, _VM, _VM],
        out_specs=[_rows(tm, f), _rows(tm, f), _rows(tm, d), _rows(tm, d)],
        out_shape=[jax.ShapeDtypeStruct((t_total, f), BF16), jax.ShapeDtypeStruct((t_total, f), BF16),
                   jax.ShapeDtypeStruct((t_total, d), BF16), jax.ShapeDtypeStruct((t_total, d), F32)],
        args=(x2, vec, wg, wu, wd), exch=exch)


def _final_fwd_bwd(x, target, vec, tm):
    t_total, d = x.shape

    def body(x_ref, t_ref, vec_ref, dx_ref, acc_ref):
        @pl.when(pl.program_id(0) == 0)
        def _():
            acc_ref[...] = jnp.zeros_like(acc_ref)

        g = vec_ref[0:1, :]
        xhat, r = _rms(x_ref[...])
        e = xhat * g - t_ref[...]
        acc_ref[1:2, :] += jnp.zeros((1, d), F32) + 0.5 * jnp.sum(jnp.mean(e * e, axis=-1, keepdims=True))
        dout = e * (1.0 / d)
        acc_ref[0:1, :] += _colsum(dout * xhat)
        dxn = dout * g
        dx_ref[...] = r * (dxn - xhat * jnp.mean(dxn * xhat, axis=-1, keepdims=True))

    return _call(
        body, name="final_fwd_bwd", nsteps=t_total // tm,
        in_specs=[_rows(tm, d), _rows(tm, d), _full(vec.shape)],
        out_specs=[_rows(tm, d), _VM],
        out_shape=[jax.ShapeDtypeStruct((t_total, d), F32), jax.ShapeDtypeStruct((8, d), F32)],
        args=(x, target, vec))


def _zero_at_start(*refs):
    @pl.when(pl.program_id(0) == 0)
    def _():
        for ref in refs:
            ref[...] = jnp.zeros_like(ref)


def _ffn_bwd_down(dx3, fout, a, b, vec, wd, tm, exch=None):
    t_total, d = dx3.shape
    f = a.shape[1]

    def body(dx_ref, f_ref, a_ref, b_ref, vec_ref, wd_ref, da_ref, db_ref, dwd_ref, acc_ref):
        _zero_at_start(dwd_ref, acc_ref)
        dx = dx_ref[...]
        acc_ref[0:1, :] += _colsum(dx * f_ref[...].astype(F32))
        dy = (dx * vec_ref[0:1, :]).astype(BF16)
        for lo, hi in _ffn_chunks(f, FFN_CHUNK // 2):
            av = a_ref[:, lo:hi].astype(F32)
            bv = b_ref[:, lo:hi].astype(F32)
            sg = _sigmoid(av)
            silu = av * sg
            ds = _dot_nt(dy, wd_ref[lo:hi, :])
            da_ref[:, lo:hi] = (ds * bv * (sg * (1.0 + av * (1.0 - sg)))).astype(BF16)
            db_ref[:, lo:hi] = (ds * silu).astype(BF16)
            dwd_ref[lo:hi, :] += _dot_tn((silu * bv).astype(BF16), dy)

    return _call(
        body, name="ffn_bwd_down", nsteps=t_total // tm,
        in_specs=[_rows(tm, d), _rows(tm, d), _rows(tm, f), _rows(tm, f), _full(vec.shape), _VM],
        out_specs=[_rows(tm, f), _rows(tm, f), _VM, _VM],
        out_shape=[jax.ShapeDtypeStruct((t_total, f), BF16), jax.ShapeDtypeStruct((t_total, f), BF16),
                   jax.ShapeDtypeStruct(wd.shape, F32), jax.ShapeDtypeStruct((8, d), F32)],
        args=(dx3, fout, a, b, vec, wd), exch=exch)


def _ffn_bwd_up(da, db, x2, dx3, vec, wg, wu, tm, exch=None):
    t_total, d = x2.shape
    f = da.shape[1]

    def body(da_ref, db_ref, x_ref, dx_ref, vec_ref, wg_ref, wu_ref, dx2_ref, dwg_ref, dwu_ref, acc_ref):
        _zero_at_start(dwg_ref, dwu_ref, acc_ref)
        xv = x_ref[...]
        g, sh, sc = vec_ref[0:1, :], vec_ref[1:2, :], vec_ref[2:3, :]
        h = _norm_mod(xv, g, sh, sc).astype(BF16)
        dav = da_ref[...]
        dbv = db_ref[...]
        dwg_ref[...] += _dot_tn(h, dav)
        dwu_ref[...] += _dot_tn(h, dbv)
        dh = _dot_nt(dav, wg_ref[...]) + _dot_nt(dbv, wu_ref[...])
        dxn, dsh, dsc, dg = _norm_mod_bwd(dh, xv, g, sc)
        acc_ref[0:1, :] += dsh
        acc_ref[1:2, :] += dsc
        acc_ref[2:3, :] += dg
        dx2_ref[...] = dx_ref[...] + dxn

    return _call(
        body, name="ffn_bwd_up", nsteps=t_total // tm,
        in_specs=[_rows(tm, f), _rows(tm, f), _rows(tm, d), _rows(tm, d), _full(vec.shape), _VM, _VM],
        out_specs=[_rows(tm, d), _VM, _VM, _VM],
        out_shape=[jax.ShapeDtypeStruct((t_total, d), F32), jax.ShapeDtypeStruct(wg.shape, F32),
                   jax.ShapeDtypeStruct(wu.shape, F32), jax.ShapeDtypeStruct((8, d), F32)],
        args=(da, db, x2, dx3, vec, wg, wu), exch=exch)


def _ab_bwd_out(dx, y, u, vec, conv, wpool, scale, wout, tm, exch=None):
    t_total, d = dx.shape
    nu = u.shape[1]
    da = nu // 4
    gw = da // len(POOL_WINDOWS)
    nsteps = t_total // tm

    def body(dx_ref, y_ref, up_ref, uc_ref, un_ref, vec_ref, conv_ref, wpool_ref, scale_ref, wout_ref,
             dpre_ref, dwout_ref, dwpool_ref, acc_ref, q_ext, p_ext):
        _zero_at_start(dwout_ref, dwpool_ref, acc_ref)
        i = pl.program_id(0)
        dxv = dx_ref[...]
        acc_ref[0:1, :] += _colsum(dxv * y_ref[...].astype(F32))
        dy = (dxv * vec_ref[0:1, :]).astype(BF16)
        bg, cq, pooled, ybpre = _ab_core(up_ref, uc_ref, un_ref, conv_ref, wpool_ref, q_ext, p_ext,
                                         i, nsteps, tm, t_total)
        cat = jnp.concatenate([bg * cq, ybpre * scale_ref[...]], axis=1).astype(BF16)
        dwout_ref[...] += _dot_tn(cat, dy)
        dcat = _dot_nt(dy, wout_ref[...])
        dya = dcat[:, 0:da]
        dyb = dcat[:, da:2 * da]
        acc_ref[1:2, 0:da] += _colsum(dyb * ybpre)
        dybpre = (dyb * scale_ref[...]).astype(BF16)
        dpooled = []
        for g in range(len(POOL_WINDOWS)):
            dg = dybpre[:, g * gw:(g + 1) * gw]
            dwpool_ref[g] += _dot_tn(pooled[g], dg)
            dpooled.append(_dot_nt(dg, wpool_ref[g]))
        dpre_ref[...] = jnp.concatenate([dya * cq, dya * bg] + dpooled, axis=1).astype(BF16)

    return _call(
        body, name="ab_bwd_out", nsteps=nsteps,
        in_specs=[_rows(tm, d), _rows(tm, d), *_halo_specs(tm, nu, t_total), _full(vec.shape),
                  _full(conv.shape), _full(wpool.shape), _full(scale.shape), _VM],
        out_specs=[_rows(tm, 3 * da), _VM, _VM, _VM],
        out_shape=[jax.ShapeDtypeStruct((t_total, 3 * da), BF16), jax.ShapeDtypeStruct(wout.shape, F32),
                   jax.ShapeDtypeStruct(wpool.shape, F32), jax.ShapeDtypeStruct((8, d), F32)],
        scratch_shapes=[pltpu.VMEM((tm + 2 * HALO, da), F32), pltpu.VMEM((tm + 2 * HALO, da), F32)],
        args=(dx, y, u, u, u, vec, conv, wpool, scale, wout), exch=exch)


def _ab_bwd_in(dpre, u, x, dx, vec, conv, win, tm, exch=None):
    t_total, d = x.shape
    nu = u.shape[1]
    da = nu // 4
    gw = da // len(POOL_WINDOWS)
    nsteps = t_total // tm

    def body(dp_ref, dc_ref, dn_ref, up_ref, uc_ref, un_ref, x_ref, dx_ref, vec_ref, conv_ref, win_ref,
             dxin_ref, dwin_ref, dconv_ref, acc_ref, dcq_ext, q_ext, dpl_ext):
        _zero_at_start(dwin_ref, dconv_ref, acc_ref)
        i = pl.program_id(0)

        def ucols(ref, k):
            return ref[:, k * da:(k + 1) * da].astype(F32)

        def dcols(ref, k):
            return ref[:, k * da:(k + 1) * da].astype(F32)

        _fill_ext(dcq_ext, dcols(dp_ref, 1), dcols(dc_ref, 1), dcols(dn_ref, 1), i, nsteps, tm)
        _fill_ext(q_ext, ucols(up_ref, 1) * ucols(up_ref, 2), ucols(uc_ref, 1) * ucols(uc_ref, 2),
                  ucols(un_ref, 1) * ucols(un_ref, 2), i, nsteps, tm)
        _fill_ext(dpl_ext, dcols(dp_ref, 2), dcols(dc_ref, 2), dcols(dn_ref, 2), i, nsteps, tm)
        dq = (conv_ref[0:1, :] * dcq_ext[HALO + 1:HALO + 1 + tm, :] + conv_ref[1:2, :] * dcq_ext[HALO:HALO + tm, :]
              + conv_ref[2:3, :] * dcq_ext[HALO - 1:HALO - 1 + tm, :])
        dcq = dcq_ext[HALO:HALO + tm, :]
        for k in range(3):
            dconv_ref[k:k + 1, :] += _colsum(dcq * q_ext[HALO + k - 1:HALO + k - 1 + tm, :])
        dcg = dq * ucols(uc_ref, 2)
        dv = dq * ucols(uc_ref, 1)
        t_ext = i * tm - HALO + lax.broadcasted_iota(jnp.int32, (tm + 2 * HALO, 1), 0)
        dps = []
        for g, wdw in enumerate(POOL_WINDOWS):
            left = wdw // 2
            right = wdw - 1 - left
            lo, hi = g * gw, (g + 1) * gw
            dpg = dpl_ext[HALO:HALO + tm, lo:hi]
            dpl_ext[:, lo:hi] = dpl_ext[:, lo:hi] / _window_count(t_ext, wdw, t_total)
            s = dpl_ext[HALO - right:HALO - right + tm, lo:hi]
            for o in range(-right + 1, left + 1):
                s = s + dpl_ext[HALO + o:HALO + o + tm, lo:hi]
            dps.append(s - dpg)
        du = jnp.concatenate([dcols(dc_ref, 0), dcg, dv] + dps, axis=1).astype(BF16)
        xv = x_ref[...]
        g, sh, sc = vec_ref[0:1, :], vec_ref[1:2, :], vec_ref[2:3, :]
        h = _norm_mod(xv, g, sh, sc).astype(BF16)
        dwin_ref[...] += _dot_tn(h, du)
        dh = _dot_nt(du, win_ref[...])
        dxn, dsh, dsc, dg = _norm_mod_bwd(dh, xv, g, sc)
        acc_ref[0:1, :] += dsh
        acc_ref[1:2, :] += dsc
        acc_ref[2:3, :] += dg
        dxin_ref[...] = dx_ref[...] + dxn

    ext = pltpu.VMEM((tm + 2 * HALO, da), F32)
    return _call(
        body, name="ab_bwd_in", nsteps=nsteps,
        in_specs=[*_halo_specs(tm, 3 * da, t_total), *_halo_specs(tm, nu, t_total), _rows(tm, d), _rows(tm, d),
                  _full(vec.shape), _full(conv.shape), _VM],
        out_specs=[_rows(tm, d), _VM, _VM, _VM],
        out_shape=[jax.ShapeDtypeStruct((t_total, d), F32), jax.ShapeDtypeStruct(win.shape, F32),
                   jax.ShapeDtypeStruct((8, da), F32), jax.ShapeDtypeStruct((8, d), F32)],
        scratch_shapes=[ext, ext, ext],
        args=(dpre, dpre, dpre, u, u, u, x, dx, vec, conv, win), exch=exch)


def _cf_bwd_out(dx, y, zc, vec, wpw2, tm, exch=None):
    t_total, d = dx.shape

    def body(dx_ref, y_ref, zc_ref, vec_ref, w_ref, dzc_ref, dw_ref, acc_ref):
        _zero_at_start(dw_ref, acc_ref)
        dxv = dx_ref[...]
        acc_ref[0:1, :] += _colsum(dxv * y_ref[...].astype(F32))
        dyf = dxv * vec_ref[0:1, :]
        acc_ref[1:2, :] += _colsum(dyf)
        dy = dyf.astype(BF16)
        zn, rstd = _layer_norm_stats(zc_ref[...].astype(F32))
        lng = vec_ref[1:2, :]
        zl = zn * lng + vec_ref[2:3, :]
        sg = _sigmoid(zl)
        dw_ref[...] += _dot_tn((zl * sg).astype(BF16), dy)
        dzl = _dot_nt(dy, w_ref[...]) * (sg * (1.0 + zl * (1.0 - sg)))
        acc_ref[2:3, :] += _colsum(dzl * zn)
        acc_ref[3:4, :] += _colsum(dzl)
        dzn = dzl * lng
        dzc = rstd * (dzn - jnp.mean(dzn, axis=-1, keepdims=True)
                      - zn * jnp.mean(dzn * zn, axis=-1, keepdims=True))
        acc_ref[4:5, :] += _colsum(dzc)
        dzc_ref[...] = dzc.astype(BF16)

    return _call(
        body, name="cf_bwd_out", nsteps=t_total // tm,
        in_specs=[_rows(tm, d), _rows(tm, d), _rows(tm, d), _full(vec.shape), _VM],
        out_specs=[_rows(tm, d), _VM, _VM],
        out_shape=[jax.ShapeDtypeStruct((t_total, d), BF16), jax.ShapeDtypeStruct(wpw2.shape, F32),
                   jax.ShapeDtypeStruct((8, d), F32)],
        args=(dx, y, zc, vec, wpw2), exch=exch)


def _cf_bwd_in(dzc, u, x, dx, vec, wdw, wpw1, tm, exch=None):
    t_total, d = x.shape
    nu = u.shape[1]
    nsteps = t_total // tm
    left = (CONF_KERNEL - 1) // 2

    def body(dp_ref, dc_ref, dn_ref, up_ref, uc_ref, un_ref, x_ref, dx_ref, vec_ref, wdw_ref, w_ref,
             dxin_ref, dw_ref, dwdw_ref, db1_ref, acc_ref, dzc_ext, z_ext, sh_ref, dz_buf):
        _zero_at_start(dw_ref, dwdw_ref, db1_ref, acc_ref)
        i = pl.program_id(0)
        _fill_ext(dzc_ext, dp_ref[...].astype(F32), dc_ref[...].astype(F32), dn_ref[...].astype(F32),
                  i, nsteps, tm)
        _glu_ext(up_ref, uc_ref, un_ref, z_ext, i, nsteps, tm)
        for lo in range(0, d, CONV_COLS):
            hi = lo + CONV_COLS
            _fill_shifts(sh_ref, dzc_ext, lo, hi, tm)
            acc = wdw_ref[0:1, lo:hi] * _shifted(sh_ref, left, tm)
            for k in range(1, CONF_KERNEL):
                acc = acc + wdw_ref[k:k + 1, lo:hi] * _shifted(sh_ref, left - k, tm)
            dz_buf[:, lo:hi] = acc
            dzc = dzc_ext[HALO:HALO + tm, lo:hi]
            _fill_shifts(sh_ref, z_ext, lo, hi, tm)
            for k in range(CONF_KERNEL):
                dwdw_ref[k:k + 1, lo:hi] += _colsum(dzc * _shifted(sh_ref, k - left, tm))
        dz = dz_buf[...]
        av = uc_ref[:, 0:d].astype(F32)
        sg = _sigmoid(uc_ref[:, d:2 * d].astype(F32))
        duf = jnp.concatenate([dz * sg, dz * av * sg * (1.0 - sg)], axis=1)
        db1_ref[0:1, :] += _colsum(duf)
        du = duf.astype(BF16)
        xv = x_ref[...]
        g, sh, sc = vec_ref[0:1, :], vec_ref[1:2, :], vec_ref[2:3, :]
        h = _norm_mod(xv, g, sh, sc).astype(BF16)
        dw_ref[...] += _dot_tn(h, du)
        dh = _dot_nt(du, w_ref[...])
        dxn, dsh, dsc, dg = _norm_mod_bwd(dh, xv, g, sc)
        acc_ref[0:1, :] += dsh
        acc_ref[1:2, :] += dsc
        acc_ref[2:3, :] += dg
        dxin_ref[...] = dx_ref[...] + dxn

    ext = pltpu.VMEM((tm + 2 * HALO, d), F32)
    return _call(
        body, name="cf_bwd_in", nsteps=nsteps,
        in_specs=[*_halo_specs(tm, d, t_total), *_halo_specs(tm, nu, t_total), _rows(tm, d), _rows(tm, d),
                  _full(vec.shape), _full(wdw.shape), _VM],
        out_specs=[_rows(tm, d), _VM, _VM, _VM, _VM],
        out_shape=[jax.ShapeDtypeStruct((t_total, d), F32), jax.ShapeDtypeStruct(wpw1.shape, F32),
                   jax.ShapeDtypeStruct((32, d), F32), jax.ShapeDtypeStruct((8, nu), F32),
                   jax.ShapeDtypeStruct((8, d), F32)],
        scratch_shapes=[ext, ext, _shift_scratch(tm), pltpu.VMEM((tm, d), F32)],
        args=(dzc, dzc, dzc, u, u, u, x, dx, vec, wdw, wpw1), exch=exch)


def _mod_fwd(c_all, w_mod, b_cols):
    nl, d, ncol = w_mod.shape
    nb = c_all.shape[0]

    def body(c_ref, w_ref, b_ref, o_ref):
        cv = c_ref[...]
        ca = cv * _sigmoid(cv)
        o_ref[0] = jnp.dot(ca, w_ref[0], preferred_element_type=F32, precision=HIGHEST) + b_ref[0]

    return _pcall(
        body, name="mod_fwd", grid=(nl,),
        in_specs=[_full(c_all.shape), pl.BlockSpec((1, d, ncol), lambda l: (l, 0, 0)),
                  pl.BlockSpec((1, 1, ncol), lambda l: (l, 0, 0))],
        out_specs=pl.BlockSpec((1, nb, ncol), lambda l: (l, 0, 0)),
        out_shape=jax.ShapeDtypeStruct((nl, nb, ncol), F32),
        compiler_params=_seq_params(),
    )(c_all, w_mod, b_cols.reshape(nl, 1, ncol))


def _mod_bwd(c_all_t, dmod_cols):
    d, nb = c_all_t.shape
    nl, _, ncol = dmod_cols.shape

    def body(c_ref, dm_ref, o_ref):
        cv = c_ref[...]
        ca = cv * _sigmoid(cv)
        o_ref[0] = jnp.dot(ca, dm_ref[0], preferred_element_type=F32, precision=HIGHEST)

    return _pcall(
        body, name="mod_bwd", grid=(nl,),
        in_specs=[_full(c_all_t.shape), pl.BlockSpec((1, nb, ncol), lambda l: (l, 0, 0))],
        out_specs=pl.BlockSpec((1, d, ncol), lambda l: (l, 0, 0)),
        out_shape=jax.ShapeDtypeStruct((nl, d, ncol), F32),
        compiler_params=_seq_params(),
    )(c_all_t, dmod_cols)


def _row_block(r, c):
    if r * c <= EW_BLOCK_ELEMS:
        return r
    best = None
    for br in range(8, r, 8):
        if r % br == 0 and br * c <= EW_BLOCK_ELEMS:
            best = br
    assert best is not None, (r, c)
    return best


def _as2d(a):
    return a.reshape(-1, a.shape[-1])


def _adamw(w, gparts, m, v):
    shape = w.shape
    w2, m2, v2 = _as2d(w), _as2d(m), _as2d(v)
    g2 = [_as2d(g) for g in gparts]
    r, c = w2.shape
    br = _row_block(r, c)
    ng = len(g2)

    def body(*refs):
        w_ref, m_ref, v_ref = refs[0:3]
        g_refs = refs[3:3 + ng]
        g = g_refs[0][...]
        for gr in g_refs[1:]:
            g = g + gr[...]
        _adamw_update(g, w_ref[...], m_ref[...], v_ref[...], refs[3 + ng:])

    spec = pl.BlockSpec((br, c), lambda i: (i, 0))
    outs = _pcall(
        body, name="adamw", grid=(r // br,),
        in_specs=[spec] * (3 + ng), out_specs=[spec] * 4,
        out_shape=[jax.ShapeDtypeStruct((r, c), F32)] * 4,
        compiler_params=_seq_params(),
    )(w2, m2, v2, *g2)
    return tuple(o.reshape(shape) for o in outs)


def _adamw_update(g, w, m, v, out_refs):
    go_ref, d_ref, mo_ref, vo_ref = out_refs
    mn = ADAM_B1 * m + (1.0 - ADAM_B1) * g
    vn = ADAM_B2 * v + (1.0 - ADAM_B2) * (g * g)
    m_hat = mn / (1.0 - ADAM_B1 ** ADAM_STEP)
    v_hat = vn / (1.0 - ADAM_B2 ** ADAM_STEP)
    go_ref[...] = g.reshape(go_ref.shape)
    d_ref[...] = (-ADAM_LR * (m_hat / (jnp.sqrt(v_hat) + ADAM_EPS) + ADAM_WD * w)).reshape(d_ref.shape)
    mo_ref[...] = mn.reshape(mo_ref.shape)
    vo_ref[...] = vn.reshape(vo_ref.shape)


def _adamw_partials(w, partials, m, v):
    nl, a, b = w.shape
    br = _row_block(a, b)
    nb = a // br

    def body(*refs):
        w_ref, m_ref, v_ref = refs[0:3]
        p_refs = refs[3:3 + nl]
        for layer in range(nl):
            @pl.when(pl.program_id(0) == layer)
            def _(layer=layer):
                halves = []
                for core in range(2):
                    acc = p_refs[layer][core, 0].astype(F32)
                    for chip in range(1, N_CHIPS):
                        acc = acc + p_refs[layer][core, chip].astype(F32)
                    halves.append(acc)
                _adamw_update(halves[0] + halves[1], w_ref[0], m_ref[0], v_ref[0], refs[3 + nl:])

    def part_spec(layer):
        def index(l, i):
            return 0, 0, jnp.where(l == layer, i, jnp.where(l < layer, 0, nb - 1)), 0
        return pl.BlockSpec((2, N_CHIPS, br, b), index)

    spec = pl.BlockSpec((1, br, b), lambda l, i: (l, i, 0))
    return tuple(_pcall(
        body, name="adamw_partials", grid=(nl, nb),
        in_specs=[spec] * 3 + [part_spec(layer) for layer in range(nl)], out_specs=[spec] * 4,
        out_shape=[jax.ShapeDtypeStruct(w.shape, F32)] * 4,
        compiler_params=pltpu.CompilerParams(dimension_semantics=("arbitrary", "arbitrary"),
                                             vmem_limit_bytes=VMEM_LIMIT),
    )(w, m, v, *partials))


def _allgather8(block, with_sum):
    m_per, n = block.shape

    def body(x_ref, out_ref, *rest):
        if with_sum:
            sum_ref, send_sems, recv_sems, local_sem = rest
        else:
            send_sems, recv_sems, local_sem = rest
        x, y, c = _place()
        me, sibling = (x, y, c), (x, y, 1 - c)
        chips = [(1 - x, y), (x, 1 - y), (1 - x, 1 - y)]

        def rows(px, py, pc):
            return out_ref.at[pl.ds((4 * px + 2 * py + pc) * m_per, m_per), :]

        def copy(k, blk, to, src=None):
            return pltpu.make_async_remote_copy(
                src_ref=rows(*blk) if src is None else src, dst_ref=rows(*blk),
                send_sem=send_sems.at[k], recv_sem=recv_sems.at[k], device_id=to, device_id_type=MESH)

        mine = pltpu.make_async_copy(x_ref, rows(*me), local_sem)
        mine.start()
        first = [copy(0, me, sibling, src=x_ref)]
        first += [copy(1 + j, me, (*chip, c), src=x_ref) for j, chip in enumerate(chips)]
        for cp in first:
            cp.start()
        passed = [copy(4 + j, (*chip, c), sibling) for j, chip in enumerate(chips)]
        for j, chip in enumerate(chips):
            copy(1 + j, (*chip, c), me).wait_recv()
            passed[j].start()
        copy(0, sibling, me).wait_recv()
        for j, chip in enumerate(chips):
            copy(4 + j, (*chip, 1 - c), me).wait_recv()
        for cp in first + passed:
            cp.wait_send()
        mine.wait()
        if with_sum:
            acc = out_ref[0:m_per, :]
            for k in range(1, N_DEV):
                acc = acc + out_ref[k * m_per:(k + 1) * m_per, :]
            sum_ref[...] = acc

    out_shape = [jax.ShapeDtypeStruct((N_DEV * m_per, n), F32)]
    out_specs = [_VM]
    if with_sum:
        out_shape.append(jax.ShapeDtypeStruct((m_per, n), F32))
        out_specs.append(_VM)
    res = _pcall(
        body, name="allgather8_sum" if with_sum else "allgather8",
        in_specs=[_VM], out_specs=out_specs, out_shape=out_shape,
        scratch_shapes=[pltpu.SemaphoreType.DMA((7,)), pltpu.SemaphoreType.DMA((7,)), pltpu.SemaphoreType.DMA],
        compiler_params=pltpu.CompilerParams(vmem_limit_bytes=VMEM_LIMIT),
    )(block)
    return res if with_sum else res[0]


def _cols_to_chips(w):
    *lead, a, nb = w.shape
    w = w.reshape(*lead, a, N_CHIPS, nb // N_CHIPS)
    return jnp.moveaxis(w, -2, 0)


def _chips_to_cols(g):
    g = jnp.moveaxis(g, 0, -2)
    *lead, a, k, b = g.shape
    return g.reshape(*lead, a, k * b)


def _my_cols(full, chip):
    w = full.shape[-1] // N_CHIPS
    return lax.dynamic_slice_in_dim(full, chip * w, w, axis=full.ndim - 1)


def _pad_rows(a, rows):
    return jnp.pad(a, ((0, rows - a.shape[0]), (0, 0)))


def _to_lanes(a):
    flat = a.reshape(-1)
    n = -(-flat.shape[0] // (8 * LANES)) * (8 * LANES)
    return jnp.pad(flat, (0, n - flat.shape[0])).reshape(-1, LANES)


class _Packer:
    def __init__(self):
        self.items = []
        self.rows = 0

    def add(self, name, a):
        lanes = _to_lanes(a)
        self.items.append((name, self.rows, a.shape, lanes))
        self.rows += lanes.shape[0]

    def pack(self):
        total = -(-self.rows // 8) * 8
        return _pad_rows(jnp.concatenate([it[3] for it in self.items], axis=0), total)

    def unpack(self, buf):
        out = {}
        for name, row, shape, lanes in self.items:
            size = 1
            for s in shape:
                size *= s
            out[name] = buf[row:row + lanes.shape[0]].reshape(-1)[:size].reshape(shape)
        return out


TM_SEQ = 512
TM_FFN = 256


LAYER_KEYS = ("in", "out", "gate", "up", "down")
COL_KEYS = ("in", "gate", "up")


def _layer_big_names(layer):
    i = layer // 2
    mix = (("ab_w_in", i), ("ab_w_out", i)) if layer % 2 == 0 else (("cf_w_pw1", i), ("cf_w_pw2", i))
    return dict(zip(LAYER_KEYS, mix + (("ffn_w_gate", layer), ("ffn_w_up", layer), ("ffn_w_down", layer))))


def _unpack_weight(key, g):
    g = g.reshape(N_CHIPS, -1, g.shape[-1])
    return _chips_to_cols(g) if key in COL_KEYS else g.reshape(-1, g.shape[-1])


def _chunk_grad(key, dw):
    parts = _cols_to_chips(dw) if key in COL_KEYS else dw.reshape(N_CHIPS, -1, dw.shape[-1])
    return parts.astype(BF16)


def _local_step(x, target, mods, p, shards):
    t_total, d = x.shape
    depth = mods.shape[0]
    tm = min(TM_SEQ, t_total)
    tmf = min(TM_FFN, t_total)
    saved = []
    xin = x
    weights = [{} for _ in range(depth)]

    def carried(stage, layer):
        if layer == 0:
            return {"in": (0, ("out", "gate")), "mix": (0, ("up", "down")), "ffn": (1, ("in", "out", "gate", "up"))}[stage]
        return {"in": (layer, ("down",)), "mix": (layer + 1, ("in", "out")), "ffn": (layer + 1, ("gate", "up"))}[stage]

    def gather(stage, layer):
        of, keys = carried(stage, layer)
        if of >= depth:
            return None
        return _Gather([shards[of][k].reshape(2, -1, shards[of][k].shape[-1]) for k in keys])

    def keep(stage, layer, arrs):
        of, keys = carried(stage, layer)
        for k, g in zip(keys, arrs):
            weights[of][k] = _unpack_weight(k, g)

    weights[0]["in"] = _unpack_weight("in", _standalone(_Gather([shards[0]["in"].reshape(2, -1, shards[0]["in"].shape[-1])]))[0])
    for layer in range(depth):
        i = layer // 2
        lw = weights[layer]
        sh1, sc1, g1, sh2, sc2, g2 = (mods[layer, k:k + 1] for k in range(6))
        vec_in = jnp.concatenate([p["norm_mix_g"][layer:layer + 1], sh1, sc1], axis=0)
        bias = None if layer % 2 == 0 else p["cf_b_pw1"][i:i + 1]
        (u,), arrived = _in_proj(xin, vec_in, lw["in"], bias, tm, exch=gather("in", layer))
        keep("in", layer, arrived)
        if layer % 2 == 0:
            (y, x2), arrived = _ab_fwd(u, xin, g1, p["ab_conv"][i], p["ab_w_pool"][i].astype(BF16),
                                       p["ab_pool_scale"][i:i + 1], lw["out"], tm, exch=gather("mix", layer))
            zc = None
        else:
            vec_cf = jnp.concatenate([g1, p["cf_b_dw"][i:i + 1], p["cf_ln_g"][i:i + 1], p["cf_ln_b"][i:i + 1],
                                      p["cf_b_pw2"][i:i + 1]], axis=0)
            (zc, y, x2), arrived = _cf_fwd(u, xin, vec_cf, _pad_rows(p["cf_w_dw"][i], 32), lw["out"], tm,
                                           exch=gather("mix", layer))
        keep("mix", layer, arrived)
        vec_ffn = jnp.concatenate([p["norm_ffn_g"][layer:layer + 1], sh2, sc2, g2], axis=0)
        (a, b, fout, x3), arrived = _ffn_fwd(x2, vec_ffn, lw["gate"], lw["up"], lw["down"], tmf,
                                             exch=gather("ffn", layer))
        keep("ffn", layer, arrived)
        saved.append((xin, u, y, zc, x2, a, b, fout))
        xin = x3

    (dx, fin), _ = _final_fwd_bwd(xin, target, p["final_norm_g"].reshape(1, d), tm)
    grads = {"final_norm_g": fin[0], "loss": fin[1, 0:1]}
    per_layer = {k: [None] * depth for k in ("norm_mix_g", "norm_ffn_g")}
    half = {k: [None] * (depth // 2) for k in (
        "ab_conv", "ab_w_pool", "ab_pool_scale", "cf_b_pw1", "cf_w_dw", "cf_b_dw", "cf_ln_g", "cf_ln_b", "cf_b_pw2")}
    dmods = [None] * depth
    received = {}
    pending = None
    for layer in reversed(range(depth)):
        i = layer // 2
        lw = weights[layer]
        xin, u, y, zc, x2, a, b, fout = saved[layer]
        sh1, sc1, g1, sh2, sc2, g2 = (mods[layer, k:k + 1] for k in range(6))
        above = _Scatter([pending]) if pending is not None else None
        (da, db, dwd, acc_d), arrived = _ffn_bwd_down(dx, fout, a, b, g2, lw["down"], tmf, exch=above)
        if pending is not None:
            received[(layer + 1, "in")] = arrived[0]
        vec_ffn = jnp.concatenate([p["norm_ffn_g"][layer:layer + 1], sh2, sc2], axis=0)
        (dx2, dwg, dwu, acc_u), arrived = _ffn_bwd_up(da, db, x2, dx, vec_ffn, lw["gate"], lw["up"], tmf,
                                                      exch=_Scatter([_chunk_grad("down", dwd)]))
        received[(layer, "down")] = arrived[0]
        per_layer["norm_ffn_g"][layer] = acc_u[2]
        vec_in = jnp.concatenate([p["norm_mix_g"][layer:layer + 1], sh1, sc1], axis=0)
        send_gate = _Scatter([_chunk_grad("gate", dwg)])
        if layer % 2 == 0:
            (dpre, dwout, dwpool, acc_o), arrived = _ab_bwd_out(
                dx2, y, u, g1, p["ab_conv"][i], p["ab_w_pool"][i].astype(BF16), p["ab_pool_scale"][i:i + 1],
                lw["out"], tm, exch=send_gate)
            received[(layer, "gate")] = arrived[0]
            send_up_out = _Scatter([_chunk_grad("up", dwu), _chunk_grad("out", dwout)])
            (dx, dwin, dconv, acc_i), arrived = _ab_bwd_in(dpre, u, xin, dx2, vec_in, p["ab_conv"][i], lw["in"], tm,
                                                           exch=send_up_out)
            half["ab_w_pool"][i] = dwpool
            half["ab_pool_scale"][i] = acc_o[1, 0:d // 2]
            half["ab_conv"][i] = dconv[0:3]
        else:
            vec_cf = jnp.concatenate([g1, p["cf_ln_g"][i:i + 1], p["cf_ln_b"][i:i + 1]], axis=0)
            (dzc, dwout, acc_o), arrived = _cf_bwd_out(dx2, y, zc, vec_cf, lw["out"], tm, exch=send_gate)
            received[(layer, "gate")] = arrived[0]
            send_up_out = _Scatter([_chunk_grad("up", dwu), _chunk_grad("out", dwout)])
            (dx, dwin, dwdw, db1, acc_i), arrived = _cf_bwd_in(
                dzc, u, xin, dx2, vec_in, _pad_rows(p["cf_w_dw"][i], 32), lw["in"], tm, exch=send_up_out)
            half["cf_b_pw2"][i] = acc_o[1]
            half["cf_ln_g"][i] = acc_o[2]
            half["cf_ln_b"][i] = acc_o[3]
            half["cf_b_dw"][i] = acc_o[4]
            half["cf_w_dw"][i] = dwdw[0:CONF_KERNEL]
            half["cf_b_pw1"][i] = db1[0]
        received[(layer, "up")], received[(layer, "out")] = arrived
        per_layer["norm_mix_g"][layer] = acc_i[2]
        dmods[layer] = jnp.stack([acc_i[0], acc_i[1], acc_o[0], acc_u[0], acc_u[1], acc_d[0]], axis=0)
        pending = _chunk_grad("in", dwin)
    received[(0, "in")] = _standalone(_Scatter([pending]))[0]
    for k, v in {**per_layer, **half}.items():
        grads[k] = jnp.stack(v, axis=0)
    return dx, grads, jnp.stack(dmods, axis=0), received


SMALL_COLS = ("ab_conv", "cf_b_pw1", "cf_w_dw", "cf_b_dw", "cf_ln_g", "cf_ln_b", "cf_b_pw2")
SMALL_REPL = ("norm_mix_g", "norm_ffn_g", "ab_w_pool", "ab_pool_scale", "final_norm_g")
WEIGHTS = ("norm_mix_g", "norm_ffn_g", "w_mod", "b_mod", "ab_w_in", "ab_conv", "ab_w_pool", "ab_pool_scale",
           "ab_w_out", "cf_w_pw1", "cf_b_pw1", "cf_w_dw", "cf_b_dw", "cf_ln_g", "cf_ln_b", "cf_w_pw2",
           "cf_b_pw2", "ffn_w_gate", "ffn_w_up", "ffn_w_down", "final_norm_g")


def kernel(x, c, norm_mix_g, norm_ffn_g, w_mod, b_mod, ab_w_in, ab_conv, ab_w_pool, ab_pool_scale, ab_w_out, cf_w_pw1, cf_b_pw1, cf_w_dw, cf_b_dw, cf_ln_g, cf_ln_b, cf_w_pw2, cf_b_pw2, ffn_w_gate, ffn_w_up, ffn_w_down, final_norm_g, loss_target, m_norm_mix_g, m_norm_ffn_g, m_w_mod, m_b_mod, m_ab_w_in, m_ab_conv, m_ab_w_pool, m_ab_pool_scale, m_ab_w_out, m_cf_w_pw1, m_cf_b_pw1, m_cf_w_dw, m_cf_b_dw, m_cf_ln_g, m_cf_ln_b, m_cf_w_pw2, m_cf_b_pw2, m_ffn_w_gate, m_ffn_w_up, m_ffn_w_down, m_final_norm_g, v_norm_mix_g, v_norm_ffn_g, v_w_mod, v_b_mod, v_ab_w_in, v_ab_conv, v_ab_w_pool, v_ab_pool_scale, v_ab_w_out, v_cf_w_pw1, v_cf_b_pw1, v_cf_w_dw, v_cf_b_dw, v_cf_ln_g, v_cf_ln_b, v_cf_w_pw2, v_cf_b_pw2, v_ffn_w_gate, v_ffn_w_up, v_ffn_w_down, v_final_norm_g):
    w = dict(norm_mix_g=norm_mix_g, norm_ffn_g=norm_ffn_g, w_mod=w_mod, b_mod=b_mod, ab_w_in=ab_w_in,
             ab_conv=ab_conv, ab_w_pool=ab_w_pool, ab_pool_scale=ab_pool_scale, ab_w_out=ab_w_out,
             cf_w_pw1=cf_w_pw1, cf_b_pw1=cf_b_pw1, cf_w_dw=cf_w_dw, cf_b_dw=cf_b_dw, cf_ln_g=cf_ln_g,
             cf_ln_b=cf_ln_b, cf_w_pw2=cf_w_pw2, cf_b_pw2=cf_b_pw2, ffn_w_gate=ffn_w_gate, ffn_w_up=ffn_w_up,
             ffn_w_down=ffn_w_down, final_norm_g=final_norm_g)
    mom = dict(norm_mix_g=m_norm_mix_g, norm_ffn_g=m_norm_ffn_g, w_mod=m_w_mod, b_mod=m_b_mod, ab_w_in=m_ab_w_in,
               ab_conv=m_ab_conv, ab_w_pool=m_ab_w_pool, ab_pool_scale=m_ab_pool_scale, ab_w_out=m_ab_w_out,
               cf_w_pw1=m_cf_w_pw1, cf_b_pw1=m_cf_b_pw1, cf_w_dw=m_cf_w_dw, cf_b_dw=m_cf_b_dw, cf_ln_g=m_cf_ln_g,
               cf_ln_b=m_cf_ln_b, cf_w_pw2=m_cf_w_pw2, cf_b_pw2=m_cf_b_pw2, ffn_w_gate=m_ffn_w_gate,
               ffn_w_up=m_ffn_w_up, ffn_w_down=m_ffn_w_down, final_norm_g=m_final_norm_g)
    var = dict(norm_mix_g=v_norm_mix_g, norm_ffn_g=v_norm_ffn_g, w_mod=v_w_mod, b_mod=v_b_mod, ab_w_in=v_ab_w_in,
               ab_conv=v_ab_conv, ab_w_pool=v_ab_w_pool, ab_pool_scale=v_ab_pool_scale, ab_w_out=v_ab_w_out,
               cf_w_pw1=v_cf_w_pw1, cf_b_pw1=v_cf_b_pw1, cf_w_dw=v_cf_w_dw, cf_b_dw=v_cf_b_dw, cf_ln_g=v_cf_ln_g,
               cf_ln_b=v_cf_ln_b, cf_w_pw2=v_cf_w_pw2, cf_b_pw2=v_cf_b_pw2, ffn_w_gate=v_ffn_w_gate,
               ffn_w_up=v_ffn_w_up, ffn_w_down=v_ffn_w_down, final_norm_g=v_final_norm_g)
    px, py, pc = _place()
    chip = 2 * px + py
    dev = 2 * chip + pc
    depth, d, mod_cols = w_mod.shape
    x = x[0]
    target = loss_target[0]

    small_in = _Packer()
    small_in.add("c", c)
    for name in SMALL_COLS:
        small_in.add(name, w[name])
    gathered = _allgather8(small_in.pack(), with_sum=False).reshape(N_DEV, -1, LANES)
    per_dev = [small_in.unpack(gathered[k]) for k in range(N_DEV)]
    c_all = jnp.concatenate([pd["c"] for pd in per_dev], axis=0)
    params = {name: jnp.concatenate([per_dev[2 * k][name] for k in range(N_CHIPS)], axis=-1)
              for name in SMALL_COLS}
    for name in SMALL_REPL:
        params[name] = w[name]

    mod_part = _mod_fwd(c_all, w_mod, _my_cols(b_mod, chip))
    mod_all = _allgather8(mod_part.reshape(-1, LANES), with_sum=False)
    mod_all = mod_all.reshape(N_CHIPS, 2, depth, N_DEV, mod_cols)[:, 0]
    mod_all = jnp.moveaxis(mod_all, 0, 2).reshape(depth, N_DEV, N_CHIPS * mod_cols)
    mods = lax.dynamic_index_in_dim(mod_all, dev, axis=1, keepdims=False).reshape(depth, 6, d)

    shards = [{k: w[name][idx].astype(BF16) for k, (name, idx) in _layer_big_names(layer).items()}
              for layer in range(depth)]
    grad_x, grads, dmods, received = _local_step(x, target, mods, params, shards)

    small_out = _Packer()
    small_out.add("dmods", dmods)
    for name in ("loss",) + SMALL_REPL + SMALL_COLS:
        small_out.add(name, grads[name])
    parts_all, parts_sum = _allgather8(small_out.pack(), with_sum=True)
    small_sum = small_out.unpack(parts_sum)
    loss = small_sum["loss"][0]
    dmods_all = jnp.stack([small_out.unpack(pa)["dmods"] for pa in parts_all.reshape(N_DEV, -1, LANES)], axis=1)
    dmods_all = dmods_all.reshape(depth, N_DEV, 6 * d)

    g_final = {}
    g_final["w_mod"] = [_mod_bwd(c_all.T, _my_cols(dmods_all, chip))]
    g_final["b_mod"] = [small_sum["dmods"].reshape(depth, 6 * d)]
    for name in SMALL_REPL:
        g_final[name] = [small_sum[name]]
    for name in SMALL_COLS:
        g_final[name] = [_my_cols(small_sum[name], chip)]

    partials = {}
    for layer in range(depth):
        for k, (name, _) in _layer_big_names(layer).items():
            partials.setdefault(name, []).append(received[(layer, k)])
    out_g, out_d, out_m, out_v = [], [], [], []
    for name in WEIGHTS:
        if name in partials:
            g, dlt, mn, vn = _adamw_partials(w[name], partials[name], mom[name], var[name])
        else:
            g, dlt, mn, vn = _adamw(w[name], g_final[name], mom[name], var[name])
        out_g.append(g)
        out_d.append(dlt)
        out_m.append(mn)
        out_v.append(vn)
    return (loss, grad_x[None], *out_g, *out_d, *out_m, *out_v)
```

```python
import functools

import jax
import jax.numpy as jnp
from jax import lax
from jax.experimental import pallas as pl
from jax.experimental.pallas import tpu as pltpu

F32 = jnp.float32
BF16 = jnp.bfloat16
RMS_EPS = 1e-6
LN_EPS = 1e-5
ADAM_LR = 0.001
ADAM_B1 = 0.9
ADAM_B2 = 0.999
ADAM_EPS = 1e-08
ADAM_WD = 0.01
ADAM_STEP = 10
POOL_WINDOWS = (2, 4, 8, 16)
CONF_KERNEL = 31
N_CHIPS = 4
N_DEV = 8
HALO = 16
CONV_COLS = 256
FFN_CHUNK = 1536
LANES = 1024
VMEM_LIMIT = 56 * 1024 * 1024
EW_BLOCK_ELEMS = 256 * 1024
MESH = pl.DeviceIdType.MESH
HIGHEST = lax.Precision.HIGHEST

_pcall = pl.pallas_call


def _dot(a, b):
    return jnp.dot(a, b, preferred_element_type=F32)


def _dot_tn(a, b):
    return lax.dot_general(a, b, (((0,), (0,)), ((), ())), preferred_element_type=F32)


def _dot_nt(a, b):
    return lax.dot_general(a, b, (((1,), (1,)), ((), ())), preferred_element_type=F32)


def _colsum(v):
    return jnp.sum(v, axis=0, keepdims=True)


def _sigmoid(v):
    return 1.0 / (1.0 + jnp.exp(-v))


def _rows(tm, c):
    return pl.BlockSpec((tm, c), lambda i: (i, 0))


def _full(shape):
    nd = len(shape)
    return pl.BlockSpec(shape, lambda i: (0,) * nd)


_VM = pl.BlockSpec(memory_space=pltpu.VMEM)
_ANY = pl.BlockSpec(memory_space=pl.ANY)


def _halo_specs(tm, c, t_total):
    r = tm // HALO
    last = t_total // HALO - 1
    prev = pl.BlockSpec((HALO, c), lambda i: (jnp.maximum(i * r - 1, 0), 0))
    nxt = pl.BlockSpec((HALO, c), lambda i: (jnp.minimum((i + 1) * r, last), 0))
    return prev, _rows(tm, c), nxt


def _seq_params():
    return pltpu.CompilerParams(dimension_semantics=("arbitrary",), vmem_limit_bytes=VMEM_LIMIT)


def _place():
    return lax.axis_index("x"), lax.axis_index("y"), lax.axis_index("c")


def _peer_chips(x, y):
    return [(1 - x, y), (x, 1 - y), (1 - x, 1 - y)]


class _Gather:
    tag = "gather"

    def __init__(self, arrs):
        self.arrs = list(arrs)

    def out_shapes(self):
        return [jax.ShapeDtypeStruct((N_CHIPS,) + a.shape, a.dtype) for a in self.arrs]

    def sems(self):
        n = len(self.arrs)
        return [pltpu.SemaphoreType.DMA((3 * n,)) for _ in range(4)] + [pltpu.SemaphoreType.DMA((n,))]

    def _copies(self, ins, outs, sems, kinds):
        ici_send, ici_recv, d2d_send, d2d_recv, local_sems = sems
        x, y, c = _place()
        me = 2 * x + y
        found = {kind: [] for kind in kinds}
        for j in range(len(ins)):
            if "local" in kinds:
                found["local"].append(pltpu.make_async_copy(ins[j], outs[j].at[me], local_sems.at[j]))
            for k, (px, py) in enumerate(_peer_chips(x, y)):
                ici = dict(send_sem=ici_send.at[3 * j + k], recv_sem=ici_recv.at[3 * j + k],
                           device_id=(px, py, c), device_id_type=MESH)
                d2d = dict(send_sem=d2d_send.at[3 * j + k], recv_sem=d2d_recv.at[3 * j + k],
                           device_id=(x, y, 1 - c), device_id_type=MESH)
                theirs = outs[j].at[2 * px + py]
                if "send" in kinds:
                    found["send"].append(pltpu.make_async_remote_copy(
                        src_ref=ins[j].at[c], dst_ref=outs[j].at[me, c], **ici))
                if "arrival" in kinds:
                    found["arrival"].append(pltpu.make_async_remote_copy(
                        src_ref=ins[j].at[c], dst_ref=theirs.at[c], **ici))
                if "pass" in kinds:
                    found["pass"].append(pltpu.make_async_remote_copy(
                        src_ref=theirs.at[c], dst_ref=theirs.at[c], **d2d))
                if "passed" in kinds:
                    found["passed"].append(pltpu.make_async_remote_copy(
                        src_ref=theirs.at[c], dst_ref=theirs.at[1 - c], **d2d))
        return found

    def start(self, ins, outs, sems):
        found = self._copies(ins, outs, sems, ("local", "send"))
        for cp in found["local"] + found["send"]:
            cp.start()

    def mid(self, ins, outs, sems):
        found = self._copies(ins, outs, sems, ("arrival", "pass"))
        for arrived, onward in zip(found["arrival"], found["pass"]):
            arrived.wait_recv()
            onward.start()

    def wait(self, ins, outs, sems):
        found = self._copies(ins, outs, sems, ("local", "send", "pass", "passed"))
        for cp in found["passed"]:
            cp.wait_recv()
        for cp in found["send"] + found["pass"]:
            cp.wait_send()
        for cp in found["local"]:
            cp.wait()


class _Scatter:
    tag = "scatter"

    def __init__(self, arrs):
        self.arrs = list(arrs)

    def out_shapes(self):
        return [jax.ShapeDtypeStruct((2,) + a.shape, a.dtype) for a in self.arrs]

    def sems(self):
        n = len(self.arrs)
        dma = pltpu.SemaphoreType.DMA
        return [dma((3 * n,)), dma((3 * n,)), dma((4 * n,)), dma((4 * n,)), dma((n,))]

    def _copies(self, ins, outs, sems, kinds):
        ici_send, ici_recv, d2d_send, d2d_recv, local_sems = sems
        x, y, c = _place()
        me = 2 * x + y
        found = {kind: [] for kind in kinds}
        for j in range(len(ins)):
            def d2d(k):
                return dict(send_sem=d2d_send.at[4 * j + k], recv_sem=d2d_recv.at[4 * j + k],
                            device_id=(x, y, 1 - c), device_id_type=MESH)

            if "local" in kinds:
                found["local"].append(pltpu.make_async_copy(ins[j].at[me], outs[j].at[0, me], local_sems.at[j]))
            if "own" in kinds:
                found["own"].append(pltpu.make_async_remote_copy(
                    src_ref=ins[j].at[me], dst_ref=outs[j].at[1, me], **d2d(3)))
            if "passed" in kinds:
                found["passed"].append(pltpu.make_async_remote_copy(
                    src_ref=ins[j].at[me], dst_ref=outs[j].at[1, me], **d2d(3)))
            for k, (px, py) in enumerate(_peer_chips(x, y)):
                ici = dict(send_sem=ici_send.at[3 * j + k], recv_sem=ici_recv.at[3 * j + k],
                           device_id=(px, py, c), device_id_type=MESH)
                peer = 2 * px + py
                if "send" in kinds:
                    found["send"].append(pltpu.make_async_remote_copy(
                        src_ref=ins[j].at[peer], dst_ref=outs[j].at[0, me], **ici))
                if "arrival" in kinds:
                    found["arrival"].append(pltpu.make_async_remote_copy(
                        src_ref=ins[j].at[me], dst_ref=outs[j].at[0, peer], **ici))
                if "pass" in kinds:
                    found["pass"].append(pltpu.make_async_remote_copy(
                        src_ref=outs[j].at[0, peer], dst_ref=outs[j].at[1, peer], **d2d(k)))
                if "passed" in kinds:
                    found["passed"].append(pltpu.make_async_remote_copy(
                        src_ref=outs[j].at[0, peer], dst_ref=outs[j].at[1, peer], **d2d(k)))
        return found

    def start(self, ins, outs, sems):
        found = self._copies(ins, outs, sems, ("local", "own", "send"))
        for cp in found["local"] + found["own"] + found["send"]:
            cp.start()

    def mid(self, ins, outs, sems):
        found = self._copies(ins, outs, sems, ("arrival", "pass"))
        for arrived, onward in zip(found["arrival"], found["pass"]):
            arrived.wait_recv()
            onward.start()

    def wait(self, ins, outs, sems):
        found = self._copies(ins, outs, sems, ("local", "own", "send", "pass", "passed"))
        for cp in found["passed"]:
            cp.wait_recv()
        for cp in found["own"] + found["send"] + found["pass"]:
            cp.wait_send()
        for cp in found["local"]:
            cp.wait()


def _standalone(op):
    n = len(op.arrs)

    def body(*refs):
        for phase in (op.start, op.mid, op.wait):
            phase(refs[:n], refs[n:2 * n], refs[2 * n:])

    return _pcall(body, name="chip_" + op.tag, in_specs=[_ANY] * n, out_specs=[_ANY] * n,
                  out_shape=op.out_shapes(), scratch_shapes=op.sems())(*op.arrs)


def _call(body, *, name, nsteps, in_specs, out_specs, out_shape, args, scratch_shapes=(), exch=None):
    if exch is None:
        outs = _pcall(body, name=name, grid=(nsteps,), in_specs=list(in_specs), out_specs=list(out_specs),
                      out_shape=list(out_shape), scratch_shapes=list(scratch_shapes),
                      compiler_params=_seq_params())(*args)
        return list(outs), []
    n, ni, no, ns = len(exch.arrs), len(in_specs), len(out_specs), len(scratch_shapes)

    def hosted(*refs):
        xin = refs[ni:ni + n]
        xout = refs[ni + n + no:ni + 2 * n + no]
        scr = refs[ni + 2 * n + no:]

        @pl.when(pl.program_id(0) == 0)
        def _():
            exch.start(xin, xout, scr[ns:])

        body(*refs[:ni], *refs[ni + n:ni + n + no], *scr[:ns])

        @pl.when(pl.program_id(0) == max(nsteps - 3, 0))
        def _():
            exch.mid(xin, xout, scr[ns:])

        @pl.when(pl.program_id(0) == nsteps - 1)
        def _():
            exch.wait(xin, xout, scr[ns:])

    outs = _pcall(hosted, name=name + "_" + exch.tag, grid=(nsteps,),
                  in_specs=[*in_specs, *[_ANY] * n], out_specs=[*out_specs, *[_ANY] * n],
                  out_shape=[*out_shape, *exch.out_shapes()],
                  scratch_shapes=[*scratch_shapes, *exch.sems()],
                  compiler_params=_seq_params())(*args, *exch.arrs)
    return list(outs[:no]), list(outs[no:])


def _rms(x):
    r = lax.rsqrt(jnp.mean(x * x, axis=-1, keepdims=True) + RMS_EPS)
    return x * r, r


def _norm_mod(x, g, sh, sc):
    xhat, _ = _rms(x)
    return xhat * g * (1.0 + sc) + sh


def _norm_mod_bwd(dh, x, g, sc):
    xhat, r = _rms(x)
    n = xhat * g
    dsh = _colsum(dh)
    dsc = _colsum(dh * n)
    dn = dh * (1.0 + sc)
    dg = _colsum(dn * xhat)
    dxn = dn * g
    dx = r * (dxn - xhat * jnp.mean(dxn * xhat, axis=-1, keepdims=True))
    return dx, dsh, dsc, dg


def _fill_ext(ext_ref, prev, cur, nxt, i, nsteps, tm):
    ext_ref[0:HALO, :] = jnp.where(i > 0, prev, 0.0)
    ext_ref[HALO:HALO + tm, :] = cur
    ext_ref[HALO + tm:HALO + tm + HALO, :] = jnp.where(i < nsteps - 1, nxt, 0.0)


def _shift_scratch(tm):
    return pltpu.VMEM((8, tm + 2 * HALO - 8, CONV_COLS), F32)


def _fill_shifts(sh_ref, ext_ref, lo, hi, tm):
    for b in range(8):
        sh_ref[b] = ext_ref[b:b + tm + 2 * HALO - 8, lo:hi]


def _shifted(sh_ref, offset, tm):
    b = offset % 8
    start = HALO + offset - b
    return sh_ref[b, start:start + tm, :]


def _window_count(t, wdw, t_total):
    left = wdw // 2
    right = wdw - 1 - left
    cnt = jnp.minimum(t + right, t_total - 1) - jnp.maximum(t - left, 0) + 1
    return jnp.maximum(cnt, 1).astype(F32)


def _in_proj(x, vec, w, bias, tm, exch=None):
    t_total, d = x.shape
    n = w.shape[1]
    has_bias = bias is not None

    def body(*refs):
        if has_bias:
            x_ref, vec_ref, w_ref, b_ref, u_ref = refs
        else:
            x_ref, vec_ref, w_ref, u_ref = refs
        h = _norm_mod(x_ref[...], vec_ref[0:1, :], vec_ref[1:2, :], vec_ref[2:3, :])
        u = _dot(h.astype(BF16), w_ref[...])
        if has_bias:
            u = u + b_ref[...]
        u_ref[...] = u.astype(BF16)

    in_specs = [_rows(tm, d), _full(vec.shape), _VM]
    args = [x, vec, w]
    if has_bias:
        in_specs.append(_full(bias.shape))
        args.append(bias)
    return _call(
        body, name="in_proj_bias" if has_bias else "in_proj", nsteps=t_total // tm,
        in_specs=in_specs, out_specs=[_rows(tm, n)], out_shape=[jax.ShapeDtypeStruct((t_total, n), BF16)],
        args=args, exch=exch)


def _ab_core(up_ref, uc_ref, un_ref, conv_ref, wpool_ref, q_ext, p_ext, i, nsteps, tm, t_total):
    da = uc_ref.shape[1] // 4

    def cols(ref, k):
        return ref[:, k * da:(k + 1) * da].astype(F32)

    _fill_ext(q_ext, cols(up_ref, 1) * cols(up_ref, 2), cols(uc_ref, 1) * cols(uc_ref, 2),
              cols(un_ref, 1) * cols(un_ref, 2), i, nsteps, tm)
    _fill_ext(p_ext, cols(up_ref, 3), cols(uc_ref, 3), cols(un_ref, 3), i, nsteps, tm)
    bg = cols(uc_ref, 0)
    cq = (conv_ref[0:1, :] * q_ext[HALO - 1:HALO - 1 + tm, :] + conv_ref[1:2, :] * q_ext[HALO:HALO + tm, :]
          + conv_ref[2:3, :] * q_ext[HALO + 1:HALO + 1 + tm, :])
    t = i * tm + lax.broadcasted_iota(jnp.int32, (tm, 1), 0)
    gw = da // len(POOL_WINDOWS)
    pooled, ybpre = [], []
    for g, wdw in enumerate(POOL_WINDOWS):
        left = wdw // 2
        right = wdw - 1 - left
        lo, hi = g * gw, (g + 1) * gw
        s = p_ext[HALO - left:HALO - left + tm, lo:hi]
        for o in range(-left + 1, right + 1):
            s = s + p_ext[HALO + o:HALO + o + tm, lo:hi]
        pg = s / _window_count(t, wdw, t_total) - p_ext[HALO:HALO + tm, lo:hi]
        pooled.append(pg.astype(BF16))
        ybpre.append(_dot(pooled[-1], wpool_ref[g]))
    return bg, cq, pooled, jnp.concatenate(ybpre, axis=1)


def _ab_fwd(u, x, vec, conv, wpool, scale, wout, tm, exch=None):
    t_total, d = x.shape
    nu = u.shape[1]
    da = nu // 4
    nsteps = t_total // tm

    def body(up_ref, uc_ref, un_ref, x_ref, vec_ref, conv_ref, wpool_ref, scale_ref, wout_ref,
             y_ref, x2_ref, q_ext, p_ext):
        i = pl.program_id(0)
        bg, cq, _, ybpre = _ab_core(up_ref, uc_ref, un_ref, conv_ref, wpool_ref, q_ext, p_ext,
                                    i, nsteps, tm, t_total)
        cat = jnp.concatenate([bg * cq, ybpre * scale_ref[...]], axis=1).astype(BF16)
        y = _dot(cat, wout_ref[...])
        y_ref[...] = y.astype(BF16)
        x2_ref[...] = x_ref[...] + vec_ref[0:1, :] * y

    return _call(
        body, name="ab_fwd", nsteps=nsteps,
        in_specs=[*_halo_specs(tm, nu, t_total), _rows(tm, d), _full(vec.shape), _full(conv.shape),
                  _full(wpool.shape), _full(scale.shape), _VM],
        out_specs=[_rows(tm, d), _rows(tm, d)],
        out_shape=[jax.ShapeDtypeStruct((t_total, d), BF16), jax.ShapeDtypeStruct((t_total, d), F32)],
        scratch_shapes=[pltpu.VMEM((tm + 2 * HALO, da), F32), pltpu.VMEM((tm + 2 * HALO, da), F32)],
        args=(u, u, u, x, vec, conv, wpool, scale, wout), exch=exch)


def _glu_ext(up_ref, uc_ref, un_ref, z_ext, i, nsteps, tm):
    dz = uc_ref.shape[1] // 2

    def glu(ref):
        return ref[:, 0:dz].astype(F32) * _sigmoid(ref[:, dz:2 * dz].astype(F32))

    _fill_ext(z_ext, glu(up_ref), glu(uc_ref), glu(un_ref), i, nsteps, tm)


def _layer_norm_stats(zc):
    mu = jnp.mean(zc, axis=-1, keepdims=True)
    dlt = zc - mu
    rstd = lax.rsqrt(jnp.mean(dlt * dlt, axis=-1, keepdims=True) + LN_EPS)
    return dlt * rstd, rstd


def _cf_fwd(u, x, vec, wdw, wpw2, tm, exch=None):
    t_total, d = x.shape
    nu = u.shape[1]
    nsteps = t_total // tm
    left = (CONF_KERNEL - 1) // 2

    def body(up_ref, uc_ref, un_ref, x_ref, vec_ref, wdw_ref, wpw2_ref, zc_ref, y_ref, x2_ref, z_ext, sh_ref,
             zc_buf):
        i = pl.program_id(0)
        _glu_ext(up_ref, uc_ref, un_ref, z_ext, i, nsteps, tm)
        for lo in range(0, d, CONV_COLS):
            hi = lo + CONV_COLS
            _fill_shifts(sh_ref, z_ext, lo, hi, tm)
            acc = wdw_ref[0:1, lo:hi] * _shifted(sh_ref, -left, tm)
            for k in range(1, CONF_KERNEL):
                acc = acc + wdw_ref[k:k + 1, lo:hi] * _shifted(sh_ref, k - left, tm)
            zc_buf[:, lo:hi] = acc
        zc = zc_buf[...] + vec_ref[1:2, :]
        zc_ref[...] = zc.astype(BF16)
        zn, _ = _layer_norm_stats(zc)
        zl = zn * vec_ref[2:3, :] + vec_ref[3:4, :]
        zs = zl * _sigmoid(zl)
        y = _dot(zs.astype(BF16), wpw2_ref[...]) + vec_ref[4:5, :]
        y_ref[...] = y.astype(BF16)
        x2_ref[...] = x_ref[...] + vec_ref[0:1, :] * y

    return _call(
        body, name="cf_fwd", nsteps=nsteps,
        in_specs=[*_halo_specs(tm, nu, t_total), _rows(tm, d), _full(vec.shape), _full(wdw.shape), _VM],
        out_specs=[_rows(tm, d), _rows(tm, d), _rows(tm, d)],
        out_shape=[jax.ShapeDtypeStruct((t_total, d), BF16), jax.ShapeDtypeStruct((t_total, d), BF16),
                   jax.ShapeDtypeStruct((t_total, d), F32)],
        scratch_shapes=[pltpu.VMEM((tm + 2 * HALO, d), F32), _shift_scratch(tm), pltpu.VMEM((tm, d), F32)],
        args=(u, u, u, x, vec, wdw, wpw2), exch=exch)


def _ffn_chunks(f, width=FFN_CHUNK):
    return [(lo, min(lo + width, f)) for lo in range(0, f, width)]


def _ffn_fwd(x2, vec, wg, wu, wd, tm, exch=None):
    t_total, d = x2.shape
    f = wg.shape[0]

    def body(x_ref, vec_ref, wg_ref, wu_ref, wd_ref, a_ref, b_ref, f_ref, x3_ref):
        xv = x_ref[...]
        h = _norm_mod(xv, vec_ref[0:1, :], vec_ref[1:2, :], vec_ref[2:3, :]).astype(BF16)
        y = None
        for lo, hi in _ffn_chunks(f):
            a = _dot_nt(h, wg_ref[lo:hi, :])
            b = _dot_nt(h, wu_ref[lo:hi, :])
            a_ref[:, lo:hi] = a.astype(BF16)
            b_ref[:, lo:hi] = b.astype(BF16)
            s = (a * _sigmoid(a) * b).astype(BF16)
            part = _dot(s, wd_ref[lo:hi, :])
            y = part if y is None else y + part
        f_ref[...] = y.astype(BF16)
        x3_ref[...] = xv + vec_ref[3:4, :] * y

    return _call(
        body, name="ffn_fwd", nsteps=t_total // tm,
        in_specs=[_rows(tm, d), _full(vec.shape), _VM, _VM, _VM],
        out_specs=[_rows(tm, f), _rows(tm, f), _rows(tm, d), _rows(tm, d)],
        out_shape=[jax.ShapeDtypeStruct((t_total, f), BF16), jax.ShapeDtypeStruct((t_total, f), BF16),
                   jax.ShapeDtypeStruct((t_total, d), BF16), jax.ShapeDtypeStruct((t_total, d), F32)],
        args=(x2, vec, wg, wu, wd), exch=exch)


def _final_fwd_bwd(x, target, vec, tm):
    t_total, d = x.shape

    def body(x_ref, t_ref, vec_ref, dx_ref, acc_ref):
        @pl.when(pl.program_id(0) == 0)
        def _():
            acc_ref[...] = jnp.zeros_like(acc_ref)

        g = vec_ref[0:1, :]
        xhat, r = _rms(x_ref[...])
        e = xhat * g - t_ref[...]
        acc_ref[1:2, :] += jnp.zeros((1, d), F32) + 0.5 * jnp.sum(jnp.mean(e * e, axis=-1, keepdims=True))
        dout = e * (1.0 / d)
        acc_ref[0:1, :] += _colsum(dout * xhat)
        dxn = dout * g
        dx_ref[...] = r * (dxn - xhat * jnp.mean(dxn * xhat, axis=-1, keepdims=True))

    return _call(
        body, name="final_fwd_bwd", nsteps=t_total // tm,
        in_specs=[_rows(tm, d), _rows(tm, d), _full(vec.shape)],
        out_specs=[_rows(tm, d), _VM],
        out_shape=[jax.ShapeDtypeStruct((t_total, d), F32), jax.ShapeDtypeStruct((8, d), F32)],
        args=(x, target, vec))


def _zero_at_start(*refs):
    @pl.when(pl.program_id(0) == 0)
    def _():
        for ref in refs:
            ref[...] = jnp.zeros_like(ref)


def _ffn_bwd_down(dx3, fout, a, b, vec, wd, tm, exch=None):
    t_total, d = dx3.shape
    f = a.shape[1]

    def body(dx_ref, f_ref, a_ref, b_ref, vec_ref, wd_ref, da_ref, db_ref, dwd_ref, acc_ref):
        _zero_at_start(dwd_ref, acc_ref)
        dx = dx_ref[...]
        acc_ref[0:1, :] += _colsum(dx * f_ref[...].astype(F32))
        dy = (dx * vec_ref[0:1, :]).astype(BF16)
        for lo, hi in _ffn_chunks(f, FFN_CHUNK // 2):
            av = a_ref[:, lo:hi].astype(F32)
            bv = b_ref[:, lo:hi].astype(F32)
            sg = _sigmoid(av)
            silu = av * sg
            ds = _dot_nt(dy, wd_ref[lo:hi, :])
            da_ref[:, lo:hi] = (ds * bv * (sg * (1.0 + av * (1.0 - sg)))).astype(BF16)
            db_ref[:, lo:hi] = (ds * silu).astype(BF16)
            dwd_ref[lo:hi, :] += _dot_tn((silu * bv).astype(BF16), dy)

    return _call(
        body, name="ffn_bwd_down", nsteps=t_total // tm,
        in_specs=[_rows(tm, d), _rows(tm, d), _rows(tm, f), _rows(tm, f), _full(vec.shape), _VM],
        out_specs=[_rows(tm, f), _rows(tm, f), _VM, _VM],
        out_shape=[jax.ShapeDtypeStruct((t_total, f), BF16), jax.ShapeDtypeStruct((t_total, f), BF16),
                   jax.ShapeDtypeStruct(wd.shape, F32), jax.ShapeDtypeStruct((8, d), F32)],
        args=(dx3, fout, a, b, vec, wd), exch=exch)


def _ffn_bwd_up(da, db, x2, dx3, vec, wg, wu, tm, exch=None):
    t_total, d = x2.shape
    f = da.shape[1]

    def body(da_ref, db_ref, x_ref, dx_ref, vec_ref, wg_ref, wu_ref, dx2_ref, dwg_ref, dwu_ref, acc_ref):
        _zero_at_start(dwg_ref, dwu_ref, acc_ref)
        xv = x_ref[...]
        g, sh, sc = vec_ref[0:1, :], vec_ref[1:2, :], vec_ref[2:3, :]
        h = _norm_mod(xv, g, sh, sc).astype(BF16)
        dav = da_ref[...]
        dbv = db_ref[...]
        dwg_ref[...] += _dot_tn(dav, h)
        dwu_ref[...] += _dot_tn(dbv, h)
        dh = _dot(dav, wg_ref[...]) + _dot(dbv, wu_ref[...])
        dxn, dsh, dsc, dg = _norm_mod_bwd(dh, xv, g, sc)
        acc_ref[0:1, :] += dsh
        acc_ref[1:2, :] += dsc
        acc_ref[2:3, :] += dg
        dx2_ref[...] = dx_ref[...] + dxn

    return _call(
        body, name="ffn_bwd_up", nsteps=t_total // tm,
        in_specs=[_rows(tm, f), _rows(tm, f), _rows(tm, d), _rows(tm, d), _full(vec.shape), _VM, _VM],
        out_specs=[_rows(tm, d), _VM, _VM, _VM],
        out_shape=[jax.ShapeDtypeStruct((t_total, d), F32), jax.ShapeDtypeStruct(wg.shape, F32),
                   jax.ShapeDtypeStruct(wu.shape, F32), jax.ShapeDtypeStruct((8, d), F32)],
        args=(da, db, x2, dx3, vec, wg, wu), exch=exch)


def _ab_bwd_out(dx, y, u, vec, conv, wpool, scale, wout, tm, exch=None):
    t_total, d = dx.shape
    nu = u.shape[1]
    da = nu // 4
    gw = da // len(POOL_WINDOWS)
    nsteps = t_total // tm

    def body(dx_ref, y_ref, up_ref, uc_ref, un_ref, vec_ref, conv_ref, wpool_ref, scale_ref, wout_ref,
             dpre_ref, dwout_ref, dwpool_ref, acc_ref, q_ext, p_ext):
        _zero_at_start(dwout_ref, dwpool_ref, acc_ref)
        i = pl.program_id(0)
        dxv = dx_ref[...]
        acc_ref[0:1, :] += _colsum(dxv * y_ref[...].astype(F32))
        dy = (dxv * vec_ref[0:1, :]).astype(BF16)
        bg, cq, pooled, ybpre = _ab_core(up_ref, uc_ref, un_ref, conv_ref, wpool_ref, q_ext, p_ext,
                                         i, nsteps, tm, t_total)
        cat = jnp.concatenate([bg * cq, ybpre * scale_ref[...]], axis=1).astype(BF16)
        dwout_ref[...] += _dot_tn(cat, dy)
        dcat = _dot_nt(dy, wout_ref[...])
        dya = dcat[:, 0:da]
        dyb = dcat[:, da:2 * da]
        acc_ref[1:2, 0:da] += _colsum(dyb * ybpre)
        dybpre = (dyb * scale_ref[...]).astype(BF16)
        dpooled = []
        for g in range(len(POOL_WINDOWS)):
            dg = dybpre[:, g * gw:(g + 1) * gw]
            dwpool_ref[g] += _dot_tn(pooled[g], dg)
            dpooled.append(_dot_nt(dg, wpool_ref[g]))
        dpre_ref[...] = jnp.concatenate([dya * cq, dya * bg] + dpooled, axis=1).astype(BF16)

    return _call(
        body, name="ab_bwd_out", nsteps=nsteps,
        in_specs=[_rows(tm, d), _rows(tm, d), *_halo_specs(tm, nu, t_total), _full(vec.shape),
                  _full(conv.shape), _full(wpool.shape), _full(scale.shape), _VM],
        out_specs=[_rows(tm, 3 * da), _VM, _VM, _VM],
        out_shape=[jax.ShapeDtypeStruct((t_total, 3 * da), BF16), jax.ShapeDtypeStruct(wout.shape, F32),
                   jax.ShapeDtypeStruct(wpool.shape, F32), jax.ShapeDtypeStruct((8, d), F32)],
        scratch_shapes=[pltpu.VMEM((tm + 2 * HALO, da), F32), pltpu.VMEM((tm + 2 * HALO, da), F32)],
        args=(dx, y, u, u, u, vec, conv, wpool, scale, wout), exch=exch)


def _ab_bwd_in(dpre, u, x, dx, vec, conv, win, tm, exch=None):
    t_total, d = x.shape
    nu = u.shape[1]
    da = nu // 4
    gw = da // len(POOL_WINDOWS)
    nsteps = t_total // tm

    def body(dp_ref, dc_ref, dn_ref, up_ref, uc_ref, un_ref, x_ref, dx_ref, vec_ref, conv_ref, win_ref,
             dxin_ref, dwin_ref, dconv_ref, acc_ref, dcq_ext, q_ext, dpl_ext):
        _zero_at_start(dwin_ref, dconv_ref, acc_ref)
        i = pl.program_id(0)

        def ucols(ref, k):
            return ref[:, k * da:(k + 1) * da].astype(F32)

        def dcols(ref, k):
            return ref[:, k * da:(k + 1) * da].astype(F32)

        _fill_ext(dcq_ext, dcols(dp_ref, 1), dcols(dc_ref, 1), dcols(dn_ref, 1), i, nsteps, tm)
        _fill_ext(q_ext, ucols(up_ref, 1) * ucols(up_ref, 2), ucols(uc_ref, 1) * ucols(uc_ref, 2),
                  ucols(un_ref, 1) * ucols(un_ref, 2), i, nsteps, tm)
        _fill_ext(dpl_ext, dcols(dp_ref, 2), dcols(dc_ref, 2), dcols(dn_ref, 2), i, nsteps, tm)
        dq = (conv_ref[0:1, :] * dcq_ext[HALO + 1:HALO + 1 + tm, :] + conv_ref[1:2, :] * dcq_ext[HALO:HALO + tm, :]
              + conv_ref[2:3, :] * dcq_ext[HALO - 1:HALO - 1 + tm, :])
        dcq = dcq_ext[HALO:HALO + tm, :]
        for k in range(3):
            dconv_ref[k:k + 1, :] += _colsum(dcq * q_ext[HALO + k - 1:HALO + k - 1 + tm, :])
        dcg = dq * ucols(uc_ref, 2)
        dv = dq * ucols(uc_ref, 1)
        t_ext = i * tm - HALO + lax.broadcasted_iota(jnp.int32, (tm + 2 * HALO, 1), 0)
        dps = []
        for g, wdw in enumerate(POOL_WINDOWS):
            left = wdw // 2
            right = wdw - 1 - left
            lo, hi = g * gw, (g + 1) * gw
            dpg = dpl_ext[HALO:HALO + tm, lo:hi]
            dpl_ext[:, lo:hi] = dpl_ext[:, lo:hi] / _window_count(t_ext, wdw, t_total)
            s = dpl_ext[HALO - right:HALO - right + tm, lo:hi]
            for o in range(-right + 1, left + 1):
                s = s + dpl_ext[HALO + o:HALO + o + tm, lo:hi]
            dps.append(s - dpg)
        du = jnp.concatenate([dcols(dc_ref, 0), dcg, dv] + dps, axis=1).astype(BF16)
        xv = x_ref[...]
        g, sh, sc = vec_ref[0:1, :], vec_ref[1:2, :], vec_ref[2:3, :]
        h = _norm_mod(xv, g, sh, sc).astype(BF16)
        dwin_ref[...] += _dot_tn(h, du)
        dh = _dot_nt(du, win_ref[...])
        dxn, dsh, dsc, dg = _norm_mod_bwd(dh, xv, g, sc)
        acc_ref[0:1, :] += dsh
        acc_ref[1:2, :] += dsc
        acc_ref[2:3, :] += dg
        dxin_ref[...] = dx_ref[...] + dxn

    ext = pltpu.VMEM((tm + 2 * HALO, da), F32)
    return _call(
        body, name="ab_bwd_in", nsteps=nsteps,
        in_specs=[*_halo_specs(tm, 3 * da, t_total), *_halo_specs(tm, nu, t_total), _rows(tm, d), _rows(tm, d),
                  _full(vec.shape), _full(conv.shape), _VM],
        out_specs=[_rows(tm, d), _VM, _VM, _VM],
        out_shape=[jax.ShapeDtypeStruct((t_total, d), F32), jax.ShapeDtypeStruct(win.shape, F32),
                   jax.ShapeDtypeStruct((8, da), F32), jax.ShapeDtypeStruct((8, d), F32)],
        scratch_shapes=[ext, ext, ext],
        args=(dpre, dpre, dpre, u, u, u, x, dx, vec, conv, win), exch=exch)


def _cf_bwd_out(dx, y, zc, vec, wpw2, tm, exch=None):
    t_total, d = dx.shape

    def body(dx_ref, y_ref, zc_ref, vec_ref, w_ref, dzc_ref, dw_ref, acc_ref):
        _zero_at_start(dw_ref, acc_ref)
        dxv = dx_ref[...]
        acc_ref[0:1, :] += _colsum(dxv * y_ref[...].astype(F32))
        dyf = dxv * vec_ref[0:1, :]
        acc_ref[1:2, :] += _colsum(dyf)
        dy = dyf.astype(BF16)
        zn, rstd = _layer_norm_stats(zc_ref[...].astype(F32))
        lng = vec_ref[1:2, :]
        zl = zn * lng + vec_ref[2:3, :]
        sg = _sigmoid(zl)
        dw_ref[...] += _dot_tn((zl * sg).astype(BF16), dy)
        dzl = _dot_nt(dy, w_ref[...]) * (sg * (1.0 + zl * (1.0 - sg)))
        acc_ref[2:3, :] += _colsum(dzl * zn)
        acc_ref[3:4, :] += _colsum(dzl)
        dzn = dzl * lng
        dzc = rstd * (dzn - jnp.mean(dzn, axis=-1, keepdims=True)
                      - zn * jnp.mean(dzn * zn, axis=-1, keepdims=True))
        acc_ref[4:5, :] += _colsum(dzc)
        dzc_ref[...] = dzc.astype(BF16)

    return _call(
        body, name="cf_bwd_out", nsteps=t_total // tm,
        in_specs=[_rows(tm, d), _rows(tm, d), _rows(tm, d), _full(vec.shape), _VM],
        out_specs=[_rows(tm, d), _VM, _VM],
        out_shape=[jax.ShapeDtypeStruct((t_total, d), BF16), jax.ShapeDtypeStruct(wpw2.shape, F32),
                   jax.ShapeDtypeStruct((8, d), F32)],
        args=(dx, y, zc, vec, wpw2), exch=exch)


def _cf_bwd_in(dzc, u, x, dx, vec, wdw, wpw1, tm, exch=None):
    t_total, d = x.shape
    nu = u.shape[1]
    nsteps = t_total // tm
    left = (CONF_KERNEL - 1) // 2

    def body(dp_ref, dc_ref, dn_ref, up_ref, uc_ref, un_ref, x_ref, dx_ref, vec_ref, wdw_ref, w_ref,
             dxin_ref, dw_ref, dwdw_ref, db1_ref, acc_ref, dzc_ext, z_ext, sh_ref, dz_buf):
        _zero_at_start(dw_ref, dwdw_ref, db1_ref, acc_ref)
        i = pl.program_id(0)
        _fill_ext(dzc_ext, dp_ref[...].astype(F32), dc_ref[...].astype(F32), dn_ref[...].astype(F32),
                  i, nsteps, tm)
        _glu_ext(up_ref, uc_ref, un_ref, z_ext, i, nsteps, tm)
        for lo in range(0, d, CONV_COLS):
            hi = lo + CONV_COLS
            _fill_shifts(sh_ref, dzc_ext, lo, hi, tm)
            acc = wdw_ref[0:1, lo:hi] * _shifted(sh_ref, left, tm)
            for k in range(1, CONF_KERNEL):
                acc = acc + wdw_ref[k:k + 1, lo:hi] * _shifted(sh_ref, left - k, tm)
            dz_buf[:, lo:hi] = acc
            dzc = dzc_ext[HALO:HALO + tm, lo:hi]
            _fill_shifts(sh_ref, z_ext, lo, hi, tm)
            for k in range(CONF_KERNEL):
                dwdw_ref[k:k + 1, lo:hi] += _colsum(dzc * _shifted(sh_ref, k - left, tm))
        dz = dz_buf[...]
        av = uc_ref[:, 0:d].astype(F32)
        sg = _sigmoid(uc_ref[:, d:2 * d].astype(F32))
        duf = jnp.concatenate([dz * sg, dz * av * sg * (1.0 - sg)], axis=1)
        db1_ref[0:1, :] += _colsum(duf)
        du = duf.astype(BF16)
        xv = x_ref[...]
        g, sh, sc = vec_ref[0:1, :], vec_ref[1:2, :], vec_ref[2:3, :]
        h = _norm_mod(xv, g, sh, sc).astype(BF16)
        dw_ref[...] += _dot_tn(h, du)
        dh = _dot_nt(du, w_ref[...])
        dxn, dsh, dsc, dg = _norm_mod_bwd(dh, xv, g, sc)
        acc_ref[0:1, :] += dsh
        acc_ref[1:2, :] += dsc
        acc_ref[2:3, :] += dg
        dxin_ref[...] = dx_ref[...] + dxn

    ext = pltpu.VMEM((tm + 2 * HALO, d), F32)
    return _call(
        body, name="cf_bwd_in", nsteps=nsteps,
        in_specs=[*_halo_specs(tm, d, t_total), *_halo_specs(tm, nu, t_total), _rows(tm, d), _rows(tm, d),
                  _full(vec.shape), _full(wdw.shape), _VM],
        out_specs=[_rows(tm, d), _VM, _VM, _VM, _VM],
        out_shape=[jax.ShapeDtypeStruct((t_total, d), F32), jax.ShapeDtypeStruct(wpw1.shape, F32),
                   jax.ShapeDtypeStruct((32, d), F32), jax.ShapeDtypeStruct((8, nu), F32),
                   jax.ShapeDtypeStruct((8, d), F32)],
        scratch_shapes=[ext, ext, _shift_scratch(tm), pltpu.VMEM((tm, d), F32)],
        args=(dzc, dzc, dzc, u, u, u, x, dx, vec, wdw, wpw1), exch=exch)


def _mod_fwd(c_all, w_mod, b_cols):
    nl, d, ncol = w_mod.shape
    nb = c_all.shape[0]

    def body(c_ref, w_ref, b_ref, o_ref):
        cv = c_ref[...]
        ca = cv * _sigmoid(cv)
        o_ref[0] = jnp.dot(ca, w_ref[0], preferred_element_type=F32, precision=HIGHEST) + b_ref[0]

    return _pcall(
        body, name="mod_fwd", grid=(nl,),
        in_specs=[_full(c_all.shape), pl.BlockSpec((1, d, ncol), lambda l: (l, 0, 0)),
                  pl.BlockSpec((1, 1, ncol), lambda l: (l, 0, 0))],
        out_specs=pl.BlockSpec((1, nb, ncol), lambda l: (l, 0, 0)),
        out_shape=jax.ShapeDtypeStruct((nl, nb, ncol), F32),
        compiler_params=_seq_params(),
    )(c_all, w_mod, b_cols.reshape(nl, 1, ncol))


def _mod_bwd(c_all_t, dmod_cols):
    d, nb = c_all_t.shape
    nl, _, ncol = dmod_cols.shape

    def body(c_ref, dm_ref, o_ref):
        cv = c_ref[...]
        ca = cv * _sigmoid(cv)
        o_ref[0] = jnp.dot(ca, dm_ref[0], preferred_element_type=F32, precision=HIGHEST)

    return _pcall(
        body, name="mod_bwd", grid=(nl,),
        in_specs=[_full(c_all_t.shape), pl.BlockSpec((1, nb, ncol), lambda l: (l, 0, 0))],
        out_specs=pl.BlockSpec((1, d, ncol), lambda l: (l, 0, 0)),
        out_shape=jax.ShapeDtypeStruct((nl, d, ncol), F32),
        compiler_params=_seq_params(),
    )(c_all_t, dmod_cols)


def _row_block(r, c):
    if r * c <= EW_BLOCK_ELEMS:
        return r
    best = None
    for br in range(8, r, 8):
        if r % br == 0 and br * c <= EW_BLOCK_ELEMS:
            best = br
    assert best is not None, (r, c)
    return best


def _as2d(a):
    return a.reshape(-1, a.shape[-1])


def _adamw(w, gparts, m, v):
    shape = w.shape
    w2, m2, v2 = _as2d(w), _as2d(m), _as2d(v)
    g2 = [_as2d(g) for g in gparts]
    r, c = w2.shape
    br = _row_block(r, c)
    ng = len(g2)

    def body(*refs):
        w_ref, m_ref, v_ref = refs[0:3]
        g_refs = refs[3:3 + ng]
        g = g_refs[0][...]
        for gr in g_refs[1:]:
            g = g + gr[...]
        _adamw_update(g, w_ref[...], m_ref[...], v_ref[...], refs[3 + ng:])

    spec = pl.BlockSpec((br, c), lambda i: (i, 0))
    outs = _pcall(
        body, name="adamw", grid=(r // br,),
        in_specs=[spec] * (3 + ng), out_specs=[spec] * 4,
        out_shape=[jax.ShapeDtypeStruct((r, c), F32)] * 4,
        compiler_params=_seq_params(),
    )(w2, m2, v2, *g2)
    return tuple(o.reshape(shape) for o in outs)


def _adamw_update(g, w, m, v, out_refs):
    go_ref, d_ref, mo_ref, vo_ref = out_refs
    mn = ADAM_B1 * m + (1.0 - ADAM_B1) * g
    vn = ADAM_B2 * v + (1.0 - ADAM_B2) * (g * g)
    m_hat = mn / (1.0 - ADAM_B1 ** ADAM_STEP)
    v_hat = vn / (1.0 - ADAM_B2 ** ADAM_STEP)
    go_ref[...] = g.reshape(go_ref.shape)
    d_ref[...] = (-ADAM_LR * (m_hat / (jnp.sqrt(v_hat) + ADAM_EPS) + ADAM_WD * w)).reshape(d_ref.shape)
    mo_ref[...] = mn.reshape(mo_ref.shape)
    vo_ref[...] = vn.reshape(vo_ref.shape)


def _adamw_partials(w, partials, m, v, exch=None):
    nl, a, b = w.shape
    br = _row_block(a, b)
    nb = a // br
    n = 0 if exch is None else len(exch.arrs)
    ni = 3 + nl

    def body(*refs):
        w_ref, m_ref, v_ref = refs[0:3]
        p_refs = refs[3:ni]
        xin, out_refs, xout, sems = refs[ni:ni + n], refs[ni + n:ni + n + 4], refs[ni + n + 4:ni + 2 * n + 4], refs[ni + 2 * n + 4:]
        if exch is not None:
            @pl.when((pl.program_id(0) == 0) & (pl.program_id(1) == 0))
            def _():
                exch.start(xin, xout, sems)

        for layer in range(nl):
            @pl.when(pl.program_id(0) == layer)
            def _(layer=layer):
                halves = []
                for core in range(2):
                    acc = p_refs[layer][core, 0].astype(F32)
                    for chip in range(1, N_CHIPS):
                        acc = acc + p_refs[layer][core, chip].astype(F32)
                    halves.append(acc)
                _adamw_update(halves[0] + halves[1], w_ref[...], m_ref[...], v_ref[...], out_refs)

        if exch is not None:
            @pl.when((pl.program_id(0) == nl - 1) & (pl.program_id(1) == nb - 1))
            def _():
                exch.mid(xin, xout, sems)
                exch.wait(xin, xout, sems)

    def part_spec(layer):
        def index(l, i):
            return 0, 0, jnp.where(l == layer, i, jnp.where(l < layer, 0, nb - 1)), 0
        return pl.BlockSpec((2, N_CHIPS, br, b), index)

    spec = pl.BlockSpec((br, b), lambda l, i: (l * nb + i, 0))
    outs = _pcall(
        body, name="adamw_partials" if exch is None else "adamw_partials_" + exch.tag, grid=(nl, nb),
        in_specs=[spec] * 3 + [part_spec(layer) for layer in range(nl)] + [_ANY] * n,
        out_specs=[spec] * 4 + [_ANY] * n,
        out_shape=[jax.ShapeDtypeStruct((nl * a, b), F32)] * 4 + ([] if exch is None else exch.out_shapes()),
        scratch_shapes=[] if exch is None else exch.sems(),
        compiler_params=pltpu.CompilerParams(dimension_semantics=("arbitrary", "arbitrary"),
                                             vmem_limit_bytes=VMEM_LIMIT),
    )(_as2d(w), _as2d(m), _as2d(v), *partials, *([] if exch is None else exch.arrs))
    return tuple(o.reshape(w.shape) for o in outs[:4]), list(outs[4:])


def _allgather8(block, with_sum):
    m_per, n = block.shape

    def body(x_ref, out_ref, *rest):
        if with_sum:
            sum_ref, send_sems, recv_sems, local_sem = rest
        else:
            send_sems, recv_sems, local_sem = rest
        x, y, c = _place()
        me, sibling = (x, y, c), (x, y, 1 - c)
        chips = [(1 - x, y), (x, 1 - y), (1 - x, 1 - y)]

        def rows(px, py, pc):
            return out_ref.at[pl.ds((4 * px + 2 * py + pc) * m_per, m_per), :]

        def copy(k, blk, to, src=None):
            return pltpu.make_async_remote_copy(
                src_ref=rows(*blk) if src is None else src, dst_ref=rows(*blk),
                send_sem=send_sems.at[k], recv_sem=recv_sems.at[k], device_id=to, device_id_type=MESH)

        mine = pltpu.make_async_copy(x_ref, rows(*me), local_sem)
        mine.start()
        first = [copy(0, me, sibling, src=x_ref)]
        first += [copy(1 + j, me, (*chip, c), src=x_ref) for j, chip in enumerate(chips)]
        for cp in first:
            cp.start()
        passed = [copy(4 + j, (*chip, c), sibling) for j, chip in enumerate(chips)]
        for j, chip in enumerate(chips):
            copy(1 + j, (*chip, c), me).wait_recv()
            passed[j].start()
        copy(0, sibling, me).wait_recv()
        for j, chip in enumerate(chips):
            copy(4 + j, (*chip, 1 - c), me).wait_recv()
        for cp in first + passed:
            cp.wait_send()
        mine.wait()
        if with_sum:
            acc = out_ref[0:m_per, :]
            for k in range(1, N_DEV):
                acc = acc + out_ref[k * m_per:(k + 1) * m_per, :]
            sum_ref[...] = acc

    out_shape = [jax.ShapeDtypeStruct((N_DEV * m_per, n), F32)]
    out_specs = [_VM]
    if with_sum:
        out_shape.append(jax.ShapeDtypeStruct((m_per, n), F32))
        out_specs.append(_VM)
    res = _pcall(
        body, name="allgather8_sum" if with_sum else "allgather8",
        in_specs=[_VM], out_specs=out_specs, out_shape=out_shape,
        scratch_shapes=[pltpu.SemaphoreType.DMA((7,)), pltpu.SemaphoreType.DMA((7,)), pltpu.SemaphoreType.DMA],
        compiler_params=pltpu.CompilerParams(vmem_limit_bytes=VMEM_LIMIT),
    )(block)
    return res if with_sum else res[0]


def _cols_to_chips(w):
    *lead, a, nb = w.shape
    w = w.reshape(*lead, a, N_CHIPS, nb // N_CHIPS)
    return jnp.moveaxis(w, -2, 0)


def _chips_to_cols(g):
    g = jnp.moveaxis(g, 0, -2)
    *lead, a, k, b = g.shape
    return g.reshape(*lead, a, k * b)


def _my_cols(full, chip):
    w = full.shape[-1] // N_CHIPS
    return lax.dynamic_slice_in_dim(full, chip * w, w, axis=full.ndim - 1)


def _pad_rows(a, rows):
    return jnp.pad(a, ((0, rows - a.shape[0]), (0, 0)))


def _to_lanes(a):
    flat = a.reshape(-1)
    n = -(-flat.shape[0] // (8 * LANES)) * (8 * LANES)
    return jnp.pad(flat, (0, n - flat.shape[0])).reshape(-1, LANES)


class _Packer:
    def __init__(self):
        self.items = []
        self.rows = 0

    def add(self, name, a):
        lanes = _to_lanes(a)
        self.items.append((name, self.rows, a.shape, lanes))
        self.rows += lanes.shape[0]

    def pack(self):
        total = -(-self.rows // 8) * 8
        return _pad_rows(jnp.concatenate([it[3] for it in self.items], axis=0), total)

    def unpack(self, buf):
        out = {}
        for name, row, shape, lanes in self.items:
            size = 1
            for s in shape:
                size *= s
            out[name] = buf[row:row + lanes.shape[0]].reshape(-1)[:size].reshape(shape)
        return out


TM_SEQ = 512
TM_FFN = 256


LAYER_KEYS = ("in", "out", "gate", "up", "down")
COL_KEYS = ("in",)
TRANSPOSED = ("ffn_w_gate", "ffn_w_up")


def _layer_big_names(layer):
    i = layer // 2
    mix = (("ab_w_in", i), ("ab_w_out", i)) if layer % 2 == 0 else (("cf_w_pw1", i), ("cf_w_pw2", i))
    return dict(zip(LAYER_KEYS, mix + (("ffn_w_gate", layer), ("ffn_w_up", layer), ("ffn_w_down", layer))))


def _unpack_weight(key, g):
    g = g.reshape(N_CHIPS, -1, g.shape[-1])
    return _chips_to_cols(g) if key in COL_KEYS else g.reshape(-1, g.shape[-1])


def _chunk_grad(key, dw):
    parts = _cols_to_chips(dw) if key in COL_KEYS else dw.reshape(N_CHIPS, -1, dw.shape[-1])
    return parts.astype(BF16)


def _local_step(x, target, mods, p, shards):
    t_total, d = x.shape
    depth = mods.shape[0]
    tm = min(TM_SEQ, t_total)
    tmf = min(TM_FFN, t_total)
    saved = []
    xin = x
    weights = [{} for _ in range(depth)]

    def carried(stage, layer):
        if layer == 0:
            return {"in": (0, ("out", "gate")), "mix": (0, ("up", "down")), "ffn": (1, ("in", "out", "gate", "up"))}[stage]
        return {"in": (layer, ("down",)), "mix": (layer + 1, ("in", "out")), "ffn": (layer + 1, ("gate", "up"))}[stage]

    def gather(stage, layer):
        of, keys = carried(stage, layer)
        if of >= depth:
            return None
        return _Gather([shards[of][k].reshape(2, -1, shards[of][k].shape[-1]) for k in keys])

    def keep(stage, layer, arrs):
        of, keys = carried(stage, layer)
        for k, g in zip(keys, arrs):
            weights[of][k] = _unpack_weight(k, g)

    weights[0]["in"] = _unpack_weight("in", _standalone(_Gather([shards[0]["in"].reshape(2, -1, shards[0]["in"].shape[-1])]))[0])
    for layer in range(depth):
        i = layer // 2
        lw = weights[layer]
        sh1, sc1, g1, sh2, sc2, g2 = (mods[layer, k:k + 1] for k in range(6))
        vec_in = jnp.concatenate([p["norm_mix_g"][layer:layer + 1], sh1, sc1], axis=0)
        bias = None if layer % 2 == 0 else p["cf_b_pw1"][i:i + 1]
        (u,), arrived = _in_proj(xin, vec_in, lw["in"], bias, tm, exch=gather("in", layer))
        keep("in", layer, arrived)
        if layer % 2 == 0:
            (y, x2), arrived = _ab_fwd(u, xin, g1, p["ab_conv"][i], p["ab_w_pool"][i].astype(BF16),
                                       p["ab_pool_scale"][i:i + 1], lw["out"], tm, exch=gather("mix", layer))
            zc = None
        else:
            vec_cf = jnp.concatenate([g1, p["cf_b_dw"][i:i + 1], p["cf_ln_g"][i:i + 1], p["cf_ln_b"][i:i + 1],
                                      p["cf_b_pw2"][i:i + 1]], axis=0)
            (zc, y, x2), arrived = _cf_fwd(u, xin, vec_cf, _pad_rows(p["cf_w_dw"][i], 32), lw["out"], tm,
                                           exch=gather("mix", layer))
        keep("mix", layer, arrived)
        vec_ffn = jnp.concatenate([p["norm_ffn_g"][layer:layer + 1], sh2, sc2, g2], axis=0)
        (a, b, fout, x3), arrived = _ffn_fwd(x2, vec_ffn, lw["gate"], lw["up"], lw["down"], tmf,
                                             exch=gather("ffn", layer))
        keep("ffn", layer, arrived)
        saved.append((xin, u, y, zc, x2, a, b, fout))
        xin = x3

    (dx, fin), _ = _final_fwd_bwd(xin, target, p["final_norm_g"].reshape(1, d), tm)
    grads = {"final_norm_g": fin[0], "loss": fin[1, 0:1]}
    per_layer = {k: [None] * depth for k in ("norm_mix_g", "norm_ffn_g")}
    half = {k: [None] * (depth // 2) for k in (
        "ab_conv", "ab_w_pool", "ab_pool_scale", "cf_b_pw1", "cf_w_dw", "cf_b_dw", "cf_ln_g", "cf_ln_b", "cf_b_pw2")}
    dmods = [None] * depth
    received = {}
    pending = None
    for layer in reversed(range(depth)):
        i = layer // 2
        lw = weights[layer]
        xin, u, y, zc, x2, a, b, fout = saved[layer]
        sh1, sc1, g1, sh2, sc2, g2 = (mods[layer, k:k + 1] for k in range(6))
        above = _Scatter([pending]) if pending is not None else None
        (da, db, dwd, acc_d), arrived = _ffn_bwd_down(dx, fout, a, b, g2, lw["down"], tmf, exch=above)
        if pending is not None:
            received[(layer + 1, "in")] = arrived[0]
        vec_ffn = jnp.concatenate([p["norm_ffn_g"][layer:layer + 1], sh2, sc2], axis=0)
        (dx2, dwg, dwu, acc_u), arrived = _ffn_bwd_up(da, db, x2, dx, vec_ffn, lw["gate"], lw["up"], tmf,
                                                      exch=_Scatter([_chunk_grad("down", dwd)]))
        received[(layer, "down")] = arrived[0]
        per_layer["norm_ffn_g"][layer] = acc_u[2]
        vec_in = jnp.concatenate([p["norm_mix_g"][layer:layer + 1], sh1, sc1], axis=0)
        send_gate = _Scatter([_chunk_grad("gate", dwg)])
        if layer % 2 == 0:
            (dpre, dwout, dwpool, acc_o), arrived = _ab_bwd_out(
                dx2, y, u, g1, p["ab_conv"][i], p["ab_w_pool"][i].astype(BF16), p["ab_pool_scale"][i:i + 1],
                lw["out"], tm, exch=send_gate)
            received[(layer, "gate")] = arrived[0]
            send_up_out = _Scatter([_chunk_grad("up", dwu), _chunk_grad("out", dwout)])
            (dx, dwin, dconv, acc_i), arrived = _ab_bwd_in(dpre, u, xin, dx2, vec_in, p["ab_conv"][i], lw["in"], tm,
                                                           exch=send_up_out)
            half["ab_w_pool"][i] = dwpool
            half["ab_pool_scale"][i] = acc_o[1, 0:d // 2]
            half["ab_conv"][i] = dconv[0:3]
        else:
            vec_cf = jnp.concatenate([g1, p["cf_ln_g"][i:i + 1], p["cf_ln_b"][i:i + 1]], axis=0)
            (dzc, dwout, acc_o), arrived = _cf_bwd_out(dx2, y, zc, vec_cf, lw["out"], tm, exch=send_gate)
            received[(layer, "gate")] = arrived[0]
            send_up_out = _Scatter([_chunk_grad("up", dwu), _chunk_grad("out", dwout)])
            (dx, dwin, dwdw, db1, acc_i), arrived = _cf_bwd_in(
                dzc, u, xin, dx2, vec_in, _pad_rows(p["cf_w_dw"][i], 32), lw["in"], tm, exch=send_up_out)
            half["cf_b_pw2"][i] = acc_o[1]
            half["cf_ln_g"][i] = acc_o[2]
            half["cf_ln_b"][i] = acc_o[3]
            half["cf_b_dw"][i] = acc_o[4]
            half["cf_w_dw"][i] = dwdw[0:CONF_KERNEL]
            half["cf_b_pw1"][i] = db1[0]
        received[(layer, "up")], received[(layer, "out")] = arrived
        per_layer["norm_mix_g"][layer] = acc_i[2]
        dmods[layer] = jnp.stack([acc_i[0], acc_i[1], acc_o[0], acc_u[0], acc_u[1], acc_d[0]], axis=0)
        pending = _chunk_grad("in", dwin)
    for k, v in {**per_layer, **half}.items():
        grads[k] = jnp.stack(v, axis=0)
    return dx, grads, jnp.stack(dmods, axis=0), received, pending


SMALL_COLS = ("ab_conv", "cf_b_pw1", "cf_w_dw", "cf_b_dw", "cf_ln_g", "cf_ln_b", "cf_b_pw2")
SMALL_REPL = ("norm_mix_g", "norm_ffn_g", "ab_w_pool", "ab_pool_scale", "final_norm_g")
WEIGHTS = ("norm_mix_g", "norm_ffn_g", "w_mod", "b_mod", "ab_w_in", "ab_conv", "ab_w_pool", "ab_pool_scale",
           "ab_w_out", "cf_w_pw1", "cf_b_pw1", "cf_w_dw", "cf_b_dw", "cf_ln_g", "cf_ln_b", "cf_w_pw2",
           "cf_b_pw2", "ffn_w_gate", "ffn_w_up", "ffn_w_down", "final_norm_g")


def kernel(x, c, norm_mix_g, norm_ffn_g, w_mod, b_mod, ab_w_in, ab_conv, ab_w_pool, ab_pool_scale, ab_w_out, cf_w_pw1, cf_b_pw1, cf_w_dw, cf_b_dw, cf_ln_g, cf_ln_b, cf_w_pw2, cf_b_pw2, ffn_w_gate, ffn_w_up, ffn_w_down, final_norm_g, loss_target, m_norm_mix_g, m_norm_ffn_g, m_w_mod, m_b_mod, m_ab_w_in, m_ab_conv, m_ab_w_pool, m_ab_pool_scale, m_ab_w_out, m_cf_w_pw1, m_cf_b_pw1, m_cf_w_dw, m_cf_b_dw, m_cf_ln_g, m_cf_ln_b, m_cf_w_pw2, m_cf_b_pw2, m_ffn_w_gate, m_ffn_w_up, m_ffn_w_down, m_final_norm_g, v_norm_mix_g, v_norm_ffn_g, v_w_mod, v_b_mod, v_ab_w_in, v_ab_conv, v_ab_w_pool, v_ab_pool_scale, v_ab_w_out, v_cf_w_pw1, v_cf_b_pw1, v_cf_w_dw, v_cf_b_dw, v_cf_ln_g, v_cf_ln_b, v_cf_w_pw2, v_cf_b_pw2, v_ffn_w_gate, v_ffn_w_up, v_ffn_w_down, v_final_norm_g):
    w = dict(norm_mix_g=norm_mix_g, norm_ffn_g=norm_ffn_g, w_mod=w_mod, b_mod=b_mod, ab_w_in=ab_w_in,
             ab_conv=ab_conv, ab_w_pool=ab_w_pool, ab_pool_scale=ab_pool_scale, ab_w_out=ab_w_out,
             cf_w_pw1=cf_w_pw1, cf_b_pw1=cf_b_pw1, cf_w_dw=cf_w_dw, cf_b_dw=cf_b_dw, cf_ln_g=cf_ln_g,
             cf_ln_b=cf_ln_b, cf_w_pw2=cf_w_pw2, cf_b_pw2=cf_b_pw2, ffn_w_gate=ffn_w_gate, ffn_w_up=ffn_w_up,
             ffn_w_down=ffn_w_down, final_norm_g=final_norm_g)
    mom = dict(norm_mix_g=m_norm_mix_g, norm_ffn_g=m_norm_ffn_g, w_mod=m_w_mod, b_mod=m_b_mod, ab_w_in=m_ab_w_in,
               ab_conv=m_ab_conv, ab_w_pool=m_ab_w_pool, ab_pool_scale=m_ab_pool_scale, ab_w_out=m_ab_w_out,
               cf_w_pw1=m_cf_w_pw1, cf_b_pw1=m_cf_b_pw1, cf_w_dw=m_cf_w_dw, cf_b_dw=m_cf_b_dw, cf_ln_g=m_cf_ln_g,
               cf_ln_b=m_cf_ln_b, cf_w_pw2=m_cf_w_pw2, cf_b_pw2=m_cf_b_pw2, ffn_w_gate=m_ffn_w_gate,
               ffn_w_up=m_ffn_w_up, ffn_w_down=m_ffn_w_down, final_norm_g=m_final_norm_g)
    var = dict(norm_mix_g=v_norm_mix_g, norm_ffn_g=v_norm_ffn_g, w_mod=v_w_mod, b_mod=v_b_mod, ab_w_in=v_ab_w_in,
               ab_conv=v_ab_conv, ab_w_pool=v_ab_w_pool, ab_pool_scale=v_ab_pool_scale, ab_w_out=v_ab_w_out,
               cf_w_pw1=v_cf_w_pw1, cf_b_pw1=v_cf_b_pw1, cf_w_dw=v_cf_w_dw, cf_b_dw=v_cf_b_dw, cf_ln_g=v_cf_ln_g,
               cf_ln_b=v_cf_ln_b, cf_w_pw2=v_cf_w_pw2, cf_b_pw2=v_cf_b_pw2, ffn_w_gate=v_ffn_w_gate,
               ffn_w_up=v_ffn_w_up, ffn_w_down=v_ffn_w_down, final_norm_g=v_final_norm_g)
    px, py, pc = _place()
    chip = 2 * px + py
    dev = 2 * chip + pc
    depth, d, mod_cols = w_mod.shape
    x = x[0]
    target = loss_target[0]

    small_in = _Packer()
    small_in.add("c", c)
    for name in SMALL_COLS:
        small_in.add(name, w[name])
    gathered = _allgather8(small_in.pack(), with_sum=False).reshape(N_DEV, -1, LANES)
    per_dev = [small_in.unpack(gathered[k]) for k in range(N_DEV)]
    c_all = jnp.concatenate([pd["c"] for pd in per_dev], axis=0)
    params = {name: jnp.concatenate([per_dev[2 * k][name] for k in range(N_CHIPS)], axis=-1)
              for name in SMALL_COLS}
    for name in SMALL_REPL:
        params[name] = w[name]

    mod_part = _mod_fwd(c_all, w_mod, _my_cols(b_mod, chip))
    mod_all = _allgather8(mod_part.reshape(-1, LANES), with_sum=False)
    mod_all = mod_all.reshape(N_CHIPS, 2, depth, N_DEV, mod_cols)[:, 0]
    mod_all = jnp.moveaxis(mod_all, 0, 2).reshape(depth, N_DEV, N_CHIPS * mod_cols)
    mods = lax.dynamic_index_in_dim(mod_all, dev, axis=1, keepdims=False).reshape(depth, 6, d)

    def rows_major(name, t):
        return jnp.swapaxes(t, 1, 2) if name in TRANSPOSED else t

    shards = [{k: rows_major(name, w[name])[idx].astype(BF16) for k, (name, idx) in _layer_big_names(layer).items()}
              for layer in range(depth)]
    grad_x, grads, dmods, received, last_chunk = _local_step(x, target, mods, params, shards)

    small_out = _Packer()
    small_out.add("dmods", dmods)
    for name in ("loss",) + SMALL_REPL + SMALL_COLS:
        small_out.add(name, grads[name])
    parts_all, parts_sum = _allgather8(small_out.pack(), with_sum=True)
    small_sum = small_out.unpack(parts_sum)
    loss = small_sum["loss"][0]
    dmods_all = jnp.stack([small_out.unpack(pa)["dmods"] for pa in parts_all.reshape(N_DEV, -1, LANES)], axis=1)
    dmods_all = dmods_all.reshape(depth, N_DEV, 6 * d)

    g_final = {}
    g_final["w_mod"] = [_mod_bwd(c_all.T, _my_cols(dmods_all, chip))]
    g_final["b_mod"] = [small_sum["dmods"].reshape(depth, 6 * d)]
    for name in SMALL_REPL:
        g_final[name] = [small_sum[name]]
    for name in SMALL_COLS:
        g_final[name] = [_my_cols(small_sum[name], chip)]

    def update_large(name, exch=None):
        parts = [received[(layer, k)] for layer in range(depth)
                 for k, (other, _) in _layer_big_names(layer).items() if other == name]
        outs, arrived = _adamw_partials(rows_major(name, w[name]), parts, rows_major(name, mom[name]),
                                        rows_major(name, var[name]), exch=exch)
        return [rows_major(name, o) for o in outs], arrived

    updates = {}
    updates["ffn_w_down"], arrived = update_large("ffn_w_down", exch=_Scatter([last_chunk]))
    received[(0, "in")] = arrived[0]
    for name in WEIGHTS:
        if name in updates:
            continue
        if any(name == other for layer in range(depth) for other, _ in _layer_big_names(layer).values()):
            updates[name], _ = update_large(name)
        else:
            updates[name] = _adamw(w[name], g_final[name], mom[name], var[name])
    return (loss, grad_x[None], *[updates[name][0] for name in WEIGHTS], *[updates[name][1] for name in WEIGHTS],
            *[updates[name][2] for name in WEIGHTS], *[updates[name][3] for name in WEIGHTS])
```

```python
import functools

import jax
import jax.numpy as jnp
from jax import lax
from jax.experimental import pallas as pl
from jax.experimental.pallas import tpu as pltpu

F32 = jnp.float32
BF16 = jnp.bfloat16
RMS_EPS = 1e-6
LN_EPS = 1e-5
ADAM_LR = 0.001
ADAM_B1 = 0.9
ADAM_B2 = 0.999
ADAM_EPS = 1e-08
ADAM_WD = 0.01
ADAM_STEP = 10
POOL_WINDOWS = (2, 4, 8, 16)
CONF_KERNEL = 31
N_CHIPS = 4
N_DEV = 8
HALO = 16
CONV_COLS = 256
FFN_CHUNK = 1536
LANES = 1024
VMEM_LIMIT = 56 * 1024 * 1024
EW_BLOCK_ELEMS = 256 * 1024
MESH = pl.DeviceIdType.MESH
HIGHEST = lax.Precision.HIGHEST

_pcall = pl.pallas_call


def _dot(a, b):
    return jnp.dot(a, b, preferred_element_type=F32)


def _dot_tn(a, b):
    return lax.dot_general(a, b, (((0,), (0,)), ((), ())), preferred_element_type=F32)


def _dot_nt(a, b):
    return lax.dot_general(a, b, (((1,), (1,)), ((), ())), preferred_element_type=F32)


def _colsum(v):
    return jnp.sum(v, axis=0, keepdims=True)


def _sigmoid(v):
    return 1.0 / (1.0 + jnp.exp(-v))


def _rows(tm, c):
    return pl.BlockSpec((tm, c), lambda i: (i, 0))


def _full(shape):
    nd = len(shape)
    return pl.BlockSpec(shape, lambda i: (0,) * nd)


_VM = pl.BlockSpec(memory_space=pltpu.VMEM)
_ANY = pl.BlockSpec(memory_space=pl.ANY)


def _halo_specs(tm, c, t_total):
    r = tm // HALO
    last = t_total // HALO - 1
    prev = pl.BlockSpec((HALO, c), lambda i: (jnp.maximum(i * r - 1, 0), 0))
    nxt = pl.BlockSpec((HALO, c), lambda i: (jnp.minimum((i + 1) * r, last), 0))
    return prev, _rows(tm, c), nxt


def _seq_params():
    return pltpu.CompilerParams(dimension_semantics=("arbitrary",), vmem_limit_bytes=VMEM_LIMIT)


def _place():
    return lax.axis_index("x"), lax.axis_index("y"), lax.axis_index("c")


def _peer_chips(x, y):
    return [(1 - x, y), (x, 1 - y), (1 - x, 1 - y)]


class _Gather:
    tag = "gather"

    def __init__(self, arrs):
        self.arrs = list(arrs)

    def out_shapes(self):
        return [jax.ShapeDtypeStruct((N_CHIPS,) + a.shape, a.dtype) for a in self.arrs]

    def sems(self):
        n = len(self.arrs)
        return [pltpu.SemaphoreType.DMA((3 * n,)) for _ in range(4)] + [pltpu.SemaphoreType.DMA((n,))]

    def _copies(self, ins, outs, sems, kinds):
        ici_send, ici_recv, d2d_send, d2d_recv, local_sems = sems
        x, y, c = _place()
        me = 2 * x + y
        found = {kind: [] for kind in kinds}
        for j in range(len(ins)):
            if "local" in kinds:
                found["local"].append(pltpu.make_async_copy(ins[j], outs[j].at[me], local_sems.at[j]))
            for k, (px, py) in enumerate(_peer_chips(x, y)):
                ici = dict(send_sem=ici_send.at[3 * j + k], recv_sem=ici_recv.at[3 * j + k],
                           device_id=(px, py, c), device_id_type=MESH)
                d2d = dict(send_sem=d2d_send.at[3 * j + k], recv_sem=d2d_recv.at[3 * j + k],
                           device_id=(x, y, 1 - c), device_id_type=MESH)
                theirs = outs[j].at[2 * px + py]
                if "send" in kinds:
                    found["send"].append(pltpu.make_async_remote_copy(
                        src_ref=ins[j].at[c], dst_ref=outs[j].at[me, c], **ici))
                if "arrival" in kinds:
                    found["arrival"].append(pltpu.make_async_remote_copy(
                        src_ref=ins[j].at[c], dst_ref=theirs.at[c], **ici))
                if "pass" in kinds:
                    found["pass"].append(pltpu.make_async_remote_copy(
                        src_ref=theirs.at[c], dst_ref=theirs.at[c], **d2d))
                if "passed" in kinds:
                    found["passed"].append(pltpu.make_async_remote_copy(
                        src_ref=theirs.at[c], dst_ref=theirs.at[1 - c], **d2d))
        return found

    def start(self, ins, outs, sems):
        found = self._copies(ins, outs, sems, ("local", "send"))
        for cp in found["local"] + found["send"]:
            cp.start()

    def mid(self, ins, outs, sems):
        found = self._copies(ins, outs, sems, ("arrival", "pass"))
        for arrived, onward in zip(found["arrival"], found["pass"]):
            arrived.wait_recv()
            onward.start()

    def wait(self, ins, outs, sems):
        found = self._copies(ins, outs, sems, ("local", "send", "pass", "passed"))
        for cp in found["passed"]:
            cp.wait_recv()
        for cp in found["send"] + found["pass"]:
            cp.wait_send()
        for cp in found["local"]:
            cp.wait()


class _Scatter:
    tag = "scatter"

    def __init__(self, arrs):
        self.arrs = list(arrs)

    def out_shapes(self):
        return [jax.ShapeDtypeStruct((2,) + a.shape, a.dtype) for a in self.arrs]

    def sems(self):
        n = len(self.arrs)
        dma = pltpu.SemaphoreType.DMA
        return [dma((3 * n,)), dma((3 * n,)), dma((4 * n,)), dma((4 * n,)), dma((n,))]

    def _copies(self, ins, outs, sems, kinds):
        ici_send, ici_recv, d2d_send, d2d_recv, local_sems = sems
        x, y, c = _place()
        me = 2 * x + y
        found = {kind: [] for kind in kinds}
        for j in range(len(ins)):
            def d2d(k):
                return dict(send_sem=d2d_send.at[4 * j + k], recv_sem=d2d_recv.at[4 * j + k],
                            device_id=(x, y, 1 - c), device_id_type=MESH)

            if "local" in kinds:
                found["local"].append(pltpu.make_async_copy(ins[j].at[me], outs[j].at[0, me], local_sems.at[j]))
            if "own" in kinds:
                found["own"].append(pltpu.make_async_remote_copy(
                    src_ref=ins[j].at[me], dst_ref=outs[j].at[1, me], **d2d(3)))
            if "passed" in kinds:
                found["passed"].append(pltpu.make_async_remote_copy(
                    src_ref=ins[j].at[me], dst_ref=outs[j].at[1, me], **d2d(3)))
            for k, (px, py) in enumerate(_peer_chips(x, y)):
                ici = dict(send_sem=ici_send.at[3 * j + k], recv_sem=ici_recv.at[3 * j + k],
                           device_id=(px, py, c), device_id_type=MESH)
                peer = 2 * px + py
                if "send" in kinds:
                    found["send"].append(pltpu.make_async_remote_copy(
                        src_ref=ins[j].at[peer], dst_ref=outs[j].at[0, me], **ici))
                if "arrival" in kinds:
                    found["arrival"].append(pltpu.make_async_remote_copy(
                        src_ref=ins[j].at[me], dst_ref=outs[j].at[0, peer], **ici))
                if "pass" in kinds:
                    found["pass"].append(pltpu.make_async_remote_copy(
                        src_ref=outs[j].at[0, peer], dst_ref=outs[j].at[1, peer], **d2d(k)))
                if "passed" in kinds:
                    found["passed"].append(pltpu.make_async_remote_copy(
                        src_ref=outs[j].at[0, peer], dst_ref=outs[j].at[1, peer], **d2d(k)))
        return found

    def start(self, ins, outs, sems):
        found = self._copies(ins, outs, sems, ("local", "own", "send"))
        for cp in found["local"] + found["own"] + found["send"]:
            cp.start()

    def mid(self, ins, outs, sems):
        found = self._copies(ins, outs, sems, ("arrival", "pass"))
        for arrived, onward in zip(found["arrival"], found["pass"]):
            arrived.wait_recv()
            onward.start()

    def wait(self, ins, outs, sems):
        found = self._copies(ins, outs, sems, ("local", "own", "send", "pass", "passed"))
        for cp in found["passed"]:
            cp.wait_recv()
        for cp in found["own"] + found["send"] + found["pass"]:
            cp.wait_send()
        for cp in found["local"]:
            cp.wait()


def _call(body, *, name, nsteps, in_specs, out_specs, out_shape, args, scratch_shapes=(), exch=None):
    if exch is None:
        outs = _pcall(body, name=name, grid=(nsteps,), in_specs=list(in_specs), out_specs=list(out_specs),
                      out_shape=list(out_shape), scratch_shapes=list(scratch_shapes),
                      compiler_params=_seq_params())(*args)
        return list(outs), []
    n, ni, no, ns = len(exch.arrs), len(in_specs), len(out_specs), len(scratch_shapes)

    def hosted(*refs):
        xin = refs[ni:ni + n]
        xout = refs[ni + n + no:ni + 2 * n + no]
        scr = refs[ni + 2 * n + no:]

        @pl.when(pl.program_id(0) == 0)
        def _():
            exch.start(xin, xout, scr[ns:])

        body(*refs[:ni], *refs[ni + n:ni + n + no], *scr[:ns])

        @pl.when(pl.program_id(0) == max(nsteps - 3, 0))
        def _():
            exch.mid(xin, xout, scr[ns:])

        @pl.when(pl.program_id(0) == nsteps - 1)
        def _():
            exch.wait(xin, xout, scr[ns:])

    outs = _pcall(hosted, name=name + "_" + exch.tag, grid=(nsteps,),
                  in_specs=[*in_specs, *[_ANY] * n], out_specs=[*out_specs, *[_ANY] * n],
                  out_shape=[*out_shape, *exch.out_shapes()],
                  scratch_shapes=[*scratch_shapes, *exch.sems()],
                  compiler_params=_seq_params())(*args, *exch.arrs)
    return list(outs[:no]), list(outs[no:])


def _rms(x):
    r = lax.rsqrt(jnp.mean(x * x, axis=-1, keepdims=True) + RMS_EPS)
    return x * r, r


def _norm_mod(x, g, sh, sc):
    xhat, _ = _rms(x)
    return xhat * g * (1.0 + sc) + sh


def _norm_mod_bwd(dh, x, g, sc):
    xhat, r = _rms(x)
    n = xhat * g
    dsh = _colsum(dh)
    dsc = _colsum(dh * n)
    dn = dh * (1.0 + sc)
    dg = _colsum(dn * xhat)
    dxn = dn * g
    dx = r * (dxn - xhat * jnp.mean(dxn * xhat, axis=-1, keepdims=True))
    return dx, dsh, dsc, dg


def _fill_ext(ext_ref, prev, cur, nxt, i, nsteps, tm):
    ext_ref[0:HALO, :] = jnp.where(i > 0, prev, 0.0)
    ext_ref[HALO:HALO + tm, :] = cur
    ext_ref[HALO + tm:HALO + tm + HALO, :] = jnp.where(i < nsteps - 1, nxt, 0.0)


def _shift_scratch(tm):
    return pltpu.VMEM((8, tm + 2 * HALO - 8, CONV_COLS), F32)


def _fill_shifts(sh_ref, ext_ref, lo, hi, tm):
    for b in range(8):
        sh_ref[b] = ext_ref[b:b + tm + 2 * HALO - 8, lo:hi]


def _shifted(sh_ref, offset, tm):
    b = offset % 8
    start = HALO + offset - b
    return sh_ref[b, start:start + tm, :]


def _window_count(t, wdw, t_total):
    left = wdw // 2
    right = wdw - 1 - left
    cnt = jnp.minimum(t + right, t_total - 1) - jnp.maximum(t - left, 0) + 1
    return jnp.maximum(cnt, 1).astype(F32)


def _in_proj(x, vec, w, bias, tm, exch=None):
    t_total, d = x.shape
    nk = w.shape[2]
    n = N_CHIPS * nk
    has_bias = bias is not None

    def body(*refs):
        if has_bias:
            x_ref, vec_ref, w_ref, b_ref, u_ref = refs
        else:
            x_ref, vec_ref, w_ref, u_ref = refs
        h = _norm_mod(x_ref[...], vec_ref[0:1, :], vec_ref[1:2, :], vec_ref[2:3, :])
        h = h.astype(BF16)
        for k in range(N_CHIPS):
            u = _dot(h, w_ref[k])
            if has_bias:
                u = u + b_ref[:, k * nk:(k + 1) * nk]
            u_ref[:, k * nk:(k + 1) * nk] = u.astype(BF16)

    in_specs = [_rows(tm, d), _full(vec.shape), _VM]
    args = [x, vec, w]
    if has_bias:
        in_specs.append(_full(bias.shape))
        args.append(bias)
    return _call(
        body, name="in_proj_bias" if has_bias else "in_proj", nsteps=t_total // tm,
        in_specs=in_specs, out_specs=[_rows(tm, n)], out_shape=[jax.ShapeDtypeStruct((t_total, n), BF16)],
        args=args, exch=exch)


def _in_proj_bwd(h, du, w_ref, dw_ref):
    nk = w_ref.shape[2]
    dh = None
    for k in range(N_CHIPS):
        duk = du[:, k * nk:(k + 1) * nk]
        dw_ref[k] += _dot_tn(h, duk)
        part = _dot_nt(duk, w_ref[k])
        dh = part if dh is None else dh + part
    return dh


def _ab_core(up_ref, uc_ref, un_ref, conv_ref, wpool_ref, q_ext, p_ext, i, nsteps, tm, t_total):
    da = uc_ref.shape[1] // 4

    def cols(ref, k):
        return ref[:, k * da:(k + 1) * da].astype(F32)

    _fill_ext(q_ext, cols(up_ref, 1) * cols(up_ref, 2), cols(uc_ref, 1) * cols(uc_ref, 2),
              cols(un_ref, 1) * cols(un_ref, 2), i, nsteps, tm)
    _fill_ext(p_ext, cols(up_ref, 3), cols(uc_ref, 3), cols(un_ref, 3), i, nsteps, tm)
    bg = cols(uc_ref, 0)
    cq = (conv_ref[0:1, :] * q_ext[HALO - 1:HALO - 1 + tm, :] + conv_ref[1:2, :] * q_ext[HALO:HALO + tm, :]
          + conv_ref[2:3, :] * q_ext[HALO + 1:HALO + 1 + tm, :])
    t = i * tm + lax.broadcasted_iota(jnp.int32, (tm, 1), 0)
    gw = da // len(POOL_WINDOWS)
    pooled, ybpre = [], []
    for g, wdw in enumerate(POOL_WINDOWS):
        left = wdw // 2
        right = wdw - 1 - left
        lo, hi = g * gw, (g + 1) * gw
        s = p_ext[HALO - left:HALO - left + tm, lo:hi]
        for o in range(-left + 1, right + 1):
            s = s + p_ext[HALO + o:HALO + o + tm, lo:hi]
        pg = s / _window_count(t, wdw, t_total) - p_ext[HALO:HALO + tm, lo:hi]
        pooled.append(pg.astype(BF16))
        ybpre.append(_dot(pooled[-1], wpool_ref[g]))
    return bg, cq, pooled, jnp.concatenate(ybpre, axis=1)


def _ab_fwd(u, x, vec, conv, wpool, scale, wout, tm, exch=None):
    t_total, d = x.shape
    nu = u.shape[1]
    da = nu // 4
    nsteps = t_total // tm

    def body(up_ref, uc_ref, un_ref, x_ref, vec_ref, conv_ref, wpool_ref, scale_ref, wout_ref,
             y_ref, x2_ref, q_ext, p_ext):
        i = pl.program_id(0)
        bg, cq, _, ybpre = _ab_core(up_ref, uc_ref, un_ref, conv_ref, wpool_ref, q_ext, p_ext,
                                    i, nsteps, tm, t_total)
        cat = jnp.concatenate([bg * cq, ybpre * scale_ref[...]], axis=1).astype(BF16)
        y = _dot(cat, wout_ref[...])
        y_ref[...] = y.astype(BF16)
        x2_ref[...] = x_ref[...] + vec_ref[0:1, :] * y

    return _call(
        body, name="ab_fwd", nsteps=nsteps,
        in_specs=[*_halo_specs(tm, nu, t_total), _rows(tm, d), _full(vec.shape), _full(conv.shape),
                  _full(wpool.shape), _full(scale.shape), _VM],
        out_specs=[_rows(tm, d), _rows(tm, d)],
        out_shape=[jax.ShapeDtypeStruct((t_total, d), BF16), jax.ShapeDtypeStruct((t_total, d), F32)],
        scratch_shapes=[pltpu.VMEM((tm + 2 * HALO, da), F32), pltpu.VMEM((tm + 2 * HALO, da), F32)],
        args=(u, u, u, x, vec, conv, wpool, scale, wout), exch=exch)


def _glu_ext(up_ref, uc_ref, un_ref, z_ext, i, nsteps, tm):
    dz = uc_ref.shape[1] // 2

    def glu(ref):
        return ref[:, 0:dz].astype(F32) * _sigmoid(ref[:, dz:2 * dz].astype(F32))

    _fill_ext(z_ext, glu(up_ref), glu(uc_ref), glu(un_ref), i, nsteps, tm)


def _layer_norm_stats(zc):
    mu = jnp.mean(zc, axis=-1, keepdims=True)
    dlt = zc - mu
    rstd = lax.rsqrt(jnp.mean(dlt * dlt, axis=-1, keepdims=True) + LN_EPS)
    return dlt * rstd, rstd


def _cf_fwd(u, x, vec, wdw, wpw2, tm, exch=None):
    t_total, d = x.shape
    nu = u.shape[1]
    nsteps = t_total // tm
    left = (CONF_KERNEL - 1) // 2

    def body(up_ref, uc_ref, un_ref, x_ref, vec_ref, wdw_ref, wpw2_ref, zc_ref, y_ref, x2_ref, z_ext, sh_ref,
             zc_buf):
        i = pl.program_id(0)
        _glu_ext(up_ref, uc_ref, un_ref, z_ext, i, nsteps, tm)
        for lo in range(0, d, CONV_COLS):
            hi = lo + CONV_COLS
            _fill_shifts(sh_ref, z_ext, lo, hi, tm)
            acc = wdw_ref[0:1, lo:hi] * _shifted(sh_ref, -left, tm)
            for k in range(1, CONF_KERNEL):
                acc = acc + wdw_ref[k:k + 1, lo:hi] * _shifted(sh_ref, k - left, tm)
            zc_buf[:, lo:hi] = acc
        zc = zc_buf[...] + vec_ref[1:2, :]
        zc_ref[...] = zc.astype(BF16)
        zn, _ = _layer_norm_stats(zc)
        zl = zn * vec_ref[2:3, :] + vec_ref[3:4, :]
        zs = zl * _sigmoid(zl)
        y = _dot(zs.astype(BF16), wpw2_ref[...]) + vec_ref[4:5, :]
        y_ref[...] = y.astype(BF16)
        x2_ref[...] = x_ref[...] + vec_ref[0:1, :] * y

    return _call(
        body, name="cf_fwd", nsteps=nsteps,
        in_specs=[*_halo_specs(tm, nu, t_total), _rows(tm, d), _full(vec.shape), _full(wdw.shape), _VM],
        out_specs=[_rows(tm, d), _rows(tm, d), _rows(tm, d)],
        out_shape=[jax.ShapeDtypeStruct((t_total, d), BF16), jax.ShapeDtypeStruct((t_total, d), BF16),
                   jax.ShapeDtypeStruct((t_total, d), F32)],
        scratch_shapes=[pltpu.VMEM((tm + 2 * HALO, d), F32), _shift_scratch(tm), pltpu.VMEM((tm, d), F32)],
        args=(u, u, u, x, vec, wdw, wpw2), exch=exch)


def _ffn_chunks(f, width=FFN_CHUNK):
    return [(lo, min(lo + width, f)) for lo in range(0, f, width)]


def _ffn_fwd(x2, vec, wg, wu, wd, tm, exch=None):
    t_total, d = x2.shape
    f = wg.shape[0]

    def body(x_ref, vec_ref, wg_ref, wu_ref, wd_ref, a_ref, b_ref, f_ref, x3_ref):
        xv = x_ref[...]
        h = _norm_mod(xv, vec_ref[0:1, :], vec_ref[1:2, :], vec_ref[2:3, :]).astype(BF16)
        y = None
        for lo, hi in _ffn_chunks(f):
            a = _dot_nt(h, wg_ref[lo:hi, :])
            b = _dot_nt(h, wu_ref[lo:hi, :])
            a_ref[:, lo:hi] = a.astype(BF16)
            b_ref[:, lo:hi] = b.astype(BF16)
            s = (a * _sigmoid(a) * b).astype(BF16)
            part = _dot(s, wd_ref[lo:hi, :])
            y = part if y is None else y + part
        f_ref[...] = y.astype(BF16)
        x3_ref[...] = xv + vec_ref[3:4, :] * y

    return _call(
        body, name="ffn_fwd", nsteps=t_total // tm,
        in_specs=[_rows(tm, d), _full(vec.shape), _VM, _VM, _VM],
        out_specs=[_rows(tm, f), _rows(tm, f), _rows(tm, d), _rows(tm, d)],
        out_shape=[jax.ShapeDtypeStruct((t_total, f), BF16), jax.ShapeDtypeStruct((t_total, f), BF16),
                   jax.ShapeDtypeStruct((t_total, d), BF16), jax.ShapeDtypeStruct((t_total, d), F32)],
        args=(x2, vec, wg, wu, wd), exch=exch)


def _final_fwd_bwd(x, target, vec, tm):
    t_total, d = x.shape

    def body(x_ref, t_ref, vec_ref, dx_ref, acc_ref):
        @pl.when(pl.program_id(0) == 0)
        def _():
            acc_ref[...] = jnp.zeros_like(acc_ref)

        g = vec_ref[0:1, :]
        xhat, r = _rms(x_ref[...])
        e = xhat * g - t_ref[...]
        acc_ref[1:2, :] += jnp.zeros((1, d), F32) + 0.5 * jnp.sum(jnp.mean(e * e, axis=-1, keepdims=True))
        dout = e * (1.0 / d)
        acc_ref[0:1, :] += _colsum(dout * xhat)
        dxn = dout * g
        dx_ref[...] = r * (dxn - xhat * jnp.mean(dxn * xhat, axis=-1, keepdims=True))

    return _call(
        body, name="final_fwd_bwd", nsteps=t_total // tm,
        in_specs=[_rows(tm, d), _rows(tm, d), _full(vec.shape)],
        out_specs=[_rows(tm, d), _VM],
        out_shape=[jax.ShapeDtypeStruct((t_total, d), F32), jax.ShapeDtypeStruct((8, d), F32)],
        args=(x, target, vec))


def _zero_at_start(*refs):
    @pl.when(pl.program_id(0) == 0)
    def _():
        for ref in refs:
            ref[...] = jnp.zeros_like(ref)


def _ffn_bwd_down(dx3, fout, a, b, vec, wd, tm, exch=None):
    t_total, d = dx3.shape
    f = a.shape[1]

    def body(dx_ref, f_ref, a_ref, b_ref, vec_ref, wd_ref, da_ref, db_ref, dwd_ref, acc_ref):
        _zero_at_start(dwd_ref, acc_ref)
        dx = dx_ref[...]
        acc_ref[0:1, :] += _colsum(dx * f_ref[...].astype(F32))
        dy = (dx * vec_ref[0:1, :]).astype(BF16)
        for lo, hi in _ffn_chunks(f, FFN_CHUNK // 2):
            av = a_ref[:, lo:hi].astype(F32)
            bv = b_ref[:, lo:hi].astype(F32)
            sg = _sigmoid(av)
            silu = av * sg
            ds = _dot_nt(dy, wd_ref[lo:hi, :])
            da_ref[:, lo:hi] = (ds * bv * (sg * (1.0 + av * (1.0 - sg)))).astype(BF16)
            db_ref[:, lo:hi] = (ds * silu).astype(BF16)
            dwd_ref[lo:hi, :] += _dot_tn((silu * bv).astype(BF16), dy)

    return _call(
        body, name="ffn_bwd_down", nsteps=t_total // tm,
        in_specs=[_rows(tm, d), _rows(tm, d), _rows(tm, f), _rows(tm, f), _full(vec.shape), _VM],
        out_specs=[_rows(tm, f), _rows(tm, f), _VM, _VM],
        out_shape=[jax.ShapeDtypeStruct((t_total, f), BF16), jax.ShapeDtypeStruct((t_total, f), BF16),
                   jax.ShapeDtypeStruct(wd.shape, F32), jax.ShapeDtypeStruct((8, d), F32)],
        args=(dx3, fout, a, b, vec, wd), exch=exch)


def _ffn_bwd_up(da, db, x2, dx3, vec, wg, wu, tm, exch=None):
    t_total, d = x2.shape
    f = da.shape[1]

    def body(da_ref, db_ref, x_ref, dx_ref, vec_ref, wg_ref, wu_ref, dx2_ref, dwg_ref, dwu_ref, acc_ref):
        _zero_at_start(dwg_ref, dwu_ref, acc_ref)
        xv = x_ref[...]
        g, sh, sc = vec_ref[0:1, :], vec_ref[1:2, :], vec_ref[2:3, :]
        h = _norm_mod(xv, g, sh, sc).astype(BF16)
        dav = da_ref[...]
        dbv = db_ref[...]
        dwg_ref[...] += _dot_tn(dav, h)
        dwu_ref[...] += _dot_tn(dbv, h)
        dh = _dot(dav, wg_ref[...]) + _dot(dbv, wu_ref[...])
        dxn, dsh, dsc, dg = _norm_mod_bwd(dh, xv, g, sc)
        acc_ref[0:1, :] += dsh
        acc_ref[1:2, :] += dsc
        acc_ref[2:3, :] += dg
        dx2_ref[...] = dx_ref[...] + dxn

    return _call(
        body, name="ffn_bwd_up", nsteps=t_total // tm,
        in_specs=[_rows(tm, f), _rows(tm, f), _rows(tm, d), _rows(tm, d), _full(vec.shape), _VM, _VM],
        out_specs=[_rows(tm, d), _VM, _VM, _VM],
        out_shape=[jax.ShapeDtypeStruct((t_total, d), F32), jax.ShapeDtypeStruct(wg.shape, F32),
                   jax.ShapeDtypeStruct(wu.shape, F32), jax.ShapeDtypeStruct((8, d), F32)],
        args=(da, db, x2, dx3, vec, wg, wu), exch=exch)


def _ab_bwd_out(dx, y, u, vec, conv, wpool, scale, wout, tm, exch=None):
    t_total, d = dx.shape
    nu = u.shape[1]
    da = nu // 4
    gw = da // len(POOL_WINDOWS)
    nsteps = t_total // tm

    def body(dx_ref, y_ref, up_ref, uc_ref, un_ref, vec_ref, conv_ref, wpool_ref, scale_ref, wout_ref,
             dpre_ref, dwout_ref, dwpool_ref, acc_ref, q_ext, p_ext):
        _zero_at_start(dwout_ref, dwpool_ref, acc_ref)
        i = pl.program_id(0)
        dxv = dx_ref[...]
        acc_ref[0:1, :] += _colsum(dxv * y_ref[...].astype(F32))
        dy = (dxv * vec_ref[0:1, :]).astype(BF16)
        bg, cq, pooled, ybpre = _ab_core(up_ref, uc_ref, un_ref, conv_ref, wpool_ref, q_ext, p_ext,
                                         i, nsteps, tm, t_total)
        cat = jnp.concatenate([bg * cq, ybpre * scale_ref[...]], axis=1).astype(BF16)
        dwout_ref[...] += _dot_tn(cat, dy)
        dcat = _dot_nt(dy, wout_ref[...])
        dya = dcat[:, 0:da]
        dyb = dcat[:, da:2 * da]
        acc_ref[1:2, 0:da] += _colsum(dyb * ybpre)
        dybpre = (dyb * scale_ref[...]).astype(BF16)
        dpooled = []
        for g in range(len(POOL_WINDOWS)):
            dg = dybpre[:, g * gw:(g + 1) * gw]
            dwpool_ref[g] += _dot_tn(pooled[g], dg)
            dpooled.append(_dot_nt(dg, wpool_ref[g]))
        dpre_ref[...] = jnp.concatenate([dya * cq, dya * bg] + dpooled, axis=1).astype(BF16)

    return _call(
        body, name="ab_bwd_out", nsteps=nsteps,
        in_specs=[_rows(tm, d), _rows(tm, d), *_halo_specs(tm, nu, t_total), _full(vec.shape),
                  _full(conv.shape), _full(wpool.shape), _full(scale.shape), _VM],
        out_specs=[_rows(tm, 3 * da), _VM, _VM, _VM],
        out_shape=[jax.ShapeDtypeStruct((t_total, 3 * da), BF16), jax.ShapeDtypeStruct(wout.shape, F32),
                   jax.ShapeDtypeStruct(wpool.shape, F32), jax.ShapeDtypeStruct((8, d), F32)],
        scratch_shapes=[pltpu.VMEM((tm + 2 * HALO, da), F32), pltpu.VMEM((tm + 2 * HALO, da), F32)],
        args=(dx, y, u, u, u, vec, conv, wpool, scale, wout), exch=exch)


def _ab_bwd_in(dpre, u, x, dx, vec, conv, win, tm, exch=None):
    t_total, d = x.shape
    nu = u.shape[1]
    da = nu // 4
    gw = da // len(POOL_WINDOWS)
    nsteps = t_total // tm

    def body(dp_ref, dc_ref, dn_ref, up_ref, uc_ref, un_ref, x_ref, dx_ref, vec_ref, conv_ref, win_ref,
             dxin_ref, dwin_ref, dconv_ref, acc_ref, dcq_ext, q_ext, dpl_ext):
        _zero_at_start(dwin_ref, dconv_ref, acc_ref)
        i = pl.program_id(0)

        def ucols(ref, k):
            return ref[:, k * da:(k + 1) * da].astype(F32)

        def dcols(ref, k):
            return ref[:, k * da:(k + 1) * da].astype(F32)

        _fill_ext(dcq_ext, dcols(dp_ref, 1), dcols(dc_ref, 1), dcols(dn_ref, 1), i, nsteps, tm)
        _fill_ext(q_ext, ucols(up_ref, 1) * ucols(up_ref, 2), ucols(uc_ref, 1) * ucols(uc_ref, 2),
                  ucols(un_ref, 1) * ucols(un_ref, 2), i, nsteps, tm)
        _fill_ext(dpl_ext, dcols(dp_ref, 2), dcols(dc_ref, 2), dcols(dn_ref, 2), i, nsteps, tm)
        dq = (conv_ref[0:1, :] * dcq_ext[HALO + 1:HALO + 1 + tm, :] + conv_ref[1:2, :] * dcq_ext[HALO:HALO + tm, :]
              + conv_ref[2:3, :] * dcq_ext[HALO - 1:HALO - 1 + tm, :])
        dcq = dcq_ext[HALO:HALO + tm, :]
        for k in range(3):
            dconv_ref[k:k + 1, :] += _colsum(dcq * q_ext[HALO + k - 1:HALO + k - 1 + tm, :])
        dcg = dq * ucols(uc_ref, 2)
        dv = dq * ucols(uc_ref, 1)
        t_ext = i * tm - HALO + lax.broadcasted_iota(jnp.int32, (tm + 2 * HALO, 1), 0)
        dps = []
        for g, wdw in enumerate(POOL_WINDOWS):
            left = wdw // 2
            right = wdw - 1 - left
            lo, hi = g * gw, (g + 1) * gw
            dpg = dpl_ext[HALO:HALO + tm, lo:hi]
            dpl_ext[:, lo:hi] = dpl_ext[:, lo:hi] / _window_count(t_ext, wdw, t_total)
            s = dpl_ext[HALO - right:HALO - right + tm, lo:hi]
            for o in range(-right + 1, left + 1):
                s = s + dpl_ext[HALO + o:HALO + o + tm, lo:hi]
            dps.append(s - dpg)
        du = jnp.concatenate([dcols(dc_ref, 0), dcg, dv] + dps, axis=1).astype(BF16)
        xv = x_ref[...]
        g, sh, sc = vec_ref[0:1, :], vec_ref[1:2, :], vec_ref[2:3, :]
        h = _norm_mod(xv, g, sh, sc).astype(BF16)
        dh = _in_proj_bwd(h, du, win_ref, dwin_ref)
        dxn, dsh, dsc, dg = _norm_mod_bwd(dh, xv, g, sc)
        acc_ref[0:1, :] += dsh
        acc_ref[1:2, :] += dsc
        acc_ref[2:3, :] += dg
        dxin_ref[...] = dx_ref[...] + dxn

    ext = pltpu.VMEM((tm + 2 * HALO, da), F32)
    return _call(
        body, name="ab_bwd_in", nsteps=nsteps,
        in_specs=[*_halo_specs(tm, 3 * da, t_total), *_halo_specs(tm, nu, t_total), _rows(tm, d), _rows(tm, d),
                  _full(vec.shape), _full(conv.shape), _VM],
        out_specs=[_rows(tm, d), _VM, _VM, _VM],
        out_shape=[jax.ShapeDtypeStruct((t_total, d), F32), jax.ShapeDtypeStruct(win.shape, F32),
                   jax.ShapeDtypeStruct((8, da), F32), jax.ShapeDtypeStruct((8, d), F32)],
        scratch_shapes=[ext, ext, ext],
        args=(dpre, dpre, dpre, u, u, u, x, dx, vec, conv, win), exch=exch)


def _cf_bwd_out(dx, y, zc, vec, wpw2, tm, exch=None):
    t_total, d = dx.shape

    def body(dx_ref, y_ref, zc_ref, vec_ref, w_ref, dzc_ref, dw_ref, acc_ref):
        _zero_at_start(dw_ref, acc_ref)
        dxv = dx_ref[...]
        acc_ref[0:1, :] += _colsum(dxv * y_ref[...].astype(F32))
        dyf = dxv * vec_ref[0:1, :]
        acc_ref[1:2, :] += _colsum(dyf)
        dy = dyf.astype(BF16)
        zn, rstd = _layer_norm_stats(zc_ref[...].astype(F32))
        lng = vec_ref[1:2, :]
        zl = zn * lng + vec_ref[2:3, :]
        sg = _sigmoid(zl)
        dw_ref[...] += _dot_tn((zl * sg).astype(BF16), dy)
        dzl = _dot_nt(dy, w_ref[...]) * (sg * (1.0 + zl * (1.0 - sg)))
        acc_ref[2:3, :] += _colsum(dzl * zn)
        acc_ref[3:4, :] += _colsum(dzl)
        dzn = dzl * lng
        dzc = rstd * (dzn - jnp.mean(dzn, axis=-1, keepdims=True)
                      - zn * jnp.mean(dzn * zn, axis=-1, keepdims=True))
        acc_ref[4:5, :] += _colsum(dzc)
        dzc_ref[...] = dzc.astype(BF16)

    return _call(
        body, name="cf_bwd_out", nsteps=t_total // tm,
        in_specs=[_rows(tm, d), _rows(tm, d), _rows(tm, d), _full(vec.shape), _VM],
        out_specs=[_rows(tm, d), _VM, _VM],
        out_shape=[jax.ShapeDtypeStruct((t_total, d), BF16), jax.ShapeDtypeStruct(wpw2.shape, F32),
                   jax.ShapeDtypeStruct((8, d), F32)],
        args=(dx, y, zc, vec, wpw2), exch=exch)


def _cf_bwd_in(dzc, u, x, dx, vec, wdw, wpw1, tm, exch=None):
    t_total, d = x.shape
    nu = u.shape[1]
    nsteps = t_total // tm
    left = (CONF_KERNEL - 1) // 2

    def body(dp_ref, dc_ref, dn_ref, up_ref, uc_ref, un_ref, x_ref, dx_ref, vec_ref, wdw_ref, w_ref,
             dxin_ref, dw_ref, dwdw_ref, db1_ref, acc_ref, dzc_ext, z_ext, sh_ref, dz_buf):
        _zero_at_start(dw_ref, dwdw_ref, db1_ref, acc_ref)
        i = pl.program_id(0)
        _fill_ext(dzc_ext, dp_ref[...].astype(F32), dc_ref[...].astype(F32), dn_ref[...].astype(F32),
                  i, nsteps, tm)
        _glu_ext(up_ref, uc_ref, un_ref, z_ext, i, nsteps, tm)
        for lo in range(0, d, CONV_COLS):
            hi = lo + CONV_COLS
            _fill_shifts(sh_ref, dzc_ext, lo, hi, tm)
            acc = wdw_ref[0:1, lo:hi] * _shifted(sh_ref, left, tm)
            for k in range(1, CONF_KERNEL):
                acc = acc + wdw_ref[k:k + 1, lo:hi] * _shifted(sh_ref, left - k, tm)
            dz_buf[:, lo:hi] = acc
            dzc = dzc_ext[HALO:HALO + tm, lo:hi]
            _fill_shifts(sh_ref, z_ext, lo, hi, tm)
            for k in range(CONF_KERNEL):
                dwdw_ref[k:k + 1, lo:hi] += _colsum(dzc * _shifted(sh_ref, k - left, tm))
        dz = dz_buf[...]
        av = uc_ref[:, 0:d].astype(F32)
        sg = _sigmoid(uc_ref[:, d:2 * d].astype(F32))
        duf = jnp.concatenate([dz * sg, dz * av * sg * (1.0 - sg)], axis=1)
        db1_ref[0:1, :] += _colsum(duf)
        du = duf.astype(BF16)
        xv = x_ref[...]
        g, sh, sc = vec_ref[0:1, :], vec_ref[1:2, :], vec_ref[2:3, :]
        h = _norm_mod(xv, g, sh, sc).astype(BF16)
        dh = _in_proj_bwd(h, du, w_ref, dw_ref)
        dxn, dsh, dsc, dg = _norm_mod_bwd(dh, xv, g, sc)
        acc_ref[0:1, :] += dsh
        acc_ref[1:2, :] += dsc
        acc_ref[2:3, :] += dg
        dxin_ref[...] = dx_ref[...] + dxn

    ext = pltpu.VMEM((tm + 2 * HALO, d), F32)
    return _call(
        body, name="cf_bwd_in", nsteps=nsteps,
        in_specs=[*_halo_specs(tm, d, t_total), *_halo_specs(tm, nu, t_total), _rows(tm, d), _rows(tm, d),
                  _full(vec.shape), _full(wdw.shape), _VM],
        out_specs=[_rows(tm, d), _VM, _VM, _VM, _VM],
        out_shape=[jax.ShapeDtypeStruct((t_total, d), F32), jax.ShapeDtypeStruct(wpw1.shape, F32),
                   jax.ShapeDtypeStruct((32, d), F32), jax.ShapeDtypeStruct((8, nu), F32),
                   jax.ShapeDtypeStruct((8, d), F32)],
        scratch_shapes=[ext, ext, _shift_scratch(tm), pltpu.VMEM((tm, d), F32)],
        args=(dzc, dzc, dzc, u, u, u, x, dx, vec, wdw, wpw1), exch=exch)


def _mod_fwd(c_all, w_mod, b_cols):
    nl, d, ncol = w_mod.shape
    nb = c_all.shape[0]

    def body(c_ref, w_ref, b_ref, o_ref):
        cv = c_ref[...]
        ca = cv * _sigmoid(cv)
        o_ref[0] = jnp.dot(ca, w_ref[0], preferred_element_type=F32, precision=HIGHEST) + b_ref[0]

    return _pcall(
        body, name="mod_fwd", grid=(nl,),
        in_specs=[_full(c_all.shape), pl.BlockSpec((1, d, ncol), lambda l: (l, 0, 0)),
                  pl.BlockSpec((1, 1, ncol), lambda l: (l, 0, 0))],
        out_specs=pl.BlockSpec((1, nb, ncol), lambda l: (l, 0, 0)),
        out_shape=jax.ShapeDtypeStruct((nl, nb, ncol), F32),
        compiler_params=_seq_params(),
    )(c_all, w_mod, b_cols.reshape(nl, 1, ncol))


def _mod_bwd(c_all_t, dmod_cols):
    d, nb = c_all_t.shape
    nl, _, ncol = dmod_cols.shape

    def body(c_ref, dm_ref, o_ref):
        cv = c_ref[...]
        ca = cv * _sigmoid(cv)
        o_ref[0] = jnp.dot(ca, dm_ref[0], preferred_element_type=F32, precision=HIGHEST)

    return _pcall(
        body, name="mod_bwd", grid=(nl,),
        in_specs=[_full(c_all_t.shape), pl.BlockSpec((1, nb, ncol), lambda l: (l, 0, 0))],
        out_specs=pl.BlockSpec((1, d, ncol), lambda l: (l, 0, 0)),
        out_shape=jax.ShapeDtypeStruct((nl, d, ncol), F32),
        compiler_params=_seq_params(),
    )(c_all_t, dmod_cols)


def _row_block(r, c):
    if r * c <= EW_BLOCK_ELEMS:
        return r
    best = None
    for br in range(8, r, 8):
        if r % br == 0 and br * c <= EW_BLOCK_ELEMS:
            best = br
    assert best is not None, (r, c)
    return best


def _as2d(a):
    return a.reshape(-1, a.shape[-1])


def _adamw(w, gparts, m, v):
    shape = w.shape
    w2, m2, v2 = _as2d(w), _as2d(m), _as2d(v)
    g2 = [_as2d(g) for g in gparts]
    r, c = w2.shape
    br = _row_block(r, c)
    ng = len(g2)

    def body(*refs):
        w_ref, m_ref, v_ref = refs[0:3]
        g_refs = refs[3:3 + ng]
        g = g_refs[0][...]
        for gr in g_refs[1:]:
            g = g + gr[...]
        _adamw_update(g, w_ref[...], m_ref[...], v_ref[...], refs[3 + ng:])

    spec = pl.BlockSpec((br, c), lambda i: (i, 0))
    outs = _pcall(
        body, name="adamw", grid=(r // br,),
        in_specs=[spec] * (3 + ng), out_specs=[spec] * 4,
        out_shape=[jax.ShapeDtypeStruct((r, c), F32)] * 4,
        compiler_params=_seq_params(),
    )(w2, m2, v2, *g2)
    return tuple(o.reshape(shape) for o in outs)


def _adamw_update(g, w, m, v, out_refs):
    go_ref, d_ref, mo_ref, vo_ref = out_refs
    mn = ADAM_B1 * m + (1.0 - ADAM_B1) * g
    vn = ADAM_B2 * v + (1.0 - ADAM_B2) * (g * g)
    m_hat = mn / (1.0 - ADAM_B1 ** ADAM_STEP)
    v_hat = vn / (1.0 - ADAM_B2 ** ADAM_STEP)
    go_ref[...] = g.reshape(go_ref.shape)
    d_ref[...] = (-ADAM_LR * (m_hat / (jnp.sqrt(v_hat) + ADAM_EPS) + ADAM_WD * w)).reshape(d_ref.shape)
    mo_ref[...] = mn.reshape(mo_ref.shape)
    vo_ref[...] = vn.reshape(vo_ref.shape)


def _adamw_partials(w, partials, m, v):
    nl, a, b = w.shape
    br = _row_block(a, b)
    nb = a // br

    def body(*refs):
        w_ref, m_ref, v_ref = refs[0:3]
        p_refs = refs[3:3 + nl]
        out_refs = refs[3 + nl:]
        for layer in range(nl):
            @pl.when(pl.program_id(0) == layer)
            def _(layer=layer):
                halves = []
                for core in range(2):
                    acc = p_refs[layer][core, 0].astype(F32)
                    for chip in range(1, N_CHIPS):
                        acc = acc + p_refs[layer][core, chip].astype(F32)
                    halves.append(acc)
                _adamw_update(halves[0] + halves[1], w_ref[...], m_ref[...], v_ref[...], out_refs)

    def part_spec(layer):
        def index(l, i):
            return 0, 0, jnp.where(l == layer, i, jnp.where(l < layer, 0, nb - 1)), 0
        return pl.BlockSpec((2, N_CHIPS, br, b), index)

    spec = pl.BlockSpec((br, b), lambda l, i: (l * nb + i, 0))
    outs = _pcall(
        body, name="adamw_partials", grid=(nl, nb),
        in_specs=[spec] * 3 + [part_spec(layer) for layer in range(nl)], out_specs=[spec] * 4,
        out_shape=[jax.ShapeDtypeStruct((nl * a, b), F32)] * 4,
        compiler_params=pltpu.CompilerParams(dimension_semantics=("arbitrary", "arbitrary"),
                                             vmem_limit_bytes=VMEM_LIMIT),
    )(_as2d(w), _as2d(m), _as2d(v), *partials)
    return tuple(o.reshape(w.shape) for o in outs)


def _allgather8(block, with_sum, exch=None):
    m_per, n = block.shape
    nx = 0 if exch is None else len(exch.arrs)
    nvm = 2 if with_sum else 1

    def body(x_ref, *rest):
        xin, out_ref = rest[:nx], rest[nx]
        sum_ref = rest[nx + 1] if with_sum else None
        xout = rest[nx + nvm:2 * nx + nvm]
        send_sems, recv_sems, local_sem = rest[2 * nx + nvm:2 * nx + nvm + 3]
        xsems = rest[2 * nx + nvm + 3:]
        if exch is not None:
            exch.start(xin, xout, xsems)
        x, y, c = _place()
        me, sibling = (x, y, c), (x, y, 1 - c)
        chips = [(1 - x, y), (x, 1 - y), (1 - x, 1 - y)]

        def rows(px, py, pc):
            return out_ref.at[pl.ds((4 * px + 2 * py + pc) * m_per, m_per), :]

        def copy(k, blk, to, src=None):
            return pltpu.make_async_remote_copy(
                src_ref=rows(*blk) if src is None else src, dst_ref=rows(*blk),
                send_sem=send_sems.at[k], recv_sem=recv_sems.at[k], device_id=to, device_id_type=MESH)

        mine = pltpu.make_async_copy(x_ref, rows(*me), local_sem)
        mine.start()
        first = [copy(0, me, sibling, src=x_ref)]
        first += [copy(1 + j, me, (*chip, c), src=x_ref) for j, chip in enumerate(chips)]
        for cp in first:
            cp.start()
        passed = [copy(4 + j, (*chip, c), sibling) for j, chip in enumerate(chips)]
        for j, chip in enumerate(chips):
            copy(1 + j, (*chip, c), me).wait_recv()
            passed[j].start()
        copy(0, sibling, me).wait_recv()
        for j, chip in enumerate(chips):
            copy(4 + j, (*chip, 1 - c), me).wait_recv()
        for cp in first + passed:
            cp.wait_send()
        mine.wait()
        if exch is not None:
            exch.mid(xin, xout, xsems)
            exch.wait(xin, xout, xsems)
        if with_sum:
            acc = out_ref[0:m_per, :]
            for k in range(1, N_DEV):
                acc = acc + out_ref[k * m_per:(k + 1) * m_per, :]
            sum_ref[...] = acc

    out_shape = [jax.ShapeDtypeStruct((N_DEV * m_per, n), F32)]
    out_specs = [_VM]
    if with_sum:
        out_shape.append(jax.ShapeDtypeStruct((m_per, n), F32))
        out_specs.append(_VM)
    res = _pcall(
        body, name=("allgather8_sum" if with_sum else "allgather8") + ("" if exch is None else "_" + exch.tag),
        in_specs=[_VM] + [_ANY] * nx, out_specs=out_specs + [_ANY] * nx,
        out_shape=out_shape + ([] if exch is None else exch.out_shapes()),
        scratch_shapes=[pltpu.SemaphoreType.DMA((7,)), pltpu.SemaphoreType.DMA((7,)), pltpu.SemaphoreType.DMA]
        + ([] if exch is None else exch.sems()),
        compiler_params=pltpu.CompilerParams(vmem_limit_bytes=VMEM_LIMIT),
    )(block, *([] if exch is None else exch.arrs))
    return list(res[:nvm]), list(res[nvm:])


def _my_cols(full, chip):
    w = full.shape[-1] // N_CHIPS
    return lax.dynamic_slice_in_dim(full, chip * w, w, axis=full.ndim - 1)


def _pad_rows(a, rows):
    return jnp.pad(a, ((0, rows - a.shape[0]), (0, 0)))


def _to_lanes(a):
    flat = a.reshape(-1)
    n = -(-flat.shape[0] // (8 * LANES)) * (8 * LANES)
    return jnp.pad(flat, (0, n - flat.shape[0])).reshape(-1, LANES)


class _Packer:
    def __init__(self):
        self.items = []
        self.rows = 0

    def add(self, name, a):
        lanes = _to_lanes(a)
        self.items.append((name, self.rows, a.shape, lanes))
        self.rows += lanes.shape[0]

    def pack(self):
        total = -(-self.rows // 8) * 8
        return _pad_rows(jnp.concatenate([it[3] for it in self.items], axis=0), total)

    def unpack(self, buf):
        out = {}
        for name, row, shape, lanes in self.items:
            size = 1
            for s in shape:
                size *= s
            out[name] = buf[row:row + lanes.shape[0]].reshape(-1)[:size].reshape(shape)
        return out


TM_SEQ = 512
TM_FFN = 256


LAYER_KEYS = ("in", "out", "gate", "up", "down")
BLOCKED_KEYS = ("in",)
TRANSPOSED = ("ffn_w_gate", "ffn_w_up")


def _layer_big_names(layer):
    i = layer // 2
    mix = (("ab_w_in", i), ("ab_w_out", i)) if layer % 2 == 0 else (("cf_w_pw1", i), ("cf_w_pw2", i))
    return dict(zip(LAYER_KEYS, mix + (("ffn_w_gate", layer), ("ffn_w_up", layer), ("ffn_w_down", layer))))


def _unpack_weight(key, g):
    g = g.reshape(N_CHIPS, -1, g.shape[-1])
    return g if key in BLOCKED_KEYS else g.reshape(-1, g.shape[-1])


def _chunk_grad(key, dw):
    parts = dw if key in BLOCKED_KEYS else dw.reshape(N_CHIPS, -1, dw.shape[-1])
    return parts.astype(BF16)


def _local_step(x, target, mods, p, shards, first_in):
    t_total, d = x.shape
    depth = mods.shape[0]
    tm = min(TM_SEQ, t_total)
    tmf = min(TM_FFN, t_total)
    saved = []
    xin = x
    weights = [{} for _ in range(depth)]

    def carried(stage, layer):
        if layer == 0:
            return {"in": (0, ("out", "gate")), "mix": (0, ("up", "down")), "ffn": (1, ("in", "out", "gate", "up"))}[stage]
        return {"in": (layer, ("down",)), "mix": (layer + 1, ("in", "out")), "ffn": (layer + 1, ("gate", "up"))}[stage]

    def gather(stage, layer):
        of, keys = carried(stage, layer)
        if of >= depth:
            return None
        return _Gather([shards[of][k].reshape(2, -1, shards[of][k].shape[-1]) for k in keys])

    def keep(stage, layer, arrs):
        of, keys = carried(stage, layer)
        for k, g in zip(keys, arrs):
            weights[of][k] = _unpack_weight(k, g)

    weights[0]["in"] = _unpack_weight("in", first_in)
    for layer in range(depth):
        i = layer // 2
        lw = weights[layer]
        sh1, sc1, g1, sh2, sc2, g2 = (mods[layer, k:k + 1] for k in range(6))
        vec_in = jnp.concatenate([p["norm_mix_g"][layer:layer + 1], sh1, sc1], axis=0)
        bias = None if layer % 2 == 0 else p["cf_b_pw1"][i:i + 1]
        (u,), arrived = _in_proj(xin, vec_in, lw["in"], bias, tm, exch=gather("in", layer))
        keep("in", layer, arrived)
        if layer % 2 == 0:
            (y, x2), arrived = _ab_fwd(u, xin, g1, p["ab_conv"][i], p["ab_w_pool"][i].astype(BF16),
                                       p["ab_pool_scale"][i:i + 1], lw["out"], tm, exch=gather("mix", layer))
            zc = None
        else:
            vec_cf = jnp.concatenate([g1, p["cf_b_dw"][i:i + 1], p["cf_ln_g"][i:i + 1], p["cf_ln_b"][i:i + 1],
                                      p["cf_b_pw2"][i:i + 1]], axis=0)
            (zc, y, x2), arrived = _cf_fwd(u, xin, vec_cf, _pad_rows(p["cf_w_dw"][i], 32), lw["out"], tm,
                                           exch=gather("mix", layer))
        keep("mix", layer, arrived)
        vec_ffn = jnp.concatenate([p["norm_ffn_g"][layer:layer + 1], sh2, sc2, g2], axis=0)
        (a, b, fout, x3), arrived = _ffn_fwd(x2, vec_ffn, lw["gate"], lw["up"], lw["down"], tmf,
                                             exch=gather("ffn", layer))
        keep("ffn", layer, arrived)
        saved.append((xin, u, y, zc, x2, a, b, fout))
        xin = x3

    (dx, fin), _ = _final_fwd_bwd(xin, target, p["final_norm_g"].reshape(1, d), tm)
    grads = {"final_norm_g": fin[0], "loss": fin[1, 0:1]}
    per_layer = {k: [None] * depth for k in ("norm_mix_g", "norm_ffn_g")}
    half = {k: [None] * (depth // 2) for k in (
        "ab_conv", "ab_w_pool", "ab_pool_scale", "cf_b_pw1", "cf_w_dw", "cf_b_dw", "cf_ln_g", "cf_ln_b", "cf_b_pw2")}
    dmods = [None] * depth
    received = {}
    pending = None
    for layer in reversed(range(depth)):
        i = layer // 2
        lw = weights[layer]
        xin, u, y, zc, x2, a, b, fout = saved[layer]
        sh1, sc1, g1, sh2, sc2, g2 = (mods[layer, k:k + 1] for k in range(6))
        above = _Scatter([pending]) if pending is not None else None
        (da, db, dwd, acc_d), arrived = _ffn_bwd_down(dx, fout, a, b, g2, lw["down"], tmf, exch=above)
        if pending is not None:
            received[(layer + 1, "in")] = arrived[0]
        vec_ffn = jnp.concatenate([p["norm_ffn_g"][layer:layer + 1], sh2, sc2], axis=0)
        (dx2, dwg, dwu, acc_u), arrived = _ffn_bwd_up(da, db, x2, dx, vec_ffn, lw["gate"], lw["up"], tmf,
                                                      exch=_Scatter([_chunk_grad("down", dwd)]))
        received[(layer, "down")] = arrived[0]
        per_layer["norm_ffn_g"][layer] = acc_u[2]
        vec_in = jnp.concatenate([p["norm_mix_g"][layer:layer + 1], sh1, sc1], axis=0)
        send_gate = _Scatter([_chunk_grad("gate", dwg)])
        if layer % 2 == 0:
            (dpre, dwout, dwpool, acc_o), arrived = _ab_bwd_out(
                dx2, y, u, g1, p["ab_conv"][i], p["ab_w_pool"][i].astype(BF16), p["ab_pool_scale"][i:i + 1],
                lw["out"], tm, exch=send_gate)
            received[(layer, "gate")] = arrived[0]
            send_up_out = _Scatter([_chunk_grad("up", dwu), _chunk_grad("out", dwout)])
            (dx, dwin, dconv, acc_i), arrived = _ab_bwd_in(dpre, u, xin, dx2, vec_in, p["ab_conv"][i], lw["in"], tm,
                                                           exch=send_up_out)
            half["ab_w_pool"][i] = dwpool
            half["ab_pool_scale"][i] = acc_o[1, 0:d // 2]
            half["ab_conv"][i] = dconv[0:3]
        else:
            vec_cf = jnp.concatenate([g1, p["cf_ln_g"][i:i + 1], p["cf_ln_b"][i:i + 1]], axis=0)
            (dzc, dwout, acc_o), arrived = _cf_bwd_out(dx2, y, zc, vec_cf, lw["out"], tm, exch=send_gate)
            received[(layer, "gate")] = arrived[0]
            send_up_out = _Scatter([_chunk_grad("up", dwu), _chunk_grad("out", dwout)])
            (dx, dwin, dwdw, db1, acc_i), arrived = _cf_bwd_in(
                dzc, u, xin, dx2, vec_in, _pad_rows(p["cf_w_dw"][i], 32), lw["in"], tm, exch=send_up_out)
            half["cf_b_pw2"][i] = acc_o[1]
            half["cf_ln_g"][i] = acc_o[2]
            half["cf_ln_b"][i] = acc_o[3]
            half["cf_b_dw"][i] = acc_o[4]
            half["cf_w_dw"][i] = dwdw[0:CONF_KERNEL]
            half["cf_b_pw1"][i] = db1[0]
        received[(layer, "up")], received[(layer, "out")] = arrived
        per_layer["norm_mix_g"][layer] = acc_i[2]
        dmods[layer] = jnp.stack([acc_i[0], acc_i[1], acc_o[0], acc_u[0], acc_u[1], acc_d[0]], axis=0)
        pending = _chunk_grad("in", dwin)
    for k, v in {**per_layer, **half}.items():
        grads[k] = jnp.stack(v, axis=0)
    return dx, grads, jnp.stack(dmods, axis=0), received, pending


SMALL_COLS = ("ab_conv", "cf_b_pw1", "cf_w_dw", "cf_b_dw", "cf_ln_g", "cf_ln_b", "cf_b_pw2")
SMALL_REPL = ("norm_mix_g", "norm_ffn_g", "ab_w_pool", "ab_pool_scale", "final_norm_g")
WEIGHTS = ("norm_mix_g", "norm_ffn_g", "w_mod", "b_mod", "ab_w_in", "ab_conv", "ab_w_pool", "ab_pool_scale",
           "ab_w_out", "cf_w_pw1", "cf_b_pw1", "cf_w_dw", "cf_b_dw", "cf_ln_g", "cf_ln_b", "cf_w_pw2",
           "cf_b_pw2", "ffn_w_gate", "ffn_w_up", "ffn_w_down", "final_norm_g")


def kernel(x, c, norm_mix_g, norm_ffn_g, w_mod, b_mod, ab_w_in, ab_conv, ab_w_pool, ab_pool_scale, ab_w_out, cf_w_pw1, cf_b_pw1, cf_w_dw, cf_b_dw, cf_ln_g, cf_ln_b, cf_w_pw2, cf_b_pw2, ffn_w_gate, ffn_w_up, ffn_w_down, final_norm_g, loss_target, m_norm_mix_g, m_norm_ffn_g, m_w_mod, m_b_mod, m_ab_w_in, m_ab_conv, m_ab_w_pool, m_ab_pool_scale, m_ab_w_out, m_cf_w_pw1, m_cf_b_pw1, m_cf_w_dw, m_cf_b_dw, m_cf_ln_g, m_cf_ln_b, m_cf_w_pw2, m_cf_b_pw2, m_ffn_w_gate, m_ffn_w_up, m_ffn_w_down, m_final_norm_g, v_norm_mix_g, v_norm_ffn_g, v_w_mod, v_b_mod, v_ab_w_in, v_ab_conv, v_ab_w_pool, v_ab_pool_scale, v_ab_w_out, v_cf_w_pw1, v_cf_b_pw1, v_cf_w_dw, v_cf_b_dw, v_cf_ln_g, v_cf_ln_b, v_cf_w_pw2, v_cf_b_pw2, v_ffn_w_gate, v_ffn_w_up, v_ffn_w_down, v_final_norm_g):
    w = dict(norm_mix_g=norm_mix_g, norm_ffn_g=norm_ffn_g, w_mod=w_mod, b_mod=b_mod, ab_w_in=ab_w_in,
             ab_conv=ab_conv, ab_w_pool=ab_w_pool, ab_pool_scale=ab_pool_scale, ab_w_out=ab_w_out,
             cf_w_pw1=cf_w_pw1, cf_b_pw1=cf_b_pw1, cf_w_dw=cf_w_dw, cf_b_dw=cf_b_dw, cf_ln_g=cf_ln_g,
             cf_ln_b=cf_ln_b, cf_w_pw2=cf_w_pw2, cf_b_pw2=cf_b_pw2, ffn_w_gate=ffn_w_gate, ffn_w_up=ffn_w_up,
             ffn_w_down=ffn_w_down, final_norm_g=final_norm_g)
    mom = dict(norm_mix_g=m_norm_mix_g, norm_ffn_g=m_norm_ffn_g, w_mod=m_w_mod, b_mod=m_b_mod, ab_w_in=m_ab_w_in,
               ab_conv=m_ab_conv, ab_w_pool=m_ab_w_pool, ab_pool_scale=m_ab_pool_scale, ab_w_out=m_ab_w_out,
               cf_w_pw1=m_cf_w_pw1, cf_b_pw1=m_cf_b_pw1, cf_w_dw=m_cf_w_dw, cf_b_dw=m_cf_b_dw, cf_ln_g=m_cf_ln_g,
               cf_ln_b=m_cf_ln_b, cf_w_pw2=m_cf_w_pw2, cf_b_pw2=m_cf_b_pw2, ffn_w_gate=m_ffn_w_gate,
               ffn_w_up=m_ffn_w_up, ffn_w_down=m_ffn_w_down, final_norm_g=m_final_norm_g)
    var = dict(norm_mix_g=v_norm_mix_g, norm_ffn_g=v_norm_ffn_g, w_mod=v_w_mod, b_mod=v_b_mod, ab_w_in=v_ab_w_in,
               ab_conv=v_ab_conv, ab_w_pool=v_ab_w_pool, ab_pool_scale=v_ab_pool_scale, ab_w_out=v_ab_w_out,
               cf_w_pw1=v_cf_w_pw1, cf_b_pw1=v_cf_b_pw1, cf_w_dw=v_cf_w_dw, cf_b_dw=v_cf_b_dw, cf_ln_g=v_cf_ln_g,
               cf_ln_b=v_cf_ln_b, cf_w_pw2=v_cf_w_pw2, cf_b_pw2=v_cf_b_pw2, ffn_w_gate=v_ffn_w_gate,
               ffn_w_up=v_ffn_w_up, ffn_w_down=v_ffn_w_down, final_norm_g=v_final_norm_g)
    px, py, pc = _place()
    chip = 2 * px + py
    dev = 2 * chip + pc
    depth, d, mod_cols = w_mod.shape
    x = x[0]
    target = loss_target[0]

    def rows_major(name, t):
        return jnp.swapaxes(t, 1, 2) if name in TRANSPOSED else t

    shards = [{k: rows_major(name, w[name])[idx].astype(BF16) for k, (name, idx) in _layer_big_names(layer).items()}
              for layer in range(depth)]

    small_in = _Packer()
    small_in.add("c", c)
    for name in SMALL_COLS:
        small_in.add(name, w[name])
    first_in = shards[0]["in"]
    (gathered,), (first_in,) = _allgather8(small_in.pack(), with_sum=False,
                                           exch=_Gather([first_in.reshape(2, -1, first_in.shape[-1])]))
    gathered = gathered.reshape(N_DEV, -1, LANES)
    per_dev = [small_in.unpack(gathered[k]) for k in range(N_DEV)]
    c_all = jnp.concatenate([pd["c"] for pd in per_dev], axis=0)
    params = {name: jnp.concatenate([per_dev[2 * k][name] for k in range(N_CHIPS)], axis=-1)
              for name in SMALL_COLS}
    for name in SMALL_REPL:
        params[name] = w[name]

    mod_part = _mod_fwd(c_all, w_mod, _my_cols(b_mod, chip))
    (mod_all,), _ = _allgather8(mod_part.reshape(-1, LANES), with_sum=False)
    mod_all = mod_all.reshape(N_CHIPS, 2, depth, N_DEV, mod_cols)[:, 0]
    mod_all = jnp.moveaxis(mod_all, 0, 2).reshape(depth, N_DEV, N_CHIPS * mod_cols)
    mods = lax.dynamic_index_in_dim(mod_all, dev, axis=1, keepdims=False).reshape(depth, 6, d)

    grad_x, grads, dmods, received, last_chunk = _local_step(x, target, mods, params, shards, first_in)

    small_out = _Packer()
    small_out.add("dmods", dmods)
    for name in ("loss",) + SMALL_REPL + SMALL_COLS:
        small_out.add(name, grads[name])
    (parts_all, parts_sum), (received[(0, "in")],) = _allgather8(small_out.pack(), with_sum=True,
                                                                 exch=_Scatter([last_chunk]))
    small_sum = small_out.unpack(parts_sum)
    loss = small_sum["loss"][0]
    dmods_all = jnp.stack([small_out.unpack(pa)["dmods"] for pa in parts_all.reshape(N_DEV, -1, LANES)], axis=1)
    dmods_all = dmods_all.reshape(depth, N_DEV, 6 * d)

    g_final = {}
    g_final["w_mod"] = [_mod_bwd(c_all.T, _my_cols(dmods_all, chip))]
    g_final["b_mod"] = [small_sum["dmods"].reshape(depth, 6 * d)]
    for name in SMALL_REPL:
        g_final[name] = [small_sum[name]]
    for name in SMALL_COLS:
        g_final[name] = [_my_cols(small_sum[name], chip)]

    updates = {}
    for name in WEIGHTS:
        parts = [received[(layer, k)] for layer in range(depth)
                 for k, (other, _) in _layer_big_names(layer).items() if other == name]
        if parts:
            outs = _adamw_partials(rows_major(name, w[name]), parts, rows_major(name, mom[name]),
                                   rows_major(name, var[name]))
            updates[name] = [rows_major(name, o) for o in outs]
        else:
            updates[name] = _adamw(w[name], g_final[name], mom[name], var[name])
    return (loss, grad_x[None], *[updates[name][0] for name in WEIGHTS], *[updates[name][1] for name in WEIGHTS],
            *[updates[name][2] for name in WEIGHTS], *[updates[name][3] for name in WEIGHTS])
```

```python
import functools

import jax
import jax.numpy as jnp
from jax import lax
from jax.experimental import pallas as pl
from jax.experimental.pallas import tpu as pltpu

F32 = jnp.float32
BF16 = jnp.bfloat16
RMS_EPS = 1e-6
LN_EPS = 1e-5
ADAM_LR = 0.001
ADAM_B1 = 0.9
ADAM_B2 = 0.999
ADAM_EPS = 1e-08
ADAM_WD = 0.01
ADAM_STEP = 10
POOL_WINDOWS = (2, 4, 8, 16)
CONF_KERNEL = 31
N_CHIPS = 4
N_DEV = 8
HALO = 16
CONV_COLS = 256
FFN_CHUNK = 1536
LANES = 1024
VMEM_LIMIT = 56 * 1024 * 1024
EW_BLOCK_ELEMS = 256 * 1024
MESH = pl.DeviceIdType.MESH
HIGHEST = lax.Precision.HIGHEST

_pcall = pl.pallas_call


def _dot(a, b):
    return jnp.dot(a, b, preferred_element_type=F32)


def _dot_tn(a, b):
    return lax.dot_general(a, b, (((0,), (0,)), ((), ())), preferred_element_type=F32)


def _dot_nt(a, b):
    return lax.dot_general(a, b, (((1,), (1,)), ((), ())), preferred_element_type=F32)


def _colsum(v):
    return jnp.sum(v, axis=0, keepdims=True)


def _sigmoid(v):
    return 1.0 / (1.0 + jnp.exp(-v))


def _rows(tm, c):
    return pl.BlockSpec((tm, c), lambda i: (i, 0))


def _full(shape):
    nd = len(shape)
    return pl.BlockSpec(shape, lambda i: (0,) * nd)


_VM = pl.BlockSpec(memory_space=pltpu.VMEM)
_ANY = pl.BlockSpec(memory_space=pl.ANY)


def _halo_specs(tm, c, t_total):
    r = tm // HALO
    last = t_total // HALO - 1
    prev = pl.BlockSpec((HALO, c), lambda i: (jnp.maximum(i * r - 1, 0), 0))
    nxt = pl.BlockSpec((HALO, c), lambda i: (jnp.minimum((i + 1) * r, last), 0))
    return prev, _rows(tm, c), nxt


def _seq_params():
    return pltpu.CompilerParams(dimension_semantics=("arbitrary",), vmem_limit_bytes=VMEM_LIMIT)


def _place():
    return lax.axis_index("x"), lax.axis_index("y"), lax.axis_index("c")


def _peer_chips(x, y):
    return [(1 - x, y), (x, 1 - y), (1 - x, 1 - y)]


class _Gather:
    tag = "gather"

    def __init__(self, arrs):
        self.arrs = list(arrs)

    def out_shapes(self):
        return [jax.ShapeDtypeStruct((N_CHIPS,) + a.shape, a.dtype) for a in self.arrs]

    def sems(self):
        n = len(self.arrs)
        return [pltpu.SemaphoreType.DMA((3 * n,)) for _ in range(4)] + [pltpu.SemaphoreType.DMA((n,))]

    def _copies(self, ins, outs, sems, kinds):
        ici_send, ici_recv, d2d_send, d2d_recv, local_sems = sems
        x, y, c = _place()
        me = 2 * x + y
        found = {kind: [] for kind in kinds}
        for j in range(len(ins)):
            if "local" in kinds:
                found["local"].append(pltpu.make_async_copy(ins[j], outs[j].at[me], local_sems.at[j]))
            for k, (px, py) in enumerate(_peer_chips(x, y)):
                ici = dict(send_sem=ici_send.at[3 * j + k], recv_sem=ici_recv.at[3 * j + k],
                           device_id=(px, py, c), device_id_type=MESH)
                d2d = dict(send_sem=d2d_send.at[3 * j + k], recv_sem=d2d_recv.at[3 * j + k],
                           device_id=(x, y, 1 - c), device_id_type=MESH)
                theirs = outs[j].at[2 * px + py]
                if "send" in kinds:
                    found["send"].append(pltpu.make_async_remote_copy(
                        src_ref=ins[j].at[c], dst_ref=outs[j].at[me, c], **ici))
                if "arrival" in kinds:
                    found["arrival"].append(pltpu.make_async_remote_copy(
                        src_ref=ins[j].at[c], dst_ref=theirs.at[c], **ici))
                if "pass" in kinds:
                    found["pass"].append(pltpu.make_async_remote_copy(
                        src_ref=theirs.at[c], dst_ref=theirs.at[c], **d2d))
                if "passed" in kinds:
                    found["passed"].append(pltpu.make_async_remote_copy(
                        src_ref=theirs.at[c], dst_ref=theirs.at[1 - c], **d2d))
        return found

    def start(self, ins, outs, sems):
        found = self._copies(ins, outs, sems, ("local", "send"))
        for cp in found["local"] + found["send"]:
            cp.start()

    def mid(self, ins, outs, sems):
        found = self._copies(ins, outs, sems, ("arrival", "pass"))
        for arrived, onward in zip(found["arrival"], found["pass"]):
            arrived.wait_recv()
            onward.start()

    def wait(self, ins, outs, sems):
        found = self._copies(ins, outs, sems, ("local", "send", "pass", "passed"))
        for cp in found["passed"]:
            cp.wait_recv()
        for cp in found["send"] + found["pass"]:
            cp.wait_send()
        for cp in found["local"]:
            cp.wait()


class _Scatter:
    tag = "scatter"

    def __init__(self, arrs):
        self.arrs = list(arrs)

    def out_shapes(self):
        return [jax.ShapeDtypeStruct((2,) + a.shape, a.dtype) for a in self.arrs]

    def sems(self):
        n = len(self.arrs)
        dma = pltpu.SemaphoreType.DMA
        return [dma((3 * n,)), dma((3 * n,)), dma((4 * n,)), dma((4 * n,)), dma((n,))]

    def _copies(self, ins, outs, sems, kinds):
        ici_send, ici_recv, d2d_send, d2d_recv, local_sems = sems
        x, y, c = _place()
        me = 2 * x + y
        found = {kind: [] for kind in kinds}
        for j in range(len(ins)):
            def d2d(k):
                return dict(send_sem=d2d_send.at[4 * j + k], recv_sem=d2d_recv.at[4 * j + k],
                            device_id=(x, y, 1 - c), device_id_type=MESH)

            if "local" in kinds:
                found["local"].append(pltpu.make_async_copy(ins[j].at[me], outs[j].at[0, me], local_sems.at[j]))
            if "own" in kinds:
                found["own"].append(pltpu.make_async_remote_copy(
                    src_ref=ins[j].at[me], dst_ref=outs[j].at[1, me], **d2d(3)))
            if "passed" in kinds:
                found["passed"].append(pltpu.make_async_remote_copy(
                    src_ref=ins[j].at[me], dst_ref=outs[j].at[1, me], **d2d(3)))
            for k, (px, py) in enumerate(_peer_chips(x, y)):
                ici = dict(send_sem=ici_send.at[3 * j + k], recv_sem=ici_recv.at[3 * j + k],
                           device_id=(px, py, c), device_id_type=MESH)
                peer = 2 * px + py
                if "send" in kinds:
                    found["send"].append(pltpu.make_async_remote_copy(
                        src_ref=ins[j].at[peer], dst_ref=outs[j].at[0, me], **ici))
                if "arrival" in kinds:
                    found["arrival"].append(pltpu.make_async_remote_copy(
                        src_ref=ins[j].at[me], dst_ref=outs[j].at[0, peer], **ici))
                if "pass" in kinds:
                    found["pass"].append(pltpu.make_async_remote_copy(
                        src_ref=outs[j].at[0, peer], dst_ref=outs[j].at[1, peer], **d2d(k)))
                if "passed" in kinds:
                    found["passed"].append(pltpu.make_async_remote_copy(
                        src_ref=outs[j].at[0, peer], dst_ref=outs[j].at[1, peer], **d2d(k)))
        return found

    def start(self, ins, outs, sems):
        found = self._copies(ins, outs, sems, ("local", "own", "send"))
        for cp in found["local"] + found["own"] + found["send"]:
            cp.start()

    def mid(self, ins, outs, sems):
        found = self._copies(ins, outs, sems, ("arrival", "pass"))
        for arrived, onward in zip(found["arrival"], found["pass"]):
            arrived.wait_recv()
            onward.start()

    def wait(self, ins, outs, sems):
        found = self._copies(ins, outs, sems, ("local", "own", "send", "pass", "passed"))
        for cp in found["passed"]:
            cp.wait_recv()
        for cp in found["own"] + found["send"] + found["pass"]:
            cp.wait_send()
        for cp in found["local"]:
            cp.wait()


def _call(body, *, name, nsteps, in_specs, out_specs, out_shape, args, scratch_shapes=(), exch=None):
    if exch is None:
        outs = _pcall(body, name=name, grid=(nsteps,), in_specs=list(in_specs), out_specs=list(out_specs),
                      out_shape=list(out_shape), scratch_shapes=list(scratch_shapes),
                      compiler_params=_seq_params())(*args)
        return list(outs), []
    n, ni, no, ns = len(exch.arrs), len(in_specs), len(out_specs), len(scratch_shapes)

    def hosted(*refs):
        xin = refs[ni:ni + n]
        xout = refs[ni + n + no:ni + 2 * n + no]
        scr = refs[ni + 2 * n + no:]

        @pl.when(pl.program_id(0) == 0)
        def _():
            exch.start(xin, xout, scr[ns:])

        body(*refs[:ni], *refs[ni + n:ni + n + no], *scr[:ns])

        @pl.when(pl.program_id(0) == max(nsteps - 3, 0))
        def _():
            exch.mid(xin, xout, scr[ns:])

        @pl.when(pl.program_id(0) == nsteps - 1)
        def _():
            exch.wait(xin, xout, scr[ns:])

    outs = _pcall(hosted, name=name + "_" + exch.tag, grid=(nsteps,),
                  in_specs=[*in_specs, *[_ANY] * n], out_specs=[*out_specs, *[_ANY] * n],
                  out_shape=[*out_shape, *exch.out_shapes()],
                  scratch_shapes=[*scratch_shapes, *exch.sems()],
                  compiler_params=_seq_params())(*args, *exch.arrs)
    return list(outs[:no]), list(outs[no:])


def _rms(x):
    r = lax.rsqrt(jnp.mean(x * x, axis=-1, keepdims=True) + RMS_EPS)
    return x * r, r


def _norm_mod(x, g, sh, sc):
    xhat, _ = _rms(x)
    return xhat * g * (1.0 + sc) + sh


def _norm_mod_bwd(dh, x, g, sc):
    xhat, r = _rms(x)
    n = xhat * g
    dsh = _colsum(dh)
    dsc = _colsum(dh * n)
    dn = dh * (1.0 + sc)
    dg = _colsum(dn * xhat)
    dxn = dn * g
    dx = r * (dxn - xhat * jnp.mean(dxn * xhat, axis=-1, keepdims=True))
    return dx, dsh, dsc, dg


def _fill_ext(ext_ref, prev, cur, nxt, i, nsteps, tm):
    ext_ref[0:HALO, :] = jnp.where(i > 0, prev, 0.0)
    ext_ref[HALO:HALO + tm, :] = cur
    ext_ref[HALO + tm:HALO + tm + HALO, :] = jnp.where(i < nsteps - 1, nxt, 0.0)


def _shift_scratch(tm):
    return pltpu.VMEM((8, tm + 2 * HALO - 8, CONV_COLS), F32)


def _fill_shifts(sh_ref, ext_ref, lo, hi, tm):
    for b in range(8):
        sh_ref[b] = ext_ref[b:b + tm + 2 * HALO - 8, lo:hi]


def _shifted(sh_ref, offset, tm):
    b = offset % 8
    start = HALO + offset - b
    return sh_ref[b, start:start + tm, :]


def _window_count(t, wdw, t_total):
    left = wdw // 2
    right = wdw - 1 - left
    cnt = jnp.minimum(t + right, t_total - 1) - jnp.maximum(t - left, 0) + 1
    return jnp.maximum(cnt, 1).astype(F32)


def _in_proj(x, vec, w, bias, tm, exch=None):
    t_total, d = x.shape
    nk = w.shape[2]
    n = N_CHIPS * nk
    has_bias = bias is not None

    def body(*refs):
        if has_bias:
            x_ref, vec_ref, w_ref, b_ref, u_ref = refs
        else:
            x_ref, vec_ref, w_ref, u_ref = refs
        h = _norm_mod(x_ref[...], vec_ref[0:1, :], vec_ref[1:2, :], vec_ref[2:3, :])
        h = h.astype(BF16)
        for k in range(N_CHIPS):
            u = _dot(h, w_ref[k])
            if has_bias:
                u = u + b_ref[:, k * nk:(k + 1) * nk]
            u_ref[:, k * nk:(k + 1) * nk] = u.astype(BF16)

    in_specs = [_rows(tm, d), _full(vec.shape), _VM]
    args = [x, vec, w]
    if has_bias:
        in_specs.append(_full(bias.shape))
        args.append(bias)
    return _call(
        body, name="in_proj_bias" if has_bias else "in_proj", nsteps=t_total // tm,
        in_specs=in_specs, out_specs=[_rows(tm, n)], out_shape=[jax.ShapeDtypeStruct((t_total, n), BF16)],
        args=args, exch=exch)


def _in_proj_bwd(h, du, w_ref, dw_ref):
    nk = w_ref.shape[2]
    dh = None
    for k in range(N_CHIPS):
        duk = du[:, k * nk:(k + 1) * nk]
        dw_ref[k] += _dot_tn(h, duk)
        part = _dot_nt(duk, w_ref[k])
        dh = part if dh is None else dh + part
    return dh


def _ab_core(up_ref, uc_ref, un_ref, conv_ref, wpool_ref, q_ext, p_ext, i, nsteps, tm, t_total):
    da = uc_ref.shape[1] // 4

    def cols(ref, k):
        return ref[:, k * da:(k + 1) * da].astype(F32)

    _fill_ext(q_ext, cols(up_ref, 1) * cols(up_ref, 2), cols(uc_ref, 1) * cols(uc_ref, 2),
              cols(un_ref, 1) * cols(un_ref, 2), i, nsteps, tm)
    _fill_ext(p_ext, cols(up_ref, 3), cols(uc_ref, 3), cols(un_ref, 3), i, nsteps, tm)
    bg = cols(uc_ref, 0)
    cq = (conv_ref[0:1, :] * q_ext[HALO - 1:HALO - 1 + tm, :] + conv_ref[1:2, :] * q_ext[HALO:HALO + tm, :]
          + conv_ref[2:3, :] * q_ext[HALO + 1:HALO + 1 + tm, :])
    t = i * tm + lax.broadcasted_iota(jnp.int32, (tm, 1), 0)
    gw = da // len(POOL_WINDOWS)
    pooled, ybpre = [], []
    for g, wdw in enumerate(POOL_WINDOWS):
        left = wdw // 2
        right = wdw - 1 - left
        lo, hi = g * gw, (g + 1) * gw
        s = p_ext[HALO - left:HALO - left + tm, lo:hi]
        for o in range(-left + 1, right + 1):
            s = s + p_ext[HALO + o:HALO + o + tm, lo:hi]
        pg = s / _window_count(t, wdw, t_total) - p_ext[HALO:HALO + tm, lo:hi]
        pooled.append(pg.astype(BF16))
        ybpre.append(_dot(pooled[-1], wpool_ref[g]))
    return bg, cq, pooled, jnp.concatenate(ybpre, axis=1)


def _ab_fwd(u, x, vec, conv, wpool, scale, wout, tm, exch=None):
    t_total, d = x.shape
    nu = u.shape[1]
    da = nu // 4
    nsteps = t_total // tm

    def body(up_ref, uc_ref, un_ref, x_ref, vec_ref, conv_ref, wpool_ref, scale_ref, wout_ref,
             y_ref, x2_ref, q_ext, p_ext):
        i = pl.program_id(0)
        bg, cq, _, ybpre = _ab_core(up_ref, uc_ref, un_ref, conv_ref, wpool_ref, q_ext, p_ext,
                                    i, nsteps, tm, t_total)
        cat = jnp.concatenate([bg * cq, ybpre * scale_ref[...]], axis=1).astype(BF16)
        y = _dot(cat, wout_ref[...])
        y_ref[...] = y.astype(BF16)
        x2_ref[...] = x_ref[...] + vec_ref[0:1, :] * y

    return _call(
        body, name="ab_fwd", nsteps=nsteps,
        in_specs=[*_halo_specs(tm, nu, t_total), _rows(tm, d), _full(vec.shape), _full(conv.shape),
                  _full(wpool.shape), _full(scale.shape), _VM],
        out_specs=[_rows(tm, d), _rows(tm, d)],
        out_shape=[jax.ShapeDtypeStruct((t_total, d), BF16), jax.ShapeDtypeStruct((t_total, d), F32)],
        scratch_shapes=[pltpu.VMEM((tm + 2 * HALO, da), F32), pltpu.VMEM((tm + 2 * HALO, da), F32)],
        args=(u, u, u, x, vec, conv, wpool, scale, wout), exch=exch)


def _glu_ext(up_ref, uc_ref, un_ref, z_ext, i, nsteps, tm):
    dz = uc_ref.shape[1] // 2

    def glu(ref):
        return ref[:, 0:dz].astype(F32) * _sigmoid(ref[:, dz:2 * dz].astype(F32))

    _fill_ext(z_ext, glu(up_ref), glu(uc_ref), glu(un_ref), i, nsteps, tm)


def _layer_norm_stats(zc):
    mu = jnp.mean(zc, axis=-1, keepdims=True)
    dlt = zc - mu
    rstd = lax.rsqrt(jnp.mean(dlt * dlt, axis=-1, keepdims=True) + LN_EPS)
    return dlt * rstd, rstd


def _cf_fwd(u, x, vec, wdw, wpw2, tm, exch=None):
    t_total, d = x.shape
    nu = u.shape[1]
    nsteps = t_total // tm
    left = (CONF_KERNEL - 1) // 2

    def body(up_ref, uc_ref, un_ref, x_ref, vec_ref, wdw_ref, wpw2_ref, zc_ref, y_ref, x2_ref, z_ext, sh_ref,
             zc_buf):
        i = pl.program_id(0)
        _glu_ext(up_ref, uc_ref, un_ref, z_ext, i, nsteps, tm)
        for lo in range(0, d, CONV_COLS):
            hi = lo + CONV_COLS
            _fill_shifts(sh_ref, z_ext, lo, hi, tm)
            acc = wdw_ref[0:1, lo:hi] * _shifted(sh_ref, -left, tm)
            for k in range(1, CONF_KERNEL):
                acc = acc + wdw_ref[k:k + 1, lo:hi] * _shifted(sh_ref, k - left, tm)
            zc_buf[:, lo:hi] = acc
        zc = zc_buf[...] + vec_ref[1:2, :]
        zc_ref[...] = zc.astype(BF16)
        zn, _ = _layer_norm_stats(zc)
        zl = zn * vec_ref[2:3, :] + vec_ref[3:4, :]
        zs = zl * _sigmoid(zl)
        y = _dot(zs.astype(BF16), wpw2_ref[...]) + vec_ref[4:5, :]
        y_ref[...] = y.astype(BF16)
        x2_ref[...] = x_ref[...] + vec_ref[0:1, :] * y

    return _call(
        body, name="cf_fwd", nsteps=nsteps,
        in_specs=[*_halo_specs(tm, nu, t_total), _rows(tm, d), _full(vec.shape), _full(wdw.shape), _VM],
        out_specs=[_rows(tm, d), _rows(tm, d), _rows(tm, d)],
        out_shape=[jax.ShapeDtypeStruct((t_total, d), BF16), jax.ShapeDtypeStruct((t_total, d), BF16),
                   jax.ShapeDtypeStruct((t_total, d), F32)],
        scratch_shapes=[pltpu.VMEM((tm + 2 * HALO, d), F32), _shift_scratch(tm), pltpu.VMEM((tm, d), F32)],
        args=(u, u, u, x, vec, wdw, wpw2), exch=exch)


def _ffn_chunks(f, width=FFN_CHUNK):
    return [(lo, min(lo + width, f)) for lo in range(0, f, width)]


def _ffn_fwd(x2, vec, wg, wu, wd, tm, exch=None):
    t_total, d = x2.shape
    f = wg.shape[0]

    def body(x_ref, vec_ref, wg_ref, wu_ref, wd_ref, a_ref, b_ref, f_ref, x3_ref):
        xv = x_ref[...]
        h = _norm_mod(xv, vec_ref[0:1, :], vec_ref[1:2, :], vec_ref[2:3, :]).astype(BF16)
        y = None
        for lo, hi in _ffn_chunks(f):
            a = _dot_nt(h, wg_ref[lo:hi, :])
            b = _dot_nt(h, wu_ref[lo:hi, :])
            a_ref[:, lo:hi] = a.astype(BF16)
            b_ref[:, lo:hi] = b.astype(BF16)
            s = (a * _sigmoid(a) * b).astype(BF16)
            part = _dot(s, wd_ref[lo:hi, :])
            y = part if y is None else y + part
        f_ref[...] = y.astype(BF16)
        x3_ref[...] = xv + vec_ref[3:4, :] * y

    return _call(
        body, name="ffn_fwd", nsteps=t_total // tm,
        in_specs=[_rows(tm, d), _full(vec.shape), _VM, _VM, _VM],
        out_specs=[_rows(tm, f), _rows(tm, f), _rows(tm, d), _rows(tm, d)],
        out_shape=[jax.ShapeDtypeStruct((t_total, f), BF16), jax.ShapeDtypeStruct((t_total, f), BF16),
                   jax.ShapeDtypeStruct((t_total, d), BF16), jax.ShapeDtypeStruct((t_total, d), F32)],
        args=(x2, vec, wg, wu, wd), exch=exch)


def _final_fwd_bwd(x, target, vec, tm):
    t_total, d = x.shape

    def body(x_ref, t_ref, vec_ref, dx_ref, acc_ref):
        @pl.when(pl.program_id(0) == 0)
        def _():
            acc_ref[...] = jnp.zeros_like(acc_ref)

        g = vec_ref[0:1, :]
        xhat, r = _rms(x_ref[...])
        e = xhat * g - t_ref[...]
        acc_ref[1:2, :] += jnp.zeros((1, d), F32) + 0.5 * jnp.sum(jnp.mean(e * e, axis=-1, keepdims=True))
        dout = e * (1.0 / d)
        acc_ref[0:1, :] += _colsum(dout * xhat)
        dxn = dout * g
        dx_ref[...] = r * (dxn - xhat * jnp.mean(dxn * xhat, axis=-1, keepdims=True))

    return _call(
        body, name="final_fwd_bwd", nsteps=t_total // tm,
        in_specs=[_rows(tm, d), _rows(tm, d), _full(vec.shape)],
        out_specs=[_rows(tm, d), _VM],
        out_shape=[jax.ShapeDtypeStruct((t_total, d), F32), jax.ShapeDtypeStruct((8, d), F32)],
        args=(x, target, vec))


def _zero_at_start(*refs):
    @pl.when(pl.program_id(0) == 0)
    def _():
        for ref in refs:
            ref[...] = jnp.zeros_like(ref)


def _ffn_bwd_down(dx3, fout, a, b, vec, wd, tm, exch=None):
    t_total, d = dx3.shape
    f = a.shape[1]

    def body(dx_ref, f_ref, a_ref, b_ref, vec_ref, wd_ref, da_ref, db_ref, dwd_ref, acc_ref):
        _zero_at_start(dwd_ref, acc_ref)
        dx = dx_ref[...]
        acc_ref[0:1, :] += _colsum(dx * f_ref[...].astype(F32))
        dy = (dx * vec_ref[0:1, :]).astype(BF16)
        for lo, hi in _ffn_chunks(f, FFN_CHUNK // 2):
            av = a_ref[:, lo:hi].astype(F32)
            bv = b_ref[:, lo:hi].astype(F32)
            sg = _sigmoid(av)
            silu = av * sg
            ds = _dot_nt(dy, wd_ref[lo:hi, :])
            da_ref[:, lo:hi] = (ds * bv * (sg * (1.0 + av * (1.0 - sg)))).astype(BF16)
            db_ref[:, lo:hi] = (ds * silu).astype(BF16)
            dwd_ref[lo:hi, :] += _dot_tn((silu * bv).astype(BF16), dy)

    return _call(
        body, name="ffn_bwd_down", nsteps=t_total // tm,
        in_specs=[_rows(tm, d), _rows(tm, d), _rows(tm, f), _rows(tm, f), _full(vec.shape), _VM],
        out_specs=[_rows(tm, f), _rows(tm, f), _VM, _VM],
        out_shape=[jax.ShapeDtypeStruct((t_total, f), BF16), jax.ShapeDtypeStruct((t_total, f), BF16),
                   jax.ShapeDtypeStruct(wd.shape, F32), jax.ShapeDtypeStruct((8, d), F32)],
        args=(dx3, fout, a, b, vec, wd), exch=exch)


def _ffn_bwd_up(da, db, x2, dx3, vec, wg, wu, tm, exch=None):
    t_total, d = x2.shape
    f = da.shape[1]

    def body(da_ref, db_ref, x_ref, dx_ref, vec_ref, wg_ref, wu_ref, dx2_ref, dwg_ref, dwu_ref, acc_ref):
        _zero_at_start(dwg_ref, dwu_ref, acc_ref)
        xv = x_ref[...]
        g, sh, sc = vec_ref[0:1, :], vec_ref[1:2, :], vec_ref[2:3, :]
        h = _norm_mod(xv, g, sh, sc).astype(BF16)
        dav = da_ref[...]
        dbv = db_ref[...]
        dwg_ref[...] += _dot_tn(dav, h)
        dwu_ref[...] += _dot_tn(dbv, h)
        dh = _dot(dav, wg_ref[...]) + _dot(dbv, wu_ref[...])
        dxn, dsh, dsc, dg = _norm_mod_bwd(dh, xv, g, sc)
        acc_ref[0:1, :] += dsh
        acc_ref[1:2, :] += dsc
        acc_ref[2:3, :] += dg
        dx2_ref[...] = dx_ref[...] + dxn

    return _call(
        body, name="ffn_bwd_up", nsteps=t_total // tm,
        in_specs=[_rows(tm, f), _rows(tm, f), _rows(tm, d), _rows(tm, d), _full(vec.shape), _VM, _VM],
        out_specs=[_rows(tm, d), _VM, _VM, _VM],
        out_shape=[jax.ShapeDtypeStruct((t_total, d), F32), jax.ShapeDtypeStruct(wg.shape, F32),
                   jax.ShapeDtypeStruct(wu.shape, F32), jax.ShapeDtypeStruct((8, d), F32)],
        args=(da, db, x2, dx3, vec, wg, wu), exch=exch)


def _ab_bwd_out(dx, y, u, vec, conv, wpool, scale, wout, tm, exch=None):
    t_total, d = dx.shape
    nu = u.shape[1]
    da = nu // 4
    gw = da // len(POOL_WINDOWS)
    nsteps = t_total // tm

    def body(dx_ref, y_ref, up_ref, uc_ref, un_ref, vec_ref, conv_ref, wpool_ref, scale_ref, wout_ref,
             dpre_ref, dwout_ref, dwpool_ref, acc_ref, q_ext, p_ext):
        _zero_at_start(dwout_ref, dwpool_ref, acc_ref)
        i = pl.program_id(0)
        dxv = dx_ref[...]
        acc_ref[0:1, :] += _colsum(dxv * y_ref[...].astype(F32))
        dy = (dxv * vec_ref[0:1, :]).astype(BF16)
        bg, cq, pooled, ybpre = _ab_core(up_ref, uc_ref, un_ref, conv_ref, wpool_ref, q_ext, p_ext,
                                         i, nsteps, tm, t_total)
        cat = jnp.concatenate([bg * cq, ybpre * scale_ref[...]], axis=1).astype(BF16)
        dwout_ref[...] += _dot_tn(cat, dy)
        dcat = _dot_nt(dy, wout_ref[...])
        dya = dcat[:, 0:da]
        dyb = dcat[:, da:2 * da]
        acc_ref[1:2, 0:da] += _colsum(dyb * ybpre)
        dybpre = (dyb * scale_ref[...]).astype(BF16)
        dpooled = []
        for g in range(len(POOL_WINDOWS)):
            dg = dybpre[:, g * gw:(g + 1) * gw]
            dwpool_ref[g] += _dot_tn(pooled[g], dg)
            dpooled.append(_dot_nt(dg, wpool_ref[g]))
        dpre_ref[...] = jnp.concatenate([dya * cq, dya * bg] + dpooled, axis=1).astype(BF16)

    return _call(
        body, name="ab_bwd_out", nsteps=nsteps,
        in_specs=[_rows(tm, d), _rows(tm, d), *_halo_specs(tm, nu, t_total), _full(vec.shape),
                  _full(conv.shape), _full(wpool.shape), _full(scale.shape), _VM],
        out_specs=[_rows(tm, 3 * da), _VM, _VM, _VM],
        out_shape=[jax.ShapeDtypeStruct((t_total, 3 * da), BF16), jax.ShapeDtypeStruct(wout.shape, F32),
                   jax.ShapeDtypeStruct(wpool.shape, F32), jax.ShapeDtypeStruct((8, d), F32)],
        scratch_shapes=[pltpu.VMEM((tm + 2 * HALO, da), F32), pltpu.VMEM((tm + 2 * HALO, da), F32)],
        args=(dx, y, u, u, u, vec, conv, wpool, scale, wout), exch=exch)


def _ab_bwd_in(dpre, u, x, dx, vec, conv, win, tm, exch=None):
    t_total, d = x.shape
    nu = u.shape[1]
    da = nu // 4
    gw = da // len(POOL_WINDOWS)
    nsteps = t_total // tm

    def body(dp_ref, dc_ref, dn_ref, up_ref, uc_ref, un_ref, x_ref, dx_ref, vec_ref, conv_ref, win_ref,
             dxin_ref, dwin_ref, dconv_ref, acc_ref, dcq_ext, q_ext, dpl_ext):
        _zero_at_start(dwin_ref, dconv_ref, acc_ref)
        i = pl.program_id(0)

        def ucols(ref, k):
            return ref[:, k * da:(k + 1) * da].astype(F32)

        def dcols(ref, k):
            return ref[:, k * da:(k + 1) * da].astype(F32)

        _fill_ext(dcq_ext, dcols(dp_ref, 1), dcols(dc_ref, 1), dcols(dn_ref, 1), i, nsteps, tm)
        _fill_ext(q_ext, ucols(up_ref, 1) * ucols(up_ref, 2), ucols(uc_ref, 1) * ucols(uc_ref, 2),
                  ucols(un_ref, 1) * ucols(un_ref, 2), i, nsteps, tm)
        _fill_ext(dpl_ext, dcols(dp_ref, 2), dcols(dc_ref, 2), dcols(dn_ref, 2), i, nsteps, tm)
        dq = (conv_ref[0:1, :] * dcq_ext[HALO + 1:HALO + 1 + tm, :] + conv_ref[1:2, :] * dcq_ext[HALO:HALO + tm, :]
              + conv_ref[2:3, :] * dcq_ext[HALO - 1:HALO - 1 + tm, :])
        dcq = dcq_ext[HALO:HALO + tm, :]
        for k in range(3):
            dconv_ref[k:k + 1, :] += _colsum(dcq * q_ext[HALO + k - 1:HALO + k - 1 + tm, :])
        dcg = dq * ucols(uc_ref, 2)
        dv = dq * ucols(uc_ref, 1)
        t_ext = i * tm - HALO + lax.broadcasted_iota(jnp.int32, (tm + 2 * HALO, 1), 0)
        dps = []
        for g, wdw in enumerate(POOL_WINDOWS):
            left = wdw // 2
            right = wdw - 1 - left
            lo, hi = g * gw, (g + 1) * gw
            dpg = dpl_ext[HALO:HALO + tm, lo:hi]
            dpl_ext[:, lo:hi] = dpl_ext[:, lo:hi] / _window_count(t_ext, wdw, t_total)
            s = dpl_ext[HALO - right:HALO - right + tm, lo:hi]
            for o in range(-right + 1, left + 1):
                s = s + dpl_ext[HALO + o:HALO + o + tm, lo:hi]
            dps.append(s - dpg)
        du = jnp.concatenate([dcols(dc_ref, 0), dcg, dv] + dps, axis=1).astype(BF16)
        xv = x_ref[...]
        g, sh, sc = vec_ref[0:1, :], vec_ref[1:2, :], vec_ref[2:3, :]
        h = _norm_mod(xv, g, sh, sc).astype(BF16)
        dh = _in_proj_bwd(h, du, win_ref, dwin_ref)
        dxn, dsh, dsc, dg = _norm_mod_bwd(dh, xv, g, sc)
        acc_ref[0:1, :] += dsh
        acc_ref[1:2, :] += dsc
        acc_ref[2:3, :] += dg
        dxin_ref[...] = dx_ref[...] + dxn

    ext = pltpu.VMEM((tm + 2 * HALO, da), F32)
    return _call(
        body, name="ab_bwd_in", nsteps=nsteps,
        in_specs=[*_halo_specs(tm, 3 * da, t_total), *_halo_specs(tm, nu, t_total), _rows(tm, d), _rows(tm, d),
                  _full(vec.shape), _full(conv.shape), _VM],
        out_specs=[_rows(tm, d), _VM, _VM, _VM],
        out_shape=[jax.ShapeDtypeStruct((t_total, d), F32), jax.ShapeDtypeStruct(win.shape, F32),
                   jax.ShapeDtypeStruct((8, da), F32), jax.ShapeDtypeStruct((8, d), F32)],
        scratch_shapes=[ext, ext, ext],
        args=(dpre, dpre, dpre, u, u, u, x, dx, vec, conv, win), exch=exch)


def _cf_bwd_out(dx, y, zc, vec, wpw2, tm, exch=None):
    t_total, d = dx.shape

    def body(dx_ref, y_ref, zc_ref, vec_ref, w_ref, dzc_ref, dw_ref, acc_ref):
        _zero_at_start(dw_ref, acc_ref)
        dxv = dx_ref[...]
        acc_ref[0:1, :] += _colsum(dxv * y_ref[...].astype(F32))
        dyf = dxv * vec_ref[0:1, :]
        acc_ref[1:2, :] += _colsum(dyf)
        dy = dyf.astype(BF16)
        zn, rstd = _layer_norm_stats(zc_ref[...].astype(F32))
        lng = vec_ref[1:2, :]
        zl = zn * lng + vec_ref[2:3, :]
        sg = _sigmoid(zl)
        dw_ref[...] += _dot_tn((zl * sg).astype(BF16), dy)
        dzl = _dot_nt(dy, w_ref[...]) * (sg * (1.0 + zl * (1.0 - sg)))
        acc_ref[2:3, :] += _colsum(dzl * zn)
        acc_ref[3:4, :] += _colsum(dzl)
        dzn = dzl * lng
        dzc = rstd * (dzn - jnp.mean(dzn, axis=-1, keepdims=True)
                      - zn * jnp.mean(dzn * zn, axis=-1, keepdims=True))
        acc_ref[4:5, :] += _colsum(dzc)
        dzc_ref[...] = dzc.astype(BF16)

    return _call(
        body, name="cf_bwd_out", nsteps=t_total // tm,
        in_specs=[_rows(tm, d), _rows(tm, d), _rows(tm, d), _full(vec.shape), _VM],
        out_specs=[_rows(tm, d), _VM, _VM],
        out_shape=[jax.ShapeDtypeStruct((t_total, d), BF16), jax.ShapeDtypeStruct(wpw2.shape, F32),
                   jax.ShapeDtypeStruct((8, d), F32)],
        args=(dx, y, zc, vec, wpw2), exch=exch)


def _cf_bwd_in(dzc, u, x, dx, vec, wdw, wpw1, tm, exch=None):
    t_total, d = x.shape
    nu = u.shape[1]
    nsteps = t_total // tm
    left = (CONF_KERNEL - 1) // 2

    def body(dp_ref, dc_ref, dn_ref, up_ref, uc_ref, un_ref, x_ref, dx_ref, vec_ref, wdw_ref, w_ref,
             dxin_ref, dw_ref, dwdw_ref, db1_ref, acc_ref, dzc_ext, z_ext, sh_ref, dz_buf):
        _zero_at_start(dw_ref, dwdw_ref, db1_ref, acc_ref)
        i = pl.program_id(0)
        _fill_ext(dzc_ext, dp_ref[...].astype(F32), dc_ref[...].astype(F32), dn_ref[...].astype(F32),
                  i, nsteps, tm)
        _glu_ext(up_ref, uc_ref, un_ref, z_ext, i, nsteps, tm)
        for lo in range(0, d, CONV_COLS):
            hi = lo + CONV_COLS
            _fill_shifts(sh_ref, dzc_ext, lo, hi, tm)
            acc = wdw_ref[0:1, lo:hi] * _shifted(sh_ref, left, tm)
            for k in range(1, CONF_KERNEL):
                acc = acc + wdw_ref[k:k + 1, lo:hi] * _shifted(sh_ref, left - k, tm)
            dz_buf[:, lo:hi] = acc
            dzc = dzc_ext[HALO:HALO + tm, lo:hi]
            _fill_shifts(sh_ref, z_ext, lo, hi, tm)
            for k in range(CONF_KERNEL):
                dwdw_ref[k:k + 1, lo:hi] += _colsum(dzc * _shifted(sh_ref, k - left, tm))
        dz = dz_buf[...]
        av = uc_ref[:, 0:d].astype(F32)
        sg = _sigmoid(uc_ref[:, d:2 * d].astype(F32))
        duf = jnp.concatenate([dz * sg, dz * av * sg * (1.0 - sg)], axis=1)
        db1_ref[0:1, :] += _colsum(duf)
        du = duf.astype(BF16)
        xv = x_ref[...]
        g, sh, sc = vec_ref[0:1, :], vec_ref[1:2, :], vec_ref[2:3, :]
        h = _norm_mod(xv, g, sh, sc).astype(BF16)
        dh = _in_proj_bwd(h, du, w_ref, dw_ref)
        dxn, dsh, dsc, dg = _norm_mod_bwd(dh, xv, g, sc)
        acc_ref[0:1, :] += dsh
        acc_ref[1:2, :] += dsc
        acc_ref[2:3, :] += dg
        dxin_ref[...] = dx_ref[...] + dxn

    ext = pltpu.VMEM((tm + 2 * HALO, d), F32)
    return _call(
        body, name="cf_bwd_in", nsteps=nsteps,
        in_specs=[*_halo_specs(tm, d, t_total), *_halo_specs(tm, nu, t_total), _rows(tm, d), _rows(tm, d),
                  _full(vec.shape), _full(wdw.shape), _VM],
        out_specs=[_rows(tm, d), _VM, _VM, _VM, _VM],
        out_shape=[jax.ShapeDtypeStruct((t_total, d), F32), jax.ShapeDtypeStruct(wpw1.shape, F32),
                   jax.ShapeDtypeStruct((32, d), F32), jax.ShapeDtypeStruct((8, nu), F32),
                   jax.ShapeDtypeStruct((8, d), F32)],
        scratch_shapes=[ext, ext, _shift_scratch(tm), pltpu.VMEM((tm, d), F32)],
        args=(dzc, dzc, dzc, u, u, u, x, dx, vec, wdw, wpw1), exch=exch)


def _mod_fwd(c_all, w_mod, b_cols, exch=None):
    nl, d, ncol = w_mod.shape
    nb = c_all.shape[0]

    def body(c_ref, w_ref, b_ref, o_ref):
        cv = c_ref[...]
        ca = cv * _sigmoid(cv)
        o_ref[0] = jnp.dot(ca, w_ref[0], preferred_element_type=F32, precision=HIGHEST) + b_ref[0]

    (out,), arrived = _call(
        body, name="mod_fwd", nsteps=nl,
        in_specs=[_full(c_all.shape), pl.BlockSpec((1, d, ncol), lambda l: (l, 0, 0)),
                  pl.BlockSpec((1, 1, ncol), lambda l: (l, 0, 0))],
        out_specs=[pl.BlockSpec((1, nb, ncol), lambda l: (l, 0, 0))],
        out_shape=[jax.ShapeDtypeStruct((nl, nb, ncol), F32)],
        args=(c_all, w_mod, b_cols.reshape(nl, 1, ncol)), exch=exch)
    return out, arrived


def _mod_bwd(c_all_t, dmod_cols):
    d, nb = c_all_t.shape
    nl, _, ncol = dmod_cols.shape

    def body(c_ref, dm_ref, o_ref):
        cv = c_ref[...]
        ca = cv * _sigmoid(cv)
        o_ref[0] = jnp.dot(ca, dm_ref[0], preferred_element_type=F32, precision=HIGHEST)

    return _pcall(
        body, name="mod_bwd", grid=(nl,),
        in_specs=[_full(c_all_t.shape), pl.BlockSpec((1, nb, ncol), lambda l: (l, 0, 0))],
        out_specs=pl.BlockSpec((1, d, ncol), lambda l: (l, 0, 0)),
        out_shape=jax.ShapeDtypeStruct((nl, d, ncol), F32),
        compiler_params=_seq_params(),
    )(c_all_t, dmod_cols)


def _row_block(r, c):
    if r * c <= EW_BLOCK_ELEMS:
        return r
    best = None
    for br in range(8, r, 8):
        if r % br == 0 and br * c <= EW_BLOCK_ELEMS:
            best = br
    assert best is not None, (r, c)
    return best


def _as2d(a):
    return a.reshape(-1, a.shape[-1])


def _adamw(w, gparts, m, v):
    shape = w.shape
    w2, m2, v2 = _as2d(w), _as2d(m), _as2d(v)
    g2 = [_as2d(g) for g in gparts]
    r, c = w2.shape
    br = _row_block(r, c)
    ng = len(g2)

    def body(*refs):
        w_ref, m_ref, v_ref = refs[0:3]
        g_refs = refs[3:3 + ng]
        g = g_refs[0][...]
        for gr in g_refs[1:]:
            g = g + gr[...]
        _adamw_update(g, w_ref[...], m_ref[...], v_ref[...], refs[3 + ng:])

    spec = pl.BlockSpec((br, c), lambda i: (i, 0))
    outs = _pcall(
        body, name="adamw", grid=(r // br,),
        in_specs=[spec] * (3 + ng), out_specs=[spec] * 4,
        out_shape=[jax.ShapeDtypeStruct((r, c), F32)] * 4,
        compiler_params=_seq_params(),
    )(w2, m2, v2, *g2)
    return tuple(o.reshape(shape) for o in outs)


def _adamw_update(g, w, m, v, out_refs):
    go_ref, d_ref, mo_ref, vo_ref = out_refs
    mn = ADAM_B1 * m + (1.0 - ADAM_B1) * g
    vn = ADAM_B2 * v + (1.0 - ADAM_B2) * (g * g)
    m_hat = mn / (1.0 - ADAM_B1 ** ADAM_STEP)
    v_hat = vn / (1.0 - ADAM_B2 ** ADAM_STEP)
    go_ref[...] = g.reshape(go_ref.shape)
    d_ref[...] = (-ADAM_LR * (m_hat / (jnp.sqrt(v_hat) + ADAM_EPS) + ADAM_WD * w)).reshape(d_ref.shape)
    mo_ref[...] = mn.reshape(mo_ref.shape)
    vo_ref[...] = vn.reshape(vo_ref.shape)


def _adamw_partials(w, partials, m, v):
    nl, a, b = w.shape
    br = _row_block(a, b)
    nb = a // br

    def body(*refs):
        w_ref, m_ref, v_ref = refs[0:3]
        p_refs = refs[3:3 + nl]
        out_refs = refs[3 + nl:]
        for layer in range(nl):
            @pl.when(pl.program_id(0) == layer)
            def _(layer=layer):
                halves = []
                for core in range(2):
                    acc = p_refs[layer][core, 0].astype(F32)
                    for chip in range(1, N_CHIPS):
                        acc = acc + p_refs[layer][core, chip].astype(F32)
                    halves.append(acc)
                _adamw_update(halves[0] + halves[1], w_ref[...], m_ref[...], v_ref[...], out_refs)

    def part_spec(layer):
        def index(l, i):
            return 0, 0, jnp.where(l == layer, i, jnp.where(l < layer, 0, nb - 1)), 0
        return pl.BlockSpec((2, N_CHIPS, br, b), index)

    spec = pl.BlockSpec((br, b), lambda l, i: (l * nb + i, 0))
    outs = _pcall(
        body, name="adamw_partials", grid=(nl, nb),
        in_specs=[spec] * 3 + [part_spec(layer) for layer in range(nl)], out_specs=[spec] * 4,
        out_shape=[jax.ShapeDtypeStruct((nl * a, b), F32)] * 4,
        compiler_params=pltpu.CompilerParams(dimension_semantics=("arbitrary", "arbitrary"),
                                             vmem_limit_bytes=VMEM_LIMIT),
    )(_as2d(w), _as2d(m), _as2d(v), *partials)
    return tuple(o.reshape(w.shape) for o in outs)


def _allgather8(block, with_sum, exch=None):
    m_per, n = block.shape
    nx = 0 if exch is None else len(exch.arrs)
    nvm = 2 if with_sum else 1

    def body(x_ref, *rest):
        xin, out_ref = rest[:nx], rest[nx]
        sum_ref = rest[nx + 1] if with_sum else None
        xout = rest[nx + nvm:2 * nx + nvm]
        send_sems, recv_sems, local_sem = rest[2 * nx + nvm:2 * nx + nvm + 3]
        xsems = rest[2 * nx + nvm + 3:]
        if exch is not None:
            exch.start(xin, xout, xsems)
        x, y, c = _place()
        me, sibling = (x, y, c), (x, y, 1 - c)
        chips = [(1 - x, y), (x, 1 - y), (1 - x, 1 - y)]

        def rows(px, py, pc):
            return out_ref.at[pl.ds((4 * px + 2 * py + pc) * m_per, m_per), :]

        def copy(k, blk, to, src=None):
            return pltpu.make_async_remote_copy(
                src_ref=rows(*blk) if src is None else src, dst_ref=rows(*blk),
                send_sem=send_sems.at[k], recv_sem=recv_sems.at[k], device_id=to, device_id_type=MESH)

        mine = pltpu.make_async_copy(x_ref, rows(*me), local_sem)
        mine.start()
        first = [copy(0, me, sibling, src=x_ref)]
        first += [copy(1 + j, me, (*chip, c), src=x_ref) for j, chip in enumerate(chips)]
        for cp in first:
            cp.start()
        passed = [copy(4 + j, (*chip, c), sibling) for j, chip in enumerate(chips)]
        for j, chip in enumerate(chips):
            copy(1 + j, (*chip, c), me).wait_recv()
            passed[j].start()
        copy(0, sibling, me).wait_recv()
        for j, chip in enumerate(chips):
            copy(4 + j, (*chip, 1 - c), me).wait_recv()
        for cp in first + passed:
            cp.wait_send()
        mine.wait()
        if exch is not None:
            exch.mid(xin, xout, xsems)
            exch.wait(xin, xout, xsems)
        if with_sum:
            acc = out_ref[0:m_per, :]
            for k in range(1, N_DEV):
                acc = acc + out_ref[k * m_per:(k + 1) * m_per, :]
            sum_ref[...] = acc

    out_shape = [jax.ShapeDtypeStruct((N_DEV * m_per, n), F32)]
    out_specs = [_VM]
    if with_sum:
        out_shape.append(jax.ShapeDtypeStruct((m_per, n), F32))
        out_specs.append(_VM)
    res = _pcall(
        body, name=("allgather8_sum" if with_sum else "allgather8") + ("" if exch is None else "_" + exch.tag),
        in_specs=[_VM] + [_ANY] * nx, out_specs=out_specs + [_ANY] * nx,
        out_shape=out_shape + ([] if exch is None else exch.out_shapes()),
        scratch_shapes=[pltpu.SemaphoreType.DMA((7,)), pltpu.SemaphoreType.DMA((7,)), pltpu.SemaphoreType.DMA]
        + ([] if exch is None else exch.sems()),
        compiler_params=pltpu.CompilerParams(vmem_limit_bytes=VMEM_LIMIT),
    )(block, *([] if exch is None else exch.arrs))
    return list(res[:nvm]), list(res[nvm:])


def _my_cols(full, chip):
    w = full.shape[-1] // N_CHIPS
    return lax.dynamic_slice_in_dim(full, chip * w, w, axis=full.ndim - 1)


def _pad_rows(a, rows):
    return jnp.pad(a, ((0, rows - a.shape[0]), (0, 0)))


def _to_lanes(a):
    flat = a.reshape(-1)
    n = -(-flat.shape[0] // (8 * LANES)) * (8 * LANES)
    return jnp.pad(flat, (0, n - flat.shape[0])).reshape(-1, LANES)


class _Packer:
    def __init__(self):
        self.items = []
        self.rows = 0

    def add(self, name, a):
        lanes = _to_lanes(a)
        self.items.append((name, self.rows, a.shape, lanes))
        self.rows += lanes.shape[0]

    def pack(self):
        total = -(-self.rows // 8) * 8
        return _pad_rows(jnp.concatenate([it[3] for it in self.items], axis=0), total)

    def unpack(self, buf):
        out = {}
        for name, row, shape, lanes in self.items:
            size = 1
            for s in shape:
                size *= s
            out[name] = buf[row:row + lanes.shape[0]].reshape(-1)[:size].reshape(shape)
        return out


TM_SEQ = 512
TM_FFN = 256


LAYER_KEYS = ("in", "out", "gate", "up", "down")
BLOCKED_KEYS = ("in",)
TRANSPOSED = ("ffn_w_gate", "ffn_w_up")


def _layer_big_names(layer):
    i = layer // 2
    mix = (("ab_w_in", i), ("ab_w_out", i)) if layer % 2 == 0 else (("cf_w_pw1", i), ("cf_w_pw2", i))
    return dict(zip(LAYER_KEYS, mix + (("ffn_w_gate", layer), ("ffn_w_up", layer), ("ffn_w_down", layer))))


def _unpack_weight(key, g):
    g = g.reshape(N_CHIPS, -1, g.shape[-1])
    return g if key in BLOCKED_KEYS else g.reshape(-1, g.shape[-1])


def _chunk_grad(key, dw):
    parts = dw if key in BLOCKED_KEYS else dw.reshape(N_CHIPS, -1, dw.shape[-1])
    return parts.astype(BF16)


def _local_step(x, target, mods, p, shards, first):
    t_total, d = x.shape
    depth = mods.shape[0]
    tm = min(TM_SEQ, t_total)
    tmf = min(TM_FFN, t_total)
    saved = []
    xin = x
    weights = [{} for _ in range(depth)]

    def carried(stage, layer):
        if layer == 0:
            return {"in": (0, ("down",)), "mix": (depth, ()), "ffn": (1, ("in", "out", "gate", "up"))}[stage]
        return {"in": (layer, ("down",)), "mix": (layer + 1, ("in", "out")), "ffn": (layer + 1, ("gate", "up"))}[stage]

    def gather(stage, layer):
        of, keys = carried(stage, layer)
        if of >= depth:
            return None
        return _Gather([shards[of][k].reshape(2, -1, shards[of][k].shape[-1]) for k in keys])

    def keep(stage, layer, arrs):
        of, keys = carried(stage, layer)
        for k, g in zip(keys, arrs):
            weights[of][k] = _unpack_weight(k, g)

    for k, g in first.items():
        weights[0][k] = _unpack_weight(k, g)
    for layer in range(depth):
        i = layer // 2
        lw = weights[layer]
        sh1, sc1, g1, sh2, sc2, g2 = (mods[layer, k:k + 1] for k in range(6))
        vec_in = jnp.concatenate([p["norm_mix_g"][layer:layer + 1], sh1, sc1], axis=0)
        bias = None if layer % 2 == 0 else p["cf_b_pw1"][i:i + 1]
        (u,), arrived = _in_proj(xin, vec_in, lw["in"], bias, tm, exch=gather("in", layer))
        keep("in", layer, arrived)
        if layer % 2 == 0:
            (y, x2), arrived = _ab_fwd(u, xin, g1, p["ab_conv"][i], p["ab_w_pool"][i].astype(BF16),
                                       p["ab_pool_scale"][i:i + 1], lw["out"], tm, exch=gather("mix", layer))
            zc = None
        else:
            vec_cf = jnp.concatenate([g1, p["cf_b_dw"][i:i + 1], p["cf_ln_g"][i:i + 1], p["cf_ln_b"][i:i + 1],
                                      p["cf_b_pw2"][i:i + 1]], axis=0)
            (zc, y, x2), arrived = _cf_fwd(u, xin, vec_cf, _pad_rows(p["cf_w_dw"][i], 32), lw["out"], tm,
                                           exch=gather("mix", layer))
        keep("mix", layer, arrived)
        vec_ffn = jnp.concatenate([p["norm_ffn_g"][layer:layer + 1], sh2, sc2, g2], axis=0)
        (a, b, fout, x3), arrived = _ffn_fwd(x2, vec_ffn, lw["gate"], lw["up"], lw["down"], tmf,
                                             exch=gather("ffn", layer))
        keep("ffn", layer, arrived)
        saved.append((xin, u, y, zc, x2, a, b, fout))
        xin = x3

    (dx, fin), _ = _final_fwd_bwd(xin, target, p["final_norm_g"].reshape(1, d), tm)
    grads = {"final_norm_g": fin[0], "loss": fin[1, 0:1]}
    per_layer = {k: [None] * depth for k in ("norm_mix_g", "norm_ffn_g")}
    half = {k: [None] * (depth // 2) for k in (
        "ab_conv", "ab_w_pool", "ab_pool_scale", "cf_b_pw1", "cf_w_dw", "cf_b_dw", "cf_ln_g", "cf_ln_b", "cf_b_pw2")}
    dmods = [None] * depth
    received = {}
    pending = None
    for layer in reversed(range(depth)):
        i = layer // 2
        lw = weights[layer]
        xin, u, y, zc, x2, a, b, fout = saved[layer]
        sh1, sc1, g1, sh2, sc2, g2 = (mods[layer, k:k + 1] for k in range(6))
        above = _Scatter([pending]) if pending is not None else None
        (da, db, dwd, acc_d), arrived = _ffn_bwd_down(dx, fout, a, b, g2, lw["down"], tmf, exch=above)
        if pending is not None:
            received[(layer + 1, "in")] = arrived[0]
        vec_ffn = jnp.concatenate([p["norm_ffn_g"][layer:layer + 1], sh2, sc2], axis=0)
        (dx2, dwg, dwu, acc_u), arrived = _ffn_bwd_up(da, db, x2, dx, vec_ffn, lw["gate"], lw["up"], tmf,
                                                      exch=_Scatter([_chunk_grad("down", dwd)]))
        received[(layer, "down")] = arrived[0]
        per_layer["norm_ffn_g"][layer] = acc_u[2]
        vec_in = jnp.concatenate([p["norm_mix_g"][layer:layer + 1], sh1, sc1], axis=0)
        send_gate = _Scatter([_chunk_grad("gate", dwg)])
        if layer % 2 == 0:
            (dpre, dwout, dwpool, acc_o), arrived = _ab_bwd_out(
                dx2, y, u, g1, p["ab_conv"][i], p["ab_w_pool"][i].astype(BF16), p["ab_pool_scale"][i:i + 1],
                lw["out"], tm, exch=send_gate)
            received[(layer, "gate")] = arrived[0]
            send_up_out = _Scatter([_chunk_grad("up", dwu), _chunk_grad("out", dwout)])
            (dx, dwin, dconv, acc_i), arrived = _ab_bwd_in(dpre, u, xin, dx2, vec_in, p["ab_conv"][i], lw["in"], tm,
                                                           exch=send_up_out)
            half["ab_w_pool"][i] = dwpool
            half["ab_pool_scale"][i] = acc_o[1, 0:d // 2]
            half["ab_conv"][i] = dconv[0:3]
        else:
            vec_cf = jnp.concatenate([g1, p["cf_ln_g"][i:i + 1], p["cf_ln_b"][i:i + 1]], axis=0)
            (dzc, dwout, acc_o), arrived = _cf_bwd_out(dx2, y, zc, vec_cf, lw["out"], tm, exch=send_gate)
            received[(layer, "gate")] = arrived[0]
            send_up_out = _Scatter([_chunk_grad("up", dwu), _chunk_grad("out", dwout)])
            (dx, dwin, dwdw, db1, acc_i), arrived = _cf_bwd_in(
                dzc, u, xin, dx2, vec_in, _pad_rows(p["cf_w_dw"][i], 32), lw["in"], tm, exch=send_up_out)
            half["cf_b_pw2"][i] = acc_o[1]
            half["cf_ln_g"][i] = acc_o[2]
            half["cf_ln_b"][i] = acc_o[3]
            half["cf_b_dw"][i] = acc_o[4]
            half["cf_w_dw"][i] = dwdw[0:CONF_KERNEL]
            half["cf_b_pw1"][i] = db1[0]
        received[(layer, "up")], received[(layer, "out")] = arrived
        per_layer["norm_mix_g"][layer] = acc_i[2]
        dmods[layer] = jnp.stack([acc_i[0], acc_i[1], acc_o[0], acc_u[0], acc_u[1], acc_d[0]], axis=0)
        pending = _chunk_grad("in", dwin)
    for k, v in {**per_layer, **half}.items():
        grads[k] = jnp.stack(v, axis=0)
    return dx, grads, jnp.stack(dmods, axis=0), received, pending


SMALL_COLS = ("ab_conv", "cf_b_pw1", "cf_w_dw", "cf_b_dw", "cf_ln_g", "cf_ln_b", "cf_b_pw2")
SMALL_REPL = ("norm_mix_g", "norm_ffn_g", "ab_w_pool", "ab_pool_scale", "final_norm_g")
WEIGHTS = ("norm_mix_g", "norm_ffn_g", "w_mod", "b_mod", "ab_w_in", "ab_conv", "ab_w_pool", "ab_pool_scale",
           "ab_w_out", "cf_w_pw1", "cf_b_pw1", "cf_w_dw", "cf_b_dw", "cf_ln_g", "cf_ln_b", "cf_w_pw2",
           "cf_b_pw2", "ffn_w_gate", "ffn_w_up", "ffn_w_down", "final_norm_g")


def kernel(x, c, norm_mix_g, norm_ffn_g, w_mod, b_mod, ab_w_in, ab_conv, ab_w_pool, ab_pool_scale, ab_w_out, cf_w_pw1, cf_b_pw1, cf_w_dw, cf_b_dw, cf_ln_g, cf_ln_b, cf_w_pw2, cf_b_pw2, ffn_w_gate, ffn_w_up, ffn_w_down, final_norm_g, loss_target, m_norm_mix_g, m_norm_ffn_g, m_w_mod, m_b_mod, m_ab_w_in, m_ab_conv, m_ab_w_pool, m_ab_pool_scale, m_ab_w_out, m_cf_w_pw1, m_cf_b_pw1, m_cf_w_dw, m_cf_b_dw, m_cf_ln_g, m_cf_ln_b, m_cf_w_pw2, m_cf_b_pw2, m_ffn_w_gate, m_ffn_w_up, m_ffn_w_down, m_final_norm_g, v_norm_mix_g, v_norm_ffn_g, v_w_mod, v_b_mod, v_ab_w_in, v_ab_conv, v_ab_w_pool, v_ab_pool_scale, v_ab_w_out, v_cf_w_pw1, v_cf_b_pw1, v_cf_w_dw, v_cf_b_dw, v_cf_ln_g, v_cf_ln_b, v_cf_w_pw2, v_cf_b_pw2, v_ffn_w_gate, v_ffn_w_up, v_ffn_w_down, v_final_norm_g):
    w = dict(norm_mix_g=norm_mix_g, norm_ffn_g=norm_ffn_g, w_mod=w_mod, b_mod=b_mod, ab_w_in=ab_w_in,
             ab_conv=ab_conv, ab_w_pool=ab_w_pool, ab_pool_scale=ab_pool_scale, ab_w_out=ab_w_out,
             cf_w_pw1=cf_w_pw1, cf_b_pw1=cf_b_pw1, cf_w_dw=cf_w_dw, cf_b_dw=cf_b_dw, cf_ln_g=cf_ln_g,
             cf_ln_b=cf_ln_b, cf_w_pw2=cf_w_pw2, cf_b_pw2=cf_b_pw2, ffn_w_gate=ffn_w_gate, ffn_w_up=ffn_w_up,
             ffn_w_down=ffn_w_down, final_norm_g=final_norm_g)
    mom = dict(norm_mix_g=m_norm_mix_g, norm_ffn_g=m_norm_ffn_g, w_mod=m_w_mod, b_mod=m_b_mod, ab_w_in=m_ab_w_in,
               ab_conv=m_ab_conv, ab_w_pool=m_ab_w_pool, ab_pool_scale=m_ab_pool_scale, ab_w_out=m_ab_w_out,
               cf_w_pw1=m_cf_w_pw1, cf_b_pw1=m_cf_b_pw1, cf_w_dw=m_cf_w_dw, cf_b_dw=m_cf_b_dw, cf_ln_g=m_cf_ln_g,
               cf_ln_b=m_cf_ln_b, cf_w_pw2=m_cf_w_pw2, cf_b_pw2=m_cf_b_pw2, ffn_w_gate=m_ffn_w_gate,
               ffn_w_up=m_ffn_w_up, ffn_w_down=m_ffn_w_down, final_norm_g=m_final_norm_g)
    var = dict(norm_mix_g=v_norm_mix_g, norm_ffn_g=v_norm_ffn_g, w_mod=v_w_mod, b_mod=v_b_mod, ab_w_in=v_ab_w_in,
               ab_conv=v_ab_conv, ab_w_pool=v_ab_w_pool, ab_pool_scale=v_ab_pool_scale, ab_w_out=v_ab_w_out,
               cf_w_pw1=v_cf_w_pw1, cf_b_pw1=v_cf_b_pw1, cf_w_dw=v_cf_w_dw, cf_b_dw=v_cf_b_dw, cf_ln_g=v_cf_ln_g,
               cf_ln_b=v_cf_ln_b, cf_w_pw2=v_cf_w_pw2, cf_b_pw2=v_cf_b_pw2, ffn_w_gate=v_ffn_w_gate,
               ffn_w_up=v_ffn_w_up, ffn_w_down=v_ffn_w_down, final_norm_g=v_final_norm_g)
    px, py, pc = _place()
    chip = 2 * px + py
    dev = 2 * chip + pc
    depth, d, mod_cols = w_mod.shape
    x = x[0]
    target = loss_target[0]

    def rows_major(name, t):
        return jnp.swapaxes(t, 1, 2) if name in TRANSPOSED else t

    shards = [{k: rows_major(name, w[name])[idx].astype(BF16) for k, (name, idx) in _layer_big_names(layer).items()}
              for layer in range(depth)]

    small_in = _Packer()
    small_in.add("c", c)
    for name in SMALL_COLS:
        small_in.add(name, w[name])
    def first_gather(*keys):
        return _Gather([shards[0][k].reshape(2, -1, shards[0][k].shape[-1]) for k in keys])

    first = {}
    (gathered,), (first["in"], first["out"]) = _allgather8(small_in.pack(), with_sum=False,
                                                           exch=first_gather("in", "out"))
    gathered = gathered.reshape(N_DEV, -1, LANES)
    per_dev = [small_in.unpack(gathered[k]) for k in range(N_DEV)]
    c_all = jnp.concatenate([pd["c"] for pd in per_dev], axis=0)
    params = {name: jnp.concatenate([per_dev[2 * k][name] for k in range(N_CHIPS)], axis=-1)
              for name in SMALL_COLS}
    for name in SMALL_REPL:
        params[name] = w[name]

    mod_part, (first["gate"],) = _mod_fwd(c_all, w_mod, _my_cols(b_mod, chip), exch=first_gather("gate"))
    (mod_all,), (first["up"],) = _allgather8(mod_part.reshape(-1, LANES), with_sum=False, exch=first_gather("up"))
    mod_all = mod_all.reshape(N_CHIPS, 2, depth, N_DEV, mod_cols)[:, 0]
    mod_all = jnp.moveaxis(mod_all, 0, 2).reshape(depth, N_DEV, N_CHIPS * mod_cols)
    mods = lax.dynamic_index_in_dim(mod_all, dev, axis=1, keepdims=False).reshape(depth, 6, d)

    grad_x, grads, dmods, received, last_chunk = _local_step(x, target, mods, params, shards, first)

    small_out = _Packer()
    small_out.add("dmods", dmods)
    for name in ("loss",) + SMALL_REPL + SMALL_COLS:
        small_out.add(name, grads[name])
    (parts_all, parts_sum), (received[(0, "in")],) = _allgather8(small_out.pack(), with_sum=True,
                                                                 exch=_Scatter([last_chunk]))
    small_sum = small_out.unpack(parts_sum)
    loss = small_sum["loss"][0]
    dmods_all = jnp.stack([small_out.unpack(pa)["dmods"] for pa in parts_all.reshape(N_DEV, -1, LANES)], axis=1)
    dmods_all = dmods_all.reshape(depth, N_DEV, 6 * d)

    g_final = {}
    g_final["w_mod"] = [_mod_bwd(c_all.T, _my_cols(dmods_all, chip))]
    g_final["b_mod"] = [small_sum["dmods"].reshape(depth, 6 * d)]
    for name in SMALL_REPL:
        g_final[name] = [small_sum[name]]
    for name in SMALL_COLS:
        g_final[name] = [_my_cols(small_sum[name], chip)]

    updates = {}
    for name in WEIGHTS:
        parts = [received[(layer, k)] for layer in range(depth)
                 for k, (other, _) in _layer_big_names(layer).items() if other == name]
        if parts:
            outs = _adamw_partials(rows_major(name, w[name]), parts, rows_major(name, mom[name]),
                                   rows_major(name, var[name]))
            updates[name] = [rows_major(name, o) for o in outs]
        else:
            updates[name] = _adamw(w[name], g_final[name], mom[name], var[name])
    return (loss, grad_x[None], *[updates[name][0] for name in WEIGHTS], *[updates[name][1] for name in WEIGHTS],
            *[updates[name][2] for name in WEIGHTS], *[updates[name][3] for name in WEIGHTS])
```

```python
import functools

import jax
import jax.numpy as jnp
from jax import lax
from jax.experimental import pallas as pl
from jax.experimental.pallas import tpu as pltpu

F32 = jnp.float32
BF16 = jnp.bfloat16
RMS_EPS = 1e-6
LN_EPS = 1e-5
ADAM_LR = 0.001
ADAM_B1 = 0.9
ADAM_B2 = 0.999
ADAM_EPS = 1e-08
ADAM_WD = 0.01
ADAM_STEP = 10
POOL_WINDOWS = (2, 4, 8, 16)
CONF_KERNEL = 31
N_CHIPS = 4
N_DEV = 8
HALO = 16
CONV_COLS = 256
FFN_CHUNK = 1536
LANES = 1024
VMEM_LIMIT = 56 * 1024 * 1024
EW_BLOCK_ELEMS = 256 * 1024
MESH = pl.DeviceIdType.MESH
HIGHEST = lax.Precision.HIGHEST

_pcall = pl.pallas_call


def _dot(a, b):
    return jnp.dot(a, b, preferred_element_type=F32)


def _dot_tn(a, b):
    return lax.dot_general(a, b, (((0,), (0,)), ((), ())), preferred_element_type=F32)


def _dot_nt(a, b):
    return lax.dot_general(a, b, (((1,), (1,)), ((), ())), preferred_element_type=F32)


def _colsum(v):
    return jnp.sum(v, axis=0, keepdims=True)


def _sigmoid(v):
    return 1.0 / (1.0 + jnp.exp(-v))


def _rows(tm, c):
    return pl.BlockSpec((tm, c), lambda i: (i, 0))


def _full(shape):
    nd = len(shape)
    return pl.BlockSpec(shape, lambda i: (0,) * nd)


_VM = pl.BlockSpec(memory_space=pltpu.VMEM)
_ANY = pl.BlockSpec(memory_space=pl.ANY)


def _halo_specs(tm, c, t_total):
    r = tm // HALO
    last = t_total // HALO - 1
    prev = pl.BlockSpec((HALO, c), lambda i: (jnp.maximum(i * r - 1, 0), 0))
    nxt = pl.BlockSpec((HALO, c), lambda i: (jnp.minimum((i + 1) * r, last), 0))
    return prev, _rows(tm, c), nxt


def _seq_params():
    return pltpu.CompilerParams(dimension_semantics=("arbitrary",), vmem_limit_bytes=VMEM_LIMIT)


def _place():
    return lax.axis_index("x"), lax.axis_index("y"), lax.axis_index("c")


def _peer_chips(x, y):
    return [(1 - x, y), (x, 1 - y), (1 - x, 1 - y)]


class _Gather:
    tag = "gather"

    def __init__(self, arrs):
        self.arrs = list(arrs)

    def out_shapes(self):
        return [jax.ShapeDtypeStruct((N_CHIPS,) + a.shape, a.dtype) for a in self.arrs]

    def sems(self):
        n = len(self.arrs)
        return [pltpu.SemaphoreType.DMA((3 * n,)) for _ in range(4)] + [pltpu.SemaphoreType.DMA((n,))]

    def _copies(self, ins, outs, sems, kinds):
        ici_send, ici_recv, d2d_send, d2d_recv, local_sems = sems
        x, y, c = _place()
        me = 2 * x + y
        found = {kind: [] for kind in kinds}
        for j in range(len(ins)):
            if "local" in kinds:
                found["local"].append(pltpu.make_async_copy(ins[j], outs[j].at[me], local_sems.at[j]))
            for k, (px, py) in enumerate(_peer_chips(x, y)):
                ici = dict(send_sem=ici_send.at[3 * j + k], recv_sem=ici_recv.at[3 * j + k],
                           device_id=(px, py, c), device_id_type=MESH)
                d2d = dict(send_sem=d2d_send.at[3 * j + k], recv_sem=d2d_recv.at[3 * j + k],
                           device_id=(x, y, 1 - c), device_id_type=MESH)
                theirs = outs[j].at[2 * px + py]
                if "send" in kinds:
                    found["send"].append(pltpu.make_async_remote_copy(
                        src_ref=ins[j].at[c], dst_ref=outs[j].at[me, c], **ici))
                if "arrival" in kinds:
                    found["arrival"].append(pltpu.make_async_remote_copy(
                        src_ref=ins[j].at[c], dst_ref=theirs.at[c], **ici))
                if "pass" in kinds:
                    found["pass"].append(pltpu.make_async_remote_copy(
                        src_ref=theirs.at[c], dst_ref=theirs.at[c], **d2d))
                if "passed" in kinds:
                    found["passed"].append(pltpu.make_async_remote_copy(
                        src_ref=theirs.at[c], dst_ref=theirs.at[1 - c], **d2d))
        return found

    def start(self, ins, outs, sems):
        found = self._copies(ins, outs, sems, ("local", "send"))
        for cp in found["local"] + found["send"]:
            cp.start()

    def mid(self, ins, outs, sems):
        found = self._copies(ins, outs, sems, ("arrival", "pass"))
        for arrived, onward in zip(found["arrival"], found["pass"]):
            arrived.wait_recv()
            onward.start()

    def wait(self, ins, outs, sems):
        found = self._copies(ins, outs, sems, ("local", "send", "pass", "passed"))
        for cp in found["passed"]:
            cp.wait_recv()
        for cp in found["send"] + found["pass"]:
            cp.wait_send()
        for cp in found["local"]:
            cp.wait()


class _Scatter:
    tag = "scatter"

    def __init__(self, arrs):
        self.arrs = list(arrs)

    def out_shapes(self):
        return [jax.ShapeDtypeStruct((2,) + a.shape, a.dtype) for a in self.arrs]

    def sems(self):
        n = len(self.arrs)
        dma = pltpu.SemaphoreType.DMA
        return [dma((3 * n,)), dma((3 * n,)), dma((4 * n,)), dma((4 * n,)), dma((n,))]

    def _copies(self, ins, outs, sems, kinds):
        ici_send, ici_recv, d2d_send, d2d_recv, local_sems = sems
        x, y, c = _place()
        me = 2 * x + y
        found = {kind: [] for kind in kinds}
        for j in range(len(ins)):
            def d2d(k):
                return dict(send_sem=d2d_send.at[4 * j + k], recv_sem=d2d_recv.at[4 * j + k],
                            device_id=(x, y, 1 - c), device_id_type=MESH)

            if "local" in kinds:
                found["local"].append(pltpu.make_async_copy(ins[j].at[me], outs[j].at[0, me], local_sems.at[j]))
            if "own" in kinds:
                found["own"].append(pltpu.make_async_remote_copy(
                    src_ref=ins[j].at[me], dst_ref=outs[j].at[1, me], **d2d(3)))
            if "passed" in kinds:
                found["passed"].append(pltpu.make_async_remote_copy(
                    src_ref=ins[j].at[me], dst_ref=outs[j].at[1, me], **d2d(3)))
            for k, (px, py) in enumerate(_peer_chips(x, y)):
                ici = dict(send_sem=ici_send.at[3 * j + k], recv_sem=ici_recv.at[3 * j + k],
                           device_id=(px, py, c), device_id_type=MESH)
                peer = 2 * px + py
                if "send" in kinds:
                    found["send"].append(pltpu.make_async_remote_copy(
                        src_ref=ins[j].at[peer], dst_ref=outs[j].at[0, me], **ici))
                if "arrival" in kinds:
                    found["arrival"].append(pltpu.make_async_remote_copy(
                        src_ref=ins[j].at[me], dst_ref=outs[j].at[0, peer], **ici))
                if "pass" in kinds:
                    found["pass"].append(pltpu.make_async_remote_copy(
                        src_ref=outs[j].at[0, peer], dst_ref=outs[j].at[1, peer], **d2d(k)))
                if "passed" in kinds:
                    found["passed"].append(pltpu.make_async_remote_copy(
                        src_ref=outs[j].at[0, peer], dst_ref=outs[j].at[1, peer], **d2d(k)))
        return found

    def start(self, ins, outs, sems):
        found = self._copies(ins, outs, sems, ("local", "own", "send"))
        for cp in found["local"] + found["own"] + found["send"]:
            cp.start()

    def mid(self, ins, outs, sems):
        found = self._copies(ins, outs, sems, ("arrival", "pass"))
        for arrived, onward in zip(found["arrival"], found["pass"]):
            arrived.wait_recv()
            onward.start()

    def wait(self, ins, outs, sems):
        found = self._copies(ins, outs, sems, ("local", "own", "send", "pass", "passed"))
        for cp in found["passed"]:
            cp.wait_recv()
        for cp in found["own"] + found["send"] + found["pass"]:
            cp.wait_send()
        for cp in found["local"]:
            cp.wait()


def _call(body, *, name, nsteps, in_specs, out_specs, out_shape, args, scratch_shapes=(), exch=None):
    if exch is None:
        outs = _pcall(body, name=name, grid=(nsteps,), in_specs=list(in_specs), out_specs=list(out_specs),
                      out_shape=list(out_shape), scratch_shapes=list(scratch_shapes),
                      compiler_params=_seq_params())(*args)
        return list(outs), []
    n, ni, no, ns = len(exch.arrs), len(in_specs), len(out_specs), len(scratch_shapes)

    def hosted(*refs):
        xin = refs[ni:ni + n]
        xout = refs[ni + n + no:ni + 2 * n + no]
        scr = refs[ni + 2 * n + no:]

        @pl.when(pl.program_id(0) == 0)
        def _():
            exch.start(xin, xout, scr[ns:])

        body(*refs[:ni], *refs[ni + n:ni + n + no], *scr[:ns])

        @pl.when(pl.program_id(0) == max(nsteps - 3, 0))
        def _():
            exch.mid(xin, xout, scr[ns:])

        @pl.when(pl.program_id(0) == nsteps - 1)
        def _():
            exch.wait(xin, xout, scr[ns:])

    outs = _pcall(hosted, name=name + "_" + exch.tag, grid=(nsteps,),
                  in_specs=[*in_specs, *[_ANY] * n], out_specs=[*out_specs, *[_ANY] * n],
                  out_shape=[*out_shape, *exch.out_shapes()],
                  scratch_shapes=[*scratch_shapes, *exch.sems()],
                  compiler_params=_seq_params())(*args, *exch.arrs)
    return list(outs[:no]), list(outs[no:])


def _rms(x):
    r = lax.rsqrt(jnp.mean(x * x, axis=-1, keepdims=True) + RMS_EPS)
    return x * r, r


def _norm_mod(x, g, sh, sc):
    xhat, _ = _rms(x)
    return xhat * g * (1.0 + sc) + sh


def _norm_mod_bwd(dh, x, g, sc):
    xhat, r = _rms(x)
    n = xhat * g
    dsh = _colsum(dh)
    dsc = _colsum(dh * n)
    dn = dh * (1.0 + sc)
    dg = _colsum(dn * xhat)
    dxn = dn * g
    dx = r * (dxn - xhat * jnp.mean(dxn * xhat, axis=-1, keepdims=True))
    return dx, dsh, dsc, dg


def _fill_ext(ext_ref, prev, cur, nxt, i, nsteps, tm):
    ext_ref[0:HALO, :] = jnp.where(i > 0, prev, 0.0)
    ext_ref[HALO:HALO + tm, :] = cur
    ext_ref[HALO + tm:HALO + tm + HALO, :] = jnp.where(i < nsteps - 1, nxt, 0.0)


def _shift_scratch(tm):
    return pltpu.VMEM((8, tm + 2 * HALO - 8, CONV_COLS), F32)


def _fill_shifts(sh_ref, ext_ref, lo, hi, tm):
    for b in range(8):
        sh_ref[b] = ext_ref[b:b + tm + 2 * HALO - 8, lo:hi]


def _shifted(sh_ref, offset, tm):
    b = offset % 8
    start = HALO + offset - b
    return sh_ref[b, start:start + tm, :]


def _window_count(t, wdw, t_total):
    left = wdw // 2
    right = wdw - 1 - left
    cnt = jnp.minimum(t + right, t_total - 1) - jnp.maximum(t - left, 0) + 1
    return jnp.maximum(cnt, 1).astype(F32)


def _in_proj(x, vec, w, bias, tm, exch=None):
    t_total, d = x.shape
    nk = w.shape[2]
    n = N_CHIPS * nk
    has_bias = bias is not None

    def body(*refs):
        if has_bias:
            x_ref, vec_ref, w_ref, b_ref, u_ref = refs
        else:
            x_ref, vec_ref, w_ref, u_ref = refs
        h = _norm_mod(x_ref[...], vec_ref[0:1, :], vec_ref[1:2, :], vec_ref[2:3, :])
        h = h.astype(BF16)
        for k in range(N_CHIPS):
            u = _dot(h, w_ref[k])
            if has_bias:
                u = u + b_ref[:, k * nk:(k + 1) * nk]
            u_ref[:, k * nk:(k + 1) * nk] = u.astype(BF16)

    in_specs = [_rows(tm, d), _full(vec.shape), _VM]
    args = [x, vec, w]
    if has_bias:
        in_specs.append(_full(bias.shape))
        args.append(bias)
    return _call(
        body, name="in_proj_bias" if has_bias else "in_proj", nsteps=t_total // tm,
        in_specs=in_specs, out_specs=[_rows(tm, n)], out_shape=[jax.ShapeDtypeStruct((t_total, n), BF16)],
        args=args, exch=exch)


def _in_proj_bwd(h, du, w_ref, dw_ref):
    nk = w_ref.shape[2]
    dh = None
    for k in range(N_CHIPS):
        duk = du[:, k * nk:(k + 1) * nk]
        dw_ref[k] += _dot_tn(h, duk)
        part = _dot_nt(duk, w_ref[k])
        dh = part if dh is None else dh + part
    return dh


def _ab_core(up_ref, uc_ref, un_ref, conv_ref, wpool_ref, q_ext, p_ext, i, nsteps, tm, t_total):
    da = uc_ref.shape[1] // 4

    def cols(ref, k):
        return ref[:, k * da:(k + 1) * da].astype(F32)

    _fill_ext(q_ext, cols(up_ref, 1) * cols(up_ref, 2), cols(uc_ref, 1) * cols(uc_ref, 2),
              cols(un_ref, 1) * cols(un_ref, 2), i, nsteps, tm)
    _fill_ext(p_ext, cols(up_ref, 3), cols(uc_ref, 3), cols(un_ref, 3), i, nsteps, tm)
    bg = cols(uc_ref, 0)
    cq = (conv_ref[0:1, :] * q_ext[HALO - 1:HALO - 1 + tm, :] + conv_ref[1:2, :] * q_ext[HALO:HALO + tm, :]
          + conv_ref[2:3, :] * q_ext[HALO + 1:HALO + 1 + tm, :])
    t = i * tm + lax.broadcasted_iota(jnp.int32, (tm, 1), 0)
    gw = da // len(POOL_WINDOWS)
    pooled, ybpre = [], []
    for g, wdw in enumerate(POOL_WINDOWS):
        left = wdw // 2
        right = wdw - 1 - left
        lo, hi = g * gw, (g + 1) * gw
        s = p_ext[HALO - left:HALO - left + tm, lo:hi]
        for o in range(-left + 1, right + 1):
            s = s + p_ext[HALO + o:HALO + o + tm, lo:hi]
        pg = s / _window_count(t, wdw, t_total) - p_ext[HALO:HALO + tm, lo:hi]
        pooled.append(pg.astype(BF16))
        ybpre.append(_dot(pooled[-1], wpool_ref[g]))
    return bg, cq, pooled, jnp.concatenate(ybpre, axis=1)


def _ab_fwd(u, x, vec, conv, wpool, scale, wout, tm, exch=None):
    t_total, d = x.shape
    nu = u.shape[1]
    da = nu // 4
    nsteps = t_total // tm

    def body(up_ref, uc_ref, un_ref, x_ref, vec_ref, conv_ref, wpool_ref, scale_ref, wout_ref,
             y_ref, x2_ref, q_ext, p_ext):
        i = pl.program_id(0)
        bg, cq, _, ybpre = _ab_core(up_ref, uc_ref, un_ref, conv_ref, wpool_ref, q_ext, p_ext,
                                    i, nsteps, tm, t_total)
        cat = jnp.concatenate([bg * cq, ybpre * scale_ref[...]], axis=1).astype(BF16)
        y = _dot(cat, wout_ref[...])
        y_ref[...] = y.astype(BF16)
        x2_ref[...] = x_ref[...] + vec_ref[0:1, :] * y

    return _call(
        body, name="ab_fwd", nsteps=nsteps,
        in_specs=[*_halo_specs(tm, nu, t_total), _rows(tm, d), _full(vec.shape), _full(conv.shape),
                  _full(wpool.shape), _full(scale.shape), _VM],
        out_specs=[_rows(tm, d), _rows(tm, d)],
        out_shape=[jax.ShapeDtypeStruct((t_total, d), BF16), jax.ShapeDtypeStruct((t_total, d), F32)],
        scratch_shapes=[pltpu.VMEM((tm + 2 * HALO, da), F32), pltpu.VMEM((tm + 2 * HALO, da), F32)],
        args=(u, u, u, x, vec, conv, wpool, scale, wout), exch=exch)


def _glu_ext(up_ref, uc_ref, un_ref, z_ext, i, nsteps, tm):
    dz = uc_ref.shape[1] // 2

    def glu(ref):
        return ref[:, 0:dz].astype(F32) * _sigmoid(ref[:, dz:2 * dz].astype(F32))

    _fill_ext(z_ext, glu(up_ref), glu(uc_ref), glu(un_ref), i, nsteps, tm)


def _layer_norm_stats(zc):
    mu = jnp.mean(zc, axis=-1, keepdims=True)
    dlt = zc - mu
    rstd = lax.rsqrt(jnp.mean(dlt * dlt, axis=-1, keepdims=True) + LN_EPS)
    return dlt * rstd, rstd


def _cf_fwd(u, x, vec, wdw, wpw2, tm, exch=None):
    t_total, d = x.shape
    nu = u.shape[1]
    nsteps = t_total // tm
    left = (CONF_KERNEL - 1) // 2

    def body(up_ref, uc_ref, un_ref, x_ref, vec_ref, wdw_ref, wpw2_ref, zc_ref, y_ref, x2_ref, z_ext, sh_ref,
             zc_buf):
        i = pl.program_id(0)
        _glu_ext(up_ref, uc_ref, un_ref, z_ext, i, nsteps, tm)
        for lo in range(0, d, CONV_COLS):
            hi = lo + CONV_COLS
            _fill_shifts(sh_ref, z_ext, lo, hi, tm)
            acc = wdw_ref[0:1, lo:hi] * _shifted(sh_ref, -left, tm)
            for k in range(1, CONF_KERNEL):
                acc = acc + wdw_ref[k:k + 1, lo:hi] * _shifted(sh_ref, k - left, tm)
            zc_buf[:, lo:hi] = acc
        zc = zc_buf[...] + vec_ref[1:2, :]
        zc_ref[...] = zc.astype(BF16)
        zn, _ = _layer_norm_stats(zc)
        zl = zn * vec_ref[2:3, :] + vec_ref[3:4, :]
        zs = zl * _sigmoid(zl)
        y = _dot(zs.astype(BF16), wpw2_ref[...]) + vec_ref[4:5, :]
        y_ref[...] = y.astype(BF16)
        x2_ref[...] = x_ref[...] + vec_ref[0:1, :] * y

    return _call(
        body, name="cf_fwd", nsteps=nsteps,
        in_specs=[*_halo_specs(tm, nu, t_total), _rows(tm, d), _full(vec.shape), _full(wdw.shape), _VM],
        out_specs=[_rows(tm, d), _rows(tm, d), _rows(tm, d)],
        out_shape=[jax.ShapeDtypeStruct((t_total, d), BF16), jax.ShapeDtypeStruct((t_total, d), BF16),
                   jax.ShapeDtypeStruct((t_total, d), F32)],
        scratch_shapes=[pltpu.VMEM((tm + 2 * HALO, d), F32), _shift_scratch(tm), pltpu.VMEM((tm, d), F32)],
        args=(u, u, u, x, vec, wdw, wpw2), exch=exch)


def _ffn_chunks(f, width=FFN_CHUNK):
    return [(lo, min(lo + width, f)) for lo in range(0, f, width)]


def _ffn_fwd(x2, vec, wg, wu, wd, tm, exch=None):
    t_total, d = x2.shape
    f = wg.shape[0]

    def body(x_ref, vec_ref, wg_ref, wu_ref, wd_ref, a_ref, b_ref, f_ref, x3_ref):
        xv = x_ref[...]
        h = _norm_mod(xv, vec_ref[0:1, :], vec_ref[1:2, :], vec_ref[2:3, :]).astype(BF16)
        y = None
        for lo, hi in _ffn_chunks(f):
            a = _dot_nt(h, wg_ref[lo:hi, :])
            b = _dot_nt(h, wu_ref[lo:hi, :])
            a_ref[:, lo:hi] = a.astype(BF16)
            b_ref[:, lo:hi] = b.astype(BF16)
            s = (a * _sigmoid(a) * b).astype(BF16)
            part = _dot(s, wd_ref[lo:hi, :])
            y = part if y is None else y + part
        f_ref[...] = y.astype(BF16)
        x3_ref[...] = xv + vec_ref[3:4, :] * y

    return _call(
        body, name="ffn_fwd", nsteps=t_total // tm,
        in_specs=[_rows(tm, d), _full(vec.shape), _VM, _VM, _VM],
        out_specs=[_rows(tm, f), _rows(tm, f), _rows(tm, d), _rows(tm, d)],
        out_shape=[jax.ShapeDtypeStruct((t_total, f), BF16), jax.ShapeDtypeStruct((t_total, f), BF16),
                   jax.ShapeDtypeStruct((t_total, d), BF16), jax.ShapeDtypeStruct((t_total, d), F32)],
        args=(x2, vec, wg, wu, wd), exch=exch)


def _final_fwd_bwd(x, target, vec, tm):
    t_total, d = x.shape

    def body(x_ref, t_ref, vec_ref, dx_ref, acc_ref):
        @pl.when(pl.program_id(0) == 0)
        def _():
            acc_ref[...] = jnp.zeros_like(acc_ref)

        g = vec_ref[0:1, :]
        xhat, r = _rms(x_ref[...])
        e = xhat * g - t_ref[...]
        acc_ref[1:2, :] += jnp.zeros((1, d), F32) + 0.5 * jnp.sum(jnp.mean(e * e, axis=-1, keepdims=True))
        dout = e * (1.0 / d)
        acc_ref[0:1, :] += _colsum(dout * xhat)
        dxn = dout * g
        dx_ref[...] = r * (dxn - xhat * jnp.mean(dxn * xhat, axis=-1, keepdims=True))

    return _call(
        body, name="final_fwd_bwd", nsteps=t_total // tm,
        in_specs=[_rows(tm, d), _rows(tm, d), _full(vec.shape)],
        out_specs=[_rows(tm, d), _VM],
        out_shape=[jax.ShapeDtypeStruct((t_total, d), F32), jax.ShapeDtypeStruct((8, d), F32)],
        args=(x, target, vec))


def _zero_at_start(*refs):
    @pl.when(pl.program_id(0) == 0)
    def _():
        for ref in refs:
            ref[...] = jnp.zeros_like(ref)


def _emit_bf16_at_end(nsteps, acc_ref, out_ref):
    @pl.when(pl.program_id(0) == nsteps - 1)
    def _():
        out_ref[...] = acc_ref[...].astype(BF16)


def _ffn_bwd_down(dx3, fout, a, b, vec, wd, tm, exch=None):
    t_total, d = dx3.shape
    f = a.shape[1]

    def body(dx_ref, f_ref, a_ref, b_ref, vec_ref, wd_ref, da_ref, db_ref, dwd_out, acc_ref, dwd_ref):
        _zero_at_start(dwd_ref, acc_ref)
        dx = dx_ref[...]
        acc_ref[0:1, :] += _colsum(dx * f_ref[...].astype(F32))
        dy = (dx * vec_ref[0:1, :]).astype(BF16)
        for lo, hi in _ffn_chunks(f, FFN_CHUNK // 2):
            av = a_ref[:, lo:hi].astype(F32)
            bv = b_ref[:, lo:hi].astype(F32)
            sg = _sigmoid(av)
            silu = av * sg
            ds = _dot_nt(dy, wd_ref[lo:hi, :])
            da_ref[:, lo:hi] = (ds * bv * (sg * (1.0 + av * (1.0 - sg)))).astype(BF16)
            db_ref[:, lo:hi] = (ds * silu).astype(BF16)
            dwd_ref[lo:hi, :] += _dot_tn((silu * bv).astype(BF16), dy)
        _emit_bf16_at_end(t_total // tm, dwd_ref, dwd_out)

    return _call(
        body, name="ffn_bwd_down", nsteps=t_total // tm,
        in_specs=[_rows(tm, d), _rows(tm, d), _rows(tm, f), _rows(tm, f), _full(vec.shape), _VM],
        out_specs=[_rows(tm, f), _rows(tm, f), _VM, _VM],
        out_shape=[jax.ShapeDtypeStruct((t_total, f), BF16), jax.ShapeDtypeStruct((t_total, f), BF16),
                   jax.ShapeDtypeStruct(wd.shape, BF16), jax.ShapeDtypeStruct((8, d), F32)],
        scratch_shapes=[pltpu.VMEM(wd.shape, F32)],
        args=(dx3, fout, a, b, vec, wd), exch=exch)


def _ffn_bwd_up(da, db, x2, dx3, vec, wg, wu, tm, exch=None):
    t_total, d = x2.shape
    f = da.shape[1]

    def body(da_ref, db_ref, x_ref, dx_ref, vec_ref, wg_ref, wu_ref, dx2_ref, dwg_ref, dwu_ref, acc_ref):
        _zero_at_start(dwg_ref, dwu_ref, acc_ref)
        xv = x_ref[...]
        g, sh, sc = vec_ref[0:1, :], vec_ref[1:2, :], vec_ref[2:3, :]
        h = _norm_mod(xv, g, sh, sc).astype(BF16)
        dav = da_ref[...]
        dbv = db_ref[...]
        dwg_ref[...] += _dot_tn(dav, h)
        dwu_ref[...] += _dot_tn(dbv, h)
        dh = _dot(dav, wg_ref[...]) + _dot(dbv, wu_ref[...])
        dxn, dsh, dsc, dg = _norm_mod_bwd(dh, xv, g, sc)
        acc_ref[0:1, :] += dsh
        acc_ref[1:2, :] += dsc
        acc_ref[2:3, :] += dg
        dx2_ref[...] = dx_ref[...] + dxn

    return _call(
        body, name="ffn_bwd_up", nsteps=t_total // tm,
        in_specs=[_rows(tm, f), _rows(tm, f), _rows(tm, d), _rows(tm, d), _full(vec.shape), _VM, _VM],
        out_specs=[_rows(tm, d), _VM, _VM, _VM],
        out_shape=[jax.ShapeDtypeStruct((t_total, d), F32), jax.ShapeDtypeStruct(wg.shape, F32),
                   jax.ShapeDtypeStruct(wu.shape, F32), jax.ShapeDtypeStruct((8, d), F32)],
        args=(da, db, x2, dx3, vec, wg, wu), exch=exch)


def _ab_bwd_out(dx, y, u, vec, conv, wpool, scale, wout, tm, exch=None):
    t_total, d = dx.shape
    nu = u.shape[1]
    da = nu // 4
    gw = da // len(POOL_WINDOWS)
    nsteps = t_total // tm

    def body(dx_ref, y_ref, up_ref, uc_ref, un_ref, vec_ref, conv_ref, wpool_ref, scale_ref, wout_ref,
             dpre_ref, dwout_out, dwpool_ref, acc_ref, q_ext, p_ext, dwout_ref):
        _zero_at_start(dwout_ref, dwpool_ref, acc_ref)
        i = pl.program_id(0)
        dxv = dx_ref[...]
        acc_ref[0:1, :] += _colsum(dxv * y_ref[...].astype(F32))
        dy = (dxv * vec_ref[0:1, :]).astype(BF16)
        bg, cq, pooled, ybpre = _ab_core(up_ref, uc_ref, un_ref, conv_ref, wpool_ref, q_ext, p_ext,
                                         i, nsteps, tm, t_total)
        cat = jnp.concatenate([bg * cq, ybpre * scale_ref[...]], axis=1).astype(BF16)
        dwout_ref[...] += _dot_tn(cat, dy)
        dcat = _dot_nt(dy, wout_ref[...])
        dya = dcat[:, 0:da]
        dyb = dcat[:, da:2 * da]
        acc_ref[1:2, 0:da] += _colsum(dyb * ybpre)
        dybpre = (dyb * scale_ref[...]).astype(BF16)
        dpooled = []
        for g in range(len(POOL_WINDOWS)):
            dg = dybpre[:, g * gw:(g + 1) * gw]
            dwpool_ref[g] += _dot_tn(pooled[g], dg)
            dpooled.append(_dot_nt(dg, wpool_ref[g]))
        dpre_ref[...] = jnp.concatenate([dya * cq, dya * bg] + dpooled, axis=1).astype(BF16)
        _emit_bf16_at_end(nsteps, dwout_ref, dwout_out)

    return _call(
        body, name="ab_bwd_out", nsteps=nsteps,
        in_specs=[_rows(tm, d), _rows(tm, d), *_halo_specs(tm, nu, t_total), _full(vec.shape),
                  _full(conv.shape), _full(wpool.shape), _full(scale.shape), _VM],
        out_specs=[_rows(tm, 3 * da), _VM, _VM, _VM],
        out_shape=[jax.ShapeDtypeStruct((t_total, 3 * da), BF16), jax.ShapeDtypeStruct(wout.shape, BF16),
                   jax.ShapeDtypeStruct(wpool.shape, F32), jax.ShapeDtypeStruct((8, d), F32)],
        scratch_shapes=[pltpu.VMEM((tm + 2 * HALO, da), F32), pltpu.VMEM((tm + 2 * HALO, da), F32),
                        pltpu.VMEM(wout.shape, F32)],
        args=(dx, y, u, u, u, vec, conv, wpool, scale, wout), exch=exch)


def _ab_bwd_in(dpre, u, x, dx, vec, conv, win, tm, exch=None):
    t_total, d = x.shape
    nu = u.shape[1]
    da = nu // 4
    gw = da // len(POOL_WINDOWS)
    nsteps = t_total // tm

    def body(dp_ref, dc_ref, dn_ref, up_ref, uc_ref, un_ref, x_ref, dx_ref, vec_ref, conv_ref, win_ref,
             dxin_ref, dwin_out, dconv_ref, acc_ref, dcq_ext, q_ext, dpl_ext, dwin_ref):
        _zero_at_start(dwin_ref, dconv_ref, acc_ref)
        i = pl.program_id(0)

        def ucols(ref, k):
            return ref[:, k * da:(k + 1) * da].astype(F32)

        def dcols(ref, k):
            return ref[:, k * da:(k + 1) * da].astype(F32)

        _fill_ext(dcq_ext, dcols(dp_ref, 1), dcols(dc_ref, 1), dcols(dn_ref, 1), i, nsteps, tm)
        _fill_ext(q_ext, ucols(up_ref, 1) * ucols(up_ref, 2), ucols(uc_ref, 1) * ucols(uc_ref, 2),
                  ucols(un_ref, 1) * ucols(un_ref, 2), i, nsteps, tm)
        _fill_ext(dpl_ext, dcols(dp_ref, 2), dcols(dc_ref, 2), dcols(dn_ref, 2), i, nsteps, tm)
        dq = (conv_ref[0:1, :] * dcq_ext[HALO + 1:HALO + 1 + tm, :] + conv_ref[1:2, :] * dcq_ext[HALO:HALO + tm, :]
              + conv_ref[2:3, :] * dcq_ext[HALO - 1:HALO - 1 + tm, :])
        dcq = dcq_ext[HALO:HALO + tm, :]
        for k in range(3):
            dconv_ref[k:k + 1, :] += _colsum(dcq * q_ext[HALO + k - 1:HALO + k - 1 + tm, :])
        dcg = dq * ucols(uc_ref, 2)
        dv = dq * ucols(uc_ref, 1)
        t_ext = i * tm - HALO + lax.broadcasted_iota(jnp.int32, (tm + 2 * HALO, 1), 0)
        dps = []
        for g, wdw in enumerate(POOL_WINDOWS):
            left = wdw // 2
            right = wdw - 1 - left
            lo, hi = g * gw, (g + 1) * gw
            dpg = dpl_ext[HALO:HALO + tm, lo:hi]
            dpl_ext[:, lo:hi] = dpl_ext[:, lo:hi] / _window_count(t_ext, wdw, t_total)
            s = dpl_ext[HALO - right:HALO - right + tm, lo:hi]
            for o in range(-right + 1, left + 1):
                s = s + dpl_ext[HALO + o:HALO + o + tm, lo:hi]
            dps.append(s - dpg)
        du = jnp.concatenate([dcols(dc_ref, 0), dcg, dv] + dps, axis=1).astype(BF16)
        xv = x_ref[...]
        g, sh, sc = vec_ref[0:1, :], vec_ref[1:2, :], vec_ref[2:3, :]
        h = _norm_mod(xv, g, sh, sc).astype(BF16)
        dh = _in_proj_bwd(h, du, win_ref, dwin_ref)
        dxn, dsh, dsc, dg = _norm_mod_bwd(dh, xv, g, sc)
        acc_ref[0:1, :] += dsh
        acc_ref[1:2, :] += dsc
        acc_ref[2:3, :] += dg
        dxin_ref[...] = dx_ref[...] + dxn
        _emit_bf16_at_end(nsteps, dwin_ref, dwin_out)

    ext = pltpu.VMEM((tm + 2 * HALO, da), F32)
    return _call(
        body, name="ab_bwd_in", nsteps=nsteps,
        in_specs=[*_halo_specs(tm, 3 * da, t_total), *_halo_specs(tm, nu, t_total), _rows(tm, d), _rows(tm, d),
                  _full(vec.shape), _full(conv.shape), _VM],
        out_specs=[_rows(tm, d), _VM, _VM, _VM],
        out_shape=[jax.ShapeDtypeStruct((t_total, d), F32), jax.ShapeDtypeStruct(win.shape, BF16),
                   jax.ShapeDtypeStruct((8, da), F32), jax.ShapeDtypeStruct((8, d), F32)],
        scratch_shapes=[ext, ext, ext, pltpu.VMEM(win.shape, F32)],
        args=(dpre, dpre, dpre, u, u, u, x, dx, vec, conv, win), exch=exch)


def _cf_bwd_out(dx, y, zc, vec, wpw2, tm, exch=None):
    t_total, d = dx.shape

    def body(dx_ref, y_ref, zc_ref, vec_ref, w_ref, dzc_ref, dw_out, acc_ref, dw_ref):
        _zero_at_start(dw_ref, acc_ref)
        dxv = dx_ref[...]
        acc_ref[0:1, :] += _colsum(dxv * y_ref[...].astype(F32))
        dyf = dxv * vec_ref[0:1, :]
        acc_ref[1:2, :] += _colsum(dyf)
        dy = dyf.astype(BF16)
        zn, rstd = _layer_norm_stats(zc_ref[...].astype(F32))
        lng = vec_ref[1:2, :]
        zl = zn * lng + vec_ref[2:3, :]
        sg = _sigmoid(zl)
        dw_ref[...] += _dot_tn((zl * sg).astype(BF16), dy)
        dzl = _dot_nt(dy, w_ref[...]) * (sg * (1.0 + zl * (1.0 - sg)))
        acc_ref[2:3, :] += _colsum(dzl * zn)
        acc_ref[3:4, :] += _colsum(dzl)
        dzn = dzl * lng
        dzc = rstd * (dzn - jnp.mean(dzn, axis=-1, keepdims=True)
                      - zn * jnp.mean(dzn * zn, axis=-1, keepdims=True))
        acc_ref[4:5, :] += _colsum(dzc)
        dzc_ref[...] = dzc.astype(BF16)
        _emit_bf16_at_end(t_total // tm, dw_ref, dw_out)

    return _call(
        body, name="cf_bwd_out", nsteps=t_total // tm,
        in_specs=[_rows(tm, d), _rows(tm, d), _rows(tm, d), _full(vec.shape), _VM],
        out_specs=[_rows(tm, d), _VM, _VM],
        out_shape=[jax.ShapeDtypeStruct((t_total, d), BF16), jax.ShapeDtypeStruct(wpw2.shape, BF16),
                   jax.ShapeDtypeStruct((8, d), F32)],
        scratch_shapes=[pltpu.VMEM(wpw2.shape, F32)],
        args=(dx, y, zc, vec, wpw2), exch=exch)


def _cf_bwd_in(dzc, u, x, dx, vec, wdw, wpw1, tm, exch=None):
    t_total, d = x.shape
    nu = u.shape[1]
    nsteps = t_total // tm
    left = (CONF_KERNEL - 1) // 2

    def body(dp_ref, dc_ref, dn_ref, up_ref, uc_ref, un_ref, x_ref, dx_ref, vec_ref, wdw_ref, w_ref,
             dxin_ref, dw_out, dwdw_ref, db1_ref, acc_ref, dzc_ext, z_ext, sh_ref, dz_buf, dw_ref):
        _zero_at_start(dw_ref, dwdw_ref, db1_ref, acc_ref)
        i = pl.program_id(0)
        _fill_ext(dzc_ext, dp_ref[...].astype(F32), dc_ref[...].astype(F32), dn_ref[...].astype(F32),
                  i, nsteps, tm)
        _glu_ext(up_ref, uc_ref, un_ref, z_ext, i, nsteps, tm)
        for lo in range(0, d, CONV_COLS):
            hi = lo + CONV_COLS
            _fill_shifts(sh_ref, dzc_ext, lo, hi, tm)
            acc = wdw_ref[0:1, lo:hi] * _shifted(sh_ref, left, tm)
            for k in range(1, CONF_KERNEL):
                acc = acc + wdw_ref[k:k + 1, lo:hi] * _shifted(sh_ref, left - k, tm)
            dz_buf[:, lo:hi] = acc
            dzc = dzc_ext[HALO:HALO + tm, lo:hi]
            _fill_shifts(sh_ref, z_ext, lo, hi, tm)
            for k in range(CONF_KERNEL):
                dwdw_ref[k:k + 1, lo:hi] += _colsum(dzc * _shifted(sh_ref, k - left, tm))
        dz = dz_buf[...]
        av = uc_ref[:, 0:d].astype(F32)
        sg = _sigmoid(uc_ref[:, d:2 * d].astype(F32))
        duf = jnp.concatenate([dz * sg, dz * av * sg * (1.0 - sg)], axis=1)
        db1_ref[0:1, :] += _colsum(duf)
        du = duf.astype(BF16)
        xv = x_ref[...]
        g, sh, sc = vec_ref[0:1, :], vec_ref[1:2, :], vec_ref[2:3, :]
        h = _norm_mod(xv, g, sh, sc).astype(BF16)
        dh = _in_proj_bwd(h, du, w_ref, dw_ref)
        dxn, dsh, dsc, dg = _norm_mod_bwd(dh, xv, g, sc)
        acc_ref[0:1, :] += dsh
        acc_ref[1:2, :] += dsc
        acc_ref[2:3, :] += dg
        dxin_ref[...] = dx_ref[...] + dxn
        _emit_bf16_at_end(nsteps, dw_ref, dw_out)

    ext = pltpu.VMEM((tm + 2 * HALO, d), F32)
    return _call(
        body, name="cf_bwd_in", nsteps=nsteps,
        in_specs=[*_halo_specs(tm, d, t_total), *_halo_specs(tm, nu, t_total), _rows(tm, d), _rows(tm, d),
                  _full(vec.shape), _full(wdw.shape), _VM],
        out_specs=[_rows(tm, d), _VM, _VM, _VM, _VM],
        out_shape=[jax.ShapeDtypeStruct((t_total, d), F32), jax.ShapeDtypeStruct(wpw1.shape, BF16),
                   jax.ShapeDtypeStruct((32, d), F32), jax.ShapeDtypeStruct((8, nu), F32),
                   jax.ShapeDtypeStruct((8, d), F32)],
        scratch_shapes=[ext, ext, _shift_scratch(tm), pltpu.VMEM((tm, d), F32), pltpu.VMEM(wpw1.shape, F32)],
        args=(dzc, dzc, dzc, u, u, u, x, dx, vec, wdw, wpw1), exch=exch)


def _mod_fwd(c_all, w_mod, b_cols):
    nl, d, ncol = w_mod.shape
    nb = c_all.shape[0]

    def body(c_ref, w_ref, b_ref, o_ref):
        cv = c_ref[...]
        ca = cv * _sigmoid(cv)
        o_ref[0] = jnp.dot(ca, w_ref[0], preferred_element_type=F32, precision=HIGHEST) + b_ref[0]

    return _pcall(
        body, name="mod_fwd", grid=(nl,),
        in_specs=[_full(c_all.shape), pl.BlockSpec((1, d, ncol), lambda l: (l, 0, 0)),
                  pl.BlockSpec((1, 1, ncol), lambda l: (l, 0, 0))],
        out_specs=pl.BlockSpec((1, nb, ncol), lambda l: (l, 0, 0)),
        out_shape=jax.ShapeDtypeStruct((nl, nb, ncol), F32),
        compiler_params=_seq_params(),
    )(c_all, w_mod, b_cols.reshape(nl, 1, ncol))


def _mod_bwd(c_all_t, dmod_cols):
    d, nb = c_all_t.shape
    nl, _, ncol = dmod_cols.shape

    def body(c_ref, dm_ref, o_ref):
        cv = c_ref[...]
        ca = cv * _sigmoid(cv)
        o_ref[0] = jnp.dot(ca, dm_ref[0], preferred_element_type=F32, precision=HIGHEST)

    return _pcall(
        body, name="mod_bwd", grid=(nl,),
        in_specs=[_full(c_all_t.shape), pl.BlockSpec((1, nb, ncol), lambda l: (l, 0, 0))],
        out_specs=pl.BlockSpec((1, d, ncol), lambda l: (l, 0, 0)),
        out_shape=jax.ShapeDtypeStruct((nl, d, ncol), F32),
        compiler_params=_seq_params(),
    )(c_all_t, dmod_cols)


def _row_block(r, c):
    if r * c <= EW_BLOCK_ELEMS:
        return r
    best = None
    for br in range(8, r, 8):
        if r % br == 0 and br * c <= EW_BLOCK_ELEMS:
            best = br
    assert best is not None, (r, c)
    return best


def _as2d(a):
    return a.reshape(-1, a.shape[-1])


def _adamw(w, gparts, m, v):
    shape = w.shape
    w2, m2, v2 = _as2d(w), _as2d(m), _as2d(v)
    g2 = [_as2d(g) for g in gparts]
    r, c = w2.shape
    br = _row_block(r, c)
    ng = len(g2)

    def body(*refs):
        w_ref, m_ref, v_ref = refs[0:3]
        g_refs = refs[3:3 + ng]
        g = g_refs[0][...]
        for gr in g_refs[1:]:
            g = g + gr[...]
        _adamw_update(g, w_ref[...], m_ref[...], v_ref[...], refs[3 + ng:])

    spec = pl.BlockSpec((br, c), lambda i: (i, 0))
    outs = _pcall(
        body, name="adamw", grid=(r // br,),
        in_specs=[spec] * (3 + ng), out_specs=[spec] * 4,
        out_shape=[jax.ShapeDtypeStruct((r, c), F32)] * 4,
        compiler_params=_seq_params(),
    )(w2, m2, v2, *g2)
    return tuple(o.reshape(shape) for o in outs)


def _adamw_update(g, w, m, v, out_refs):
    go_ref, d_ref, mo_ref, vo_ref = out_refs
    mn = ADAM_B1 * m + (1.0 - ADAM_B1) * g
    vn = ADAM_B2 * v + (1.0 - ADAM_B2) * (g * g)
    m_hat = mn / (1.0 - ADAM_B1 ** ADAM_STEP)
    v_hat = vn / (1.0 - ADAM_B2 ** ADAM_STEP)
    go_ref[...] = g.reshape(go_ref.shape)
    d_ref[...] = (-ADAM_LR * (m_hat / (jnp.sqrt(v_hat) + ADAM_EPS) + ADAM_WD * w)).reshape(d_ref.shape)
    mo_ref[...] = mn.reshape(mo_ref.shape)
    vo_ref[...] = vn.reshape(vo_ref.shape)


def _adamw_partials(w, partials, m, v):
    nl, a, b = w.shape
    br = _row_block(a, b)
    nb = a // br

    def body(*refs):
        w_ref, m_ref, v_ref = refs[0:3]
        p_refs = refs[3:3 + nl]
        out_refs = refs[3 + nl:]
        for layer in range(nl):
            @pl.when(pl.program_id(0) == layer)
            def _(layer=layer):
                halves = []
                for core in range(2):
                    acc = p_refs[layer][core, 0].astype(F32)
                    for chip in range(1, N_CHIPS):
                        acc = acc + p_refs[layer][core, chip].astype(F32)
                    halves.append(acc)
                _adamw_update(halves[0] + halves[1], w_ref[...], m_ref[...], v_ref[...], out_refs)

    def part_spec(layer):
        def index(l, i):
            return 0, 0, jnp.where(l == layer, i, jnp.where(l < layer, 0, nb - 1)), 0
        return pl.BlockSpec((2, N_CHIPS, br, b), index)

    spec = pl.BlockSpec((br, b), lambda l, i: (l * nb + i, 0))
    outs = _pcall(
        body, name="adamw_partials", grid=(nl, nb),
        in_specs=[spec] * 3 + [part_spec(layer) for layer in range(nl)], out_specs=[spec] * 4,
        out_shape=[jax.ShapeDtypeStruct((nl * a, b), F32)] * 4,
        compiler_params=pltpu.CompilerParams(dimension_semantics=("arbitrary", "arbitrary"),
                                             vmem_limit_bytes=VMEM_LIMIT),
    )(_as2d(w), _as2d(m), _as2d(v), *partials)
    return tuple(o.reshape(w.shape) for o in outs)


def _allgather8(block, with_sum, exch=None):
    m_per, n = block.shape
    nx = 0 if exch is None else len(exch.arrs)
    nvm = 2 if with_sum else 1

    def body(x_ref, *rest):
        xin, out_ref = rest[:nx], rest[nx]
        sum_ref = rest[nx + 1] if with_sum else None
        xout = rest[nx + nvm:2 * nx + nvm]
        send_sems, recv_sems, local_sem = rest[2 * nx + nvm:2 * nx + nvm + 3]
        xsems = rest[2 * nx + nvm + 3:]
        if exch is not None:
            exch.start(xin, xout, xsems)
        x, y, c = _place()
        me, sibling = (x, y, c), (x, y, 1 - c)
        chips = [(1 - x, y), (x, 1 - y), (1 - x, 1 - y)]

        def rows(px, py, pc):
            return out_ref.at[pl.ds((4 * px + 2 * py + pc) * m_per, m_per), :]

        def copy(k, blk, to, src=None):
            return pltpu.make_async_remote_copy(
                src_ref=rows(*blk) if src is None else src, dst_ref=rows(*blk),
                send_sem=send_sems.at[k], recv_sem=recv_sems.at[k], device_id=to, device_id_type=MESH)

        mine = pltpu.make_async_copy(x_ref, rows(*me), local_sem)
        mine.start()
        first = [copy(0, me, sibling, src=x_ref)]
        first += [copy(1 + j, me, (*chip, c), src=x_ref) for j, chip in enumerate(chips)]
        for cp in first:
            cp.start()
        passed = [copy(4 + j, (*chip, c), sibling) for j, chip in enumerate(chips)]
        for j, chip in enumerate(chips):
            copy(1 + j, (*chip, c), me).wait_recv()
            passed[j].start()
        copy(0, sibling, me).wait_recv()
        for j, chip in enumerate(chips):
            copy(4 + j, (*chip, 1 - c), me).wait_recv()
        for cp in first + passed:
            cp.wait_send()
        mine.wait()
        if exch is not None:
            exch.mid(xin, xout, xsems)
            exch.wait(xin, xout, xsems)
        if with_sum:
            acc = out_ref[0:m_per, :]
            for k in range(1, N_DEV):
                acc = acc + out_ref[k * m_per:(k + 1) * m_per, :]
            sum_ref[...] = acc

    out_shape = [jax.ShapeDtypeStruct((N_DEV * m_per, n), F32)]
    out_specs = [_VM]
    if with_sum:
        out_shape.append(jax.ShapeDtypeStruct((m_per, n), F32))
        out_specs.append(_VM)
    res = _pcall(
        body, name=("allgather8_sum" if with_sum else "allgather8") + ("" if exch is None else "_" + exch.tag),
        in_specs=[_VM] + [_ANY] * nx, out_specs=out_specs + [_ANY] * nx,
        out_shape=out_shape + ([] if exch is None else exch.out_shapes()),
        scratch_shapes=[pltpu.SemaphoreType.DMA((7,)), pltpu.SemaphoreType.DMA((7,)), pltpu.SemaphoreType.DMA]
        + ([] if exch is None else exch.sems()),
        compiler_params=pltpu.CompilerParams(vmem_limit_bytes=VMEM_LIMIT),
    )(block, *([] if exch is None else exch.arrs))
    return list(res[:nvm]), list(res[nvm:])


def _my_cols(full, chip):
    w = full.shape[-1] // N_CHIPS
    return lax.dynamic_slice_in_dim(full, chip * w, w, axis=full.ndim - 1)


def _pad_rows(a, rows):
    return jnp.pad(a, ((0, rows - a.shape[0]), (0, 0)))


def _to_lanes(a):
    flat = a.reshape(-1)
    n = -(-flat.shape[0] // (8 * LANES)) * (8 * LANES)
    return jnp.pad(flat, (0, n - flat.shape[0])).reshape(-1, LANES)


class _Packer:
    def __init__(self):
        self.items = []
        self.rows = 0

    def add(self, name, a):
        lanes = _to_lanes(a)
        self.items.append((name, self.rows, a.shape, lanes))
        self.rows += lanes.shape[0]

    def pack(self):
        total = -(-self.rows // 8) * 8
        return _pad_rows(jnp.concatenate([it[3] for it in self.items], axis=0), total)

    def unpack(self, buf):
        out = {}
        for name, row, shape, lanes in self.items:
            size = 1
            for s in shape:
                size *= s
            out[name] = buf[row:row + lanes.shape[0]].reshape(-1)[:size].reshape(shape)
        return out


TM_SEQ = 512
TM_FFN = 256


LAYER_KEYS = ("in", "out", "gate", "up", "down")
BLOCKED_KEYS = ("in",)
TRANSPOSED = ("ffn_w_gate", "ffn_w_up")


def _layer_big_names(layer):
    i = layer // 2
    mix = (("ab_w_in", i), ("ab_w_out", i)) if layer % 2 == 0 else (("cf_w_pw1", i), ("cf_w_pw2", i))
    return dict(zip(LAYER_KEYS, mix + (("ffn_w_gate", layer), ("ffn_w_up", layer), ("ffn_w_down", layer))))


def _unpack_weight(key, g):
    g = g.reshape(N_CHIPS, -1, g.shape[-1])
    return g if key in BLOCKED_KEYS else g.reshape(-1, g.shape[-1])


def _chunk_grad(key, dw):
    parts = dw if key in BLOCKED_KEYS else dw.reshape(N_CHIPS, -1, dw.shape[-1])
    return parts.astype(BF16)


def _local_step(x, target, mods, p, shards, first):
    t_total, d = x.shape
    depth = mods.shape[0]
    tm = min(TM_SEQ, t_total)
    tmf = min(TM_FFN, t_total)
    saved = []
    xin = x
    weights = [{} for _ in range(depth)]

    def carried(stage, layer):
        if layer == 0:
            return {"in": (0, ("out", "gate")), "mix": (0, ("up", "down")), "ffn": (1, ("in", "out", "gate", "up"))}[stage]
        return {"in": (layer, ("down",)), "mix": (layer + 1, ("in", "out")), "ffn": (layer + 1, ("gate", "up"))}[stage]

    def gather(stage, layer):
        of, keys = carried(stage, layer)
        if of >= depth:
            return None
        return _Gather([shards[of][k].reshape(2, -1, shards[of][k].shape[-1]) for k in keys])

    def keep(stage, layer, arrs):
        of, keys = carried(stage, layer)
        for k, g in zip(keys, arrs):
            weights[of][k] = _unpack_weight(k, g)

    for k, g in first.items():
        weights[0][k] = _unpack_weight(k, g)
    for layer in range(depth):
        i = layer // 2
        lw = weights[layer]
        sh1, sc1, g1, sh2, sc2, g2 = (mods[layer, k:k + 1] for k in range(6))
        vec_in = jnp.concatenate([p["norm_mix_g"][layer:layer + 1], sh1, sc1], axis=0)
        bias = None if layer % 2 == 0 else p["cf_b_pw1"][i:i + 1]
        (u,), arrived = _in_proj(xin, vec_in, lw["in"], bias, tm, exch=gather("in", layer))
        keep("in", layer, arrived)
        if layer % 2 == 0:
            (y, x2), arrived = _ab_fwd(u, xin, g1, p["ab_conv"][i], p["ab_w_pool"][i].astype(BF16),
                                       p["ab_pool_scale"][i:i + 1], lw["out"], tm, exch=gather("mix", layer))
            zc = None
        else:
            vec_cf = jnp.concatenate([g1, p["cf_b_dw"][i:i + 1], p["cf_ln_g"][i:i + 1], p["cf_ln_b"][i:i + 1],
                                      p["cf_b_pw2"][i:i + 1]], axis=0)
            (zc, y, x2), arrived = _cf_fwd(u, xin, vec_cf, _pad_rows(p["cf_w_dw"][i], 32), lw["out"], tm,
                                           exch=gather("mix", layer))
        keep("mix", layer, arrived)
        vec_ffn = jnp.concatenate([p["norm_ffn_g"][layer:layer + 1], sh2, sc2, g2], axis=0)
        (a, b, fout, x3), arrived = _ffn_fwd(x2, vec_ffn, lw["gate"], lw["up"], lw["down"], tmf,
                                             exch=gather("ffn", layer))
        keep("ffn", layer, arrived)
        saved.append((xin, u, y, zc, x2, a, b, fout))
        xin = x3

    (dx, fin), _ = _final_fwd_bwd(xin, target, p["final_norm_g"].reshape(1, d), tm)
    grads = {"final_norm_g": fin[0], "loss": fin[1, 0:1]}
    per_layer = {k: [None] * depth for k in ("norm_mix_g", "norm_ffn_g")}
    half = {k: [None] * (depth // 2) for k in (
        "ab_conv", "ab_w_pool", "ab_pool_scale", "cf_b_pw1", "cf_w_dw", "cf_b_dw", "cf_ln_g", "cf_ln_b", "cf_b_pw2")}
    dmods = [None] * depth
    received = {}
    pending = None
    for layer in reversed(range(depth)):
        i = layer // 2
        lw = weights[layer]
        xin, u, y, zc, x2, a, b, fout = saved[layer]
        sh1, sc1, g1, sh2, sc2, g2 = (mods[layer, k:k + 1] for k in range(6))
        above = _Scatter([pending]) if pending is not None else None
        (da, db, dwd, acc_d), arrived = _ffn_bwd_down(dx, fout, a, b, g2, lw["down"], tmf, exch=above)
        if pending is not None:
            received[(layer + 1, "in")] = arrived[0]
        vec_ffn = jnp.concatenate([p["norm_ffn_g"][layer:layer + 1], sh2, sc2], axis=0)
        (dx2, dwg, dwu, acc_u), arrived = _ffn_bwd_up(da, db, x2, dx, vec_ffn, lw["gate"], lw["up"], tmf,
                                                      exch=_Scatter([_chunk_grad("down", dwd)]))
        received[(layer, "down")] = arrived[0]
        per_layer["norm_ffn_g"][layer] = acc_u[2]
        vec_in = jnp.concatenate([p["norm_mix_g"][layer:layer + 1], sh1, sc1], axis=0)
        send_gate = _Scatter([_chunk_grad("gate", dwg)])
        if layer % 2 == 0:
            (dpre, dwout, dwpool, acc_o), arrived = _ab_bwd_out(
                dx2, y, u, g1, p["ab_conv"][i], p["ab_w_pool"][i].astype(BF16), p["ab_pool_scale"][i:i + 1],
                lw["out"], tm, exch=send_gate)
            received[(layer, "gate")] = arrived[0]
            send_up_out = _Scatter([_chunk_grad("up", dwu), _chunk_grad("out", dwout)])
            (dx, dwin, dconv, acc_i), arrived = _ab_bwd_in(dpre, u, xin, dx2, vec_in, p["ab_conv"][i], lw["in"], tm,
                                                           exch=send_up_out)
            half["ab_w_pool"][i] = dwpool
            half["ab_pool_scale"][i] = acc_o[1, 0:d // 2]
            half["ab_conv"][i] = dconv[0:3]
        else:
            vec_cf = jnp.concatenate([g1, p["cf_ln_g"][i:i + 1], p["cf_ln_b"][i:i + 1]], axis=0)
            (dzc, dwout, acc_o), arrived = _cf_bwd_out(dx2, y, zc, vec_cf, lw["out"], tm, exch=send_gate)
            received[(layer, "gate")] = arrived[0]
            send_up_out = _Scatter([_chunk_grad("up", dwu), _chunk_grad("out", dwout)])
            (dx, dwin, dwdw, db1, acc_i), arrived = _cf_bwd_in(
                dzc, u, xin, dx2, vec_in, _pad_rows(p["cf_w_dw"][i], 32), lw["in"], tm, exch=send_up_out)
            half["cf_b_pw2"][i] = acc_o[1]
            half["cf_ln_g"][i] = acc_o[2]
            half["cf_ln_b"][i] = acc_o[3]
            half["cf_b_dw"][i] = acc_o[4]
            half["cf_w_dw"][i] = dwdw[0:CONF_KERNEL]
            half["cf_b_pw1"][i] = db1[0]
        received[(layer, "up")], received[(layer, "out")] = arrived
        per_layer["norm_mix_g"][layer] = acc_i[2]
        dmods[layer] = jnp.stack([acc_i[0], acc_i[1], acc_o[0], acc_u[0], acc_u[1], acc_d[0]], axis=0)
        pending = _chunk_grad("in", dwin)
    for k, v in {**per_layer, **half}.items():
        grads[k] = jnp.stack(v, axis=0)
    return dx, grads, jnp.stack(dmods, axis=0), received, pending


SMALL_COLS = ("ab_conv", "cf_b_pw1", "cf_w_dw", "cf_b_dw", "cf_ln_g", "cf_ln_b", "cf_b_pw2")
SMALL_REPL = ("norm_mix_g", "norm_ffn_g", "ab_w_pool", "ab_pool_scale", "final_norm_g")
WEIGHTS = ("norm_mix_g", "norm_ffn_g", "w_mod", "b_mod", "ab_w_in", "ab_conv", "ab_w_pool", "ab_pool_scale",
           "ab_w_out", "cf_w_pw1", "cf_b_pw1", "cf_w_dw", "cf_b_dw", "cf_ln_g", "cf_ln_b", "cf_w_pw2",
           "cf_b_pw2", "ffn_w_gate", "ffn_w_up", "ffn_w_down", "final_norm_g")


def kernel(x, c, norm_mix_g, norm_ffn_g, w_mod, b_mod, ab_w_in, ab_conv, ab_w_pool, ab_pool_scale, ab_w_out, cf_w_pw1, cf_b_pw1, cf_w_dw, cf_b_dw, cf_ln_g, cf_ln_b, cf_w_pw2, cf_b_pw2, ffn_w_gate, ffn_w_up, ffn_w_down, final_norm_g, loss_target, m_norm_mix_g, m_norm_ffn_g, m_w_mod, m_b_mod, m_ab_w_in, m_ab_conv, m_ab_w_pool, m_ab_pool_scale, m_ab_w_out, m_cf_w_pw1, m_cf_b_pw1, m_cf_w_dw, m_cf_b_dw, m_cf_ln_g, m_cf_ln_b, m_cf_w_pw2, m_cf_b_pw2, m_ffn_w_gate, m_ffn_w_up, m_ffn_w_down, m_final_norm_g, v_norm_mix_g, v_norm_ffn_g, v_w_mod, v_b_mod, v_ab_w_in, v_ab_conv, v_ab_w_pool, v_ab_pool_scale, v_ab_w_out, v_cf_w_pw1, v_cf_b_pw1, v_cf_w_dw, v_cf_b_dw, v_cf_ln_g, v_cf_ln_b, v_cf_w_pw2, v_cf_b_pw2, v_ffn_w_gate, v_ffn_w_up, v_ffn_w_down, v_final_norm_g):
    w = dict(norm_mix_g=norm_mix_g, norm_ffn_g=norm_ffn_g, w_mod=w_mod, b_mod=b_mod, ab_w_in=ab_w_in,
             ab_conv=ab_conv, ab_w_pool=ab_w_pool, ab_pool_scale=ab_pool_scale, ab_w_out=ab_w_out,
             cf_w_pw1=cf_w_pw1, cf_b_pw1=cf_b_pw1, cf_w_dw=cf_w_dw, cf_b_dw=cf_b_dw, cf_ln_g=cf_ln_g,
             cf_ln_b=cf_ln_b, cf_w_pw2=cf_w_pw2, cf_b_pw2=cf_b_pw2, ffn_w_gate=ffn_w_gate, ffn_w_up=ffn_w_up,
             ffn_w_down=ffn_w_down, final_norm_g=final_norm_g)
    mom = dict(norm_mix_g=m_norm_mix_g, norm_ffn_g=m_norm_ffn_g, w_mod=m_w_mod, b_mod=m_b_mod, ab_w_in=m_ab_w_in,
               ab_conv=m_ab_conv, ab_w_pool=m_ab_w_pool, ab_pool_scale=m_ab_pool_scale, ab_w_out=m_ab_w_out,
               cf_w_pw1=m_cf_w_pw1, cf_b_pw1=m_cf_b_pw1, cf_w_dw=m_cf_w_dw, cf_b_dw=m_cf_b_dw, cf_ln_g=m_cf_ln_g,
               cf_ln_b=m_cf_ln_b, cf_w_pw2=m_cf_w_pw2, cf_b_pw2=m_cf_b_pw2, ffn_w_gate=m_ffn_w_gate,
               ffn_w_up=m_ffn_w_up, ffn_w_down=m_ffn_w_down, final_norm_g=m_final_norm_g)
    var = dict(norm_mix_g=v_norm_mix_g, norm_ffn_g=v_norm_ffn_g, w_mod=v_w_mod, b_mod=v_b_mod, ab_w_in=v_ab_w_in,
               ab_conv=v_ab_conv, ab_w_pool=v_ab_w_pool, ab_pool_scale=v_ab_pool_scale, ab_w_out=v_ab_w_out,
               cf_w_pw1=v_cf_w_pw1, cf_b_pw1=v_cf_b_pw1, cf_w_dw=v_cf_w_dw, cf_b_dw=v_cf_b_dw, cf_ln_g=v_cf_ln_g,
               cf_ln_b=v_cf_ln_b, cf_w_pw2=v_cf_w_pw2, cf_b_pw2=v_cf_b_pw2, ffn_w_gate=v_ffn_w_gate,
               ffn_w_up=v_ffn_w_up, ffn_w_down=v_ffn_w_down, final_norm_g=v_final_norm_g)
    px, py, pc = _place()
    chip = 2 * px + py
    dev = 2 * chip + pc
    depth, d, mod_cols = w_mod.shape
    x = x[0]
    target = loss_target[0]

    def rows_major(name, t):
        return jnp.swapaxes(t, 1, 2) if name in TRANSPOSED else t

    shards = [{k: rows_major(name, w[name])[idx].astype(BF16) for k, (name, idx) in _layer_big_names(layer).items()}
              for layer in range(depth)]

    small_in = _Packer()
    small_in.add("c", c)
    for name in SMALL_COLS:
        small_in.add(name, w[name])
    def first_gather(*keys):
        return _Gather([shards[0][k].reshape(2, -1, shards[0][k].shape[-1]) for k in keys])

    first = {}
    (gathered,), (first["in"],) = _allgather8(small_in.pack(), with_sum=False, exch=first_gather("in"))
    gathered = gathered.reshape(N_DEV, -1, LANES)
    per_dev = [small_in.unpack(gathered[k]) for k in range(N_DEV)]
    c_all = jnp.concatenate([pd["c"] for pd in per_dev], axis=0)
    params = {name: jnp.concatenate([per_dev[2 * k][name] for k in range(N_CHIPS)], axis=-1)
              for name in SMALL_COLS}
    for name in SMALL_REPL:
        params[name] = w[name]

    mod_part = _mod_fwd(c_all, w_mod, _my_cols(b_mod, chip))
    (mod_all,), _ = _allgather8(mod_part.reshape(-1, LANES), with_sum=False)
    mod_all = mod_all.reshape(N_CHIPS, 2, depth, N_DEV, mod_cols)[:, 0]
    mod_all = jnp.moveaxis(mod_all, 0, 2).reshape(depth, N_DEV, N_CHIPS * mod_cols)
    mods = lax.dynamic_index_in_dim(mod_all, dev, axis=1, keepdims=False).reshape(depth, 6, d)

    grad_x, grads, dmods, received, last_chunk = _local_step(x, target, mods, params, shards, first)

    small_out = _Packer()
    small_out.add("dmods", dmods)
    for name in ("loss",) + SMALL_REPL + SMALL_COLS:
        small_out.add(name, grads[name])
    (parts_all, parts_sum), (received[(0, "in")],) = _allgather8(small_out.pack(), with_sum=True,
                                                                 exch=_Scatter([last_chunk]))
    small_sum = small_out.unpack(parts_sum)
    loss = small_sum["loss"][0]
    dmods_all = jnp.stack([small_out.unpack(pa)["dmods"] for pa in parts_all.reshape(N_DEV, -1, LANES)], axis=1)
    dmods_all = dmods_all.reshape(depth, N_DEV, 6 * d)

    g_final = {}
    g_final["w_mod"] = [_mod_bwd(c_all.T, _my_cols(dmods_all, chip))]
    g_final["b_mod"] = [small_sum["dmods"].reshape(depth, 6 * d)]
    for name in SMALL_REPL:
        g_final[name] = [small_sum[name]]
    for name in SMALL_COLS:
        g_final[name] = [_my_cols(small_sum[name], chip)]

    updates = {}
    for name in WEIGHTS:
        parts = [received[(layer, k)] for layer in range(depth)
                 for k, (other, _) in _layer_big_names(layer).items() if other == name]
        if parts:
            outs = _adamw_partials(rows_major(name, w[name]), parts, rows_major(name, mom[name]),
                                   rows_major(name, var[name]))
            updates[name] = [rows_major(name, o) for o in outs]
        else:
            updates[name] = _adamw(w[name], g_final[name], mom[name], var[name])
    return (loss, grad_x[None], *[updates[name][0] for name in WEIGHTS], *[updates[name][1] for name in WEIGHTS],
            *[updates[name][2] for name in WEIGHTS], *[updates[name][3] for name in WEIGHTS])
```

```python
import functools

import jax
import jax.numpy as jnp
from jax import lax
from jax.experimental import pallas as pl
from jax.experimental.pallas import tpu as pltpu

F32 = jnp.float32
BF16 = jnp.bfloat16
RMS_EPS = 1e-6
LN_EPS = 1e-5
ADAM_LR = 0.001
ADAM_B1 = 0.9
ADAM_B2 = 0.999
ADAM_EPS = 1e-08
ADAM_WD = 0.01
ADAM_STEP = 10
POOL_WINDOWS = (2, 4, 8, 16)
CONF_KERNEL = 31
N_CHIPS = 4
N_DEV = 8
HALO = 16
CONV_COLS = 256
FFN_CHUNK = 1536
LANES = 1024
VMEM_LIMIT = 56 * 1024 * 1024
VMEM_LIMIT_WIDE = 60 * 1024 * 1024
EW_BLOCK_ELEMS = 256 * 1024
MESH = pl.DeviceIdType.MESH
HIGHEST = lax.Precision.HIGHEST

_pcall = pl.pallas_call


def _dot(a, b):
    return jnp.dot(a, b, preferred_element_type=F32)


def _dot_tn(a, b):
    return lax.dot_general(a, b, (((0,), (0,)), ((), ())), preferred_element_type=F32)


def _dot_nt(a, b):
    return lax.dot_general(a, b, (((1,), (1,)), ((), ())), preferred_element_type=F32)


def _colsum(v):
    return jnp.sum(v, axis=0, keepdims=True)


def _sigmoid(v):
    return 1.0 / (1.0 + jnp.exp(-v))


def _rows(tm, c):
    return pl.BlockSpec((tm, c), lambda i: (i, 0))


def _full(shape):
    nd = len(shape)
    return pl.BlockSpec(shape, lambda i: (0,) * nd)


_VM = pl.BlockSpec(memory_space=pltpu.VMEM)
_ANY = pl.BlockSpec(memory_space=pl.ANY)


def _halo_specs(tm, c, t_total):
    r = tm // HALO
    last = t_total // HALO - 1
    prev = pl.BlockSpec((HALO, c), lambda i: (jnp.maximum(i * r - 1, 0), 0))
    nxt = pl.BlockSpec((HALO, c), lambda i: (jnp.minimum((i + 1) * r, last), 0))
    return prev, _rows(tm, c), nxt


def _seq_params(vmem_limit=VMEM_LIMIT):
    return pltpu.CompilerParams(dimension_semantics=("arbitrary",), vmem_limit_bytes=vmem_limit)


def _place():
    return lax.axis_index("x"), lax.axis_index("y"), lax.axis_index("c")


def _peer_chips(x, y):
    return [(1 - x, y), (x, 1 - y), (1 - x, 1 - y)]


class _Gather:
    tag = "gather"

    def __init__(self, arrs):
        self.arrs = list(arrs)

    def out_shapes(self):
        return [jax.ShapeDtypeStruct((N_CHIPS,) + a.shape, a.dtype) for a in self.arrs]

    def sems(self):
        n = len(self.arrs)
        return [pltpu.SemaphoreType.DMA((3 * n,)) for _ in range(4)] + [pltpu.SemaphoreType.DMA((n,))]

    def _copies(self, ins, outs, sems, kinds):
        ici_send, ici_recv, d2d_send, d2d_recv, local_sems = sems
        x, y, c = _place()
        me = 2 * x + y
        found = {kind: [] for kind in kinds}
        for j in range(len(ins)):
            if "local" in kinds:
                found["local"].append(pltpu.make_async_copy(ins[j], outs[j].at[me], local_sems.at[j]))
            for k, (px, py) in enumerate(_peer_chips(x, y)):
                ici = dict(send_sem=ici_send.at[3 * j + k], recv_sem=ici_recv.at[3 * j + k],
                           device_id=(px, py, c), device_id_type=MESH)
                d2d = dict(send_sem=d2d_send.at[3 * j + k], recv_sem=d2d_recv.at[3 * j + k],
                           device_id=(x, y, 1 - c), device_id_type=MESH)
                theirs = outs[j].at[2 * px + py]
                if "send" in kinds:
                    found["send"].append(pltpu.make_async_remote_copy(
                        src_ref=ins[j].at[c], dst_ref=outs[j].at[me, c], **ici))
                if "arrival" in kinds:
                    found["arrival"].append(pltpu.make_async_remote_copy(
                        src_ref=ins[j].at[c], dst_ref=theirs.at[c], **ici))
                if "pass" in kinds:
                    found["pass"].append(pltpu.make_async_remote_copy(
                        src_ref=theirs.at[c], dst_ref=theirs.at[c], **d2d))
                if "passed" in kinds:
                    found["passed"].append(pltpu.make_async_remote_copy(
                        src_ref=theirs.at[c], dst_ref=theirs.at[1 - c], **d2d))
        return found

    def start(self, ins, outs, sems):
        found = self._copies(ins, outs, sems, ("local", "send"))
        for cp in found["local"] + found["send"]:
            cp.start()

    def mid(self, ins, outs, sems):
        found = self._copies(ins, outs, sems, ("arrival", "pass"))
        for arrived, onward in zip(found["arrival"], found["pass"]):
            arrived.wait_recv()
            onward.start()

    def wait(self, ins, outs, sems):
        found = self._copies(ins, outs, sems, ("local", "send", "pass", "passed"))
        for cp in found["passed"]:
            cp.wait_recv()
        for cp in found["send"] + found["pass"]:
            cp.wait_send()
        for cp in found["local"]:
            cp.wait()


class _Scatter:
    tag = "scatter"

    def __init__(self, arrs):
        self.arrs = list(arrs)

    def out_shapes(self):
        return [jax.ShapeDtypeStruct((2,) + a.shape, a.dtype) for a in self.arrs]

    def sems(self):
        n = len(self.arrs)
        dma = pltpu.SemaphoreType.DMA
        return [dma((3 * n,)), dma((3 * n,)), dma((4 * n,)), dma((4 * n,)), dma((n,))]

    def _copies(self, ins, outs, sems, kinds):
        ici_send, ici_recv, d2d_send, d2d_recv, local_sems = sems
        x, y, c = _place()
        me = 2 * x + y
        found = {kind: [] for kind in kinds}
        for j in range(len(ins)):
            def d2d(k):
                return dict(send_sem=d2d_send.at[4 * j + k], recv_sem=d2d_recv.at[4 * j + k],
                            device_id=(x, y, 1 - c), device_id_type=MESH)

            if "local" in kinds:
                found["local"].append(pltpu.make_async_copy(ins[j].at[me], outs[j].at[0, me], local_sems.at[j]))
            if "own" in kinds:
                found["own"].append(pltpu.make_async_remote_copy(
                    src_ref=ins[j].at[me], dst_ref=outs[j].at[1, me], **d2d(3)))
            if "passed" in kinds:
                found["passed"].append(pltpu.make_async_remote_copy(
                    src_ref=ins[j].at[me], dst_ref=outs[j].at[1, me], **d2d(3)))
            for k, (px, py) in enumerate(_peer_chips(x, y)):
                ici = dict(send_sem=ici_send.at[3 * j + k], recv_sem=ici_recv.at[3 * j + k],
                           device_id=(px, py, c), device_id_type=MESH)
                peer = 2 * px + py
                if "send" in kinds:
                    found["send"].append(pltpu.make_async_remote_copy(
                        src_ref=ins[j].at[peer], dst_ref=outs[j].at[0, me], **ici))
                if "arrival" in kinds:
                    found["arrival"].append(pltpu.make_async_remote_copy(
                        src_ref=ins[j].at[me], dst_ref=outs[j].at[0, peer], **ici))
                if "pass" in kinds:
                    found["pass"].append(pltpu.make_async_remote_copy(
                        src_ref=outs[j].at[0, peer], dst_ref=outs[j].at[1, peer], **d2d(k)))
                if "passed" in kinds:
                    found["passed"].append(pltpu.make_async_remote_copy(
                        src_ref=outs[j].at[0, peer], dst_ref=outs[j].at[1, peer], **d2d(k)))
        return found

    def start(self, ins, outs, sems):
        found = self._copies(ins, outs, sems, ("local", "own", "send"))
        for cp in found["local"] + found["own"] + found["send"]:
            cp.start()

    def mid(self, ins, outs, sems):
        found = self._copies(ins, outs, sems, ("arrival", "pass"))
        for arrived, onward in zip(found["arrival"], found["pass"]):
            arrived.wait_recv()
            onward.start()

    def wait(self, ins, outs, sems):
        found = self._copies(ins, outs, sems, ("local", "own", "send", "pass", "passed"))
        for cp in found["passed"]:
            cp.wait_recv()
        for cp in found["own"] + found["send"] + found["pass"]:
            cp.wait_send()
        for cp in found["local"]:
            cp.wait()


def _call(body, *, name, nsteps, in_specs, out_specs, out_shape, args, scratch_shapes=(), exch=None,
          vmem_limit=VMEM_LIMIT):
    if exch is None:
        outs = _pcall(body, name=name, grid=(nsteps,), in_specs=list(in_specs), out_specs=list(out_specs),
                      out_shape=list(out_shape), scratch_shapes=list(scratch_shapes),
                      compiler_params=_seq_params(vmem_limit))(*args)
        return list(outs), []
    n, ni, no, ns = len(exch.arrs), len(in_specs), len(out_specs), len(scratch_shapes)

    def hosted(*refs):
        xin = refs[ni:ni + n]
        xout = refs[ni + n + no:ni + 2 * n + no]
        scr = refs[ni + 2 * n + no:]

        @pl.when(pl.program_id(0) == 0)
        def _():
            exch.start(xin, xout, scr[ns:])

        body(*refs[:ni], *refs[ni + n:ni + n + no], *scr[:ns])

        @pl.when(pl.program_id(0) == max(nsteps - 3, 0))
        def _():
            exch.mid(xin, xout, scr[ns:])

        @pl.when(pl.program_id(0) == nsteps - 1)
        def _():
            exch.wait(xin, xout, scr[ns:])

    outs = _pcall(hosted, name=name + "_" + exch.tag, grid=(nsteps,),
                  in_specs=[*in_specs, *[_ANY] * n], out_specs=[*out_specs, *[_ANY] * n],
                  out_shape=[*out_shape, *exch.out_shapes()],
                  scratch_shapes=[*scratch_shapes, *exch.sems()],
                  compiler_params=_seq_params(vmem_limit))(*args, *exch.arrs)
    return list(outs[:no]), list(outs[no:])


def _rms(x):
    r = lax.rsqrt(jnp.mean(x * x, axis=-1, keepdims=True) + RMS_EPS)
    return x * r, r


def _norm_mod(x, g, sh, sc):
    xhat, _ = _rms(x)
    return xhat * g * (1.0 + sc) + sh


def _norm_mod_bwd(dh, x, g, sc):
    xhat, r = _rms(x)
    n = xhat * g
    dsh = _colsum(dh)
    dsc = _colsum(dh * n)
    dn = dh * (1.0 + sc)
    dg = _colsum(dn * xhat)
    dxn = dn * g
    dx = r * (dxn - xhat * jnp.mean(dxn * xhat, axis=-1, keepdims=True))
    return dx, dsh, dsc, dg


def _fill_ext(ext_ref, prev, cur, nxt, i, nsteps, tm):
    ext_ref[0:HALO, :] = jnp.where(i > 0, prev, 0.0)
    ext_ref[HALO:HALO + tm, :] = cur
    ext_ref[HALO + tm:HALO + tm + HALO, :] = jnp.where(i < nsteps - 1, nxt, 0.0)


def _shift_scratch(tm):
    return pltpu.VMEM((8, tm + 2 * HALO - 8, CONV_COLS), F32)


def _fill_shifts(sh_ref, ext_ref, lo, hi, tm):
    for b in range(8):
        sh_ref[b] = ext_ref[b:b + tm + 2 * HALO - 8, lo:hi]


def _shifted(sh_ref, offset, tm):
    b = offset % 8
    start = HALO + offset - b
    return sh_ref[b, start:start + tm, :]


def _window_count(t, wdw, t_total):
    left = wdw // 2
    right = wdw - 1 - left
    cnt = jnp.minimum(t + right, t_total - 1) - jnp.maximum(t - left, 0) + 1
    return jnp.maximum(cnt, 1).astype(F32)


def _in_proj(x, vec, w, bias, tm, exch=None):
    t_total, d = x.shape
    nk = w.shape[2]
    n = N_CHIPS * nk
    has_bias = bias is not None

    def body(*refs):
        if has_bias:
            x_ref, vec_ref, w_ref, b_ref, u_ref = refs
        else:
            x_ref, vec_ref, w_ref, u_ref = refs
        h = _norm_mod(x_ref[...], vec_ref[0:1, :], vec_ref[1:2, :], vec_ref[2:3, :])
        h = h.astype(BF16)
        for k in range(N_CHIPS):
            u = _dot(h, w_ref[k])
            if has_bias:
                u = u + b_ref[:, k * nk:(k + 1) * nk]
            u_ref[:, k * nk:(k + 1) * nk] = u.astype(BF16)

    in_specs = [_rows(tm, d), _full(vec.shape), _VM]
    args = [x, vec, w]
    if has_bias:
        in_specs.append(_full(bias.shape))
        args.append(bias)
    return _call(
        body, name="in_proj_bias" if has_bias else "in_proj", nsteps=t_total // tm,
        in_specs=in_specs, out_specs=[_rows(tm, n)], out_shape=[jax.ShapeDtypeStruct((t_total, n), BF16)],
        args=args, exch=exch)


def _in_proj_bwd(h, du, w_ref, dw_ref):
    nk = w_ref.shape[2]
    dh = None
    for k in range(N_CHIPS):
        duk = du[:, k * nk:(k + 1) * nk]
        dw_ref[k] += _dot_tn(h, duk)
        part = _dot_nt(duk, w_ref[k])
        dh = part if dh is None else dh + part
    return dh


def _ab_core(up_ref, uc_ref, un_ref, conv_ref, wpool_ref, q_ext, p_ext, i, nsteps, tm, t_total):
    da = uc_ref.shape[1] // 4

    def cols(ref, k):
        return ref[:, k * da:(k + 1) * da].astype(F32)

    _fill_ext(q_ext, cols(up_ref, 1) * cols(up_ref, 2), cols(uc_ref, 1) * cols(uc_ref, 2),
              cols(un_ref, 1) * cols(un_ref, 2), i, nsteps, tm)
    _fill_ext(p_ext, cols(up_ref, 3), cols(uc_ref, 3), cols(un_ref, 3), i, nsteps, tm)
    bg = cols(uc_ref, 0)
    cq = (conv_ref[0:1, :] * q_ext[HALO - 1:HALO - 1 + tm, :] + conv_ref[1:2, :] * q_ext[HALO:HALO + tm, :]
          + conv_ref[2:3, :] * q_ext[HALO + 1:HALO + 1 + tm, :])
    t = i * tm + lax.broadcasted_iota(jnp.int32, (tm, 1), 0)
    gw = da // len(POOL_WINDOWS)
    pooled, ybpre = [], []
    for g, wdw in enumerate(POOL_WINDOWS):
        left = wdw // 2
        right = wdw - 1 - left
        lo, hi = g * gw, (g + 1) * gw
        s = p_ext[HALO - left:HALO - left + tm, lo:hi]
        for o in range(-left + 1, right + 1):
            s = s + p_ext[HALO + o:HALO + o + tm, lo:hi]
        pg = s / _window_count(t, wdw, t_total) - p_ext[HALO:HALO + tm, lo:hi]
        pooled.append(pg.astype(BF16))
        ybpre.append(_dot(pooled[-1], wpool_ref[g]))
    return bg, cq, pooled, jnp.concatenate(ybpre, axis=1)


def _ab_fwd(u, x, vec, conv, wpool, scale, wout, tm, exch=None):
    t_total, d = x.shape
    nu = u.shape[1]
    da = nu // 4
    nsteps = t_total // tm

    def body(up_ref, uc_ref, un_ref, x_ref, vec_ref, conv_ref, wpool_ref, scale_ref, wout_ref,
             y_ref, x2_ref, q_ext, p_ext):
        i = pl.program_id(0)
        bg, cq, _, ybpre = _ab_core(up_ref, uc_ref, un_ref, conv_ref, wpool_ref, q_ext, p_ext,
                                    i, nsteps, tm, t_total)
        cat = jnp.concatenate([bg * cq, ybpre * scale_ref[...]], axis=1).astype(BF16)
        y = _dot(cat, wout_ref[...])
        y_ref[...] = y.astype(BF16)
        x2_ref[...] = x_ref[...] + vec_ref[0:1, :] * y

    return _call(
        body, name="ab_fwd", nsteps=nsteps,
        in_specs=[*_halo_specs(tm, nu, t_total), _rows(tm, d), _full(vec.shape), _full(conv.shape),
                  _full(wpool.shape), _full(scale.shape), _VM],
        out_specs=[_rows(tm, d), _rows(tm, d)],
        out_shape=[jax.ShapeDtypeStruct((t_total, d), BF16), jax.ShapeDtypeStruct((t_total, d), F32)],
        scratch_shapes=[pltpu.VMEM((tm + 2 * HALO, da), F32), pltpu.VMEM((tm + 2 * HALO, da), F32)],
        args=(u, u, u, x, vec, conv, wpool, scale, wout), exch=exch)


def _glu_ext(up_ref, uc_ref, un_ref, z_ext, i, nsteps, tm):
    dz = uc_ref.shape[1] // 2

    def glu(ref):
        return ref[:, 0:dz].astype(F32) * _sigmoid(ref[:, dz:2 * dz].astype(F32))

    _fill_ext(z_ext, glu(up_ref), glu(uc_ref), glu(un_ref), i, nsteps, tm)


def _layer_norm_stats(zc):
    mu = jnp.mean(zc, axis=-1, keepdims=True)
    dlt = zc - mu
    rstd = lax.rsqrt(jnp.mean(dlt * dlt, axis=-1, keepdims=True) + LN_EPS)
    return dlt * rstd, rstd


def _cf_fwd(u, x, vec, wdw, wpw2, tm, exch=None):
    t_total, d = x.shape
    nu = u.shape[1]
    nsteps = t_total // tm
    left = (CONF_KERNEL - 1) // 2

    def body(up_ref, uc_ref, un_ref, x_ref, vec_ref, wdw_ref, wpw2_ref, zc_ref, y_ref, x2_ref, z_ext, sh_ref,
             zc_buf):
        i = pl.program_id(0)
        _glu_ext(up_ref, uc_ref, un_ref, z_ext, i, nsteps, tm)
        for lo in range(0, d, CONV_COLS):
            hi = lo + CONV_COLS
            _fill_shifts(sh_ref, z_ext, lo, hi, tm)
            acc = wdw_ref[0:1, lo:hi] * _shifted(sh_ref, -left, tm)
            for k in range(1, CONF_KERNEL):
                acc = acc + wdw_ref[k:k + 1, lo:hi] * _shifted(sh_ref, k - left, tm)
            zc_buf[:, lo:hi] = acc
        zc = zc_buf[...] + vec_ref[1:2, :]
        zc_ref[...] = zc.astype(BF16)
        zn, _ = _layer_norm_stats(zc)
        zl = zn * vec_ref[2:3, :] + vec_ref[3:4, :]
        zs = zl * _sigmoid(zl)
        y = _dot(zs.astype(BF16), wpw2_ref[...]) + vec_ref[4:5, :]
        y_ref[...] = y.astype(BF16)
        x2_ref[...] = x_ref[...] + vec_ref[0:1, :] * y

    return _call(
        body, name="cf_fwd", nsteps=nsteps,
        in_specs=[*_halo_specs(tm, nu, t_total), _rows(tm, d), _full(vec.shape), _full(wdw.shape), _VM],
        out_specs=[_rows(tm, d), _rows(tm, d), _rows(tm, d)],
        out_shape=[jax.ShapeDtypeStruct((t_total, d), BF16), jax.ShapeDtypeStruct((t_total, d), BF16),
                   jax.ShapeDtypeStruct((t_total, d), F32)],
        scratch_shapes=[pltpu.VMEM((tm + 2 * HALO, d), F32), _shift_scratch(tm), pltpu.VMEM((tm, d), F32)],
        args=(u, u, u, x, vec, wdw, wpw2), exch=exch)


def _ffn_chunks(f, width=FFN_CHUNK):
    return [(lo, min(lo + width, f)) for lo in range(0, f, width)]


def _ffn_fwd(x2, vec, wg, wu, wd, tm, exch=None):
    t_total, d = x2.shape
    f = wg.shape[0]

    def body(x_ref, vec_ref, wg_ref, wu_ref, wd_ref, a_ref, b_ref, f_ref, x3_ref):
        xv = x_ref[...]
        h = _norm_mod(xv, vec_ref[0:1, :], vec_ref[1:2, :], vec_ref[2:3, :]).astype(BF16)
        y = None
        for lo, hi in _ffn_chunks(f):
            a = _dot_nt(h, wg_ref[lo:hi, :])
            b = _dot_nt(h, wu_ref[lo:hi, :])
            a_ref[:, lo:hi] = a.astype(BF16)
            b_ref[:, lo:hi] = b.astype(BF16)
            s = (a * _sigmoid(a) * b).astype(BF16)
            part = _dot(s, wd_ref[lo:hi, :])
            y = part if y is None else y + part
        f_ref[...] = y.astype(BF16)
        x3_ref[...] = xv + vec_ref[3:4, :] * y

    return _call(
        body, name="ffn_fwd", nsteps=t_total // tm,
        in_specs=[_rows(tm, d), _full(vec.shape), _VM, _VM, _VM],
        out_specs=[_rows(tm, f), _rows(tm, f), _rows(tm, d), _rows(tm, d)],
        out_shape=[jax.ShapeDtypeStruct((t_total, f), BF16), jax.ShapeDtypeStruct((t_total, f), BF16),
                   jax.ShapeDtypeStruct((t_total, d), BF16), jax.ShapeDtypeStruct((t_total, d), F32)],
        args=(x2, vec, wg, wu, wd), exch=exch)


def _final_fwd_bwd(x, target, vec, tm):
    t_total, d = x.shape

    def body(x_ref, t_ref, vec_ref, dx_ref, acc_ref):
        @pl.when(pl.program_id(0) == 0)
        def _():
            acc_ref[...] = jnp.zeros_like(acc_ref)

        g = vec_ref[0:1, :]
        xhat, r = _rms(x_ref[...])
        e = xhat * g - t_ref[...]
        acc_ref[1:2, :] += jnp.zeros((1, d), F32) + 0.5 * jnp.sum(jnp.mean(e * e, axis=-1, keepdims=True))
        dout = e * (1.0 / d)
        acc_ref[0:1, :] += _colsum(dout * xhat)
        dxn = dout * g
        dx_ref[...] = r * (dxn - xhat * jnp.mean(dxn * xhat, axis=-1, keepdims=True))

    return _call(
        body, name="final_fwd_bwd", nsteps=t_total // tm,
        in_specs=[_rows(tm, d), _rows(tm, d), _full(vec.shape)],
        out_specs=[_rows(tm, d), _VM],
        out_shape=[jax.ShapeDtypeStruct((t_total, d), F32), jax.ShapeDtypeStruct((8, d), F32)],
        args=(x, target, vec))


def _zero_at_start(*refs):
    @pl.when(pl.program_id(0) == 0)
    def _():
        for ref in refs:
            ref[...] = jnp.zeros_like(ref)


def _emit_bf16_at_end(nsteps, acc_ref, out_ref):
    @pl.when(pl.program_id(0) == nsteps - 1)
    def _():
        out_ref[...] = acc_ref[...].astype(BF16)


def _ffn_bwd_down(dx3, fout, a, b, vec, wd, tm, exch=None):
    t_total, d = dx3.shape
    f = a.shape[1]

    def body(dx_ref, f_ref, a_ref, b_ref, vec_ref, wd_ref, da_ref, db_ref, dwd_out, acc_ref, dwd_ref):
        _zero_at_start(dwd_ref, acc_ref)
        dx = dx_ref[...]
        acc_ref[0:1, :] += _colsum(dx * f_ref[...].astype(F32))
        dy = (dx * vec_ref[0:1, :]).astype(BF16)
        for lo, hi in _ffn_chunks(f, FFN_CHUNK // 2):
            av = a_ref[:, lo:hi].astype(F32)
            bv = b_ref[:, lo:hi].astype(F32)
            sg = _sigmoid(av)
            silu = av * sg
            ds = _dot_nt(dy, wd_ref[lo:hi, :])
            da_ref[:, lo:hi] = (ds * bv * (sg * (1.0 + av * (1.0 - sg)))).astype(BF16)
            db_ref[:, lo:hi] = (ds * silu).astype(BF16)
            dwd_ref[lo:hi, :] += _dot_tn((silu * bv).astype(BF16), dy)
        _emit_bf16_at_end(t_total // tm, dwd_ref, dwd_out)

    return _call(
        body, name="ffn_bwd_down", nsteps=t_total // tm,
        in_specs=[_rows(tm, d), _rows(tm, d), _rows(tm, f), _rows(tm, f), _full(vec.shape), _VM],
        out_specs=[_rows(tm, f), _rows(tm, f), _VM, _VM],
        out_shape=[jax.ShapeDtypeStruct((t_total, f), BF16), jax.ShapeDtypeStruct((t_total, f), BF16),
                   jax.ShapeDtypeStruct(wd.shape, BF16), jax.ShapeDtypeStruct((8, d), F32)],
        scratch_shapes=[pltpu.VMEM(wd.shape, F32)],
        args=(dx3, fout, a, b, vec, wd), exch=exch)


def _ffn_bwd_up(da, db, x2, dx3, vec, wg, wu, tm, exch=None):
    t_total, d = x2.shape
    f = da.shape[1]

    def body(da_ref, db_ref, x_ref, dx_ref, vec_ref, wg_ref, wu_ref, dx2_ref, dwg_out, dwu_out, acc_ref,
             dwg_ref, dwu_ref):
        _zero_at_start(dwg_ref, dwu_ref, acc_ref)
        xv = x_ref[...]
        g, sh, sc = vec_ref[0:1, :], vec_ref[1:2, :], vec_ref[2:3, :]
        h = _norm_mod(xv, g, sh, sc).astype(BF16)
        dav = da_ref[...]
        dbv = db_ref[...]
        dwg_ref[...] += _dot_tn(dav, h)
        dwu_ref[...] += _dot_tn(dbv, h)
        dh = _dot(dav, wg_ref[...]) + _dot(dbv, wu_ref[...])
        dxn, dsh, dsc, dg = _norm_mod_bwd(dh, xv, g, sc)
        acc_ref[0:1, :] += dsh
        acc_ref[1:2, :] += dsc
        acc_ref[2:3, :] += dg
        dx2_ref[...] = dx_ref[...] + dxn
        _emit_bf16_at_end(t_total // tm, dwg_ref, dwg_out)
        _emit_bf16_at_end(t_total // tm, dwu_ref, dwu_out)

    return _call(
        body, name="ffn_bwd_up", nsteps=t_total // tm,
        in_specs=[_rows(tm, f), _rows(tm, f), _rows(tm, d), _rows(tm, d), _full(vec.shape), _VM, _VM],
        out_specs=[_rows(tm, d), _VM, _VM, _VM],
        out_shape=[jax.ShapeDtypeStruct((t_total, d), F32), jax.ShapeDtypeStruct(wg.shape, BF16),
                   jax.ShapeDtypeStruct(wu.shape, BF16), jax.ShapeDtypeStruct((8, d), F32)],
        scratch_shapes=[pltpu.VMEM(wg.shape, F32), pltpu.VMEM(wu.shape, F32)],
        args=(da, db, x2, dx3, vec, wg, wu), exch=exch, vmem_limit=VMEM_LIMIT_WIDE)


def _ab_bwd_out(dx, y, u, vec, conv, wpool, scale, wout, tm, exch=None):
    t_total, d = dx.shape
    nu = u.shape[1]
    da = nu // 4
    gw = da // len(POOL_WINDOWS)
    nsteps = t_total // tm

    def body(dx_ref, y_ref, up_ref, uc_ref, un_ref, vec_ref, conv_ref, wpool_ref, scale_ref, wout_ref,
             dpre_ref, dwout_out, dwpool_ref, acc_ref, q_ext, p_ext, dwout_ref):
        _zero_at_start(dwout_ref, dwpool_ref, acc_ref)
        i = pl.program_id(0)
        dxv = dx_ref[...]
        acc_ref[0:1, :] += _colsum(dxv * y_ref[...].astype(F32))
        dy = (dxv * vec_ref[0:1, :]).astype(BF16)
        bg, cq, pooled, ybpre = _ab_core(up_ref, uc_ref, un_ref, conv_ref, wpool_ref, q_ext, p_ext,
                                         i, nsteps, tm, t_total)
        cat = jnp.concatenate([bg * cq, ybpre * scale_ref[...]], axis=1).astype(BF16)
        dwout_ref[...] += _dot_tn(cat, dy)
        dcat = _dot_nt(dy, wout_ref[...])
        dya = dcat[:, 0:da]
        dyb = dcat[:, da:2 * da]
        acc_ref[1:2, 0:da] += _colsum(dyb * ybpre)
        dybpre = (dyb * scale_ref[...]).astype(BF16)
        dpooled = []
        for g in range(len(POOL_WINDOWS)):
            dg = dybpre[:, g * gw:(g + 1) * gw]
            dwpool_ref[g] += _dot_tn(pooled[g], dg)
            dpooled.append(_dot_nt(dg, wpool_ref[g]))
        dpre_ref[...] = jnp.concatenate([dya * cq, dya * bg] + dpooled, axis=1).astype(BF16)
        _emit_bf16_at_end(nsteps, dwout_ref, dwout_out)

    return _call(
        body, name="ab_bwd_out", nsteps=nsteps,
        in_specs=[_rows(tm, d), _rows(tm, d), *_halo_specs(tm, nu, t_total), _full(vec.shape),
                  _full(conv.shape), _full(wpool.shape), _full(scale.shape), _VM],
        out_specs=[_rows(tm, 3 * da), _VM, _VM, _VM],
        out_shape=[jax.ShapeDtypeStruct((t_total, 3 * da), BF16), jax.ShapeDtypeStruct(wout.shape, BF16),
                   jax.ShapeDtypeStruct(wpool.shape, F32), jax.ShapeDtypeStruct((8, d), F32)],
        scratch_shapes=[pltpu.VMEM((tm + 2 * HALO, da), F32), pltpu.VMEM((tm + 2 * HALO, da), F32),
                        pltpu.VMEM(wout.shape, F32)],
        args=(dx, y, u, u, u, vec, conv, wpool, scale, wout), exch=exch)


def _ab_bwd_in(dpre, u, x, dx, vec, conv, win, tm, exch=None):
    t_total, d = x.shape
    nu = u.shape[1]
    da = nu // 4
    gw = da // len(POOL_WINDOWS)
    nsteps = t_total // tm

    def body(dp_ref, dc_ref, dn_ref, up_ref, uc_ref, un_ref, x_ref, dx_ref, vec_ref, conv_ref, win_ref,
             dxin_ref, dwin_out, dconv_ref, acc_ref, dcq_ext, q_ext, dpl_ext, dwin_ref):
        _zero_at_start(dwin_ref, dconv_ref, acc_ref)
        i = pl.program_id(0)

        def ucols(ref, k):
            return ref[:, k * da:(k + 1) * da].astype(F32)

        def dcols(ref, k):
            return ref[:, k * da:(k + 1) * da].astype(F32)

        _fill_ext(dcq_ext, dcols(dp_ref, 1), dcols(dc_ref, 1), dcols(dn_ref, 1), i, nsteps, tm)
        _fill_ext(q_ext, ucols(up_ref, 1) * ucols(up_ref, 2), ucols(uc_ref, 1) * ucols(uc_ref, 2),
                  ucols(un_ref, 1) * ucols(un_ref, 2), i, nsteps, tm)
        _fill_ext(dpl_ext, dcols(dp_ref, 2), dcols(dc_ref, 2), dcols(dn_ref, 2), i, nsteps, tm)
        dq = (conv_ref[0:1, :] * dcq_ext[HALO + 1:HALO + 1 + tm, :] + conv_ref[1:2, :] * dcq_ext[HALO:HALO + tm, :]
              + conv_ref[2:3, :] * dcq_ext[HALO - 1:HALO - 1 + tm, :])
        dcq = dcq_ext[HALO:HALO + tm, :]
        for k in range(3):
            dconv_ref[k:k + 1, :] += _colsum(dcq * q_ext[HALO + k - 1:HALO + k - 1 + tm, :])
        dcg = dq * ucols(uc_ref, 2)
        dv = dq * ucols(uc_ref, 1)
        t_ext = i * tm - HALO + lax.broadcasted_iota(jnp.int32, (tm + 2 * HALO, 1), 0)
        dps = []
        for g, wdw in enumerate(POOL_WINDOWS):
            left = wdw // 2
            right = wdw - 1 - left
            lo, hi = g * gw, (g + 1) * gw
            dpg = dpl_ext[HALO:HALO + tm, lo:hi]
            dpl_ext[:, lo:hi] = dpl_ext[:, lo:hi] / _window_count(t_ext, wdw, t_total)
            s = dpl_ext[HALO - right:HALO - right + tm, lo:hi]
            for o in range(-right + 1, left + 1):
                s = s + dpl_ext[HALO + o:HALO + o + tm, lo:hi]
            dps.append(s - dpg)
        du = jnp.concatenate([dcols(dc_ref, 0), dcg, dv] + dps, axis=1).astype(BF16)
        xv = x_ref[...]
        g, sh, sc = vec_ref[0:1, :], vec_ref[1:2, :], vec_ref[2:3, :]
        h = _norm_mod(xv, g, sh, sc).astype(BF16)
        dh = _in_proj_bwd(h, du, win_ref, dwin_ref)
        dxn, dsh, dsc, dg = _norm_mod_bwd(dh, xv, g, sc)
        acc_ref[0:1, :] += dsh
        acc_ref[1:2, :] += dsc
        acc_ref[2:3, :] += dg
        dxin_ref[...] = dx_ref[...] + dxn
        _emit_bf16_at_end(nsteps, dwin_ref, dwin_out)

    ext = pltpu.VMEM((tm + 2 * HALO, da), F32)
    return _call(
        body, name="ab_bwd_in", nsteps=nsteps,
        in_specs=[*_halo_specs(tm, 3 * da, t_total), *_halo_specs(tm, nu, t_total), _rows(tm, d), _rows(tm, d),
                  _full(vec.shape), _full(conv.shape), _VM],
        out_specs=[_rows(tm, d), _VM, _VM, _VM],
        out_shape=[jax.ShapeDtypeStruct((t_total, d), F32), jax.ShapeDtypeStruct(win.shape, BF16),
                   jax.ShapeDtypeStruct((8, da), F32), jax.ShapeDtypeStruct((8, d), F32)],
        scratch_shapes=[ext, ext, ext, pltpu.VMEM(win.shape, F32)],
        args=(dpre, dpre, dpre, u, u, u, x, dx, vec, conv, win), exch=exch)


def _cf_bwd_out(dx, y, zc, vec, wpw2, tm, exch=None):
    t_total, d = dx.shape

    def body(dx_ref, y_ref, zc_ref, vec_ref, w_ref, dzc_ref, dw_out, acc_ref, dw_ref):
        _zero_at_start(dw_ref, acc_ref)
        dxv = dx_ref[...]
        acc_ref[0:1, :] += _colsum(dxv * y_ref[...].astype(F32))
        dyf = dxv * vec_ref[0:1, :]
        acc_ref[1:2, :] += _colsum(dyf)
        dy = dyf.astype(BF16)
        zn, rstd = _layer_norm_stats(zc_ref[...].astype(F32))
        lng = vec_ref[1:2, :]
        zl = zn * lng + vec_ref[2:3, :]
        sg = _sigmoid(zl)
        dw_ref[...] += _dot_tn((zl * sg).astype(BF16), dy)
        dzl = _dot_nt(dy, w_ref[...]) * (sg * (1.0 + zl * (1.0 - sg)))
        acc_ref[2:3, :] += _colsum(dzl * zn)
        acc_ref[3:4, :] += _colsum(dzl)
        dzn = dzl * lng
        dzc = rstd * (dzn - jnp.mean(dzn, axis=-1, keepdims=True)
                      - zn * jnp.mean(dzn * zn, axis=-1, keepdims=True))
        acc_ref[4:5, :] += _colsum(dzc)
        dzc_ref[...] = dzc.astype(BF16)
        _emit_bf16_at_end(t_total // tm, dw_ref, dw_out)

    return _call(
        body, name="cf_bwd_out", nsteps=t_total // tm,
        in_specs=[_rows(tm, d), _rows(tm, d), _rows(tm, d), _full(vec.shape), _VM],
        out_specs=[_rows(tm, d), _VM, _VM],
        out_shape=[jax.ShapeDtypeStruct((t_total, d), BF16), jax.ShapeDtypeStruct(wpw2.shape, BF16),
                   jax.ShapeDtypeStruct((8, d), F32)],
        scratch_shapes=[pltpu.VMEM(wpw2.shape, F32)],
        args=(dx, y, zc, vec, wpw2), exch=exch)


def _cf_bwd_in(dzc, u, x, dx, vec, wdw, wpw1, tm, exch=None):
    t_total, d = x.shape
    nu = u.shape[1]
    nsteps = t_total // tm
    left = (CONF_KERNEL - 1) // 2

    def body(dp_ref, dc_ref, dn_ref, up_ref, uc_ref, un_ref, x_ref, dx_ref, vec_ref, wdw_ref, w_ref,
             dxin_ref, dw_out, dwdw_ref, db1_ref, acc_ref, dzc_ext, z_ext, sh_ref, dz_buf, dw_ref):
        _zero_at_start(dw_ref, dwdw_ref, db1_ref, acc_ref)
        i = pl.program_id(0)
        _fill_ext(dzc_ext, dp_ref[...].astype(F32), dc_ref[...].astype(F32), dn_ref[...].astype(F32),
                  i, nsteps, tm)
        _glu_ext(up_ref, uc_ref, un_ref, z_ext, i, nsteps, tm)
        for lo in range(0, d, CONV_COLS):
            hi = lo + CONV_COLS
            _fill_shifts(sh_ref, dzc_ext, lo, hi, tm)
            acc = wdw_ref[0:1, lo:hi] * _shifted(sh_ref, left, tm)
            for k in range(1, CONF_KERNEL):
                acc = acc + wdw_ref[k:k + 1, lo:hi] * _shifted(sh_ref, left - k, tm)
            dz_buf[:, lo:hi] = acc
            dzc = dzc_ext[HALO:HALO + tm, lo:hi]
            _fill_shifts(sh_ref, z_ext, lo, hi, tm)
            for k in range(CONF_KERNEL):
                dwdw_ref[k:k + 1, lo:hi] += _colsum(dzc * _shifted(sh_ref, k - left, tm))
        dz = dz_buf[...]
        av = uc_ref[:, 0:d].astype(F32)
        sg = _sigmoid(uc_ref[:, d:2 * d].astype(F32))
        duf = jnp.concatenate([dz * sg, dz * av * sg * (1.0 - sg)], axis=1)
        db1_ref[0:1, :] += _colsum(duf)
        du = duf.astype(BF16)
        xv = x_ref[...]
        g, sh, sc = vec_ref[0:1, :], vec_ref[1:2, :], vec_ref[2:3, :]
        h = _norm_mod(xv, g, sh, sc).astype(BF16)
        dh = _in_proj_bwd(h, du, w_ref, dw_ref)
        dxn, dsh, dsc, dg = _norm_mod_bwd(dh, xv, g, sc)
        acc_ref[0:1, :] += dsh
        acc_ref[1:2, :] += dsc
        acc_ref[2:3, :] += dg
        dxin_ref[...] = dx_ref[...] + dxn
        _emit_bf16_at_end(nsteps, dw_ref, dw_out)

    ext = pltpu.VMEM((tm + 2 * HALO, d), F32)
    return _call(
        body, name="cf_bwd_in", nsteps=nsteps,
        in_specs=[*_halo_specs(tm, d, t_total), *_halo_specs(tm, nu, t_total), _rows(tm, d), _rows(tm, d),
                  _full(vec.shape), _full(wdw.shape), _VM],
        out_specs=[_rows(tm, d), _VM, _VM, _VM, _VM],
        out_shape=[jax.ShapeDtypeStruct((t_total, d), F32), jax.ShapeDtypeStruct(wpw1.shape, BF16),
                   jax.ShapeDtypeStruct((32, d), F32), jax.ShapeDtypeStruct((8, nu), F32),
                   jax.ShapeDtypeStruct((8, d), F32)],
        scratch_shapes=[ext, ext, _shift_scratch(tm), pltpu.VMEM((tm, d), F32), pltpu.VMEM(wpw1.shape, F32)],
        args=(dzc, dzc, dzc, u, u, u, x, dx, vec, wdw, wpw1), exch=exch)


def _mod_fwd(c_all, w_mod, b_cols):
    nl, d, ncol = w_mod.shape
    nb = c_all.shape[0]

    def body(c_ref, w_ref, b_ref, o_ref):
        cv = c_ref[...]
        ca = cv * _sigmoid(cv)
        o_ref[0] = jnp.dot(ca, w_ref[0], preferred_element_type=F32, precision=HIGHEST) + b_ref[0]

    return _pcall(
        body, name="mod_fwd", grid=(nl,),
        in_specs=[_full(c_all.shape), pl.BlockSpec((1, d, ncol), lambda l: (l, 0, 0)),
                  pl.BlockSpec((1, 1, ncol), lambda l: (l, 0, 0))],
        out_specs=pl.BlockSpec((1, nb, ncol), lambda l: (l, 0, 0)),
        out_shape=jax.ShapeDtypeStruct((nl, nb, ncol), F32),
        compiler_params=_seq_params(),
    )(c_all, w_mod, b_cols.reshape(nl, 1, ncol))


def _mod_bwd(c_all_t, dmod_cols):
    d, nb = c_all_t.shape
    nl, _, ncol = dmod_cols.shape

    def body(c_ref, dm_ref, o_ref):
        cv = c_ref[...]
        ca = cv * _sigmoid(cv)
        o_ref[0] = jnp.dot(ca, dm_ref[0], preferred_element_type=F32, precision=HIGHEST)

    return _pcall(
        body, name="mod_bwd", grid=(nl,),
        in_specs=[_full(c_all_t.shape), pl.BlockSpec((1, nb, ncol), lambda l: (l, 0, 0))],
        out_specs=pl.BlockSpec((1, d, ncol), lambda l: (l, 0, 0)),
        out_shape=jax.ShapeDtypeStruct((nl, d, ncol), F32),
        compiler_params=_seq_params(),
    )(c_all_t, dmod_cols)


def _row_block(r, c):
    if r * c <= EW_BLOCK_ELEMS:
        return r
    best = None
    for br in range(8, r, 8):
        if r % br == 0 and br * c <= EW_BLOCK_ELEMS:
            best = br
    assert best is not None, (r, c)
    return best


def _as2d(a):
    return a.reshape(-1, a.shape[-1])


def _adamw(w, gparts, m, v):
    shape = w.shape
    w2, m2, v2 = _as2d(w), _as2d(m), _as2d(v)
    g2 = [_as2d(g) for g in gparts]
    r, c = w2.shape
    br = _row_block(r, c)
    ng = len(g2)

    def body(*refs):
        w_ref, m_ref, v_ref = refs[0:3]
        g_refs = refs[3:3 + ng]
        g = g_refs[0][...]
        for gr in g_refs[1:]:
            g = g + gr[...]
        _adamw_update(g, w_ref[...], m_ref[...], v_ref[...], refs[3 + ng:])

    spec = pl.BlockSpec((br, c), lambda i: (i, 0))
    outs = _pcall(
        body, name="adamw", grid=(r // br,),
        in_specs=[spec] * (3 + ng), out_specs=[spec] * 4,
        out_shape=[jax.ShapeDtypeStruct((r, c), F32)] * 4,
        compiler_params=_seq_params(),
    )(w2, m2, v2, *g2)
    return tuple(o.reshape(shape) for o in outs)


def _adamw_update(g, w, m, v, out_refs):
    go_ref, d_ref, mo_ref, vo_ref = out_refs
    mn = ADAM_B1 * m + (1.0 - ADAM_B1) * g
    vn = ADAM_B2 * v + (1.0 - ADAM_B2) * (g * g)
    m_hat = mn / (1.0 - ADAM_B1 ** ADAM_STEP)
    v_hat = vn / (1.0 - ADAM_B2 ** ADAM_STEP)
    go_ref[...] = g.reshape(go_ref.shape)
    d_ref[...] = (-ADAM_LR * (m_hat / (jnp.sqrt(v_hat) + ADAM_EPS) + ADAM_WD * w)).reshape(d_ref.shape)
    mo_ref[...] = mn.reshape(mo_ref.shape)
    vo_ref[...] = vn.reshape(vo_ref.shape)


def _adamw_partials(w, partials, m, v):
    nl, a, b = w.shape
    br = _row_block(a, b)
    nb = a // br

    def body(*refs):
        w_ref, m_ref, v_ref = refs[0:3]
        p_refs = refs[3:3 + nl]
        out_refs = refs[3 + nl:]
        for layer in range(nl):
            @pl.when(pl.program_id(0) == layer)
            def _(layer=layer):
                halves = []
                for core in range(2):
                    acc = p_refs[layer][core, 0].astype(F32)
                    for chip in range(1, N_CHIPS):
                        acc = acc + p_refs[layer][core, chip].astype(F32)
                    halves.append(acc)
                _adamw_update(halves[0] + halves[1], w_ref[...], m_ref[...], v_ref[...], out_refs)

    def part_spec(layer):
        def index(l, i):
            return 0, 0, jnp.where(l == layer, i, jnp.where(l < layer, 0, nb - 1)), 0
        return pl.BlockSpec((2, N_CHIPS, br, b), index)

    spec = pl.BlockSpec((br, b), lambda l, i: (l * nb + i, 0))
    outs = _pcall(
        body, name="adamw_partials", grid=(nl, nb),
        in_specs=[spec] * 3 + [part_spec(layer) for layer in range(nl)], out_specs=[spec] * 4,
        out_shape=[jax.ShapeDtypeStruct((nl * a, b), F32)] * 4,
        compiler_params=pltpu.CompilerParams(dimension_semantics=("arbitrary", "arbitrary"),
                                             vmem_limit_bytes=VMEM_LIMIT),
    )(_as2d(w), _as2d(m), _as2d(v), *partials)
    return tuple(o.reshape(w.shape) for o in outs)


def _allgather8(block, with_sum, exch=None):
    m_per, n = block.shape
    nx = 0 if exch is None else len(exch.arrs)
    nvm = 2 if with_sum else 1

    def body(x_ref, *rest):
        xin, out_ref = rest[:nx], rest[nx]
        sum_ref = rest[nx + 1] if with_sum else None
        xout = rest[nx + nvm:2 * nx + nvm]
        send_sems, recv_sems, local_sem = rest[2 * nx + nvm:2 * nx + nvm + 3]
        xsems = rest[2 * nx + nvm + 3:]
        if exch is not None:
            exch.start(xin, xout, xsems)
        x, y, c = _place()
        me, sibling = (x, y, c), (x, y, 1 - c)
        chips = [(1 - x, y), (x, 1 - y), (1 - x, 1 - y)]

        def rows(px, py, pc):
            return out_ref.at[pl.ds((4 * px + 2 * py + pc) * m_per, m_per), :]

        def copy(k, blk, to, src=None):
            return pltpu.make_async_remote_copy(
                src_ref=rows(*blk) if src is None else src, dst_ref=rows(*blk),
                send_sem=send_sems.at[k], recv_sem=recv_sems.at[k], device_id=to, device_id_type=MESH)

        mine = pltpu.make_async_copy(x_ref, rows(*me), local_sem)
        mine.start()
        first = [copy(0, me, sibling, src=x_ref)]
        first += [copy(1 + j, me, (*chip, c), src=x_ref) for j, chip in enumerate(chips)]
        for cp in first:
            cp.start()
        passed = [copy(4 + j, (*chip, c), sibling) for j, chip in enumerate(chips)]
        for j, chip in enumerate(chips):
            copy(1 + j, (*chip, c), me).wait_recv()
            passed[j].start()
        copy(0, sibling, me).wait_recv()
        for j, chip in enumerate(chips):
            copy(4 + j, (*chip, 1 - c), me).wait_recv()
        for cp in first + passed:
            cp.wait_send()
        mine.wait()
        if exch is not None:
            exch.mid(xin, xout, xsems)
            exch.wait(xin, xout, xsems)
        if with_sum:
            acc = out_ref[0:m_per, :]
            for k in range(1, N_DEV):
                acc = acc + out_ref[k * m_per:(k + 1) * m_per, :]
            sum_ref[...] = acc

    out_shape = [jax.ShapeDtypeStruct((N_DEV * m_per, n), F32)]
    out_specs = [_VM]
    if with_sum:
        out_shape.append(jax.ShapeDtypeStruct((m_per, n), F32))
        out_specs.append(_VM)
    res = _pcall(
        body, name=("allgather8_sum" if with_sum else "allgather8") + ("" if exch is None else "_" + exch.tag),
        in_specs=[_VM] + [_ANY] * nx, out_specs=out_specs + [_ANY] * nx,
        out_shape=out_shape + ([] if exch is None else exch.out_shapes()),
        scratch_shapes=[pltpu.SemaphoreType.DMA((7,)), pltpu.SemaphoreType.DMA((7,)), pltpu.SemaphoreType.DMA]
        + ([] if exch is None else exch.sems()),
        compiler_params=pltpu.CompilerParams(vmem_limit_bytes=VMEM_LIMIT),
    )(block, *([] if exch is None else exch.arrs))
    return list(res[:nvm]), list(res[nvm:])


def _my_cols(full, chip):
    w = full.shape[-1] // N_CHIPS
    return lax.dynamic_slice_in_dim(full, chip * w, w, axis=full.ndim - 1)


def _pad_rows(a, rows):
    return jnp.pad(a, ((0, rows - a.shape[0]), (0, 0)))


def _to_lanes(a):
    flat = a.reshape(-1)
    n = -(-flat.shape[0] // (8 * LANES)) * (8 * LANES)
    return jnp.pad(flat, (0, n - flat.shape[0])).reshape(-1, LANES)


class _Packer:
    def __init__(self):
        self.items = []
        self.rows = 0

    def add(self, name, a):
        lanes = _to_lanes(a)
        self.items.append((name, self.rows, a.shape, lanes))
        self.rows += lanes.shape[0]

    def pack(self):
        total = -(-self.rows // 8) * 8
        return _pad_rows(jnp.concatenate([it[3] for it in self.items], axis=0), total)

    def unpack(self, buf):
        out = {}
        for name, row, shape, lanes in self.items:
            size = 1
            for s in shape:
                size *= s
            out[name] = buf[row:row + lanes.shape[0]].reshape(-1)[:size].reshape(shape)
        return out


TM_SEQ = 512
TM_FFN = 256


LAYER_KEYS = ("in", "out", "gate", "up", "down")
BLOCKED_KEYS = ("in",)
TRANSPOSED = ("ffn_w_gate", "ffn_w_up")


def _layer_big_names(layer):
    i = layer // 2
    mix = (("ab_w_in", i), ("ab_w_out", i)) if layer % 2 == 0 else (("cf_w_pw1", i), ("cf_w_pw2", i))
    return dict(zip(LAYER_KEYS, mix + (("ffn_w_gate", layer), ("ffn_w_up", layer), ("ffn_w_down", layer))))


def _unpack_weight(key, g):
    g = g.reshape(N_CHIPS, -1, g.shape[-1])
    return g if key in BLOCKED_KEYS else g.reshape(-1, g.shape[-1])


def _chunk_grad(key, dw):
    parts = dw if key in BLOCKED_KEYS else dw.reshape(N_CHIPS, -1, dw.shape[-1])
    return parts.astype(BF16)


def _local_step(x, target, mods, p, shards, first):
    t_total, d = x.shape
    depth = mods.shape[0]
    tm = min(TM_SEQ, t_total)
    tmf = min(TM_FFN, t_total)
    saved = []
    xin = x
    weights = [{} for _ in range(depth)]

    def carried(stage, layer):
        if layer == 0:
            return {"in": (0, ("out", "gate")), "mix": (0, ("up", "down")), "ffn": (1, ("in", "out", "gate", "up"))}[stage]
        return {"in": (layer, ("down",)), "mix": (layer + 1, ("in", "out")), "ffn": (layer + 1, ("gate", "up"))}[stage]

    def gather(stage, layer):
        of, keys = carried(stage, layer)
        if of >= depth:
            return None
        return _Gather([shards[of][k].reshape(2, -1, shards[of][k].shape[-1]) for k in keys])

    def keep(stage, layer, arrs):
        of, keys = carried(stage, layer)
        for k, g in zip(keys, arrs):
            weights[of][k] = _unpack_weight(k, g)

    for k, g in first.items():
        weights[0][k] = _unpack_weight(k, g)
    for layer in range(depth):
        i = layer // 2
        lw = weights[layer]
        sh1, sc1, g1, sh2, sc2, g2 = (mods[layer, k:k + 1] for k in range(6))
        vec_in = jnp.concatenate([p["norm_mix_g"][layer:layer + 1], sh1, sc1], axis=0)
        bias = None if layer % 2 == 0 else p["cf_b_pw1"][i:i + 1]
        (u,), arrived = _in_proj(xin, vec_in, lw["in"], bias, tm, exch=gather("in", layer))
        keep("in", layer, arrived)
        if layer % 2 == 0:
            (y, x2), arrived = _ab_fwd(u, xin, g1, p["ab_conv"][i], p["ab_w_pool"][i].astype(BF16),
                                       p["ab_pool_scale"][i:i + 1], lw["out"], tm, exch=gather("mix", layer))
            zc = None
        else:
            vec_cf = jnp.concatenate([g1, p["cf_b_dw"][i:i + 1], p["cf_ln_g"][i:i + 1], p["cf_ln_b"][i:i + 1],
                                      p["cf_b_pw2"][i:i + 1]], axis=0)
            (zc, y, x2), arrived = _cf_fwd(u, xin, vec_cf, _pad_rows(p["cf_w_dw"][i], 32), lw["out"], tm,
                                           exch=gather("mix", layer))
        keep("mix", layer, arrived)
        vec_ffn = jnp.concatenate([p["norm_ffn_g"][layer:layer + 1], sh2, sc2, g2], axis=0)
        (a, b, fout, x3), arrived = _ffn_fwd(x2, vec_ffn, lw["gate"], lw["up"], lw["down"], tmf,
                                             exch=gather("ffn", layer))
        keep("ffn", layer, arrived)
        saved.append((xin, u, y, zc, x2, a, b, fout))
        xin = x3

    (dx, fin), _ = _final_fwd_bwd(xin, target, p["final_norm_g"].reshape(1, d), tm)
    grads = {"final_norm_g": fin[0], "loss": fin[1, 0:1]}
    per_layer = {k: [None] * depth for k in ("norm_mix_g", "norm_ffn_g")}
    half = {k: [None] * (depth // 2) for k in (
        "ab_conv", "ab_w_pool", "ab_pool_scale", "cf_b_pw1", "cf_w_dw", "cf_b_dw", "cf_ln_g", "cf_ln_b", "cf_b_pw2")}
    dmods = [None] * depth
    received = {}
    pending = None
    for layer in reversed(range(depth)):
        i = layer // 2
        lw = weights[layer]
        xin, u, y, zc, x2, a, b, fout = saved[layer]
        sh1, sc1, g1, sh2, sc2, g2 = (mods[layer, k:k + 1] for k in range(6))
        above = _Scatter([pending]) if pending is not None else None
        (da, db, dwd, acc_d), arrived = _ffn_bwd_down(dx, fout, a, b, g2, lw["down"], tmf, exch=above)
        if pending is not None:
            received[(layer + 1, "in")] = arrived[0]
        vec_ffn = jnp.concatenate([p["norm_ffn_g"][layer:layer + 1], sh2, sc2], axis=0)
        (dx2, dwg, dwu, acc_u), arrived = _ffn_bwd_up(da, db, x2, dx, vec_ffn, lw["gate"], lw["up"], tmf,
                                                      exch=_Scatter([_chunk_grad("down", dwd)]))
        received[(layer, "down")] = arrived[0]
        per_layer["norm_ffn_g"][layer] = acc_u[2]
        vec_in = jnp.concatenate([p["norm_mix_g"][layer:layer + 1], sh1, sc1], axis=0)
        send_gate = _Scatter([_chunk_grad("gate", dwg)])
        if layer % 2 == 0:
            (dpre, dwout, dwpool, acc_o), arrived = _ab_bwd_out(
                dx2, y, u, g1, p["ab_conv"][i], p["ab_w_pool"][i].astype(BF16), p["ab_pool_scale"][i:i + 1],
                lw["out"], tm, exch=send_gate)
            received[(layer, "gate")] = arrived[0]
            send_up_out = _Scatter([_chunk_grad("up", dwu), _chunk_grad("out", dwout)])
            (dx, dwin, dconv, acc_i), arrived = _ab_bwd_in(dpre, u, xin, dx2, vec_in, p["ab_conv"][i], lw["in"], tm,
                                                           exch=send_up_out)
            half["ab_w_pool"][i] = dwpool
            half["ab_pool_scale"][i] = acc_o[1, 0:d // 2]
            half["ab_conv"][i] = dconv[0:3]
        else:
            vec_cf = jnp.concatenate([g1, p["cf_ln_g"][i:i + 1], p["cf_ln_b"][i:i + 1]], axis=0)
            (dzc, dwout, acc_o), arrived = _cf_bwd_out(dx2, y, zc, vec_cf, lw["out"], tm, exch=send_gate)
            received[(layer, "gate")] = arrived[0]
            send_up_out = _Scatter([_chunk_grad("up", dwu), _chunk_grad("out", dwout)])
            (dx, dwin, dwdw, db1, acc_i), arrived = _cf_bwd_in(
                dzc, u, xin, dx2, vec_in, _pad_rows(p["cf_w_dw"][i], 32), lw["in"], tm, exch=send_up_out)
            half["cf_b_pw2"][i] = acc_o[1]
            half["cf_ln_g"][i] = acc_o[2]
            half["cf_ln_b"][i] = acc_o[3]
            half["cf_b_dw"][i] = acc_o[4]
            half["cf_w_dw"][i] = dwdw[0:CONF_KERNEL]
            half["cf_b_pw1"][i] = db1[0]
        received[(layer, "up")], received[(layer, "out")] = arrived
        per_layer["norm_mix_g"][layer] = acc_i[2]
        dmods[layer] = jnp.stack([acc_i[0], acc_i[1], acc_o[0], acc_u[0], acc_u[1], acc_d[0]], axis=0)
        pending = _chunk_grad("in", dwin)
    for k, v in {**per_layer, **half}.items():
        grads[k] = jnp.stack(v, axis=0)
    return dx, grads, jnp.stack(dmods, axis=0), received, pending


SMALL_COLS = ("ab_conv", "cf_b_pw1", "cf_w_dw", "cf_b_dw", "cf_ln_g", "cf_ln_b", "cf_b_pw2")
SMALL_REPL = ("norm_mix_g", "norm_ffn_g", "ab_w_pool", "ab_pool_scale", "final_norm_g")
WEIGHTS = ("norm_mix_g", "norm_ffn_g", "w_mod", "b_mod", "ab_w_in", "ab_conv", "ab_w_pool", "ab_pool_scale",
           "ab_w_out", "cf_w_pw1", "cf_b_pw1", "cf_w_dw", "cf_b_dw", "cf_ln_g", "cf_ln_b", "cf_w_pw2",
           "cf_b_pw2", "ffn_w_gate", "ffn_w_up", "ffn_w_down", "final_norm_g")


def kernel(x, c, norm_mix_g, norm_ffn_g, w_mod, b_mod, ab_w_in, ab_conv, ab_w_pool, ab_pool_scale, ab_w_out, cf_w_pw1, cf_b_pw1, cf_w_dw, cf_b_dw, cf_ln_g, cf_ln_b, cf_w_pw2, cf_b_pw2, ffn_w_gate, ffn_w_up, ffn_w_down, final_norm_g, loss_target, m_norm_mix_g, m_norm_ffn_g, m_w_mod, m_b_mod, m_ab_w_in, m_ab_conv, m_ab_w_pool, m_ab_pool_scale, m_ab_w_out, m_cf_w_pw1, m_cf_b_pw1, m_cf_w_dw, m_cf_b_dw, m_cf_ln_g, m_cf_ln_b, m_cf_w_pw2, m_cf_b_pw2, m_ffn_w_gate, m_ffn_w_up, m_ffn_w_down, m_final_norm_g, v_norm_mix_g, v_norm_ffn_g, v_w_mod, v_b_mod, v_ab_w_in, v_ab_conv, v_ab_w_pool, v_ab_pool_scale, v_ab_w_out, v_cf_w_pw1, v_cf_b_pw1, v_cf_w_dw, v_cf_b_dw, v_cf_ln_g, v_cf_ln_b, v_cf_w_pw2, v_cf_b_pw2, v_ffn_w_gate, v_ffn_w_up, v_ffn_w_down, v_final_norm_g):
    w = dict(norm_mix_g=norm_mix_g, norm_ffn_g=norm_ffn_g, w_mod=w_mod, b_mod=b_mod, ab_w_in=ab_w_in,
             ab_conv=ab_conv, ab_w_pool=ab_w_pool, ab_pool_scale=ab_pool_scale, ab_w_out=ab_w_out,
             cf_w_pw1=cf_w_pw1, cf_b_pw1=cf_b_pw1, cf_w_dw=cf_w_dw, cf_b_dw=cf_b_dw, cf_ln_g=cf_ln_g,
             cf_ln_b=cf_ln_b, cf_w_pw2=cf_w_pw2, cf_b_pw2=cf_b_pw2, ffn_w_gate=ffn_w_gate, ffn_w_up=ffn_w_up,
             ffn_w_down=ffn_w_down, final_norm_g=final_norm_g)
    mom = dict(norm_mix_g=m_norm_mix_g, norm_ffn_g=m_norm_ffn_g, w_mod=m_w_mod, b_mod=m_b_mod, ab_w_in=m_ab_w_in,
               ab_conv=m_ab_conv, ab_w_pool=m_ab_w_pool, ab_pool_scale=m_ab_pool_scale, ab_w_out=m_ab_w_out,
               cf_w_pw1=m_cf_w_pw1, cf_b_pw1=m_cf_b_pw1, cf_w_dw=m_cf_w_dw, cf_b_dw=m_cf_b_dw, cf_ln_g=m_cf_ln_g,
               cf_ln_b=m_cf_ln_b, cf_w_pw2=m_cf_w_pw2, cf_b_pw2=m_cf_b_pw2, ffn_w_gate=m_ffn_w_gate,
               ffn_w_up=m_ffn_w_up, ffn_w_down=m_ffn_w_down, final_norm_g=m_final_norm_g)
    var = dict(norm_mix_g=v_norm_mix_g, norm_ffn_g=v_norm_ffn_g, w_mod=v_w_mod, b_mod=v_b_mod, ab_w_in=v_ab_w_in,
               ab_conv=v_ab_conv, ab_w_pool=v_ab_w_pool, ab_pool_scale=v_ab_pool_scale, ab_w_out=v_ab_w_out,
               cf_w_pw1=v_cf_w_pw1, cf_b_pw1=v_cf_b_pw1, cf_w_dw=v_cf_w_dw, cf_b_dw=v_cf_b_dw, cf_ln_g=v_cf_ln_g,
               cf_ln_b=v_cf_ln_b, cf_w_pw2=v_cf_w_pw2, cf_b_pw2=v_cf_b_pw2, ffn_w_gate=v_ffn_w_gate,
               ffn_w_up=v_ffn_w_up, ffn_w_down=v_ffn_w_down, final_norm_g=v_final_norm_g)
    px, py, pc = _place()
    chip = 2 * px + py
    dev = 2 * chip + pc
    depth, d, mod_cols = w_mod.shape
    x = x[0]
    target = loss_target[0]

    def rows_major(name, t):
        return jnp.swapaxes(t, 1, 2) if name in TRANSPOSED else t

    shards = [{k: rows_major(name, w[name])[idx].astype(BF16) for k, (name, idx) in _layer_big_names(layer).items()}
              for layer in range(depth)]

    small_in = _Packer()
    small_in.add("c", c)
    for name in SMALL_COLS:
        small_in.add(name, w[name])
    def first_gather(*keys):
        return _Gather([shards[0][k].reshape(2, -1, shards[0][k].shape[-1]) for k in keys])

    first = {}
    (gathered,), (first["in"],) = _allgather8(small_in.pack(), with_sum=False, exch=first_gather("in"))
    gathered = gathered.reshape(N_DEV, -1, LANES)
    per_dev = [small_in.unpack(gathered[k]) for k in range(N_DEV)]
    c_all = jnp.concatenate([pd["c"] for pd in per_dev], axis=0)
    params = {name: jnp.concatenate([per_dev[2 * k][name] for k in range(N_CHIPS)], axis=-1)
              for name in SMALL_COLS}
    for name in SMALL_REPL:
        params[name] = w[name]

    mod_part = _mod_fwd(c_all, w_mod, _my_cols(b_mod, chip))
    (mod_all,), _ = _allgather8(mod_part.reshape(-1, LANES), with_sum=False)
    mod_all = mod_all.reshape(N_CHIPS, 2, depth, N_DEV, mod_cols)[:, 0]
    mod_all = jnp.moveaxis(mod_all, 0, 2).reshape(depth, N_DEV, N_CHIPS * mod_cols)
    mods = lax.dynamic_index_in_dim(mod_all, dev, axis=1, keepdims=False).reshape(depth, 6, d)

    grad_x, grads, dmods, received, last_chunk = _local_step(x, target, mods, params, shards, first)

    small_out = _Packer()
    small_out.add("dmods", dmods)
    for name in ("loss",) + SMALL_REPL + SMALL_COLS:
        small_out.add(name, grads[name])
    (parts_all, parts_sum), (received[(0, "in")],) = _allgather8(small_out.pack(), with_sum=True,
                                                                 exch=_Scatter([last_chunk]))
    small_sum = small_out.unpack(parts_sum)
    loss = small_sum["loss"][0]
    dmods_all = jnp.stack([small_out.unpack(pa)["dmods"] for pa in parts_all.reshape(N_DEV, -1, LANES)], axis=1)
    dmods_all = dmods_all.reshape(depth, N_DEV, 6 * d)

    g_final = {}
    g_final["w_mod"] = [_mod_bwd(c_all.T, _my_cols(dmods_all, chip))]
    g_final["b_mod"] = [small_sum["dmods"].reshape(depth, 6 * d)]
    for name in SMALL_REPL:
        g_final[name] = [small_sum[name]]
    for name in SMALL_COLS:
        g_final[name] = [_my_cols(small_sum[name], chip)]

    updates = {}
    for name in WEIGHTS:
        parts = [received[(layer, k)] for layer in range(depth)
                 for k, (other, _) in _layer_big_names(layer).items() if other == name]
        if parts:
            outs = _adamw_partials(rows_major(name, w[name]), parts, rows_major(name, mom[name]),
                                   rows_major(name, var[name]))
            updates[name] = [rows_major(name, o) for o in outs]
        else:
            updates[name] = _adamw(w[name], g_final[name], mom[name], var[name])
    return (loss, grad_x[None], *[updates[name][0] for name in WEIGHTS], *[updates[name][1] for name in WEIGHTS],
            *[updates[name][2] for name in WEIGHTS], *[updates[name][3] for name in WEIGHTS])
```

```python
import functools

import jax
import jax.numpy as jnp
from jax import lax
from jax.experimental import pallas as pl
from jax.experimental.pallas import tpu as pltpu

F32 = jnp.float32
BF16 = jnp.bfloat16
RMS_EPS = 1e-6
LN_EPS = 1e-5
ADAM_LR = 0.001
ADAM_B1 = 0.9
ADAM_B2 = 0.999
ADAM_EPS = 1e-08
ADAM_WD = 0.01
ADAM_STEP = 10
POOL_WINDOWS = (2, 4, 8, 16)
CONF_KERNEL = 31
N_CHIPS = 4
N_DEV = 8
HALO = 16
CONV_COLS = 256
FFN_CHUNK = 1536
LANES = 1024
VMEM_LIMIT = 56 * 1024 * 1024
VMEM_LIMIT_WIDE = 60 * 1024 * 1024
EW_BLOCK_ELEMS = 256 * 1024
MESH = pl.DeviceIdType.MESH
HIGHEST = lax.Precision.HIGHEST

_pcall = pl.pallas_call


def _dot(a, b):
    return jnp.dot(a, b, preferred_element_type=F32)


def _dot_tn(a, b):
    return lax.dot_general(a, b, (((0,), (0,)), ((), ())), preferred_element_type=F32)


def _dot_nt(a, b):
    return lax.dot_general(a, b, (((1,), (1,)), ((), ())), preferred_element_type=F32)


def _colsum(v):
    return jnp.sum(v, axis=0, keepdims=True)


def _sigmoid(v):
    return 1.0 / (1.0 + jnp.exp(-v))


def _rows(tm, c):
    return pl.BlockSpec((tm, c), lambda i: (i, 0))


def _full(shape):
    nd = len(shape)
    return pl.BlockSpec(shape, lambda i: (0,) * nd)


_VM = pl.BlockSpec(memory_space=pltpu.VMEM)
_ANY = pl.BlockSpec(memory_space=pl.ANY)


def _halo_specs(tm, c, t_total):
    r = tm // HALO
    last = t_total // HALO - 1
    prev = pl.BlockSpec((HALO, c), lambda i: (jnp.maximum(i * r - 1, 0), 0))
    nxt = pl.BlockSpec((HALO, c), lambda i: (jnp.minimum((i + 1) * r, last), 0))
    return prev, _rows(tm, c), nxt


def _seq_params(vmem_limit=VMEM_LIMIT):
    return pltpu.CompilerParams(dimension_semantics=("arbitrary",), vmem_limit_bytes=vmem_limit)


def _place():
    return lax.axis_index("x"), lax.axis_index("y"), lax.axis_index("c")


def _peer_chips(x, y):
    return [(1 - x, y), (x, 1 - y), (1 - x, 1 - y)]


class _Gather:
    tag = "gather"

    def __init__(self, arrs):
        self.arrs = list(arrs)

    def out_shapes(self):
        return [jax.ShapeDtypeStruct((N_CHIPS,) + a.shape, a.dtype) for a in self.arrs]

    def sems(self):
        n = len(self.arrs)
        return [pltpu.SemaphoreType.DMA((3 * n,)) for _ in range(4)] + [pltpu.SemaphoreType.DMA((n,))]

    def _copies(self, ins, outs, sems, kinds):
        ici_send, ici_recv, d2d_send, d2d_recv, local_sems = sems
        x, y, c = _place()
        me = 2 * x + y
        found = {kind: [] for kind in kinds}
        for j in range(len(ins)):
            if "local" in kinds:
                found["local"].append(pltpu.make_async_copy(ins[j], outs[j].at[me], local_sems.at[j]))
            for k, (px, py) in enumerate(_peer_chips(x, y)):
                ici = dict(send_sem=ici_send.at[3 * j + k], recv_sem=ici_recv.at[3 * j + k],
                           device_id=(px, py, c), device_id_type=MESH)
                d2d = dict(send_sem=d2d_send.at[3 * j + k], recv_sem=d2d_recv.at[3 * j + k],
                           device_id=(x, y, 1 - c), device_id_type=MESH)
                theirs = outs[j].at[2 * px + py]
                if "send" in kinds:
                    found["send"].append(pltpu.make_async_remote_copy(
                        src_ref=ins[j].at[c], dst_ref=outs[j].at[me, c], **ici))
                if "arrival" in kinds:
                    found["arrival"].append(pltpu.make_async_remote_copy(
                        src_ref=ins[j].at[c], dst_ref=theirs.at[c], **ici))
                if "pass" in kinds:
                    found["pass"].append(pltpu.make_async_remote_copy(
                        src_ref=theirs.at[c], dst_ref=theirs.at[c], **d2d))
                if "passed" in kinds:
                    found["passed"].append(pltpu.make_async_remote_copy(
                        src_ref=theirs.at[c], dst_ref=theirs.at[1 - c], **d2d))
        return found

    def start(self, ins, outs, sems):
        found = self._copies(ins, outs, sems, ("local", "send"))
        for cp in found["local"] + found["send"]:
            cp.start()

    def mid(self, ins, outs, sems):
        found = self._copies(ins, outs, sems, ("arrival", "pass"))
        for arrived, onward in zip(found["arrival"], found["pass"]):
            arrived.wait_recv()
            onward.start()

    def wait(self, ins, outs, sems):
        found = self._copies(ins, outs, sems, ("local", "send", "pass", "passed"))
        for cp in found["passed"]:
            cp.wait_recv()
        for cp in found["send"] + found["pass"]:
            cp.wait_send()
        for cp in found["local"]:
            cp.wait()


class _Scatter:
    tag = "scatter"

    def __init__(self, arrs):
        self.arrs = list(arrs)

    def out_shapes(self):
        return [jax.ShapeDtypeStruct((2,) + a.shape, a.dtype) for a in self.arrs]

    def sems(self):
        n = len(self.arrs)
        dma = pltpu.SemaphoreType.DMA
        return [dma((3 * n,)), dma((3 * n,)), dma((4 * n,)), dma((4 * n,)), dma((n,))]

    def _copies(self, ins, outs, sems, kinds):
        ici_send, ici_recv, d2d_send, d2d_recv, local_sems = sems
        x, y, c = _place()
        me = 2 * x + y
        found = {kind: [] for kind in kinds}
        for j in range(len(ins)):
            def d2d(k):
                return dict(send_sem=d2d_send.at[4 * j + k], recv_sem=d2d_recv.at[4 * j + k],
                            device_id=(x, y, 1 - c), device_id_type=MESH)

            if "local" in kinds:
                found["local"].append(pltpu.make_async_copy(ins[j].at[me], outs[j].at[0, me], local_sems.at[j]))
            if "own" in kinds:
                found["own"].append(pltpu.make_async_remote_copy(
                    src_ref=ins[j].at[me], dst_ref=outs[j].at[1, me], **d2d(3)))
            if "passed" in kinds:
                found["passed"].append(pltpu.make_async_remote_copy(
                    src_ref=ins[j].at[me], dst_ref=outs[j].at[1, me], **d2d(3)))
            for k, (px, py) in enumerate(_peer_chips(x, y)):
                ici = dict(send_sem=ici_send.at[3 * j + k], recv_sem=ici_recv.at[3 * j + k],
                           device_id=(px, py, c), device_id_type=MESH)
                peer = 2 * px + py
                if "send" in kinds:
                    found["send"].append(pltpu.make_async_remote_copy(
                        src_ref=ins[j].at[peer], dst_ref=outs[j].at[0, me], **ici))
                if "arrival" in kinds:
                    found["arrival"].append(pltpu.make_async_remote_copy(
                        src_ref=ins[j].at[me], dst_ref=outs[j].at[0, peer], **ici))
                if "pass" in kinds:
                    found["pass"].append(pltpu.make_async_remote_copy(
                        src_ref=outs[j].at[0, peer], dst_ref=outs[j].at[1, peer], **d2d(k)))
                if "passed" in kinds:
                    found["passed"].append(pltpu.make_async_remote_copy(
                        src_ref=outs[j].at[0, peer], dst_ref=outs[j].at[1, peer], **d2d(k)))
        return found

    def start(self, ins, outs, sems):
        found = self._copies(ins, outs, sems, ("local", "own", "send"))
        for cp in found["local"] + found["own"] + found["send"]:
            cp.start()

    def mid(self, ins, outs, sems):
        found = self._copies(ins, outs, sems, ("arrival", "pass"))
        for arrived, onward in zip(found["arrival"], found["pass"]):
            arrived.wait_recv()
            onward.start()

    def wait(self, ins, outs, sems):
        found = self._copies(ins, outs, sems, ("local", "own", "send", "pass", "passed"))
        for cp in found["passed"]:
            cp.wait_recv()
        for cp in found["own"] + found["send"] + found["pass"]:
            cp.wait_send()
        for cp in found["local"]:
            cp.wait()


def _call(body, *, name, nsteps, in_specs, out_specs, out_shape, args, scratch_shapes=(), exch=None,
          vmem_limit=VMEM_LIMIT):
    if exch is None:
        outs = _pcall(body, name=name, grid=(nsteps,), in_specs=list(in_specs), out_specs=list(out_specs),
                      out_shape=list(out_shape), scratch_shapes=list(scratch_shapes),
                      compiler_params=_seq_params(vmem_limit))(*args)
        return list(outs), []
    n, ni, no, ns = len(exch.arrs), len(in_specs), len(out_specs), len(scratch_shapes)

    def hosted(*refs):
        xin = refs[ni:ni + n]
        xout = refs[ni + n + no:ni + 2 * n + no]
        scr = refs[ni + 2 * n + no:]

        @pl.when(pl.program_id(0) == 0)
        def _():
            exch.start(xin, xout, scr[ns:])

        body(*refs[:ni], *refs[ni + n:ni + n + no], *scr[:ns])

        @pl.when(pl.program_id(0) == max(nsteps - 3, 0))
        def _():
            exch.mid(xin, xout, scr[ns:])

        @pl.when(pl.program_id(0) == nsteps - 1)
        def _():
            exch.wait(xin, xout, scr[ns:])

    outs = _pcall(hosted, name=name + "_" + exch.tag, grid=(nsteps,),
                  in_specs=[*in_specs, *[_ANY] * n], out_specs=[*out_specs, *[_ANY] * n],
                  out_shape=[*out_shape, *exch.out_shapes()],
                  scratch_shapes=[*scratch_shapes, *exch.sems()],
                  compiler_params=_seq_params(vmem_limit))(*args, *exch.arrs)
    return list(outs[:no]), list(outs[no:])


def _rms(x):
    r = lax.rsqrt(jnp.mean(x * x, axis=-1, keepdims=True) + RMS_EPS)
    return x * r, r


def _norm_mod(x, g, sh, sc):
    xhat, _ = _rms(x)
    return xhat * g * (1.0 + sc) + sh


def _norm_mod_bwd(dh, x, g, sc):
    xhat, r = _rms(x)
    n = xhat * g
    dsh = _colsum(dh)
    dsc = _colsum(dh * n)
    dn = dh * (1.0 + sc)
    dg = _colsum(dn * xhat)
    dxn = dn * g
    dx = r * (dxn - xhat * jnp.mean(dxn * xhat, axis=-1, keepdims=True))
    return dx, dsh, dsc, dg


def _fill_ext(ext_ref, prev, cur, nxt, i, nsteps, tm):
    ext_ref[0:HALO, :] = jnp.where(i > 0, prev, 0.0)
    ext_ref[HALO:HALO + tm, :] = cur
    ext_ref[HALO + tm:HALO + tm + HALO, :] = jnp.where(i < nsteps - 1, nxt, 0.0)


def _shift_scratch(tm):
    return pltpu.VMEM((8, tm + 2 * HALO - 8, CONV_COLS), F32)


def _fill_shifts(sh_ref, ext_ref, lo, hi, tm):
    for b in range(8):
        sh_ref[b] = ext_ref[b:b + tm + 2 * HALO - 8, lo:hi]


def _shifted(sh_ref, offset, tm):
    b = offset % 8
    start = HALO + offset - b
    return sh_ref[b, start:start + tm, :]


def _window_count(t, wdw, t_total):
    left = wdw // 2
    right = wdw - 1 - left
    cnt = jnp.minimum(t + right, t_total - 1) - jnp.maximum(t - left, 0) + 1
    return jnp.maximum(cnt, 1).astype(F32)


def _in_proj(x, vec, w, bias, tm, exch=None):
    t_total, d = x.shape
    nk = w.shape[2]
    n = N_CHIPS * nk
    has_bias = bias is not None

    def body(*refs):
        if has_bias:
            x_ref, vec_ref, w_ref, b_ref, u_ref = refs
        else:
            x_ref, vec_ref, w_ref, u_ref = refs
        h = _norm_mod(x_ref[...], vec_ref[0:1, :], vec_ref[1:2, :], vec_ref[2:3, :])
        h = h.astype(BF16)
        for k in range(N_CHIPS):
            u = _dot(h, w_ref[k])
            if has_bias:
                u = u + b_ref[:, k * nk:(k + 1) * nk]
            u_ref[:, k * nk:(k + 1) * nk] = u.astype(BF16)

    in_specs = [_rows(tm, d), _full(vec.shape), _VM]
    args = [x, vec, w]
    if has_bias:
        in_specs.append(_full(bias.shape))
        args.append(bias)
    return _call(
        body, name="in_proj_bias" if has_bias else "in_proj", nsteps=t_total // tm,
        in_specs=in_specs, out_specs=[_rows(tm, n)], out_shape=[jax.ShapeDtypeStruct((t_total, n), BF16)],
        args=args, exch=exch)


def _in_proj_bwd(h, du, w_ref, dw_ref):
    nk = w_ref.shape[2]
    dh = None
    for k in range(N_CHIPS):
        duk = du[:, k * nk:(k + 1) * nk]
        dw_ref[k] += _dot_tn(h, duk)
        part = _dot_nt(duk, w_ref[k])
        dh = part if dh is None else dh + part
    return dh


def _ab_core(up_ref, uc_ref, un_ref, conv_ref, wpool_ref, q_ext, p_ext, i, nsteps, tm, t_total):
    da = uc_ref.shape[1] // 4

    def cols(ref, k):
        return ref[:, k * da:(k + 1) * da].astype(F32)

    _fill_ext(q_ext, cols(up_ref, 1) * cols(up_ref, 2), cols(uc_ref, 1) * cols(uc_ref, 2),
              cols(un_ref, 1) * cols(un_ref, 2), i, nsteps, tm)
    _fill_ext(p_ext, cols(up_ref, 3), cols(uc_ref, 3), cols(un_ref, 3), i, nsteps, tm)
    bg = cols(uc_ref, 0)
    cq = (conv_ref[0:1, :] * q_ext[HALO - 1:HALO - 1 + tm, :] + conv_ref[1:2, :] * q_ext[HALO:HALO + tm, :]
          + conv_ref[2:3, :] * q_ext[HALO + 1:HALO + 1 + tm, :])
    t = i * tm + lax.broadcasted_iota(jnp.int32, (tm, 1), 0)
    gw = da // len(POOL_WINDOWS)
    pooled, ybpre = [], []
    for g, wdw in enumerate(POOL_WINDOWS):
        left = wdw // 2
        right = wdw - 1 - left
        lo, hi = g * gw, (g + 1) * gw
        s = p_ext[HALO - left:HALO - left + tm, lo:hi]
        for o in range(-left + 1, right + 1):
            s = s + p_ext[HALO + o:HALO + o + tm, lo:hi]
        pg = s / _window_count(t, wdw, t_total) - p_ext[HALO:HALO + tm, lo:hi]
        pooled.append(pg.astype(BF16))
        ybpre.append(_dot(pooled[-1], wpool_ref[g]))
    return bg, cq, pooled, jnp.concatenate(ybpre, axis=1)


def _ab_fwd(u, x, vec, conv, wpool, scale, wout, tm, exch=None):
    t_total, d = x.shape
    nu = u.shape[1]
    da = nu // 4
    nsteps = t_total // tm

    def body(up_ref, uc_ref, un_ref, x_ref, vec_ref, conv_ref, wpool_ref, scale_ref, wout_ref,
             y_ref, x2_ref, q_ext, p_ext):
        i = pl.program_id(0)
        bg, cq, _, ybpre = _ab_core(up_ref, uc_ref, un_ref, conv_ref, wpool_ref, q_ext, p_ext,
                                    i, nsteps, tm, t_total)
        cat = jnp.concatenate([bg * cq, ybpre * scale_ref[...]], axis=1).astype(BF16)
        y = _dot(cat, wout_ref[...])
        y_ref[...] = y.astype(BF16)
        x2_ref[...] = x_ref[...] + vec_ref[0:1, :] * y

    return _call(
        body, name="ab_fwd", nsteps=nsteps,
        in_specs=[*_halo_specs(tm, nu, t_total), _rows(tm, d), _full(vec.shape), _full(conv.shape),
                  _full(wpool.shape), _full(scale.shape), _VM],
        out_specs=[_rows(tm, d), _rows(tm, d)],
        out_shape=[jax.ShapeDtypeStruct((t_total, d), BF16), jax.ShapeDtypeStruct((t_total, d), F32)],
        scratch_shapes=[pltpu.VMEM((tm + 2 * HALO, da), F32), pltpu.VMEM((tm + 2 * HALO, da), F32)],
        args=(u, u, u, x, vec, conv, wpool, scale, wout), exch=exch)


def _glu_ext(up_ref, uc_ref, un_ref, z_ext, i, nsteps, tm):
    dz = uc_ref.shape[1] // 2

    def glu(ref):
        return ref[:, 0:dz].astype(F32) * _sigmoid(ref[:, dz:2 * dz].astype(F32))

    _fill_ext(z_ext, glu(up_ref), glu(uc_ref), glu(un_ref), i, nsteps, tm)


def _layer_norm_stats(zc):
    mu = jnp.mean(zc, axis=-1, keepdims=True)
    dlt = zc - mu
    rstd = lax.rsqrt(jnp.mean(dlt * dlt, axis=-1, keepdims=True) + LN_EPS)
    return dlt * rstd, rstd


def _cf_fwd(u, x, vec, wdw, wpw2, tm, exch=None):
    t_total, d = x.shape
    nu = u.shape[1]
    nsteps = t_total // tm
    left = (CONF_KERNEL - 1) // 2

    def body(up_ref, uc_ref, un_ref, x_ref, vec_ref, wdw_ref, wpw2_ref, zc_ref, y_ref, x2_ref, z_ext, sh_ref,
             zc_buf):
        i = pl.program_id(0)
        _glu_ext(up_ref, uc_ref, un_ref, z_ext, i, nsteps, tm)
        for lo in range(0, d, CONV_COLS):
            hi = lo + CONV_COLS
            _fill_shifts(sh_ref, z_ext, lo, hi, tm)
            acc = wdw_ref[0:1, lo:hi] * _shifted(sh_ref, -left, tm)
            for k in range(1, CONF_KERNEL):
                acc = acc + wdw_ref[k:k + 1, lo:hi] * _shifted(sh_ref, k - left, tm)
            zc_buf[:, lo:hi] = acc
        zc = zc_buf[...] + vec_ref[1:2, :]
        zc_ref[...] = zc.astype(BF16)
        zn, _ = _layer_norm_stats(zc)
        zl = zn * vec_ref[2:3, :] + vec_ref[3:4, :]
        zs = zl * _sigmoid(zl)
        y = _dot(zs.astype(BF16), wpw2_ref[...]) + vec_ref[4:5, :]
        y_ref[...] = y.astype(BF16)
        x2_ref[...] = x_ref[...] + vec_ref[0:1, :] * y

    return _call(
        body, name="cf_fwd", nsteps=nsteps,
        in_specs=[*_halo_specs(tm, nu, t_total), _rows(tm, d), _full(vec.shape), _full(wdw.shape), _VM],
        out_specs=[_rows(tm, d), _rows(tm, d), _rows(tm, d)],
        out_shape=[jax.ShapeDtypeStruct((t_total, d), BF16), jax.ShapeDtypeStruct((t_total, d), BF16),
                   jax.ShapeDtypeStruct((t_total, d), F32)],
        scratch_shapes=[pltpu.VMEM((tm + 2 * HALO, d), F32), _shift_scratch(tm), pltpu.VMEM((tm, d), F32)],
        args=(u, u, u, x, vec, wdw, wpw2), exch=exch)


def _ffn_chunks(f, width=FFN_CHUNK):
    return [(lo, min(lo + width, f)) for lo in range(0, f, width)]


def _ffn_fwd(x2, vec, wg, wu, wd, tm, exch=None):
    t_total, d = x2.shape
    f = wg.shape[0]

    def body(x_ref, vec_ref, wg_ref, wu_ref, wd_ref, a_ref, b_ref, f_ref, x3_ref):
        xv = x_ref[...]
        h = _norm_mod(xv, vec_ref[0:1, :], vec_ref[1:2, :], vec_ref[2:3, :]).astype(BF16)
        y = None
        for lo, hi in _ffn_chunks(f):
            a = _dot_nt(h, wg_ref[lo:hi, :])
            b = _dot_nt(h, wu_ref[lo:hi, :])
            a_ref[:, lo:hi] = a.astype(BF16)
            b_ref[:, lo:hi] = b.astype(BF16)
            s = (a * _sigmoid(a) * b).astype(BF16)
            part = _dot(s, wd_ref[lo:hi, :])
            y = part if y is None else y + part
        f_ref[...] = y.astype(BF16)
        x3_ref[...] = xv + vec_ref[3:4, :] * y

    return _call(
        body, name="ffn_fwd", nsteps=t_total // tm,
        in_specs=[_rows(tm, d), _full(vec.shape), _VM, _VM, _VM],
        out_specs=[_rows(tm, f), _rows(tm, f), _rows(tm, d), _rows(tm, d)],
        out_shape=[jax.ShapeDtypeStruct((t_total, f), BF16), jax.ShapeDtypeStruct((t_total, f), BF16),
                   jax.ShapeDtypeStruct((t_total, d), BF16), jax.ShapeDtypeStruct((t_total, d), F32)],
        args=(x2, vec, wg, wu, wd), exch=exch)


def _final_fwd_bwd(x, target, vec, tm):
    t_total, d = x.shape

    def body(x_ref, t_ref, vec_ref, dx_ref, acc_ref):
        @pl.when(pl.program_id(0) == 0)
        def _():
            acc_ref[...] = jnp.zeros_like(acc_ref)

        g = vec_ref[0:1, :]
        xhat, r = _rms(x_ref[...])
        e = xhat * g - t_ref[...]
        acc_ref[1:2, :] += jnp.zeros((1, d), F32) + 0.5 * jnp.sum(jnp.mean(e * e, axis=-1, keepdims=True))
        dout = e * (1.0 / d)
        acc_ref[0:1, :] += _colsum(dout * xhat)
        dxn = dout * g
        dx_ref[...] = r * (dxn - xhat * jnp.mean(dxn * xhat, axis=-1, keepdims=True))

    return _call(
        body, name="final_fwd_bwd", nsteps=t_total // tm,
        in_specs=[_rows(tm, d), _rows(tm, d), _full(vec.shape)],
        out_specs=[_rows(tm, d), _VM],
        out_shape=[jax.ShapeDtypeStruct((t_total, d), F32), jax.ShapeDtypeStruct((8, d), F32)],
        args=(x, target, vec))


def _zero_at_start(*refs):
    @pl.when(pl.program_id(0) == 0)
    def _():
        for ref in refs:
            ref[...] = jnp.zeros_like(ref)


def _emit_bf16_at_end(nsteps, acc_ref, out_ref):
    @pl.when(pl.program_id(0) == nsteps - 1)
    def _():
        out_ref[...] = acc_ref[...].astype(BF16)


def _ffn_bwd_down(dx3, fout, a, b, vec, wd, tm, exch=None):
    t_total, d = dx3.shape
    f = a.shape[1]

    def body(dx_ref, f_ref, a_ref, b_ref, vec_ref, wd_ref, da_ref, db_ref, dwd_out, acc_ref, dwd_ref):
        _zero_at_start(dwd_ref, acc_ref)
        dx = dx_ref[...]
        acc_ref[0:1, :] += _colsum(dx * f_ref[...].astype(F32))
        dy = (dx * vec_ref[0:1, :]).astype(BF16)
        for lo, hi in _ffn_chunks(f, FFN_CHUNK // 3):
            av = a_ref[:, lo:hi].astype(F32)
            bv = b_ref[:, lo:hi].astype(F32)
            sg = _sigmoid(av)
            silu = av * sg
            ds = _dot_nt(dy, wd_ref[lo:hi, :])
            da_ref[:, lo:hi] = (ds * bv * (sg * (1.0 + av * (1.0 - sg)))).astype(BF16)
            db_ref[:, lo:hi] = (ds * silu).astype(BF16)
            dwd_ref[lo:hi, :] += _dot_tn((silu * bv).astype(BF16), dy)
        _emit_bf16_at_end(t_total // tm, dwd_ref, dwd_out)

    return _call(
        body, name="ffn_bwd_down", nsteps=t_total // tm,
        in_specs=[_rows(tm, d), _rows(tm, d), _rows(tm, f), _rows(tm, f), _full(vec.shape), _VM],
        out_specs=[_rows(tm, f), _rows(tm, f), _VM, _VM],
        out_shape=[jax.ShapeDtypeStruct((t_total, f), BF16), jax.ShapeDtypeStruct((t_total, f), BF16),
                   jax.ShapeDtypeStruct(wd.shape, BF16), jax.ShapeDtypeStruct((8, d), F32)],
        scratch_shapes=[pltpu.VMEM(wd.shape, F32)],
        args=(dx3, fout, a, b, vec, wd), exch=exch)


def _ffn_bwd_up(da, db, x2, dx3, vec, wg, wu, tm, exch=None):
    t_total, d = x2.shape
    f = da.shape[1]

    def body(da_ref, db_ref, x_ref, dx_ref, vec_ref, wg_ref, wu_ref, dx2_ref, dwg_out, dwu_out, acc_ref,
             dwg_ref, dwu_ref):
        _zero_at_start(dwg_ref, dwu_ref, acc_ref)
        xv = x_ref[...]
        g, sh, sc = vec_ref[0:1, :], vec_ref[1:2, :], vec_ref[2:3, :]
        h = _norm_mod(xv, g, sh, sc).astype(BF16)
        dav = da_ref[...]
        dbv = db_ref[...]
        dwg_ref[...] += _dot_tn(dav, h)
        dwu_ref[...] += _dot_tn(dbv, h)
        dh = _dot(dav, wg_ref[...]) + _dot(dbv, wu_ref[...])
        dxn, dsh, dsc, dg = _norm_mod_bwd(dh, xv, g, sc)
        acc_ref[0:1, :] += dsh
        acc_ref[1:2, :] += dsc
        acc_ref[2:3, :] += dg
        dx2_ref[...] = dx_ref[...] + dxn
        _emit_bf16_at_end(t_total // tm, dwg_ref, dwg_out)
        _emit_bf16_at_end(t_total // tm, dwu_ref, dwu_out)

    return _call(
        body, name="ffn_bwd_up", nsteps=t_total // tm,
        in_specs=[_rows(tm, f), _rows(tm, f), _rows(tm, d), _rows(tm, d), _full(vec.shape), _VM, _VM],
        out_specs=[_rows(tm, d), _VM, _VM, _VM],
        out_shape=[jax.ShapeDtypeStruct((t_total, d), F32), jax.ShapeDtypeStruct(wg.shape, BF16),
                   jax.ShapeDtypeStruct(wu.shape, BF16), jax.ShapeDtypeStruct((8, d), F32)],
        scratch_shapes=[pltpu.VMEM(wg.shape, F32), pltpu.VMEM(wu.shape, F32)],
        args=(da, db, x2, dx3, vec, wg, wu), exch=exch, vmem_limit=VMEM_LIMIT_WIDE)


def _ab_bwd_out(dx, y, u, vec, conv, wpool, scale, wout, tm, exch=None):
    t_total, d = dx.shape
    nu = u.shape[1]
    da = nu // 4
    gw = da // len(POOL_WINDOWS)
    nsteps = t_total // tm

    def body(dx_ref, y_ref, up_ref, uc_ref, un_ref, vec_ref, conv_ref, wpool_ref, scale_ref, wout_ref,
             dpre_ref, dwout_out, dwpool_ref, acc_ref, q_ext, p_ext, dwout_ref):
        _zero_at_start(dwout_ref, dwpool_ref, acc_ref)
        i = pl.program_id(0)
        dxv = dx_ref[...]
        acc_ref[0:1, :] += _colsum(dxv * y_ref[...].astype(F32))
        dy = (dxv * vec_ref[0:1, :]).astype(BF16)
        bg, cq, pooled, ybpre = _ab_core(up_ref, uc_ref, un_ref, conv_ref, wpool_ref, q_ext, p_ext,
                                         i, nsteps, tm, t_total)
        cat = jnp.concatenate([bg * cq, ybpre * scale_ref[...]], axis=1).astype(BF16)
        dwout_ref[...] += _dot_tn(cat, dy)
        dcat = _dot_nt(dy, wout_ref[...])
        dya = dcat[:, 0:da]
        dyb = dcat[:, da:2 * da]
        acc_ref[1:2, 0:da] += _colsum(dyb * ybpre)
        dybpre = (dyb * scale_ref[...]).astype(BF16)
        dpooled = []
        for g in range(len(POOL_WINDOWS)):
            dg = dybpre[:, g * gw:(g + 1) * gw]
            dwpool_ref[g] += _dot_tn(pooled[g], dg)
            dpooled.append(_dot_nt(dg, wpool_ref[g]))
        dpre_ref[...] = jnp.concatenate([dya * cq, dya * bg] + dpooled, axis=1).astype(BF16)
        _emit_bf16_at_end(nsteps, dwout_ref, dwout_out)

    return _call(
        body, name="ab_bwd_out", nsteps=nsteps,
        in_specs=[_rows(tm, d), _rows(tm, d), *_halo_specs(tm, nu, t_total), _full(vec.shape),
                  _full(conv.shape), _full(wpool.shape), _full(scale.shape), _VM],
        out_specs=[_rows(tm, 3 * da), _VM, _VM, _VM],
        out_shape=[jax.ShapeDtypeStruct((t_total, 3 * da), BF16), jax.ShapeDtypeStruct(wout.shape, BF16),
                   jax.ShapeDtypeStruct(wpool.shape, F32), jax.ShapeDtypeStruct((8, d), F32)],
        scratch_shapes=[pltpu.VMEM((tm + 2 * HALO, da), F32), pltpu.VMEM((tm + 2 * HALO, da), F32),
                        pltpu.VMEM(wout.shape, F32)],
        args=(dx, y, u, u, u, vec, conv, wpool, scale, wout), exch=exch)


def _ab_bwd_in(dpre, u, x, dx, vec, conv, win, tm, exch=None):
    t_total, d = x.shape
    nu = u.shape[1]
    da = nu // 4
    gw = da // len(POOL_WINDOWS)
    nsteps = t_total // tm

    def body(dp_ref, dc_ref, dn_ref, up_ref, uc_ref, un_ref, x_ref, dx_ref, vec_ref, conv_ref, win_ref,
             dxin_ref, dwin_out, dconv_ref, acc_ref, dcq_ext, q_ext, dpl_ext, dwin_ref):
        _zero_at_start(dwin_ref, dconv_ref, acc_ref)
        i = pl.program_id(0)

        def ucols(ref, k):
            return ref[:, k * da:(k + 1) * da].astype(F32)

        def dcols(ref, k):
            return ref[:, k * da:(k + 1) * da].astype(F32)

        _fill_ext(dcq_ext, dcols(dp_ref, 1), dcols(dc_ref, 1), dcols(dn_ref, 1), i, nsteps, tm)
        _fill_ext(q_ext, ucols(up_ref, 1) * ucols(up_ref, 2), ucols(uc_ref, 1) * ucols(uc_ref, 2),
                  ucols(un_ref, 1) * ucols(un_ref, 2), i, nsteps, tm)
        _fill_ext(dpl_ext, dcols(dp_ref, 2), dcols(dc_ref, 2), dcols(dn_ref, 2), i, nsteps, tm)
        dq = (conv_ref[0:1, :] * dcq_ext[HALO + 1:HALO + 1 + tm, :] + conv_ref[1:2, :] * dcq_ext[HALO:HALO + tm, :]
              + conv_ref[2:3, :] * dcq_ext[HALO - 1:HALO - 1 + tm, :])
        dcq = dcq_ext[HALO:HALO + tm, :]
        for k in range(3):
            dconv_ref[k:k + 1, :] += _colsum(dcq * q_ext[HALO + k - 1:HALO + k - 1 + tm, :])
        dcg = dq * ucols(uc_ref, 2)
        dv = dq * ucols(uc_ref, 1)
        t_ext = i * tm - HALO + lax.broadcasted_iota(jnp.int32, (tm + 2 * HALO, 1), 0)
        dps = []
        for g, wdw in enumerate(POOL_WINDOWS):
            left = wdw // 2
            right = wdw - 1 - left
            lo, hi = g * gw, (g + 1) * gw
            dpg = dpl_ext[HALO:HALO + tm, lo:hi]
            dpl_ext[:, lo:hi] = dpl_ext[:, lo:hi] / _window_count(t_ext, wdw, t_total)
            s = dpl_ext[HALO - right:HALO - right + tm, lo:hi]
            for o in range(-right + 1, left + 1):
                s = s + dpl_ext[HALO + o:HALO + o + tm, lo:hi]
            dps.append(s - dpg)
        du = jnp.concatenate([dcols(dc_ref, 0), dcg, dv] + dps, axis=1).astype(BF16)
        xv = x_ref[...]
        g, sh, sc = vec_ref[0:1, :], vec_ref[1:2, :], vec_ref[2:3, :]
        h = _norm_mod(xv, g, sh, sc).astype(BF16)
        dh = _in_proj_bwd(h, du, win_ref, dwin_ref)
        dxn, dsh, dsc, dg = _norm_mod_bwd(dh, xv, g, sc)
        acc_ref[0:1, :] += dsh
        acc_ref[1:2, :] += dsc
        acc_ref[2:3, :] += dg
        dxin_ref[...] = dx_ref[...] + dxn
        _emit_bf16_at_end(nsteps, dwin_ref, dwin_out)

    ext = pltpu.VMEM((tm + 2 * HALO, da), F32)
    return _call(
        body, name="ab_bwd_in", nsteps=nsteps,
        in_specs=[*_halo_specs(tm, 3 * da, t_total), *_halo_specs(tm, nu, t_total), _rows(tm, d), _rows(tm, d),
                  _full(vec.shape), _full(conv.shape), _VM],
        out_specs=[_rows(tm, d), _VM, _VM, _VM],
        out_shape=[jax.ShapeDtypeStruct((t_total, d), F32), jax.ShapeDtypeStruct(win.shape, BF16),
                   jax.ShapeDtypeStruct((8, da), F32), jax.ShapeDtypeStruct((8, d), F32)],
        scratch_shapes=[ext, ext, ext, pltpu.VMEM(win.shape, F32)],
        args=(dpre, dpre, dpre, u, u, u, x, dx, vec, conv, win), exch=exch)


def _cf_bwd_out(dx, y, zc, vec, wpw2, tm, exch=None):
    t_total, d = dx.shape

    def body(dx_ref, y_ref, zc_ref, vec_ref, w_ref, dzc_ref, dw_out, acc_ref, dw_ref):
        _zero_at_start(dw_ref, acc_ref)
        dxv = dx_ref[...]
        acc_ref[0:1, :] += _colsum(dxv * y_ref[...].astype(F32))
        dyf = dxv * vec_ref[0:1, :]
        acc_ref[1:2, :] += _colsum(dyf)
        dy = dyf.astype(BF16)
        zn, rstd = _layer_norm_stats(zc_ref[...].astype(F32))
        lng = vec_ref[1:2, :]
        zl = zn * lng + vec_ref[2:3, :]
        sg = _sigmoid(zl)
        dw_ref[...] += _dot_tn((zl * sg).astype(BF16), dy)
        dzl = _dot_nt(dy, w_ref[...]) * (sg * (1.0 + zl * (1.0 - sg)))
        acc_ref[2:3, :] += _colsum(dzl * zn)
        acc_ref[3:4, :] += _colsum(dzl)
        dzn = dzl * lng
        dzc = rstd * (dzn - jnp.mean(dzn, axis=-1, keepdims=True)
                      - zn * jnp.mean(dzn * zn, axis=-1, keepdims=True))
        acc_ref[4:5, :] += _colsum(dzc)
        dzc_ref[...] = dzc.astype(BF16)
        _emit_bf16_at_end(t_total // tm, dw_ref, dw_out)

    return _call(
        body, name="cf_bwd_out", nsteps=t_total // tm,
        in_specs=[_rows(tm, d), _rows(tm, d), _rows(tm, d), _full(vec.shape), _VM],
        out_specs=[_rows(tm, d), _VM, _VM],
        out_shape=[jax.ShapeDtypeStruct((t_total, d), BF16), jax.ShapeDtypeStruct(wpw2.shape, BF16),
                   jax.ShapeDtypeStruct((8, d), F32)],
        scratch_shapes=[pltpu.VMEM(wpw2.shape, F32)],
        args=(dx, y, zc, vec, wpw2), exch=exch)


def _cf_bwd_in(dzc, u, x, dx, vec, wdw, wpw1, tm, exch=None):
    t_total, d = x.shape
    nu = u.shape[1]
    nsteps = t_total // tm
    left = (CONF_KERNEL - 1) // 2

    def body(dp_ref, dc_ref, dn_ref, up_ref, uc_ref, un_ref, x_ref, dx_ref, vec_ref, wdw_ref, w_ref,
             dxin_ref, dw_out, dwdw_ref, db1_ref, acc_ref, dzc_ext, z_ext, sh_ref, dz_buf, dw_ref):
        _zero_at_start(dw_ref, dwdw_ref, db1_ref, acc_ref)
        i = pl.program_id(0)
        _fill_ext(dzc_ext, dp_ref[...].astype(F32), dc_ref[...].astype(F32), dn_ref[...].astype(F32),
                  i, nsteps, tm)
        _glu_ext(up_ref, uc_ref, un_ref, z_ext, i, nsteps, tm)
        for lo in range(0, d, CONV_COLS):
            hi = lo + CONV_COLS
            _fill_shifts(sh_ref, dzc_ext, lo, hi, tm)
            acc = wdw_ref[0:1, lo:hi] * _shifted(sh_ref, left, tm)
            for k in range(1, CONF_KERNEL):
                acc = acc + wdw_ref[k:k + 1, lo:hi] * _shifted(sh_ref, left - k, tm)
            dz_buf[:, lo:hi] = acc
            dzc = dzc_ext[HALO:HALO + tm, lo:hi]
            _fill_shifts(sh_ref, z_ext, lo, hi, tm)
            for k in range(CONF_KERNEL):
                dwdw_ref[k:k + 1, lo:hi] += _colsum(dzc * _shifted(sh_ref, k - left, tm))
        dz = dz_buf[...]
        av = uc_ref[:, 0:d].astype(F32)
        sg = _sigmoid(uc_ref[:, d:2 * d].astype(F32))
        duf = jnp.concatenate([dz * sg, dz * av * sg * (1.0 - sg)], axis=1)
        db1_ref[0:1, :] += _colsum(duf)
        du = duf.astype(BF16)
        xv = x_ref[...]
        g, sh, sc = vec_ref[0:1, :], vec_ref[1:2, :], vec_ref[2:3, :]
        h = _norm_mod(xv, g, sh, sc).astype(BF16)
        dh = _in_proj_bwd(h, du, w_ref, dw_ref)
        dxn, dsh, dsc, dg = _norm_mod_bwd(dh, xv, g, sc)
        acc_ref[0:1, :] += dsh
        acc_ref[1:2, :] += dsc
        acc_ref[2:3, :] += dg
        dxin_ref[...] = dx_ref[...] + dxn
        _emit_bf16_at_end(nsteps, dw_ref, dw_out)

    ext = pltpu.VMEM((tm + 2 * HALO, d), F32)
    return _call(
        body, name="cf_bwd_in", nsteps=nsteps,
        in_specs=[*_halo_specs(tm, d, t_total), *_halo_specs(tm, nu, t_total), _rows(tm, d), _rows(tm, d),
                  _full(vec.shape), _full(wdw.shape), _VM],
        out_specs=[_rows(tm, d), _VM, _VM, _VM, _VM],
        out_shape=[jax.ShapeDtypeStruct((t_total, d), F32), jax.ShapeDtypeStruct(wpw1.shape, BF16),
                   jax.ShapeDtypeStruct((32, d), F32), jax.ShapeDtypeStruct((8, nu), F32),
                   jax.ShapeDtypeStruct((8, d), F32)],
        scratch_shapes=[ext, ext, _shift_scratch(tm), pltpu.VMEM((tm, d), F32), pltpu.VMEM(wpw1.shape, F32)],
        args=(dzc, dzc, dzc, u, u, u, x, dx, vec, wdw, wpw1), exch=exch)


def _mod_fwd(c_all, w_mod, b_cols):
    nl, d, ncol = w_mod.shape
    nb = c_all.shape[0]

    def body(c_ref, w_ref, b_ref, o_ref):
        cv = c_ref[...]
        ca = cv * _sigmoid(cv)
        o_ref[0] = jnp.dot(ca, w_ref[0], preferred_element_type=F32, precision=HIGHEST) + b_ref[0]

    return _pcall(
        body, name="mod_fwd", grid=(nl,),
        in_specs=[_full(c_all.shape), pl.BlockSpec((1, d, ncol), lambda l: (l, 0, 0)),
                  pl.BlockSpec((1, 1, ncol), lambda l: (l, 0, 0))],
        out_specs=pl.BlockSpec((1, nb, ncol), lambda l: (l, 0, 0)),
        out_shape=jax.ShapeDtypeStruct((nl, nb, ncol), F32),
        compiler_params=_seq_params(),
    )(c_all, w_mod, b_cols.reshape(nl, 1, ncol))


def _mod_bwd(c_all_t, dmod_cols):
    d, nb = c_all_t.shape
    nl, _, ncol = dmod_cols.shape

    def body(c_ref, dm_ref, o_ref):
        cv = c_ref[...]
        ca = cv * _sigmoid(cv)
        o_ref[0] = jnp.dot(ca, dm_ref[0], preferred_element_type=F32, precision=HIGHEST)

    return _pcall(
        body, name="mod_bwd", grid=(nl,),
        in_specs=[_full(c_all_t.shape), pl.BlockSpec((1, nb, ncol), lambda l: (l, 0, 0))],
        out_specs=pl.BlockSpec((1, d, ncol), lambda l: (l, 0, 0)),
        out_shape=jax.ShapeDtypeStruct((nl, d, ncol), F32),
        compiler_params=_seq_params(),
    )(c_all_t, dmod_cols)


def _row_block(r, c):
    if r * c <= EW_BLOCK_ELEMS:
        return r
    best = None
    for br in range(8, r, 8):
        if r % br == 0 and br * c <= EW_BLOCK_ELEMS:
            best = br
    assert best is not None, (r, c)
    return best


def _as2d(a):
    return a.reshape(-1, a.shape[-1])


def _adamw(w, gparts, m, v):
    shape = w.shape
    w2, m2, v2 = _as2d(w), _as2d(m), _as2d(v)
    g2 = [_as2d(g) for g in gparts]
    r, c = w2.shape
    br = _row_block(r, c)
    ng = len(g2)

    def body(*refs):
        w_ref, m_ref, v_ref = refs[0:3]
        g_refs = refs[3:3 + ng]
        g = g_refs[0][...]
        for gr in g_refs[1:]:
            g = g + gr[...]
        _adamw_update(g, w_ref[...], m_ref[...], v_ref[...], refs[3 + ng:])

    spec = pl.BlockSpec((br, c), lambda i: (i, 0))
    outs = _pcall(
        body, name="adamw", grid=(r // br,),
        in_specs=[spec] * (3 + ng), out_specs=[spec] * 4,
        out_shape=[jax.ShapeDtypeStruct((r, c), F32)] * 4,
        compiler_params=_seq_params(),
    )(w2, m2, v2, *g2)
    return tuple(o.reshape(shape) for o in outs)


def _adamw_update(g, w, m, v, out_refs):
    go_ref, d_ref, mo_ref, vo_ref = out_refs
    mn = ADAM_B1 * m + (1.0 - ADAM_B1) * g
    vn = ADAM_B2 * v + (1.0 - ADAM_B2) * (g * g)
    m_hat = mn / (1.0 - ADAM_B1 ** ADAM_STEP)
    v_hat = vn / (1.0 - ADAM_B2 ** ADAM_STEP)
    go_ref[...] = g.reshape(go_ref.shape)
    d_ref[...] = (-ADAM_LR * (m_hat / (jnp.sqrt(v_hat) + ADAM_EPS) + ADAM_WD * w)).reshape(d_ref.shape)
    mo_ref[...] = mn.reshape(mo_ref.shape)
    vo_ref[...] = vn.reshape(vo_ref.shape)


def _adamw_partials(w, partials, m, v):
    nl, a, b = w.shape
    br = _row_block(a, b)
    nb = a // br

    def body(*refs):
        w_ref, m_ref, v_ref = refs[0:3]
        p_refs = refs[3:3 + nl]
        out_refs = refs[3 + nl:]
        for layer in range(nl):
            @pl.when(pl.program_id(0) == layer)
            def _(layer=layer):
                halves = []
                for core in range(2):
                    acc = p_refs[layer][core, 0].astype(F32)
                    for chip in range(1, N_CHIPS):
                        acc = acc + p_refs[layer][core, chip].astype(F32)
                    halves.append(acc)
                _adamw_update(halves[0] + halves[1], w_ref[...], m_ref[...], v_ref[...], out_refs)

    def part_spec(layer):
        def index(l, i):
            return 0, 0, jnp.where(l == layer, i, jnp.where(l < layer, 0, nb - 1)), 0
        return pl.BlockSpec((2, N_CHIPS, br, b), index)

    spec = pl.BlockSpec((br, b), lambda l, i: (l * nb + i, 0))
    outs = _pcall(
        body, name="adamw_partials", grid=(nl, nb),
        in_specs=[spec] * 3 + [part_spec(layer) for layer in range(nl)], out_specs=[spec] * 4,
        out_shape=[jax.ShapeDtypeStruct((nl * a, b), F32)] * 4,
        compiler_params=pltpu.CompilerParams(dimension_semantics=("arbitrary", "arbitrary"),
                                             vmem_limit_bytes=VMEM_LIMIT),
    )(_as2d(w), _as2d(m), _as2d(v), *partials)
    return tuple(o.reshape(w.shape) for o in outs)


def _allgather8(block, with_sum, exch=None):
    m_per, n = block.shape
    nx = 0 if exch is None else len(exch.arrs)
    nvm = 2 if with_sum else 1

    def body(x_ref, *rest):
        xin, out_ref = rest[:nx], rest[nx]
        sum_ref = rest[nx + 1] if with_sum else None
        xout = rest[nx + nvm:2 * nx + nvm]
        send_sems, recv_sems, local_sem = rest[2 * nx + nvm:2 * nx + nvm + 3]
        xsems = rest[2 * nx + nvm + 3:]
        if exch is not None:
            exch.start(xin, xout, xsems)
        x, y, c = _place()
        me, sibling = (x, y, c), (x, y, 1 - c)
        chips = [(1 - x, y), (x, 1 - y), (1 - x, 1 - y)]

        def rows(px, py, pc):
            return out_ref.at[pl.ds((4 * px + 2 * py + pc) * m_per, m_per), :]

        def copy(k, blk, to, src=None):
            return pltpu.make_async_remote_copy(
                src_ref=rows(*blk) if src is None else src, dst_ref=rows(*blk),
                send_sem=send_sems.at[k], recv_sem=recv_sems.at[k], device_id=to, device_id_type=MESH)

        mine = pltpu.make_async_copy(x_ref, rows(*me), local_sem)
        mine.start()
        first = [copy(0, me, sibling, src=x_ref)]
        first += [copy(1 + j, me, (*chip, c), src=x_ref) for j, chip in enumerate(chips)]
        for cp in first:
            cp.start()
        passed = [copy(4 + j, (*chip, c), sibling) for j, chip in enumerate(chips)]
        for j, chip in enumerate(chips):
            copy(1 + j, (*chip, c), me).wait_recv()
            passed[j].start()
        copy(0, sibling, me).wait_recv()
        for j, chip in enumerate(chips):
            copy(4 + j, (*chip, 1 - c), me).wait_recv()
        for cp in first + passed:
            cp.wait_send()
        mine.wait()
        if exch is not None:
            exch.mid(xin, xout, xsems)
            exch.wait(xin, xout, xsems)
        if with_sum:
            acc = out_ref[0:m_per, :]
            for k in range(1, N_DEV):
                acc = acc + out_ref[k * m_per:(k + 1) * m_per, :]
            sum_ref[...] = acc

    out_shape = [jax.ShapeDtypeStruct((N_DEV * m_per, n), F32)]
    out_specs = [_VM]
    if with_sum:
        out_shape.append(jax.ShapeDtypeStruct((m_per, n), F32))
        out_specs.append(_VM)
    res = _pcall(
        body, name=("allgather8_sum" if with_sum else "allgather8") + ("" if exch is None else "_" + exch.tag),
        in_specs=[_VM] + [_ANY] * nx, out_specs=out_specs + [_ANY] * nx,
        out_shape=out_shape + ([] if exch is None else exch.out_shapes()),
        scratch_shapes=[pltpu.SemaphoreType.DMA((7,)), pltpu.SemaphoreType.DMA((7,)), pltpu.SemaphoreType.DMA]
        + ([] if exch is None else exch.sems()),
        compiler_params=pltpu.CompilerParams(vmem_limit_bytes=VMEM_LIMIT),
    )(block, *([] if exch is None else exch.arrs))
    return list(res[:nvm]), list(res[nvm:])


def _my_cols(full, chip):
    w = full.shape[-1] // N_CHIPS
    return lax.dynamic_slice_in_dim(full, chip * w, w, axis=full.ndim - 1)


def _pad_rows(a, rows):
    return jnp.pad(a, ((0, rows - a.shape[0]), (0, 0)))


def _to_lanes(a):
    flat = a.reshape(-1)
    n = -(-flat.shape[0] // (8 * LANES)) * (8 * LANES)
    return jnp.pad(flat, (0, n - flat.shape[0])).reshape(-1, LANES)


class _Packer:
    def __init__(self):
        self.items = []
        self.rows = 0

    def add(self, name, a):
        lanes = _to_lanes(a)
        self.items.append((name, self.rows, a.shape, lanes))
        self.rows += lanes.shape[0]

    def pack(self):
        total = -(-self.rows // 8) * 8
        return _pad_rows(jnp.concatenate([it[3] for it in self.items], axis=0), total)

    def unpack(self, buf):
        out = {}
        for name, row, shape, lanes in self.items:
            size = 1
            for s in shape:
                size *= s
            out[name] = buf[row:row + lanes.shape[0]].reshape(-1)[:size].reshape(shape)
        return out


TM_SEQ = 512
TM_FFN = 256
TM_FFN_FWD = 512


LAYER_KEYS = ("in", "out", "gate", "up", "down")
BLOCKED_KEYS = ("in",)
TRANSPOSED = ("ffn_w_gate", "ffn_w_up")


def _layer_big_names(layer):
    i = layer // 2
    mix = (("ab_w_in", i), ("ab_w_out", i)) if layer % 2 == 0 else (("cf_w_pw1", i), ("cf_w_pw2", i))
    return dict(zip(LAYER_KEYS, mix + (("ffn_w_gate", layer), ("ffn_w_up", layer), ("ffn_w_down", layer))))


def _unpack_weight(key, g):
    g = g.reshape(N_CHIPS, -1, g.shape[-1])
    return g if key in BLOCKED_KEYS else g.reshape(-1, g.shape[-1])


def _chunk_grad(key, dw):
    parts = dw if key in BLOCKED_KEYS else dw.reshape(N_CHIPS, -1, dw.shape[-1])
    return parts.astype(BF16)


def _local_step(x, target, mods, p, shards, first):
    t_total, d = x.shape
    depth = mods.shape[0]
    tm = min(TM_SEQ, t_total)
    tmf = min(TM_FFN, t_total)
    saved = []
    xin = x
    weights = [{} for _ in range(depth)]

    def carried(stage, layer):
        if layer == 0:
            return {"in": (0, ("out", "gate")), "mix": (0, ("up", "down")), "ffn": (1, ("in", "out", "gate", "up"))}[stage]
        return {"in": (layer, ("down",)), "mix": (layer + 1, ("in", "out")), "ffn": (layer + 1, ("gate", "up"))}[stage]

    def gather(stage, layer):
        of, keys = carried(stage, layer)
        if of >= depth:
            return None
        return _Gather([shards[of][k].reshape(2, -1, shards[of][k].shape[-1]) for k in keys])

    def keep(stage, layer, arrs):
        of, keys = carried(stage, layer)
        for k, g in zip(keys, arrs):
            weights[of][k] = _unpack_weight(k, g)

    for k, g in first.items():
        weights[0][k] = _unpack_weight(k, g)
    for layer in range(depth):
        i = layer // 2
        lw = weights[layer]
        sh1, sc1, g1, sh2, sc2, g2 = (mods[layer, k:k + 1] for k in range(6))
        vec_in = jnp.concatenate([p["norm_mix_g"][layer:layer + 1], sh1, sc1], axis=0)
        bias = None if layer % 2 == 0 else p["cf_b_pw1"][i:i + 1]
        (u,), arrived = _in_proj(xin, vec_in, lw["in"], bias, tm, exch=gather("in", layer))
        keep("in", layer, arrived)
        if layer % 2 == 0:
            (y, x2), arrived = _ab_fwd(u, xin, g1, p["ab_conv"][i], p["ab_w_pool"][i].astype(BF16),
                                       p["ab_pool_scale"][i:i + 1], lw["out"], tm, exch=gather("mix", layer))
            zc = None
        else:
            vec_cf = jnp.concatenate([g1, p["cf_b_dw"][i:i + 1], p["cf_ln_g"][i:i + 1], p["cf_ln_b"][i:i + 1],
                                      p["cf_b_pw2"][i:i + 1]], axis=0)
            (zc, y, x2), arrived = _cf_fwd(u, xin, vec_cf, _pad_rows(p["cf_w_dw"][i], 32), lw["out"], tm,
                                           exch=gather("mix", layer))
        keep("mix", layer, arrived)
        vec_ffn = jnp.concatenate([p["norm_ffn_g"][layer:layer + 1], sh2, sc2, g2], axis=0)
        (a, b, fout, x3), arrived = _ffn_fwd(x2, vec_ffn, lw["gate"], lw["up"], lw["down"], min(TM_FFN_FWD, t_total),
                                             exch=gather("ffn", layer))
        keep("ffn", layer, arrived)
        saved.append((xin, u, y, zc, x2, a, b, fout))
        xin = x3

    (dx, fin), _ = _final_fwd_bwd(xin, target, p["final_norm_g"].reshape(1, d), tm)
    grads = {"final_norm_g": fin[0], "loss": fin[1, 0:1]}
    per_layer = {k: [None] * depth for k in ("norm_mix_g", "norm_ffn_g")}
    half = {k: [None] * (depth // 2) for k in (
        "ab_conv", "ab_w_pool", "ab_pool_scale", "cf_b_pw1", "cf_w_dw", "cf_b_dw", "cf_ln_g", "cf_ln_b", "cf_b_pw2")}
    dmods = [None] * depth
    received = {}
    pending = None
    for layer in reversed(range(depth)):
        i = layer // 2
        lw = weights[layer]
        xin, u, y, zc, x2, a, b, fout = saved[layer]
        sh1, sc1, g1, sh2, sc2, g2 = (mods[layer, k:k + 1] for k in range(6))
        above = _Scatter([pending]) if pending is not None else None
        (da, db, dwd, acc_d), arrived = _ffn_bwd_down(dx, fout, a, b, g2, lw["down"], tmf, exch=above)
        if pending is not None:
            received[(layer + 1, "in")] = arrived[0]
        vec_ffn = jnp.concatenate([p["norm_ffn_g"][layer:layer + 1], sh2, sc2], axis=0)
        (dx2, dwg, dwu, acc_u), arrived = _ffn_bwd_up(da, db, x2, dx, vec_ffn, lw["gate"], lw["up"], tmf,
                                                      exch=_Scatter([_chunk_grad("down", dwd)]))
        received[(layer, "down")] = arrived[0]
        per_layer["norm_ffn_g"][layer] = acc_u[2]
        vec_in = jnp.concatenate([p["norm_mix_g"][layer:layer + 1], sh1, sc1], axis=0)
        send_gate = _Scatter([_chunk_grad("gate", dwg)])
        if layer % 2 == 0:
            (dpre, dwout, dwpool, acc_o), arrived = _ab_bwd_out(
                dx2, y, u, g1, p["ab_conv"][i], p["ab_w_pool"][i].astype(BF16), p["ab_pool_scale"][i:i + 1],
                lw["out"], tm, exch=send_gate)
            received[(layer, "gate")] = arrived[0]
            send_up_out = _Scatter([_chunk_grad("up", dwu), _chunk_grad("out", dwout)])
            (dx, dwin, dconv, acc_i), arrived = _ab_bwd_in(dpre, u, xin, dx2, vec_in, p["ab_conv"][i], lw["in"], tm,
                                                           exch=send_up_out)
            half["ab_w_pool"][i] = dwpool
            half["ab_pool_scale"][i] = acc_o[1, 0:d // 2]
            half["ab_conv"][i] = dconv[0:3]
        else:
            vec_cf = jnp.concatenate([g1, p["cf_ln_g"][i:i + 1], p["cf_ln_b"][i:i + 1]], axis=0)
            (dzc, dwout, acc_o), arrived = _cf_bwd_out(dx2, y, zc, vec_cf, lw["out"], tm, exch=send_gate)
            received[(layer, "gate")] = arrived[0]
            send_up_out = _Scatter([_chunk_grad("up", dwu), _chunk_grad("out", dwout)])
            (dx, dwin, dwdw, db1, acc_i), arrived = _cf_bwd_in(
                dzc, u, xin, dx2, vec_in, _pad_rows(p["cf_w_dw"][i], 32), lw["in"], tm, exch=send_up_out)
            half["cf_b_pw2"][i] = acc_o[1]
            half["cf_ln_g"][i] = acc_o[2]
            half["cf_ln_b"][i] = acc_o[3]
            half["cf_b_dw"][i] = acc_o[4]
            half["cf_w_dw"][i] = dwdw[0:CONF_KERNEL]
            half["cf_b_pw1"][i] = db1[0]
        received[(layer, "up")], received[(layer, "out")] = arrived
        per_layer["norm_mix_g"][layer] = acc_i[2]
        dmods[layer] = jnp.stack([acc_i[0], acc_i[1], acc_o[0], acc_u[0], acc_u[1], acc_d[0]], axis=0)
        pending = _chunk_grad("in", dwin)
    for k, v in {**per_layer, **half}.items():
        grads[k] = jnp.stack(v, axis=0)
    return dx, grads, jnp.stack(dmods, axis=0), received, pending


SMALL_COLS = ("ab_conv", "cf_b_pw1", "cf_w_dw", "cf_b_dw", "cf_ln_g", "cf_ln_b", "cf_b_pw2")
SMALL_REPL = ("norm_mix_g", "norm_ffn_g", "ab_w_pool", "ab_pool_scale", "final_norm_g")
WEIGHTS = ("norm_mix_g", "norm_ffn_g", "w_mod", "b_mod", "ab_w_in", "ab_conv", "ab_w_pool", "ab_pool_scale",
           "ab_w_out", "cf_w_pw1", "cf_b_pw1", "cf_w_dw", "cf_b_dw", "cf_ln_g", "cf_ln_b", "cf_w_pw2",
           "cf_b_pw2", "ffn_w_gate", "ffn_w_up", "ffn_w_down", "final_norm_g")


def kernel(x, c, norm_mix_g, norm_ffn_g, w_mod, b_mod, ab_w_in, ab_conv, ab_w_pool, ab_pool_scale, ab_w_out, cf_w_pw1, cf_b_pw1, cf_w_dw, cf_b_dw, cf_ln_g, cf_ln_b, cf_w_pw2, cf_b_pw2, ffn_w_gate, ffn_w_up, ffn_w_down, final_norm_g, loss_target, m_norm_mix_g, m_norm_ffn_g, m_w_mod, m_b_mod, m_ab_w_in, m_ab_conv, m_ab_w_pool, m_ab_pool_scale, m_ab_w_out, m_cf_w_pw1, m_cf_b_pw1, m_cf_w_dw, m_cf_b_dw, m_cf_ln_g, m_cf_ln_b, m_cf_w_pw2, m_cf_b_pw2, m_ffn_w_gate, m_ffn_w_up, m_ffn_w_down, m_final_norm_g, v_norm_mix_g, v_norm_ffn_g, v_w_mod, v_b_mod, v_ab_w_in, v_ab_conv, v_ab_w_pool, v_ab_pool_scale, v_ab_w_out, v_cf_w_pw1, v_cf_b_pw1, v_cf_w_dw, v_cf_b_dw, v_cf_ln_g, v_cf_ln_b, v_cf_w_pw2, v_cf_b_pw2, v_ffn_w_gate, v_ffn_w_up, v_ffn_w_down, v_final_norm_g):
    w = dict(norm_mix_g=norm_mix_g, norm_ffn_g=norm_ffn_g, w_mod=w_mod, b_mod=b_mod, ab_w_in=ab_w_in,
             ab_conv=ab_conv, ab_w_pool=ab_w_pool, ab_pool_scale=ab_pool_scale, ab_w_out=ab_w_out,
             cf_w_pw1=cf_w_pw1, cf_b_pw1=cf_b_pw1, cf_w_dw=cf_w_dw, cf_b_dw=cf_b_dw, cf_ln_g=cf_ln_g,
             cf_ln_b=cf_ln_b, cf_w_pw2=cf_w_pw2, cf_b_pw2=cf_b_pw2, ffn_w_gate=ffn_w_gate, ffn_w_up=ffn_w_up,
             ffn_w_down=ffn_w_down, final_norm_g=final_norm_g)
    mom = dict(norm_mix_g=m_norm_mix_g, norm_ffn_g=m_norm_ffn_g, w_mod=m_w_mod, b_mod=m_b_mod, ab_w_in=m_ab_w_in,
               ab_conv=m_ab_conv, ab_w_pool=m_ab_w_pool, ab_pool_scale=m_ab_pool_scale, ab_w_out=m_ab_w_out,
               cf_w_pw1=m_cf_w_pw1, cf_b_pw1=m_cf_b_pw1, cf_w_dw=m_cf_w_dw, cf_b_dw=m_cf_b_dw, cf_ln_g=m_cf_ln_g,
               cf_ln_b=m_cf_ln_b, cf_w_pw2=m_cf_w_pw2, cf_b_pw2=m_cf_b_pw2, ffn_w_gate=m_ffn_w_gate,
               ffn_w_up=m_ffn_w_up, ffn_w_down=m_ffn_w_down, final_norm_g=m_final_norm_g)
    var = dict(norm_mix_g=v_norm_mix_g, norm_ffn_g=v_norm_ffn_g, w_mod=v_w_mod, b_mod=v_b_mod, ab_w_in=v_ab_w_in,
               ab_conv=v_ab_conv, ab_w_pool=v_ab_w_pool, ab_pool_scale=v_ab_pool_scale, ab_w_out=v_ab_w_out,
               cf_w_pw1=v_cf_w_pw1, cf_b_pw1=v_cf_b_pw1, cf_w_dw=v_cf_w_dw, cf_b_dw=v_cf_b_dw, cf_ln_g=v_cf_ln_g,
               cf_ln_b=v_cf_ln_b, cf_w_pw2=v_cf_w_pw2, cf_b_pw2=v_cf_b_pw2, ffn_w_gate=v_ffn_w_gate,
               ffn_w_up=v_ffn_w_up, ffn_w_down=v_ffn_w_down, final_norm_g=v_final_norm_g)
    px, py, pc = _place()
    chip = 2 * px + py
    dev = 2 * chip + pc
    depth, d, mod_cols = w_mod.shape
    x = x[0]
    target = loss_target[0]

    def rows_major(name, t):
        return jnp.swapaxes(t, 1, 2) if name in TRANSPOSED else t

    shards = [{k: rows_major(name, w[name])[idx].astype(BF16) for k, (name, idx) in _layer_big_names(layer).items()}
              for layer in range(depth)]

    small_in = _Packer()
    small_in.add("c", c)
    for name in SMALL_COLS:
        small_in.add(name, w[name])
    def first_gather(*keys):
        return _Gather([shards[0][k].reshape(2, -1, shards[0][k].shape[-1]) for k in keys])

    first = {}
    (gathered,), (first["in"],) = _allgather8(small_in.pack(), with_sum=False, exch=first_gather("in"))
    gathered = gathered.reshape(N_DEV, -1, LANES)
    per_dev = [small_in.unpack(gathered[k]) for k in range(N_DEV)]
    c_all = jnp.concatenate([pd["c"] for pd in per_dev], axis=0)
    params = {name: jnp.concatenate([per_dev[2 * k][name] for k in range(N_CHIPS)], axis=-1)
              for name in SMALL_COLS}
    for name in SMALL_REPL:
        params[name] = w[name]

    mod_part = _mod_fwd(c_all, w_mod, _my_cols(b_mod, chip))
    (mod_all,), _ = _allgather8(mod_part.reshape(-1, LANES), with_sum=False)
    mod_all = mod_all.reshape(N_CHIPS, 2, depth, N_DEV, mod_cols)[:, 0]
    mod_all = jnp.moveaxis(mod_all, 0, 2).reshape(depth, N_DEV, N_CHIPS * mod_cols)
    mods = lax.dynamic_index_in_dim(mod_all, dev, axis=1, keepdims=False).reshape(depth, 6, d)

    grad_x, grads, dmods, received, last_chunk = _local_step(x, target, mods, params, shards, first)

    small_out = _Packer()
    small_out.add("dmods", dmods)
    for name in ("loss",) + SMALL_REPL + SMALL_COLS:
        small_out.add(name, grads[name])
    (parts_all, parts_sum), (received[(0, "in")],) = _allgather8(small_out.pack(), with_sum=True,
                                                                 exch=_Scatter([last_chunk]))
    small_sum = small_out.unpack(parts_sum)
    loss = small_sum["loss"][0]
    dmods_all = jnp.stack([small_out.unpack(pa)["dmods"] for pa in parts_all.reshape(N_DEV, -1, LANES)], axis=1)
    dmods_all = dmods_all.reshape(depth, N_DEV, 6 * d)

    g_final = {}
    g_final["w_mod"] = [_mod_bwd(c_all.T, _my_cols(dmods_all, chip))]
    g_final["b_mod"] = [small_sum["dmods"].reshape(depth, 6 * d)]
    for name in SMALL_REPL:
        g_final[name] = [small_sum[name]]
    for name in SMALL_COLS:
        g_final[name] = [_my_cols(small_sum[name], chip)]

    updates = {}
    for name in WEIGHTS:
        parts = [received[(layer, k)] for layer in range(depth)
                 for k, (other, _) in _layer_big_names(layer).items() if other == name]
        if parts:
            outs = _adamw_partials(rows_major(name, w[name]), parts, rows_major(name, mom[name]),
                                   rows_major(name, var[name]))
            updates[name] = [rows_major(name, o) for o in outs]
        else:
            updates[name] = _adamw(w[name], g_final[name], mom[name], var[name])
    return (loss, grad_x[None], *[updates[name][0] for name in WEIGHTS], *[updates[name][1] for name in WEIGHTS],
            *[updates[name][2] for name in WEIGHTS], *[updates[name][3] for name in WEIGHTS])
```

```python
import functools

import jax
import jax.numpy as jnp
from jax import lax
from jax.experimental import pallas as pl
from jax.experimental.pallas import tpu as pltpu

F32 = jnp.float32
BF16 = jnp.bfloat16
RMS_EPS = 1e-6
LN_EPS = 1e-5
ADAM_LR = 0.001
ADAM_B1 = 0.9
ADAM_B2 = 0.999
ADAM_EPS = 1e-08
ADAM_WD = 0.01
ADAM_STEP = 10
POOL_WINDOWS = (2, 4, 8, 16)
CONF_KERNEL = 31
N_CHIPS = 4
N_DEV = 8
HALO = 16
CONV_COLS = 256
FFN_CHUNK = 1536
LANES = 1024
VMEM_LIMIT = 56 * 1024 * 1024
VMEM_LIMIT_WIDE = 60 * 1024 * 1024
EW_BLOCK_ELEMS = 256 * 1024
MESH = pl.DeviceIdType.MESH
HIGHEST = lax.Precision.HIGHEST

_pcall = pl.pallas_call


def _dot(a, b):
    return jnp.dot(a, b, preferred_element_type=F32)


def _dot_tn(a, b):
    return lax.dot_general(a, b, (((0,), (0,)), ((), ())), preferred_element_type=F32)


def _dot_nt(a, b):
    return lax.dot_general(a, b, (((1,), (1,)), ((), ())), preferred_element_type=F32)


def _colsum(v):
    return jnp.sum(v, axis=0, keepdims=True)


def _sigmoid(v):
    return 1.0 / (1.0 + jnp.exp(-v))


def _rows(tm, c):
    return pl.BlockSpec((tm, c), lambda i: (i, 0))


def _full(shape):
    nd = len(shape)
    return pl.BlockSpec(shape, lambda i: (0,) * nd)


_VM = pl.BlockSpec(memory_space=pltpu.VMEM)
_ANY = pl.BlockSpec(memory_space=pl.ANY)


def _halo_specs(tm, c, t_total):
    r = tm // HALO
    last = t_total // HALO - 1
    prev = pl.BlockSpec((HALO, c), lambda i: (jnp.maximum(i * r - 1, 0), 0))
    nxt = pl.BlockSpec((HALO, c), lambda i: (jnp.minimum((i + 1) * r, last), 0))
    return prev, _rows(tm, c), nxt


def _seq_params(vmem_limit=VMEM_LIMIT):
    return pltpu.CompilerParams(dimension_semantics=("arbitrary",), vmem_limit_bytes=vmem_limit)


def _place():
    return lax.axis_index("x"), lax.axis_index("y"), lax.axis_index("c")


def _peer_chips(x, y):
    return [(1 - x, y), (x, 1 - y), (1 - x, 1 - y)]


class _Gather:
    tag = "gather"

    def __init__(self, arrs):
        self.arrs = list(arrs)

    def out_shapes(self):
        return [jax.ShapeDtypeStruct((N_CHIPS,) + a.shape, a.dtype) for a in self.arrs]

    def sems(self):
        n = len(self.arrs)
        return [pltpu.SemaphoreType.DMA((3 * n,)) for _ in range(4)] + [pltpu.SemaphoreType.DMA((n,))]

    def _copies(self, ins, outs, sems, kinds):
        ici_send, ici_recv, d2d_send, d2d_recv, local_sems = sems
        x, y, c = _place()
        me = 2 * x + y
        found = {kind: [] for kind in kinds}
        for j in range(len(ins)):
            if "local" in kinds:
                found["local"].append(pltpu.make_async_copy(ins[j], outs[j].at[me], local_sems.at[j]))
            for k, (px, py) in enumerate(_peer_chips(x, y)):
                ici = dict(send_sem=ici_send.at[3 * j + k], recv_sem=ici_recv.at[3 * j + k],
                           device_id=(px, py, c), device_id_type=MESH)
                d2d = dict(send_sem=d2d_send.at[3 * j + k], recv_sem=d2d_recv.at[3 * j + k],
                           device_id=(x, y, 1 - c), device_id_type=MESH)
                theirs = outs[j].at[2 * px + py]
                if "send" in kinds:
                    found["send"].append(pltpu.make_async_remote_copy(
                        src_ref=ins[j].at[c], dst_ref=outs[j].at[me, c], **ici))
                if "arrival" in kinds:
                    found["arrival"].append(pltpu.make_async_remote_copy(
                        src_ref=ins[j].at[c], dst_ref=theirs.at[c], **ici))
                if "pass" in kinds:
                    found["pass"].append(pltpu.make_async_remote_copy(
                        src_ref=theirs.at[c], dst_ref=theirs.at[c], **d2d))
                if "passed" in kinds:
                    found["passed"].append(pltpu.make_async_remote_copy(
                        src_ref=theirs.at[c], dst_ref=theirs.at[1 - c], **d2d))
        return found

    def start(self, ins, outs, sems):
        found = self._copies(ins, outs, sems, ("local", "send"))
        for cp in found["local"] + found["send"]:
            cp.start()

    def mid(self, ins, outs, sems):
        found = self._copies(ins, outs, sems, ("arrival", "pass"))
        for arrived, onward in zip(found["arrival"], found["pass"]):
            arrived.wait_recv()
            onward.start()

    def wait(self, ins, outs, sems):
        found = self._copies(ins, outs, sems, ("local", "send", "pass", "passed"))
        for cp in found["passed"]:
            cp.wait_recv()
        for cp in found["send"] + found["pass"]:
            cp.wait_send()
        for cp in found["local"]:
            cp.wait()


class _Scatter:
    tag = "scatter"

    def __init__(self, arrs):
        self.arrs = list(arrs)

    def out_shapes(self):
        return [jax.ShapeDtypeStruct((2,) + a.shape, a.dtype) for a in self.arrs]

    def sems(self):
        n = len(self.arrs)
        dma = pltpu.SemaphoreType.DMA
        return [dma((3 * n,)), dma((3 * n,)), dma((4 * n,)), dma((4 * n,)), dma((n,))]

    def _copies(self, ins, outs, sems, kinds):
        ici_send, ici_recv, d2d_send, d2d_recv, local_sems = sems
        x, y, c = _place()
        me = 2 * x + y
        found = {kind: [] for kind in kinds}
        for j in range(len(ins)):
            def d2d(k):
                return dict(send_sem=d2d_send.at[4 * j + k], recv_sem=d2d_recv.at[4 * j + k],
                            device_id=(x, y, 1 - c), device_id_type=MESH)

            if "local" in kinds:
                found["local"].append(pltpu.make_async_copy(ins[j].at[me], outs[j].at[0, me], local_sems.at[j]))
            if "own" in kinds:
                found["own"].append(pltpu.make_async_remote_copy(
                    src_ref=ins[j].at[me], dst_ref=outs[j].at[1, me], **d2d(3)))
            if "passed" in kinds:
                found["passed"].append(pltpu.make_async_remote_copy(
                    src_ref=ins[j].at[me], dst_ref=outs[j].at[1, me], **d2d(3)))
            for k, (px, py) in enumerate(_peer_chips(x, y)):
                ici = dict(send_sem=ici_send.at[3 * j + k], recv_sem=ici_recv.at[3 * j + k],
                           device_id=(px, py, c), device_id_type=MESH)
                peer = 2 * px + py
                if "send" in kinds:
                    found["send"].append(pltpu.make_async_remote_copy(
                        src_ref=ins[j].at[peer], dst_ref=outs[j].at[0, me], **ici))
                if "arrival" in kinds:
                    found["arrival"].append(pltpu.make_async_remote_copy(
                        src_ref=ins[j].at[me], dst_ref=outs[j].at[0, peer], **ici))
                if "pass" in kinds:
                    found["pass"].append(pltpu.make_async_remote_copy(
                        src_ref=outs[j].at[0, peer], dst_ref=outs[j].at[1, peer], **d2d(k)))
                if "passed" in kinds:
                    found["passed"].append(pltpu.make_async_remote_copy(
                        src_ref=outs[j].at[0, peer], dst_ref=outs[j].at[1, peer], **d2d(k)))
        return found

    def start(self, ins, outs, sems):
        found = self._copies(ins, outs, sems, ("local", "own", "send"))
        for cp in found["local"] + found["own"] + found["send"]:
            cp.start()

    def mid(self, ins, outs, sems):
        found = self._copies(ins, outs, sems, ("arrival", "pass"))
        for arrived, onward in zip(found["arrival"], found["pass"]):
            arrived.wait_recv()
            onward.start()

    def wait(self, ins, outs, sems):
        found = self._copies(ins, outs, sems, ("local", "own", "send", "pass", "passed"))
        for cp in found["passed"]:
            cp.wait_recv()
        for cp in found["own"] + found["send"] + found["pass"]:
            cp.wait_send()
        for cp in found["local"]:
            cp.wait()


def _call(body, *, name, nsteps, in_specs, out_specs, out_shape, args, scratch_shapes=(), exch=None,
          vmem_limit=VMEM_LIMIT):
    if exch is None:
        outs = _pcall(body, name=name, grid=(nsteps,), in_specs=list(in_specs), out_specs=list(out_specs),
                      out_shape=list(out_shape), scratch_shapes=list(scratch_shapes),
                      compiler_params=_seq_params(vmem_limit))(*args)
        return list(outs), []
    n, ni, no, ns = len(exch.arrs), len(in_specs), len(out_specs), len(scratch_shapes)

    def hosted(*refs):
        xin = refs[ni:ni + n]
        xout = refs[ni + n + no:ni + 2 * n + no]
        scr = refs[ni + 2 * n + no:]

        @pl.when(pl.program_id(0) == 0)
        def _():
            exch.start(xin, xout, scr[ns:])

        body(*refs[:ni], *refs[ni + n:ni + n + no], *scr[:ns])

        @pl.when(pl.program_id(0) == max(nsteps - 3, 0))
        def _():
            exch.mid(xin, xout, scr[ns:])

        @pl.when(pl.program_id(0) == nsteps - 1)
        def _():
            exch.wait(xin, xout, scr[ns:])

    outs = _pcall(hosted, name=name + "_" + exch.tag, grid=(nsteps,),
                  in_specs=[*in_specs, *[_ANY] * n], out_specs=[*out_specs, *[_ANY] * n],
                  out_shape=[*out_shape, *exch.out_shapes()],
                  scratch_shapes=[*scratch_shapes, *exch.sems()],
                  compiler_params=_seq_params(vmem_limit))(*args, *exch.arrs)
    return list(outs[:no]), list(outs[no:])


def _rms(x):
    r = lax.rsqrt(jnp.mean(x * x, axis=-1, keepdims=True) + RMS_EPS)
    return x * r, r


def _norm_mod(x, g, sh, sc):
    xhat, _ = _rms(x)
    return xhat * g * (1.0 + sc) + sh


def _norm_mod_bwd(dh, x, g, sc):
    xhat, r = _rms(x)
    n = xhat * g
    dsh = _colsum(dh)
    dsc = _colsum(dh * n)
    dn = dh * (1.0 + sc)
    dg = _colsum(dn * xhat)
    dxn = dn * g
    dx = r * (dxn - xhat * jnp.mean(dxn * xhat, axis=-1, keepdims=True))
    return dx, dsh, dsc, dg


def _fill_ext(ext_ref, prev, cur, nxt, i, nsteps, tm):
    ext_ref[0:HALO, :] = jnp.where(i > 0, prev, 0.0)
    ext_ref[HALO:HALO + tm, :] = cur
    ext_ref[HALO + tm:HALO + tm + HALO, :] = jnp.where(i < nsteps - 1, nxt, 0.0)


def _shift_scratch(tm):
    return pltpu.VMEM((8, tm + 2 * HALO - 8, CONV_COLS), F32)


def _fill_shifts(sh_ref, ext_ref, lo, hi, tm):
    for b in range(8):
        sh_ref[b] = ext_ref[b:b + tm + 2 * HALO - 8, lo:hi]


def _shifted(sh_ref, offset, tm):
    b = offset % 8
    start = HALO + offset - b
    return sh_ref[b, start:start + tm, :]


def _window_count(t, wdw, t_total):
    left = wdw // 2
    right = wdw - 1 - left
    cnt = jnp.minimum(t + right, t_total - 1) - jnp.maximum(t - left, 0) + 1
    return jnp.maximum(cnt, 1).astype(F32)


def _in_proj(x, vec, w, bias, tm, exch=None):
    t_total, d = x.shape
    nk = w.shape[2]
    n = N_CHIPS * nk
    has_bias = bias is not None

    def body(*refs):
        if has_bias:
            x_ref, vec_ref, w_ref, b_ref, u_ref = refs
        else:
            x_ref, vec_ref, w_ref, u_ref = refs
        h = _norm_mod(x_ref[...], vec_ref[0:1, :], vec_ref[1:2, :], vec_ref[2:3, :])
        h = h.astype(BF16)
        for k in range(N_CHIPS):
            u = _dot(h, w_ref[k])
            if has_bias:
                u = u + b_ref[:, k * nk:(k + 1) * nk]
            u_ref[:, k * nk:(k + 1) * nk] = u.astype(BF16)

    in_specs = [_rows(tm, d), _full(vec.shape), _VM]
    args = [x, vec, w]
    if has_bias:
        in_specs.append(_full(bias.shape))
        args.append(bias)
    return _call(
        body, name="in_proj_bias" if has_bias else "in_proj", nsteps=t_total // tm,
        in_specs=in_specs, out_specs=[_rows(tm, n)], out_shape=[jax.ShapeDtypeStruct((t_total, n), BF16)],
        args=args, exch=exch)


def _in_proj_bwd(h, du, w_ref, dw_ref):
    nk = w_ref.shape[2]
    dh = None
    for k in range(N_CHIPS):
        duk = du[:, k * nk:(k + 1) * nk]
        dw_ref[k] += _dot_tn(h, duk)
        part = _dot_nt(duk, w_ref[k])
        dh = part if dh is None else dh + part
    return dh


def _ab_core(up_ref, uc_ref, un_ref, conv_ref, wpool_ref, q_ext, p_ext, i, nsteps, tm, t_total):
    da = uc_ref.shape[1] // 4

    def cols(ref, k):
        return ref[:, k * da:(k + 1) * da].astype(F32)

    _fill_ext(q_ext, cols(up_ref, 1) * cols(up_ref, 2), cols(uc_ref, 1) * cols(uc_ref, 2),
              cols(un_ref, 1) * cols(un_ref, 2), i, nsteps, tm)
    _fill_ext(p_ext, cols(up_ref, 3), cols(uc_ref, 3), cols(un_ref, 3), i, nsteps, tm)
    bg = cols(uc_ref, 0)
    cq = (conv_ref[0:1, :] * q_ext[HALO - 1:HALO - 1 + tm, :] + conv_ref[1:2, :] * q_ext[HALO:HALO + tm, :]
          + conv_ref[2:3, :] * q_ext[HALO + 1:HALO + 1 + tm, :])
    t = i * tm + lax.broadcasted_iota(jnp.int32, (tm, 1), 0)
    gw = da // len(POOL_WINDOWS)
    pooled, ybpre = [], []
    for g, wdw in enumerate(POOL_WINDOWS):
        left = wdw // 2
        right = wdw - 1 - left
        lo, hi = g * gw, (g + 1) * gw
        s = p_ext[HALO - left:HALO - left + tm, lo:hi]
        for o in range(-left + 1, right + 1):
            s = s + p_ext[HALO + o:HALO + o + tm, lo:hi]
        pg = s / _window_count(t, wdw, t_total) - p_ext[HALO:HALO + tm, lo:hi]
        pooled.append(pg.astype(BF16))
        ybpre.append(_dot(pooled[-1], wpool_ref[g]))
    return bg, cq, pooled, jnp.concatenate(ybpre, axis=1)


def _ab_fwd(u, x, vec, conv, wpool, scale, wout, tm, exch=None):
    t_total, d = x.shape
    nu = u.shape[1]
    da = nu // 4
    nsteps = t_total // tm

    def body(up_ref, uc_ref, un_ref, x_ref, vec_ref, conv_ref, wpool_ref, scale_ref, wout_ref,
             y_ref, x2_ref, q_ext, p_ext):
        i = pl.program_id(0)
        bg, cq, _, ybpre = _ab_core(up_ref, uc_ref, un_ref, conv_ref, wpool_ref, q_ext, p_ext,
                                    i, nsteps, tm, t_total)
        cat = jnp.concatenate([bg * cq, ybpre * scale_ref[...]], axis=1).astype(BF16)
        y = _dot(cat, wout_ref[...])
        y_ref[...] = y.astype(BF16)
        x2_ref[...] = x_ref[...] + vec_ref[0:1, :] * y

    return _call(
        body, name="ab_fwd", nsteps=nsteps,
        in_specs=[*_halo_specs(tm, nu, t_total), _rows(tm, d), _full(vec.shape), _full(conv.shape),
                  _full(wpool.shape), _full(scale.shape), _VM],
        out_specs=[_rows(tm, d), _rows(tm, d)],
        out_shape=[jax.ShapeDtypeStruct((t_total, d), BF16), jax.ShapeDtypeStruct((t_total, d), F32)],
        scratch_shapes=[pltpu.VMEM((tm + 2 * HALO, da), F32), pltpu.VMEM((tm + 2 * HALO, da), F32)],
        args=(u, u, u, x, vec, conv, wpool, scale, wout), exch=exch)


def _glu_ext(up_ref, uc_ref, un_ref, z_ext, i, nsteps, tm):
    dz = uc_ref.shape[1] // 2

    def glu(ref):
        return ref[:, 0:dz].astype(F32) * _sigmoid(ref[:, dz:2 * dz].astype(F32))

    _fill_ext(z_ext, glu(up_ref), glu(uc_ref), glu(un_ref), i, nsteps, tm)


def _layer_norm_stats(zc):
    mu = jnp.mean(zc, axis=-1, keepdims=True)
    dlt = zc - mu
    rstd = lax.rsqrt(jnp.mean(dlt * dlt, axis=-1, keepdims=True) + LN_EPS)
    return dlt * rstd, rstd


def _cf_fwd(u, x, vec, wdw, wpw2, tm, exch=None):
    t_total, d = x.shape
    nu = u.shape[1]
    nsteps = t_total // tm
    left = (CONF_KERNEL - 1) // 2

    def body(up_ref, uc_ref, un_ref, x_ref, vec_ref, wdw_ref, wpw2_ref, zc_ref, y_ref, x2_ref, z_ext, sh_ref,
             zc_buf):
        i = pl.program_id(0)
        _glu_ext(up_ref, uc_ref, un_ref, z_ext, i, nsteps, tm)
        for lo in range(0, d, CONV_COLS):
            hi = lo + CONV_COLS
            _fill_shifts(sh_ref, z_ext, lo, hi, tm)
            acc = wdw_ref[0:1, lo:hi] * _shifted(sh_ref, -left, tm)
            for k in range(1, CONF_KERNEL):
                acc = acc + wdw_ref[k:k + 1, lo:hi] * _shifted(sh_ref, k - left, tm)
            zc_buf[:, lo:hi] = acc
        zc = zc_buf[...] + vec_ref[1:2, :]
        zc_ref[...] = zc.astype(BF16)
        zn, _ = _layer_norm_stats(zc)
        zl = zn * vec_ref[2:3, :] + vec_ref[3:4, :]
        zs = zl * _sigmoid(zl)
        y = _dot(zs.astype(BF16), wpw2_ref[...]) + vec_ref[4:5, :]
        y_ref[...] = y.astype(BF16)
        x2_ref[...] = x_ref[...] + vec_ref[0:1, :] * y

    return _call(
        body, name="cf_fwd", nsteps=nsteps,
        in_specs=[*_halo_specs(tm, nu, t_total), _rows(tm, d), _full(vec.shape), _full(wdw.shape), _VM],
        out_specs=[_rows(tm, d), _rows(tm, d), _rows(tm, d)],
        out_shape=[jax.ShapeDtypeStruct((t_total, d), BF16), jax.ShapeDtypeStruct((t_total, d), BF16),
                   jax.ShapeDtypeStruct((t_total, d), F32)],
        scratch_shapes=[pltpu.VMEM((tm + 2 * HALO, d), F32), _shift_scratch(tm), pltpu.VMEM((tm, d), F32)],
        args=(u, u, u, x, vec, wdw, wpw2), exch=exch)


def _ffn_chunks(f, width=FFN_CHUNK):
    return [(lo, min(lo + width, f)) for lo in range(0, f, width)]


def _ffn_fwd(x2, vec, wg, wu, wd, tm, exch=None):
    t_total, d = x2.shape
    f = wg.shape[0]

    def body(x_ref, vec_ref, wg_ref, wu_ref, wd_ref, a_ref, b_ref, f_ref, x3_ref):
        xv = x_ref[...]
        h = _norm_mod(xv, vec_ref[0:1, :], vec_ref[1:2, :], vec_ref[2:3, :]).astype(BF16)
        y = None
        for lo, hi in _ffn_chunks(f):
            a = _dot_nt(h, wg_ref[lo:hi, :])
            b = _dot_nt(h, wu_ref[lo:hi, :])
            a_ref[:, lo:hi] = a.astype(BF16)
            b_ref[:, lo:hi] = b.astype(BF16)
            s = (a * _sigmoid(a) * b).astype(BF16)
            part = _dot(s, wd_ref[lo:hi, :])
            y = part if y is None else y + part
        f_ref[...] = y.astype(BF16)
        x3_ref[...] = xv + vec_ref[3:4, :] * y

    return _call(
        body, name="ffn_fwd", nsteps=t_total // tm,
        in_specs=[_rows(tm, d), _full(vec.shape), _VM, _VM, _VM],
        out_specs=[_rows(tm, f), _rows(tm, f), _rows(tm, d), _rows(tm, d)],
        out_shape=[jax.ShapeDtypeStruct((t_total, f), BF16), jax.ShapeDtypeStruct((t_total, f), BF16),
                   jax.ShapeDtypeStruct((t_total, d), BF16), jax.ShapeDtypeStruct((t_total, d), F32)],
        args=(x2, vec, wg, wu, wd), exch=exch)


def _final_fwd_bwd(x, target, vec, tm):
    t_total, d = x.shape

    def body(x_ref, t_ref, vec_ref, dx_ref, acc_ref):
        @pl.when(pl.program_id(0) == 0)
        def _():
            acc_ref[...] = jnp.zeros_like(acc_ref)

        g = vec_ref[0:1, :]
        xhat, r = _rms(x_ref[...])
        e = xhat * g - t_ref[...]
        acc_ref[1:2, :] += jnp.zeros((1, d), F32) + 0.5 * jnp.sum(jnp.mean(e * e, axis=-1, keepdims=True))
        dout = e * (1.0 / d)
        acc_ref[0:1, :] += _colsum(dout * xhat)
        dxn = dout * g
        dx_ref[...] = r * (dxn - xhat * jnp.mean(dxn * xhat, axis=-1, keepdims=True))

    return _call(
        body, name="final_fwd_bwd", nsteps=t_total // tm,
        in_specs=[_rows(tm, d), _rows(tm, d), _full(vec.shape)],
        out_specs=[_rows(tm, d), _VM],
        out_shape=[jax.ShapeDtypeStruct((t_total, d), F32), jax.ShapeDtypeStruct((8, d), F32)],
        args=(x, target, vec))


def _zero_at_start(*refs):
    @pl.when(pl.program_id(0) == 0)
    def _():
        for ref in refs:
            ref[...] = jnp.zeros_like(ref)


def _emit_bf16_at_end(nsteps, acc_ref, out_ref):
    @pl.when(pl.program_id(0) == nsteps - 1)
    def _():
        out_ref[...] = acc_ref[...].astype(BF16)


def _ffn_bwd_down(dx3, fout, a, b, vec, wd, tm, exch=None):
    t_total, d = dx3.shape
    f = a.shape[1]

    def body(dx_ref, f_ref, a_ref, b_ref, vec_ref, wd_ref, da_ref, db_ref, dwd_out, acc_ref, dwd_ref):
        _zero_at_start(dwd_ref, acc_ref)
        dx = dx_ref[...]
        acc_ref[0:1, :] += _colsum(dx * f_ref[...].astype(F32))
        dy = (dx * vec_ref[0:1, :]).astype(BF16)
        for lo, hi in _ffn_chunks(f, FFN_CHUNK // 3):
            av = a_ref[:, lo:hi].astype(F32)
            bv = b_ref[:, lo:hi].astype(F32)
            sg = _sigmoid(av)
            silu = av * sg
            ds = _dot_nt(dy, wd_ref[lo:hi, :])
            da_ref[:, lo:hi] = (ds * bv * (sg * (1.0 + av * (1.0 - sg)))).astype(BF16)
            db_ref[:, lo:hi] = (ds * silu).astype(BF16)
            dwd_ref[lo:hi, :] += _dot_tn((silu * bv).astype(BF16), dy)
        _emit_bf16_at_end(t_total // tm, dwd_ref, dwd_out)

    return _call(
        body, name="ffn_bwd_down", nsteps=t_total // tm,
        in_specs=[_rows(tm, d), _rows(tm, d), _rows(tm, f), _rows(tm, f), _full(vec.shape), _VM],
        out_specs=[_rows(tm, f), _rows(tm, f), _VM, _VM],
        out_shape=[jax.ShapeDtypeStruct((t_total, f), BF16), jax.ShapeDtypeStruct((t_total, f), BF16),
                   jax.ShapeDtypeStruct(wd.shape, BF16), jax.ShapeDtypeStruct((8, d), F32)],
        scratch_shapes=[pltpu.VMEM(wd.shape, F32)],
        args=(dx3, fout, a, b, vec, wd), exch=exch)


def _ffn_bwd_up(da, db, x2, dx3, vec, wg, wu, tm, exch=None):
    t_total, d = x2.shape
    f = da.shape[1]

    def body(da_ref, db_ref, x_ref, dx_ref, vec_ref, wg_ref, wu_ref, dx2_ref, dwg_out, dwu_out, acc_ref,
             dwg_ref, dwu_ref):
        _zero_at_start(dwg_ref, dwu_ref, acc_ref)
        xv = x_ref[...]
        g, sh, sc = vec_ref[0:1, :], vec_ref[1:2, :], vec_ref[2:3, :]
        h = _norm_mod(xv, g, sh, sc).astype(BF16)
        dav = da_ref[...]
        dbv = db_ref[...]
        dwg_ref[...] += _dot_tn(dav, h)
        dwu_ref[...] += _dot_tn(dbv, h)
        dh = _dot(dav, wg_ref[...]) + _dot(dbv, wu_ref[...])
        dxn, dsh, dsc, dg = _norm_mod_bwd(dh, xv, g, sc)
        acc_ref[0:1, :] += dsh
        acc_ref[1:2, :] += dsc
        acc_ref[2:3, :] += dg
        dx2_ref[...] = dx_ref[...] + dxn
        _emit_bf16_at_end(t_total // tm, dwg_ref, dwg_out)
        _emit_bf16_at_end(t_total // tm, dwu_ref, dwu_out)

    return _call(
        body, name="ffn_bwd_up", nsteps=t_total // tm,
        in_specs=[_rows(tm, f), _rows(tm, f), _rows(tm, d), _rows(tm, d), _full(vec.shape), _VM, _VM],
        out_specs=[_rows(tm, d), _VM, _VM, _VM],
        out_shape=[jax.ShapeDtypeStruct((t_total, d), F32), jax.ShapeDtypeStruct(wg.shape, BF16),
                   jax.ShapeDtypeStruct(wu.shape, BF16), jax.ShapeDtypeStruct((8, d), F32)],
        scratch_shapes=[pltpu.VMEM(wg.shape, F32), pltpu.VMEM(wu.shape, F32)],
        args=(da, db, x2, dx3, vec, wg, wu), exch=exch, vmem_limit=VMEM_LIMIT_WIDE)


def _ab_bwd_out(dx, y, u, vec, conv, wpool, scale, wout, tm, exch=None):
    t_total, d = dx.shape
    nu = u.shape[1]
    da = nu // 4
    gw = da // len(POOL_WINDOWS)
    nsteps = t_total // tm

    def body(dx_ref, y_ref, up_ref, uc_ref, un_ref, vec_ref, conv_ref, wpool_ref, scale_ref, wout_ref,
             dpre_ref, dwout_out, dwpool_ref, acc_ref, q_ext, p_ext, dwout_ref):
        _zero_at_start(dwout_ref, dwpool_ref, acc_ref)
        i = pl.program_id(0)
        dxv = dx_ref[...]
        acc_ref[0:1, :] += _colsum(dxv * y_ref[...].astype(F32))
        dy = (dxv * vec_ref[0:1, :]).astype(BF16)
        bg, cq, pooled, ybpre = _ab_core(up_ref, uc_ref, un_ref, conv_ref, wpool_ref, q_ext, p_ext,
                                         i, nsteps, tm, t_total)
        cat = jnp.concatenate([bg * cq, ybpre * scale_ref[...]], axis=1).astype(BF16)
        dwout_ref[...] += _dot_tn(cat, dy)
        dcat = _dot_nt(dy, wout_ref[...])
        dya = dcat[:, 0:da]
        dyb = dcat[:, da:2 * da]
        acc_ref[1:2, 0:da] += _colsum(dyb * ybpre)
        dybpre = (dyb * scale_ref[...]).astype(BF16)
        dpooled = []
        for g in range(len(POOL_WINDOWS)):
            dg = dybpre[:, g * gw:(g + 1) * gw]
            dwpool_ref[g] += _dot_tn(pooled[g], dg)
            dpooled.append(_dot_nt(dg, wpool_ref[g]))
        dpre_ref[...] = jnp.concatenate([dya * cq, dya * bg] + dpooled, axis=1).astype(BF16)
        _emit_bf16_at_end(nsteps, dwout_ref, dwout_out)

    return _call(
        body, name="ab_bwd_out", nsteps=nsteps,
        in_specs=[_rows(tm, d), _rows(tm, d), *_halo_specs(tm, nu, t_total), _full(vec.shape),
                  _full(conv.shape), _full(wpool.shape), _full(scale.shape), _VM],
        out_specs=[_rows(tm, 3 * da), _VM, _VM, _VM],
        out_shape=[jax.ShapeDtypeStruct((t_total, 3 * da), BF16), jax.ShapeDtypeStruct(wout.shape, BF16),
                   jax.ShapeDtypeStruct(wpool.shape, F32), jax.ShapeDtypeStruct((8, d), F32)],
        scratch_shapes=[pltpu.VMEM((tm + 2 * HALO, da), F32), pltpu.VMEM((tm + 2 * HALO, da), F32),
                        pltpu.VMEM(wout.shape, F32)],
        args=(dx, y, u, u, u, vec, conv, wpool, scale, wout), exch=exch)


def _ab_bwd_in(dpre, u, x, dx, vec, conv, win, tm, exch=None):
    t_total, d = x.shape
    nu = u.shape[1]
    da = nu // 4
    gw = da // len(POOL_WINDOWS)
    nsteps = t_total // tm

    def body(dp_ref, dc_ref, dn_ref, up_ref, uc_ref, un_ref, x_ref, dx_ref, vec_ref, conv_ref, win_ref,
             dxin_ref, dwin_out, dconv_ref, acc_ref, dcq_ext, q_ext, dpl_ext, dwin_ref):
        _zero_at_start(dwin_ref, dconv_ref, acc_ref)
        i = pl.program_id(0)

        def ucols(ref, k):
            return ref[:, k * da:(k + 1) * da].astype(F32)

        def dcols(ref, k):
            return ref[:, k * da:(k + 1) * da].astype(F32)

        _fill_ext(dcq_ext, dcols(dp_ref, 1), dcols(dc_ref, 1), dcols(dn_ref, 1), i, nsteps, tm)
        _fill_ext(q_ext, ucols(up_ref, 1) * ucols(up_ref, 2), ucols(uc_ref, 1) * ucols(uc_ref, 2),
                  ucols(un_ref, 1) * ucols(un_ref, 2), i, nsteps, tm)
        _fill_ext(dpl_ext, dcols(dp_ref, 2), dcols(dc_ref, 2), dcols(dn_ref, 2), i, nsteps, tm)
        dq = (conv_ref[0:1, :] * dcq_ext[HALO + 1:HALO + 1 + tm, :] + conv_ref[1:2, :] * dcq_ext[HALO:HALO + tm, :]
              + conv_ref[2:3, :] * dcq_ext[HALO - 1:HALO - 1 + tm, :])
        dcq = dcq_ext[HALO:HALO + tm, :]
        for k in range(3):
            dconv_ref[k:k + 1, :] += _colsum(dcq * q_ext[HALO + k - 1:HALO + k - 1 + tm, :])
        dcg = dq * ucols(uc_ref, 2)
        dv = dq * ucols(uc_ref, 1)
        t_ext = i * tm - HALO + lax.broadcasted_iota(jnp.int32, (tm + 2 * HALO, 1), 0)
        dps = []
        for g, wdw in enumerate(POOL_WINDOWS):
            left = wdw // 2
            right = wdw - 1 - left
            lo, hi = g * gw, (g + 1) * gw
            dpg = dpl_ext[HALO:HALO + tm, lo:hi]
            dpl_ext[:, lo:hi] = dpl_ext[:, lo:hi] / _window_count(t_ext, wdw, t_total)
            s = dpl_ext[HALO - right:HALO - right + tm, lo:hi]
            for o in range(-right + 1, left + 1):
                s = s + dpl_ext[HALO + o:HALO + o + tm, lo:hi]
            dps.append(s - dpg)
        du = jnp.concatenate([dcols(dc_ref, 0), dcg, dv] + dps, axis=1).astype(BF16)
        xv = x_ref[...]
        g, sh, sc = vec_ref[0:1, :], vec_ref[1:2, :], vec_ref[2:3, :]
        h = _norm_mod(xv, g, sh, sc).astype(BF16)
        dh = _in_proj_bwd(h, du, win_ref, dwin_ref)
        dxn, dsh, dsc, dg = _norm_mod_bwd(dh, xv, g, sc)
        acc_ref[0:1, :] += dsh
        acc_ref[1:2, :] += dsc
        acc_ref[2:3, :] += dg
        dxin_ref[...] = dx_ref[...] + dxn
        _emit_bf16_at_end(nsteps, dwin_ref, dwin_out)

    ext = pltpu.VMEM((tm + 2 * HALO, da), F32)
    return _call(
        body, name="ab_bwd_in", nsteps=nsteps,
        in_specs=[*_halo_specs(tm, 3 * da, t_total), *_halo_specs(tm, nu, t_total), _rows(tm, d), _rows(tm, d),
                  _full(vec.shape), _full(conv.shape), _VM],
        out_specs=[_rows(tm, d), _VM, _VM, _VM],
        out_shape=[jax.ShapeDtypeStruct((t_total, d), F32), jax.ShapeDtypeStruct(win.shape, BF16),
                   jax.ShapeDtypeStruct((8, da), F32), jax.ShapeDtypeStruct((8, d), F32)],
        scratch_shapes=[ext, ext, ext, pltpu.VMEM(win.shape, F32)],
        args=(dpre, dpre, dpre, u, u, u, x, dx, vec, conv, win), exch=exch)


def _cf_bwd_out(dx, y, zc, vec, wpw2, tm, exch=None):
    t_total, d = dx.shape

    def body(dx_ref, y_ref, zc_ref, vec_ref, w_ref, dzc_ref, dw_out, acc_ref, dw_ref):
        _zero_at_start(dw_ref, acc_ref)
        dxv = dx_ref[...]
        acc_ref[0:1, :] += _colsum(dxv * y_ref[...].astype(F32))
        dyf = dxv * vec_ref[0:1, :]
        acc_ref[1:2, :] += _colsum(dyf)
        dy = dyf.astype(BF16)
        zn, rstd = _layer_norm_stats(zc_ref[...].astype(F32))
        lng = vec_ref[1:2, :]
        zl = zn * lng + vec_ref[2:3, :]
        sg = _sigmoid(zl)
        dw_ref[...] += _dot_tn((zl * sg).astype(BF16), dy)
        dzl = _dot_nt(dy, w_ref[...]) * (sg * (1.0 + zl * (1.0 - sg)))
        acc_ref[2:3, :] += _colsum(dzl * zn)
        acc_ref[3:4, :] += _colsum(dzl)
        dzn = dzl * lng
        dzc = rstd * (dzn - jnp.mean(dzn, axis=-1, keepdims=True)
                      - zn * jnp.mean(dzn * zn, axis=-1, keepdims=True))
        acc_ref[4:5, :] += _colsum(dzc)
        dzc_ref[...] = dzc.astype(BF16)
        _emit_bf16_at_end(t_total // tm, dw_ref, dw_out)

    return _call(
        body, name="cf_bwd_out", nsteps=t_total // tm,
        in_specs=[_rows(tm, d), _rows(tm, d), _rows(tm, d), _full(vec.shape), _VM],
        out_specs=[_rows(tm, d), _VM, _VM],
        out_shape=[jax.ShapeDtypeStruct((t_total, d), BF16), jax.ShapeDtypeStruct(wpw2.shape, BF16),
                   jax.ShapeDtypeStruct((8, d), F32)],
        scratch_shapes=[pltpu.VMEM(wpw2.shape, F32)],
        args=(dx, y, zc, vec, wpw2), exch=exch)


def _cf_bwd_in(dzc, u, x, dx, vec, wdw, wpw1, tm, exch=None):
    t_total, d = x.shape
    nu = u.shape[1]
    nsteps = t_total // tm
    left = (CONF_KERNEL - 1) // 2

    def body(dp_ref, dc_ref, dn_ref, up_ref, uc_ref, un_ref, x_ref, dx_ref, vec_ref, wdw_ref, w_ref,
             dxin_ref, dw_out, dwdw_ref, db1_ref, acc_ref, dzc_ext, z_ext, sh_ref, dz_buf, dw_ref):
        _zero_at_start(dw_ref, dwdw_ref, db1_ref, acc_ref)
        i = pl.program_id(0)
        _fill_ext(dzc_ext, dp_ref[...].astype(F32), dc_ref[...].astype(F32), dn_ref[...].astype(F32),
                  i, nsteps, tm)
        _glu_ext(up_ref, uc_ref, un_ref, z_ext, i, nsteps, tm)
        for lo in range(0, d, CONV_COLS):
            hi = lo + CONV_COLS
            _fill_shifts(sh_ref, dzc_ext, lo, hi, tm)
            acc = wdw_ref[0:1, lo:hi] * _shifted(sh_ref, left, tm)
            for k in range(1, CONF_KERNEL):
                acc = acc + wdw_ref[k:k + 1, lo:hi] * _shifted(sh_ref, left - k, tm)
            dz_buf[:, lo:hi] = acc
            dzc = dzc_ext[HALO:HALO + tm, lo:hi]
            _fill_shifts(sh_ref, z_ext, lo, hi, tm)
            for k in range(CONF_KERNEL):
                dwdw_ref[k:k + 1, lo:hi] += _colsum(dzc * _shifted(sh_ref, k - left, tm))
        dz = dz_buf[...]
        av = uc_ref[:, 0:d].astype(F32)
        sg = _sigmoid(uc_ref[:, d:2 * d].astype(F32))
        duf = jnp.concatenate([dz * sg, dz * av * sg * (1.0 - sg)], axis=1)
        db1_ref[0:1, :] += _colsum(duf)
        du = duf.astype(BF16)
        xv = x_ref[...]
        g, sh, sc = vec_ref[0:1, :], vec_ref[1:2, :], vec_ref[2:3, :]
        h = _norm_mod(xv, g, sh, sc).astype(BF16)
        dh = _in_proj_bwd(h, du, w_ref, dw_ref)
        dxn, dsh, dsc, dg = _norm_mod_bwd(dh, xv, g, sc)
        acc_ref[0:1, :] += dsh
        acc_ref[1:2, :] += dsc
        acc_ref[2:3, :] += dg
        dxin_ref[...] = dx_ref[...] + dxn
        _emit_bf16_at_end(nsteps, dw_ref, dw_out)

    ext = pltpu.VMEM((tm + 2 * HALO, d), F32)
    return _call(
        body, name="cf_bwd_in", nsteps=nsteps,
        in_specs=[*_halo_specs(tm, d, t_total), *_halo_specs(tm, nu, t_total), _rows(tm, d), _rows(tm, d),
                  _full(vec.shape), _full(wdw.shape), _VM],
        out_specs=[_rows(tm, d), _VM, _VM, _VM, _VM],
        out_shape=[jax.ShapeDtypeStruct((t_total, d), F32), jax.ShapeDtypeStruct(wpw1.shape, BF16),
                   jax.ShapeDtypeStruct((32, d), F32), jax.ShapeDtypeStruct((8, nu), F32),
                   jax.ShapeDtypeStruct((8, d), F32)],
        scratch_shapes=[ext, ext, _shift_scratch(tm), pltpu.VMEM((tm, d), F32), pltpu.VMEM(wpw1.shape, F32)],
        args=(dzc, dzc, dzc, u, u, u, x, dx, vec, wdw, wpw1), exch=exch)


def _mod_fwd(c_all, w_mod, b_cols):
    nl, d, ncol = w_mod.shape
    nb = c_all.shape[0]

    def body(c_ref, w_ref, b_ref, o_ref):
        cv = c_ref[...]
        ca = cv * _sigmoid(cv)
        o_ref[0] = jnp.dot(ca, w_ref[0], preferred_element_type=F32, precision=HIGHEST) + b_ref[0]

    return _pcall(
        body, name="mod_fwd", grid=(nl,),
        in_specs=[_full(c_all.shape), pl.BlockSpec((1, d, ncol), lambda l: (l, 0, 0)),
                  pl.BlockSpec((1, 1, ncol), lambda l: (l, 0, 0))],
        out_specs=pl.BlockSpec((1, nb, ncol), lambda l: (l, 0, 0)),
        out_shape=jax.ShapeDtypeStruct((nl, nb, ncol), F32),
        compiler_params=_seq_params(),
    )(c_all, w_mod, b_cols.reshape(nl, 1, ncol))


def _mod_bwd(c_all_t, dmod_cols):
    d, nb = c_all_t.shape
    nl, _, ncol = dmod_cols.shape

    def body(c_ref, dm_ref, o_ref):
        cv = c_ref[...]
        ca = cv * _sigmoid(cv)
        o_ref[0] = jnp.dot(ca, dm_ref[0], preferred_element_type=F32, precision=HIGHEST)

    return _pcall(
        body, name="mod_bwd", grid=(nl,),
        in_specs=[_full(c_all_t.shape), pl.BlockSpec((1, nb, ncol), lambda l: (l, 0, 0))],
        out_specs=pl.BlockSpec((1, d, ncol), lambda l: (l, 0, 0)),
        out_shape=jax.ShapeDtypeStruct((nl, d, ncol), F32),
        compiler_params=_seq_params(),
    )(c_all_t, dmod_cols)


def _row_block(r, c):
    if r * c <= EW_BLOCK_ELEMS:
        return r
    best = None
    for br in range(8, r, 8):
        if r % br == 0 and br * c <= EW_BLOCK_ELEMS:
            best = br
    assert best is not None, (r, c)
    return best


def _as2d(a):
    return a.reshape(-1, a.shape[-1])


def _adamw(w, gparts, m, v):
    shape = w.shape
    w2, m2, v2 = _as2d(w), _as2d(m), _as2d(v)
    g2 = [_as2d(g) for g in gparts]
    r, c = w2.shape
    br = _row_block(r, c)
    ng = len(g2)

    def body(*refs):
        w_ref, m_ref, v_ref = refs[0:3]
        g_refs = refs[3:3 + ng]
        g = g_refs[0][...]
        for gr in g_refs[1:]:
            g = g + gr[...]
        _adamw_update(g, w_ref[...], m_ref[...], v_ref[...], refs[3 + ng:])

    spec = pl.BlockSpec((br, c), lambda i: (i, 0))
    outs = _pcall(
        body, name="adamw", grid=(r // br,),
        in_specs=[spec] * (3 + ng), out_specs=[spec] * 4,
        out_shape=[jax.ShapeDtypeStruct((r, c), F32)] * 4,
        compiler_params=_seq_params(),
    )(w2, m2, v2, *g2)
    return tuple(o.reshape(shape) for o in outs)


def _adamw_update(g, w, m, v, out_refs):
    go_ref, d_ref, mo_ref, vo_ref = out_refs
    mn = ADAM_B1 * m + (1.0 - ADAM_B1) * g
    vn = ADAM_B2 * v + (1.0 - ADAM_B2) * (g * g)
    m_hat = mn / (1.0 - ADAM_B1 ** ADAM_STEP)
    v_hat = vn / (1.0 - ADAM_B2 ** ADAM_STEP)
    go_ref[...] = g.reshape(go_ref.shape)
    d_ref[...] = (-ADAM_LR * (m_hat / (jnp.sqrt(v_hat) + ADAM_EPS) + ADAM_WD * w)).reshape(d_ref.shape)
    mo_ref[...] = mn.reshape(mo_ref.shape)
    vo_ref[...] = vn.reshape(vo_ref.shape)


def _adamw_partials(w, partials, m, v):
    nl, a, b = w.shape
    br = _row_block(a, b)
    nb = a // br

    def body(*refs):
        w_ref, m_ref, v_ref = refs[0:3]
        p_refs = refs[3:3 + nl]
        out_refs = refs[3 + nl:]
        for layer in range(nl):
            @pl.when(pl.program_id(0) == layer)
            def _(layer=layer):
                halves = []
                for core in range(2):
                    acc = p_refs[layer][core, 0].astype(F32)
                    for chip in range(1, N_CHIPS):
                        acc = acc + p_refs[layer][core, chip].astype(F32)
                    halves.append(acc)
                _adamw_update(halves[0] + halves[1], w_ref[...], m_ref[...], v_ref[...], out_refs)

    def part_spec(layer):
        def index(l, i):
            return 0, 0, jnp.where(l == layer, i, jnp.where(l < layer, 0, nb - 1)), 0
        return pl.BlockSpec((2, N_CHIPS, br, b), index)

    spec = pl.BlockSpec((br, b), lambda l, i: (l * nb + i, 0))
    outs = _pcall(
        body, name="adamw_partials", grid=(nl, nb),
        in_specs=[spec] * 3 + [part_spec(layer) for layer in range(nl)], out_specs=[spec] * 4,
        out_shape=[jax.ShapeDtypeStruct((nl * a, b), F32)] * 4,
        compiler_params=pltpu.CompilerParams(dimension_semantics=("arbitrary", "arbitrary"),
                                             vmem_limit_bytes=VMEM_LIMIT),
    )(_as2d(w), _as2d(m), _as2d(v), *partials)
    return tuple(o.reshape(w.shape) for o in outs)


def _allgather8(block, with_sum, exch=None):
    m_per, n = block.shape
    nx = 0 if exch is None else len(exch.arrs)
    nvm = 2 if with_sum else 1

    def body(x_ref, *rest):
        xin, out_ref = rest[:nx], rest[nx]
        sum_ref = rest[nx + 1] if with_sum else None
        xout = rest[nx + nvm:2 * nx + nvm]
        send_sems, recv_sems, local_sem = rest[2 * nx + nvm:2 * nx + nvm + 3]
        xsems = rest[2 * nx + nvm + 3:]
        if exch is not None:
            exch.start(xin, xout, xsems)
        x, y, c = _place()
        me, sibling = (x, y, c), (x, y, 1 - c)
        chips = [(1 - x, y), (x, 1 - y), (1 - x, 1 - y)]

        def rows(px, py, pc):
            return out_ref.at[pl.ds((4 * px + 2 * py + pc) * m_per, m_per), :]

        def copy(k, blk, to, src=None):
            return pltpu.make_async_remote_copy(
                src_ref=rows(*blk) if src is None else src, dst_ref=rows(*blk),
                send_sem=send_sems.at[k], recv_sem=recv_sems.at[k], device_id=to, device_id_type=MESH)

        mine = pltpu.make_async_copy(x_ref, rows(*me), local_sem)
        mine.start()
        first = [copy(0, me, sibling, src=x_ref)]
        first += [copy(1 + j, me, (*chip, c), src=x_ref) for j, chip in enumerate(chips)]
        for cp in first:
            cp.start()
        passed = [copy(4 + j, (*chip, c), sibling) for j, chip in enumerate(chips)]
        for j, chip in enumerate(chips):
            copy(1 + j, (*chip, c), me).wait_recv()
            passed[j].start()
        copy(0, sibling, me).wait_recv()
        for j, chip in enumerate(chips):
            copy(4 + j, (*chip, 1 - c), me).wait_recv()
        for cp in first + passed:
            cp.wait_send()
        mine.wait()
        if exch is not None:
            exch.mid(xin, xout, xsems)
            exch.wait(xin, xout, xsems)
        if with_sum:
            acc = out_ref[0:m_per, :]
            for k in range(1, N_DEV):
                acc = acc + out_ref[k * m_per:(k + 1) * m_per, :]
            sum_ref[...] = acc

    out_shape = [jax.ShapeDtypeStruct((N_DEV * m_per, n), F32)]
    out_specs = [_VM]
    if with_sum:
        out_shape.append(jax.ShapeDtypeStruct((m_per, n), F32))
        out_specs.append(_VM)
    res = _pcall(
        body, name=("allgather8_sum" if with_sum else "allgather8") + ("" if exch is None else "_" + exch.tag),
        in_specs=[_VM] + [_ANY] * nx, out_specs=out_specs + [_ANY] * nx,
        out_shape=out_shape + ([] if exch is None else exch.out_shapes()),
        scratch_shapes=[pltpu.SemaphoreType.DMA((7,)), pltpu.SemaphoreType.DMA((7,)), pltpu.SemaphoreType.DMA]
        + ([] if exch is None else exch.sems()),
        compiler_params=pltpu.CompilerParams(vmem_limit_bytes=VMEM_LIMIT),
    )(block, *([] if exch is None else exch.arrs))
    return list(res[:nvm]), list(res[nvm:])


def _my_cols(full, chip):
    w = full.shape[-1] // N_CHIPS
    return lax.dynamic_slice_in_dim(full, chip * w, w, axis=full.ndim - 1)


def _pad_rows(a, rows):
    return jnp.pad(a, ((0, rows - a.shape[0]), (0, 0)))


def _to_lanes(a):
    flat = a.reshape(-1)
    n = -(-flat.shape[0] // (8 * LANES)) * (8 * LANES)
    return jnp.pad(flat, (0, n - flat.shape[0])).reshape(-1, LANES)


class _Packer:
    def __init__(self):
        self.items = []
        self.rows = 0

    def add(self, name, a):
        lanes = _to_lanes(a)
        self.items.append((name, self.rows, a.shape, lanes))
        self.rows += lanes.shape[0]

    def pack(self):
        total = -(-self.rows // 8) * 8
        return _pad_rows(jnp.concatenate([it[3] for it in self.items], axis=0), total)

    def unpack(self, buf):
        out = {}
        for name, row, shape, lanes in self.items:
            size = 1
            for s in shape:
                size *= s
            out[name] = buf[row:row + lanes.shape[0]].reshape(-1)[:size].reshape(shape)
        return out


TM_SEQ = 512
TM_FFN = 256
TM_FFN_FWD = 512
TM_WIDE = 1024


LAYER_KEYS = ("in", "out", "gate", "up", "down")
BLOCKED_KEYS = ("in",)
TRANSPOSED = ("ffn_w_gate", "ffn_w_up")


def _layer_big_names(layer):
    i = layer // 2
    mix = (("ab_w_in", i), ("ab_w_out", i)) if layer % 2 == 0 else (("cf_w_pw1", i), ("cf_w_pw2", i))
    return dict(zip(LAYER_KEYS, mix + (("ffn_w_gate", layer), ("ffn_w_up", layer), ("ffn_w_down", layer))))


def _unpack_weight(key, g):
    g = g.reshape(N_CHIPS, -1, g.shape[-1])
    return g if key in BLOCKED_KEYS else g.reshape(-1, g.shape[-1])


def _chunk_grad(key, dw):
    parts = dw if key in BLOCKED_KEYS else dw.reshape(N_CHIPS, -1, dw.shape[-1])
    return parts.astype(BF16)


def _local_step(x, target, mods, p, shards, first):
    t_total, d = x.shape
    depth = mods.shape[0]
    tm = min(TM_SEQ, t_total)
    tmw = min(TM_WIDE, t_total)
    tmf = min(TM_FFN, t_total)
    saved = []
    xin = x
    weights = [{} for _ in range(depth)]

    def carried(stage, layer):
        if layer == 0:
            return {"in": (0, ("out", "gate")), "mix": (0, ("up", "down")), "ffn": (1, ("in", "out", "gate", "up"))}[stage]
        return {"in": (layer, ("down",)), "mix": (layer + 1, ("in", "out")), "ffn": (layer + 1, ("gate", "up"))}[stage]

    def gather(stage, layer):
        of, keys = carried(stage, layer)
        if of >= depth:
            return None
        return _Gather([shards[of][k].reshape(2, -1, shards[of][k].shape[-1]) for k in keys])

    def keep(stage, layer, arrs):
        of, keys = carried(stage, layer)
        for k, g in zip(keys, arrs):
            weights[of][k] = _unpack_weight(k, g)

    for k, g in first.items():
        weights[0][k] = _unpack_weight(k, g)
    for layer in range(depth):
        i = layer // 2
        lw = weights[layer]
        sh1, sc1, g1, sh2, sc2, g2 = (mods[layer, k:k + 1] for k in range(6))
        vec_in = jnp.concatenate([p["norm_mix_g"][layer:layer + 1], sh1, sc1], axis=0)
        bias = None if layer % 2 == 0 else p["cf_b_pw1"][i:i + 1]
        (u,), arrived = _in_proj(xin, vec_in, lw["in"], bias, tmw, exch=gather("in", layer))
        keep("in", layer, arrived)
        if layer % 2 == 0:
            (y, x2), arrived = _ab_fwd(u, xin, g1, p["ab_conv"][i], p["ab_w_pool"][i].astype(BF16),
                                       p["ab_pool_scale"][i:i + 1], lw["out"], tm, exch=gather("mix", layer))
            zc = None
        else:
            vec_cf = jnp.concatenate([g1, p["cf_b_dw"][i:i + 1], p["cf_ln_g"][i:i + 1], p["cf_ln_b"][i:i + 1],
                                      p["cf_b_pw2"][i:i + 1]], axis=0)
            (zc, y, x2), arrived = _cf_fwd(u, xin, vec_cf, _pad_rows(p["cf_w_dw"][i], 32), lw["out"], tm,
                                           exch=gather("mix", layer))
        keep("mix", layer, arrived)
        vec_ffn = jnp.concatenate([p["norm_ffn_g"][layer:layer + 1], sh2, sc2, g2], axis=0)
        (a, b, fout, x3), arrived = _ffn_fwd(x2, vec_ffn, lw["gate"], lw["up"], lw["down"], min(TM_FFN_FWD, t_total),
                                             exch=gather("ffn", layer))
        keep("ffn", layer, arrived)
        saved.append((xin, u, y, zc, x2, a, b, fout))
        xin = x3

    (dx, fin), _ = _final_fwd_bwd(xin, target, p["final_norm_g"].reshape(1, d), tm)
    grads = {"final_norm_g": fin[0], "loss": fin[1, 0:1]}
    per_layer = {k: [None] * depth for k in ("norm_mix_g", "norm_ffn_g")}
    half = {k: [None] * (depth // 2) for k in (
        "ab_conv", "ab_w_pool", "ab_pool_scale", "cf_b_pw1", "cf_w_dw", "cf_b_dw", "cf_ln_g", "cf_ln_b", "cf_b_pw2")}
    dmods = [None] * depth
    received = {}
    pending = None
    for layer in reversed(range(depth)):
        i = layer // 2
        lw = weights[layer]
        xin, u, y, zc, x2, a, b, fout = saved[layer]
        sh1, sc1, g1, sh2, sc2, g2 = (mods[layer, k:k + 1] for k in range(6))
        above = _Scatter([pending]) if pending is not None else None
        (da, db, dwd, acc_d), arrived = _ffn_bwd_down(dx, fout, a, b, g2, lw["down"], tmf, exch=above)
        if pending is not None:
            received[(layer + 1, "in")] = arrived[0]
        vec_ffn = jnp.concatenate([p["norm_ffn_g"][layer:layer + 1], sh2, sc2], axis=0)
        (dx2, dwg, dwu, acc_u), arrived = _ffn_bwd_up(da, db, x2, dx, vec_ffn, lw["gate"], lw["up"], tmf,
                                                      exch=_Scatter([_chunk_grad("down", dwd)]))
        received[(layer, "down")] = arrived[0]
        per_layer["norm_ffn_g"][layer] = acc_u[2]
        vec_in = jnp.concatenate([p["norm_mix_g"][layer:layer + 1], sh1, sc1], axis=0)
        send_gate = _Scatter([_chunk_grad("gate", dwg)])
        if layer % 2 == 0:
            (dpre, dwout, dwpool, acc_o), arrived = _ab_bwd_out(
                dx2, y, u, g1, p["ab_conv"][i], p["ab_w_pool"][i].astype(BF16), p["ab_pool_scale"][i:i + 1],
                lw["out"], tmw, exch=send_gate)
            received[(layer, "gate")] = arrived[0]
            send_up_out = _Scatter([_chunk_grad("up", dwu), _chunk_grad("out", dwout)])
            (dx, dwin, dconv, acc_i), arrived = _ab_bwd_in(dpre, u, xin, dx2, vec_in, p["ab_conv"][i], lw["in"], tm,
                                                           exch=send_up_out)
            half["ab_w_pool"][i] = dwpool
            half["ab_pool_scale"][i] = acc_o[1, 0:d // 2]
            half["ab_conv"][i] = dconv[0:3]
        else:
            vec_cf = jnp.concatenate([g1, p["cf_ln_g"][i:i + 1], p["cf_ln_b"][i:i + 1]], axis=0)
            (dzc, dwout, acc_o), arrived = _cf_bwd_out(dx2, y, zc, vec_cf, lw["out"], tmw, exch=send_gate)
            received[(layer, "gate")] = arrived[0]
            send_up_out = _Scatter([_chunk_grad("up", dwu), _chunk_grad("out", dwout)])
            (dx, dwin, dwdw, db1, acc_i), arrived = _cf_bwd_in(
                dzc, u, xin, dx2, vec_in, _pad_rows(p["cf_w_dw"][i], 32), lw["in"], tm, exch=send_up_out)
            half["cf_b_pw2"][i] = acc_o[1]
            half["cf_ln_g"][i] = acc_o[2]
            half["cf_ln_b"][i] = acc_o[3]
            half["cf_b_dw"][i] = acc_o[4]
            half["cf_w_dw"][i] = dwdw[0:CONF_KERNEL]
            half["cf_b_pw1"][i] = db1[0]
        received[(layer, "up")], received[(layer, "out")] = arrived
        per_layer["norm_mix_g"][layer] = acc_i[2]
        dmods[layer] = jnp.stack([acc_i[0], acc_i[1], acc_o[0], acc_u[0], acc_u[1], acc_d[0]], axis=0)
        pending = _chunk_grad("in", dwin)
    for k, v in {**per_layer, **half}.items():
        grads[k] = jnp.stack(v, axis=0)
    return dx, grads, jnp.stack(dmods, axis=0), received, pending


SMALL_COLS = ("ab_conv", "cf_b_pw1", "cf_w_dw", "cf_b_dw", "cf_ln_g", "cf_ln_b", "cf_b_pw2")
SMALL_REPL = ("norm_mix_g", "norm_ffn_g", "ab_w_pool", "ab_pool_scale", "final_norm_g")
WEIGHTS = ("norm_mix_g", "norm_ffn_g", "w_mod", "b_mod", "ab_w_in", "ab_conv", "ab_w_pool", "ab_pool_scale",
           "ab_w_out", "cf_w_pw1", "cf_b_pw1", "cf_w_dw", "cf_b_dw", "cf_ln_g", "cf_ln_b", "cf_w_pw2",
           "cf_b_pw2", "ffn_w_gate", "ffn_w_up", "ffn_w_down", "final_norm_g")


def kernel(x, c, norm_mix_g, norm_ffn_g, w_mod, b_mod, ab_w_in, ab_conv, ab_w_pool, ab_pool_scale, ab_w_out, cf_w_pw1, cf_b_pw1, cf_w_dw, cf_b_dw, cf_ln_g, cf_ln_b, cf_w_pw2, cf_b_pw2, ffn_w_gate, ffn_w_up, ffn_w_down, final_norm_g, loss_target, m_norm_mix_g, m_norm_ffn_g, m_w_mod, m_b_mod, m_ab_w_in, m_ab_conv, m_ab_w_pool, m_ab_pool_scale, m_ab_w_out, m_cf_w_pw1, m_cf_b_pw1, m_cf_w_dw, m_cf_b_dw, m_cf_ln_g, m_cf_ln_b, m_cf_w_pw2, m_cf_b_pw2, m_ffn_w_gate, m_ffn_w_up, m_ffn_w_down, m_final_norm_g, v_norm_mix_g, v_norm_ffn_g, v_w_mod, v_b_mod, v_ab_w_in, v_ab_conv, v_ab_w_pool, v_ab_pool_scale, v_ab_w_out, v_cf_w_pw1, v_cf_b_pw1, v_cf_w_dw, v_cf_b_dw, v_cf_ln_g, v_cf_ln_b, v_cf_w_pw2, v_cf_b_pw2, v_ffn_w_gate, v_ffn_w_up, v_ffn_w_down, v_final_norm_g):
    w = dict(norm_mix_g=norm_mix_g, norm_ffn_g=norm_ffn_g, w_mod=w_mod, b_mod=b_mod, ab_w_in=ab_w_in,
             ab_conv=ab_conv, ab_w_pool=ab_w_pool, ab_pool_scale=ab_pool_scale, ab_w_out=ab_w_out,
             cf_w_pw1=cf_w_pw1, cf_b_pw1=cf_b_pw1, cf_w_dw=cf_w_dw, cf_b_dw=cf_b_dw, cf_ln_g=cf_ln_g,
             cf_ln_b=cf_ln_b, cf_w_pw2=cf_w_pw2, cf_b_pw2=cf_b_pw2, ffn_w_gate=ffn_w_gate, ffn_w_up=ffn_w_up,
             ffn_w_down=ffn_w_down, final_norm_g=final_norm_g)
    mom = dict(norm_mix_g=m_norm_mix_g, norm_ffn_g=m_norm_ffn_g, w_mod=m_w_mod, b_mod=m_b_mod, ab_w_in=m_ab_w_in,
               ab_conv=m_ab_conv, ab_w_pool=m_ab_w_pool, ab_pool_scale=m_ab_pool_scale, ab_w_out=m_ab_w_out,
               cf_w_pw1=m_cf_w_pw1, cf_b_pw1=m_cf_b_pw1, cf_w_dw=m_cf_w_dw, cf_b_dw=m_cf_b_dw, cf_ln_g=m_cf_ln_g,
               cf_ln_b=m_cf_ln_b, cf_w_pw2=m_cf_w_pw2, cf_b_pw2=m_cf_b_pw2, ffn_w_gate=m_ffn_w_gate,
               ffn_w_up=m_ffn_w_up, ffn_w_down=m_ffn_w_down, final_norm_g=m_final_norm_g)
    var = dict(norm_mix_g=v_norm_mix_g, norm_ffn_g=v_norm_ffn_g, w_mod=v_w_mod, b_mod=v_b_mod, ab_w_in=v_ab_w_in,
               ab_conv=v_ab_conv, ab_w_pool=v_ab_w_pool, ab_pool_scale=v_ab_pool_scale, ab_w_out=v_ab_w_out,
               cf_w_pw1=v_cf_w_pw1, cf_b_pw1=v_cf_b_pw1, cf_w_dw=v_cf_w_dw, cf_b_dw=v_cf_b_dw, cf_ln_g=v_cf_ln_g,
               cf_ln_b=v_cf_ln_b, cf_w_pw2=v_cf_w_pw2, cf_b_pw2=v_cf_b_pw2, ffn_w_gate=v_ffn_w_gate,
               ffn_w_up=v_ffn_w_up, ffn_w_down=v_ffn_w_down, final_norm_g=v_final_norm_g)
    px, py, pc = _place()
    chip = 2 * px + py
    dev = 2 * chip + pc
    depth, d, mod_cols = w_mod.shape
    x = x[0]
    target = loss_target[0]

    def rows_major(name, t):
        return jnp.swapaxes(t, 1, 2) if name in TRANSPOSED else t

    shards = [{k: rows_major(name, w[name])[idx].astype(BF16) for k, (name, idx) in _layer_big_names(layer).items()}
              for layer in range(depth)]

    small_in = _Packer()
    small_in.add("c", c)
    for name in SMALL_COLS:
        small_in.add(name, w[name])
    def first_gather(*keys):
        return _Gather([shards[0][k].reshape(2, -1, shards[0][k].shape[-1]) for k in keys])

    first = {}
    (gathered,), (first["in"],) = _allgather8(small_in.pack(), with_sum=False, exch=first_gather("in"))
    gathered = gathered.reshape(N_DEV, -1, LANES)
    per_dev = [small_in.unpack(gathered[k]) for k in range(N_DEV)]
    c_all = jnp.concatenate([pd["c"] for pd in per_dev], axis=0)
    params = {name: jnp.concatenate([per_dev[2 * k][name] for k in range(N_CHIPS)], axis=-1)
              for name in SMALL_COLS}
    for name in SMALL_REPL:
        params[name] = w[name]

    mod_part = _mod_fwd(c_all, w_mod, _my_cols(b_mod, chip))
    (mod_all,), _ = _allgather8(mod_part.reshape(-1, LANES), with_sum=False)
    mod_all = mod_all.reshape(N_CHIPS, 2, depth, N_DEV, mod_cols)[:, 0]
    mod_all = jnp.moveaxis(mod_all, 0, 2).reshape(depth, N_DEV, N_CHIPS * mod_cols)
    mods = lax.dynamic_index_in_dim(mod_all, dev, axis=1, keepdims=False).reshape(depth, 6, d)

    grad_x, grads, dmods, received, last_chunk = _local_step(x, target, mods, params, shards, first)

    small_out = _Packer()
    small_out.add("dmods", dmods)
    for name in ("loss",) + SMALL_REPL + SMALL_COLS:
        small_out.add(name, grads[name])
    (parts_all, parts_sum), (received[(0, "in")],) = _allgather8(small_out.pack(), with_sum=True,
                                                                 exch=_Scatter([last_chunk]))
    small_sum = small_out.unpack(parts_sum)
    loss = small_sum["loss"][0]
    dmods_all = jnp.stack([small_out.unpack(pa)["dmods"] for pa in parts_all.reshape(N_DEV, -1, LANES)], axis=1)
    dmods_all = dmods_all.reshape(depth, N_DEV, 6 * d)

    g_final = {}
    g_final["w_mod"] = [_mod_bwd(c_all.T, _my_cols(dmods_all, chip))]
    g_final["b_mod"] = [small_sum["dmods"].reshape(depth, 6 * d)]
    for name in SMALL_REPL:
        g_final[name] = [small_sum[name]]
    for name in SMALL_COLS:
        g_final[name] = [_my_cols(small_sum[name], chip)]

    updates = {}
    for name in WEIGHTS:
        parts = [received[(layer, k)] for layer in range(depth)
                 for k, (other, _) in _layer_big_names(layer).items() if other == name]
        if parts:
            outs = _adamw_partials(rows_major(name, w[name]), parts, rows_major(name, mom[name]),
                                   rows_major(name, var[name]))
            updates[name] = [rows_major(name, o) for o in outs]
        else:
            updates[name] = _adamw(w[name], g_final[name], mom[name], var[name])
    return (loss, grad_x[None], *[updates[name][0] for name in WEIGHTS], *[updates[name][1] for name in WEIGHTS],
            *[updates[name][2] for name in WEIGHTS], *[updates[name][3] for name in WEIGHTS])
```

```python
import functools

import jax
import jax.numpy as jnp
from jax import lax
from jax.experimental import pallas as pl
from jax.experimental.pallas import tpu as pltpu

F32 = jnp.float32
BF16 = jnp.bfloat16
RMS_EPS = 1e-6
LN_EPS = 1e-5
ADAM_LR = 0.001
ADAM_B1 = 0.9
ADAM_B2 = 0.999
ADAM_EPS = 1e-08
ADAM_WD = 0.01
ADAM_STEP = 10
POOL_WINDOWS = (2, 4, 8, 16)
CONF_KERNEL = 31
N_CHIPS = 4
N_DEV = 8
HALO = 16
CONV_COLS = 256
FFN_CHUNK = 1536
LANES = 1024
VMEM_LIMIT = 56 * 1024 * 1024
VMEM_LIMIT_WIDE = 60 * 1024 * 1024
EW_BLOCK_ELEMS = 256 * 1024
MESH = pl.DeviceIdType.MESH
HIGHEST = lax.Precision.HIGHEST

_pcall = pl.pallas_call


def _dot(a, b):
    return jnp.dot(a, b, preferred_element_type=F32)


def _dot_tn(a, b):
    return lax.dot_general(a, b, (((0,), (0,)), ((), ())), preferred_element_type=F32)


def _dot_nt(a, b):
    return lax.dot_general(a, b, (((1,), (1,)), ((), ())), preferred_element_type=F32)


def _colsum(v):
    return jnp.sum(v, axis=0, keepdims=True)


def _sigmoid(v):
    return 1.0 / (1.0 + jnp.exp(-v))


def _rows(tm, c):
    return pl.BlockSpec((tm, c), lambda i: (i, 0))


def _full(shape):
    nd = len(shape)
    return pl.BlockSpec(shape, lambda i: (0,) * nd)


_VM = pl.BlockSpec(memory_space=pltpu.VMEM)
_ANY = pl.BlockSpec(memory_space=pl.ANY)


def _halo_specs(tm, c, t_total):
    r = tm // HALO
    last = t_total // HALO - 1
    prev = pl.BlockSpec((HALO, c), lambda i: (jnp.maximum(i * r - 1, 0), 0))
    nxt = pl.BlockSpec((HALO, c), lambda i: (jnp.minimum((i + 1) * r, last), 0))
    return prev, _rows(tm, c), nxt


def _seq_params(vmem_limit=VMEM_LIMIT):
    return pltpu.CompilerParams(dimension_semantics=("arbitrary",), vmem_limit_bytes=vmem_limit)


def _place():
    return lax.axis_index("x"), lax.axis_index("y"), lax.axis_index("c")


def _peer_chips(x, y):
    return [(1 - x, y), (x, 1 - y), (1 - x, 1 - y)]


class _Gather:
    tag = "gather"

    def __init__(self, arrs):
        self.arrs = list(arrs)

    def out_shapes(self):
        return [jax.ShapeDtypeStruct((N_CHIPS,) + a.shape, a.dtype) for a in self.arrs]

    def sems(self):
        n = len(self.arrs)
        return [pltpu.SemaphoreType.DMA((3 * n,)) for _ in range(4)] + [pltpu.SemaphoreType.DMA((n,))]

    def _copies(self, ins, outs, sems, kinds):
        ici_send, ici_recv, d2d_send, d2d_recv, local_sems = sems
        x, y, c = _place()
        me = 2 * x + y
        found = {kind: [] for kind in kinds}
        for j in range(len(ins)):
            if "local" in kinds:
                found["local"].append(pltpu.make_async_copy(ins[j], outs[j].at[me], local_sems.at[j]))
            for k, (px, py) in enumerate(_peer_chips(x, y)):
                ici = dict(send_sem=ici_send.at[3 * j + k], recv_sem=ici_recv.at[3 * j + k],
                           device_id=(px, py, c), device_id_type=MESH)
                d2d = dict(send_sem=d2d_send.at[3 * j + k], recv_sem=d2d_recv.at[3 * j + k],
                           device_id=(x, y, 1 - c), device_id_type=MESH)
                theirs = outs[j].at[2 * px + py]
                if "send" in kinds:
                    found["send"].append(pltpu.make_async_remote_copy(
                        src_ref=ins[j].at[c], dst_ref=outs[j].at[me, c], **ici))
                if "arrival" in kinds:
                    found["arrival"].append(pltpu.make_async_remote_copy(
                        src_ref=ins[j].at[c], dst_ref=theirs.at[c], **ici))
                if "pass" in kinds:
                    found["pass"].append(pltpu.make_async_remote_copy(
                        src_ref=theirs.at[c], dst_ref=theirs.at[c], **d2d))
                if "passed" in kinds:
                    found["passed"].append(pltpu.make_async_remote_copy(
                        src_ref=theirs.at[c], dst_ref=theirs.at[1 - c], **d2d))
        return found

    def start(self, ins, outs, sems):
        found = self._copies(ins, outs, sems, ("local", "send"))
        for cp in found["local"] + found["send"]:
            cp.start()

    def mid(self, ins, outs, sems):
        found = self._copies(ins, outs, sems, ("arrival", "pass"))
        for arrived, onward in zip(found["arrival"], found["pass"]):
            arrived.wait_recv()
            onward.start()

    def wait(self, ins, outs, sems):
        found = self._copies(ins, outs, sems, ("local", "send", "pass", "passed"))
        for cp in found["passed"]:
            cp.wait_recv()
        for cp in found["send"] + found["pass"]:
            cp.wait_send()
        for cp in found["local"]:
            cp.wait()


class _Scatter:
    tag = "scatter"

    def __init__(self, arrs):
        self.arrs = list(arrs)

    def out_shapes(self):
        return [jax.ShapeDtypeStruct((2,) + a.shape, a.dtype) for a in self.arrs]

    def sems(self):
        n = len(self.arrs)
        dma = pltpu.SemaphoreType.DMA
        return [dma((3 * n,)), dma((3 * n,)), dma((4 * n,)), dma((4 * n,)), dma((n,))]

    def _copies(self, ins, outs, sems, kinds):
        ici_send, ici_recv, d2d_send, d2d_recv, local_sems = sems
        x, y, c = _place()
        me = 2 * x + y
        found = {kind: [] for kind in kinds}
        for j in range(len(ins)):
            def d2d(k):
                return dict(send_sem=d2d_send.at[4 * j + k], recv_sem=d2d_recv.at[4 * j + k],
                            device_id=(x, y, 1 - c), device_id_type=MESH)

            if "local" in kinds:
                found["local"].append(pltpu.make_async_copy(ins[j].at[me], outs[j].at[0, me], local_sems.at[j]))
            if "own" in kinds:
                found["own"].append(pltpu.make_async_remote_copy(
                    src_ref=ins[j].at[me], dst_ref=outs[j].at[1, me], **d2d(3)))
            if "passed" in kinds:
                found["passed"].append(pltpu.make_async_remote_copy(
                    src_ref=ins[j].at[me], dst_ref=outs[j].at[1, me], **d2d(3)))
            for k, (px, py) in enumerate(_peer_chips(x, y)):
                ici = dict(send_sem=ici_send.at[3 * j + k], recv_sem=ici_recv.at[3 * j + k],
                           device_id=(px, py, c), device_id_type=MESH)
                peer = 2 * px + py
                if "send" in kinds:
                    found["send"].append(pltpu.make_async_remote_copy(
                        src_ref=ins[j].at[peer], dst_ref=outs[j].at[0, me], **ici))
                if "arrival" in kinds:
                    found["arrival"].append(pltpu.make_async_remote_copy(
                        src_ref=ins[j].at[me], dst_ref=outs[j].at[0, peer], **ici))
                if "pass" in kinds:
                    found["pass"].append(pltpu.make_async_remote_copy(
                        src_ref=outs[j].at[0, peer], dst_ref=outs[j].at[1, peer], **d2d(k)))
                if "passed" in kinds:
                    found["passed"].append(pltpu.make_async_remote_copy(
                        src_ref=outs[j].at[0, peer], dst_ref=outs[j].at[1, peer], **d2d(k)))
        return found

    def start(self, ins, outs, sems):
        found = self._copies(ins, outs, sems, ("local", "own", "send"))
        for cp in found["local"] + found["own"] + found["send"]:
            cp.start()

    def mid(self, ins, outs, sems):
        found = self._copies(ins, outs, sems, ("arrival", "pass"))
        for arrived, onward in zip(found["arrival"], found["pass"]):
            arrived.wait_recv()
            onward.start()

    def wait(self, ins, outs, sems):
        found = self._copies(ins, outs, sems, ("local", "own", "send", "pass", "passed"))
        for cp in found["passed"]:
            cp.wait_recv()
        for cp in found["own"] + found["send"] + found["pass"]:
            cp.wait_send()
        for cp in found["local"]:
            cp.wait()


def _call(body, *, name, nsteps, in_specs, out_specs, out_shape, args, scratch_shapes=(), exch=None,
          vmem_limit=VMEM_LIMIT):
    if exch is None:
        outs = _pcall(body, name=name, grid=(nsteps,), in_specs=list(in_specs), out_specs=list(out_specs),
                      out_shape=list(out_shape), scratch_shapes=list(scratch_shapes),
                      compiler_params=_seq_params(vmem_limit))(*args)
        return list(outs), []
    n, ni, no, ns = len(exch.arrs), len(in_specs), len(out_specs), len(scratch_shapes)

    def hosted(*refs):
        xin = refs[ni:ni + n]
        xout = refs[ni + n + no:ni + 2 * n + no]
        scr = refs[ni + 2 * n + no:]

        @pl.when(pl.program_id(0) == 0)
        def _():
            exch.start(xin, xout, scr[ns:])

        body(*refs[:ni], *refs[ni + n:ni + n + no], *scr[:ns])

        @pl.when(pl.program_id(0) == max(nsteps - 3, 0))
        def _():
            exch.mid(xin, xout, scr[ns:])

        @pl.when(pl.program_id(0) == nsteps - 1)
        def _():
            exch.wait(xin, xout, scr[ns:])

    outs = _pcall(hosted, name=name + "_" + exch.tag, grid=(nsteps,),
                  in_specs=[*in_specs, *[_ANY] * n], out_specs=[*out_specs, *[_ANY] * n],
                  out_shape=[*out_shape, *exch.out_shapes()],
                  scratch_shapes=[*scratch_shapes, *exch.sems()],
                  compiler_params=_seq_params(vmem_limit))(*args, *exch.arrs)
    return list(outs[:no]), list(outs[no:])


def _rms(x):
    r = lax.rsqrt(jnp.mean(x * x, axis=-1, keepdims=True) + RMS_EPS)
    return x * r, r


def _norm_mod(x, g, sh, sc):
    xhat, _ = _rms(x)
    return xhat * g * (1.0 + sc) + sh


def _norm_mod_bwd(dh, x, g, sc):
    xhat, r = _rms(x)
    n = xhat * g
    dsh = _colsum(dh)
    dsc = _colsum(dh * n)
    dn = dh * (1.0 + sc)
    dg = _colsum(dn * xhat)
    dxn = dn * g
    dx = r * (dxn - xhat * jnp.mean(dxn * xhat, axis=-1, keepdims=True))
    return dx, dsh, dsc, dg


def _fill_ext(ext_ref, prev, cur, nxt, i, nsteps, tm):
    ext_ref[0:HALO, :] = jnp.where(i > 0, prev, 0.0)
    ext_ref[HALO:HALO + tm, :] = cur
    ext_ref[HALO + tm:HALO + tm + HALO, :] = jnp.where(i < nsteps - 1, nxt, 0.0)


def _shift_scratch(tm):
    return pltpu.VMEM((8, tm + 2 * HALO - 8, CONV_COLS), F32)


def _fill_shifts(sh_ref, ext_ref, lo, hi, tm):
    for b in range(8):
        sh_ref[b] = ext_ref[b:b + tm + 2 * HALO - 8, lo:hi]


def _shifted(sh_ref, offset, tm):
    b = offset % 8
    start = HALO + offset - b
    return sh_ref[b, start:start + tm, :]


def _window_count(t, wdw, t_total):
    left = wdw // 2
    right = wdw - 1 - left
    cnt = jnp.minimum(t + right, t_total - 1) - jnp.maximum(t - left, 0) + 1
    return jnp.maximum(cnt, 1).astype(F32)


def _in_proj(x, vec, w, bias, tm, exch=None):
    t_total, d = x.shape
    nk = w.shape[2]
    n = N_CHIPS * nk
    has_bias = bias is not None

    def body(*refs):
        if has_bias:
            x_ref, vec_ref, w_ref, b_ref, u_ref = refs
        else:
            x_ref, vec_ref, w_ref, u_ref = refs
        h = _norm_mod(x_ref[...], vec_ref[0:1, :], vec_ref[1:2, :], vec_ref[2:3, :])
        h = h.astype(BF16)
        for k in range(N_CHIPS):
            u = _dot(h, w_ref[k])
            if has_bias:
                u = u + b_ref[:, k * nk:(k + 1) * nk]
            u_ref[:, k * nk:(k + 1) * nk] = u.astype(BF16)

    in_specs = [_rows(tm, d), _full(vec.shape), _VM]
    args = [x, vec, w]
    if has_bias:
        in_specs.append(_full(bias.shape))
        args.append(bias)
    return _call(
        body, name="in_proj_bias" if has_bias else "in_proj", nsteps=t_total // tm,
        in_specs=in_specs, out_specs=[_rows(tm, n)], out_shape=[jax.ShapeDtypeStruct((t_total, n), BF16)],
        args=args, exch=exch)


def _in_proj_bwd(h, du, w_ref, dw_ref):
    nk = w_ref.shape[2]
    dh = None
    for k in range(N_CHIPS):
        duk = du[:, k * nk:(k + 1) * nk]
        dw_ref[k] += _dot_tn(h, duk)
        part = _dot_nt(duk, w_ref[k])
        dh = part if dh is None else dh + part
    return dh


def _ab_core(up_ref, uc_ref, un_ref, conv_ref, wpool_ref, q_ext, p_ext, i, nsteps, tm, t_total):
    da = uc_ref.shape[1] // 4

    def cols(ref, k):
        return ref[:, k * da:(k + 1) * da].astype(F32)

    _fill_ext(q_ext, cols(up_ref, 1) * cols(up_ref, 2), cols(uc_ref, 1) * cols(uc_ref, 2),
              cols(un_ref, 1) * cols(un_ref, 2), i, nsteps, tm)
    _fill_ext(p_ext, cols(up_ref, 3), cols(uc_ref, 3), cols(un_ref, 3), i, nsteps, tm)
    bg = cols(uc_ref, 0)
    cq = (conv_ref[0:1, :] * q_ext[HALO - 1:HALO - 1 + tm, :] + conv_ref[1:2, :] * q_ext[HALO:HALO + tm, :]
          + conv_ref[2:3, :] * q_ext[HALO + 1:HALO + 1 + tm, :])
    t = i * tm + lax.broadcasted_iota(jnp.int32, (tm, 1), 0)
    gw = da // len(POOL_WINDOWS)
    pooled, ybpre = [], []
    for g, wdw in enumerate(POOL_WINDOWS):
        left = wdw // 2
        right = wdw - 1 - left
        lo, hi = g * gw, (g + 1) * gw
        s = p_ext[HALO - left:HALO - left + tm, lo:hi]
        for o in range(-left + 1, right + 1):
            s = s + p_ext[HALO + o:HALO + o + tm, lo:hi]
        pg = s / _window_count(t, wdw, t_total) - p_ext[HALO:HALO + tm, lo:hi]
        pooled.append(pg.astype(BF16))
        ybpre.append(_dot(pooled[-1], wpool_ref[g]))
    return bg, cq, pooled, jnp.concatenate(ybpre, axis=1)


def _ab_fwd(u, x, vec, conv, wpool, scale, wout, tm, exch=None):
    t_total, d = x.shape
    nu = u.shape[1]
    da = nu // 4
    nsteps = t_total // tm

    def body(up_ref, uc_ref, un_ref, x_ref, vec_ref, conv_ref, wpool_ref, scale_ref, wout_ref,
             y_ref, x2_ref, q_ext, p_ext):
        i = pl.program_id(0)
        bg, cq, _, ybpre = _ab_core(up_ref, uc_ref, un_ref, conv_ref, wpool_ref, q_ext, p_ext,
                                    i, nsteps, tm, t_total)
        cat = jnp.concatenate([bg * cq, ybpre * scale_ref[...]], axis=1).astype(BF16)
        y = _dot(cat, wout_ref[...])
        y_ref[...] = y.astype(BF16)
        x2_ref[...] = x_ref[...] + vec_ref[0:1, :] * y

    return _call(
        body, name="ab_fwd", nsteps=nsteps,
        in_specs=[*_halo_specs(tm, nu, t_total), _rows(tm, d), _full(vec.shape), _full(conv.shape),
                  _full(wpool.shape), _full(scale.shape), _VM],
        out_specs=[_rows(tm, d), _rows(tm, d)],
        out_shape=[jax.ShapeDtypeStruct((t_total, d), BF16), jax.ShapeDtypeStruct((t_total, d), F32)],
        scratch_shapes=[pltpu.VMEM((tm + 2 * HALO, da), F32), pltpu.VMEM((tm + 2 * HALO, da), F32)],
        args=(u, u, u, x, vec, conv, wpool, scale, wout), exch=exch)


def _glu_ext(up_ref, uc_ref, un_ref, z_ext, i, nsteps, tm):
    dz = uc_ref.shape[1] // 2

    def glu(ref):
        return ref[:, 0:dz].astype(F32) * _sigmoid(ref[:, dz:2 * dz].astype(F32))

    _fill_ext(z_ext, glu(up_ref), glu(uc_ref), glu(un_ref), i, nsteps, tm)


def _layer_norm_stats(zc):
    mu = jnp.mean(zc, axis=-1, keepdims=True)
    dlt = zc - mu
    rstd = lax.rsqrt(jnp.mean(dlt * dlt, axis=-1, keepdims=True) + LN_EPS)
    return dlt * rstd, rstd


def _cf_fwd(u, x, vec, wdw, wpw2, tm, exch=None):
    t_total, d = x.shape
    nu = u.shape[1]
    nsteps = t_total // tm
    left = (CONF_KERNEL - 1) // 2

    def body(up_ref, uc_ref, un_ref, x_ref, vec_ref, wdw_ref, wpw2_ref, zc_ref, y_ref, x2_ref, z_ext, sh_ref,
             zc_buf):
        i = pl.program_id(0)
        _glu_ext(up_ref, uc_ref, un_ref, z_ext, i, nsteps, tm)
        for lo in range(0, d, CONV_COLS):
            hi = lo + CONV_COLS
            _fill_shifts(sh_ref, z_ext, lo, hi, tm)
            acc = wdw_ref[0:1, lo:hi] * _shifted(sh_ref, -left, tm)
            for k in range(1, CONF_KERNEL):
                acc = acc + wdw_ref[k:k + 1, lo:hi] * _shifted(sh_ref, k - left, tm)
            zc_buf[:, lo:hi] = acc
        zc = zc_buf[...] + vec_ref[1:2, :]
        zc_ref[...] = zc.astype(BF16)
        zn, _ = _layer_norm_stats(zc)
        zl = zn * vec_ref[2:3, :] + vec_ref[3:4, :]
        zs = zl * _sigmoid(zl)
        y = _dot(zs.astype(BF16), wpw2_ref[...]) + vec_ref[4:5, :]
        y_ref[...] = y.astype(BF16)
        x2_ref[...] = x_ref[...] + vec_ref[0:1, :] * y

    return _call(
        body, name="cf_fwd", nsteps=nsteps,
        in_specs=[*_halo_specs(tm, nu, t_total), _rows(tm, d), _full(vec.shape), _full(wdw.shape), _VM],
        out_specs=[_rows(tm, d), _rows(tm, d), _rows(tm, d)],
        out_shape=[jax.ShapeDtypeStruct((t_total, d), BF16), jax.ShapeDtypeStruct((t_total, d), BF16),
                   jax.ShapeDtypeStruct((t_total, d), F32)],
        scratch_shapes=[pltpu.VMEM((tm + 2 * HALO, d), F32), _shift_scratch(tm), pltpu.VMEM((tm, d), F32)],
        args=(u, u, u, x, vec, wdw, wpw2), exch=exch)


def _ffn_chunks(f, width=FFN_CHUNK):
    return [(lo, min(lo + width, f)) for lo in range(0, f, width)]


def _ffn_fwd(x2, vec, wg, wu, wd, tm, exch=None):
    t_total, d = x2.shape
    f = wg.shape[0]

    def body(x_ref, vec_ref, wg_ref, wu_ref, wd_ref, a_ref, b_ref, f_ref, x3_ref):
        xv = x_ref[...]
        h = _norm_mod(xv, vec_ref[0:1, :], vec_ref[1:2, :], vec_ref[2:3, :]).astype(BF16)
        y = None
        for lo, hi in _ffn_chunks(f):
            a = _dot_nt(h, wg_ref[lo:hi, :])
            b = _dot_nt(h, wu_ref[lo:hi, :])
            a_ref[:, lo:hi] = a.astype(BF16)
            b_ref[:, lo:hi] = b.astype(BF16)
            s = (a * _sigmoid(a) * b).astype(BF16)
            part = _dot(s, wd_ref[lo:hi, :])
            y = part if y is None else y + part
        f_ref[...] = y.astype(BF16)
        x3_ref[...] = xv + vec_ref[3:4, :] * y

    return _call(
        body, name="ffn_fwd", nsteps=t_total // tm,
        in_specs=[_rows(tm, d), _full(vec.shape), _VM, _VM, _VM],
        out_specs=[_rows(tm, f), _rows(tm, f), _rows(tm, d), _rows(tm, d)],
        out_shape=[jax.ShapeDtypeStruct((t_total, f), BF16), jax.ShapeDtypeStruct((t_total, f), BF16),
                   jax.ShapeDtypeStruct((t_total, d), BF16), jax.ShapeDtypeStruct((t_total, d), F32)],
        args=(x2, vec, wg, wu, wd), exch=exch)


def _final_fwd_bwd(x, target, vec, tm):
    t_total, d = x.shape

    def body(x_ref, t_ref, vec_ref, dx_ref, acc_ref):
        @pl.when(pl.program_id(0) == 0)
        def _():
            acc_ref[...] = jnp.zeros_like(acc_ref)

        g = vec_ref[0:1, :]
        xhat, r = _rms(x_ref[...])
        e = xhat * g - t_ref[...]
        acc_ref[1:2, :] += jnp.zeros((1, d), F32) + 0.5 * jnp.sum(jnp.mean(e * e, axis=-1, keepdims=True))
        dout = e * (1.0 / d)
        acc_ref[0:1, :] += _colsum(dout * xhat)
        dxn = dout * g
        dx_ref[...] = r * (dxn - xhat * jnp.mean(dxn * xhat, axis=-1, keepdims=True))

    return _call(
        body, name="final_fwd_bwd", nsteps=t_total // tm,
        in_specs=[_rows(tm, d), _rows(tm, d), _full(vec.shape)],
        out_specs=[_rows(tm, d), _VM],
        out_shape=[jax.ShapeDtypeStruct((t_total, d), F32), jax.ShapeDtypeStruct((8, d), F32)],
        args=(x, target, vec))


def _zero_at_start(*refs):
    @pl.when(pl.program_id(0) == 0)
    def _():
        for ref in refs:
            ref[...] = jnp.zeros_like(ref)


def _emit_bf16_at_end(nsteps, acc_ref, out_ref):
    @pl.when(pl.program_id(0) == nsteps - 1)
    def _():
        out_ref[...] = acc_ref[...].astype(BF16)


def _ffn_bwd_down(dx3, fout, a, b, vec, wd, tm, exch=None):
    t_total, d = dx3.shape
    f = a.shape[1]

    def body(dx_ref, f_ref, a_ref, b_ref, vec_ref, wd_ref, da_ref, db_ref, dwd_out, acc_ref, dwd_ref):
        _zero_at_start(dwd_ref, acc_ref)
        dx = dx_ref[...]
        acc_ref[0:1, :] += _colsum(dx * f_ref[...].astype(F32))
        dy = (dx * vec_ref[0:1, :]).astype(BF16)
        for lo, hi in _ffn_chunks(f, FFN_CHUNK // 3):
            av = a_ref[:, lo:hi].astype(F32)
            bv = b_ref[:, lo:hi].astype(F32)
            sg = _sigmoid(av)
            silu = av * sg
            ds = _dot_nt(dy, wd_ref[lo:hi, :])
            da_ref[:, lo:hi] = (ds * bv * (sg * (1.0 + av * (1.0 - sg)))).astype(BF16)
            db_ref[:, lo:hi] = (ds * silu).astype(BF16)
            dwd_ref[lo:hi, :] += _dot_tn((silu * bv).astype(BF16), dy)
        _emit_bf16_at_end(t_total // tm, dwd_ref, dwd_out)

    return _call(
        body, name="ffn_bwd_down", nsteps=t_total // tm,
        in_specs=[_rows(tm, d), _rows(tm, d), _rows(tm, f), _rows(tm, f), _full(vec.shape), _VM],
        out_specs=[_rows(tm, f), _rows(tm, f), _VM, _VM],
        out_shape=[jax.ShapeDtypeStruct((t_total, f), BF16), jax.ShapeDtypeStruct((t_total, f), BF16),
                   jax.ShapeDtypeStruct(wd.shape, BF16), jax.ShapeDtypeStruct((8, d), F32)],
        scratch_shapes=[pltpu.VMEM(wd.shape, F32)],
        args=(dx3, fout, a, b, vec, wd), exch=exch)


def _ffn_bwd_up(da, db, x2, dx3, vec, wg, wu, tm, exch=None):
    t_total, d = x2.shape
    f = da.shape[1]

    def body(da_ref, db_ref, x_ref, dx_ref, vec_ref, wg_ref, wu_ref, dx2_ref, dwg_out, dwu_out, acc_ref,
             dwg_ref, dwu_ref):
        _zero_at_start(dwg_ref, dwu_ref, acc_ref)
        xv = x_ref[...]
        g, sh, sc = vec_ref[0:1, :], vec_ref[1:2, :], vec_ref[2:3, :]
        h = _norm_mod(xv, g, sh, sc).astype(BF16)
        dav = da_ref[...]
        dbv = db_ref[...]
        dwg_ref[...] += _dot_tn(dav, h)
        dwu_ref[...] += _dot_tn(dbv, h)
        dh = _dot(dav, wg_ref[...]) + _dot(dbv, wu_ref[...])
        dxn, dsh, dsc, dg = _norm_mod_bwd(dh, xv, g, sc)
        acc_ref[0:1, :] += dsh
        acc_ref[1:2, :] += dsc
        acc_ref[2:3, :] += dg
        dx2_ref[...] = dx_ref[...] + dxn
        _emit_bf16_at_end(t_total // tm, dwg_ref, dwg_out)
        _emit_bf16_at_end(t_total // tm, dwu_ref, dwu_out)

    return _call(
        body, name="ffn_bwd_up", nsteps=t_total // tm,
        in_specs=[_rows(tm, f), _rows(tm, f), _rows(tm, d), _rows(tm, d), _full(vec.shape), _VM, _VM],
        out_specs=[_rows(tm, d), _VM, _VM, _VM],
        out_shape=[jax.ShapeDtypeStruct((t_total, d), F32), jax.ShapeDtypeStruct(wg.shape, BF16),
                   jax.ShapeDtypeStruct(wu.shape, BF16), jax.ShapeDtypeStruct((8, d), F32)],
        scratch_shapes=[pltpu.VMEM(wg.shape, F32), pltpu.VMEM(wu.shape, F32)],
        args=(da, db, x2, dx3, vec, wg, wu), exch=exch, vmem_limit=VMEM_LIMIT_WIDE)


def _ab_bwd_out(dx, y, u, vec, conv, wpool, scale, wout, tm, exch=None):
    t_total, d = dx.shape
    nu = u.shape[1]
    da = nu // 4
    gw = da // len(POOL_WINDOWS)
    nsteps = t_total // tm

    def body(dx_ref, y_ref, up_ref, uc_ref, un_ref, vec_ref, conv_ref, wpool_ref, scale_ref, wout_ref,
             dpre_ref, dwout_out, dwpool_ref, acc_ref, q_ext, p_ext, dwout_ref):
        _zero_at_start(dwout_ref, dwpool_ref, acc_ref)
        i = pl.program_id(0)
        dxv = dx_ref[...]
        acc_ref[0:1, :] += _colsum(dxv * y_ref[...].astype(F32))
        dy = (dxv * vec_ref[0:1, :]).astype(BF16)
        bg, cq, pooled, ybpre = _ab_core(up_ref, uc_ref, un_ref, conv_ref, wpool_ref, q_ext, p_ext,
                                         i, nsteps, tm, t_total)
        cat = jnp.concatenate([bg * cq, ybpre * scale_ref[...]], axis=1).astype(BF16)
        dwout_ref[...] += _dot_tn(cat, dy)
        dcat = _dot_nt(dy, wout_ref[...])
        dya = dcat[:, 0:da]
        dyb = dcat[:, da:2 * da]
        acc_ref[1:2, 0:da] += _colsum(dyb * ybpre)
        dybpre = (dyb * scale_ref[...]).astype(BF16)
        dpooled = []
        for g in range(len(POOL_WINDOWS)):
            dg = dybpre[:, g * gw:(g + 1) * gw]
            dwpool_ref[g] += _dot_tn(pooled[g], dg)
            dpooled.append(_dot_nt(dg, wpool_ref[g]))
        dpre_ref[...] = jnp.concatenate([dya * cq, dya * bg] + dpooled, axis=1).astype(BF16)
        _emit_bf16_at_end(nsteps, dwout_ref, dwout_out)

    return _call(
        body, name="ab_bwd_out", nsteps=nsteps,
        in_specs=[_rows(tm, d), _rows(tm, d), *_halo_specs(tm, nu, t_total), _full(vec.shape),
                  _full(conv.shape), _full(wpool.shape), _full(scale.shape), _VM],
        out_specs=[_rows(tm, 3 * da), _VM, _VM, _VM],
        out_shape=[jax.ShapeDtypeStruct((t_total, 3 * da), BF16), jax.ShapeDtypeStruct(wout.shape, BF16),
                   jax.ShapeDtypeStruct(wpool.shape, F32), jax.ShapeDtypeStruct((8, d), F32)],
        scratch_shapes=[pltpu.VMEM((tm + 2 * HALO, da), F32), pltpu.VMEM((tm + 2 * HALO, da), F32),
                        pltpu.VMEM(wout.shape, F32)],
        args=(dx, y, u, u, u, vec, conv, wpool, scale, wout), exch=exch)


def _ab_bwd_in(dpre, u, x, dx, vec, conv, win, tm, exch=None):
    t_total, d = x.shape
    nu = u.shape[1]
    da = nu // 4
    gw = da // len(POOL_WINDOWS)
    nsteps = t_total // tm

    def body(dp_ref, dc_ref, dn_ref, up_ref, uc_ref, un_ref, x_ref, dx_ref, vec_ref, conv_ref, win_ref,
             dxin_ref, dwin_out, dconv_ref, acc_ref, dcq_ext, q_ext, dpl_ext, dwin_ref):
        _zero_at_start(dwin_ref, dconv_ref, acc_ref)
        i = pl.program_id(0)

        def ucols(ref, k):
            return ref[:, k * da:(k + 1) * da].astype(F32)

        def dcols(ref, k):
            return ref[:, k * da:(k + 1) * da].astype(F32)

        _fill_ext(dcq_ext, dcols(dp_ref, 1), dcols(dc_ref, 1), dcols(dn_ref, 1), i, nsteps, tm)
        _fill_ext(q_ext, ucols(up_ref, 1) * ucols(up_ref, 2), ucols(uc_ref, 1) * ucols(uc_ref, 2),
                  ucols(un_ref, 1) * ucols(un_ref, 2), i, nsteps, tm)
        _fill_ext(dpl_ext, dcols(dp_ref, 2), dcols(dc_ref, 2), dcols(dn_ref, 2), i, nsteps, tm)
        dq = (conv_ref[0:1, :] * dcq_ext[HALO + 1:HALO + 1 + tm, :] + conv_ref[1:2, :] * dcq_ext[HALO:HALO + tm, :]
              + conv_ref[2:3, :] * dcq_ext[HALO - 1:HALO - 1 + tm, :])
        dcq = dcq_ext[HALO:HALO + tm, :]
        for k in range(3):
            dconv_ref[k:k + 1, :] += _colsum(dcq * q_ext[HALO + k - 1:HALO + k - 1 + tm, :])
        dcg = dq * ucols(uc_ref, 2)
        dv = dq * ucols(uc_ref, 1)
        t_ext = i * tm - HALO + lax.broadcasted_iota(jnp.int32, (tm + 2 * HALO, 1), 0)
        dps = []
        for g, wdw in enumerate(POOL_WINDOWS):
            left = wdw // 2
            right = wdw - 1 - left
            lo, hi = g * gw, (g + 1) * gw
            dpg = dpl_ext[HALO:HALO + tm, lo:hi]
            dpl_ext[:, lo:hi] = dpl_ext[:, lo:hi] / _window_count(t_ext, wdw, t_total)
            s = dpl_ext[HALO - right:HALO - right + tm, lo:hi]
            for o in range(-right + 1, left + 1):
                s = s + dpl_ext[HALO + o:HALO + o + tm, lo:hi]
            dps.append(s - dpg)
        du = jnp.concatenate([dcols(dc_ref, 0), dcg, dv] + dps, axis=1).astype(BF16)
        xv = x_ref[...]
        g, sh, sc = vec_ref[0:1, :], vec_ref[1:2, :], vec_ref[2:3, :]
        h = _norm_mod(xv, g, sh, sc).astype(BF16)
        dh = _in_proj_bwd(h, du, win_ref, dwin_ref)
        dxn, dsh, dsc, dg = _norm_mod_bwd(dh, xv, g, sc)
        acc_ref[0:1, :] += dsh
        acc_ref[1:2, :] += dsc
        acc_ref[2:3, :] += dg
        dxin_ref[...] = dx_ref[...] + dxn
        _emit_bf16_at_end(nsteps, dwin_ref, dwin_out)

    ext = pltpu.VMEM((tm + 2 * HALO, da), F32)
    return _call(
        body, name="ab_bwd_in", nsteps=nsteps,
        in_specs=[*_halo_specs(tm, 3 * da, t_total), *_halo_specs(tm, nu, t_total), _rows(tm, d), _rows(tm, d),
                  _full(vec.shape), _full(conv.shape), _VM],
        out_specs=[_rows(tm, d), _VM, _VM, _VM],
        out_shape=[jax.ShapeDtypeStruct((t_total, d), F32), jax.ShapeDtypeStruct(win.shape, BF16),
                   jax.ShapeDtypeStruct((8, da), F32), jax.ShapeDtypeStruct((8, d), F32)],
        scratch_shapes=[ext, ext, ext, pltpu.VMEM(win.shape, F32)],
        args=(dpre, dpre, dpre, u, u, u, x, dx, vec, conv, win), exch=exch)


def _cf_bwd_out(dx, y, zc, vec, wpw2, tm, exch=None):
    t_total, d = dx.shape

    def body(dx_ref, y_ref, zc_ref, vec_ref, w_ref, dzc_ref, dw_out, acc_ref, dw_ref):
        _zero_at_start(dw_ref, acc_ref)
        dxv = dx_ref[...]
        acc_ref[0:1, :] += _colsum(dxv * y_ref[...].astype(F32))
        dyf = dxv * vec_ref[0:1, :]
        acc_ref[1:2, :] += _colsum(dyf)
        dy = dyf.astype(BF16)
        zn, rstd = _layer_norm_stats(zc_ref[...].astype(F32))
        lng = vec_ref[1:2, :]
        zl = zn * lng + vec_ref[2:3, :]
        sg = _sigmoid(zl)
        dw_ref[...] += _dot_tn((zl * sg).astype(BF16), dy)
        dzl = _dot_nt(dy, w_ref[...]) * (sg * (1.0 + zl * (1.0 - sg)))
        acc_ref[2:3, :] += _colsum(dzl * zn)
        acc_ref[3:4, :] += _colsum(dzl)
        dzn = dzl * lng
        dzc = rstd * (dzn - jnp.mean(dzn, axis=-1, keepdims=True)
                      - zn * jnp.mean(dzn * zn, axis=-1, keepdims=True))
        acc_ref[4:5, :] += _colsum(dzc)
        dzc_ref[...] = dzc.astype(BF16)
        _emit_bf16_at_end(t_total // tm, dw_ref, dw_out)

    return _call(
        body, name="cf_bwd_out", nsteps=t_total // tm,
        in_specs=[_rows(tm, d), _rows(tm, d), _rows(tm, d), _full(vec.shape), _VM],
        out_specs=[_rows(tm, d), _VM, _VM],
        out_shape=[jax.ShapeDtypeStruct((t_total, d), BF16), jax.ShapeDtypeStruct(wpw2.shape, BF16),
                   jax.ShapeDtypeStruct((8, d), F32)],
        scratch_shapes=[pltpu.VMEM(wpw2.shape, F32)],
        args=(dx, y, zc, vec, wpw2), exch=exch)


def _cf_bwd_in(dzc, u, x, dx, vec, wdw, wpw1, tm, exch=None):
    t_total, d = x.shape
    nu = u.shape[1]
    nsteps = t_total // tm
    left = (CONF_KERNEL - 1) // 2

    def body(dp_ref, dc_ref, dn_ref, up_ref, uc_ref, un_ref, x_ref, dx_ref, vec_ref, wdw_ref, w_ref,
             dxin_ref, dw_out, dwdw_ref, db1_ref, acc_ref, dzc_ext, z_ext, sh_ref, dz_buf, dw_ref):
        _zero_at_start(dw_ref, dwdw_ref, db1_ref, acc_ref)
        i = pl.program_id(0)
        _fill_ext(dzc_ext, dp_ref[...].astype(F32), dc_ref[...].astype(F32), dn_ref[...].astype(F32),
                  i, nsteps, tm)
        _glu_ext(up_ref, uc_ref, un_ref, z_ext, i, nsteps, tm)
        for lo in range(0, d, CONV_COLS):
            hi = lo + CONV_COLS
            _fill_shifts(sh_ref, dzc_ext, lo, hi, tm)
            acc = wdw_ref[0:1, lo:hi] * _shifted(sh_ref, left, tm)
            for k in range(1, CONF_KERNEL):
                acc = acc + wdw_ref[k:k + 1, lo:hi] * _shifted(sh_ref, left - k, tm)
            dz_buf[:, lo:hi] = acc
            dzc = dzc_ext[HALO:HALO + tm, lo:hi]
            _fill_shifts(sh_ref, z_ext, lo, hi, tm)
            for k in range(CONF_KERNEL):
                dwdw_ref[k:k + 1, lo:hi] += _colsum(dzc * _shifted(sh_ref, k - left, tm))
        dz = dz_buf[...]
        av = uc_ref[:, 0:d].astype(F32)
        sg = _sigmoid(uc_ref[:, d:2 * d].astype(F32))
        duf = jnp.concatenate([dz * sg, dz * av * sg * (1.0 - sg)], axis=1)
        db1_ref[0:1, :] += _colsum(duf)
        du = duf.astype(BF16)
        xv = x_ref[...]
        g, sh, sc = vec_ref[0:1, :], vec_ref[1:2, :], vec_ref[2:3, :]
        h = _norm_mod(xv, g, sh, sc).astype(BF16)
        dh = _in_proj_bwd(h, du, w_ref, dw_ref)
        dxn, dsh, dsc, dg = _norm_mod_bwd(dh, xv, g, sc)
        acc_ref[0:1, :] += dsh
        acc_ref[1:2, :] += dsc
        acc_ref[2:3, :] += dg
        dxin_ref[...] = dx_ref[...] + dxn
        _emit_bf16_at_end(nsteps, dw_ref, dw_out)

    ext = pltpu.VMEM((tm + 2 * HALO, d), F32)
    return _call(
        body, name="cf_bwd_in", nsteps=nsteps,
        in_specs=[*_halo_specs(tm, d, t_total), *_halo_specs(tm, nu, t_total), _rows(tm, d), _rows(tm, d),
                  _full(vec.shape), _full(wdw.shape), _VM],
        out_specs=[_rows(tm, d), _VM, _VM, _VM, _VM],
        out_shape=[jax.ShapeDtypeStruct((t_total, d), F32), jax.ShapeDtypeStruct(wpw1.shape, BF16),
                   jax.ShapeDtypeStruct((32, d), F32), jax.ShapeDtypeStruct((8, nu), F32),
                   jax.ShapeDtypeStruct((8, d), F32)],
        scratch_shapes=[ext, ext, _shift_scratch(tm), pltpu.VMEM((tm, d), F32), pltpu.VMEM(wpw1.shape, F32)],
        args=(dzc, dzc, dzc, u, u, u, x, dx, vec, wdw, wpw1), exch=exch)


def _mod_fwd(c_all, w_mod, b_cols):
    nl, d, ncol = w_mod.shape
    nb = c_all.shape[0]

    def body(c_ref, w_ref, b_ref, o_ref):
        cv = c_ref[...]
        ca = cv * _sigmoid(cv)
        o_ref[0] = jnp.dot(ca, w_ref[0], preferred_element_type=F32, precision=HIGHEST) + b_ref[0]

    return _pcall(
        body, name="mod_fwd", grid=(nl,),
        in_specs=[_full(c_all.shape), pl.BlockSpec((1, d, ncol), lambda l: (l, 0, 0)),
                  pl.BlockSpec((1, 1, ncol), lambda l: (l, 0, 0))],
        out_specs=pl.BlockSpec((1, nb, ncol), lambda l: (l, 0, 0)),
        out_shape=jax.ShapeDtypeStruct((nl, nb, ncol), F32),
        compiler_params=_seq_params(),
    )(c_all, w_mod, b_cols.reshape(nl, 1, ncol))


def _mod_bwd(c_all_t, dmod_cols):
    d, nb = c_all_t.shape
    nl, _, ncol = dmod_cols.shape

    def body(c_ref, dm_ref, o_ref):
        cv = c_ref[...]
        ca = cv * _sigmoid(cv)
        o_ref[0] = jnp.dot(ca, dm_ref[0], preferred_element_type=F32, precision=HIGHEST)

    return _pcall(
        body, name="mod_bwd", grid=(nl,),
        in_specs=[_full(c_all_t.shape), pl.BlockSpec((1, nb, ncol), lambda l: (l, 0, 0))],
        out_specs=pl.BlockSpec((1, d, ncol), lambda l: (l, 0, 0)),
        out_shape=jax.ShapeDtypeStruct((nl, d, ncol), F32),
        compiler_params=_seq_params(),
    )(c_all_t, dmod_cols)


def _row_block(r, c):
    if r * c <= EW_BLOCK_ELEMS:
        return r
    best = None
    for br in range(8, r, 8):
        if r % br == 0 and br * c <= EW_BLOCK_ELEMS:
            best = br
    assert best is not None, (r, c)
    return best


def _as2d(a):
    return a.reshape(-1, a.shape[-1])


def _adamw(w, gparts, m, v):
    shape = w.shape
    w2, m2, v2 = _as2d(w), _as2d(m), _as2d(v)
    g2 = [_as2d(g) for g in gparts]
    r, c = w2.shape
    br = _row_block(r, c)
    ng = len(g2)

    def body(*refs):
        w_ref, m_ref, v_ref = refs[0:3]
        g_refs = refs[3:3 + ng]
        g = g_refs[0][...]
        for gr in g_refs[1:]:
            g = g + gr[...]
        _adamw_update(g, w_ref[...], m_ref[...], v_ref[...], refs[3 + ng:])

    spec = pl.BlockSpec((br, c), lambda i: (i, 0))
    outs = _pcall(
        body, name="adamw", grid=(r // br,),
        in_specs=[spec] * (3 + ng), out_specs=[spec] * 4,
        out_shape=[jax.ShapeDtypeStruct((r, c), F32)] * 4,
        compiler_params=_seq_params(),
    )(w2, m2, v2, *g2)
    return tuple(o.reshape(shape) for o in outs)


def _adamw_update(g, w, m, v, out_refs):
    go_ref, d_ref, mo_ref, vo_ref = out_refs
    mn = ADAM_B1 * m + (1.0 - ADAM_B1) * g
    vn = ADAM_B2 * v + (1.0 - ADAM_B2) * (g * g)
    m_hat = mn / (1.0 - ADAM_B1 ** ADAM_STEP)
    v_hat = vn / (1.0 - ADAM_B2 ** ADAM_STEP)
    go_ref[...] = g.reshape(go_ref.shape)
    d_ref[...] = (-ADAM_LR * (m_hat / (jnp.sqrt(v_hat) + ADAM_EPS) + ADAM_WD * w)).reshape(d_ref.shape)
    mo_ref[...] = mn.reshape(mo_ref.shape)
    vo_ref[...] = vn.reshape(vo_ref.shape)


def _adamw_partials(w, partials, m, v):
    nl, a, b = w.shape
    br = _row_block(a, b)
    nb = a // br

    def body(*refs):
        w_ref, m_ref, v_ref = refs[0:3]
        p_refs = refs[3:3 + nl]
        out_refs = refs[3 + nl:]
        for layer in range(nl):
            @pl.when(pl.program_id(0) == layer)
            def _(layer=layer):
                halves = []
                for core in range(2):
                    acc = p_refs[layer][core, 0].astype(F32)
                    for chip in range(1, N_CHIPS):
                        acc = acc + p_refs[layer][core, chip].astype(F32)
                    halves.append(acc)
                _adamw_update(halves[0] + halves[1], w_ref[...], m_ref[...], v_ref[...], out_refs)

    def part_spec(layer):
        def index(l, i):
            return 0, 0, jnp.where(l == layer, i, jnp.where(l < layer, 0, nb - 1)), 0
        return pl.BlockSpec((2, N_CHIPS, br, b), index)

    spec = pl.BlockSpec((br, b), lambda l, i: (l * nb + i, 0))
    outs = _pcall(
        body, name="adamw_partials", grid=(nl, nb),
        in_specs=[spec] * 3 + [part_spec(layer) for layer in range(nl)], out_specs=[spec] * 4,
        out_shape=[jax.ShapeDtypeStruct((nl * a, b), F32)] * 4,
        compiler_params=pltpu.CompilerParams(dimension_semantics=("arbitrary", "arbitrary"),
                                             vmem_limit_bytes=VMEM_LIMIT),
    )(_as2d(w), _as2d(m), _as2d(v), *partials)
    return tuple(o.reshape(w.shape) for o in outs)


def _allgather8(block, with_sum, exch=None):
    m_per, n = block.shape
    nx = 0 if exch is None else len(exch.arrs)
    nvm = 2 if with_sum else 1

    def body(x_ref, *rest):
        xin, out_ref = rest[:nx], rest[nx]
        sum_ref = rest[nx + 1] if with_sum else None
        xout = rest[nx + nvm:2 * nx + nvm]
        send_sems, recv_sems, local_sem = rest[2 * nx + nvm:2 * nx + nvm + 3]
        xsems = rest[2 * nx + nvm + 3:]
        if exch is not None:
            exch.start(xin, xout, xsems)
        x, y, c = _place()
        me, sibling = (x, y, c), (x, y, 1 - c)
        chips = [(1 - x, y), (x, 1 - y), (1 - x, 1 - y)]

        def rows(px, py, pc):
            return out_ref.at[pl.ds((4 * px + 2 * py + pc) * m_per, m_per), :]

        def copy(k, blk, to, src=None):
            return pltpu.make_async_remote_copy(
                src_ref=rows(*blk) if src is None else src, dst_ref=rows(*blk),
                send_sem=send_sems.at[k], recv_sem=recv_sems.at[k], device_id=to, device_id_type=MESH)

        mine = pltpu.make_async_copy(x_ref, rows(*me), local_sem)
        mine.start()
        first = [copy(0, me, sibling, src=x_ref)]
        first += [copy(1 + j, me, (*chip, c), src=x_ref) for j, chip in enumerate(chips)]
        for cp in first:
            cp.start()
        passed = [copy(4 + j, (*chip, c), sibling) for j, chip in enumerate(chips)]
        for j, chip in enumerate(chips):
            copy(1 + j, (*chip, c), me).wait_recv()
            passed[j].start()
        copy(0, sibling, me).wait_recv()
        for j, chip in enumerate(chips):
            copy(4 + j, (*chip, 1 - c), me).wait_recv()
        for cp in first + passed:
            cp.wait_send()
        mine.wait()
        if exch is not None:
            exch.mid(xin, xout, xsems)
            exch.wait(xin, xout, xsems)
        if with_sum:
            acc = out_ref[0:m_per, :]
            for k in range(1, N_DEV):
                acc = acc + out_ref[k * m_per:(k + 1) * m_per, :]
            sum_ref[...] = acc

    out_shape = [jax.ShapeDtypeStruct((N_DEV * m_per, n), F32)]
    out_specs = [_VM]
    if with_sum:
        out_shape.append(jax.ShapeDtypeStruct((m_per, n), F32))
        out_specs.append(_VM)
    res = _pcall(
        body, name=("allgather8_sum" if with_sum else "allgather8") + ("" if exch is None else "_" + exch.tag),
        in_specs=[_VM] + [_ANY] * nx, out_specs=out_specs + [_ANY] * nx,
        out_shape=out_shape + ([] if exch is None else exch.out_shapes()),
        scratch_shapes=[pltpu.SemaphoreType.DMA((7,)), pltpu.SemaphoreType.DMA((7,)), pltpu.SemaphoreType.DMA]
        + ([] if exch is None else exch.sems()),
        compiler_params=pltpu.CompilerParams(vmem_limit_bytes=VMEM_LIMIT),
    )(block, *([] if exch is None else exch.arrs))
    return list(res[:nvm]), list(res[nvm:])


def _my_cols(full, chip):
    w = full.shape[-1] // N_CHIPS
    return lax.dynamic_slice_in_dim(full, chip * w, w, axis=full.ndim - 1)


def _pad_rows(a, rows):
    return jnp.pad(a, ((0, rows - a.shape[0]), (0, 0)))


def _to_lanes(a):
    flat = a.reshape(-1)
    n = -(-flat.shape[0] // (8 * LANES)) * (8 * LANES)
    return jnp.pad(flat, (0, n - flat.shape[0])).reshape(-1, LANES)


class _Packer:
    def __init__(self):
        self.items = []
        self.rows = 0

    def add(self, name, a):
        lanes = _to_lanes(a)
        self.items.append((name, self.rows, a.shape, lanes))
        self.rows += lanes.shape[0]

    def pack(self):
        total = -(-self.rows // 8) * 8
        return _pad_rows(jnp.concatenate([it[3] for it in self.items], axis=0), total)

    def unpack(self, buf):
        out = {}
        for name, row, shape, lanes in self.items:
            size = 1
            for s in shape:
                size *= s
            out[name] = buf[row:row + lanes.shape[0]].reshape(-1)[:size].reshape(shape)
        return out


TM_SEQ = 512
TM_FFN = 256
TM_FFN_FWD = 512


LAYER_KEYS = ("in", "out", "gate", "up", "down")
BLOCKED_KEYS = ("in",)
TRANSPOSED = ("ffn_w_gate", "ffn_w_up")


def _layer_big_names(layer):
    i = layer // 2
    mix = (("ab_w_in", i), ("ab_w_out", i)) if layer % 2 == 0 else (("cf_w_pw1", i), ("cf_w_pw2", i))
    return dict(zip(LAYER_KEYS, mix + (("ffn_w_gate", layer), ("ffn_w_up", layer), ("ffn_w_down", layer))))


def _unpack_weight(key, g):
    g = g.reshape(N_CHIPS, -1, g.shape[-1])
    return g if key in BLOCKED_KEYS else g.reshape(-1, g.shape[-1])


def _chunk_grad(key, dw):
    parts = dw if key in BLOCKED_KEYS else dw.reshape(N_CHIPS, -1, dw.shape[-1])
    return parts.astype(BF16)


def _local_step(x, target, mods, p, shards, first):
    t_total, d = x.shape
    depth = mods.shape[0]
    tm = min(TM_SEQ, t_total)
    tmf = min(TM_FFN, t_total)
    saved = []
    xin = x
    weights = [{} for _ in range(depth)]

    def carried(stage, layer):
        if layer == 0:
            return {"in": (0, ("out", "gate")), "mix": (0, ("up", "down")), "ffn": (1, ("in", "out", "gate", "up"))}[stage]
        return {"in": (layer, ("down",)), "mix": (layer + 1, ("in", "out")), "ffn": (layer + 1, ("gate", "up"))}[stage]

    def gather(stage, layer):
        of, keys = carried(stage, layer)
        if of >= depth:
            return None
        return _Gather([shards[of][k].reshape(2, -1, shards[of][k].shape[-1]) for k in keys])

    def keep(stage, layer, arrs):
        of, keys = carried(stage, layer)
        for k, g in zip(keys, arrs):
            weights[of][k] = _unpack_weight(k, g)

    for k, g in first.items():
        weights[0][k] = _unpack_weight(k, g)
    for layer in range(depth):
        i = layer // 2
        lw = weights[layer]
        sh1, sc1, g1, sh2, sc2, g2 = (mods[layer, k:k + 1] for k in range(6))
        vec_in = jnp.concatenate([p["norm_mix_g"][layer:layer + 1], sh1, sc1], axis=0)
        bias = None if layer % 2 == 0 else p["cf_b_pw1"][i:i + 1]
        (u,), arrived = _in_proj(xin, vec_in, lw["in"], bias, tm, exch=gather("in", layer))
        keep("in", layer, arrived)
        if layer % 2 == 0:
            (y, x2), arrived = _ab_fwd(u, xin, g1, p["ab_conv"][i], p["ab_w_pool"][i].astype(BF16),
                                       p["ab_pool_scale"][i:i + 1], lw["out"], tm, exch=gather("mix", layer))
            zc = None
        else:
            vec_cf = jnp.concatenate([g1, p["cf_b_dw"][i:i + 1], p["cf_ln_g"][i:i + 1], p["cf_ln_b"][i:i + 1],
                                      p["cf_b_pw2"][i:i + 1]], axis=0)
            (zc, y, x2), arrived = _cf_fwd(u, xin, vec_cf, _pad_rows(p["cf_w_dw"][i], 32), lw["out"], tm,
                                           exch=gather("mix", layer))
        keep("mix", layer, arrived)
        vec_ffn = jnp.concatenate([p["norm_ffn_g"][layer:layer + 1], sh2, sc2, g2], axis=0)
        (a, b, fout, x3), arrived = _ffn_fwd(x2, vec_ffn, lw["gate"], lw["up"], lw["down"], min(TM_FFN_FWD, t_total),
                                             exch=gather("ffn", layer))
        keep("ffn", layer, arrived)
        saved.append((xin, u, y, zc, x2, a, b, fout))
        xin = x3

    (dx, fin), _ = _final_fwd_bwd(xin, target, p["final_norm_g"].reshape(1, d), tm)
    grads = {"final_norm_g": fin[0], "loss": fin[1, 0:1]}
    per_layer = {k: [None] * depth for k in ("norm_mix_g", "norm_ffn_g")}
    half = {k: [None] * (depth // 2) for k in (
        "ab_conv", "ab_w_pool", "ab_pool_scale", "cf_b_pw1", "cf_w_dw", "cf_b_dw", "cf_ln_g", "cf_ln_b", "cf_b_pw2")}
    dmods = [None] * depth
    received = {}
    pending = None
    for layer in reversed(range(depth)):
        i = layer // 2
        lw = weights[layer]
        xin, u, y, zc, x2, a, b, fout = saved[layer]
        sh1, sc1, g1, sh2, sc2, g2 = (mods[layer, k:k + 1] for k in range(6))
        (da, db, dwd, acc_d), _ = _ffn_bwd_down(dx, fout, a, b, g2, lw["down"], tmf)
        vec_ffn = jnp.concatenate([p["norm_ffn_g"][layer:layer + 1], sh2, sc2], axis=0)
        leaving = [_chunk_grad("down", dwd)] + ([pending] if pending is not None else [])
        (dx2, dwg, dwu, acc_u), arrived = _ffn_bwd_up(da, db, x2, dx, vec_ffn, lw["gate"], lw["up"], tmf,
                                                      exch=_Scatter(leaving))
        received[(layer, "down")] = arrived[0]
        if pending is not None:
            received[(layer + 1, "in")] = arrived[1]
        per_layer["norm_ffn_g"][layer] = acc_u[2]
        vec_in = jnp.concatenate([p["norm_mix_g"][layer:layer + 1], sh1, sc1], axis=0)
        if layer % 2 == 0:
            send_gate = _Scatter([_chunk_grad("gate", dwg)])
            (dpre, dwout, dwpool, acc_o), arrived = _ab_bwd_out(
                dx2, y, u, g1, p["ab_conv"][i], p["ab_w_pool"][i].astype(BF16), p["ab_pool_scale"][i:i + 1],
                lw["out"], tm, exch=send_gate)
            received[(layer, "gate")] = arrived[0]
            send_up_out = _Scatter([_chunk_grad("up", dwu), _chunk_grad("out", dwout)])
            (dx, dwin, dconv, acc_i), arrived = _ab_bwd_in(dpre, u, xin, dx2, vec_in, p["ab_conv"][i], lw["in"], tm,
                                                           exch=send_up_out)
            half["ab_w_pool"][i] = dwpool
            half["ab_pool_scale"][i] = acc_o[1, 0:d // 2]
            half["ab_conv"][i] = dconv[0:3]
        else:
            vec_cf = jnp.concatenate([g1, p["cf_ln_g"][i:i + 1], p["cf_ln_b"][i:i + 1]], axis=0)
            (dzc, dwout, acc_o), _ = _cf_bwd_out(dx2, y, zc, vec_cf, lw["out"], tm)
            send_all = _Scatter([_chunk_grad("up", dwu), _chunk_grad("out", dwout), _chunk_grad("gate", dwg)])
            (dx, dwin, dwdw, db1, acc_i), arrived = _cf_bwd_in(
                dzc, u, xin, dx2, vec_in, _pad_rows(p["cf_w_dw"][i], 32), lw["in"], tm, exch=send_all)
            received[(layer, "gate")] = arrived.pop()
            half["cf_b_pw2"][i] = acc_o[1]
            half["cf_ln_g"][i] = acc_o[2]
            half["cf_ln_b"][i] = acc_o[3]
            half["cf_b_dw"][i] = acc_o[4]
            half["cf_w_dw"][i] = dwdw[0:CONF_KERNEL]
            half["cf_b_pw1"][i] = db1[0]
        received[(layer, "up")], received[(layer, "out")] = arrived
        per_layer["norm_mix_g"][layer] = acc_i[2]
        dmods[layer] = jnp.stack([acc_i[0], acc_i[1], acc_o[0], acc_u[0], acc_u[1], acc_d[0]], axis=0)
        pending = _chunk_grad("in", dwin)
    for k, v in {**per_layer, **half}.items():
        grads[k] = jnp.stack(v, axis=0)
    return dx, grads, jnp.stack(dmods, axis=0), received, pending


SMALL_COLS = ("ab_conv", "cf_b_pw1", "cf_w_dw", "cf_b_dw", "cf_ln_g", "cf_ln_b", "cf_b_pw2")
SMALL_REPL = ("norm_mix_g", "norm_ffn_g", "ab_w_pool", "ab_pool_scale", "final_norm_g")
WEIGHTS = ("norm_mix_g", "norm_ffn_g", "w_mod", "b_mod", "ab_w_in", "ab_conv", "ab_w_pool", "ab_pool_scale",
           "ab_w_out", "cf_w_pw1", "cf_b_pw1", "cf_w_dw", "cf_b_dw", "cf_ln_g", "cf_ln_b", "cf_w_pw2",
           "cf_b_pw2", "ffn_w_gate", "ffn_w_up", "ffn_w_down", "final_norm_g")


def kernel(x, c, norm_mix_g, norm_ffn_g, w_mod, b_mod, ab_w_in, ab_conv, ab_w_pool, ab_pool_scale, ab_w_out, cf_w_pw1, cf_b_pw1, cf_w_dw, cf_b_dw, cf_ln_g, cf_ln_b, cf_w_pw2, cf_b_pw2, ffn_w_gate, ffn_w_up, ffn_w_down, final_norm_g, loss_target, m_norm_mix_g, m_norm_ffn_g, m_w_mod, m_b_mod, m_ab_w_in, m_ab_conv, m_ab_w_pool, m_ab_pool_scale, m_ab_w_out, m_cf_w_pw1, m_cf_b_pw1, m_cf_w_dw, m_cf_b_dw, m_cf_ln_g, m_cf_ln_b, m_cf_w_pw2, m_cf_b_pw2, m_ffn_w_gate, m_ffn_w_up, m_ffn_w_down, m_final_norm_g, v_norm_mix_g, v_norm_ffn_g, v_w_mod, v_b_mod, v_ab_w_in, v_ab_conv, v_ab_w_pool, v_ab_pool_scale, v_ab_w_out, v_cf_w_pw1, v_cf_b_pw1, v_cf_w_dw, v_cf_b_dw, v_cf_ln_g, v_cf_ln_b, v_cf_w_pw2, v_cf_b_pw2, v_ffn_w_gate, v_ffn_w_up, v_ffn_w_down, v_final_norm_g):
    w = dict(norm_mix_g=norm_mix_g, norm_ffn_g=norm_ffn_g, w_mod=w_mod, b_mod=b_mod, ab_w_in=ab_w_in,
             ab_conv=ab_conv, ab_w_pool=ab_w_pool, ab_pool_scale=ab_pool_scale, ab_w_out=ab_w_out,
             cf_w_pw1=cf_w_pw1, cf_b_pw1=cf_b_pw1, cf_w_dw=cf_w_dw, cf_b_dw=cf_b_dw, cf_ln_g=cf_ln_g,
             cf_ln_b=cf_ln_b, cf_w_pw2=cf_w_pw2, cf_b_pw2=cf_b_pw2, ffn_w_gate=ffn_w_gate, ffn_w_up=ffn_w_up,
             ffn_w_down=ffn_w_down, final_norm_g=final_norm_g)
    mom = dict(norm_mix_g=m_norm_mix_g, norm_ffn_g=m_norm_ffn_g, w_mod=m_w_mod, b_mod=m_b_mod, ab_w_in=m_ab_w_in,
               ab_conv=m_ab_conv, ab_w_pool=m_ab_w_pool, ab_pool_scale=m_ab_pool_scale, ab_w_out=m_ab_w_out,
               cf_w_pw1=m_cf_w_pw1, cf_b_pw1=m_cf_b_pw1, cf_w_dw=m_cf_w_dw, cf_b_dw=m_cf_b_dw, cf_ln_g=m_cf_ln_g,
               cf_ln_b=m_cf_ln_b, cf_w_pw2=m_cf_w_pw2, cf_b_pw2=m_cf_b_pw2, ffn_w_gate=m_ffn_w_gate,
               ffn_w_up=m_ffn_w_up, ffn_w_down=m_ffn_w_down, final_norm_g=m_final_norm_g)
    var = dict(norm_mix_g=v_norm_mix_g, norm_ffn_g=v_norm_ffn_g, w_mod=v_w_mod, b_mod=v_b_mod, ab_w_in=v_ab_w_in,
               ab_conv=v_ab_conv, ab_w_pool=v_ab_w_pool, ab_pool_scale=v_ab_pool_scale, ab_w_out=v_ab_w_out,
               cf_w_pw1=v_cf_w_pw1, cf_b_pw1=v_cf_b_pw1, cf_w_dw=v_cf_w_dw, cf_b_dw=v_cf_b_dw, cf_ln_g=v_cf_ln_g,
               cf_ln_b=v_cf_ln_b, cf_w_pw2=v_cf_w_pw2, cf_b_pw2=v_cf_b_pw2, ffn_w_gate=v_ffn_w_gate,
               ffn_w_up=v_ffn_w_up, ffn_w_down=v_ffn_w_down, final_norm_g=v_final_norm_g)
    px, py, pc = _place()
    chip = 2 * px + py
    dev = 2 * chip + pc
    depth, d, mod_cols = w_mod.shape
    x = x[0]
    target = loss_target[0]

    def rows_major(name, t):
        return jnp.swapaxes(t, 1, 2) if name in TRANSPOSED else t

    shards = [{k: rows_major(name, w[name])[idx].astype(BF16) for k, (name, idx) in _layer_big_names(layer).items()}
              for layer in range(depth)]

    small_in = _Packer()
    small_in.add("c", c)
    for name in SMALL_COLS:
        small_in.add(name, w[name])
    def first_gather(*keys):
        return _Gather([shards[0][k].reshape(2, -1, shards[0][k].shape[-1]) for k in keys])

    first = {}
    (gathered,), (first["in"],) = _allgather8(small_in.pack(), with_sum=False, exch=first_gather("in"))
    gathered = gathered.reshape(N_DEV, -1, LANES)
    per_dev = [small_in.unpack(gathered[k]) for k in range(N_DEV)]
    c_all = jnp.concatenate([pd["c"] for pd in per_dev], axis=0)
    params = {name: jnp.concatenate([per_dev[2 * k][name] for k in range(N_CHIPS)], axis=-1)
              for name in SMALL_COLS}
    for name in SMALL_REPL:
        params[name] = w[name]

    mod_part = _mod_fwd(c_all, w_mod, _my_cols(b_mod, chip))
    (mod_all,), _ = _allgather8(mod_part.reshape(-1, LANES), with_sum=False)
    mod_all = mod_all.reshape(N_CHIPS, 2, depth, N_DEV, mod_cols)[:, 0]
    mod_all = jnp.moveaxis(mod_all, 0, 2).reshape(depth, N_DEV, N_CHIPS * mod_cols)
    mods = lax.dynamic_index_in_dim(mod_all, dev, axis=1, keepdims=False).reshape(depth, 6, d)

    grad_x, grads, dmods, received, last_chunk = _local_step(x, target, mods, params, shards, first)

    small_out = _Packer()
    small_out.add("dmods", dmods)
    for name in ("loss",) + SMALL_REPL + SMALL_COLS:
        small_out.add(name, grads[name])
    (parts_all, parts_sum), (received[(0, "in")],) = _allgather8(small_out.pack(), with_sum=True,
                                                                 exch=_Scatter([last_chunk]))
    small_sum = small_out.unpack(parts_sum)
    loss = small_sum["loss"][0]
    dmods_all = jnp.stack([small_out.unpack(pa)["dmods"] for pa in parts_all.reshape(N_DEV, -1, LANES)], axis=1)
    dmods_all = dmods_all.reshape(depth, N_DEV, 6 * d)

    g_final = {}
    g_final["w_mod"] = [_mod_bwd(c_all.T, _my_cols(dmods_all, chip))]
    g_final["b_mod"] = [small_sum["dmods"].reshape(depth, 6 * d)]
    for name in SMALL_REPL:
        g_final[name] = [small_sum[name]]
    for name in SMALL_COLS:
        g_final[name] = [_my_cols(small_sum[name], chip)]

    updates = {}
    for name in WEIGHTS:
        parts = [received[(layer, k)] for layer in range(depth)
                 for k, (other, _) in _layer_big_names(layer).items() if other == name]
        if parts:
            outs = _adamw_partials(rows_major(name, w[name]), parts, rows_major(name, mom[name]),
                                   rows_major(name, var[name]))
            updates[name] = [rows_major(name, o) for o in outs]
        else:
            updates[name] = _adamw(w[name], g_final[name], mom[name], var[name])
    return (loss, grad_x[None], *[updates[name][0] for name in WEIGHTS], *[updates[name][1] for name in WEIGHTS],
            *[updates[name][2] for name in WEIGHTS], *[updates[name][3] for name in WEIGHTS])
```

```python
import functools

import jax
import jax.numpy as jnp
from jax import lax
from jax.experimental import pallas as pl
from jax.experimental.pallas import tpu as pltpu

F32 = jnp.float32
BF16 = jnp.bfloat16
RMS_EPS = 1e-6
LN_EPS = 1e-5
ADAM_LR = 0.001
ADAM_B1 = 0.9
ADAM_B2 = 0.999
ADAM_EPS = 1e-08
ADAM_WD = 0.01
ADAM_STEP = 10
POOL_WINDOWS = (2, 4, 8, 16)
CONF_KERNEL = 31
N_CHIPS = 4
N_DEV = 8
HALO = 16
CONV_COLS = 256
FFN_CHUNK = 1536
LANES = 1024
VMEM_LIMIT = 56 * 1024 * 1024
VMEM_LIMIT_WIDE = 60 * 1024 * 1024
EW_BLOCK_ELEMS = 256 * 1024
MESH = pl.DeviceIdType.MESH
HIGHEST = lax.Precision.HIGHEST

_pcall = pl.pallas_call


def _dot(a, b):
    return jnp.dot(a, b, preferred_element_type=F32)


def _dot_tn(a, b):
    return lax.dot_general(a, b, (((0,), (0,)), ((), ())), preferred_element_type=F32)


def _dot_nt(a, b):
    return lax.dot_general(a, b, (((1,), (1,)), ((), ())), preferred_element_type=F32)


def _colsum(v):
    return jnp.sum(v, axis=0, keepdims=True)


def _sigmoid(v):
    return 1.0 / (1.0 + jnp.exp(-v))


def _rows(tm, c):
    return pl.BlockSpec((tm, c), lambda i: (i, 0))


def _full(shape):
    nd = len(shape)
    return pl.BlockSpec(shape, lambda i: (0,) * nd)


_VM = pl.BlockSpec(memory_space=pltpu.VMEM)
_ANY = pl.BlockSpec(memory_space=pl.ANY)


def _halo_specs(tm, c, t_total):
    r = tm // HALO
    last = t_total // HALO - 1
    prev = pl.BlockSpec((HALO, c), lambda i: (jnp.maximum(i * r - 1, 0), 0))
    nxt = pl.BlockSpec((HALO, c), lambda i: (jnp.minimum((i + 1) * r, last), 0))
    return prev, _rows(tm, c), nxt


def _seq_params(vmem_limit=VMEM_LIMIT):
    return pltpu.CompilerParams(dimension_semantics=("arbitrary",), vmem_limit_bytes=vmem_limit)


def _place():
    return lax.axis_index("x"), lax.axis_index("y"), lax.axis_index("c")


def _peer_chips(x, y):
    return [(1 - x, y), (x, 1 - y), (1 - x, 1 - y)]


class _Gather:
    tag = "gather"

    def __init__(self, arrs):
        self.arrs = list(arrs)

    def out_shapes(self):
        return [jax.ShapeDtypeStruct((N_CHIPS,) + a.shape, a.dtype) for a in self.arrs]

    def sems(self):
        n = len(self.arrs)
        return [pltpu.SemaphoreType.DMA((3 * n,)) for _ in range(4)] + [pltpu.SemaphoreType.DMA((n,))]

    def _copies(self, ins, outs, sems, kinds):
        ici_send, ici_recv, d2d_send, d2d_recv, local_sems = sems
        x, y, c = _place()
        me = 2 * x + y
        found = {kind: [] for kind in kinds}
        for j in range(len(ins)):
            if "local" in kinds:
                found["local"].append(pltpu.make_async_copy(ins[j], outs[j].at[me], local_sems.at[j]))
            for k, (px, py) in enumerate(_peer_chips(x, y)):
                ici = dict(send_sem=ici_send.at[3 * j + k], recv_sem=ici_recv.at[3 * j + k],
                           device_id=(px, py, c), device_id_type=MESH)
                d2d = dict(send_sem=d2d_send.at[3 * j + k], recv_sem=d2d_recv.at[3 * j + k],
                           device_id=(x, y, 1 - c), device_id_type=MESH)
                theirs = outs[j].at[2 * px + py]
                if "send" in kinds:
                    found["send"].append(pltpu.make_async_remote_copy(
                        src_ref=ins[j].at[c], dst_ref=outs[j].at[me, c], **ici))
                if "arrival" in kinds:
                    found["arrival"].append(pltpu.make_async_remote_copy(
                        src_ref=ins[j].at[c], dst_ref=theirs.at[c], **ici))
                if "pass" in kinds:
                    found["pass"].append(pltpu.make_async_remote_copy(
                        src_ref=theirs.at[c], dst_ref=theirs.at[c], **d2d))
                if "passed" in kinds:
                    found["passed"].append(pltpu.make_async_remote_copy(
                        src_ref=theirs.at[c], dst_ref=theirs.at[1 - c], **d2d))
        return found

    def start(self, ins, outs, sems):
        found = self._copies(ins, outs, sems, ("local", "send"))
        for cp in found["local"] + found["send"]:
            cp.start()

    def mid(self, ins, outs, sems):
        found = self._copies(ins, outs, sems, ("arrival", "pass"))
        for arrived, onward in zip(found["arrival"], found["pass"]):
            arrived.wait_recv()
            onward.start()

    def wait(self, ins, outs, sems):
        found = self._copies(ins, outs, sems, ("local", "send", "pass", "passed"))
        for cp in found["passed"]:
            cp.wait_recv()
        for cp in found["send"] + found["pass"]:
            cp.wait_send()
        for cp in found["local"]:
            cp.wait()


class _Scatter:
    tag = "scatter"

    def __init__(self, arrs):
        self.arrs = list(arrs)

    def out_shapes(self):
        return [jax.ShapeDtypeStruct((2,) + a.shape, a.dtype) for a in self.arrs]

    def sems(self):
        n = len(self.arrs)
        dma = pltpu.SemaphoreType.DMA
        return [dma((3 * n,)), dma((3 * n,)), dma((4 * n,)), dma((4 * n,)), dma((n,))]

    def _copies(self, ins, outs, sems, kinds):
        ici_send, ici_recv, d2d_send, d2d_recv, local_sems = sems
        x, y, c = _place()
        me = 2 * x + y
        found = {kind: [] for kind in kinds}
        for j in range(len(ins)):
            def d2d(k):
                return dict(send_sem=d2d_send.at[4 * j + k], recv_sem=d2d_recv.at[4 * j + k],
                            device_id=(x, y, 1 - c), device_id_type=MESH)

            if "local" in kinds:
                found["local"].append(pltpu.make_async_copy(ins[j].at[me], outs[j].at[0, me], local_sems.at[j]))
            if "own" in kinds:
                found["own"].append(pltpu.make_async_remote_copy(
                    src_ref=ins[j].at[me], dst_ref=outs[j].at[1, me], **d2d(3)))
            if "passed" in kinds:
                found["passed"].append(pltpu.make_async_remote_copy(
                    src_ref=ins[j].at[me], dst_ref=outs[j].at[1, me], **d2d(3)))
            for k, (px, py) in enumerate(_peer_chips(x, y)):
                ici = dict(send_sem=ici_send.at[3 * j + k], recv_sem=ici_recv.at[3 * j + k],
                           device_id=(px, py, c), device_id_type=MESH)
                peer = 2 * px + py
                if "send" in kinds:
                    found["send"].append(pltpu.make_async_remote_copy(
                        src_ref=ins[j].at[peer], dst_ref=outs[j].at[0, me], **ici))
                if "arrival" in kinds:
                    found["arrival"].append(pltpu.make_async_remote_copy(
                        src_ref=ins[j].at[me], dst_ref=outs[j].at[0, peer], **ici))
                if "pass" in kinds:
                    found["pass"].append(pltpu.make_async_remote_copy(
                        src_ref=outs[j].at[0, peer], dst_ref=outs[j].at[1, peer], **d2d(k)))
                if "passed" in kinds:
                    found["passed"].append(pltpu.make_async_remote_copy(
                        src_ref=outs[j].at[0, peer], dst_ref=outs[j].at[1, peer], **d2d(k)))
        return found

    def start(self, ins, outs, sems):
        found = self._copies(ins, outs, sems, ("local", "own", "send"))
        for cp in found["local"] + found["own"] + found["send"]:
            cp.start()

    def mid(self, ins, outs, sems):
        found = self._copies(ins, outs, sems, ("arrival", "pass"))
        for arrived, onward in zip(found["arrival"], found["pass"]):
            arrived.wait_recv()
            onward.start()

    def wait(self, ins, outs, sems):
        found = self._copies(ins, outs, sems, ("local", "own", "send", "pass", "passed"))
        for cp in found["passed"]:
            cp.wait_recv()
        for cp in found["own"] + found["send"] + found["pass"]:
            cp.wait_send()
        for cp in found["local"]:
            cp.wait()


def _call(body, *, name, nsteps, in_specs, out_specs, out_shape, args, scratch_shapes=(), exch=None,
          vmem_limit=VMEM_LIMIT):
    if exch is None:
        outs = _pcall(body, name=name, grid=(nsteps,), in_specs=list(in_specs), out_specs=list(out_specs),
                      out_shape=list(out_shape), scratch_shapes=list(scratch_shapes),
                      compiler_params=_seq_params(vmem_limit))(*args)
        return list(outs), []
    n, ni, no, ns = len(exch.arrs), len(in_specs), len(out_specs), len(scratch_shapes)

    def hosted(*refs):
        xin = refs[ni:ni + n]
        xout = refs[ni + n + no:ni + 2 * n + no]
        scr = refs[ni + 2 * n + no:]

        @pl.when(pl.program_id(0) == 0)
        def _():
            exch.start(xin, xout, scr[ns:])

        body(*refs[:ni], *refs[ni + n:ni + n + no], *scr[:ns])

        @pl.when(pl.program_id(0) == max(nsteps - 3, 0))
        def _():
            exch.mid(xin, xout, scr[ns:])

        @pl.when(pl.program_id(0) == nsteps - 1)
        def _():
            exch.wait(xin, xout, scr[ns:])

    outs = _pcall(hosted, name=name + "_" + exch.tag, grid=(nsteps,),
                  in_specs=[*in_specs, *[_ANY] * n], out_specs=[*out_specs, *[_ANY] * n],
                  out_shape=[*out_shape, *exch.out_shapes()],
                  scratch_shapes=[*scratch_shapes, *exch.sems()],
                  compiler_params=_seq_params(vmem_limit))(*args, *exch.arrs)
    return list(outs[:no]), list(outs[no:])


def _rms(x):
    r = lax.rsqrt(jnp.mean(x * x, axis=-1, keepdims=True) + RMS_EPS)
    return x * r, r


def _norm_mod(x, g, sh, sc):
    xhat, _ = _rms(x)
    return xhat * g * (1.0 + sc) + sh


def _norm_mod_bwd(dh, x, g, sc):
    xhat, r = _rms(x)
    n = xhat * g
    dsh = _colsum(dh)
    dsc = _colsum(dh * n)
    dn = dh * (1.0 + sc)
    dg = _colsum(dn * xhat)
    dxn = dn * g
    dx = r * (dxn - xhat * jnp.mean(dxn * xhat, axis=-1, keepdims=True))
    return dx, dsh, dsc, dg


def _fill_ext(ext_ref, prev, cur, nxt, i, nsteps, tm):
    ext_ref[0:HALO, :] = jnp.where(i > 0, prev, 0.0)
    ext_ref[HALO:HALO + tm, :] = cur
    ext_ref[HALO + tm:HALO + tm + HALO, :] = jnp.where(i < nsteps - 1, nxt, 0.0)


def _shift_scratch(tm):
    return pltpu.VMEM((8, tm + 2 * HALO - 8, CONV_COLS), F32)


def _fill_shifts(sh_ref, ext_ref, lo, hi, tm):
    for b in range(8):
        sh_ref[b] = ext_ref[b:b + tm + 2 * HALO - 8, lo:hi]


def _shifted(sh_ref, offset, tm):
    b = offset % 8
    start = HALO + offset - b
    return sh_ref[b, start:start + tm, :]


def _window_count(t, wdw, t_total):
    left = wdw // 2
    right = wdw - 1 - left
    cnt = jnp.minimum(t + right, t_total - 1) - jnp.maximum(t - left, 0) + 1
    return jnp.maximum(cnt, 1).astype(F32)


def _in_proj(x, vec, w, bias, tm, exch=None):
    t_total, d = x.shape
    nk = w.shape[2]
    n = N_CHIPS * nk
    has_bias = bias is not None

    def body(*refs):
        if has_bias:
            x_ref, vec_ref, w_ref, b_ref, u_ref = refs
        else:
            x_ref, vec_ref, w_ref, u_ref = refs
        h = _norm_mod(x_ref[...], vec_ref[0:1, :], vec_ref[1:2, :], vec_ref[2:3, :])
        h = h.astype(BF16)
        for k in range(N_CHIPS):
            u = _dot(h, w_ref[k])
            if has_bias:
                u = u + b_ref[:, k * nk:(k + 1) * nk]
            u_ref[:, k * nk:(k + 1) * nk] = u.astype(BF16)

    in_specs = [_rows(tm, d), _full(vec.shape), _VM]
    args = [x, vec, w]
    if has_bias:
        in_specs.append(_full(bias.shape))
        args.append(bias)
    return _call(
        body, name="in_proj_bias" if has_bias else "in_proj", nsteps=t_total // tm,
        in_specs=in_specs, out_specs=[_rows(tm, n)], out_shape=[jax.ShapeDtypeStruct((t_total, n), BF16)],
        args=args, exch=exch)


def _in_proj_bwd(h, du, w_ref, dw_ref):
    nk = w_ref.shape[2]
    dh = None
    for k in range(N_CHIPS):
        duk = du[:, k * nk:(k + 1) * nk]
        dw_ref[k] += _dot_tn(h, duk)
        part = _dot_nt(duk, w_ref[k])
        dh = part if dh is None else dh + part
    return dh


def _ab_core(up_ref, uc_ref, un_ref, conv_ref, wpool_ref, q_ext, p_ext, i, nsteps, tm, t_total):
    da = uc_ref.shape[1] // 4

    def cols(ref, k):
        return ref[:, k * da:(k + 1) * da].astype(F32)

    _fill_ext(q_ext, cols(up_ref, 1) * cols(up_ref, 2), cols(uc_ref, 1) * cols(uc_ref, 2),
              cols(un_ref, 1) * cols(un_ref, 2), i, nsteps, tm)
    _fill_ext(p_ext, cols(up_ref, 3), cols(uc_ref, 3), cols(un_ref, 3), i, nsteps, tm)
    bg = cols(uc_ref, 0)
    cq = (conv_ref[0:1, :] * q_ext[HALO - 1:HALO - 1 + tm, :] + conv_ref[1:2, :] * q_ext[HALO:HALO + tm, :]
          + conv_ref[2:3, :] * q_ext[HALO + 1:HALO + 1 + tm, :])
    t = i * tm + lax.broadcasted_iota(jnp.int32, (tm, 1), 0)
    gw = da // len(POOL_WINDOWS)
    pooled, ybpre = [], []
    for g, wdw in enumerate(POOL_WINDOWS):
        left = wdw // 2
        right = wdw - 1 - left
        lo, hi = g * gw, (g + 1) * gw
        s = p_ext[HALO - left:HALO - left + tm, lo:hi]
        for o in range(-left + 1, right + 1):
            s = s + p_ext[HALO + o:HALO + o + tm, lo:hi]
        pg = s / _window_count(t, wdw, t_total) - p_ext[HALO:HALO + tm, lo:hi]
        pooled.append(pg.astype(BF16))
        ybpre.append(_dot(pooled[-1], wpool_ref[g]))
    return bg, cq, pooled, jnp.concatenate(ybpre, axis=1)


def _ab_fwd(u, x, vec, conv, wpool, scale, wout, tm, exch=None):
    t_total, d = x.shape
    nu = u.shape[1]
    da = nu // 4
    nsteps = t_total // tm

    def body(up_ref, uc_ref, un_ref, x_ref, vec_ref, conv_ref, wpool_ref, scale_ref, wout_ref,
             y_ref, x2_ref, q_ext, p_ext):
        i = pl.program_id(0)
        bg, cq, _, ybpre = _ab_core(up_ref, uc_ref, un_ref, conv_ref, wpool_ref, q_ext, p_ext,
                                    i, nsteps, tm, t_total)
        cat = jnp.concatenate([bg * cq, ybpre * scale_ref[...]], axis=1).astype(BF16)
        y = _dot(cat, wout_ref[...])
        y_ref[...] = y.astype(BF16)
        x2_ref[...] = x_ref[...] + vec_ref[0:1, :] * y

    return _call(
        body, name="ab_fwd", nsteps=nsteps,
        in_specs=[*_halo_specs(tm, nu, t_total), _rows(tm, d), _full(vec.shape), _full(conv.shape),
                  _full(wpool.shape), _full(scale.shape), _VM],
        out_specs=[_rows(tm, d), _rows(tm, d)],
        out_shape=[jax.ShapeDtypeStruct((t_total, d), BF16), jax.ShapeDtypeStruct((t_total, d), F32)],
        scratch_shapes=[pltpu.VMEM((tm + 2 * HALO, da), F32), pltpu.VMEM((tm + 2 * HALO, da), F32)],
        args=(u, u, u, x, vec, conv, wpool, scale, wout), exch=exch)


def _glu_ext(up_ref, uc_ref, un_ref, z_ext, i, nsteps, tm):
    dz = uc_ref.shape[1] // 2

    def glu(ref):
        return ref[:, 0:dz].astype(F32) * _sigmoid(ref[:, dz:2 * dz].astype(F32))

    _fill_ext(z_ext, glu(up_ref), glu(uc_ref), glu(un_ref), i, nsteps, tm)


def _layer_norm_stats(zc):
    mu = jnp.mean(zc, axis=-1, keepdims=True)
    dlt = zc - mu
    rstd = lax.rsqrt(jnp.mean(dlt * dlt, axis=-1, keepdims=True) + LN_EPS)
    return dlt * rstd, rstd


def _cf_fwd(u, x, vec, wdw, wpw2, tm, exch=None):
    t_total, d = x.shape
    nu = u.shape[1]
    nsteps = t_total // tm
    left = (CONF_KERNEL - 1) // 2

    def body(up_ref, uc_ref, un_ref, x_ref, vec_ref, wdw_ref, wpw2_ref, zc_ref, y_ref, x2_ref, z_ext, sh_ref,
             zc_buf):
        i = pl.program_id(0)
        _glu_ext(up_ref, uc_ref, un_ref, z_ext, i, nsteps, tm)
        for lo in range(0, d, CONV_COLS):
            hi = lo + CONV_COLS
            _fill_shifts(sh_ref, z_ext, lo, hi, tm)
            acc = wdw_ref[0:1, lo:hi] * _shifted(sh_ref, -left, tm)
            for k in range(1, CONF_KERNEL):
                acc = acc + wdw_ref[k:k + 1, lo:hi] * _shifted(sh_ref, k - left, tm)
            zc_buf[:, lo:hi] = acc
        zc = zc_buf[...] + vec_ref[1:2, :]
        zc_ref[...] = zc.astype(BF16)
        zn, _ = _layer_norm_stats(zc)
        zl = zn * vec_ref[2:3, :] + vec_ref[3:4, :]
        zs = zl * _sigmoid(zl)
        y = _dot(zs.astype(BF16), wpw2_ref[...]) + vec_ref[4:5, :]
        y_ref[...] = y.astype(BF16)
        x2_ref[...] = x_ref[...] + vec_ref[0:1, :] * y

    return _call(
        body, name="cf_fwd", nsteps=nsteps,
        in_specs=[*_halo_specs(tm, nu, t_total), _rows(tm, d), _full(vec.shape), _full(wdw.shape), _VM],
        out_specs=[_rows(tm, d), _rows(tm, d), _rows(tm, d)],
        out_shape=[jax.ShapeDtypeStruct((t_total, d), BF16), jax.ShapeDtypeStruct((t_total, d), BF16),
                   jax.ShapeDtypeStruct((t_total, d), F32)],
        scratch_shapes=[pltpu.VMEM((tm + 2 * HALO, d), F32), _shift_scratch(tm), pltpu.VMEM((tm, d), F32)],
        args=(u, u, u, x, vec, wdw, wpw2), exch=exch)


def _ffn_chunks(f, width=FFN_CHUNK):
    return [(lo, min(lo + width, f)) for lo in range(0, f, width)]


def _ffn_fwd(x2, vec, wg, wu, wd, tm, exch=None):
    t_total, d = x2.shape
    f = wg.shape[0]

    def body(x_ref, vec_ref, wg_ref, wu_ref, wd_ref, a_ref, b_ref, f_ref, x3_ref):
        xv = x_ref[...]
        h = _norm_mod(xv, vec_ref[0:1, :], vec_ref[1:2, :], vec_ref[2:3, :]).astype(BF16)
        y = None
        for lo, hi in _ffn_chunks(f):
            a = _dot_nt(h, wg_ref[lo:hi, :])
            b = _dot_nt(h, wu_ref[lo:hi, :])
            a_ref[:, lo:hi] = a.astype(BF16)
            b_ref[:, lo:hi] = b.astype(BF16)
            s = (a * _sigmoid(a) * b).astype(BF16)
            part = _dot(s, wd_ref[lo:hi, :])
            y = part if y is None else y + part
        f_ref[...] = y.astype(BF16)
        x3_ref[...] = xv + vec_ref[3:4, :] * y

    return _call(
        body, name="ffn_fwd", nsteps=t_total // tm,
        in_specs=[_rows(tm, d), _full(vec.shape), _VM, _VM, _VM],
        out_specs=[_rows(tm, f), _rows(tm, f), _rows(tm, d), _rows(tm, d)],
        out_shape=[jax.ShapeDtypeStruct((t_total, f), BF16), jax.ShapeDtypeStruct((t_total, f), BF16),
                   jax.ShapeDtypeStruct((t_total, d), BF16), jax.ShapeDtypeStruct((t_total, d), F32)],
        args=(x2, vec, wg, wu, wd), exch=exch)


def _final_fwd_bwd(x, target, vec, tm):
    t_total, d = x.shape

    def body(x_ref, t_ref, vec_ref, dx_ref, acc_ref):
        @pl.when(pl.program_id(0) == 0)
        def _():
            acc_ref[...] = jnp.zeros_like(acc_ref)

        g = vec_ref[0:1, :]
        xhat, r = _rms(x_ref[...])
        e = xhat * g - t_ref[...]
        acc_ref[1:2, :] += jnp.zeros((1, d), F32) + 0.5 * jnp.sum(jnp.mean(e * e, axis=-1, keepdims=True))
        dout = e * (1.0 / d)
        acc_ref[0:1, :] += _colsum(dout * xhat)
        dxn = dout * g
        dx_ref[...] = r * (dxn - xhat * jnp.mean(dxn * xhat, axis=-1, keepdims=True))

    return _call(
        body, name="final_fwd_bwd", nsteps=t_total // tm,
        in_specs=[_rows(tm, d), _rows(tm, d), _full(vec.shape)],
        out_specs=[_rows(tm, d), _VM],
        out_shape=[jax.ShapeDtypeStruct((t_total, d), F32), jax.ShapeDtypeStruct((8, d), F32)],
        args=(x, target, vec))


def _zero_at_start(*refs):
    @pl.when(pl.program_id(0) == 0)
    def _():
        for ref in refs:
            ref[...] = jnp.zeros_like(ref)


def _emit_bf16_at_end(nsteps, acc_ref, out_ref):
    @pl.when(pl.program_id(0) == nsteps - 1)
    def _():
        out_ref[...] = acc_ref[...].astype(BF16)


def _ffn_bwd_down(dx3, fout, a, b, vec, wd, tm, exch=None):
    t_total, d = dx3.shape
    f = a.shape[1]

    def body(dx_ref, f_ref, a_ref, b_ref, vec_ref, wd_ref, da_ref, db_ref, dwd_out, acc_ref, dwd_ref):
        _zero_at_start(dwd_ref, acc_ref)
        dx = dx_ref[...]
        acc_ref[0:1, :] += _colsum(dx * f_ref[...].astype(F32))
        dy = (dx * vec_ref[0:1, :]).astype(BF16)
        for lo, hi in _ffn_chunks(f, FFN_CHUNK // 3):
            av = a_ref[:, lo:hi].astype(F32)
            bv = b_ref[:, lo:hi].astype(F32)
            sg = _sigmoid(av)
            silu = av * sg
            ds = _dot_nt(dy, wd_ref[lo:hi, :])
            da_ref[:, lo:hi] = (ds * bv * (sg * (1.0 + av * (1.0 - sg)))).astype(BF16)
            db_ref[:, lo:hi] = (ds * silu).astype(BF16)
            dwd_ref[lo:hi, :] += _dot_tn((silu * bv).astype(BF16), dy)
        _emit_bf16_at_end(t_total // tm, dwd_ref, dwd_out)

    return _call(
        body, name="ffn_bwd_down", nsteps=t_total // tm,
        in_specs=[_rows(tm, d), _rows(tm, d), _rows(tm, f), _rows(tm, f), _full(vec.shape), _VM],
        out_specs=[_rows(tm, f), _rows(tm, f), _VM, _VM],
        out_shape=[jax.ShapeDtypeStruct((t_total, f), BF16), jax.ShapeDtypeStruct((t_total, f), BF16),
                   jax.ShapeDtypeStruct(wd.shape, BF16), jax.ShapeDtypeStruct((8, d), F32)],
        scratch_shapes=[pltpu.VMEM(wd.shape, F32)],
        args=(dx3, fout, a, b, vec, wd), exch=exch)


def _ffn_bwd_up(da, db, x2, dx3, vec, wg, wu, tm, exch=None):
    t_total, d = x2.shape
    f = da.shape[1]

    def body(da_ref, db_ref, x_ref, dx_ref, vec_ref, wg_ref, wu_ref, dx2_ref, dwg_out, dwu_out, acc_ref,
             dwg_ref, dwu_ref):
        _zero_at_start(dwg_ref, dwu_ref, acc_ref)
        xv = x_ref[...]
        g, sh, sc = vec_ref[0:1, :], vec_ref[1:2, :], vec_ref[2:3, :]
        h = _norm_mod(xv, g, sh, sc).astype(BF16)
        dav = da_ref[...]
        dbv = db_ref[...]
        dwg_ref[...] += _dot_tn(dav, h)
        dwu_ref[...] += _dot_tn(dbv, h)
        dh = _dot(dav, wg_ref[...]) + _dot(dbv, wu_ref[...])
        dxn, dsh, dsc, dg = _norm_mod_bwd(dh, xv, g, sc)
        acc_ref[0:1, :] += dsh
        acc_ref[1:2, :] += dsc
        acc_ref[2:3, :] += dg
        dx2_ref[...] = dx_ref[...] + dxn
        _emit_bf16_at_end(t_total // tm, dwg_ref, dwg_out)
        _emit_bf16_at_end(t_total // tm, dwu_ref, dwu_out)

    return _call(
        body, name="ffn_bwd_up", nsteps=t_total // tm,
        in_specs=[_rows(tm, f), _rows(tm, f), _rows(tm, d), _rows(tm, d), _full(vec.shape), _VM, _VM],
        out_specs=[_rows(tm, d), _VM, _VM, _VM],
        out_shape=[jax.ShapeDtypeStruct((t_total, d), F32), jax.ShapeDtypeStruct(wg.shape, BF16),
                   jax.ShapeDtypeStruct(wu.shape, BF16), jax.ShapeDtypeStruct((8, d), F32)],
        scratch_shapes=[pltpu.VMEM(wg.shape, F32), pltpu.VMEM(wu.shape, F32)],
        args=(da, db, x2, dx3, vec, wg, wu), exch=exch, vmem_limit=VMEM_LIMIT_WIDE)


def _ab_bwd_out(dx, y, u, vec, conv, wpool, scale, wout, tm, exch=None):
    t_total, d = dx.shape
    nu = u.shape[1]
    da = nu // 4
    gw = da // len(POOL_WINDOWS)
    nsteps = t_total // tm

    def body(dx_ref, y_ref, up_ref, uc_ref, un_ref, vec_ref, conv_ref, wpool_ref, scale_ref, wout_ref,
             dpre_ref, dwout_out, dwpool_ref, acc_ref, q_ext, p_ext, dwout_ref):
        _zero_at_start(dwout_ref, dwpool_ref, acc_ref)
        i = pl.program_id(0)
        dxv = dx_ref[...]
        acc_ref[0:1, :] += _colsum(dxv * y_ref[...].astype(F32))
        dy = (dxv * vec_ref[0:1, :]).astype(BF16)
        bg, cq, pooled, ybpre = _ab_core(up_ref, uc_ref, un_ref, conv_ref, wpool_ref, q_ext, p_ext,
                                         i, nsteps, tm, t_total)
        cat = jnp.concatenate([bg * cq, ybpre * scale_ref[...]], axis=1).astype(BF16)
        dwout_ref[...] += _dot_tn(cat, dy)
        dcat = _dot_nt(dy, wout_ref[...])
        dya = dcat[:, 0:da]
        dyb = dcat[:, da:2 * da]
        acc_ref[1:2, 0:da] += _colsum(dyb * ybpre)
        dybpre = (dyb * scale_ref[...]).astype(BF16)
        dpooled = []
        for g in range(len(POOL_WINDOWS)):
            dg = dybpre[:, g * gw:(g + 1) * gw]
            dwpool_ref[g] += _dot_tn(pooled[g], dg)
            dpooled.append(_dot_nt(dg, wpool_ref[g]))
        dpre_ref[...] = jnp.concatenate([dya * cq, dya * bg] + dpooled, axis=1).astype(BF16)
        _emit_bf16_at_end(nsteps, dwout_ref, dwout_out)

    return _call(
        body, name="ab_bwd_out", nsteps=nsteps,
        in_specs=[_rows(tm, d), _rows(tm, d), *_halo_specs(tm, nu, t_total), _full(vec.shape),
                  _full(conv.shape), _full(wpool.shape), _full(scale.shape), _VM],
        out_specs=[_rows(tm, 3 * da), _VM, _VM, _VM],
        out_shape=[jax.ShapeDtypeStruct((t_total, 3 * da), BF16), jax.ShapeDtypeStruct(wout.shape, BF16),
                   jax.ShapeDtypeStruct(wpool.shape, F32), jax.ShapeDtypeStruct((8, d), F32)],
        scratch_shapes=[pltpu.VMEM((tm + 2 * HALO, da), F32), pltpu.VMEM((tm + 2 * HALO, da), F32),
                        pltpu.VMEM(wout.shape, F32)],
        args=(dx, y, u, u, u, vec, conv, wpool, scale, wout), exch=exch)


def _ab_bwd_in(dpre, u, x, dx, vec, conv, win, tm, exch=None):
    t_total, d = x.shape
    nu = u.shape[1]
    da = nu // 4
    gw = da // len(POOL_WINDOWS)
    nsteps = t_total // tm

    def body(dp_ref, dc_ref, dn_ref, up_ref, uc_ref, un_ref, x_ref, dx_ref, vec_ref, conv_ref, win_ref,
             dxin_ref, dwin_out, dconv_ref, acc_ref, dcq_ext, q_ext, dpl_ext, dwin_ref):
        _zero_at_start(dwin_ref, dconv_ref, acc_ref)
        i = pl.program_id(0)

        def ucols(ref, k):
            return ref[:, k * da:(k + 1) * da].astype(F32)

        def dcols(ref, k):
            return ref[:, k * da:(k + 1) * da].astype(F32)

        _fill_ext(dcq_ext, dcols(dp_ref, 1), dcols(dc_ref, 1), dcols(dn_ref, 1), i, nsteps, tm)
        _fill_ext(q_ext, ucols(up_ref, 1) * ucols(up_ref, 2), ucols(uc_ref, 1) * ucols(uc_ref, 2),
                  ucols(un_ref, 1) * ucols(un_ref, 2), i, nsteps, tm)
        _fill_ext(dpl_ext, dcols(dp_ref, 2), dcols(dc_ref, 2), dcols(dn_ref, 2), i, nsteps, tm)
        dq = (conv_ref[0:1, :] * dcq_ext[HALO + 1:HALO + 1 + tm, :] + conv_ref[1:2, :] * dcq_ext[HALO:HALO + tm, :]
              + conv_ref[2:3, :] * dcq_ext[HALO - 1:HALO - 1 + tm, :])
        dcq = dcq_ext[HALO:HALO + tm, :]
        for k in range(3):
            dconv_ref[k:k + 1, :] += _colsum(dcq * q_ext[HALO + k - 1:HALO + k - 1 + tm, :])
        dcg = dq * ucols(uc_ref, 2)
        dv = dq * ucols(uc_ref, 1)
        t_ext = i * tm - HALO + lax.broadcasted_iota(jnp.int32, (tm + 2 * HALO, 1), 0)
        dps = []
        for g, wdw in enumerate(POOL_WINDOWS):
            left = wdw // 2
            right = wdw - 1 - left
            lo, hi = g * gw, (g + 1) * gw
            dpg = dpl_ext[HALO:HALO + tm, lo:hi]
            dpl_ext[:, lo:hi] = dpl_ext[:, lo:hi] / _window_count(t_ext, wdw, t_total)
            s = dpl_ext[HALO - right:HALO - right + tm, lo:hi]
            for o in range(-right + 1, left + 1):
                s = s + dpl_ext[HALO + o:HALO + o + tm, lo:hi]
            dps.append(s - dpg)
        du = jnp.concatenate([dcols(dc_ref, 0), dcg, dv] + dps, axis=1).astype(BF16)
        xv = x_ref[...]
        g, sh, sc = vec_ref[0:1, :], vec_ref[1:2, :], vec_ref[2:3, :]
        h = _norm_mod(xv, g, sh, sc).astype(BF16)
        dh = _in_proj_bwd(h, du, win_ref, dwin_ref)
        dxn, dsh, dsc, dg = _norm_mod_bwd(dh, xv, g, sc)
        acc_ref[0:1, :] += dsh
        acc_ref[1:2, :] += dsc
        acc_ref[2:3, :] += dg
        dxin_ref[...] = dx_ref[...] + dxn
        _emit_bf16_at_end(nsteps, dwin_ref, dwin_out)

    ext = pltpu.VMEM((tm + 2 * HALO, da), F32)
    return _call(
        body, name="ab_bwd_in", nsteps=nsteps,
        in_specs=[*_halo_specs(tm, 3 * da, t_total), *_halo_specs(tm, nu, t_total), _rows(tm, d), _rows(tm, d),
                  _full(vec.shape), _full(conv.shape), _VM],
        out_specs=[_rows(tm, d), _VM, _VM, _VM],
        out_shape=[jax.ShapeDtypeStruct((t_total, d), F32), jax.ShapeDtypeStruct(win.shape, BF16),
                   jax.ShapeDtypeStruct((8, da), F32), jax.ShapeDtypeStruct((8, d), F32)],
        scratch_shapes=[ext, ext, ext, pltpu.VMEM(win.shape, F32)],
        args=(dpre, dpre, dpre, u, u, u, x, dx, vec, conv, win), exch=exch)


def _cf_bwd_out(dx, y, zc, vec, wpw2, tm, exch=None):
    t_total, d = dx.shape

    def body(dx_ref, y_ref, zc_ref, vec_ref, w_ref, dzc_ref, dw_out, acc_ref, dw_ref):
        _zero_at_start(dw_ref, acc_ref)
        dxv = dx_ref[...]
        acc_ref[0:1, :] += _colsum(dxv * y_ref[...].astype(F32))
        dyf = dxv * vec_ref[0:1, :]
        acc_ref[1:2, :] += _colsum(dyf)
        dy = dyf.astype(BF16)
        zn, rstd = _layer_norm_stats(zc_ref[...].astype(F32))
        lng = vec_ref[1:2, :]
        zl = zn * lng + vec_ref[2:3, :]
        sg = _sigmoid(zl)
        dw_ref[...] += _dot_tn((zl * sg).astype(BF16), dy)
        dzl = _dot_nt(dy, w_ref[...]) * (sg * (1.0 + zl * (1.0 - sg)))
        acc_ref[2:3, :] += _colsum(dzl * zn)
        acc_ref[3:4, :] += _colsum(dzl)
        dzn = dzl * lng
        dzc = rstd * (dzn - jnp.mean(dzn, axis=-1, keepdims=True)
                      - zn * jnp.mean(dzn * zn, axis=-1, keepdims=True))
        acc_ref[4:5, :] += _colsum(dzc)
        dzc_ref[...] = dzc.astype(BF16)
        _emit_bf16_at_end(t_total // tm, dw_ref, dw_out)

    return _call(
        body, name="cf_bwd_out", nsteps=t_total // tm,
        in_specs=[_rows(tm, d), _rows(tm, d), _rows(tm, d), _full(vec.shape), _VM],
        out_specs=[_rows(tm, d), _VM, _VM],
        out_shape=[jax.ShapeDtypeStruct((t_total, d), BF16), jax.ShapeDtypeStruct(wpw2.shape, BF16),
                   jax.ShapeDtypeStruct((8, d), F32)],
        scratch_shapes=[pltpu.VMEM(wpw2.shape, F32)],
        args=(dx, y, zc, vec, wpw2), exch=exch)


def _cf_bwd_in(dzc, u, x, dx, vec, wdw, wpw1, tm, exch=None):
    t_total, d = x.shape
    nu = u.shape[1]
    nsteps = t_total // tm
    left = (CONF_KERNEL - 1) // 2

    def body(dp_ref, dc_ref, dn_ref, up_ref, uc_ref, un_ref, x_ref, dx_ref, vec_ref, wdw_ref, w_ref,
             dxin_ref, dw_out, dwdw_ref, db1_ref, acc_ref, dzc_ext, z_ext, sh_ref, dz_buf, dw_ref):
        _zero_at_start(dw_ref, dwdw_ref, db1_ref, acc_ref)
        i = pl.program_id(0)
        _fill_ext(dzc_ext, dp_ref[...].astype(F32), dc_ref[...].astype(F32), dn_ref[...].astype(F32),
                  i, nsteps, tm)
        _glu_ext(up_ref, uc_ref, un_ref, z_ext, i, nsteps, tm)
        for lo in range(0, d, CONV_COLS):
            hi = lo + CONV_COLS
            _fill_shifts(sh_ref, dzc_ext, lo, hi, tm)
            acc = wdw_ref[0:1, lo:hi] * _shifted(sh_ref, left, tm)
            for k in range(1, CONF_KERNEL):
                acc = acc + wdw_ref[k:k + 1, lo:hi] * _shifted(sh_ref, left - k, tm)
            dz_buf[:, lo:hi] = acc
            dzc = dzc_ext[HALO:HALO + tm, lo:hi]
            _fill_shifts(sh_ref, z_ext, lo, hi, tm)
            for k in range(CONF_KERNEL):
                dwdw_ref[k:k + 1, lo:hi] += _colsum(dzc * _shifted(sh_ref, k - left, tm))
        dz = dz_buf[...]
        av = uc_ref[:, 0:d].astype(F32)
        sg = _sigmoid(uc_ref[:, d:2 * d].astype(F32))
        duf = jnp.concatenate([dz * sg, dz * av * sg * (1.0 - sg)], axis=1)
        db1_ref[0:1, :] += _colsum(duf)
        du = duf.astype(BF16)
        xv = x_ref[...]
        g, sh, sc = vec_ref[0:1, :], vec_ref[1:2, :], vec_ref[2:3, :]
        h = _norm_mod(xv, g, sh, sc).astype(BF16)
        dh = _in_proj_bwd(h, du, w_ref, dw_ref)
        dxn, dsh, dsc, dg = _norm_mod_bwd(dh, xv, g, sc)
        acc_ref[0:1, :] += dsh
        acc_ref[1:2, :] += dsc
        acc_ref[2:3, :] += dg
        dxin_ref[...] = dx_ref[...] + dxn
        _emit_bf16_at_end(nsteps, dw_ref, dw_out)

    ext = pltpu.VMEM((tm + 2 * HALO, d), F32)
    return _call(
        body, name="cf_bwd_in", nsteps=nsteps,
        in_specs=[*_halo_specs(tm, d, t_total), *_halo_specs(tm, nu, t_total), _rows(tm, d), _rows(tm, d),
                  _full(vec.shape), _full(wdw.shape), _VM],
        out_specs=[_rows(tm, d), _VM, _VM, _VM, _VM],
        out_shape=[jax.ShapeDtypeStruct((t_total, d), F32), jax.ShapeDtypeStruct(wpw1.shape, BF16),
                   jax.ShapeDtypeStruct((32, d), F32), jax.ShapeDtypeStruct((8, nu), F32),
                   jax.ShapeDtypeStruct((8, d), F32)],
        scratch_shapes=[ext, ext, _shift_scratch(tm), pltpu.VMEM((tm, d), F32), pltpu.VMEM(wpw1.shape, F32)],
        args=(dzc, dzc, dzc, u, u, u, x, dx, vec, wdw, wpw1), exch=exch)


def _mod_fwd(c_all, w_mod, b_cols):
    nl, d, ncol = w_mod.shape
    nb = c_all.shape[0]

    def body(c_ref, w_ref, b_ref, o_ref):
        cv = c_ref[...]
        ca = cv * _sigmoid(cv)
        o_ref[0] = jnp.dot(ca, w_ref[0], preferred_element_type=F32, precision=HIGHEST) + b_ref[0]

    return _pcall(
        body, name="mod_fwd", grid=(nl,),
        in_specs=[_full(c_all.shape), pl.BlockSpec((1, d, ncol), lambda l: (l, 0, 0)),
                  pl.BlockSpec((1, 1, ncol), lambda l: (l, 0, 0))],
        out_specs=pl.BlockSpec((1, nb, ncol), lambda l: (l, 0, 0)),
        out_shape=jax.ShapeDtypeStruct((nl, nb, ncol), F32),
        compiler_params=_seq_params(),
    )(c_all, w_mod, b_cols.reshape(nl, 1, ncol))


def _mod_bwd(c_all_t, dmod_cols):
    d, nb = c_all_t.shape
    nl, _, ncol = dmod_cols.shape

    def body(c_ref, dm_ref, o_ref):
        cv = c_ref[...]
        ca = cv * _sigmoid(cv)
        o_ref[0] = jnp.dot(ca, dm_ref[0], preferred_element_type=F32, precision=HIGHEST)

    return _pcall(
        body, name="mod_bwd", grid=(nl,),
        in_specs=[_full(c_all_t.shape), pl.BlockSpec((1, nb, ncol), lambda l: (l, 0, 0))],
        out_specs=pl.BlockSpec((1, d, ncol), lambda l: (l, 0, 0)),
        out_shape=jax.ShapeDtypeStruct((nl, d, ncol), F32),
        compiler_params=_seq_params(),
    )(c_all_t, dmod_cols)


def _row_block(r, c):
    if r * c <= EW_BLOCK_ELEMS:
        return r
    best = None
    for br in range(8, r, 8):
        if r % br == 0 and br * c <= EW_BLOCK_ELEMS:
            best = br
    assert best is not None, (r, c)
    return best


def _as2d(a):
    return a.reshape(-1, a.shape[-1])


def _adamw(w, gparts, m, v):
    shape = w.shape
    w2, m2, v2 = _as2d(w), _as2d(m), _as2d(v)
    g2 = [_as2d(g) for g in gparts]
    r, c = w2.shape
    br = _row_block(r, c)
    ng = len(g2)

    def body(*refs):
        w_ref, m_ref, v_ref = refs[0:3]
        g_refs = refs[3:3 + ng]
        g = g_refs[0][...]
        for gr in g_refs[1:]:
            g = g + gr[...]
        _adamw_update(g, w_ref[...], m_ref[...], v_ref[...], refs[3 + ng:])

    spec = pl.BlockSpec((br, c), lambda i: (i, 0))
    outs = _pcall(
        body, name="adamw", grid=(r // br,),
        in_specs=[spec] * (3 + ng), out_specs=[spec] * 4,
        out_shape=[jax.ShapeDtypeStruct((r, c), F32)] * 4,
        compiler_params=_seq_params(),
    )(w2, m2, v2, *g2)
    return tuple(o.reshape(shape) for o in outs)


def _adamw_update(g, w, m, v, out_refs):
    go_ref, d_ref, mo_ref, vo_ref = out_refs
    mn = ADAM_B1 * m + (1.0 - ADAM_B1) * g
    vn = ADAM_B2 * v + (1.0 - ADAM_B2) * (g * g)
    m_hat = mn / (1.0 - ADAM_B1 ** ADAM_STEP)
    v_hat = vn / (1.0 - ADAM_B2 ** ADAM_STEP)
    go_ref[...] = g.reshape(go_ref.shape)
    d_ref[...] = (-ADAM_LR * (m_hat / (jnp.sqrt(v_hat) + ADAM_EPS) + ADAM_WD * w)).reshape(d_ref.shape)
    mo_ref[...] = mn.reshape(mo_ref.shape)
    vo_ref[...] = vn.reshape(vo_ref.shape)


def _adamw_partials(w, partials, m, v):
    nl, a, b = w.shape
    br = _row_block(a, b)
    nb = a // br

    def body(*refs):
        w_ref, m_ref, v_ref = refs[0:3]
        p_refs = refs[3:3 + nl]
        out_refs = refs[3 + nl:]
        for layer in range(nl):
            @pl.when(pl.program_id(0) == layer)
            def _(layer=layer):
                halves = []
                for core in range(2):
                    acc = p_refs[layer][core, 0].astype(F32)
                    for chip in range(1, N_CHIPS):
                        acc = acc + p_refs[layer][core, chip].astype(F32)
                    halves.append(acc)
                _adamw_update(halves[0] + halves[1], w_ref[...], m_ref[...], v_ref[...], out_refs)

    def part_spec(layer):
        def index(l, i):
            return 0, 0, jnp.where(l == layer, i, jnp.where(l < layer, 0, nb - 1)), 0
        return pl.BlockSpec((2, N_CHIPS, br, b), index)

    spec = pl.BlockSpec((br, b), lambda l, i: (l * nb + i, 0))
    outs = _pcall(
        body, name="adamw_partials", grid=(nl, nb),
        in_specs=[spec] * 3 + [part_spec(layer) for layer in range(nl)], out_specs=[spec] * 4,
        out_shape=[jax.ShapeDtypeStruct((nl * a, b), F32)] * 4,
        compiler_params=pltpu.CompilerParams(dimension_semantics=("arbitrary", "arbitrary"),
                                             vmem_limit_bytes=VMEM_LIMIT),
    )(_as2d(w), _as2d(m), _as2d(v), *partials)
    return tuple(o.reshape(w.shape) for o in outs)


def _allgather8(block, with_sum, exch=None):
    m_per, n = block.shape
    nx = 0 if exch is None else len(exch.arrs)
    nvm = 2 if with_sum else 1

    def body(x_ref, *rest):
        xin, out_ref = rest[:nx], rest[nx]
        sum_ref = rest[nx + 1] if with_sum else None
        xout = rest[nx + nvm:2 * nx + nvm]
        send_sems, recv_sems, local_sem = rest[2 * nx + nvm:2 * nx + nvm + 3]
        xsems = rest[2 * nx + nvm + 3:]
        if exch is not None:
            exch.start(xin, xout, xsems)
        x, y, c = _place()
        me, sibling = (x, y, c), (x, y, 1 - c)
        chips = [(1 - x, y), (x, 1 - y), (1 - x, 1 - y)]

        def rows(px, py, pc):
            return out_ref.at[pl.ds((4 * px + 2 * py + pc) * m_per, m_per), :]

        def copy(k, blk, to, src=None):
            return pltpu.make_async_remote_copy(
                src_ref=rows(*blk) if src is None else src, dst_ref=rows(*blk),
                send_sem=send_sems.at[k], recv_sem=recv_sems.at[k], device_id=to, device_id_type=MESH)

        mine = pltpu.make_async_copy(x_ref, rows(*me), local_sem)
        mine.start()
        first = [copy(0, me, sibling, src=x_ref)]
        first += [copy(1 + j, me, (*chip, c), src=x_ref) for j, chip in enumerate(chips)]
        for cp in first:
            cp.start()
        passed = [copy(4 + j, (*chip, c), sibling) for j, chip in enumerate(chips)]
        for j, chip in enumerate(chips):
            copy(1 + j, (*chip, c), me).wait_recv()
            passed[j].start()
        copy(0, sibling, me).wait_recv()
        for j, chip in enumerate(chips):
            copy(4 + j, (*chip, 1 - c), me).wait_recv()
        for cp in first + passed:
            cp.wait_send()
        mine.wait()
        if exch is not None:
            exch.mid(xin, xout, xsems)
            exch.wait(xin, xout, xsems)
        if with_sum:
            acc = out_ref[0:m_per, :]
            for k in range(1, N_DEV):
                acc = acc + out_ref[k * m_per:(k + 1) * m_per, :]
            sum_ref[...] = acc

    out_shape = [jax.ShapeDtypeStruct((N_DEV * m_per, n), F32)]
    out_specs = [_VM]
    if with_sum:
        out_shape.append(jax.ShapeDtypeStruct((m_per, n), F32))
        out_specs.append(_VM)
    res = _pcall(
        body, name=("allgather8_sum" if with_sum else "allgather8") + ("" if exch is None else "_" + exch.tag),
        in_specs=[_VM] + [_ANY] * nx, out_specs=out_specs + [_ANY] * nx,
        out_shape=out_shape + ([] if exch is None else exch.out_shapes()),
        scratch_shapes=[pltpu.SemaphoreType.DMA((7,)), pltpu.SemaphoreType.DMA((7,)), pltpu.SemaphoreType.DMA]
        + ([] if exch is None else exch.sems()),
        compiler_params=pltpu.CompilerParams(vmem_limit_bytes=VMEM_LIMIT),
    )(block, *([] if exch is None else exch.arrs))
    return list(res[:nvm]), list(res[nvm:])


def _my_cols(full, chip):
    w = full.shape[-1] // N_CHIPS
    return lax.dynamic_slice_in_dim(full, chip * w, w, axis=full.ndim - 1)


def _pad_rows(a, rows):
    return jnp.pad(a, ((0, rows - a.shape[0]), (0, 0)))


def _to_lanes(a):
    flat = a.reshape(-1)
    n = -(-flat.shape[0] // (8 * LANES)) * (8 * LANES)
    return jnp.pad(flat, (0, n - flat.shape[0])).reshape(-1, LANES)


class _Packer:
    def __init__(self):
        self.items = []
        self.rows = 0

    def add(self, name, a):
        lanes = _to_lanes(a)
        self.items.append((name, self.rows, a.shape, lanes))
        self.rows += lanes.shape[0]

    def pack(self):
        total = -(-self.rows // 8) * 8
        return _pad_rows(jnp.concatenate([it[3] for it in self.items], axis=0), total)

    def unpack(self, buf):
        out = {}
        for name, row, shape, lanes in self.items:
            size = 1
            for s in shape:
                size *= s
            out[name] = buf[row:row + lanes.shape[0]].reshape(-1)[:size].reshape(shape)
        return out


TM_SEQ = 512
TM_FFN = 256
TM_FFN_FWD = 512


LAYER_KEYS = ("in", "out", "gate", "up", "down")
BLOCKED_KEYS = ("in",)
TRANSPOSED = ("ffn_w_gate", "ffn_w_up")


def _layer_big_names(layer):
    i = layer // 2
    mix = (("ab_w_in", i), ("ab_w_out", i)) if layer % 2 == 0 else (("cf_w_pw1", i), ("cf_w_pw2", i))
    return dict(zip(LAYER_KEYS, mix + (("ffn_w_gate", layer), ("ffn_w_up", layer), ("ffn_w_down", layer))))


def _unpack_weight(key, g):
    g = g.reshape(N_CHIPS, -1, g.shape[-1])
    return g if key in BLOCKED_KEYS else g.reshape(-1, g.shape[-1])


def _chunk_grad(key, dw):
    parts = dw if key in BLOCKED_KEYS else dw.reshape(N_CHIPS, -1, dw.shape[-1])
    return parts.astype(BF16)


def _local_step(x, target, mods, p, shards, first):
    t_total, d = x.shape
    depth = mods.shape[0]
    tm = min(TM_SEQ, t_total)
    tmf = min(TM_FFN, t_total)
    saved = []
    xin = x
    weights = [{} for _ in range(depth)]

    def carried(stage, layer):
        nxt = layer + 1
        if layer == 0:
            plan = {"in": [(0, "out"), (0, "gate")], "mix": [(0, "up"), (0, "down")],
                    "ffn": [(1, "in"), (1, "out"), (1, "gate"), (1, "up")]}
        elif layer % 2 == 1:
            plan = {"in": [], "mix": [(layer, "down"), (nxt, "in"), (nxt, "out")], "ffn": [(nxt, "gate"), (nxt, "up")]}
        else:
            plan = {"in": [(layer, "down")], "mix": [(nxt, "in"), (nxt, "out")], "ffn": [(nxt, "gate"), (nxt, "up")]}
        return [(of, k) for of, k in plan[stage] if of < depth]

    def gather(stage, layer):
        wanted = carried(stage, layer)
        if not wanted:
            return None
        return _Gather([shards[of][k].reshape(2, -1, shards[of][k].shape[-1]) for of, k in wanted])

    def keep(stage, layer, arrs):
        for (of, k), g in zip(carried(stage, layer), arrs):
            weights[of][k] = _unpack_weight(k, g)

    for k, g in first.items():
        weights[0][k] = _unpack_weight(k, g)
    for layer in range(depth):
        i = layer // 2
        lw = weights[layer]
        sh1, sc1, g1, sh2, sc2, g2 = (mods[layer, k:k + 1] for k in range(6))
        vec_in = jnp.concatenate([p["norm_mix_g"][layer:layer + 1], sh1, sc1], axis=0)
        bias = None if layer % 2 == 0 else p["cf_b_pw1"][i:i + 1]
        (u,), arrived = _in_proj(xin, vec_in, lw["in"], bias, tm, exch=gather("in", layer))
        keep("in", layer, arrived)
        if layer % 2 == 0:
            (y, x2), arrived = _ab_fwd(u, xin, g1, p["ab_conv"][i], p["ab_w_pool"][i].astype(BF16),
                                       p["ab_pool_scale"][i:i + 1], lw["out"], tm, exch=gather("mix", layer))
            zc = None
        else:
            vec_cf = jnp.concatenate([g1, p["cf_b_dw"][i:i + 1], p["cf_ln_g"][i:i + 1], p["cf_ln_b"][i:i + 1],
                                      p["cf_b_pw2"][i:i + 1]], axis=0)
            (zc, y, x2), arrived = _cf_fwd(u, xin, vec_cf, _pad_rows(p["cf_w_dw"][i], 32), lw["out"], tm,
                                           exch=gather("mix", layer))
        keep("mix", layer, arrived)
        vec_ffn = jnp.concatenate([p["norm_ffn_g"][layer:layer + 1], sh2, sc2, g2], axis=0)
        (a, b, fout, x3), arrived = _ffn_fwd(x2, vec_ffn, lw["gate"], lw["up"], lw["down"], min(TM_FFN_FWD, t_total),
                                             exch=gather("ffn", layer))
        keep("ffn", layer, arrived)
        saved.append((xin, u, y, zc, x2, a, b, fout))
        xin = x3

    (dx, fin), _ = _final_fwd_bwd(xin, target, p["final_norm_g"].reshape(1, d), tm)
    grads = {"final_norm_g": fin[0], "loss": fin[1, 0:1]}
    per_layer = {k: [None] * depth for k in ("norm_mix_g", "norm_ffn_g")}
    half = {k: [None] * (depth // 2) for k in (
        "ab_conv", "ab_w_pool", "ab_pool_scale", "cf_b_pw1", "cf_w_dw", "cf_b_dw", "cf_ln_g", "cf_ln_b", "cf_b_pw2")}
    dmods = [None] * depth
    received = {}
    pending = None
    for layer in reversed(range(depth)):
        i = layer // 2
        lw = weights[layer]
        xin, u, y, zc, x2, a, b, fout = saved[layer]
        sh1, sc1, g1, sh2, sc2, g2 = (mods[layer, k:k + 1] for k in range(6))
        (da, db, dwd, acc_d), _ = _ffn_bwd_down(dx, fout, a, b, g2, lw["down"], tmf)
        vec_ffn = jnp.concatenate([p["norm_ffn_g"][layer:layer + 1], sh2, sc2], axis=0)
        leaving = [_chunk_grad("down", dwd)] + ([pending] if pending is not None else [])
        (dx2, dwg, dwu, acc_u), arrived = _ffn_bwd_up(da, db, x2, dx, vec_ffn, lw["gate"], lw["up"], tmf,
                                                      exch=_Scatter(leaving))
        received[(layer, "down")] = arrived[0]
        if pending is not None:
            received[(layer + 1, "in")] = arrived[1]
        per_layer["norm_ffn_g"][layer] = acc_u[2]
        vec_in = jnp.concatenate([p["norm_mix_g"][layer:layer + 1], sh1, sc1], axis=0)
        if layer % 2 == 0:
            send_gate = _Scatter([_chunk_grad("gate", dwg)])
            (dpre, dwout, dwpool, acc_o), arrived = _ab_bwd_out(
                dx2, y, u, g1, p["ab_conv"][i], p["ab_w_pool"][i].astype(BF16), p["ab_pool_scale"][i:i + 1],
                lw["out"], tm, exch=send_gate)
            received[(layer, "gate")] = arrived[0]
            send_up_out = _Scatter([_chunk_grad("up", dwu), _chunk_grad("out", dwout)])
            (dx, dwin, dconv, acc_i), arrived = _ab_bwd_in(dpre, u, xin, dx2, vec_in, p["ab_conv"][i], lw["in"], tm,
                                                           exch=send_up_out)
            half["ab_w_pool"][i] = dwpool
            half["ab_pool_scale"][i] = acc_o[1, 0:d // 2]
            half["ab_conv"][i] = dconv[0:3]
        else:
            vec_cf = jnp.concatenate([g1, p["cf_ln_g"][i:i + 1], p["cf_ln_b"][i:i + 1]], axis=0)
            (dzc, dwout, acc_o), _ = _cf_bwd_out(dx2, y, zc, vec_cf, lw["out"], tm)
            send_all = _Scatter([_chunk_grad("up", dwu), _chunk_grad("out", dwout), _chunk_grad("gate", dwg)])
            (dx, dwin, dwdw, db1, acc_i), arrived = _cf_bwd_in(
                dzc, u, xin, dx2, vec_in, _pad_rows(p["cf_w_dw"][i], 32), lw["in"], tm, exch=send_all)
            received[(layer, "gate")] = arrived.pop()
            half["cf_b_pw2"][i] = acc_o[1]
            half["cf_ln_g"][i] = acc_o[2]
            half["cf_ln_b"][i] = acc_o[3]
            half["cf_b_dw"][i] = acc_o[4]
            half["cf_w_dw"][i] = dwdw[0:CONF_KERNEL]
            half["cf_b_pw1"][i] = db1[0]
        received[(layer, "up")], received[(layer, "out")] = arrived
        per_layer["norm_mix_g"][layer] = acc_i[2]
        dmods[layer] = jnp.stack([acc_i[0], acc_i[1], acc_o[0], acc_u[0], acc_u[1], acc_d[0]], axis=0)
        pending = _chunk_grad("in", dwin)
    for k, v in {**per_layer, **half}.items():
        grads[k] = jnp.stack(v, axis=0)
    return dx, grads, jnp.stack(dmods, axis=0), received, pending


SMALL_COLS = ("ab_conv", "cf_b_pw1", "cf_w_dw", "cf_b_dw", "cf_ln_g", "cf_ln_b", "cf_b_pw2")
SMALL_REPL = ("norm_mix_g", "norm_ffn_g", "ab_w_pool", "ab_pool_scale", "final_norm_g")
WEIGHTS = ("norm_mix_g", "norm_ffn_g", "w_mod", "b_mod", "ab_w_in", "ab_conv", "ab_w_pool", "ab_pool_scale",
           "ab_w_out", "cf_w_pw1", "cf_b_pw1", "cf_w_dw", "cf_b_dw", "cf_ln_g", "cf_ln_b", "cf_w_pw2",
           "cf_b_pw2", "ffn_w_gate", "ffn_w_up", "ffn_w_down", "final_norm_g")


def kernel(x, c, norm_mix_g, norm_ffn_g, w_mod, b_mod, ab_w_in, ab_conv, ab_w_pool, ab_pool_scale, ab_w_out, cf_w_pw1, cf_b_pw1, cf_w_dw, cf_b_dw, cf_ln_g, cf_ln_b, cf_w_pw2, cf_b_pw2, ffn_w_gate, ffn_w_up, ffn_w_down, final_norm_g, loss_target, m_norm_mix_g, m_norm_ffn_g, m_w_mod, m_b_mod, m_ab_w_in, m_ab_conv, m_ab_w_pool, m_ab_pool_scale, m_ab_w_out, m_cf_w_pw1, m_cf_b_pw1, m_cf_w_dw, m_cf_b_dw, m_cf_ln_g, m_cf_ln_b, m_cf_w_pw2, m_cf_b_pw2, m_ffn_w_gate, m_ffn_w_up, m_ffn_w_down, m_final_norm_g, v_norm_mix_g, v_norm_ffn_g, v_w_mod, v_b_mod, v_ab_w_in, v_ab_conv, v_ab_w_pool, v_ab_pool_scale, v_ab_w_out, v_cf_w_pw1, v_cf_b_pw1, v_cf_w_dw, v_cf_b_dw, v_cf_ln_g, v_cf_ln_b, v_cf_w_pw2, v_cf_b_pw2, v_ffn_w_gate, v_ffn_w_up, v_ffn_w_down, v_final_norm_g):
    w = dict(norm_mix_g=norm_mix_g, norm_ffn_g=norm_ffn_g, w_mod=w_mod, b_mod=b_mod, ab_w_in=ab_w_in,
             ab_conv=ab_conv, ab_w_pool=ab_w_pool, ab_pool_scale=ab_pool_scale, ab_w_out=ab_w_out,
             cf_w_pw1=cf_w_pw1, cf_b_pw1=cf_b_pw1, cf_w_dw=cf_w_dw, cf_b_dw=cf_b_dw, cf_ln_g=cf_ln_g,
             cf_ln_b=cf_ln_b, cf_w_pw2=cf_w_pw2, cf_b_pw2=cf_b_pw2, ffn_w_gate=ffn_w_gate, ffn_w_up=ffn_w_up,
             ffn_w_down=ffn_w_down, final_norm_g=final_norm_g)
    mom = dict(norm_mix_g=m_norm_mix_g, norm_ffn_g=m_norm_ffn_g, w_mod=m_w_mod, b_mod=m_b_mod, ab_w_in=m_ab_w_in,
               ab_conv=m_ab_conv, ab_w_pool=m_ab_w_pool, ab_pool_scale=m_ab_pool_scale, ab_w_out=m_ab_w_out,
               cf_w_pw1=m_cf_w_pw1, cf_b_pw1=m_cf_b_pw1, cf_w_dw=m_cf_w_dw, cf_b_dw=m_cf_b_dw, cf_ln_g=m_cf_ln_g,
               cf_ln_b=m_cf_ln_b, cf_w_pw2=m_cf_w_pw2, cf_b_pw2=m_cf_b_pw2, ffn_w_gate=m_ffn_w_gate,
               ffn_w_up=m_ffn_w_up, ffn_w_down=m_ffn_w_down, final_norm_g=m_final_norm_g)
    var = dict(norm_mix_g=v_norm_mix_g, norm_ffn_g=v_norm_ffn_g, w_mod=v_w_mod, b_mod=v_b_mod, ab_w_in=v_ab_w_in,
               ab_conv=v_ab_conv, ab_w_pool=v_ab_w_pool, ab_pool_scale=v_ab_pool_scale, ab_w_out=v_ab_w_out,
               cf_w_pw1=v_cf_w_pw1, cf_b_pw1=v_cf_b_pw1, cf_w_dw=v_cf_w_dw, cf_b_dw=v_cf_b_dw, cf_ln_g=v_cf_ln_g,
               cf_ln_b=v_cf_ln_b, cf_w_pw2=v_cf_w_pw2, cf_b_pw2=v_cf_b_pw2, ffn_w_gate=v_ffn_w_gate,
               ffn_w_up=v_ffn_w_up, ffn_w_down=v_ffn_w_down, final_norm_g=v_final_norm_g)
    px, py, pc = _place()
    chip = 2 * px + py
    dev = 2 * chip + pc
    depth, d, mod_cols = w_mod.shape
    x = x[0]
    target = loss_target[0]

    def rows_major(name, t):
        return jnp.swapaxes(t, 1, 2) if name in TRANSPOSED else t

    shards = [{k: rows_major(name, w[name])[idx].astype(BF16) for k, (name, idx) in _layer_big_names(layer).items()}
              for layer in range(depth)]

    small_in = _Packer()
    small_in.add("c", c)
    for name in SMALL_COLS:
        small_in.add(name, w[name])
    def first_gather(*keys):
        return _Gather([shards[0][k].reshape(2, -1, shards[0][k].shape[-1]) for k in keys])

    first = {}
    (gathered,), (first["in"],) = _allgather8(small_in.pack(), with_sum=False, exch=first_gather("in"))
    gathered = gathered.reshape(N_DEV, -1, LANES)
    per_dev = [small_in.unpack(gathered[k]) for k in range(N_DEV)]
    c_all = jnp.concatenate([pd["c"] for pd in per_dev], axis=0)
    params = {name: jnp.concatenate([per_dev[2 * k][name] for k in range(N_CHIPS)], axis=-1)
              for name in SMALL_COLS}
    for name in SMALL_REPL:
        params[name] = w[name]

    mod_part = _mod_fwd(c_all, w_mod, _my_cols(b_mod, chip))
    (mod_all,), _ = _allgather8(mod_part.reshape(-1, LANES), with_sum=False)
    mod_all = mod_all.reshape(N_CHIPS, 2, depth, N_DEV, mod_cols)[:, 0]
    mod_all = jnp.moveaxis(mod_all, 0, 2).reshape(depth, N_DEV, N_CHIPS * mod_cols)
    mods = lax.dynamic_index_in_dim(mod_all, dev, axis=1, keepdims=False).reshape(depth, 6, d)

    grad_x, grads, dmods, received, last_chunk = _local_step(x, target, mods, params, shards, first)

    small_out = _Packer()
    small_out.add("dmods", dmods)
    for name in ("loss",) + SMALL_REPL + SMALL_COLS:
        small_out.add(name, grads[name])
    (parts_all, parts_sum), (received[(0, "in")],) = _allgather8(small_out.pack(), with_sum=True,
                                                                 exch=_Scatter([last_chunk]))
    small_sum = small_out.unpack(parts_sum)
    loss = small_sum["loss"][0]
    dmods_all = jnp.stack([small_out.unpack(pa)["dmods"] for pa in parts_all.reshape(N_DEV, -1, LANES)], axis=1)
    dmods_all = dmods_all.reshape(depth, N_DEV, 6 * d)

    g_final = {}
    g_final["w_mod"] = [_mod_bwd(c_all.T, _my_cols(dmods_all, chip))]
    g_final["b_mod"] = [small_sum["dmods"].reshape(depth, 6 * d)]
    for name in SMALL_REPL:
        g_final[name] = [small_sum[name]]
    for name in SMALL_COLS:
        g_final[name] = [_my_cols(small_sum[name], chip)]

    updates = {}
    for name in WEIGHTS:
        parts = [received[(layer, k)] for layer in range(depth)
                 for k, (other, _) in _layer_big_names(layer).items() if other == name]
        if parts:
            outs = _adamw_partials(rows_major(name, w[name]), parts, rows_major(name, mom[name]),
                                   rows_major(name, var[name]))
            updates[name] = [rows_major(name, o) for o in outs]
        else:
            updates[name] = _adamw(w[name], g_final[name], mom[name], var[name])
    return (loss, grad_x[None], *[updates[name][0] for name in WEIGHTS], *[updates[name][1] for name in WEIGHTS],
            *[updates[name][2] for name in WEIGHTS], *[updates[name][3] for name in WEIGHTS])
```
